```python
import jax, jax.numpy as jnp
from jax import lax
import numpy as np

D_MODEL = 1024
BATCH = 16
SEQ = 256
DEPTH = 1
DEC_BATCH = 2
DEC_SEQ = 1024
PAST_LEN = 512

GRID_W = 64
D_CONV = 512
CONV_K = 31
D_MLSTM = 512
N_HEADS = 4
HEAD_DIM = D_MLSTM // N_HEADS
CHUNK = 64
N_GROUPS = 4
EXPERTS_PER_GROUP = 4
N_EXPERTS = N_GROUPS * EXPERTS_PER_GROUP
TOP_K_IN_GROUP = 2
D_EXPERT = 256
N_ADA = 6
EPS = 1e-6
N_IN = 2 * D_CONV + 4 * D_MLSTM + 4 * N_HEADS + 2 * D_MODEL

kernel_name = 'hybrid_conv_mlstm_hmoe_diffusion_step'


def _in_splits():
    s0 = D_CONV
    s1 = 2 * D_CONV
    s2 = s1 + D_MLSTM
    s3 = s2 + D_MLSTM
    s4 = s3 + D_MLSTM
    s5 = s4 + D_MLSTM
    s6 = s5 + 4 * N_HEADS
    s7 = s6 + D_MODEL
    return [s0, s1, s2, s3, s4, s5, s6, s7]


def rms_norm(x, g):
    x32 = x.astype(jnp.float32)
    y = x32 * lax.rsqrt(jnp.mean(x32 * x32, axis=-1, keepdims=True) + EPS)
    return (y * g.astype(jnp.float32)).astype(x.dtype)


def layer_norm(x, g, b):
    x32 = x.astype(jnp.float32)
    mu = jnp.mean(x32, axis=-1, keepdims=True)
    xc = x32 - mu
    y = xc * lax.rsqrt(jnp.mean(xc * xc, axis=-1, keepdims=True) + EPS)
    return (y * g.astype(jnp.float32) + b.astype(jnp.float32)).astype(x.dtype)


def depthwise_conv(u, w, b, grid_w):
    B, T, C = u.shape
    if grid_w is not None:
        rows = T // grid_w
        u = u.reshape(B * rows, grid_w, C)
    y = lax.conv_general_dilated(
        u, w[:, None, :].astype(u.dtype), window_strides=(1,),
        padding=[(CONV_K // 2, CONV_K // 2)],
        dimension_numbers=('NWC', 'WIO', 'NWC'), feature_group_count=C)
    return (y + b.astype(y.dtype)).reshape(B, T, C)


def mlstm_dir(q, k, v, i_pre, f_pre, C0, n0, m0):
    B, H, T, Dh = q.shape
    nc = T // CHUNK
    q = q.astype(jnp.float32) * (HEAD_DIM ** -0.5)
    k = k.astype(jnp.float32)
    v = v.astype(jnp.float32)
    li = i_pre.astype(jnp.float32)
    lf = jax.nn.log_sigmoid(f_pre.astype(jnp.float32))

    def chunks(a):
        return jnp.moveaxis(a.reshape(a.shape[:2] + (nc, CHUNK) + a.shape[3:]), 2, 0)

    causal = jnp.tril(jnp.ones((CHUNK, CHUNK), dtype=bool))

    def step(carry, xs):
        C, n, m = carry
        qc, kc, vc, ic, lfc = xs
        bcum = jnp.cumsum(lfc, axis=-1)
        dmat = bcum[..., :, None] - bcum[..., None, :] + ic[..., None, :]
        dmat = jnp.where(causal, dmat, -jnp.inf)
        m_inter = bcum + m[..., None]
        m_t = jnp.maximum(m_inter, jnp.max(dmat, axis=-1))
        w_intra = jnp.exp(dmat - m_t[..., None])
        w_inter = jnp.exp(m_inter - m_t)
        s = jnp.einsum('bhtd,bhsd->bhts', qc, kc) * w_intra
        num = jnp.einsum('bhts,bhsd->bhtd', s, vc) + w_inter[..., None] * jnp.einsum('bhtd,bhde->bhte', qc, C)
        den = jnp.sum(s, axis=-1) + w_inter * jnp.einsum('bhtd,bhd->bht', qc, n)
        h = num / jnp.maximum(jnp.abs(den), jnp.exp(-m_t))[..., None]
        b_last = bcum[..., -1]
        g = b_last[..., None] - bcum + ic
        m_new = jnp.maximum(b_last + m, jnp.max(g, axis=-1))
        wk = jnp.exp(g - m_new[..., None])
        decay = jnp.exp(b_last + m - m_new)
        C_new = decay[..., None, None] * C + jnp.einsum('bhs,bhsd,bhse->bhde', wk, kc, vc)
        n_new = decay[..., None] * n + jnp.einsum('bhs,bhsd->bhd', wk, kc)
        return (C_new, n_new, m_new), h

    init = (C0.astype(jnp.float32), n0.astype(jnp.float32), m0.astype(jnp.float32))
    (C, n, m), h = lax.scan(step, init, (chunks(q), chunks(k), chunks(v), chunks(li), chunks(lf)))
    h = jnp.moveaxis(h, 0, 2).reshape(B, H, T, Dh)
    return h, (C, n, m)


def bi_mlstm(q, k, v, ig, fg, C0, n0, m0):
    hf, (Cf, nf, mf) = mlstm_dir(q, k, v, ig[:, 0], fg[:, 0], C0[:, 0], n0[:, 0], m0[:, 0])
    flip = lambda a: jnp.flip(a, axis=2)
    hb, (Cb, nb, mb) = mlstm_dir(flip(q), flip(k), flip(v), flip(ig[:, 1]), flip(fg[:, 1]),
                                 C0[:, 1], n0[:, 1], m0[:, 1])
    h = hf + flip(hb)
    return h, (jnp.stack([Cf, Cb], axis=1), jnp.stack([nf, nb], axis=1), jnp.stack([mf, mb], axis=1))


def hier_moe(h, p):
    B, T, D = h.shape
    x = h.reshape(B * T, D)
    gl = (x @ p['w_rg']).astype(jnp.float32) + p['b_rg'].astype(jnp.float32)
    gp = jax.nn.softmax(gl, axis=-1)
    gp_sel, gsel = lax.top_k(gp, 1)
    el = (x @ p['w_re']).astype(jnp.float32) + p['b_re'].astype(jnp.float32)
    el = el.reshape(-1, N_GROUPS, EXPERTS_PER_GROUP)
    el_sel = jnp.take_along_axis(el, gsel[:, :, None], axis=1)[:, 0]
    ep = jax.nn.softmax(el_sel, axis=-1)
    topv, topi = lax.top_k(ep, TOP_K_IN_GROUP)
    topv = topv / jnp.sum(topv, axis=-1, keepdims=True)
    weights = gp_sel * topv
    idx = gsel * EXPERTS_PER_GROUP + topi
    combine = jnp.einsum('nk,nke->ne', weights, jax.nn.one_hot(idx, N_EXPERTS, dtype=jnp.float32))
    combine = combine.astype(h.dtype)
    g = jnp.einsum('nd,edf->nef', x, p['w_e_gate'])
    u = jnp.einsum('nd,edf->nef', x, p['w_e_up'])
    act = jax.nn.silu(g) * u * combine[:, :, None]
    out = jnp.einsum('nef,efd->nd', act, p['w_e_down'])
    return out.reshape(B, T, D)


def trunk_layer(x, mod, C0, n0, m0, grid_w, p):
    B, T, _ = x.shape
    shift1, scale1, gate1, shift2, scale2, gate2 = (mod[:, i, None, :] for i in range(N_ADA))
    h = rms_norm(x, p['norm1_g']) * (1 + scale1) + shift1
    z = h @ p['w_in'] + p['b_in']
    a, ga, q, k, v, o, g_if, g_ma, g_mb = jnp.split(z, _in_splits(), axis=-1)
    u = a * jax.nn.sigmoid(ga)
    u = depthwise_conv(u, p['w_dw'], p['b_dw'], grid_w)
    u = jax.nn.silu(layer_norm(u, p['conv_ln_g'], p['conv_ln_b']))
    br_a = u @ p['w_conv_out']
    to_heads = lambda t: t.reshape(B, T, N_HEADS, HEAD_DIM).transpose(0, 2, 1, 3)
    g_if = (g_if.reshape(B, T, 2, 2, N_HEADS) + p['b_gates']).transpose(0, 2, 3, 4, 1)
    hm, st = bi_mlstm(to_heads(q), to_heads(k), to_heads(v), g_if[:, :, 0], g_if[:, :, 1], C0, n0, m0)
    hm = hm * lax.rsqrt(jnp.mean(hm * hm, axis=-1, keepdims=True) + EPS)
    hm = hm.transpose(0, 2, 1, 3).reshape(B, T, D_MLSTM) * p['mlstm_hn_g'].astype(jnp.float32)
    hm = (jax.nn.sigmoid(o.astype(jnp.float32)) * hm).astype(x.dtype)
    br_b = hm @ p['w_mlstm_out']
    mixed = (jax.nn.sigmoid(g_ma) * br_a + jax.nn.sigmoid(g_mb) * br_b) @ p['w_o']
    x = x + gate1 * mixed
    h2 = rms_norm(x, p['norm2_g']) * (1 + scale2) + shift2
    x = x + gate2 * hier_moe(h2, p)
    return x, st


def setup_inputs(seed: int = 0) -> dict:
    key = jax.random.key(seed)
    ks = jax.random.split(key, 32)
    nrm = lambda k, shape, s: jax.random.normal(k, shape, jnp.float32) * s
    D = D_MODEL
    f_bias = jnp.linspace(3.0, 6.0, N_HEADS, dtype=jnp.float32)
    gate_sel = jnp.array([0.0, 1.0], jnp.float32)[None, None, :, None]
    b_gates = nrm(ks[12], (DEPTH, 2, 2, N_HEADS), 0.1) + gate_sel * f_bias
    return {
        'x_prompt': nrm(ks[0], (BATCH, SEQ, D), 1.0),
        'x_sample': nrm(ks[1], (DEC_BATCH, DEC_SEQ, D), 1.0),
        'state_C': nrm(ks[2], (DEC_BATCH, DEPTH, 2, N_HEADS, HEAD_DIM, HEAD_DIM), 0.3),
        'state_n': nrm(ks[3], (DEC_BATCH, DEPTH, 2, N_HEADS, HEAD_DIM), 0.3),
        'state_m': nrm(ks[4], (DEC_BATCH, DEPTH, 2, N_HEADS), 0.5),
        'c': nrm(ks[5], (DEC_BATCH, D), 1.0),
        'c_ctx': nrm(ks[6], (D,), 1.0),
        'norm1_g': 1.0 + nrm(ks[7], (DEPTH, D), 0.05),
        'w_ada': nrm(ks[8], (DEPTH, D, N_ADA * D), 0.5 * D ** -0.5),
        'b_ada': nrm(ks[9], (DEPTH, N_ADA * D), 0.02),
        'w_in': nrm(ks[10], (DEPTH, D, N_IN), D ** -0.5),
        'b_in': nrm(ks[11], (DEPTH, N_IN), 0.02),
        'b_gates': b_gates,
        'w_dw': nrm(ks[13], (DEPTH, CONV_K, D_CONV), CONV_K ** -0.5),
        'b_dw': nrm(ks[14], (DEPTH, D_CONV), 0.02),
        'conv_ln_g': 1.0 + nrm(ks[15], (DEPTH, D_CONV), 0.05),
        'conv_ln_b': nrm(ks[16], (DEPTH, D_CONV), 0.02),
        'w_conv_out': nrm(ks[17], (DEPTH, D_CONV, D), D_CONV ** -0.5),
        'mlstm_hn_g': 1.0 + nrm(ks[18], (DEPTH, D_MLSTM), 0.05),
        'w_mlstm_out': nrm(ks[19], (DEPTH, D_MLSTM, D), D_MLSTM ** -0.5),
        'w_o': nrm(ks[20], (DEPTH, D, D), D ** -0.5),
        'norm2_g': 1.0 + nrm(ks[21], (DEPTH, D), 0.05),
        'w_rg': nrm(ks[22], (DEPTH, D, N_GROUPS), D ** -0.5),
        'b_rg': nrm(ks[23], (DEPTH, N_GROUPS), 0.01),
        'w_re': nrm(ks[24], (DEPTH, D, N_EXPERTS), D ** -0.5),
        'b_re': nrm(ks[25], (DEPTH, N_EXPERTS), 0.01),
        'w_e_gate': nrm(ks[26], (DEPTH, N_EXPERTS, D, D_EXPERT), D ** -0.5),
        'w_e_up': nrm(ks[27], (DEPTH, N_EXPERTS, D, D_EXPERT), D ** -0.5),
        'w_e_down': nrm(ks[28], (DEPTH, N_EXPERTS, D_EXPERT, D), D_EXPERT ** -0.5),
        'norm_final_g': 1.0 + nrm(ks[29], (D,), 0.05),
    }


def reference(x_prompt, x_sample, state_C, state_n, state_m, c, c_ctx, norm1_g, w_ada, b_ada,
              w_in, b_in, b_gates, w_dw, b_dw, conv_ln_g, conv_ln_b, w_conv_out, mlstm_hn_g,
              w_mlstm_out, w_o, norm2_g, w_rg, b_rg, w_re, b_re, w_e_gate, w_e_up, w_e_down,
              norm_final_g):
    B = x_prompt.shape[0]
    zC = jnp.zeros((B, 2, N_HEADS, HEAD_DIM, HEAD_DIM), jnp.float32)
    zn = jnp.zeros((B, 2, N_HEADS, HEAD_DIM), jnp.float32)
    zm = jnp.zeros((B, 2, N_HEADS), jnp.float32)
    xp, xs = x_prompt, x_sample
    new_C, new_n, new_m = [], [], []
    for l in range(DEPTH):
        p = {
            'norm1_g': norm1_g[l], 'w_in': w_in[l], 'b_in': b_in[l], 'b_gates': b_gates[l],
            'w_dw': w_dw[l], 'b_dw': b_dw[l], 'conv_ln_g': conv_ln_g[l], 'conv_ln_b': conv_ln_b[l],
            'w_conv_out': w_conv_out[l], 'mlstm_hn_g': mlstm_hn_g[l], 'w_mlstm_out': w_mlstm_out[l],
            'w_o': w_o[l], 'norm2_g': norm2_g[l], 'w_rg': w_rg[l], 'b_rg': b_rg[l], 'w_re': w_re[l],
            'b_re': b_re[l], 'w_e_gate': w_e_gate[l], 'w_e_up': w_e_up[l], 'w_e_down': w_e_down[l],
        }
        mod_ctx = (jax.nn.silu(c_ctx)[None, :] @ w_ada[l] + b_ada[l]).reshape(1, N_ADA, D_MODEL)
        mod_lat = (jax.nn.silu(c) @ w_ada[l] + b_ada[l]).reshape(-1, N_ADA, D_MODEL)
        xp, (Cl, nl, ml) = trunk_layer(xp, mod_ctx, zC, zn, zm, None, p)
        new_C.append(Cl)
        new_n.append(nl)
        new_m.append(ml)
        xs, _ = trunk_layer(xs, mod_lat, state_C[:, l], state_n[:, l], state_m[:, l], GRID_W, p)
    new_state_C = jnp.stack(new_C, axis=1)
    new_state_n = jnp.stack(new_n, axis=1)
    new_state_m = jnp.stack(new_m, axis=1)
    y_prompt = rms_norm(xp, norm_final_g)
    y_sample = rms_norm(xs, norm_final_g)
    return (y_prompt, y_sample, new_state_C, new_state_n, new_state_m)
```

```python
import functools

import jax
import jax.numpy as jnp
from jax import lax
from jax.experimental import pallas as pl
from jax.experimental.pallas import tpu as pltpu

D_MODEL = 1024
D_CONV = 512
CONV_K = 31
D_MLSTM = 512
N_HEADS = 4
HEAD_DIM = D_MLSTM // N_HEADS
N_GROUPS = 4
EXPERTS_PER_GROUP = 4
N_EXPERTS = N_GROUPS * EXPERTS_PER_GROUP
D_EXPERT = 256
N_ADA = 6
EPS = 1e-6
GRID_W = 64

LANES = 128
SUB = 256
CONV_PAD = 16
CONV_RB = 64
N_UNITS = 2 * N_HEADS
VMEM_LIMIT = 58 * 1024 * 1024

BF16 = jnp.bfloat16
F32 = jnp.float32
NT_DIMS = (((1,), (1,)), ((), ()))


def _dot(a, b):
    return jnp.dot(a, b, preferred_element_type=F32)


def _dot_nt(a, b, precision=None):
    return lax.dot_general(a, b, NT_DIMS, preferred_element_type=F32, precision=precision)


def _sigmoid(x):
    return 1.0 / (1.0 + jnp.exp(-x))


def _log_sigmoid(x):
    return jnp.minimum(x, 0.0) - jnp.log1p(jnp.exp(-jnp.abs(x)))


def _split3(x):
    hi = x.astype(BF16).astype(F32)
    r1 = x - hi
    mid = r1.astype(BF16).astype(F32)
    lo = (r1 - mid).astype(BF16).astype(F32)
    return hi, mid, lo


def _ada_kernel(c_ref, w_ref, b_ref, o_ref):
    c = c_ref[...]
    s = (c * _sigmoid(c)).astype(BF16)
    o_ref[...] = _dot(s, w_ref[...].astype(BF16)) + b_ref[...]


def _ada(cin, w_ada, b_ada):
    n = w_ada.shape[1]
    bn = 1024
    return pl.pallas_call(
        _ada_kernel,
        grid=(n // bn,),
        in_specs=[
            pl.BlockSpec((8, D_MODEL), lambda j: (0, 0)),
            pl.BlockSpec((D_MODEL, bn), lambda j: (0, j)),
            pl.BlockSpec((1, bn), lambda j: (0, j)),
        ],
        out_specs=pl.BlockSpec((8, bn), lambda j: (0, j)),
        out_shape=jax.ShapeDtypeStruct((8, n), F32),
        compiler_params=pltpu.CompilerParams(dimension_semantics=("arbitrary",)),
        name="ada",
    )(cin, w_ada, b_ada)


_MIXER_WEIGHTS = (
    "g1", "wag", "bag", "wq", "bq", "wkT", "bk", "wv", "bv", "wog", "bog",
    "wgif", "bgif", "wgifT", "bgifT", "wgm", "bgm", "wdw", "bdw", "lng", "lnb",
    "wco", "hng", "wmo", "wo", "g2", "wrtT", "brtT",
)


def _mixer_kernel(T, P, has_state, emit_state, *refs):
    nsub = T // SUB
    nseg = SUB // P
    L = SUB
    it = iter(refs)
    x_ref = next(it)
    mod_ref = next(it)
    if has_state:
        c0_ref = next(it)
        m0_ref = next(it)
    w = {name: next(it) for name in _MIXER_WEIGHTS}
    x1_ref = next(it)
    h2_ref = next(it)
    comb_ref = next(it)
    if emit_state:
        cout_ref = next(it)
        nout_ref = next(it)
        mout_ref = next(it)
    (q_s, kT_s, v_s, so_s, gcol_s, grow_s, ma_s, sgb_s, hm_s, cst_s, upad_s) = [next(it) for _ in range(11)]

    def mod_row(i):
        return mod_ref[0, i:i + 1, :]

    zpad = jnp.zeros((CONV_PAD, D_CONV), F32)
    for seg in range(nseg):
        upad_s[seg, 0:CONV_PAD, :] = zpad
        upad_s[seg, CONV_PAD + P:CONV_PAD + P + CONV_PAD, :] = zpad

    def phase1(i, carry):
        r0 = pl.multiple_of(i * SUB, SUB)
        rows = pl.ds(r0, SUB)
        x = x_ref[0, rows, :]
        xn = x * lax.rsqrt(jnp.mean(x * x, axis=-1, keepdims=True) + EPS) * w["g1"][...]
        hb = (xn * (1.0 + mod_row(1)) + mod_row(0)).astype(BF16)

        ag = _dot(hb, w["wag"][...]) + w["bag"][...]
        u = ag[:, :D_CONV] * _sigmoid(ag[:, D_CONV:])
        for seg in range(nseg):
            upad_s[seg, CONV_PAD:CONV_PAD + P, :] = u[seg * P:(seg + 1) * P, :]
        col_blocks = []
        for cb in range(D_CONV // LANES):
            cs = slice(cb * LANES, (cb + 1) * LANES)
            row_blocks = []
            for seg in range(nseg):
                for rb in range(P // CONV_RB):
                    acc = jnp.broadcast_to(w["bdw"][0:1, cs], (CONV_RB, LANES))
                    for j in range(CONV_K):
                        a0 = rb * CONV_RB + j + (CONV_PAD - CONV_K // 2)
                        acc = acc + w["wdw"][j:j + 1, cs] * upad_s[seg, a0:a0 + CONV_RB, cs]
                    row_blocks.append(acc)
            col_blocks.append(jnp.concatenate(row_blocks, axis=0))
        cu = jnp.concatenate(col_blocks, axis=1)
        mu = jnp.mean(cu, axis=-1, keepdims=True)
        cc = cu - mu
        cn = cc * lax.rsqrt(jnp.mean(cc * cc, axis=-1, keepdims=True) + EPS) * w["lng"][...] + w["lnb"][...]
        ca = (cn * _sigmoid(cn)).astype(BF16)
        br_a = _dot(ca, w["wco"][...])

        gm = _dot(hb, w["wgm"][...]) + w["bgm"][...]
        ma_s[rows, :] = _sigmoid(gm[:, :D_MODEL]) * br_a
        sgb_s[rows, :] = _sigmoid(gm[:, D_MODEL:])

        q_s[rows, :] = ((_dot(hb, w["wq"][...]) + w["bq"][...]) * (HEAD_DIM ** -0.5)).astype(BF16)
        v_s[rows, :] = (_dot(hb, w["wv"][...]) + w["bv"][...]).astype(BF16)
        so_s[rows, :] = _sigmoid(_dot(hb, w["wog"][...]) + w["bog"][...])
        kT_s[i] = (_dot_nt(w["wkT"][...], hb) + w["bk"][...]).astype(BF16)
        gcol_s[rows, :] = _dot(hb, w["wgif"][...]) + w["bgif"][...]
        grow_s[i] = _dot_nt(w["wgifT"][...], hb) + w["bgifT"][...]
        return carry

    if nsub == 1:
        phase1(0, 0)
    else:
        lax.fori_loop(0, nsub, phase1, 0)

    t_idx = lax.broadcasted_iota(jnp.int32, (L, L), 0)
    s_idx = lax.broadcasted_iota(jnp.int32, (L, L), 1)
    lower = s_idx <= t_idx
    upper = s_idx >= t_idx
    tril_b = lower.astype(F32).astype(BF16)
    triu_b = upper.astype(F32).astype(BF16)
    ones_col = (lax.broadcasted_iota(jnp.int32, (L, HEAD_DIM), 1) == 0).astype(F32).astype(BF16)

    def gate_sums(c):
        rows = slice(c * L, (c + 1) * L)
        gcol = gcol_s[rows, :]
        grow = grow_s[c]
        lf_col = _log_sigmoid(gcol)
        lf_row = _log_sigmoid(grow)
        parts = jnp.concatenate(_split3(lf_col), axis=1).astype(BF16)
        pc = _dot(tril_b, parts)
        pre_col = pc[:, :LANES] + pc[:, LANES:2 * LANES] + pc[:, 2 * LANES:]
        parts = jnp.concatenate(_split3(lf_row), axis=0).astype(BF16)
        pr = _dot(parts, triu_b)
        pre_row = pr[0:16] + pr[16:32] + pr[32:48]
        tot_col = pre_col[L - 1:L, :]
        tot_row = pre_row[:, L - 1:L]
        suf_col = tot_col - pre_col + lf_col
        suf_row = tot_row - pre_row + lf_row
        return grow, (pre_col, suf_col), (pre_row, suf_row), tot_row

    qk_cache = {}

    def unit(d, hd, c, gates, m, first_chunk, want_state):
        grow, cum_col, cum_row, tot_row = gates
        rows = slice(c * L, (c + 1) * L)
        hs = slice(hd * HEAD_DIM, (hd + 1) * HEAD_DIM)
        idx = d * N_HEADS + hd
        r_i = d * 2 * N_HEADS + hd
        r_f = r_i + N_HEADS
        bcol = cum_col[d][:, r_f:r_f + 1]
        brow = cum_row[d][r_f:r_f + 1, :]
        irow = grow[r_i:r_i + 1, :]
        btot = tot_row[r_f:r_f + 1, :]
        qc = q_s[rows, hs]
        kTc = kT_s[c, hs, :]
        vaug = jnp.concatenate([v_s[rows, hs], ones_col], axis=1)
        if nsub == 1 and (hd, c) in qk_cache:
            qk = qk_cache[(hd, c)]
        else:
            qk = _dot(qc, kTc)
            qk_cache[(hd, c)] = qk
        dmat = jnp.where(lower if d == 0 else upper, bcol - brow + irow, -jnp.inf)
        m_inter = bcol + m
        m_t = jnp.maximum(m_inter, jnp.max(dmat, axis=1, keepdims=True))
        s_mat = (qk * jnp.exp(dmat - m_t)).astype(BF16)
        nd = _dot(s_mat, vaug)
        if has_state or not first_chunk:
            w_inter = jnp.exp(m_inter - m_t)
            nd = nd + w_inter * _dot(qc, cst_s[idx].astype(BF16))
        den = nd[:, HEAD_DIM:HEAD_DIM + 1]
        h = nd[:, :HEAD_DIM] / jnp.maximum(jnp.abs(den), jnp.exp(-m_t))
        if d == 0:
            hm_s[rows, hs] = h
        else:
            hm_s[rows, hs] = hm_s[rows, hs] + h
        if not want_state:
            return m
        g = btot - brow + irow
        m_new = jnp.maximum(btot + m, jnp.max(g, axis=1, keepdims=True))
        wk = jnp.exp(g - m_new)
        kw = (kTc.astype(F32) * wk).astype(BF16)
        upd = _dot(kw, vaug)
        if has_state or not first_chunk:
            upd = upd + jnp.exp(btot + m - m_new) * cst_s[idx]
        cst_s[idx] = upd
        return m_new

    if has_state:
        for idx in range(N_UNITS):
            cst_s[idx] = c0_ref[0, idx]
        m_state = [m0_ref[0, idx:idx + 1, 0:1] for idx in range(N_UNITS)]
    else:
        m_state = [jnp.zeros((1, 1), F32) for _ in range(N_UNITS)]

    for d in range(2):
        order = list(range(nsub)) if d == 0 else list(range(nsub - 1, -1, -1))
        for pos, c in enumerate(order):
            gates = gate_sums(c)
            last = pos == nsub - 1
            for hd in range(N_HEADS):
                idx = d * N_HEADS + hd
                m_state[idx] = unit(d, hd, c, gates, m_state[idx], pos == 0, emit_state or not last)

    if emit_state:
        for idx in range(N_UNITS):
            caug = cst_s[idx]
            cout_ref[0, idx] = caug[:, :HEAD_DIM]
            nout_ref[0, idx:idx + 1, :] = caug[:, HEAD_DIM:].T[0:1, :]
            mout_ref[0, idx:idx + 1, :] = jnp.broadcast_to(m_state[idx], (1, LANES))

    e_iota = lax.broadcasted_iota(jnp.int32, (LANES, SUB), 0)
    g_of_e = lax.shift_right_logical(e_iota, 2)
    j_of_e = lax.bitwise_and(e_iota, EXPERTS_PER_GROUP - 1)

    def phase3(i, carry):
        r0 = pl.multiple_of(i * SUB, SUB)
        rows = pl.ds(r0, SUB)
        hm = hm_s[rows, :]
        heads = []
        for hd in range(N_HEADS):
            hh = hm[:, hd * HEAD_DIM:(hd + 1) * HEAD_DIM]
            heads.append(hh * lax.rsqrt(jnp.mean(hh * hh, axis=-1, keepdims=True) + EPS))
        hn = jnp.concatenate(heads, axis=1) * w["hng"][...]
        hb2 = (so_s[rows, :] * hn).astype(BF16)
        br_b = _dot(hb2, w["wmo"][...])
        mixed = (ma_s[rows, :] + sgb_s[rows, :] * br_b).astype(BF16)
        x1 = x_ref[0, rows, :] + mod_row(2) * _dot(mixed, w["wo"][...])
        x1_ref[0, rows, :] = x1
        xn = x1 * lax.rsqrt(jnp.mean(x1 * x1, axis=-1, keepdims=True) + EPS) * w["g2"][...]
        h2 = xn * (1.0 + mod_row(4)) + mod_row(3)
        h2_ref[0, rows, :] = h2.astype(BF16)

        lt = _dot_nt(w["wrtT"][...], h2, precision=lax.Precision.HIGHEST) + w["brtT"][...]
        gl = [lt[N_EXPERTS + g:N_EXPERTS + g + 1, :] for g in range(N_GROUPS)]
        best, gsel = gl[0], jnp.zeros((1, SUB), jnp.int32)
        for g in range(1, N_GROUPS):
            better = gl[g] > best
            gsel = jnp.where(better, g, gsel)
            best = jnp.where(better, gl[g], best)
        gp_sel = 1.0 / sum(jnp.exp(v - best) for v in gl)
        el = []
        for j in range(EXPERTS_PER_GROUP):
            v = lt[j:j + 1, :]
            for g in range(1, N_GROUPS):
                r = g * EXPERTS_PER_GROUP + j
                v = jnp.where(gsel == g, lt[r:r + 1, :], v)
            el.append(v)
        l1, e1 = el[0], jnp.zeros((1, SUB), jnp.int32)
        for j in range(1, EXPERTS_PER_GROUP):
            better = el[j] > l1
            e1 = jnp.where(better, j, e1)
            l1 = jnp.where(better, el[j], l1)
        l2 = jnp.full((1, SUB), -jnp.inf, F32)
        e2 = jnp.zeros((1, SUB), jnp.int32)
        for j in range(EXPERTS_PER_GROUP):
            better = jnp.logical_and(e1 != j, el[j] > l2)
            e2 = jnp.where(better, j, e2)
            l2 = jnp.where(better, el[j], l2)
        r2 = jnp.exp(l2 - l1)
        wt1 = gp_sel / (1.0 + r2)
        wt2 = gp_sel * r2 / (1.0 + r2)
        in_group = g_of_e == gsel
        comb_t = (jnp.where(jnp.logical_and(in_group, j_of_e == e1), wt1, 0.0)
                  + jnp.where(jnp.logical_and(in_group, j_of_e == e2), wt2, 0.0))
        comb_ref[0, rows, :] = comb_t.T
        return carry

    if nsub == 1:
        phase3(0, 0)
    else:
        lax.fori_loop(0, nsub, phase3, 0)


def _const_spec(a):
    nd = a.ndim
    return pl.BlockSpec(a.shape, lambda b, _nd=nd: (0,) * _nd, pipeline_mode=pl.Buffered(1))


def _mixer(x, mod, mod_index, weights, P, state=None, emit_state=False):
    B, T, _ = x.shape
    nsub = T // SUB
    has_state = state is not None
    seq_mode = {} if nsub == 1 else {"pipeline_mode": pl.Buffered(1)}
    in_specs = [
        pl.BlockSpec((1, T, D_MODEL), lambda b: (b, 0, 0), **seq_mode),
        pl.BlockSpec((1, N_ADA, D_MODEL), lambda b: (mod_index(b), 0, 0)),
    ]
    args = [x, mod]
    if has_state:
        caug0, m0 = state
        in_specs += [
            pl.BlockSpec((1, N_UNITS, HEAD_DIM, 2 * HEAD_DIM), lambda b: (b, 0, 0, 0)),
            pl.BlockSpec((1, N_UNITS, LANES), lambda b: (b, 0, 0)),
        ]
        args += [caug0, m0]
    for name in _MIXER_WEIGHTS:
        in_specs.append(_const_spec(weights[name]))
        args.append(weights[name])
    out_shape = [
        jax.ShapeDtypeStruct((B, T, D_MODEL), F32),
        jax.ShapeDtypeStruct((B, T, D_MODEL), BF16),
        jax.ShapeDtypeStruct((B, T, LANES), F32),
    ]
    out_specs = [
        pl.BlockSpec((1, T, D_MODEL), lambda b: (b, 0, 0), **seq_mode),
        pl.BlockSpec((1, T, D_MODEL), lambda b: (b, 0, 0), **seq_mode),
        pl.BlockSpec((1, T, LANES), lambda b: (b, 0, 0)),
    ]
    if emit_state:
        out_shape += [
            jax.ShapeDtypeStruct((B, N_UNITS, HEAD_DIM, HEAD_DIM), F32),
            jax.ShapeDtypeStruct((B, N_UNITS, HEAD_DIM), F32),
            jax.ShapeDtypeStruct((B, N_UNITS, LANES), F32),
        ]
        out_specs += [
            pl.BlockSpec((1, N_UNITS, HEAD_DIM, HEAD_DIM), lambda b: (b, 0, 0, 0)),
            pl.BlockSpec((1, N_UNITS, HEAD_DIM), lambda b: (b, 0, 0)),
            pl.BlockSpec((1, N_UNITS, LANES), lambda b: (b, 0, 0)),
        ]
    scratch = [
        pltpu.VMEM((T, D_MLSTM), BF16),
        pltpu.VMEM((nsub, D_MLSTM, SUB), BF16),
        pltpu.VMEM((T, D_MLSTM), BF16),
        pltpu.VMEM((T, D_MLSTM), F32),
        pltpu.VMEM((T, LANES), F32),
        pltpu.VMEM((nsub, 16, SUB), F32),
        pltpu.VMEM((T, D_MODEL), F32),
        pltpu.VMEM((T, D_MODEL), F32),
        pltpu.VMEM((T, D_MLSTM), F32),
        pltpu.VMEM((N_UNITS, HEAD_DIM, 2 * HEAD_DIM), F32),
        pltpu.VMEM((SUB // P, P + 2 * CONV_PAD, D_CONV), F32),
    ]
    return pl.pallas_call(
        functools.partial(_mixer_kernel, T, P, has_state, emit_state),
        grid=(B,),
        in_specs=in_specs,
        out_specs=out_specs,
        out_shape=out_shape,
        scratch_shapes=scratch,
        compiler_params=pltpu.CompilerParams(
            dimension_semantics=("arbitrary",), vmem_limit_bytes=VMEM_LIMIT),
        name="mixer_T%d" % T,
    )(*args)


def _moe_kernel(x1_ref, h2_ref, comb_ref, mod_ref, wg_ref, wu_ref, wd_ref, gf_ref, y_ref, acc_ref):
    e = pl.program_id(1)

    @pl.when(e == 0)
    def _():
        acc_ref[...] = jnp.zeros_like(acc_ref)

    hb = h2_ref[...]
    g = _dot(hb, wg_ref[0].astype(BF16))
    u = _dot(hb, wu_ref[0].astype(BF16))
    comb = comb_ref[...]
    lane = lax.broadcasted_iota(jnp.int32, comb.shape, 1)
    cw = jnp.sum(jnp.where(lane == e, comb, 0.0), axis=1, keepdims=True)
    act = (g * _sigmoid(g) * u * cw).astype(BF16)
    acc_ref[...] += _dot(act, wd_ref[0].astype(BF16))

    @pl.when(e == N_EXPERTS - 1)
    def _():
        x2 = x1_ref[...] + mod_ref[0, 5:6, :] * acc_ref[...]
        y_ref[...] = x2 * lax.rsqrt(jnp.mean(x2 * x2, axis=-1, keepdims=True) + EPS) * gf_ref[...]


def _moe(x1, h2, comb, mod, mod_index, wg, wu, wd, gf, tm):
    n = x1.shape[0]
    return pl.pallas_call(
        _moe_kernel,
        grid=(n // tm, N_EXPERTS),
        in_specs=[
            pl.BlockSpec((tm, D_MODEL), lambda i, e: (i, 0)),
            pl.BlockSpec((tm, D_MODEL), lambda i, e: (i, 0)),
            pl.BlockSpec((tm, LANES), lambda i, e: (i, 0)),
            pl.BlockSpec((1, N_ADA, D_MODEL), lambda i, e: (mod_index(i), 0, 0)),
            pl.BlockSpec((1, D_MODEL, D_EXPERT), lambda i, e: (e, 0, 0)),
            pl.BlockSpec((1, D_MODEL, D_EXPERT), lambda i, e: (e, 0, 0)),
            pl.BlockSpec((1, D_EXPERT, D_MODEL), lambda i, e: (e, 0, 0)),
            pl.BlockSpec((1, D_MODEL), lambda i, e: (0, 0)),
        ],
        out_specs=pl.BlockSpec((tm, D_MODEL), lambda i, e: (i, 0)),
        out_shape=jax.ShapeDtypeStruct((n, D_MODEL), F32),
        scratch_shapes=[pltpu.VMEM((tm, D_MODEL), F32)],
        compiler_params=pltpu.CompilerParams(
            dimension_semantics=("arbitrary", "arbitrary"), vmem_limit_bytes=VMEM_LIMIT),
        name="moe",
    )(x1, h2, comb, mod, wg, wu, wd, gf)


def _prep_weights(norm1_g, w_in, b_in, b_gates, w_dw, b_dw, conv_ln_g, conv_ln_b, w_conv_out,
                  mlstm_hn_g, w_mlstm_out, w_o, norm2_g, w_rg, b_rg, w_re, b_re):
    s_a = 2 * D_CONV
    s_q = s_a + D_MLSTM
    s_k = s_q + D_MLSTM
    s_v = s_k + D_MLSTM
    s_o = s_v + D_MLSTM
    s_g = s_o + 4 * N_HEADS
    row = lambda v: v.reshape(1, -1).astype(F32)
    wb = w_in.astype(BF16)
    ngate = 4 * N_HEADS
    wgif = jnp.zeros((D_MODEL, LANES), BF16).at[:, :ngate].set(wb[:, s_o:s_g])
    bg = b_in[s_o:s_g] + b_gates.reshape(-1)
    bgif = jnp.zeros((1, LANES), F32).at[0, :ngate].set(bg)
    wrt = jnp.concatenate([w_re, w_rg], axis=1)
    wrtT = jnp.zeros((32, D_MODEL), F32).at[:N_EXPERTS + N_GROUPS].set(wrt.T)
    brtT = jnp.zeros((32, 1), F32).at[:N_EXPERTS + N_GROUPS, 0].set(jnp.concatenate([b_re, b_rg]))
    return {
        "g1": row(norm1_g),
        "wag": wb[:, :s_a], "bag": row(b_in[:s_a]),
        "wq": wb[:, s_a:s_q], "bq": row(b_in[s_a:s_q]),
        "wkT": wb[:, s_q:s_k].T, "bk": b_in[s_q:s_k].reshape(-1, 1),
        "wv": wb[:, s_k:s_v], "bv": row(b_in[s_k:s_v]),
        "wog": wb[:, s_v:s_o], "bog": row(b_in[s_v:s_o]),
        "wgif": wgif, "bgif": bgif,
        "wgifT": wb[:, s_o:s_g].T, "bgifT": bg.reshape(-1, 1),
        "wgm": wb[:, s_g:], "bgm": row(b_in[s_g:]),
        "wdw": w_dw.astype(F32), "bdw": row(b_dw), "lng": row(conv_ln_g), "lnb": row(conv_ln_b),
        "wco": w_conv_out.astype(BF16), "hng": row(mlstm_hn_g), "wmo": w_mlstm_out.astype(BF16),
        "wo": w_o.astype(BF16), "g2": row(norm2_g), "wrtT": wrtT, "brtT": brtT,
    }


def kernel(x_prompt, x_sample, state_C, state_n, state_m, c, c_ctx, norm1_g, w_ada, b_ada, w_in, b_in, b_gates, w_dw, b_dw, conv_ln_g, conv_ln_b, w_conv_out, mlstm_hn_g, w_mlstm_out, w_o, norm2_g, w_rg, b_rg, w_re, b_re, w_e_gate, w_e_up, w_e_down, norm_final_g):
    B, S, _ = x_prompt.shape
    Bd, Sd, _ = x_sample.shape
    assert w_ada.shape[0] == 1, "single trunk layer"
    assert S == SUB and Sd % SUB == 0

    cin = jnp.zeros((8, D_MODEL), F32).at[0].set(c_ctx).at[1:1 + Bd].set(c)
    mod = _ada(cin, w_ada[0], b_ada[0].reshape(1, -1)).reshape(8, N_ADA, D_MODEL)

    wts = _prep_weights(norm1_g[0], w_in[0], b_in[0], b_gates[0], w_dw[0], b_dw[0], conv_ln_g[0],
                        conv_ln_b[0], w_conv_out[0], mlstm_hn_g[0], w_mlstm_out[0], w_o[0],
                        norm2_g[0], w_rg[0], b_rg[0], w_re[0], b_re[0])

    x1p, h2p, cbp, c_new, n_new, m_new = _mixer(
        x_prompt, mod, lambda b: 0, wts, P=S, emit_state=True)

    sc = state_C[:, 0].reshape(Bd, N_UNITS, HEAD_DIM, HEAD_DIM)
    sn = state_n[:, 0].reshape(Bd, N_UNITS, HEAD_DIM, 1)
    caug0 = jnp.concatenate([sc, sn, jnp.zeros((Bd, N_UNITS, HEAD_DIM, HEAD_DIM - 1), F32)], axis=-1)
    m0 = jnp.broadcast_to(state_m[:, 0].reshape(Bd, N_UNITS, 1), (Bd, N_UNITS, LANES))
    x1s, h2s, cbs = _mixer(x_sample, mod, lambda b: 1 + b, wts, P=GRID_W, state=(caug0, m0))

    gf = norm_final_g.reshape(1, -1)
    tm = 1024
    yp = _moe(x1p.reshape(B * S, D_MODEL), h2p.reshape(B * S, D_MODEL), cbp.reshape(B * S, LANES),
              mod, lambda i: 0, w_e_gate[0], w_e_up[0], w_e_down[0], gf, tm)
    ys = _moe(x1s.reshape(Bd * Sd, D_MODEL), h2s.reshape(Bd * Sd, D_MODEL), cbs.reshape(Bd * Sd, LANES),
              mod, lambda i: 1 + (i * tm) // Sd, w_e_gate[0], w_e_up[0], w_e_down[0], gf, tm)

    return (yp.reshape(B, S, D_MODEL), ys.reshape(Bd, Sd, D_MODEL),
            c_new.reshape(B, 1, 2, N_HEADS, HEAD_DIM, HEAD_DIM),
            n_new.reshape(B, 1, 2, N_HEADS, HEAD_DIM),
            m_new[:, :, 0].reshape(B, 1, 2, N_HEADS))
```

```python
import functools

import jax
import jax.numpy as jnp
from jax import lax
from jax.experimental import pallas as pl
from jax.experimental.pallas import tpu as pltpu

D_MODEL = 1024
D_CONV = 512
CONV_K = 31
D_MLSTM = 512
N_HEADS = 4
HEAD_DIM = D_MLSTM // N_HEADS
N_GROUPS = 4
EXPERTS_PER_GROUP = 4
N_EXPERTS = N_GROUPS * EXPERTS_PER_GROUP
D_EXPERT = 256
N_ADA = 6
EPS = 1e-6
GRID_W = 64

LANES = 128
SUB = 256
CONV_PAD = 16
CONV_RB = 64
N_UNITS = 2 * N_HEADS
VMEM_LIMIT = 58 * 1024 * 1024

BF16 = jnp.bfloat16
F32 = jnp.float32
NT_DIMS = (((1,), (1,)), ((), ()))


def _dot(a, b):
    return jnp.dot(a, b, preferred_element_type=F32)


def _dot_nt(a, b, precision=None):
    return lax.dot_general(a, b, NT_DIMS, preferred_element_type=F32, precision=precision)


def _sigmoid(x):
    return 0.5 * jnp.tanh(0.5 * x) + 0.5


def _log_sigmoid(x):
    return jnp.minimum(x, 0.0) - jnp.log1p(jnp.exp(-jnp.abs(x)))


def _split3(x):
    hi = x.astype(BF16).astype(F32)
    r1 = x - hi
    mid = r1.astype(BF16).astype(F32)
    lo = (r1 - mid).astype(BF16).astype(F32)
    return hi, mid, lo


def _ada_kernel(c_ref, w_ref, b_ref, o_ref):
    c = c_ref[...]
    s = (c * _sigmoid(c)).astype(BF16)
    o_ref[...] = _dot(s, w_ref[...].astype(BF16)) + b_ref[...]


def _ada(cin, w_ada, b_ada):
    n = w_ada.shape[1]
    bn = 1024
    return pl.pallas_call(
        _ada_kernel,
        grid=(n // bn,),
        in_specs=[
            pl.BlockSpec((8, D_MODEL), lambda j: (0, 0)),
            pl.BlockSpec((D_MODEL, bn), lambda j: (0, j)),
            pl.BlockSpec((1, bn), lambda j: (0, j)),
        ],
        out_specs=pl.BlockSpec((8, bn), lambda j: (0, j)),
        out_shape=jax.ShapeDtypeStruct((8, n), F32),
        compiler_params=pltpu.CompilerParams(dimension_semantics=("arbitrary",)),
        name="ada",
    )(cin, w_ada, b_ada)


_MIXER_WEIGHTS = (
    "g1", "wag", "bag", "wq", "bq", "wkT", "bk", "wv", "bv", "wog", "bog",
    "wgif", "bgif", "wgifT", "bgifT", "wgm", "bgm", "wdw", "bdw", "lng", "lnb",
    "wco", "hng", "wmo", "wo", "g2", "wrt2", "brtT",
)


def _conv_block(upad_s, seg, base, cs, wdw_ref, bdw_ref):
    sub = 8
    first = CONV_PAD - CONV_K // 2
    acc = jnp.broadcast_to(bdw_ref[0:1, cs], (CONV_RB, LANES))
    for r in range(sub):
        z = None
        for a in range((CONV_K + first + sub - 1) // sub):
            j = sub * a + r - first
            if 0 <= j < CONV_K:
                lo = base + sub * a
                term = wdw_ref[j:j + 1, cs] * upad_s[seg, lo:lo + CONV_RB + sub, cs]
                z = term if z is None else z + term
        acc = acc + z[r:r + CONV_RB, :]
    return acc


def _mixer_kernel(T, P, has_state, emit_state, *refs):
    nsub = T // SUB
    nseg = SUB // P
    L = SUB
    it = iter(refs)
    x_ref = next(it)
    mod_ref = next(it)
    if has_state:
        c0_ref = next(it)
        m0_ref = next(it)
    w = {name: next(it) for name in _MIXER_WEIGHTS}
    x1_ref = next(it)
    h2_ref = next(it)
    comb_ref = next(it)
    if emit_state:
        cout_ref = next(it)
        nout_ref = next(it)
        mout_ref = next(it)
    (q_s, kT_s, v_s, so_s, gcol_s, grow_s, ma_s, sgb_s, hm_s, cst_s, upad_s) = [next(it) for _ in range(11)]

    def mod_row(i):
        return mod_ref[0, i:i + 1, :]

    zpad = jnp.zeros((CONV_PAD, D_CONV), F32)
    for seg in range(nseg):
        upad_s[seg, 0:CONV_PAD, :] = zpad
        upad_s[seg, CONV_PAD + P:CONV_PAD + P + CONV_PAD, :] = zpad

    def phase1(i, carry):
        r0 = pl.multiple_of(i * SUB, SUB)
        rows = pl.ds(r0, SUB)
        x = x_ref[0, rows, :]
        xn = x * lax.rsqrt(jnp.mean(x * x, axis=-1, keepdims=True) + EPS) * w["g1"][...]
        hb = (xn * (1.0 + mod_row(1)) + mod_row(0)).astype(BF16)

        ag = _dot(hb, w["wag"][...]) + w["bag"][...]
        u = ag[:, :D_CONV] * _sigmoid(ag[:, D_CONV:])
        for seg in range(nseg):
            upad_s[seg, CONV_PAD:CONV_PAD + P, :] = u[seg * P:(seg + 1) * P, :]
        col_blocks = []
        for cb in range(D_CONV // LANES):
            cs = slice(cb * LANES, (cb + 1) * LANES)
            row_blocks = []
            for seg in range(nseg):
                for rb in range(P // CONV_RB):
                    row_blocks.append(_conv_block(upad_s, seg, rb * CONV_RB, cs, w["wdw"], w["bdw"]))
            col_blocks.append(jnp.concatenate(row_blocks, axis=0))
        cu = jnp.concatenate(col_blocks, axis=1)
        mu = jnp.mean(cu, axis=-1, keepdims=True)
        cc = cu - mu
        cn = cc * lax.rsqrt(jnp.mean(cc * cc, axis=-1, keepdims=True) + EPS) * w["lng"][...] + w["lnb"][...]
        ca = (cn * _sigmoid(cn)).astype(BF16)
        br_a = _dot(ca, w["wco"][...])

        gm = _dot(hb, w["wgm"][...]) + w["bgm"][...]
        ma_s[rows, :] = _sigmoid(gm[:, :D_MODEL]) * br_a
        sgb_s[rows, :] = _sigmoid(gm[:, D_MODEL:])

        q_s[rows, :] = ((_dot(hb, w["wq"][...]) + w["bq"][...]) * (HEAD_DIM ** -0.5)).astype(BF16)
        v_s[rows, :] = (_dot(hb, w["wv"][...]) + w["bv"][...]).astype(BF16)
        so_s[rows, :] = _sigmoid(_dot(hb, w["wog"][...]) + w["bog"][...])
        kT_s[i] = (_dot_nt(w["wkT"][...], hb) + w["bk"][...]).astype(BF16)
        gcol_s[rows, :] = _dot(hb, w["wgif"][...]) + w["bgif"][...]
        grow_s[i] = _dot_nt(w["wgifT"][...], hb) + w["bgifT"][...]
        return carry

    if nsub == 1:
        phase1(0, 0)
    else:
        lax.fori_loop(0, nsub, phase1, 0)

    t_idx = lax.broadcasted_iota(jnp.int32, (L, L), 0)
    s_idx = lax.broadcasted_iota(jnp.int32, (L, L), 1)
    lower = s_idx <= t_idx
    upper = s_idx >= t_idx
    tril_b = lower.astype(F32).astype(BF16)
    triu_b = upper.astype(F32).astype(BF16)
    ones_col = (lax.broadcasted_iota(jnp.int32, (L, HEAD_DIM), 1) == 0).astype(F32).astype(BF16)

    def gate_sums(c):
        rows = slice(c * L, (c + 1) * L)
        gcol = gcol_s[rows, :]
        grow = grow_s[c]
        lf_col = _log_sigmoid(gcol)
        lf_row = _log_sigmoid(grow)
        parts = jnp.concatenate(_split3(lf_col), axis=1).astype(BF16)
        pc = _dot(tril_b, parts)
        pre_col = pc[:, :LANES] + pc[:, LANES:2 * LANES] + pc[:, 2 * LANES:]
        parts = jnp.concatenate(_split3(lf_row), axis=0).astype(BF16)
        pr = _dot(parts, triu_b)
        pre_row = pr[0:16] + pr[16:32] + pr[32:48]
        tot_col = pre_col[L - 1:L, :]
        tot_row = pre_row[:, L - 1:L]
        suf_col = tot_col - pre_col + lf_col
        suf_row = tot_row - pre_row + lf_row
        return grow, (pre_col, suf_col), (pre_row, suf_row), tot_row

    qk_cache = {}

    def unit(d, hd, c, gates, m, first_chunk, want_state):
        grow, cum_col, cum_row, tot_row = gates
        rows = slice(c * L, (c + 1) * L)
        hs = slice(hd * HEAD_DIM, (hd + 1) * HEAD_DIM)
        idx = d * N_HEADS + hd
        r_i = d * 2 * N_HEADS + hd
        r_f = r_i + N_HEADS
        bcol = cum_col[d][:, r_f:r_f + 1]
        brow = cum_row[d][r_f:r_f + 1, :]
        irow = grow[r_i:r_i + 1, :]
        btot = tot_row[r_f:r_f + 1, :]
        qc = q_s[rows, hs]
        kTc = kT_s[c, hs, :]
        vaug = jnp.concatenate([v_s[rows, hs], ones_col], axis=1)
        if nsub == 1 and (hd, c) in qk_cache:
            qk = qk_cache[(hd, c)]
        else:
            qk = _dot(qc, kTc)
            qk_cache[(hd, c)] = qk
        dmat = jnp.where(lower if d == 0 else upper, bcol + (irow - brow), -jnp.inf)
        m_inter = bcol + m
        m_t = jnp.maximum(m_inter, jnp.max(dmat, axis=1, keepdims=True))
        s_mat = (qk * jnp.exp(dmat - m_t)).astype(BF16)
        nd = _dot(s_mat, vaug)
        if has_state or not first_chunk:
            w_inter = jnp.exp(m_inter - m_t)
            nd = nd + w_inter * _dot(qc, cst_s[idx].astype(BF16))
        den = nd[:, HEAD_DIM:HEAD_DIM + 1]
        h = nd[:, :HEAD_DIM] * (1.0 / jnp.maximum(jnp.abs(den), jnp.exp(-m_t)))
        if d == 0:
            hm_s[rows, hs] = h
        else:
            hm_s[rows, hs] = hm_s[rows, hs] + h
        if not want_state:
            return m
        g = btot - brow + irow
        m_new = jnp.maximum(btot + m, jnp.max(g, axis=1, keepdims=True))
        wk = jnp.exp(g - m_new)
        kw = (kTc.astype(F32) * wk).astype(BF16)
        upd = _dot(kw, vaug)
        if has_state or not first_chunk:
            upd = upd + jnp.exp(btot + m - m_new) * cst_s[idx]
        cst_s[idx] = upd
        return m_new

    if has_state:
        for idx in range(N_UNITS):
            cst_s[idx] = c0_ref[0, idx]
        m_state = [m0_ref[0, idx:idx + 1, 0:1] for idx in range(N_UNITS)]
    else:
        m_state = [jnp.zeros((1, 1), F32) for _ in range(N_UNITS)]

    for d in range(2):
        order = list(range(nsub)) if d == 0 else list(range(nsub - 1, -1, -1))
        for pos, c in enumerate(order):
            gates = gate_sums(c)
            last = pos == nsub - 1
            for hd in range(N_HEADS):
                idx = d * N_HEADS + hd
                m_state[idx] = unit(d, hd, c, gates, m_state[idx], pos == 0, emit_state or not last)

    if emit_state:
        for idx in range(N_UNITS):
            caug = cst_s[idx]
            cout_ref[0, idx] = caug[:, :HEAD_DIM]
            nout_ref[0, idx:idx + 1, :] = caug[:, HEAD_DIM:].T[0:1, :]
            mout_ref[0, idx:idx + 1, :] = jnp.broadcast_to(m_state[idx], (1, LANES))

    e_iota = lax.broadcasted_iota(jnp.int32, (LANES, SUB), 0)
    g_of_e = lax.shift_right_logical(e_iota, 2)
    j_of_e = lax.bitwise_and(e_iota, EXPERTS_PER_GROUP - 1)

    def phase3(i, carry):
        r0 = pl.multiple_of(i * SUB, SUB)
        rows = pl.ds(r0, SUB)
        hm = hm_s[rows, :]
        heads = []
        for hd in range(N_HEADS):
            hh = hm[:, hd * HEAD_DIM:(hd + 1) * HEAD_DIM]
            heads.append(hh * lax.rsqrt(jnp.mean(hh * hh, axis=-1, keepdims=True) + EPS))
        hn = jnp.concatenate(heads, axis=1) * w["hng"][...]
        hb2 = (so_s[rows, :] * hn).astype(BF16)
        br_b = _dot(hb2, w["wmo"][...])
        mixed = (ma_s[rows, :] + sgb_s[rows, :] * br_b).astype(BF16)
        x1 = x_ref[0, rows, :] + mod_row(2) * _dot(mixed, w["wo"][...])
        x1_ref[0, rows, :] = x1
        xn = x1 * lax.rsqrt(jnp.mean(x1 * x1, axis=-1, keepdims=True) + EPS) * w["g2"][...]
        h2 = xn * (1.0 + mod_row(4)) + mod_row(3)
        h2_ref[0, rows, :] = h2.astype(BF16)

        h2_hi = h2.astype(BF16)
        h2_lo = (h2 - h2_hi.astype(F32)).astype(BF16)
        lg = _dot(h2_hi, w["wrt2"][...])
        lg = lg[:, :LANES] + lg[:, LANES:] + _dot(h2_lo, w["wrt2"][:, :LANES])
        lt = lg.T + w["brtT"][...]
        gl = [lt[N_EXPERTS + g:N_EXPERTS + g + 1, :] for g in range(N_GROUPS)]
        best, gsel = gl[0], jnp.zeros((1, SUB), jnp.int32)
        for g in range(1, N_GROUPS):
            better = gl[g] > best
            gsel = jnp.where(better, g, gsel)
            best = jnp.where(better, gl[g], best)
        gp_sel = 1.0 / sum(jnp.exp(v - best) for v in gl)
        el = []
        for j in range(EXPERTS_PER_GROUP):
            v = lt[j:j + 1, :]
            for g in range(1, N_GROUPS):
                r = g * EXPERTS_PER_GROUP + j
                v = jnp.where(gsel == g, lt[r:r + 1, :], v)
            el.append(v)
        l1, e1 = el[0], jnp.zeros((1, SUB), jnp.int32)
        for j in range(1, EXPERTS_PER_GROUP):
            better = el[j] > l1
            e1 = jnp.where(better, j, e1)
            l1 = jnp.where(better, el[j], l1)
        l2 = jnp.full((1, SUB), -jnp.inf, F32)
        e2 = jnp.zeros((1, SUB), jnp.int32)
        for j in range(EXPERTS_PER_GROUP):
            better = jnp.logical_and(e1 != j, el[j] > l2)
            e2 = jnp.where(better, j, e2)
            l2 = jnp.where(better, el[j], l2)
        r2 = jnp.exp(l2 - l1)
        wt1 = gp_sel / (1.0 + r2)
        wt2 = gp_sel * r2 / (1.0 + r2)
        in_group = g_of_e == gsel
        comb_t = (jnp.where(jnp.logical_and(in_group, j_of_e == e1), wt1, 0.0)
                  + jnp.where(jnp.logical_and(in_group, j_of_e == e2), wt2, 0.0))
        comb_ref[0, rows, :] = comb_t.T
        return carry

    if nsub == 1:
        phase3(0, 0)
    else:
        lax.fori_loop(0, nsub, phase3, 0)


def _const_spec(a):
    nd = a.ndim
    return pl.BlockSpec(a.shape, lambda b, _nd=nd: (0,) * _nd, pipeline_mode=pl.Buffered(1))


def _mixer(x, mod, mod_index, weights, P, state=None, emit_state=False):
    B, T, _ = x.shape
    nsub = T // SUB
    has_state = state is not None
    seq_mode = {} if nsub == 1 else {"pipeline_mode": pl.Buffered(1)}
    in_specs = [
        pl.BlockSpec((1, T, D_MODEL), lambda b: (b, 0, 0), **seq_mode),
        pl.BlockSpec((1, N_ADA, D_MODEL), lambda b: (mod_index(b), 0, 0)),
    ]
    args = [x, mod]
    if has_state:
        caug0, m0 = state
        in_specs += [
            pl.BlockSpec((1, N_UNITS, HEAD_DIM, 2 * HEAD_DIM), lambda b: (b, 0, 0, 0)),
            pl.BlockSpec((1, N_UNITS, LANES), lambda b: (b, 0, 0)),
        ]
        args += [caug0, m0]
    for name in _MIXER_WEIGHTS:
        in_specs.append(_const_spec(weights[name]))
        args.append(weights[name])
    out_shape = [
        jax.ShapeDtypeStruct((B, T, D_MODEL), F32),
        jax.ShapeDtypeStruct((B, T, D_MODEL), BF16),
        jax.ShapeDtypeStruct((B, T, LANES), F32),
    ]
    out_specs = [
        pl.BlockSpec((1, T, D_MODEL), lambda b: (b, 0, 0), **seq_mode),
        pl.BlockSpec((1, T, D_MODEL), lambda b: (b, 0, 0), **seq_mode),
        pl.BlockSpec((1, T, LANES), lambda b: (b, 0, 0)),
    ]
    if emit_state:
        out_shape += [
            jax.ShapeDtypeStruct((B, N_UNITS, HEAD_DIM, HEAD_DIM), F32),
            jax.ShapeDtypeStruct((B, N_UNITS, HEAD_DIM), F32),
            jax.ShapeDtypeStruct((B, N_UNITS, LANES), F32),
        ]
        out_specs += [
            pl.BlockSpec((1, N_UNITS, HEAD_DIM, HEAD_DIM), lambda b: (b, 0, 0, 0)),
            pl.BlockSpec((1, N_UNITS, HEAD_DIM), lambda b: (b, 0, 0)),
            pl.BlockSpec((1, N_UNITS, LANES), lambda b: (b, 0, 0)),
        ]
    scratch = [
        pltpu.VMEM((T, D_MLSTM), BF16),
        pltpu.VMEM((nsub, D_MLSTM, SUB), BF16),
        pltpu.VMEM((T, D_MLSTM), BF16),
        pltpu.VMEM((T, D_MLSTM), F32),
        pltpu.VMEM((T, LANES), F32),
        pltpu.VMEM((nsub, 16, SUB), F32),
        pltpu.VMEM((T, D_MODEL), F32),
        pltpu.VMEM((T, D_MODEL), F32),
        pltpu.VMEM((T, D_MLSTM), F32),
        pltpu.VMEM((N_UNITS, HEAD_DIM, 2 * HEAD_DIM), F32),
        pltpu.VMEM((SUB // P, P + 2 * CONV_PAD, D_CONV), F32),
    ]
    return pl.pallas_call(
        functools.partial(_mixer_kernel, T, P, has_state, emit_state),
        grid=(B,),
        in_specs=in_specs,
        out_specs=out_specs,
        out_shape=out_shape,
        scratch_shapes=scratch,
        compiler_params=pltpu.CompilerParams(
            dimension_semantics=("arbitrary",), vmem_limit_bytes=VMEM_LIMIT),
        name="mixer_T%d" % T,
    )(*args)


def _moe_kernel(x1_ref, h2_ref, comb_ref, mod_ref, wg_ref, wu_ref, wd_ref, gf_ref, y_ref, acc_ref):
    e = pl.program_id(1)

    @pl.when(e == 0)
    def _():
        acc_ref[...] = jnp.zeros_like(acc_ref)

    hb = h2_ref[...]
    g = _dot(hb, wg_ref[0].astype(BF16))
    u = _dot(hb, wu_ref[0].astype(BF16))
    comb = comb_ref[...]
    lane = lax.broadcasted_iota(jnp.int32, comb.shape, 1)
    cw = jnp.sum(jnp.where(lane == e, comb, 0.0), axis=1, keepdims=True)
    act = (g * _sigmoid(g) * u * cw).astype(BF16)
    acc_ref[...] += _dot(act, wd_ref[0].astype(BF16))

    @pl.when(e == N_EXPERTS - 1)
    def _():
        x2 = x1_ref[...] + mod_ref[0, 5:6, :] * acc_ref[...]
        y_ref[...] = x2 * lax.rsqrt(jnp.mean(x2 * x2, axis=-1, keepdims=True) + EPS) * gf_ref[...]


def _moe(x1, h2, comb, mod, mod_index, wg, wu, wd, gf, tm):
    n = x1.shape[0]
    return pl.pallas_call(
        _moe_kernel,
        grid=(n // tm, N_EXPERTS),
        in_specs=[
            pl.BlockSpec((tm, D_MODEL), lambda i, e: (i, 0)),
            pl.BlockSpec((tm, D_MODEL), lambda i, e: (i, 0)),
            pl.BlockSpec((tm, LANES), lambda i, e: (i, 0)),
            pl.BlockSpec((1, N_ADA, D_MODEL), lambda i, e: (mod_index(i), 0, 0)),
            pl.BlockSpec((1, D_MODEL, D_EXPERT), lambda i, e: (e, 0, 0)),
            pl.BlockSpec((1, D_MODEL, D_EXPERT), lambda i, e: (e, 0, 0)),
            pl.BlockSpec((1, D_EXPERT, D_MODEL), lambda i, e: (e, 0, 0)),
            pl.BlockSpec((1, D_MODEL), lambda i, e: (0, 0)),
        ],
        out_specs=pl.BlockSpec((tm, D_MODEL), lambda i, e: (i, 0)),
        out_shape=jax.ShapeDtypeStruct((n, D_MODEL), F32),
        scratch_shapes=[pltpu.VMEM((tm, D_MODEL), F32)],
        compiler_params=pltpu.CompilerParams(
            dimension_semantics=("arbitrary", "arbitrary"), vmem_limit_bytes=VMEM_LIMIT),
        name="moe",
    )(x1, h2, comb, mod, wg, wu, wd, gf)


def _prep_weights(norm1_g, w_in, b_in, b_gates, w_dw, b_dw, conv_ln_g, conv_ln_b, w_conv_out,
                  mlstm_hn_g, w_mlstm_out, w_o, norm2_g, w_rg, b_rg, w_re, b_re):
    s_a = 2 * D_CONV
    s_q = s_a + D_MLSTM
    s_k = s_q + D_MLSTM
    s_v = s_k + D_MLSTM
    s_o = s_v + D_MLSTM
    s_g = s_o + 4 * N_HEADS
    row = lambda v: v.reshape(1, -1).astype(F32)
    wb = w_in.astype(BF16)
    ngate = 4 * N_HEADS
    wgif = jnp.zeros((D_MODEL, LANES), BF16).at[:, :ngate].set(wb[:, s_o:s_g])
    bg = b_in[s_o:s_g] + b_gates.reshape(-1)
    bgif = jnp.zeros((1, LANES), F32).at[0, :ngate].set(bg)
    wrt = jnp.concatenate([w_re, w_rg], axis=1)
    wrt = jnp.zeros((D_MODEL, LANES), F32).at[:, :N_EXPERTS + N_GROUPS].set(wrt)
    wrt_hi = wrt.astype(BF16)
    wrt2 = jnp.concatenate([wrt_hi, (wrt - wrt_hi.astype(F32)).astype(BF16)], axis=1)
    brtT = jnp.zeros((LANES, 1), F32).at[:N_EXPERTS + N_GROUPS, 0].set(jnp.concatenate([b_re, b_rg]))
    return {
        "g1": row(norm1_g),
        "wag": wb[:, :s_a], "bag": row(b_in[:s_a]),
        "wq": wb[:, s_a:s_q], "bq": row(b_in[s_a:s_q]),
        "wkT": wb[:, s_q:s_k].T, "bk": b_in[s_q:s_k].reshape(-1, 1),
        "wv": wb[:, s_k:s_v], "bv": row(b_in[s_k:s_v]),
        "wog": wb[:, s_v:s_o], "bog": row(b_in[s_v:s_o]),
        "wgif": wgif, "bgif": bgif,
        "wgifT": wb[:, s_o:s_g].T, "bgifT": bg.reshape(-1, 1),
        "wgm": wb[:, s_g:], "bgm": row(b_in[s_g:]),
        "wdw": w_dw.astype(F32), "bdw": row(b_dw), "lng": row(conv_ln_g), "lnb": row(conv_ln_b),
        "wco": w_conv_out.astype(BF16), "hng": row(mlstm_hn_g), "wmo": w_mlstm_out.astype(BF16),
        "wo": w_o.astype(BF16), "g2": row(norm2_g), "wrt2": wrt2, "brtT": brtT,
    }


def kernel(x_prompt, x_sample, state_C, state_n, state_m, c, c_ctx, norm1_g, w_ada, b_ada, w_in, b_in, b_gates, w_dw, b_dw, conv_ln_g, conv_ln_b, w_conv_out, mlstm_hn_g, w_mlstm_out, w_o, norm2_g, w_rg, b_rg, w_re, b_re, w_e_gate, w_e_up, w_e_down, norm_final_g):
    B, S, _ = x_prompt.shape
    Bd, Sd, _ = x_sample.shape
    assert w_ada.shape[0] == 1, "single trunk layer"
    assert S == SUB and Sd % SUB == 0

    cin = jnp.zeros((8, D_MODEL), F32).at[0].set(c_ctx).at[1:1 + Bd].set(c)
    mod = _ada(cin, w_ada[0], b_ada[0].reshape(1, -1)).reshape(8, N_ADA, D_MODEL)

    wts = _prep_weights(norm1_g[0], w_in[0], b_in[0], b_gates[0], w_dw[0], b_dw[0], conv_ln_g[0],
                        conv_ln_b[0], w_conv_out[0], mlstm_hn_g[0], w_mlstm_out[0], w_o[0],
                        norm2_g[0], w_rg[0], b_rg[0], w_re[0], b_re[0])

    x1p, h2p, cbp, c_new, n_new, m_new = _mixer(
        x_prompt, mod, lambda b: 0, wts, P=S, emit_state=True)

    sc = state_C[:, 0].reshape(Bd, N_UNITS, HEAD_DIM, HEAD_DIM)
    sn = state_n[:, 0].reshape(Bd, N_UNITS, HEAD_DIM, 1)
    caug0 = jnp.concatenate([sc, sn, jnp.zeros((Bd, N_UNITS, HEAD_DIM, HEAD_DIM - 1), F32)], axis=-1)
    m0 = jnp.broadcast_to(state_m[:, 0].reshape(Bd, N_UNITS, 1), (Bd, N_UNITS, LANES))
    x1s, h2s, cbs = _mixer(x_sample, mod, lambda b: 1 + b, wts, P=GRID_W, state=(caug0, m0))

    gf = norm_final_g.reshape(1, -1)
    tm = 1024
    yp = _moe(x1p.reshape(B * S, D_MODEL), h2p.reshape(B * S, D_MODEL), cbp.reshape(B * S, LANES),
              mod, lambda i: 0, w_e_gate[0], w_e_up[0], w_e_down[0], gf, tm)
    ys = _moe(x1s.reshape(Bd * Sd, D_MODEL), h2s.reshape(Bd * Sd, D_MODEL), cbs.reshape(Bd * Sd, LANES),
              mod, lambda i: 1 + (i * tm) // Sd, w_e_gate[0], w_e_up[0], w_e_down[0], gf, tm)

    return (yp.reshape(B, S, D_MODEL), ys.reshape(Bd, Sd, D_MODEL),
            c_new.reshape(B, 1, 2, N_HEADS, HEAD_DIM, HEAD_DIM),
            n_new.reshape(B, 1, 2, N_HEADS, HEAD_DIM),
            m_new[:, :, 0].reshape(B, 1, 2, N_HEADS))
```

```python
import functools

import jax
import jax.numpy as jnp
from jax import lax
from jax.experimental import pallas as pl
from jax.experimental.pallas import tpu as pltpu

D_MODEL = 1024
D_CONV = 512
CONV_K = 31
D_MLSTM = 512
N_HEADS = 4
HEAD_DIM = D_MLSTM // N_HEADS
N_GROUPS = 4
EXPERTS_PER_GROUP = 4
N_EXPERTS = N_GROUPS * EXPERTS_PER_GROUP
D_EXPERT = 256
N_ADA = 6
EPS = 1e-6
GRID_W = 64

LANES = 128
SUB = 256
CONV_PAD = 16
CONV_RB = 64
N_UNITS = 2 * N_HEADS
ROW_ALIGN = 16
SORT_ROWS = SUB + N_GROUPS * ROW_ALIGN
MOE_TM = 512
ROUTE_GROUP_LANE = N_EXPERTS
ROUTE_RANK_LANE = N_EXPERTS + 1
VMEM_LIMIT = 58 * 1024 * 1024

BF16 = jnp.bfloat16
F32 = jnp.float32
NT_DIMS = (((1,), (1,)), ((), ()))


def _dot(a, b):
    return jnp.dot(a, b, preferred_element_type=F32)


def _dot_nt(a, b, precision=None):
    return lax.dot_general(a, b, NT_DIMS, preferred_element_type=F32, precision=precision)


def _sigmoid(x):
    return 0.5 * jnp.tanh(0.5 * x) + 0.5


def _log_sigmoid(x):
    return jnp.minimum(x, 0.0) - jnp.log1p(jnp.exp(-jnp.abs(x)))


def _split3(x):
    hi = x.astype(BF16).astype(F32)
    r1 = x - hi
    mid = r1.astype(BF16).astype(F32)
    lo = (r1 - mid).astype(BF16).astype(F32)
    return hi, mid, lo


def _ada_kernel(c_ref, w_ref, b_ref, o_ref):
    c = c_ref[...]
    s = (c * _sigmoid(c)).astype(BF16)
    o_ref[...] = _dot(s, w_ref[...].astype(BF16)) + b_ref[...]


def _ada(cin, w_ada, b_ada):
    n = w_ada.shape[1]
    bn = 1024
    return pl.pallas_call(
        _ada_kernel,
        grid=(n // bn,),
        in_specs=[
            pl.BlockSpec((8, D_MODEL), lambda j: (0, 0)),
            pl.BlockSpec((D_MODEL, bn), lambda j: (0, j)),
            pl.BlockSpec((1, bn), lambda j: (0, j)),
        ],
        out_specs=pl.BlockSpec((8, bn), lambda j: (0, j)),
        out_shape=jax.ShapeDtypeStruct((8, n), F32),
        compiler_params=pltpu.CompilerParams(dimension_semantics=("arbitrary",)),
        name="ada",
    )(cin, w_ada, b_ada)


_MIXER_WEIGHTS = (
    "g1", "wag", "bag", "wq", "bq", "wkT", "bk", "wv", "bv", "wog", "bog",
    "wgif", "bgif", "wgifT", "bgifT", "wgm", "bgm", "wdw", "bdw", "lng", "lnb",
    "wco", "hng", "wmo", "wo", "g2", "wrt2", "brtT",
)


def _conv_block(upad_s, seg, base, cs, wdw_ref, bdw_ref):
    sub = 8
    first = CONV_PAD - CONV_K // 2
    acc = jnp.broadcast_to(bdw_ref[0:1, cs], (CONV_RB, LANES))
    for r in range(sub):
        z = None
        for a in range((CONV_K + first + sub - 1) // sub):
            j = sub * a + r - first
            if 0 <= j < CONV_K:
                lo = base + sub * a
                term = wdw_ref[j:j + 1, cs] * upad_s[seg, lo:lo + CONV_RB + sub, cs]
                z = term if z is None else z + term
        acc = acc + z[r:r + CONV_RB, :]
    return acc


def _mixer_kernel(T, P, has_state, emit_state, *refs):
    nsub = T // SUB
    nseg = SUB // P
    L = SUB
    it = iter(refs)
    x_ref = next(it)
    mod_ref = next(it)
    if has_state:
        c0_ref = next(it)
        m0_ref = next(it)
    w = {name: next(it) for name in _MIXER_WEIGHTS}
    x1_ref = next(it)
    h2_ref = next(it)
    comb_ref = next(it)
    route_ref = next(it)
    cnt_ref = next(it)
    if emit_state:
        cout_ref = next(it)
        nout_ref = next(it)
        mout_ref = next(it)
    (q_s, kT_s, v_s, so_s, gcol_s, grow_s, ma_s, sgb_s, hm_s, cst_s, upad_s) = [next(it) for _ in range(11)]

    def mod_row(i):
        return mod_ref[0, i:i + 1, :]

    zpad = jnp.zeros((CONV_PAD, D_CONV), F32)
    for seg in range(nseg):
        upad_s[seg, 0:CONV_PAD, :] = zpad
        upad_s[seg, CONV_PAD + P:CONV_PAD + P + CONV_PAD, :] = zpad

    def phase1(i, carry):
        r0 = pl.multiple_of(i * SUB, SUB)
        rows = pl.ds(r0, SUB)
        x = x_ref[0, rows, :]
        xn = x * lax.rsqrt(jnp.mean(x * x, axis=-1, keepdims=True) + EPS) * w["g1"][...]
        hb = (xn * (1.0 + mod_row(1)) + mod_row(0)).astype(BF16)

        ag = _dot(hb, w["wag"][...]) + w["bag"][...]
        u = ag[:, :D_CONV] * _sigmoid(ag[:, D_CONV:])
        for seg in range(nseg):
            upad_s[seg, CONV_PAD:CONV_PAD + P, :] = u[seg * P:(seg + 1) * P, :]
        col_blocks = []
        for cb in range(D_CONV // LANES):
            cs = slice(cb * LANES, (cb + 1) * LANES)
            row_blocks = []
            for seg in range(nseg):
                for rb in range(P // CONV_RB):
                    row_blocks.append(_conv_block(upad_s, seg, rb * CONV_RB, cs, w["wdw"], w["bdw"]))
            col_blocks.append(jnp.concatenate(row_blocks, axis=0))
        cu = jnp.concatenate(col_blocks, axis=1)
        mu = jnp.mean(cu, axis=-1, keepdims=True)
        cc = cu - mu
        cn = cc * lax.rsqrt(jnp.mean(cc * cc, axis=-1, keepdims=True) + EPS) * w["lng"][...] + w["lnb"][...]
        ca = (cn * _sigmoid(cn)).astype(BF16)
        br_a = _dot(ca, w["wco"][...])

        gm = _dot(hb, w["wgm"][...]) + w["bgm"][...]
        ma_s[rows, :] = _sigmoid(gm[:, :D_MODEL]) * br_a
        sgb_s[rows, :] = _sigmoid(gm[:, D_MODEL:])

        q_s[rows, :] = ((_dot(hb, w["wq"][...]) + w["bq"][...]) * (HEAD_DIM ** -0.5)).astype(BF16)
        v_s[rows, :] = (_dot(hb, w["wv"][...]) + w["bv"][...]).astype(BF16)
        so_s[rows, :] = _sigmoid(_dot(hb, w["wog"][...]) + w["bog"][...])
        kT_s[i] = (_dot_nt(w["wkT"][...], hb) + w["bk"][...]).astype(BF16)
        gcol_s[rows, :] = _dot(hb, w["wgif"][...]) + w["bgif"][...]
        grow_s[i] = _dot_nt(w["wgifT"][...], hb) + w["bgifT"][...]
        return carry

    if nsub == 1:
        phase1(0, 0)
    else:
        lax.fori_loop(0, nsub, phase1, 0)

    t_idx = lax.broadcasted_iota(jnp.int32, (L, L), 0)
    s_idx = lax.broadcasted_iota(jnp.int32, (L, L), 1)
    lower = s_idx <= t_idx
    upper = s_idx >= t_idx
    tril_b = lower.astype(F32).astype(BF16)
    triu_b = upper.astype(F32).astype(BF16)
    ones_col = (lax.broadcasted_iota(jnp.int32, (L, HEAD_DIM), 1) == 0).astype(F32).astype(BF16)

    def gate_sums(c):
        rows = slice(c * L, (c + 1) * L)
        gcol = gcol_s[rows, :]
        grow = grow_s[c]
        lf_col = _log_sigmoid(gcol)
        lf_row = _log_sigmoid(grow)
        parts = jnp.concatenate(_split3(lf_col), axis=1).astype(BF16)
        pc = _dot(tril_b, parts)
        pre_col = pc[:, :LANES] + pc[:, LANES:2 * LANES] + pc[:, 2 * LANES:]
        parts = jnp.concatenate(_split3(lf_row), axis=0).astype(BF16)
        pr = _dot(parts, triu_b)
        pre_row = pr[0:16] + pr[16:32] + pr[32:48]
        tot_col = pre_col[L - 1:L, :]
        tot_row = pre_row[:, L - 1:L]
        suf_col = tot_col - pre_col + lf_col
        suf_row = tot_row - pre_row + lf_row
        return grow, (pre_col, suf_col), (pre_row, suf_row), tot_row

    qk_cache = {}

    def unit(d, hd, c, gates, m, first_chunk, want_state):
        grow, cum_col, cum_row, tot_row = gates
        rows = slice(c * L, (c + 1) * L)
        hs = slice(hd * HEAD_DIM, (hd + 1) * HEAD_DIM)
        idx = d * N_HEADS + hd
        r_i = d * 2 * N_HEADS + hd
        r_f = r_i + N_HEADS
        bcol = cum_col[d][:, r_f:r_f + 1]
        brow = cum_row[d][r_f:r_f + 1, :]
        irow = grow[r_i:r_i + 1, :]
        btot = tot_row[r_f:r_f + 1, :]
        qc = q_s[rows, hs]
        kTc = kT_s[c, hs, :]
        vaug = jnp.concatenate([v_s[rows, hs], ones_col], axis=1)
        if nsub == 1 and (hd, c) in qk_cache:
            qk = qk_cache[(hd, c)]
        else:
            qk = _dot(qc, kTc)
            qk_cache[(hd, c)] = qk
        dmat = jnp.where(lower if d == 0 else upper, bcol + (irow - brow), -jnp.inf)
        m_inter = bcol + m
        m_t = jnp.maximum(m_inter, jnp.max(dmat, axis=1, keepdims=True))
        s_mat = (qk * jnp.exp(dmat - m_t)).astype(BF16)
        nd = _dot(s_mat, vaug)
        if has_state or not first_chunk:
            w_inter = jnp.exp(m_inter - m_t)
            nd = nd + w_inter * _dot(qc, cst_s[idx].astype(BF16))
        den = nd[:, HEAD_DIM:HEAD_DIM + 1]
        h = nd[:, :HEAD_DIM] * (1.0 / jnp.maximum(jnp.abs(den), jnp.exp(-m_t)))
        if d == 0:
            hm_s[rows, hs] = h
        else:
            hm_s[rows, hs] = hm_s[rows, hs] + h
        if not want_state:
            return m
        g = btot - brow + irow
        m_new = jnp.maximum(btot + m, jnp.max(g, axis=1, keepdims=True))
        wk = jnp.exp(g - m_new)
        kw = (kTc.astype(F32) * wk).astype(BF16)
        upd = _dot(kw, vaug)
        if has_state or not first_chunk:
            upd = upd + jnp.exp(btot + m - m_new) * cst_s[idx]
        cst_s[idx] = upd
        return m_new

    if has_state:
        for idx in range(N_UNITS):
            cst_s[idx] = c0_ref[0, idx]
        m_state = [m0_ref[0, idx:idx + 1, 0:1] for idx in range(N_UNITS)]
    else:
        m_state = [jnp.zeros((1, 1), F32) for _ in range(N_UNITS)]

    for d in range(2):
        order = list(range(nsub)) if d == 0 else list(range(nsub - 1, -1, -1))
        for pos, c in enumerate(order):
            gates = gate_sums(c)
            last = pos == nsub - 1
            for hd in range(N_HEADS):
                idx = d * N_HEADS + hd
                m_state[idx] = unit(d, hd, c, gates, m_state[idx], pos == 0, emit_state or not last)

    if emit_state:
        for idx in range(N_UNITS):
            caug = cst_s[idx]
            cout_ref[0, idx] = caug[:, :HEAD_DIM]
            nout_ref[0, idx:idx + 1, :] = caug[:, HEAD_DIM:].T[0:1, :]
            mout_ref[0, idx:idx + 1, :] = jnp.broadcast_to(m_state[idx], (1, LANES))

    e_iota = lax.broadcasted_iota(jnp.int32, (LANES, SUB), 0)
    g_of_e = lax.shift_right_logical(e_iota, 2)
    j_of_e = lax.bitwise_and(e_iota, EXPERTS_PER_GROUP - 1)
    r8 = lax.broadcasted_iota(jnp.int32, (8, SUB), 0)
    before_b = (t_idx < s_idx).astype(F32).astype(BF16)

    def phase3(i, carry):
        r0 = pl.multiple_of(i * SUB, SUB)
        rows = pl.ds(r0, SUB)
        hm = hm_s[rows, :]
        heads = []
        for hd in range(N_HEADS):
            hh = hm[:, hd * HEAD_DIM:(hd + 1) * HEAD_DIM]
            heads.append(hh * lax.rsqrt(jnp.mean(hh * hh, axis=-1, keepdims=True) + EPS))
        hn = jnp.concatenate(heads, axis=1) * w["hng"][...]
        hb2 = (so_s[rows, :] * hn).astype(BF16)
        br_b = _dot(hb2, w["wmo"][...])
        mixed = (ma_s[rows, :] + sgb_s[rows, :] * br_b).astype(BF16)
        x1 = x_ref[0, rows, :] + mod_row(2) * _dot(mixed, w["wo"][...])
        x1_ref[0, rows, :] = x1
        xn = x1 * lax.rsqrt(jnp.mean(x1 * x1, axis=-1, keepdims=True) + EPS) * w["g2"][...]
        h2 = xn * (1.0 + mod_row(4)) + mod_row(3)
        h2_ref[0, rows, :] = h2.astype(BF16)

        h2_hi = h2.astype(BF16)
        h2_lo = (h2 - h2_hi.astype(F32)).astype(BF16)
        lg = _dot(h2_hi, w["wrt2"][...])
        lg = lg[:, :LANES] + lg[:, LANES:] + _dot(h2_lo, w["wrt2"][:, :LANES])
        lt = lg.T + w["brtT"][...]
        gl = [lt[N_EXPERTS + g:N_EXPERTS + g + 1, :] for g in range(N_GROUPS)]
        best, gsel = gl[0], jnp.zeros((1, SUB), jnp.int32)
        for g in range(1, N_GROUPS):
            better = gl[g] > best
            gsel = jnp.where(better, g, gsel)
            best = jnp.where(better, gl[g], best)
        gp_sel = 1.0 / sum(jnp.exp(v - best) for v in gl)
        el = []
        for j in range(EXPERTS_PER_GROUP):
            v = lt[j:j + 1, :]
            for g in range(1, N_GROUPS):
                r = g * EXPERTS_PER_GROUP + j
                v = jnp.where(gsel == g, lt[r:r + 1, :], v)
            el.append(v)
        l1, e1 = el[0], jnp.zeros((1, SUB), jnp.int32)
        for j in range(1, EXPERTS_PER_GROUP):
            better = el[j] > l1
            e1 = jnp.where(better, j, e1)
            l1 = jnp.where(better, el[j], l1)
        l2 = jnp.full((1, SUB), -jnp.inf, F32)
        e2 = jnp.zeros((1, SUB), jnp.int32)
        for j in range(EXPERTS_PER_GROUP):
            better = jnp.logical_and(e1 != j, el[j] > l2)
            e2 = jnp.where(better, j, e2)
            l2 = jnp.where(better, el[j], l2)
        r2 = jnp.exp(l2 - l1)
        wt1 = gp_sel / (1.0 + r2)
        wt2 = gp_sel * r2 / (1.0 + r2)
        in_group = g_of_e == gsel
        comb_t = (jnp.where(jnp.logical_and(in_group, j_of_e == e1), wt1, 0.0)
                  + jnp.where(jnp.logical_and(in_group, j_of_e == e2), wt2, 0.0))

        onehot = (r8 == gsel).astype(F32)
        rank = jnp.sum(onehot * _dot(onehot.astype(BF16), before_b), axis=0, keepdims=True)
        gsel_f = gsel.astype(F32)
        r8rows = pl.ds(pl.multiple_of(i * 8, 8), 8)
        route_ref[0, r8rows, :] = jnp.where(r8 == 0, gsel_f, jnp.where(r8 == 1, rank, 0.0))
        cnt_ref[0, r8rows, :] = jnp.broadcast_to(jnp.sum(onehot, axis=1, keepdims=True), (8, LANES))
        comb_t = jnp.where(e_iota == ROUTE_GROUP_LANE, gsel_f,
                           jnp.where(e_iota == ROUTE_RANK_LANE, rank, comb_t))
        comb_ref[0, rows, :] = comb_t.T
        return carry

    if nsub == 1:
        phase3(0, 0)
    else:
        lax.fori_loop(0, nsub, phase3, 0)


def _const_spec(a):
    nd = a.ndim
    return pl.BlockSpec(a.shape, lambda b, _nd=nd: (0,) * _nd, pipeline_mode=pl.Buffered(1))


def _mixer(x, mod, mod_index, weights, P, state=None, emit_state=False):
    B, T, _ = x.shape
    nsub = T // SUB
    has_state = state is not None
    seq_mode = {} if nsub == 1 else {"pipeline_mode": pl.Buffered(1)}
    in_specs = [
        pl.BlockSpec((1, T, D_MODEL), lambda b: (b, 0, 0), **seq_mode),
        pl.BlockSpec((1, N_ADA, D_MODEL), lambda b: (mod_index(b), 0, 0)),
    ]
    args = [x, mod]
    if has_state:
        caug0, m0 = state
        in_specs += [
            pl.BlockSpec((1, N_UNITS, HEAD_DIM, 2 * HEAD_DIM), lambda b: (b, 0, 0, 0)),
            pl.BlockSpec((1, N_UNITS, LANES), lambda b: (b, 0, 0)),
        ]
        args += [caug0, m0]
    for name in _MIXER_WEIGHTS:
        in_specs.append(_const_spec(weights[name]))
        args.append(weights[name])
    out_shape = [
        jax.ShapeDtypeStruct((B, T, D_MODEL), F32),
        jax.ShapeDtypeStruct((B, T, D_MODEL), BF16),
        jax.ShapeDtypeStruct((B, T, LANES), F32),
        jax.ShapeDtypeStruct((B, nsub * 8, SUB), F32),
        jax.ShapeDtypeStruct((B, nsub * 8, LANES), F32),
    ]
    out_specs = [
        pl.BlockSpec((1, T, D_MODEL), lambda b: (b, 0, 0), **seq_mode),
        pl.BlockSpec((1, T, D_MODEL), lambda b: (b, 0, 0), **seq_mode),
        pl.BlockSpec((1, T, LANES), lambda b: (b, 0, 0)),
        pl.BlockSpec((1, nsub * 8, SUB), lambda b: (b, 0, 0)),
        pl.BlockSpec((1, nsub * 8, LANES), lambda b: (b, 0, 0)),
    ]
    if emit_state:
        out_shape += [
            jax.ShapeDtypeStruct((B, N_UNITS, HEAD_DIM, HEAD_DIM), F32),
            jax.ShapeDtypeStruct((B, N_UNITS, HEAD_DIM), F32),
            jax.ShapeDtypeStruct((B, N_UNITS, LANES), F32),
        ]
        out_specs += [
            pl.BlockSpec((1, N_UNITS, HEAD_DIM, HEAD_DIM), lambda b: (b, 0, 0, 0)),
            pl.BlockSpec((1, N_UNITS, HEAD_DIM), lambda b: (b, 0, 0)),
            pl.BlockSpec((1, N_UNITS, LANES), lambda b: (b, 0, 0)),
        ]
    scratch = [
        pltpu.VMEM((T, D_MLSTM), BF16),
        pltpu.VMEM((nsub, D_MLSTM, SUB), BF16),
        pltpu.VMEM((T, D_MLSTM), BF16),
        pltpu.VMEM((T, D_MLSTM), F32),
        pltpu.VMEM((T, LANES), F32),
        pltpu.VMEM((nsub, 16, SUB), F32),
        pltpu.VMEM((T, D_MODEL), F32),
        pltpu.VMEM((T, D_MODEL), F32),
        pltpu.VMEM((T, D_MLSTM), F32),
        pltpu.VMEM((N_UNITS, HEAD_DIM, 2 * HEAD_DIM), F32),
        pltpu.VMEM((SUB // P, P + 2 * CONV_PAD, D_CONV), F32),
    ]
    return pl.pallas_call(
        functools.partial(_mixer_kernel, T, P, has_state, emit_state),
        grid=(B,),
        in_specs=in_specs,
        out_specs=out_specs,
        out_shape=out_shape,
        scratch_shapes=scratch,
        compiler_params=pltpu.CompilerParams(
            dimension_semantics=("arbitrary",), vmem_limit_bytes=VMEM_LIMIT),
        name="mixer_T%d" % T,
    )(*args)


def _dest_in_block(group, rank, starts):
    dest = rank
    for g in range(N_GROUPS):
        dest = dest + jnp.where(group == float(g), starts[g], 0.0)
    return dest


def _copy_segments(src_refs, dst_refs, src_starts, dst_starts, n_pieces):
    for g in range(N_GROUPS):
        def body(k, carry, g=g):
            s = pl.multiple_of(src_starts[g] + k * ROW_ALIGN, ROW_ALIGN)
            d = pl.multiple_of(dst_starts[g] + k * ROW_ALIGN, ROW_ALIGN)
            for src, dst in zip(src_refs, dst_refs):
                dst[pl.ds(d, ROW_ALIGN), :] = src[pl.ds(s, ROW_ALIGN), :]
            return carry
        lax.fori_loop(0, n_pieces[g], body, 0)


def _dispatch_kernel(n_ctx_blocks, start_ref, npiece_ref, off_ref,
                     h2c_ref, h2l_ref, cbc_ref, cbl_ref, rtc_ref, rtl_ref,
                     xs_ref, cs_ref, sx_s, sc_s):
    b = pl.program_id(0)
    is_ctx = b < n_ctx_blocks

    @pl.when(b == 0)
    def _():
        xs_ref[...] = jnp.zeros_like(xs_ref)
        cs_ref[...] = jnp.zeros_like(cs_ref)

    h2 = jnp.where(is_ctx, h2c_ref[0], h2l_ref[0])
    cb = jnp.where(is_ctx, cbc_ref[0], cbl_ref[0])
    rt = jnp.where(is_ctx, rtc_ref[0], rtl_ref[0])
    starts = [start_ref[b * N_GROUPS + g] for g in range(N_GROUPS)]
    dest = _dest_in_block(rt[0:1, :], rt[1:2, :], [s.astype(F32) for s in starts])
    row = lax.broadcasted_iota(jnp.int32, (SORT_ROWS, SUB), 0).astype(F32)
    perm = (row == dest).astype(F32).astype(BF16)
    cb_hi = cb.astype(BF16)
    cb_lo = (cb - cb_hi.astype(F32)).astype(BF16)
    sx_s[...] = _dot(perm, h2).astype(BF16)
    sc_s[...] = _dot(perm, jnp.concatenate([cb_hi, cb_lo], axis=1)).astype(BF16)
    _copy_segments((sx_s, sc_s), (xs_ref, cs_ref), starts,
                   [off_ref[b * N_GROUPS + g] for g in range(N_GROUPS)],
                   [npiece_ref[b * N_GROUPS + g] for g in range(N_GROUPS)])


def _experts_kernel(tgroup_ref, tvalid_ref, tfirst_ref, xs_ref, cs_ref, wg_ref, wu_ref, wd_ref, ys_ref,
                    wg_s, wu_s, wd_s):
    i = pl.program_id(0)

    @pl.when(tfirst_ref[i] == 1)
    def _():
        for j in range(EXPERTS_PER_GROUP):
            cols = slice(j * D_EXPERT, (j + 1) * D_EXPERT)
            wg_s[:, cols] = wg_ref[j].astype(BF16)
            wu_s[:, cols] = wu_ref[j].astype(BF16)
            wd_s[cols, :] = wd_ref[j].astype(BF16)

    @pl.when(tvalid_ref[i] == 1)
    def _():
        x = xs_ref[...]
        g = _dot(x, wg_s[...])
        u = _dot(x, wu_s[...])
        comb = cs_ref[:, :LANES].astype(F32) + cs_ref[:, LANES:].astype(F32)
        lane = lax.broadcasted_iota(jnp.int32, comb.shape, 1)
        first = tgroup_ref[i] * EXPERTS_PER_GROUP
        parts = []
        for j in range(EXPERTS_PER_GROUP):
            cols = slice(j * D_EXPERT, (j + 1) * D_EXPERT)
            cw = jnp.sum(jnp.where(lane == first + j, comb, 0.0), axis=1, keepdims=True)
            gj = g[:, cols]
            parts.append((gj * _sigmoid(gj) * u[:, cols] * cw).astype(BF16))
        ys_ref[...] = _dot(jnp.concatenate(parts, axis=1), wd_s[...]).astype(BF16)

    @pl.when(tvalid_ref[i] == 0)
    def _():
        ys_ref[...] = jnp.zeros_like(ys_ref)


def _combine_kernel(n_ctx_blocks, blocks_per_lat_seq, start_ref, npiece_ref, off_ref,
                    x1c_ref, x1l_ref, cbc_ref, cbl_ref, ys_ref, mod_ref, gf_ref, yc_ref, yl_ref, loc_s):
    b = pl.program_id(0)
    is_ctx = b < n_ctx_blocks
    starts = [start_ref[b * N_GROUPS + g] for g in range(N_GROUPS)]
    loc_s[...] = jnp.zeros_like(loc_s)
    _copy_segments((ys_ref,), (loc_s,), [off_ref[b * N_GROUPS + g] for g in range(N_GROUPS)], starts,
                   [npiece_ref[b * N_GROUPS + g] for g in range(N_GROUPS)])
    cb = jnp.where(is_ctx, cbc_ref[0], cbl_ref[0])
    dest = _dest_in_block(cb[:, ROUTE_GROUP_LANE:ROUTE_GROUP_LANE + 1],
                          cb[:, ROUTE_RANK_LANE:ROUTE_RANK_LANE + 1],
                          [s.astype(F32) for s in starts])
    col = lax.broadcasted_iota(jnp.int32, (SUB, SORT_ROWS), 1).astype(F32)
    unperm = (col == dest).astype(F32).astype(BF16)
    moe = _dot(unperm, loc_s[...])
    x1 = jnp.where(is_ctx, x1c_ref[0], x1l_ref[0])
    mrow = jnp.where(is_ctx, 0, 1 + jnp.maximum(b - n_ctx_blocks, 0) // blocks_per_lat_seq)
    x2 = x1 + mod_ref[mrow, N_ADA - 1:N_ADA, :] * moe
    y = x2 * lax.rsqrt(jnp.mean(x2 * x2, axis=-1, keepdims=True) + EPS) * gf_ref[...]

    @pl.when(is_ctx)
    def _():
        yc_ref[0] = y

    @pl.when(jnp.logical_not(is_ctx))
    def _():
        yl_ref[0] = y


def _moe_plan(cnt, n_tiles):
    cnt_al = (cnt + ROW_ALIGN - 1) // ROW_ALIGN * ROW_ALIGN
    start = jnp.cumsum(cnt_al, axis=1) - cnt_al
    gpad = (jnp.sum(cnt_al, axis=0) + MOE_TM - 1) // MOE_TM * MOE_TM
    gbase = jnp.cumsum(gpad) - gpad
    off = gbase[None, :] + jnp.cumsum(cnt_al, axis=0) - cnt_al
    tile_end = jnp.cumsum(gpad // MOE_TM)
    t = jnp.arange(n_tiles, dtype=jnp.int32)
    tgroup = jnp.sum((t[:, None] >= tile_end[None, :]).astype(jnp.int32), axis=1)
    valid = t < tile_end[-1]
    last_group = jnp.sum((tile_end[-1] - 1 >= tile_end).astype(jnp.int32))
    tgroup = jnp.where(valid, tgroup, last_group)
    first = jnp.logical_and(valid, jnp.concatenate([jnp.ones((1,), bool), tgroup[1:] != tgroup[:-1]]))
    i32 = lambda a: a.astype(jnp.int32).reshape(-1)
    return i32(start), i32(cnt_al // ROW_ALIGN), i32(off), i32(tgroup), i32(valid), i32(first)


def _moe(x1c, x1l, h2c, h2l, cbc, cbl, rtc, rtl, cnt, mod, blocks_per_lat_seq, wg, wu, wd, gf):
    nc, nl = x1c.shape[0], x1l.shape[0]
    nb = nc + nl
    n_rows_max = nb * SUB + nb * N_GROUPS * (ROW_ALIGN - 1) + N_GROUPS * (MOE_TM - ROW_ALIGN)
    n_tiles = -(-n_rows_max // MOE_TM)
    ns = n_tiles * MOE_TM
    start, npiece, off, tgroup, tvalid, tfirst = _moe_plan(cnt, n_tiles)

    cmap = lambda b, *_: (jnp.minimum(b, nc - 1), 0, 0)
    lmap = lambda b, *_: (jnp.maximum(b - nc, 0), 0, 0)
    whole = lambda *_: (0, 0)
    once = {"pipeline_mode": pl.Buffered(1)}
    arb = pltpu.CompilerParams(dimension_semantics=("arbitrary",), vmem_limit_bytes=VMEM_LIMIT)

    xs, cs = pl.pallas_call(
        functools.partial(_dispatch_kernel, nc),
        grid_spec=pltpu.PrefetchScalarGridSpec(
            num_scalar_prefetch=3, grid=(nb,),
            in_specs=[
                pl.BlockSpec((1, SUB, D_MODEL), cmap), pl.BlockSpec((1, SUB, D_MODEL), lmap),
                pl.BlockSpec((1, SUB, LANES), cmap), pl.BlockSpec((1, SUB, LANES), lmap),
                pl.BlockSpec((1, 8, SUB), cmap), pl.BlockSpec((1, 8, SUB), lmap),
            ],
            out_specs=[pl.BlockSpec((ns, D_MODEL), whole, **once), pl.BlockSpec((ns, 2 * LANES), whole, **once)],
            scratch_shapes=[pltpu.VMEM((SORT_ROWS, D_MODEL), BF16), pltpu.VMEM((SORT_ROWS, 2 * LANES), BF16)],
        ),
        out_shape=[jax.ShapeDtypeStruct((ns, D_MODEL), BF16), jax.ShapeDtypeStruct((ns, 2 * LANES), BF16)],
        compiler_params=arb,
        name="moe_dispatch",
    )(start, npiece, off, h2c, h2l, cbc, cbl, rtc, rtl)

    wmap = lambda i, tg, tv, tf: (tg[i], 0, 0)
    ys = pl.pallas_call(
        _experts_kernel,
        grid_spec=pltpu.PrefetchScalarGridSpec(
            num_scalar_prefetch=3, grid=(n_tiles,),
            in_specs=[
                pl.BlockSpec((MOE_TM, D_MODEL), lambda i, *_: (i, 0)),
                pl.BlockSpec((MOE_TM, 2 * LANES), lambda i, *_: (i, 0)),
                pl.BlockSpec((EXPERTS_PER_GROUP, D_MODEL, D_EXPERT), wmap),
                pl.BlockSpec((EXPERTS_PER_GROUP, D_MODEL, D_EXPERT), wmap),
                pl.BlockSpec((EXPERTS_PER_GROUP, D_EXPERT, D_MODEL), wmap),
            ],
            out_specs=pl.BlockSpec((MOE_TM, D_MODEL), lambda i, *_: (i, 0)),
            scratch_shapes=[pltpu.VMEM((D_MODEL, EXPERTS_PER_GROUP * D_EXPERT), BF16),
                            pltpu.VMEM((D_MODEL, EXPERTS_PER_GROUP * D_EXPERT), BF16),
                            pltpu.VMEM((EXPERTS_PER_GROUP * D_EXPERT, D_MODEL), BF16)],
        ),
        out_shape=jax.ShapeDtypeStruct((ns, D_MODEL), BF16),
        compiler_params=arb,
        name="moe_experts",
    )(tgroup, tvalid, tfirst, xs, cs, wg, wu, wd)

    yc, yl = pl.pallas_call(
        functools.partial(_combine_kernel, nc, blocks_per_lat_seq),
        grid_spec=pltpu.PrefetchScalarGridSpec(
            num_scalar_prefetch=3, grid=(nb,),
            in_specs=[
                pl.BlockSpec((1, SUB, D_MODEL), cmap), pl.BlockSpec((1, SUB, D_MODEL), lmap),
                pl.BlockSpec((1, SUB, LANES), cmap), pl.BlockSpec((1, SUB, LANES), lmap),
                pl.BlockSpec((ns, D_MODEL), whole, **once),
                pl.BlockSpec(mod.shape, lambda *_: (0, 0, 0)),
                pl.BlockSpec((1, D_MODEL), whole),
            ],
            out_specs=[pl.BlockSpec((1, SUB, D_MODEL), cmap), pl.BlockSpec((1, SUB, D_MODEL), lmap)],
            scratch_shapes=[pltpu.VMEM((SORT_ROWS, D_MODEL), BF16)],
        ),
        out_shape=[jax.ShapeDtypeStruct((nc, SUB, D_MODEL), F32), jax.ShapeDtypeStruct((nl, SUB, D_MODEL), F32)],
        compiler_params=arb,
        name="moe_combine",
    )(start, npiece, off, x1c, x1l, cbc, cbl, ys, mod, gf)
    return yc, yl


def _prep_weights(norm1_g, w_in, b_in, b_gates, w_dw, b_dw, conv_ln_g, conv_ln_b, w_conv_out,
                  mlstm_hn_g, w_mlstm_out, w_o, norm2_g, w_rg, b_rg, w_re, b_re):
    s_a = 2 * D_CONV
    s_q = s_a + D_MLSTM
    s_k = s_q + D_MLSTM
    s_v = s_k + D_MLSTM
    s_o = s_v + D_MLSTM
    s_g = s_o + 4 * N_HEADS
    row = lambda v: v.reshape(1, -1).astype(F32)
    wb = w_in.astype(BF16)
    ngate = 4 * N_HEADS
    wgif = jnp.zeros((D_MODEL, LANES), BF16).at[:, :ngate].set(wb[:, s_o:s_g])
    bg = b_in[s_o:s_g] + b_gates.reshape(-1)
    bgif = jnp.zeros((1, LANES), F32).at[0, :ngate].set(bg)
    wrt = jnp.concatenate([w_re, w_rg], axis=1)
    wrt = jnp.zeros((D_MODEL, LANES), F32).at[:, :N_EXPERTS + N_GROUPS].set(wrt)
    wrt_hi = wrt.astype(BF16)
    wrt2 = jnp.concatenate([wrt_hi, (wrt - wrt_hi.astype(F32)).astype(BF16)], axis=1)
    brtT = jnp.zeros((LANES, 1), F32).at[:N_EXPERTS + N_GROUPS, 0].set(jnp.concatenate([b_re, b_rg]))
    return {
        "g1": row(norm1_g),
        "wag": wb[:, :s_a], "bag": row(b_in[:s_a]),
        "wq": wb[:, s_a:s_q], "bq": row(b_in[s_a:s_q]),
        "wkT": wb[:, s_q:s_k].T, "bk": b_in[s_q:s_k].reshape(-1, 1),
        "wv": wb[:, s_k:s_v], "bv": row(b_in[s_k:s_v]),
        "wog": wb[:, s_v:s_o], "bog": row(b_in[s_v:s_o]),
        "wgif": wgif, "bgif": bgif,
        "wgifT": wb[:, s_o:s_g].T, "bgifT": bg.reshape(-1, 1),
        "wgm": wb[:, s_g:], "bgm": row(b_in[s_g:]),
        "wdw": w_dw.astype(F32), "bdw": row(b_dw), "lng": row(conv_ln_g), "lnb": row(conv_ln_b),
        "wco": w_conv_out.astype(BF16), "hng": row(mlstm_hn_g), "wmo": w_mlstm_out.astype(BF16),
        "wo": w_o.astype(BF16), "g2": row(norm2_g), "wrt2": wrt2, "brtT": brtT,
    }


def kernel(x_prompt, x_sample, state_C, state_n, state_m, c, c_ctx, norm1_g, w_ada, b_ada, w_in, b_in, b_gates, w_dw, b_dw, conv_ln_g, conv_ln_b, w_conv_out, mlstm_hn_g, w_mlstm_out, w_o, norm2_g, w_rg, b_rg, w_re, b_re, w_e_gate, w_e_up, w_e_down, norm_final_g):
    B, S, _ = x_prompt.shape
    Bd, Sd, _ = x_sample.shape
    assert w_ada.shape[0] == 1, "single trunk layer"
    assert S == SUB and Sd % SUB == 0

    cin = jnp.zeros((8, D_MODEL), F32).at[0].set(c_ctx).at[1:1 + Bd].set(c)
    mod = _ada(cin, w_ada[0], b_ada[0].reshape(1, -1)).reshape(8, N_ADA, D_MODEL)

    wts = _prep_weights(norm1_g[0], w_in[0], b_in[0], b_gates[0], w_dw[0], b_dw[0], conv_ln_g[0],
                        conv_ln_b[0], w_conv_out[0], mlstm_hn_g[0], w_mlstm_out[0], w_o[0],
                        norm2_g[0], w_rg[0], b_rg[0], w_re[0], b_re[0])

    x1p, h2p, cbp, rtp, cntp, c_new, n_new, m_new = _mixer(
        x_prompt, mod, lambda b: 0, wts, P=S, emit_state=True)

    sc = state_C[:, 0].reshape(Bd, N_UNITS, HEAD_DIM, HEAD_DIM)
    sn = state_n[:, 0].reshape(Bd, N_UNITS, HEAD_DIM, 1)
    caug0 = jnp.concatenate([sc, sn, jnp.zeros((Bd, N_UNITS, HEAD_DIM, HEAD_DIM - 1), F32)], axis=-1)
    m0 = jnp.broadcast_to(state_m[:, 0].reshape(Bd, N_UNITS, 1), (Bd, N_UNITS, LANES))
    x1s, h2s, cbs, rts, cnts = _mixer(x_sample, mod, lambda b: 1 + b, wts, P=GRID_W, state=(caug0, m0))

    nl = Bd * Sd // SUB
    blk = lambda a: a.reshape(nl, SUB, a.shape[-1])
    cnt = jnp.concatenate([cntp[:, :N_GROUPS, 0], cnts.reshape(nl, 8, LANES)[:, :N_GROUPS, 0]], axis=0)
    yp, ys = _moe(x1p, blk(x1s), h2p, blk(h2s), cbp, blk(cbs), rtp, rts.reshape(nl, 8, SUB),
                  cnt.astype(jnp.int32), mod, Sd // SUB, w_e_gate[0], w_e_up[0], w_e_down[0],
                  norm_final_g.reshape(1, -1))

    return (yp.reshape(B, S, D_MODEL), ys.reshape(Bd, Sd, D_MODEL),
            c_new.reshape(B, 1, 2, N_HEADS, HEAD_DIM, HEAD_DIM),
            n_new.reshape(B, 1, 2, N_HEADS, HEAD_DIM),
            m_new[:, :, 0].reshape(B, 1, 2, N_HEADS))
```

```python
import functools

import jax
import jax.numpy as jnp
from jax import lax
from jax.experimental import pallas as pl
from jax.experimental.pallas import tpu as pltpu

D_MODEL = 1024
D_CONV = 512
CONV_K = 31
D_MLSTM = 512
N_HEADS = 4
HEAD_DIM = D_MLSTM // N_HEADS
N_GROUPS = 4
EXPERTS_PER_GROUP = 4
N_EXPERTS = N_GROUPS * EXPERTS_PER_GROUP
D_EXPERT = 256
N_ADA = 6
EPS = 1e-6
GRID_W = 64

LANES = 128
SUB = 256
CONV_PAD = 16
CONV_RB = 64
N_UNITS = 2 * N_HEADS
ROW_ALIGN = 16
SORT_ROWS = SUB + N_GROUPS * ROW_ALIGN
MOE_TM = 512
ROUTE_GROUP_LANE = N_EXPERTS
ROUTE_RANK_LANE = N_EXPERTS + 1
VMEM_LIMIT = 58 * 1024 * 1024

BF16 = jnp.bfloat16
F32 = jnp.float32
NT_DIMS = (((1,), (1,)), ((), ()))


def _dot(a, b):
    return jnp.dot(a, b, preferred_element_type=F32)


def _dot_nt(a, b, precision=None):
    return lax.dot_general(a, b, NT_DIMS, preferred_element_type=F32, precision=precision)


def _sigmoid(x):
    return 0.5 * jnp.tanh(0.5 * x) + 0.5


def _log_sigmoid(x):
    return jnp.minimum(x, 0.0) - jnp.log1p(jnp.exp(-jnp.abs(x)))


def _split3(x):
    hi = x.astype(BF16).astype(F32)
    r1 = x - hi
    mid = r1.astype(BF16).astype(F32)
    lo = (r1 - mid).astype(BF16).astype(F32)
    return hi, mid, lo


def _ada_kernel(c_ref, w_ref, b_ref, o_ref):
    c = c_ref[...]
    s = (c * _sigmoid(c)).astype(BF16)
    o_ref[...] = _dot(s, w_ref[...].astype(BF16)) + b_ref[...]


def _ada(cin, w_ada, b_ada):
    n = w_ada.shape[1]
    bn = 1024
    return pl.pallas_call(
        _ada_kernel,
        grid=(n // bn,),
        in_specs=[
            pl.BlockSpec((8, D_MODEL), lambda j: (0, 0)),
            pl.BlockSpec((D_MODEL, bn), lambda j: (0, j)),
            pl.BlockSpec((1, bn), lambda j: (0, j)),
        ],
        out_specs=pl.BlockSpec((8, bn), lambda j: (0, j)),
        out_shape=jax.ShapeDtypeStruct((8, n), F32),
        compiler_params=pltpu.CompilerParams(dimension_semantics=("arbitrary",)),
        name="ada",
    )(cin, w_ada, b_ada)


_MIXER_WEIGHTS = (
    "g1", "wag", "bag", "wq", "bq", "wkT", "bk", "wv", "bv", "wog", "bog",
    "wgifT", "bgifT", "wgm", "bgm", "wdw", "bdw", "lng", "lnb",
    "wco", "hng", "wmo", "wo", "g2", "wrt2", "brtT",
)


def _conv_block(upad_s, seg, base, cs, wdw_ref, bdw_ref):
    sub = 8
    first = CONV_PAD - CONV_K // 2
    acc = jnp.broadcast_to(bdw_ref[0:1, cs], (CONV_RB, LANES))
    for r in range(sub):
        z = None
        for a in range((CONV_K + first + sub - 1) // sub):
            j = sub * a + r - first
            if 0 <= j < CONV_K:
                lo = base + sub * a
                term = wdw_ref[j:j + 1, cs] * upad_s[seg, lo:lo + CONV_RB + sub, cs]
                z = term if z is None else z + term
        acc = acc + z[r:r + CONV_RB, :]
    return acc


def _mixer_kernel(T, P, has_state, emit_state, *refs):
    nsub = T // SUB
    nseg = SUB // P
    L = SUB
    it = iter(refs)
    x_ref = next(it)
    mod_ref = next(it)
    if has_state:
        c0_ref = next(it)
        m0_ref = next(it)
    w = {name: next(it) for name in _MIXER_WEIGHTS}
    x1_ref = next(it)
    h2_ref = next(it)
    comb_ref = next(it)
    route_ref = next(it)
    cnt_ref = next(it)
    if emit_state:
        cout_ref = next(it)
        nout_ref = next(it)
        mout_ref = next(it)
    (q_s, kT_s, v_s, so_s, grow_s, ma_s, sgb_s, hm_s, cst_s, upad_s) = [next(it) for _ in range(10)]

    def mod_row(i):
        return mod_ref[0, i:i + 1, :]

    zpad = jnp.zeros((CONV_PAD, D_CONV), F32)
    for seg in range(nseg):
        upad_s[seg, 0:CONV_PAD, :] = zpad
        upad_s[seg, CONV_PAD + P:CONV_PAD + P + CONV_PAD, :] = zpad

    def phase1(i, carry):
        r0 = pl.multiple_of(i * SUB, SUB)
        rows = pl.ds(r0, SUB)
        x = x_ref[0, rows, :]
        xn = x * lax.rsqrt(jnp.mean(x * x, axis=-1, keepdims=True) + EPS) * w["g1"][...]
        hb = (xn * (1.0 + mod_row(1)) + mod_row(0)).astype(BF16)

        ag = _dot(hb, w["wag"][...]) + w["bag"][...]
        u = ag[:, :D_CONV] * _sigmoid(ag[:, D_CONV:])
        for seg in range(nseg):
            upad_s[seg, CONV_PAD:CONV_PAD + P, :] = u[seg * P:(seg + 1) * P, :]
        col_blocks = []
        for cb in range(D_CONV // LANES):
            cs = slice(cb * LANES, (cb + 1) * LANES)
            row_blocks = []
            for seg in range(nseg):
                for rb in range(P // CONV_RB):
                    row_blocks.append(_conv_block(upad_s, seg, rb * CONV_RB, cs, w["wdw"], w["bdw"]))
            col_blocks.append(jnp.concatenate(row_blocks, axis=0))
        cu = jnp.concatenate(col_blocks, axis=1)
        mu = jnp.mean(cu, axis=-1, keepdims=True)
        cc = cu - mu
        cn = cc * lax.rsqrt(jnp.mean(cc * cc, axis=-1, keepdims=True) + EPS) * w["lng"][...] + w["lnb"][...]
        ca = (cn * _sigmoid(cn)).astype(BF16)
        br_a = _dot(ca, w["wco"][...])

        gm = _dot(hb, w["wgm"][...]) + w["bgm"][...]
        ma_s[rows, :] = _sigmoid(gm[:, :D_MODEL]) * br_a
        sgb_s[rows, :] = _sigmoid(gm[:, D_MODEL:])

        q_s[rows, :] = ((_dot(hb, w["wq"][...]) + w["bq"][...]) * (HEAD_DIM ** -0.5)).astype(BF16)
        v_s[rows, :] = (_dot(hb, w["wv"][...]) + w["bv"][...]).astype(BF16)
        so_s[rows, :] = _sigmoid(_dot(hb, w["wog"][...]) + w["bog"][...])
        kT_s[i] = (_dot_nt(w["wkT"][...], hb) + w["bk"][...]).astype(BF16)
        grow_s[i] = _dot_nt(w["wgifT"][...], hb) + w["bgifT"][...]
        return carry

    if nsub == 1:
        phase1(0, 0)
    else:
        lax.fori_loop(0, nsub, phase1, 0)

    t_idx = lax.broadcasted_iota(jnp.int32, (L, L), 0)
    s_idx = lax.broadcasted_iota(jnp.int32, (L, L), 1)
    lower = s_idx <= t_idx
    upper = s_idx >= t_idx
    triu_b = upper.astype(F32).astype(BF16)
    ones_col = (lax.broadcasted_iota(jnp.int32, (L, HEAD_DIM), 1) == 0).astype(F32).astype(BF16)
    lane_u = lax.broadcasted_iota(jnp.int32, (N_UNITS, L), 1)
    is_bwd = lax.broadcasted_iota(jnp.int32, (N_UNITS, L), 0) >= N_HEADS
    pad_rows = jnp.zeros((LANES - 3 * N_UNITS, L), F32)

    def gate_prep(c, m_vec):
        g = grow_s[c]
        gi, lf = g[:N_UNITS], _log_sigmoid(g[N_UNITS:])
        pr = _dot(jnp.concatenate(_split3(lf), axis=0).astype(BF16), triu_b)
        pre = pr[0:N_UNITS] + pr[N_UNITS:2 * N_UNITS] + pr[2 * N_UNITS:]
        tot = pre[:, L - 1:L]
        bsum = jnp.where(is_bwd, tot - pre + lf, pre)
        a = gi - bsum
        pm, sm, k = a, a, 1
        while k < L:
            pm = jnp.where(lane_u >= k, jnp.maximum(pm, pltpu.roll(pm, k, axis=1)), pm)
            sm = jnp.where(lane_u < L - k, jnp.maximum(sm, pltpu.roll(sm, L - k, axis=1)), sm)
            k *= 2
        big_m = jnp.maximum(m_vec, jnp.where(is_bwd, sm, pm))
        m_end = jnp.maximum(m_vec, jnp.max(a, axis=1, keepdims=True))
        cols = jnp.concatenate(
            [big_m, jnp.exp(m_vec - big_m), jnp.exp(-bsum - big_m), pad_rows], axis=0).T
        return a, cols, jnp.exp(a - m_end), jnp.exp(m_vec - m_end), tot + m_end

    qk_cache = {}

    def unit(d, hd, c, prep, first_chunk, want_state):
        a, cols, wk, decay, _ = prep
        rows = slice(c * L, (c + 1) * L)
        hs = slice(hd * HEAD_DIM, (hd + 1) * HEAD_DIM)
        idx = d * N_HEADS + hd
        col = lambda k: cols[:, k * N_UNITS + idx:k * N_UNITS + idx + 1]
        qc = q_s[rows, hs]
        kTc = kT_s[c, hs, :]
        vaug = jnp.concatenate([v_s[rows, hs], ones_col], axis=1)
        if nsub == 1 and (hd, c) in qk_cache:
            qk = qk_cache[(hd, c)]
        else:
            qk = _dot(qc, kTc)
            qk_cache[(hd, c)] = qk
        w_intra = jnp.where(lower if d == 0 else upper, jnp.exp(a[idx:idx + 1, :] - col(0)), 0.0)
        nd = _dot((qk * w_intra).astype(BF16), vaug)
        if has_state or not first_chunk:
            nd = nd + col(1) * _dot(qc, cst_s[idx].astype(BF16))
        den = nd[:, HEAD_DIM:HEAD_DIM + 1]
        h = nd[:, :HEAD_DIM] * (1.0 / jnp.maximum(jnp.abs(den), col(2)))
        if d == 0:
            hm_s[rows, hs] = h
        else:
            hm_s[rows, hs] = hm_s[rows, hs] + h
        if want_state:
            kw = (kTc.astype(F32) * wk[idx:idx + 1, :]).astype(BF16)
            upd = _dot(kw, vaug)
            if has_state or not first_chunk:
                upd = upd + decay[idx:idx + 1, :] * cst_s[idx]
            cst_s[idx] = upd

    if has_state:
        for idx in range(N_UNITS):
            cst_s[idx] = c0_ref[0, idx]
        m_vec = m0_ref[0, :, 0:1]
    else:
        m_vec = jnp.zeros((N_UNITS, 1), F32)

    dir_rows = lax.broadcasted_iota(jnp.int32, (N_UNITS, 1), 0) >= N_HEADS
    prep = None
    for d in range(2):
        order = list(range(nsub)) if d == 0 else list(range(nsub - 1, -1, -1))
        for pos, c in enumerate(order):
            if nsub > 1 or prep is None:
                prep = gate_prep(c, m_vec)
            for hd in range(N_HEADS):
                unit(d, hd, c, prep, pos == 0, emit_state or pos < nsub - 1)
            m_vec = jnp.where(dir_rows == (d == 1), prep[4], m_vec)

    if emit_state:
        for idx in range(N_UNITS):
            caug = cst_s[idx]
            cout_ref[0, idx] = caug[:, :HEAD_DIM]
            nout_ref[0, idx:idx + 1, :] = caug[:, HEAD_DIM:].T[0:1, :]
        mout_ref[0] = jnp.broadcast_to(m_vec, (N_UNITS, LANES))

    e_iota = lax.broadcasted_iota(jnp.int32, (LANES, SUB), 0)
    g_of_e = lax.shift_right_logical(e_iota, 2)
    j_of_e = lax.bitwise_and(e_iota, EXPERTS_PER_GROUP - 1)
    r8 = lax.broadcasted_iota(jnp.int32, (8, SUB), 0)
    before_b = (t_idx < s_idx).astype(F32).astype(BF16)

    def phase3(i, carry):
        r0 = pl.multiple_of(i * SUB, SUB)
        rows = pl.ds(r0, SUB)
        hm = hm_s[rows, :]
        heads = []
        for hd in range(N_HEADS):
            hh = hm[:, hd * HEAD_DIM:(hd + 1) * HEAD_DIM]
            heads.append(hh * lax.rsqrt(jnp.mean(hh * hh, axis=-1, keepdims=True) + EPS))
        hn = jnp.concatenate(heads, axis=1) * w["hng"][...]
        hb2 = (so_s[rows, :] * hn).astype(BF16)
        br_b = _dot(hb2, w["wmo"][...])
        mixed = (ma_s[rows, :] + sgb_s[rows, :] * br_b).astype(BF16)
        x1 = x_ref[0, rows, :] + mod_row(2) * _dot(mixed, w["wo"][...])
        x1_ref[0, rows, :] = x1
        xn = x1 * lax.rsqrt(jnp.mean(x1 * x1, axis=-1, keepdims=True) + EPS) * w["g2"][...]
        h2 = xn * (1.0 + mod_row(4)) + mod_row(3)
        h2_ref[0, rows, :] = h2.astype(BF16)

        h2_hi = h2.astype(BF16)
        h2_lo = (h2 - h2_hi.astype(F32)).astype(BF16)
        lg = _dot(h2_hi, w["wrt2"][...])
        lg = lg[:, :LANES] + lg[:, LANES:] + _dot(h2_lo, w["wrt2"][:, :LANES])
        lt = lg.T + w["brtT"][...]
        gl = [lt[N_EXPERTS + g:N_EXPERTS + g + 1, :] for g in range(N_GROUPS)]
        best, gsel = gl[0], jnp.zeros((1, SUB), jnp.int32)
        for g in range(1, N_GROUPS):
            better = gl[g] > best
            gsel = jnp.where(better, g, gsel)
            best = jnp.where(better, gl[g], best)
        gp_sel = 1.0 / sum(jnp.exp(v - best) for v in gl)
        el = []
        for j in range(EXPERTS_PER_GROUP):
            v = lt[j:j + 1, :]
            for g in range(1, N_GROUPS):
                r = g * EXPERTS_PER_GROUP + j
                v = jnp.where(gsel == g, lt[r:r + 1, :], v)
            el.append(v)
        l1, e1 = el[0], jnp.zeros((1, SUB), jnp.int32)
        for j in range(1, EXPERTS_PER_GROUP):
            better = el[j] > l1
            e1 = jnp.where(better, j, e1)
            l1 = jnp.where(better, el[j], l1)
        l2 = jnp.full((1, SUB), -jnp.inf, F32)
        e2 = jnp.zeros((1, SUB), jnp.int32)
        for j in range(EXPERTS_PER_GROUP):
            better = jnp.logical_and(e1 != j, el[j] > l2)
            e2 = jnp.where(better, j, e2)
            l2 = jnp.where(better, el[j], l2)
        r2 = jnp.exp(l2 - l1)
        wt1 = gp_sel / (1.0 + r2)
        wt2 = gp_sel * r2 / (1.0 + r2)
        in_group = g_of_e == gsel
        comb_t = (jnp.where(jnp.logical_and(in_group, j_of_e == e1), wt1, 0.0)
                  + jnp.where(jnp.logical_and(in_group, j_of_e == e2), wt2, 0.0))

        onehot = (r8 == gsel).astype(F32)
        rank = jnp.sum(onehot * _dot(onehot.astype(BF16), before_b), axis=0, keepdims=True)
        gsel_f = gsel.astype(F32)
        r8rows = pl.ds(pl.multiple_of(i * 8, 8), 8)
        route_ref[0, r8rows, :] = jnp.where(r8 == 0, gsel_f, jnp.where(r8 == 1, rank, 0.0))
        cnt_ref[0, r8rows, :] = jnp.broadcast_to(jnp.sum(onehot, axis=1, keepdims=True), (8, LANES))
        comb_t = jnp.where(e_iota == ROUTE_GROUP_LANE, gsel_f,
                           jnp.where(e_iota == ROUTE_RANK_LANE, rank, comb_t))
        comb_ref[0, rows, :] = comb_t.T
        return carry

    if nsub == 1:
        phase3(0, 0)
    else:
        lax.fori_loop(0, nsub, phase3, 0)


def _const_spec(a):
    nd = a.ndim
    return pl.BlockSpec(a.shape, lambda b, _nd=nd: (0,) * _nd, pipeline_mode=pl.Buffered(1))


def _mixer(x, mod, mod_index, weights, P, state=None, emit_state=False):
    B, T, _ = x.shape
    nsub = T // SUB
    has_state = state is not None
    seq_mode = {} if nsub == 1 else {"pipeline_mode": pl.Buffered(1)}
    in_specs = [
        pl.BlockSpec((1, T, D_MODEL), lambda b: (b, 0, 0), **seq_mode),
        pl.BlockSpec((1, N_ADA, D_MODEL), lambda b: (mod_index(b), 0, 0)),
    ]
    args = [x, mod]
    if has_state:
        caug0, m0 = state
        in_specs += [
            pl.BlockSpec((1, N_UNITS, HEAD_DIM, 2 * HEAD_DIM), lambda b: (b, 0, 0, 0)),
            pl.BlockSpec((1, N_UNITS, LANES), lambda b: (b, 0, 0)),
        ]
        args += [caug0, m0]
    for name in _MIXER_WEIGHTS:
        in_specs.append(_const_spec(weights[name]))
        args.append(weights[name])
    out_shape = [
        jax.ShapeDtypeStruct((B, T, D_MODEL), F32),
        jax.ShapeDtypeStruct((B, T, D_MODEL), BF16),
        jax.ShapeDtypeStruct((B, T, LANES), F32),
        jax.ShapeDtypeStruct((B, nsub * 8, SUB), F32),
        jax.ShapeDtypeStruct((B, nsub * 8, LANES), F32),
    ]
    out_specs = [
        pl.BlockSpec((1, T, D_MODEL), lambda b: (b, 0, 0), **seq_mode),
        pl.BlockSpec((1, T, D_MODEL), lambda b: (b, 0, 0), **seq_mode),
        pl.BlockSpec((1, T, LANES), lambda b: (b, 0, 0)),
        pl.BlockSpec((1, nsub * 8, SUB), lambda b: (b, 0, 0)),
        pl.BlockSpec((1, nsub * 8, LANES), lambda b: (b, 0, 0)),
    ]
    if emit_state:
        out_shape += [
            jax.ShapeDtypeStruct((B, N_UNITS, HEAD_DIM, HEAD_DIM), F32),
            jax.ShapeDtypeStruct((B, N_UNITS, HEAD_DIM), F32),
            jax.ShapeDtypeStruct((B, N_UNITS, LANES), F32),
        ]
        out_specs += [
            pl.BlockSpec((1, N_UNITS, HEAD_DIM, HEAD_DIM), lambda b: (b, 0, 0, 0)),
            pl.BlockSpec((1, N_UNITS, HEAD_DIM), lambda b: (b, 0, 0)),
            pl.BlockSpec((1, N_UNITS, LANES), lambda b: (b, 0, 0)),
        ]
    scratch = [
        pltpu.VMEM((T, D_MLSTM), BF16),
        pltpu.VMEM((nsub, D_MLSTM, SUB), BF16),
        pltpu.VMEM((T, D_MLSTM), BF16),
        pltpu.VMEM((T, D_MLSTM), F32),
        pltpu.VMEM((nsub, 2 * N_UNITS, SUB), F32),
        pltpu.VMEM((T, D_MODEL), F32),
        pltpu.VMEM((T, D_MODEL), F32),
        pltpu.VMEM((T, D_MLSTM), F32),
        pltpu.VMEM((N_UNITS, HEAD_DIM, 2 * HEAD_DIM), F32),
        pltpu.VMEM((SUB // P, P + 2 * CONV_PAD, D_CONV), F32),
    ]
    return pl.pallas_call(
        functools.partial(_mixer_kernel, T, P, has_state, emit_state),
        grid=(B,),
        in_specs=in_specs,
        out_specs=out_specs,
        out_shape=out_shape,
        scratch_shapes=scratch,
        compiler_params=pltpu.CompilerParams(
            dimension_semantics=("arbitrary",), vmem_limit_bytes=VMEM_LIMIT),
        name="mixer_T%d" % T,
    )(*args)


def _dest_in_block(group, rank, starts):
    dest = rank
    for g in range(N_GROUPS):
        dest = dest + jnp.where(group == float(g), starts[g], 0.0)
    return dest


def _copy_segments(src_refs, dst_refs, src_starts, dst_starts, n_pieces):
    for g in range(N_GROUPS):
        def body(k, carry, g=g):
            s = pl.multiple_of(src_starts[g] + k * ROW_ALIGN, ROW_ALIGN)
            d = pl.multiple_of(dst_starts[g] + k * ROW_ALIGN, ROW_ALIGN)
            for src, dst in zip(src_refs, dst_refs):
                dst[pl.ds(d, ROW_ALIGN), :] = src[pl.ds(s, ROW_ALIGN), :]
            return carry
        lax.fori_loop(0, n_pieces[g], body, 0)


def _dispatch_kernel(n_ctx_blocks, start_ref, npiece_ref, off_ref,
                     h2c_ref, h2l_ref, cbc_ref, cbl_ref, rtc_ref, rtl_ref,
                     xs_ref, cs_ref, sx_s, sc_s):
    b = pl.program_id(0)
    is_ctx = b < n_ctx_blocks

    @pl.when(b == 0)
    def _():
        xs_ref[...] = jnp.zeros_like(xs_ref)
        cs_ref[...] = jnp.zeros_like(cs_ref)

    h2 = jnp.where(is_ctx, h2c_ref[0], h2l_ref[0])
    cb = jnp.where(is_ctx, cbc_ref[0], cbl_ref[0])
    rt = jnp.where(is_ctx, rtc_ref[0], rtl_ref[0])
    starts = [start_ref[b * N_GROUPS + g] for g in range(N_GROUPS)]
    dest = _dest_in_block(rt[0:1, :], rt[1:2, :], [s.astype(F32) for s in starts])
    row = lax.broadcasted_iota(jnp.int32, (SORT_ROWS, SUB), 0).astype(F32)
    perm = (row == dest).astype(F32).astype(BF16)
    cb_hi = cb.astype(BF16)
    cb_lo = (cb - cb_hi.astype(F32)).astype(BF16)
    sx_s[...] = _dot(perm, h2).astype(BF16)
    sc_s[...] = _dot(perm, jnp.concatenate([cb_hi, cb_lo], axis=1)).astype(BF16)
    _copy_segments((sx_s, sc_s), (xs_ref, cs_ref), starts,
                   [off_ref[b * N_GROUPS + g] for g in range(N_GROUPS)],
                   [npiece_ref[b * N_GROUPS + g] for g in range(N_GROUPS)])


def _experts_kernel(tgroup_ref, tvalid_ref, tfirst_ref, xs_ref, cs_ref, wg_ref, wu_ref, wd_ref, ys_ref,
                    wg_s, wu_s, wd_s):
    i = pl.program_id(0)

    @pl.when(tfirst_ref[i] == 1)
    def _():
        for j in range(EXPERTS_PER_GROUP):
            cols = slice(j * D_EXPERT, (j + 1) * D_EXPERT)
            wg_s[:, cols] = wg_ref[j].astype(BF16)
            wu_s[:, cols] = wu_ref[j].astype(BF16)
            wd_s[cols, :] = wd_ref[j].astype(BF16)

    @pl.when(tvalid_ref[i] == 1)
    def _():
        x = xs_ref[...]
        g = _dot(x, wg_s[...])
        u = _dot(x, wu_s[...])
        comb = cs_ref[:, :LANES].astype(F32) + cs_ref[:, LANES:].astype(F32)
        lane = lax.broadcasted_iota(jnp.int32, comb.shape, 1)
        first = tgroup_ref[i] * EXPERTS_PER_GROUP
        parts = []
        for j in range(EXPERTS_PER_GROUP):
            cols = slice(j * D_EXPERT, (j + 1) * D_EXPERT)
            cw = jnp.sum(jnp.where(lane == first + j, comb, 0.0), axis=1, keepdims=True)
            gj = g[:, cols]
            parts.append((gj * _sigmoid(gj) * u[:, cols] * cw).astype(BF16))
        ys_ref[...] = _dot(jnp.concatenate(parts, axis=1), wd_s[...]).astype(BF16)

    @pl.when(tvalid_ref[i] == 0)
    def _():
        ys_ref[...] = jnp.zeros_like(ys_ref)


def _combine_kernel(n_ctx_blocks, blocks_per_lat_seq, start_ref, npiece_ref, off_ref,
                    x1c_ref, x1l_ref, cbc_ref, cbl_ref, ys_ref, mod_ref, gf_ref, yc_ref, yl_ref, loc_s):
    b = pl.program_id(0)
    is_ctx = b < n_ctx_blocks
    starts = [start_ref[b * N_GROUPS + g] for g in range(N_GROUPS)]
    loc_s[...] = jnp.zeros_like(loc_s)
    _copy_segments((ys_ref,), (loc_s,), [off_ref[b * N_GROUPS + g] for g in range(N_GROUPS)], starts,
                   [npiece_ref[b * N_GROUPS + g] for g in range(N_GROUPS)])
    cb = jnp.where(is_ctx, cbc_ref[0], cbl_ref[0])
    dest = _dest_in_block(cb[:, ROUTE_GROUP_LANE:ROUTE_GROUP_LANE + 1],
                          cb[:, ROUTE_RANK_LANE:ROUTE_RANK_LANE + 1],
                          [s.astype(F32) for s in starts])
    col = lax.broadcasted_iota(jnp.int32, (SUB, SORT_ROWS), 1).astype(F32)
    unperm = (col == dest).astype(F32).astype(BF16)
    moe = _dot(unperm, loc_s[...])
    x1 = jnp.where(is_ctx, x1c_ref[0], x1l_ref[0])
    mrow = jnp.where(is_ctx, 0, 1 + jnp.maximum(b - n_ctx_blocks, 0) // blocks_per_lat_seq)
    x2 = x1 + mod_ref[mrow, N_ADA - 1:N_ADA, :] * moe
    y = x2 * lax.rsqrt(jnp.mean(x2 * x2, axis=-1, keepdims=True) + EPS) * gf_ref[...]

    @pl.when(is_ctx)
    def _():
        yc_ref[0] = y

    @pl.when(jnp.logical_not(is_ctx))
    def _():
        yl_ref[0] = y


def _moe_plan(cnt, n_tiles):
    cnt_al = (cnt + ROW_ALIGN - 1) // ROW_ALIGN * ROW_ALIGN
    start = jnp.cumsum(cnt_al, axis=1) - cnt_al
    gpad = (jnp.sum(cnt_al, axis=0) + MOE_TM - 1) // MOE_TM * MOE_TM
    gbase = jnp.cumsum(gpad) - gpad
    off = gbase[None, :] + jnp.cumsum(cnt_al, axis=0) - cnt_al
    tile_end = jnp.cumsum(gpad // MOE_TM)
    t = jnp.arange(n_tiles, dtype=jnp.int32)
    tgroup = jnp.sum((t[:, None] >= tile_end[None, :]).astype(jnp.int32), axis=1)
    valid = t < tile_end[-1]
    last_group = jnp.sum((tile_end[-1] - 1 >= tile_end).astype(jnp.int32))
    tgroup = jnp.where(valid, tgroup, last_group)
    first = jnp.logical_and(valid, jnp.concatenate([jnp.ones((1,), bool), tgroup[1:] != tgroup[:-1]]))
    i32 = lambda a: a.astype(jnp.int32).reshape(-1)
    return i32(start), i32(cnt_al // ROW_ALIGN), i32(off), i32(tgroup), i32(valid), i32(first)


def _moe(x1c, x1l, h2c, h2l, cbc, cbl, rtc, rtl, cnt, mod, blocks_per_lat_seq, wg, wu, wd, gf):
    nc, nl = x1c.shape[0], x1l.shape[0]
    nb = nc + nl
    n_rows_max = nb * SUB + nb * N_GROUPS * (ROW_ALIGN - 1) + N_GROUPS * (MOE_TM - ROW_ALIGN)
    n_tiles = -(-n_rows_max // MOE_TM)
    ns = n_tiles * MOE_TM
    start, npiece, off, tgroup, tvalid, tfirst = _moe_plan(cnt, n_tiles)

    cmap = lambda b, *_: (jnp.minimum(b, nc - 1), 0, 0)
    lmap = lambda b, *_: (jnp.maximum(b - nc, 0), 0, 0)
    whole = lambda *_: (0, 0)
    once = {"pipeline_mode": pl.Buffered(1)}
    arb = pltpu.CompilerParams(dimension_semantics=("arbitrary",), vmem_limit_bytes=VMEM_LIMIT)

    xs, cs = pl.pallas_call(
        functools.partial(_dispatch_kernel, nc),
        grid_spec=pltpu.PrefetchScalarGridSpec(
            num_scalar_prefetch=3, grid=(nb,),
            in_specs=[
                pl.BlockSpec((1, SUB, D_MODEL), cmap), pl.BlockSpec((1, SUB, D_MODEL), lmap),
                pl.BlockSpec((1, SUB, LANES), cmap), pl.BlockSpec((1, SUB, LANES), lmap),
                pl.BlockSpec((1, 8, SUB), cmap), pl.BlockSpec((1, 8, SUB), lmap),
            ],
            out_specs=[pl.BlockSpec((ns, D_MODEL), whole, **once), pl.BlockSpec((ns, 2 * LANES), whole, **once)],
            scratch_shapes=[pltpu.VMEM((SORT_ROWS, D_MODEL), BF16), pltpu.VMEM((SORT_ROWS, 2 * LANES), BF16)],
        ),
        out_shape=[jax.ShapeDtypeStruct((ns, D_MODEL), BF16), jax.ShapeDtypeStruct((ns, 2 * LANES), BF16)],
        compiler_params=arb,
        name="moe_dispatch",
    )(start, npiece, off, h2c, h2l, cbc, cbl, rtc, rtl)

    wmap = lambda i, tg, tv, tf: (tg[i], 0, 0)
    ys = pl.pallas_call(
        _experts_kernel,
        grid_spec=pltpu.PrefetchScalarGridSpec(
            num_scalar_prefetch=3, grid=(n_tiles,),
            in_specs=[
                pl.BlockSpec((MOE_TM, D_MODEL), lambda i, *_: (i, 0)),
                pl.BlockSpec((MOE_TM, 2 * LANES), lambda i, *_: (i, 0)),
                pl.BlockSpec((EXPERTS_PER_GROUP, D_MODEL, D_EXPERT), wmap),
                pl.BlockSpec((EXPERTS_PER_GROUP, D_MODEL, D_EXPERT), wmap),
                pl.BlockSpec((EXPERTS_PER_GROUP, D_EXPERT, D_MODEL), wmap),
            ],
            out_specs=pl.BlockSpec((MOE_TM, D_MODEL), lambda i, *_: (i, 0)),
            scratch_shapes=[pltpu.VMEM((D_MODEL, EXPERTS_PER_GROUP * D_EXPERT), BF16),
                            pltpu.VMEM((D_MODEL, EXPERTS_PER_GROUP * D_EXPERT), BF16),
                            pltpu.VMEM((EXPERTS_PER_GROUP * D_EXPERT, D_MODEL), BF16)],
        ),
        out_shape=jax.ShapeDtypeStruct((ns, D_MODEL), BF16),
        compiler_params=arb,
        name="moe_experts",
    )(tgroup, tvalid, tfirst, xs, cs, wg, wu, wd)

    yc, yl = pl.pallas_call(
        functools.partial(_combine_kernel, nc, blocks_per_lat_seq),
        grid_spec=pltpu.PrefetchScalarGridSpec(
            num_scalar_prefetch=3, grid=(nb,),
            in_specs=[
                pl.BlockSpec((1, SUB, D_MODEL), cmap), pl.BlockSpec((1, SUB, D_MODEL), lmap),
                pl.BlockSpec((1, SUB, LANES), cmap), pl.BlockSpec((1, SUB, LANES), lmap),
                pl.BlockSpec((ns, D_MODEL), whole, **once),
                pl.BlockSpec(mod.shape, lambda *_: (0, 0, 0)),
                pl.BlockSpec((1, D_MODEL), whole),
            ],
            out_specs=[pl.BlockSpec((1, SUB, D_MODEL), cmap), pl.BlockSpec((1, SUB, D_MODEL), lmap)],
            scratch_shapes=[pltpu.VMEM((SORT_ROWS, D_MODEL), BF16)],
        ),
        out_shape=[jax.ShapeDtypeStruct((nc, SUB, D_MODEL), F32), jax.ShapeDtypeStruct((nl, SUB, D_MODEL), F32)],
        compiler_params=arb,
        name="moe_combine",
    )(start, npiece, off, x1c, x1l, cbc, cbl, ys, mod, gf)
    return yc, yl


def _prep_weights(norm1_g, w_in, b_in, b_gates, w_dw, b_dw, conv_ln_g, conv_ln_b, w_conv_out,
                  mlstm_hn_g, w_mlstm_out, w_o, norm2_g, w_rg, b_rg, w_re, b_re):
    s_a = 2 * D_CONV
    s_q = s_a + D_MLSTM
    s_k = s_q + D_MLSTM
    s_v = s_k + D_MLSTM
    s_o = s_v + D_MLSTM
    s_g = s_o + 4 * N_HEADS
    row = lambda v: v.reshape(1, -1).astype(F32)
    wb = w_in.astype(BF16)
    gperm = [d * 2 * N_HEADS + gate * N_HEADS + hd
             for gate in range(2) for d in range(2) for hd in range(N_HEADS)]
    gperm = jnp.array(gperm, jnp.int32)
    bg = (b_in[s_o:s_g] + b_gates.reshape(-1))[gperm]
    wrt = jnp.concatenate([w_re, w_rg], axis=1)
    wrt = jnp.zeros((D_MODEL, LANES), F32).at[:, :N_EXPERTS + N_GROUPS].set(wrt)
    wrt_hi = wrt.astype(BF16)
    wrt2 = jnp.concatenate([wrt_hi, (wrt - wrt_hi.astype(F32)).astype(BF16)], axis=1)
    brtT = jnp.zeros((LANES, 1), F32).at[:N_EXPERTS + N_GROUPS, 0].set(jnp.concatenate([b_re, b_rg]))
    return {
        "g1": row(norm1_g),
        "wag": wb[:, :s_a], "bag": row(b_in[:s_a]),
        "wq": wb[:, s_a:s_q], "bq": row(b_in[s_a:s_q]),
        "wkT": wb[:, s_q:s_k].T, "bk": b_in[s_q:s_k].reshape(-1, 1),
        "wv": wb[:, s_k:s_v], "bv": row(b_in[s_k:s_v]),
        "wog": wb[:, s_v:s_o], "bog": row(b_in[s_v:s_o]),
        "wgifT": wb[:, s_o:s_g].T[gperm], "bgifT": bg.reshape(-1, 1),
        "wgm": wb[:, s_g:], "bgm": row(b_in[s_g:]),
        "wdw": w_dw.astype(F32), "bdw": row(b_dw), "lng": row(conv_ln_g), "lnb": row(conv_ln_b),
        "wco": w_conv_out.astype(BF16), "hng": row(mlstm_hn_g), "wmo": w_mlstm_out.astype(BF16),
        "wo": w_o.astype(BF16), "g2": row(norm2_g), "wrt2": wrt2, "brtT": brtT,
    }


def kernel(x_prompt, x_sample, state_C, state_n, state_m, c, c_ctx, norm1_g, w_ada, b_ada, w_in, b_in, b_gates, w_dw, b_dw, conv_ln_g, conv_ln_b, w_conv_out, mlstm_hn_g, w_mlstm_out, w_o, norm2_g, w_rg, b_rg, w_re, b_re, w_e_gate, w_e_up, w_e_down, norm_final_g):
    B, S, _ = x_prompt.shape
    Bd, Sd, _ = x_sample.shape
    assert w_ada.shape[0] == 1, "single trunk layer"
    assert S == SUB and Sd % SUB == 0

    cin = jnp.zeros((8, D_MODEL), F32).at[0].set(c_ctx).at[1:1 + Bd].set(c)
    mod = _ada(cin, w_ada[0], b_ada[0].reshape(1, -1)).reshape(8, N_ADA, D_MODEL)

    wts = _prep_weights(norm1_g[0], w_in[0], b_in[0], b_gates[0], w_dw[0], b_dw[0], conv_ln_g[0],
                        conv_ln_b[0], w_conv_out[0], mlstm_hn_g[0], w_mlstm_out[0], w_o[0],
                        norm2_g[0], w_rg[0], b_rg[0], w_re[0], b_re[0])

    x1p, h2p, cbp, rtp, cntp, c_new, n_new, m_new = _mixer(
        x_prompt, mod, lambda b: 0, wts, P=S, emit_state=True)

    sc = state_C[:, 0].reshape(Bd, N_UNITS, HEAD_DIM, HEAD_DIM)
    sn = state_n[:, 0].reshape(Bd, N_UNITS, HEAD_DIM, 1)
    caug0 = jnp.concatenate([sc, sn, jnp.zeros((Bd, N_UNITS, HEAD_DIM, HEAD_DIM - 1), F32)], axis=-1)
    m0 = jnp.broadcast_to(state_m[:, 0].reshape(Bd, N_UNITS, 1), (Bd, N_UNITS, LANES))
    x1s, h2s, cbs, rts, cnts = _mixer(x_sample, mod, lambda b: 1 + b, wts, P=GRID_W, state=(caug0, m0))

    nl = Bd * Sd // SUB
    blk = lambda a: a.reshape(nl, SUB, a.shape[-1])
    cnt = jnp.concatenate([cntp[:, :N_GROUPS, 0], cnts.reshape(nl, 8, LANES)[:, :N_GROUPS, 0]], axis=0)
    yp, ys = _moe(x1p, blk(x1s), h2p, blk(h2s), cbp, blk(cbs), rtp, rts.reshape(nl, 8, SUB),
                  cnt.astype(jnp.int32), mod, Sd // SUB, w_e_gate[0], w_e_up[0], w_e_down[0],
                  norm_final_g.reshape(1, -1))

    return (yp.reshape(B, S, D_MODEL), ys.reshape(Bd, Sd, D_MODEL),
            c_new.reshape(B, 1, 2, N_HEADS, HEAD_DIM, HEAD_DIM),
            n_new.reshape(B, 1, 2, N_HEADS, HEAD_DIM),
            m_new[:, :, 0].reshape(B, 1, 2, N_HEADS))
```

```python
import functools

import jax
import jax.numpy as jnp
from jax import lax
from jax.experimental import pallas as pl
from jax.experimental.pallas import tpu as pltpu

D_MODEL = 1024
D_CONV = 512
CONV_K = 31
D_MLSTM = 512
N_HEADS = 4
HEAD_DIM = D_MLSTM // N_HEADS
N_GROUPS = 4
EXPERTS_PER_GROUP = 4
N_EXPERTS = N_GROUPS * EXPERTS_PER_GROUP
D_EXPERT = 256
N_ADA = 6
EPS = 1e-6
GRID_W = 64

LANES = 128
SUB = 256
CONV_PAD = 16
CONV_RB = 64
N_UNITS = 2 * N_HEADS
ROW_ALIGN = 16
SORT_ROWS = SUB + N_GROUPS * ROW_ALIGN
MOE_TM = 512
ROUTE_GROUP_LANE = N_EXPERTS
ROUTE_RANK_LANE = N_EXPERTS + 1
VMEM_LIMIT = 58 * 1024 * 1024

BF16 = jnp.bfloat16
F32 = jnp.float32
NT_DIMS = (((1,), (1,)), ((), ()))


def _dot(a, b):
    return jnp.dot(a, b, preferred_element_type=F32)


def _dot_nt(a, b, precision=None):
    return lax.dot_general(a, b, NT_DIMS, preferred_element_type=F32, precision=precision)


def _sigmoid(x):
    return 0.5 * jnp.tanh(0.5 * x) + 0.5


def _log_sigmoid(x):
    return jnp.minimum(x, 0.0) - jnp.log1p(jnp.exp(-jnp.abs(x)))


def _split3(x):
    hi = x.astype(BF16).astype(F32)
    r1 = x - hi
    mid = r1.astype(BF16).astype(F32)
    lo = (r1 - mid).astype(BF16).astype(F32)
    return hi, mid, lo


def _ada_kernel(c_ref, w_ref, b_ref, o_ref):
    c = c_ref[...]
    s = (c * _sigmoid(c)).astype(BF16)
    o_ref[...] = _dot(s, w_ref[...].astype(BF16)) + b_ref[...]


def _ada(cin, w_ada, b_ada):
    n = w_ada.shape[1]
    bn = 1024
    return pl.pallas_call(
        _ada_kernel,
        grid=(n // bn,),
        in_specs=[
            pl.BlockSpec((8, D_MODEL), lambda j: (0, 0)),
            pl.BlockSpec((D_MODEL, bn), lambda j: (0, j)),
            pl.BlockSpec((1, bn), lambda j: (0, j)),
        ],
        out_specs=pl.BlockSpec((8, bn), lambda j: (0, j)),
        out_shape=jax.ShapeDtypeStruct((8, n), F32),
        compiler_params=pltpu.CompilerParams(dimension_semantics=("arbitrary",)),
        name="ada",
    )(cin, w_ada, b_ada)


_MIXER_WEIGHTS = (
    "g1", "wag", "bag", "wq", "bq", "wkT", "bk", "wv", "bv", "wog", "bog",
    "wgifT", "bgifT", "wgm", "bgm", "wdw", "bdw", "lng", "lnb",
    "wco", "hng", "wmo", "wo", "g2", "wrt2", "brtT",
)


def _conv_block(upad_s, seg, base, cs, wdw_ref, bdw_ref):
    sub = 8
    first = CONV_PAD - CONV_K // 2
    acc = jnp.broadcast_to(bdw_ref[0:1, cs], (CONV_RB, LANES))
    for r in range(sub):
        z = None
        for a in range((CONV_K + first + sub - 1) // sub):
            j = sub * a + r - first
            if 0 <= j < CONV_K:
                lo = base + sub * a
                term = wdw_ref[j:j + 1, cs] * upad_s[seg, lo:lo + CONV_RB + sub, cs]
                z = term if z is None else z + term
        acc = acc + z[r:r + CONV_RB, :]
    return acc


def _mixer_kernel(T, P, has_state, emit_state, *refs):
    nsub = T // SUB
    nseg = SUB // P
    L = SUB
    it = iter(refs)
    x_ref = next(it)
    mod_ref = next(it)
    if has_state:
        c0_ref = next(it)
        m0_ref = next(it)
    w = {name: next(it) for name in _MIXER_WEIGHTS}
    x1_ref = next(it)
    h2_ref = next(it)
    comb_ref = next(it)
    route_ref = next(it)
    cnt_ref = next(it)
    if emit_state:
        cout_ref = next(it)
        nout_ref = next(it)
        mout_ref = next(it)
    (q_s, kT_s, v_s, so_s, scan_s, ma_s, sgb_s, hm_s, cst_s, upad_s) = [next(it) for _ in range(10)]

    def mod_row(i):
        return mod_ref[0, i:i + 1, :]

    zpad = jnp.zeros((CONV_PAD, D_CONV), F32)
    for seg in range(nseg):
        upad_s[seg, 0:CONV_PAD, :] = zpad
        upad_s[seg, CONV_PAD + P:CONV_PAD + P + CONV_PAD, :] = zpad

    t_idx = lax.broadcasted_iota(jnp.int32, (L, L), 0)
    s_idx = lax.broadcasted_iota(jnp.int32, (L, L), 1)
    lower = s_idx <= t_idx
    upper = s_idx >= t_idx
    triu_b = upper.astype(F32).astype(BF16)
    lane_u = lax.broadcasted_iota(jnp.int32, (N_UNITS, L), 1)
    is_bwd = lax.broadcasted_iota(jnp.int32, (N_UNITS, L), 0) >= N_HEADS

    def gate_scan(g):
        gi, lf = g[:N_UNITS], _log_sigmoid(g[N_UNITS:])
        pr = _dot(jnp.concatenate(_split3(lf), axis=0).astype(BF16), triu_b)
        pre = pr[0:N_UNITS] + pr[N_UNITS:2 * N_UNITS] + pr[2 * N_UNITS:]
        tot = pre[:, L - 1:L]
        bsum = jnp.where(is_bwd, tot - pre + lf, pre)
        a = gi - bsum
        pm, sm, k = a, a, 1
        while k < L:
            pm = jnp.where(lane_u >= k, jnp.maximum(pm, pltpu.roll(pm, k, axis=1)), pm)
            sm = jnp.where(lane_u < L - k, jnp.maximum(sm, pltpu.roll(sm, L - k, axis=1)), sm)
            k *= 2
        wide = lambda v: jnp.broadcast_to(v, (N_UNITS, L))
        return jnp.concatenate([a, jnp.where(is_bwd, sm, pm), bsum, wide(tot),
                                wide(jnp.max(a, axis=1, keepdims=True))], axis=0)

    def phase1(i, carry):
        r0 = pl.multiple_of(i * SUB, SUB)
        rows = pl.ds(r0, SUB)
        x = x_ref[0, rows, :]
        xn = x * lax.rsqrt(jnp.mean(x * x, axis=-1, keepdims=True) + EPS) * w["g1"][...]
        hb = (xn * (1.0 + mod_row(1)) + mod_row(0)).astype(BF16)

        scan_s[i] = gate_scan(_dot_nt(w["wgifT"][...], hb) + w["bgifT"][...])
        ag = _dot(hb, w["wag"][...]) + w["bag"][...]
        u = ag[:, :D_CONV] * _sigmoid(ag[:, D_CONV:])
        for seg in range(nseg):
            upad_s[seg, CONV_PAD:CONV_PAD + P, :] = u[seg * P:(seg + 1) * P, :]

        def proj(name, bias, c0, width=2 * LANES):
            cs = slice(c0, c0 + width)
            return _dot(hb, w[name][:, cs]) + w[bias][:, cs]

        def gm_a(c0):
            ma_s[rows, c0:c0 + 2 * LANES] = _sigmoid(proj("wgm", "bgm", c0))

        def gm_b(c0):
            sgb_s[rows, c0:c0 + 2 * LANES] = _sigmoid(proj("wgm", "bgm", D_MODEL + c0))

        def q_part(c0):
            q_s[rows, c0:c0 + 2 * LANES] = (proj("wq", "bq", c0) * (HEAD_DIM ** -0.5)).astype(BF16)

        def v_part(c0):
            v_s[rows, c0:c0 + 2 * LANES] = proj("wv", "bv", c0).astype(BF16)

        def o_part(c0):
            so_s[rows, c0:c0 + 2 * LANES] = _sigmoid(proj("wog", "bog", c0))

        def k_part(c0):
            rs = slice(c0, c0 + 2 * LANES)
            kT_s[i, rs, :] = (_dot_nt(w["wkT"][rs, :], hb) + w["bk"][rs, :]).astype(BF16)

        jobs = ([functools.partial(gm_a, c0) for c0 in range(0, D_MODEL, 2 * LANES)]
                + [functools.partial(gm_b, c0) for c0 in range(0, D_MODEL, 2 * LANES)]
                + [functools.partial(f, c0) for f in (q_part, k_part, v_part, o_part)
                   for c0 in range(0, D_MLSTM, 2 * LANES)])
        conv = {}
        for cb in range(D_CONV // LANES):
            cs = slice(cb * LANES, (cb + 1) * LANES)
            for seg in range(nseg):
                for rb in range(P // CONV_RB):
                    conv[(cb, seg, rb)] = _conv_block(upad_s, seg, rb * CONV_RB, cs, w["wdw"], w["bdw"])
                    if jobs:
                        jobs.pop(0)()
        for job in jobs:
            job()
        cu = jnp.concatenate(
            [jnp.concatenate([conv[(cb, seg, rb)] for seg in range(nseg) for rb in range(P // CONV_RB)], axis=0)
             for cb in range(D_CONV // LANES)], axis=1)
        mu = jnp.mean(cu, axis=-1, keepdims=True)
        cc = cu - mu
        cn = cc * lax.rsqrt(jnp.mean(cc * cc, axis=-1, keepdims=True) + EPS) * w["lng"][...] + w["lnb"][...]
        ca = (cn * _sigmoid(cn)).astype(BF16)
        ma_s[rows, :] = ma_s[rows, :] * _dot(ca, w["wco"][...])
        return carry

    if nsub == 1:
        phase1(0, 0)
    else:
        lax.fori_loop(0, nsub, phase1, 0)

    ones_col = (lax.broadcasted_iota(jnp.int32, (L, HEAD_DIM), 1) == 0).astype(F32).astype(BF16)
    pad_rows = jnp.zeros((LANES - 3 * N_UNITS, L), F32)

    def gate_prep(c, m_vec):
        sc = scan_s[c]
        a, run_max, bsum = sc[0:N_UNITS], sc[N_UNITS:2 * N_UNITS], sc[2 * N_UNITS:3 * N_UNITS]
        tot, a_max = sc[3 * N_UNITS:4 * N_UNITS, 0:1], sc[4 * N_UNITS:5 * N_UNITS, 0:1]
        big_m = jnp.maximum(m_vec, run_max)
        m_end = jnp.maximum(m_vec, a_max)
        cols = jnp.concatenate(
            [big_m, jnp.exp(m_vec - big_m), jnp.exp(-bsum - big_m), pad_rows], axis=0).T
        return a, cols, jnp.exp(a - m_end), jnp.exp(m_vec - m_end), tot + m_end

    qk_cache = {}

    def unit(d, hd, c, prep, first_chunk, want_state):
        a, cols, wk, decay, _ = prep
        rows = slice(c * L, (c + 1) * L)
        hs = slice(hd * HEAD_DIM, (hd + 1) * HEAD_DIM)
        idx = d * N_HEADS + hd
        col = lambda k: cols[:, k * N_UNITS + idx:k * N_UNITS + idx + 1]
        qc = q_s[rows, hs]
        kTc = kT_s[c, hs, :]
        vaug = jnp.concatenate([v_s[rows, hs], ones_col], axis=1)
        if nsub == 1 and (hd, c) in qk_cache:
            qk = qk_cache[(hd, c)]
        else:
            qk = _dot(qc, kTc)
            qk_cache[(hd, c)] = qk
        w_intra = jnp.where(lower if d == 0 else upper, jnp.exp(a[idx:idx + 1, :] - col(0)), 0.0)
        nd = _dot((qk * w_intra).astype(BF16), vaug)
        if has_state or not first_chunk:
            nd = nd + col(1) * _dot(qc, cst_s[idx].astype(BF16))
        den = nd[:, HEAD_DIM:HEAD_DIM + 1]
        h = nd[:, :HEAD_DIM] * (1.0 / jnp.maximum(jnp.abs(den), col(2)))
        if d == 0:
            hm_s[rows, hs] = h
        else:
            hm_s[rows, hs] = hm_s[rows, hs] + h
        if want_state:
            kw = (kTc.astype(F32) * wk[idx:idx + 1, :]).astype(BF16)
            upd = _dot(kw, vaug)
            if has_state or not first_chunk:
                upd = upd + decay[idx:idx + 1, :] * cst_s[idx]
            cst_s[idx] = upd

    if has_state:
        for idx in range(N_UNITS):
            cst_s[idx] = c0_ref[0, idx]
        m_vec = m0_ref[0, :, 0:1]
    else:
        m_vec = jnp.zeros((N_UNITS, 1), F32)

    dir_rows = lax.broadcasted_iota(jnp.int32, (N_UNITS, 1), 0) >= N_HEADS
    prep = None
    for d in range(2):
        order = list(range(nsub)) if d == 0 else list(range(nsub - 1, -1, -1))
        for pos, c in enumerate(order):
            if nsub > 1 or prep is None:
                prep = gate_prep(c, m_vec)
            for hd in range(N_HEADS):
                unit(d, hd, c, prep, pos == 0, emit_state or pos < nsub - 1)
            m_vec = jnp.where(dir_rows == (d == 1), prep[4], m_vec)

    if emit_state:
        for idx in range(N_UNITS):
            caug = cst_s[idx]
            cout_ref[0, idx] = caug[:, :HEAD_DIM]
            nout_ref[0, idx:idx + 1, :] = caug[:, HEAD_DIM:].T[0:1, :]
        mout_ref[0] = jnp.broadcast_to(m_vec, (N_UNITS, LANES))

    e_iota = lax.broadcasted_iota(jnp.int32, (LANES, SUB), 0)
    g_of_e = lax.shift_right_logical(e_iota, 2)
    j_of_e = lax.bitwise_and(e_iota, EXPERTS_PER_GROUP - 1)
    r8 = lax.broadcasted_iota(jnp.int32, (8, SUB), 0)
    before_b = (t_idx < s_idx).astype(F32).astype(BF16)

    def phase3(i, carry):
        r0 = pl.multiple_of(i * SUB, SUB)
        rows = pl.ds(r0, SUB)
        hm = hm_s[rows, :]
        heads = []
        for hd in range(N_HEADS):
            hh = hm[:, hd * HEAD_DIM:(hd + 1) * HEAD_DIM]
            heads.append(hh * lax.rsqrt(jnp.mean(hh * hh, axis=-1, keepdims=True) + EPS))
        hn = jnp.concatenate(heads, axis=1) * w["hng"][...]
        hb2 = (so_s[rows, :] * hn).astype(BF16)
        br_b = _dot(hb2, w["wmo"][...])
        mixed = (ma_s[rows, :] + sgb_s[rows, :] * br_b).astype(BF16)
        x1 = x_ref[0, rows, :] + mod_row(2) * _dot(mixed, w["wo"][...])
        x1_ref[0, rows, :] = x1
        xn = x1 * lax.rsqrt(jnp.mean(x1 * x1, axis=-1, keepdims=True) + EPS) * w["g2"][...]
        h2 = xn * (1.0 + mod_row(4)) + mod_row(3)
        h2_ref[0, rows, :] = h2.astype(BF16)

        h2_hi = h2.astype(BF16)
        h2_lo = (h2 - h2_hi.astype(F32)).astype(BF16)
        lg = _dot(h2_hi, w["wrt2"][...])
        lg = lg[:, :LANES] + lg[:, LANES:] + _dot(h2_lo, w["wrt2"][:, :LANES])
        lt = lg.T + w["brtT"][...]
        gl = [lt[N_EXPERTS + g:N_EXPERTS + g + 1, :] for g in range(N_GROUPS)]
        best, gsel = gl[0], jnp.zeros((1, SUB), jnp.int32)
        for g in range(1, N_GROUPS):
            better = gl[g] > best
            gsel = jnp.where(better, g, gsel)
            best = jnp.where(better, gl[g], best)
        gp_sel = 1.0 / sum(jnp.exp(v - best) for v in gl)
        el = []
        for j in range(EXPERTS_PER_GROUP):
            v = lt[j:j + 1, :]
            for g in range(1, N_GROUPS):
                r = g * EXPERTS_PER_GROUP + j
                v = jnp.where(gsel == g, lt[r:r + 1, :], v)
            el.append(v)
        l1, e1 = el[0], jnp.zeros((1, SUB), jnp.int32)
        for j in range(1, EXPERTS_PER_GROUP):
            better = el[j] > l1
            e1 = jnp.where(better, j, e1)
            l1 = jnp.where(better, el[j], l1)
        l2 = jnp.full((1, SUB), -jnp.inf, F32)
        e2 = jnp.zeros((1, SUB), jnp.int32)
        for j in range(EXPERTS_PER_GROUP):
            better = jnp.logical_and(e1 != j, el[j] > l2)
            e2 = jnp.where(better, j, e2)
            l2 = jnp.where(better, el[j], l2)
        r2 = jnp.exp(l2 - l1)
        wt1 = gp_sel / (1.0 + r2)
        wt2 = gp_sel * r2 / (1.0 + r2)
        in_group = g_of_e == gsel
        comb_t = (jnp.where(jnp.logical_and(in_group, j_of_e == e1), wt1, 0.0)
                  + jnp.where(jnp.logical_and(in_group, j_of_e == e2), wt2, 0.0))

        onehot = (r8 == gsel).astype(F32)
        rank = jnp.sum(onehot * _dot(onehot.astype(BF16), before_b), axis=0, keepdims=True)
        gsel_f = gsel.astype(F32)
        r8rows = pl.ds(pl.multiple_of(i * 8, 8), 8)
        route_ref[0, r8rows, :] = jnp.where(r8 == 0, gsel_f, jnp.where(r8 == 1, rank, 0.0))
        cnt_ref[0, r8rows, :] = jnp.broadcast_to(jnp.sum(onehot, axis=1, keepdims=True), (8, LANES))
        comb_t = jnp.where(e_iota == ROUTE_GROUP_LANE, gsel_f,
                           jnp.where(e_iota == ROUTE_RANK_LANE, rank, comb_t))
        comb_ref[0, rows, :] = comb_t.T
        return carry

    if nsub == 1:
        phase3(0, 0)
    else:
        lax.fori_loop(0, nsub, phase3, 0)


def _const_spec(a):
    nd = a.ndim
    return pl.BlockSpec(a.shape, lambda b, _nd=nd: (0,) * _nd, pipeline_mode=pl.Buffered(1))


def _mixer(x, mod, mod_index, weights, P, state=None, emit_state=False):
    B, T, _ = x.shape
    nsub = T // SUB
    has_state = state is not None
    seq_mode = {} if nsub == 1 else {"pipeline_mode": pl.Buffered(1)}
    in_specs = [
        pl.BlockSpec((1, T, D_MODEL), lambda b: (b, 0, 0), **seq_mode),
        pl.BlockSpec((1, N_ADA, D_MODEL), lambda b: (mod_index(b), 0, 0)),
    ]
    args = [x, mod]
    if has_state:
        caug0, m0 = state
        in_specs += [
            pl.BlockSpec((1, N_UNITS, HEAD_DIM, 2 * HEAD_DIM), lambda b: (b, 0, 0, 0)),
            pl.BlockSpec((1, N_UNITS, LANES), lambda b: (b, 0, 0)),
        ]
        args += [caug0, m0]
    for name in _MIXER_WEIGHTS:
        in_specs.append(_const_spec(weights[name]))
        args.append(weights[name])
    out_shape = [
        jax.ShapeDtypeStruct((B, T, D_MODEL), F32),
        jax.ShapeDtypeStruct((B, T, D_MODEL), BF16),
        jax.ShapeDtypeStruct((B, T, LANES), F32),
        jax.ShapeDtypeStruct((B, nsub * 8, SUB), F32),
        jax.ShapeDtypeStruct((B, nsub * 8, LANES), F32),
    ]
    out_specs = [
        pl.BlockSpec((1, T, D_MODEL), lambda b: (b, 0, 0), **seq_mode),
        pl.BlockSpec((1, T, D_MODEL), lambda b: (b, 0, 0), **seq_mode),
        pl.BlockSpec((1, T, LANES), lambda b: (b, 0, 0)),
        pl.BlockSpec((1, nsub * 8, SUB), lambda b: (b, 0, 0)),
        pl.BlockSpec((1, nsub * 8, LANES), lambda b: (b, 0, 0)),
    ]
    if emit_state:
        out_shape += [
            jax.ShapeDtypeStruct((B, N_UNITS, HEAD_DIM, HEAD_DIM), F32),
            jax.ShapeDtypeStruct((B, N_UNITS, HEAD_DIM), F32),
            jax.ShapeDtypeStruct((B, N_UNITS, LANES), F32),
        ]
        out_specs += [
            pl.BlockSpec((1, N_UNITS, HEAD_DIM, HEAD_DIM), lambda b: (b, 0, 0, 0)),
            pl.BlockSpec((1, N_UNITS, HEAD_DIM), lambda b: (b, 0, 0)),
            pl.BlockSpec((1, N_UNITS, LANES), lambda b: (b, 0, 0)),
        ]
    scratch = [
        pltpu.VMEM((T, D_MLSTM), BF16),
        pltpu.VMEM((nsub, D_MLSTM, SUB), BF16),
        pltpu.VMEM((T, D_MLSTM), BF16),
        pltpu.VMEM((T, D_MLSTM), F32),
        pltpu.VMEM((nsub, 5 * N_UNITS, SUB), F32),
        pltpu.VMEM((T, D_MODEL), F32),
        pltpu.VMEM((T, D_MODEL), F32),
        pltpu.VMEM((T, D_MLSTM), F32),
        pltpu.VMEM((N_UNITS, HEAD_DIM, 2 * HEAD_DIM), F32),
        pltpu.VMEM((SUB // P, P + 2 * CONV_PAD, D_CONV), F32),
    ]
    return pl.pallas_call(
        functools.partial(_mixer_kernel, T, P, has_state, emit_state),
        grid=(B,),
        in_specs=in_specs,
        out_specs=out_specs,
        out_shape=out_shape,
        scratch_shapes=scratch,
        compiler_params=pltpu.CompilerParams(
            dimension_semantics=("arbitrary",), vmem_limit_bytes=VMEM_LIMIT),
        name="mixer_T%d" % T,
    )(*args)


def _dest_in_block(group, rank, starts):
    dest = rank
    for g in range(N_GROUPS):
        dest = dest + jnp.where(group == float(g), starts[g], 0.0)
    return dest


def _copy_segments(src_refs, dst_refs, src_starts, dst_starts, n_pieces):
    for g in range(N_GROUPS):
        def body(k, carry, g=g):
            s = pl.multiple_of(src_starts[g] + k * ROW_ALIGN, ROW_ALIGN)
            d = pl.multiple_of(dst_starts[g] + k * ROW_ALIGN, ROW_ALIGN)
            for src, dst in zip(src_refs, dst_refs):
                dst[pl.ds(d, ROW_ALIGN), :] = src[pl.ds(s, ROW_ALIGN), :]
            return carry
        lax.fori_loop(0, n_pieces[g], body, 0)


def _dispatch_kernel(n_ctx_blocks, start_ref, npiece_ref, off_ref,
                     h2c_ref, h2l_ref, cbc_ref, cbl_ref, rtc_ref, rtl_ref,
                     xs_ref, cs_ref, sx_s, sc_s):
    b = pl.program_id(0)
    is_ctx = b < n_ctx_blocks

    @pl.when(b == 0)
    def _():
        xs_ref[...] = jnp.zeros_like(xs_ref)
        cs_ref[...] = jnp.zeros_like(cs_ref)

    h2 = jnp.where(is_ctx, h2c_ref[0], h2l_ref[0])
    cb = jnp.where(is_ctx, cbc_ref[0], cbl_ref[0])
    rt = jnp.where(is_ctx, rtc_ref[0], rtl_ref[0])
    starts = [start_ref[b * N_GROUPS + g] for g in range(N_GROUPS)]
    dest = _dest_in_block(rt[0:1, :], rt[1:2, :], [s.astype(F32) for s in starts])
    row = lax.broadcasted_iota(jnp.int32, (SORT_ROWS, SUB), 0).astype(F32)
    perm = (row == dest).astype(F32).astype(BF16)
    cb_hi = cb.astype(BF16)
    cb_lo = (cb - cb_hi.astype(F32)).astype(BF16)
    sx_s[...] = _dot(perm, h2).astype(BF16)
    sc_s[...] = _dot(perm, jnp.concatenate([cb_hi, cb_lo], axis=1)).astype(BF16)
    _copy_segments((sx_s, sc_s), (xs_ref, cs_ref), starts,
                   [off_ref[b * N_GROUPS + g] for g in range(N_GROUPS)],
                   [npiece_ref[b * N_GROUPS + g] for g in range(N_GROUPS)])


def _experts_kernel(tgroup_ref, tvalid_ref, tfirst_ref, xs_ref, cs_ref, wg_ref, wu_ref, wd_ref, ys_ref,
                    wg_s, wu_s, wd_s):
    i = pl.program_id(0)

    @pl.when(tfirst_ref[i] == 1)
    def _():
        for j in range(EXPERTS_PER_GROUP):
            cols = slice(j * D_EXPERT, (j + 1) * D_EXPERT)
            wg_s[:, cols] = wg_ref[j].astype(BF16)
            wu_s[:, cols] = wu_ref[j].astype(BF16)
            wd_s[cols, :] = wd_ref[j].astype(BF16)

    @pl.when(tvalid_ref[i] == 1)
    def _():
        x = xs_ref[...]
        g = _dot(x, wg_s[...])
        u = _dot(x, wu_s[...])
        comb = cs_ref[:, :LANES].astype(F32) + cs_ref[:, LANES:].astype(F32)
        lane = lax.broadcasted_iota(jnp.int32, comb.shape, 1)
        first = tgroup_ref[i] * EXPERTS_PER_GROUP
        parts = []
        for j in range(EXPERTS_PER_GROUP):
            cols = slice(j * D_EXPERT, (j + 1) * D_EXPERT)
            cw = jnp.sum(jnp.where(lane == first + j, comb, 0.0), axis=1, keepdims=True)
            gj = g[:, cols]
            parts.append((gj * _sigmoid(gj) * u[:, cols] * cw).astype(BF16))
        ys_ref[...] = _dot(jnp.concatenate(parts, axis=1), wd_s[...]).astype(BF16)

    @pl.when(tvalid_ref[i] == 0)
    def _():
        ys_ref[...] = jnp.zeros_like(ys_ref)


def _combine_kernel(n_ctx_blocks, blocks_per_lat_seq, start_ref, npiece_ref, off_ref,
                    x1c_ref, x1l_ref, cbc_ref, cbl_ref, ys_ref, mod_ref, gf_ref, yc_ref, yl_ref, loc_s):
    b = pl.program_id(0)
    is_ctx = b < n_ctx_blocks
    starts = [start_ref[b * N_GROUPS + g] for g in range(N_GROUPS)]
    loc_s[...] = jnp.zeros_like(loc_s)
    _copy_segments((ys_ref,), (loc_s,), [off_ref[b * N_GROUPS + g] for g in range(N_GROUPS)], starts,
                   [npiece_ref[b * N_GROUPS + g] for g in range(N_GROUPS)])
    cb = jnp.where(is_ctx, cbc_ref[0], cbl_ref[0])
    dest = _dest_in_block(cb[:, ROUTE_GROUP_LANE:ROUTE_GROUP_LANE + 1],
                          cb[:, ROUTE_RANK_LANE:ROUTE_RANK_LANE + 1],
                          [s.astype(F32) for s in starts])
    col = lax.broadcasted_iota(jnp.int32, (SUB, SORT_ROWS), 1).astype(F32)
    unperm = (col == dest).astype(F32).astype(BF16)
    moe = _dot(unperm, loc_s[...])
    x1 = jnp.where(is_ctx, x1c_ref[0], x1l_ref[0])
    mrow = jnp.where(is_ctx, 0, 1 + jnp.maximum(b - n_ctx_blocks, 0) // blocks_per_lat_seq)
    x2 = x1 + mod_ref[mrow, N_ADA - 1:N_ADA, :] * moe
    y = x2 * lax.rsqrt(jnp.mean(x2 * x2, axis=-1, keepdims=True) + EPS) * gf_ref[...]

    @pl.when(is_ctx)
    def _():
        yc_ref[0] = y

    @pl.when(jnp.logical_not(is_ctx))
    def _():
        yl_ref[0] = y


def _moe_plan(cnt, n_tiles):
    cnt_al = (cnt + ROW_ALIGN - 1) // ROW_ALIGN * ROW_ALIGN
    start = jnp.cumsum(cnt_al, axis=1) - cnt_al
    gpad = (jnp.sum(cnt_al, axis=0) + MOE_TM - 1) // MOE_TM * MOE_TM
    gbase = jnp.cumsum(gpad) - gpad
    off = gbase[None, :] + jnp.cumsum(cnt_al, axis=0) - cnt_al
    tile_end = jnp.cumsum(gpad // MOE_TM)
    t = jnp.arange(n_tiles, dtype=jnp.int32)
    tgroup = jnp.sum((t[:, None] >= tile_end[None, :]).astype(jnp.int32), axis=1)
    valid = t < tile_end[-1]
    last_group = jnp.sum((tile_end[-1] - 1 >= tile_end).astype(jnp.int32))
    tgroup = jnp.where(valid, tgroup, last_group)
    first = jnp.logical_and(valid, jnp.concatenate([jnp.ones((1,), bool), tgroup[1:] != tgroup[:-1]]))
    i32 = lambda a: a.astype(jnp.int32).reshape(-1)
    return i32(start), i32(cnt_al // ROW_ALIGN), i32(off), i32(tgroup), i32(valid), i32(first)


def _moe(x1c, x1l, h2c, h2l, cbc, cbl, rtc, rtl, cnt, mod, blocks_per_lat_seq, wg, wu, wd, gf):
    nc, nl = x1c.shape[0], x1l.shape[0]
    nb = nc + nl
    n_rows_max = nb * SUB + nb * N_GROUPS * (ROW_ALIGN - 1) + N_GROUPS * (MOE_TM - ROW_ALIGN)
    n_tiles = -(-n_rows_max // MOE_TM)
    ns = n_tiles * MOE_TM
    start, npiece, off, tgroup, tvalid, tfirst = _moe_plan(cnt, n_tiles)

    cmap = lambda b, *_: (jnp.minimum(b, nc - 1), 0, 0)
    lmap = lambda b, *_: (jnp.maximum(b - nc, 0), 0, 0)
    whole = lambda *_: (0, 0)
    once = {"pipeline_mode": pl.Buffered(1)}
    arb = pltpu.CompilerParams(dimension_semantics=("arbitrary",), vmem_limit_bytes=VMEM_LIMIT)

    xs, cs = pl.pallas_call(
        functools.partial(_dispatch_kernel, nc),
        grid_spec=pltpu.PrefetchScalarGridSpec(
            num_scalar_prefetch=3, grid=(nb,),
            in_specs=[
                pl.BlockSpec((1, SUB, D_MODEL), cmap), pl.BlockSpec((1, SUB, D_MODEL), lmap),
                pl.BlockSpec((1, SUB, LANES), cmap), pl.BlockSpec((1, SUB, LANES), lmap),
                pl.BlockSpec((1, 8, SUB), cmap), pl.BlockSpec((1, 8, SUB), lmap),
            ],
            out_specs=[pl.BlockSpec((ns, D_MODEL), whole, **once), pl.BlockSpec((ns, 2 * LANES), whole, **once)],
            scratch_shapes=[pltpu.VMEM((SORT_ROWS, D_MODEL), BF16), pltpu.VMEM((SORT_ROWS, 2 * LANES), BF16)],
        ),
        out_shape=[jax.ShapeDtypeStruct((ns, D_MODEL), BF16), jax.ShapeDtypeStruct((ns, 2 * LANES), BF16)],
        compiler_params=arb,
        name="moe_dispatch",
    )(start, npiece, off, h2c, h2l, cbc, cbl, rtc, rtl)

    wmap = lambda i, tg, tv, tf: (tg[i], 0, 0)
    ys = pl.pallas_call(
        _experts_kernel,
        grid_spec=pltpu.PrefetchScalarGridSpec(
            num_scalar_prefetch=3, grid=(n_tiles,),
            in_specs=[
                pl.BlockSpec((MOE_TM, D_MODEL), lambda i, *_: (i, 0)),
                pl.BlockSpec((MOE_TM, 2 * LANES), lambda i, *_: (i, 0)),
                pl.BlockSpec((EXPERTS_PER_GROUP, D_MODEL, D_EXPERT), wmap),
                pl.BlockSpec((EXPERTS_PER_GROUP, D_MODEL, D_EXPERT), wmap),
                pl.BlockSpec((EXPERTS_PER_GROUP, D_EXPERT, D_MODEL), wmap),
            ],
            out_specs=pl.BlockSpec((MOE_TM, D_MODEL), lambda i, *_: (i, 0)),
            scratch_shapes=[pltpu.VMEM((D_MODEL, EXPERTS_PER_GROUP * D_EXPERT), BF16),
                            pltpu.VMEM((D_MODEL, EXPERTS_PER_GROUP * D_EXPERT), BF16),
                            pltpu.VMEM((EXPERTS_PER_GROUP * D_EXPERT, D_MODEL), BF16)],
        ),
        out_shape=jax.ShapeDtypeStruct((ns, D_MODEL), BF16),
        compiler_params=arb,
        name="moe_experts",
    )(tgroup, tvalid, tfirst, xs, cs, wg, wu, wd)

    yc, yl = pl.pallas_call(
        functools.partial(_combine_kernel, nc, blocks_per_lat_seq),
        grid_spec=pltpu.PrefetchScalarGridSpec(
            num_scalar_prefetch=3, grid=(nb,),
            in_specs=[
                pl.BlockSpec((1, SUB, D_MODEL), cmap), pl.BlockSpec((1, SUB, D_MODEL), lmap),
                pl.BlockSpec((1, SUB, LANES), cmap), pl.BlockSpec((1, SUB, LANES), lmap),
                pl.BlockSpec((ns, D_MODEL), whole, **once),
                pl.BlockSpec(mod.shape, lambda *_: (0, 0, 0)),
                pl.BlockSpec((1, D_MODEL), whole),
            ],
            out_specs=[pl.BlockSpec((1, SUB, D_MODEL), cmap), pl.BlockSpec((1, SUB, D_MODEL), lmap)],
            scratch_shapes=[pltpu.VMEM((SORT_ROWS, D_MODEL), BF16)],
        ),
        out_shape=[jax.ShapeDtypeStruct((nc, SUB, D_MODEL), F32), jax.ShapeDtypeStruct((nl, SUB, D_MODEL), F32)],
        compiler_params=arb,
        name="moe_combine",
    )(start, npiece, off, x1c, x1l, cbc, cbl, ys, mod, gf)
    return yc, yl


def _prep_weights(norm1_g, w_in, b_in, b_gates, w_dw, b_dw, conv_ln_g, conv_ln_b, w_conv_out,
                  mlstm_hn_g, w_mlstm_out, w_o, norm2_g, w_rg, b_rg, w_re, b_re):
    s_a = 2 * D_CONV
    s_q = s_a + D_MLSTM
    s_k = s_q + D_MLSTM
    s_v = s_k + D_MLSTM
    s_o = s_v + D_MLSTM
    s_g = s_o + 4 * N_HEADS
    row = lambda v: v.reshape(1, -1).astype(F32)
    wb = w_in.astype(BF16)
    gperm = [d * 2 * N_HEADS + gate * N_HEADS + hd
             for gate in range(2) for d in range(2) for hd in range(N_HEADS)]
    gperm = jnp.array(gperm, jnp.int32)
    bg = (b_in[s_o:s_g] + b_gates.reshape(-1))[gperm]
    wrt = jnp.concatenate([w_re, w_rg], axis=1)
    wrt = jnp.zeros((D_MODEL, LANES), F32).at[:, :N_EXPERTS + N_GROUPS].set(wrt)
    wrt_hi = wrt.astype(BF16)
    wrt2 = jnp.concatenate([wrt_hi, (wrt - wrt_hi.astype(F32)).astype(BF16)], axis=1)
    brtT = jnp.zeros((LANES, 1), F32).at[:N_EXPERTS + N_GROUPS, 0].set(jnp.concatenate([b_re, b_rg]))
    return {
        "g1": row(norm1_g),
        "wag": wb[:, :s_a], "bag": row(b_in[:s_a]),
        "wq": wb[:, s_a:s_q], "bq": row(b_in[s_a:s_q]),
        "wkT": wb[:, s_q:s_k].T, "bk": b_in[s_q:s_k].reshape(-1, 1),
        "wv": wb[:, s_k:s_v], "bv": row(b_in[s_k:s_v]),
        "wog": wb[:, s_v:s_o], "bog": row(b_in[s_v:s_o]),
        "wgifT": wb[:, s_o:s_g].T[gperm], "bgifT": bg.reshape(-1, 1),
        "wgm": wb[:, s_g:], "bgm": row(b_in[s_g:]),
        "wdw": w_dw.astype(F32), "bdw": row(b_dw), "lng": row(conv_ln_g), "lnb": row(conv_ln_b),
        "wco": w_conv_out.astype(BF16), "hng": row(mlstm_hn_g), "wmo": w_mlstm_out.astype(BF16),
        "wo": w_o.astype(BF16), "g2": row(norm2_g), "wrt2": wrt2, "brtT": brtT,
    }


def kernel(x_prompt, x_sample, state_C, state_n, state_m, c, c_ctx, norm1_g, w_ada, b_ada, w_in, b_in, b_gates, w_dw, b_dw, conv_ln_g, conv_ln_b, w_conv_out, mlstm_hn_g, w_mlstm_out, w_o, norm2_g, w_rg, b_rg, w_re, b_re, w_e_gate, w_e_up, w_e_down, norm_final_g):
    B, S, _ = x_prompt.shape
    Bd, Sd, _ = x_sample.shape
    assert w_ada.shape[0] == 1, "single trunk layer"
    assert S == SUB and Sd % SUB == 0

    cin = jnp.zeros((8, D_MODEL), F32).at[0].set(c_ctx).at[1:1 + Bd].set(c)
    mod = _ada(cin, w_ada[0], b_ada[0].reshape(1, -1)).reshape(8, N_ADA, D_MODEL)

    wts = _prep_weights(norm1_g[0], w_in[0], b_in[0], b_gates[0], w_dw[0], b_dw[0], conv_ln_g[0],
                        conv_ln_b[0], w_conv_out[0], mlstm_hn_g[0], w_mlstm_out[0], w_o[0],
                        norm2_g[0], w_rg[0], b_rg[0], w_re[0], b_re[0])

    x1p, h2p, cbp, rtp, cntp, c_new, n_new, m_new = _mixer(
        x_prompt, mod, lambda b: 0, wts, P=S, emit_state=True)

    sc = state_C[:, 0].reshape(Bd, N_UNITS, HEAD_DIM, HEAD_DIM)
    sn = state_n[:, 0].reshape(Bd, N_UNITS, HEAD_DIM, 1)
    caug0 = jnp.concatenate([sc, sn, jnp.zeros((Bd, N_UNITS, HEAD_DIM, HEAD_DIM - 1), F32)], axis=-1)
    m0 = jnp.broadcast_to(state_m[:, 0].reshape(Bd, N_UNITS, 1), (Bd, N_UNITS, LANES))
    x1s, h2s, cbs, rts, cnts = _mixer(x_sample, mod, lambda b: 1 + b, wts, P=GRID_W, state=(caug0, m0))

    nl = Bd * Sd // SUB
    blk = lambda a: a.reshape(nl, SUB, a.shape[-1])
    cnt = jnp.concatenate([cntp[:, :N_GROUPS, 0], cnts.reshape(nl, 8, LANES)[:, :N_GROUPS, 0]], axis=0)
    yp, ys = _moe(x1p, blk(x1s), h2p, blk(h2s), cbp, blk(cbs), rtp, rts.reshape(nl, 8, SUB),
                  cnt.astype(jnp.int32), mod, Sd // SUB, w_e_gate[0], w_e_up[0], w_e_down[0],
                  norm_final_g.reshape(1, -1))

    return (yp.reshape(B, S, D_MODEL), ys.reshape(Bd, Sd, D_MODEL),
            c_new.reshape(B, 1, 2, N_HEADS, HEAD_DIM, HEAD_DIM),
            n_new.reshape(B, 1, 2, N_HEADS, HEAD_DIM),
            m_new[:, :, 0].reshape(B, 1, 2, N_HEADS))
```

```python
import functools

import jax
import jax.numpy as jnp
from jax import lax
from jax.experimental import pallas as pl
from jax.experimental.pallas import tpu as pltpu

D_MODEL = 1024
D_CONV = 512
CONV_K = 31
D_MLSTM = 512
N_HEADS = 4
HEAD_DIM = D_MLSTM // N_HEADS
N_GROUPS = 4
EXPERTS_PER_GROUP = 4
N_EXPERTS = N_GROUPS * EXPERTS_PER_GROUP
D_EXPERT = 256
N_ADA = 6
EPS = 1e-6
GRID_W = 64

LANES = 128
SUB = 256
CONV_PAD = 16
CONV_RB = 64
N_UNITS = 2 * N_HEADS
ROW_ALIGN = 16
SORT_ROWS = SUB + N_GROUPS * ROW_ALIGN
MOE_TM = 512
MIX_TM = 512
ROUTE_GROUP_LANE = N_EXPERTS
ROUTE_RANK_LANE = N_EXPERTS + 1
VMEM_LIMIT = 58 * 1024 * 1024

BF16 = jnp.bfloat16
F32 = jnp.float32
NT_DIMS = (((1,), (1,)), ((), ()))


def _dot(a, b):
    return jnp.dot(a, b, preferred_element_type=F32)


def _dot_nt(a, b, precision=None):
    return lax.dot_general(a, b, NT_DIMS, preferred_element_type=F32, precision=precision)


def _sigmoid(x):
    return 0.5 * jnp.tanh(0.5 * x) + 0.5


def _log_sigmoid(x):
    return jnp.minimum(x, 0.0) - jnp.log1p(jnp.exp(-jnp.abs(x)))


def _split3(x):
    hi = x.astype(BF16).astype(F32)
    r1 = x - hi
    mid = r1.astype(BF16).astype(F32)
    lo = (r1 - mid).astype(BF16).astype(F32)
    return hi, mid, lo


def _ada_kernel(c_ref, w_ref, b_ref, o_ref):
    c = c_ref[...]
    s = (c * _sigmoid(c)).astype(BF16)
    o_ref[...] = _dot(s, w_ref[...].astype(BF16)) + b_ref[...]


def _ada(cin, w_ada, b_ada):
    n = w_ada.shape[1]
    bn = 1024
    return pl.pallas_call(
        _ada_kernel,
        grid=(n // bn,),
        in_specs=[
            pl.BlockSpec((8, D_MODEL), lambda j: (0, 0)),
            pl.BlockSpec((D_MODEL, bn), lambda j: (0, j)),
            pl.BlockSpec((1, bn), lambda j: (0, j)),
        ],
        out_specs=pl.BlockSpec((8, bn), lambda j: (0, j)),
        out_shape=jax.ShapeDtypeStruct((8, n), F32),
        compiler_params=pltpu.CompilerParams(dimension_semantics=("arbitrary",)),
        name="ada",
    )(cin, w_ada, b_ada)


_MIXER_WEIGHTS = (
    "g1", "wag", "bag", "wq", "bq", "wkT", "bk", "wv", "bv", "wog", "bog",
    "wgifT", "bgifT", "wgm", "bgm", "wdw", "bdw", "lng", "lnb",
    "wco", "hng", "wmo", "wo", "g2", "wrt2", "brtT",
)


def _conv_block(upad_s, seg, base, cs, wdw_ref, bdw_ref):
    sub = 8
    first = CONV_PAD - CONV_K // 2
    acc = jnp.broadcast_to(bdw_ref[0:1, cs], (CONV_RB, LANES))
    for r in range(sub):
        z = None
        for a in range((CONV_K + first + sub - 1) // sub):
            j = sub * a + r - first
            if 0 <= j < CONV_K:
                lo = base + sub * a
                term = wdw_ref[j:j + 1, cs] * upad_s[seg, lo:lo + CONV_RB + sub, cs]
                z = term if z is None else z + term
        acc = acc + z[r:r + CONV_RB, :]
    return acc


def _mixer_kernel(R, T, P, has_state, emit_state, *refs):
    L = SUB
    n_mt = R // MIX_TM
    cpm = MIX_TM // L
    n_seq = R // T
    cps = T // L
    nseg = MIX_TM // P
    assert not has_state or n_seq == 1
    it = iter(refs)
    x_ref = next(it)
    mod_ref = next(it)
    if has_state:
        c0_ref = next(it)
        m0_ref = next(it)
    w = {name: next(it) for name in _MIXER_WEIGHTS}
    x1_ref = next(it)
    h2_ref = next(it)
    comb_ref = next(it)
    route_ref = next(it)
    cnt_ref = next(it)
    if emit_state:
        cout_ref = next(it)
        nout_ref = next(it)
        mout_ref = next(it)
    (q_s, kT_s, v_s, so_s, scan_s, ma_s, sgb_s, hm_s, cst_s, upad_s) = [next(it) for _ in range(10)]

    def mod_row(i):
        return mod_ref[0, i:i + 1, :]

    zpad = jnp.zeros((CONV_PAD, D_CONV), F32)
    for seg in range(nseg):
        upad_s[seg, 0:CONV_PAD, :] = zpad
        upad_s[seg, CONV_PAD + P:CONV_PAD + P + CONV_PAD, :] = zpad

    t_idx = lax.broadcasted_iota(jnp.int32, (L, L), 0)
    s_idx = lax.broadcasted_iota(jnp.int32, (L, L), 1)
    lower = s_idx <= t_idx
    upper = s_idx >= t_idx
    triu_b = upper.astype(F32).astype(BF16)
    lane_u = lax.broadcasted_iota(jnp.int32, (N_UNITS, L), 1)
    is_bwd = lax.broadcasted_iota(jnp.int32, (N_UNITS, L), 0) >= N_HEADS

    def gate_scan(g):
        gi, lf = g[:N_UNITS], _log_sigmoid(g[N_UNITS:])
        pr = _dot(jnp.concatenate(_split3(lf), axis=0).astype(BF16), triu_b)
        pre = pr[0:N_UNITS] + pr[N_UNITS:2 * N_UNITS] + pr[2 * N_UNITS:]
        tot = pre[:, L - 1:L]
        bsum = jnp.where(is_bwd, tot - pre + lf, pre)
        a = gi - bsum
        pm, sm, k = a, a, 1
        while k < L:
            pm = jnp.where(lane_u >= k, jnp.maximum(pm, pltpu.roll(pm, k, axis=1)), pm)
            sm = jnp.where(lane_u < L - k, jnp.maximum(sm, pltpu.roll(sm, L - k, axis=1)), sm)
            k *= 2
        wide = lambda v: jnp.broadcast_to(v, (N_UNITS, L))
        return jnp.concatenate([a, jnp.where(is_bwd, sm, pm), bsum, wide(tot),
                                wide(jnp.max(a, axis=1, keepdims=True))], axis=0)

    def phase1(i, carry):
        r0 = pl.multiple_of(i * MIX_TM, MIX_TM)
        rows = pl.ds(r0, MIX_TM)
        x = x_ref[0, rows, :]
        xn = x * lax.rsqrt(jnp.mean(x * x, axis=-1, keepdims=True) + EPS) * w["g1"][...]
        hb = (xn * (1.0 + mod_row(1)) + mod_row(0)).astype(BF16)

        gates = _dot_nt(w["wgifT"][...], hb) + w["bgifT"][...]
        for j in range(cpm):
            scan_s[i * cpm + j] = gate_scan(gates[:, j * L:(j + 1) * L])
        ag = _dot(hb, w["wag"][...]) + w["bag"][...]
        u = ag[:, :D_CONV] * _sigmoid(ag[:, D_CONV:])
        for seg in range(nseg):
            upad_s[seg, CONV_PAD:CONV_PAD + P, :] = u[seg * P:(seg + 1) * P, :]

        def proj(name, bias, c0, width=2 * LANES):
            cs = slice(c0, c0 + width)
            return _dot(hb, w[name][:, cs]) + w[bias][:, cs]

        def gm_a(c0):
            ma_s[rows, c0:c0 + 2 * LANES] = _sigmoid(proj("wgm", "bgm", c0))

        def gm_b(c0):
            sgb_s[rows, c0:c0 + 2 * LANES] = _sigmoid(proj("wgm", "bgm", D_MODEL + c0))

        def q_part(c0):
            q_s[rows, c0:c0 + 2 * LANES] = (proj("wq", "bq", c0) * (HEAD_DIM ** -0.5)).astype(BF16)

        def v_part(c0):
            v_s[rows, c0:c0 + 2 * LANES] = proj("wv", "bv", c0).astype(BF16)

        def o_part(c0):
            so_s[rows, c0:c0 + 2 * LANES] = _sigmoid(proj("wog", "bog", c0))

        def k_part(c0):
            rs = slice(c0, c0 + 2 * LANES)
            kt = (_dot_nt(w["wkT"][rs, :], hb) + w["bk"][rs, :]).astype(BF16)
            for j in range(cpm):
                kT_s[i * cpm + j, rs, :] = kt[:, j * L:(j + 1) * L]

        jobs = ([functools.partial(gm_a, c0) for c0 in range(0, D_MODEL, 2 * LANES)]
                + [functools.partial(gm_b, c0) for c0 in range(0, D_MODEL, 2 * LANES)]
                + [functools.partial(f, c0) for f in (q_part, k_part, v_part, o_part)
                   for c0 in range(0, D_MLSTM, 2 * LANES)])
        n_jobs = len(jobs)
        conv = {}
        n_pieces = (D_CONV // LANES) * nseg * (P // CONV_RB)
        for cb in range(D_CONV // LANES):
            cs = slice(cb * LANES, (cb + 1) * LANES)
            for seg in range(nseg):
                for rb in range(P // CONV_RB):
                    conv[(cb, seg, rb)] = _conv_block(upad_s, seg, rb * CONV_RB, cs, w["wdw"], w["bdw"])
                    if jobs and len(conv) * n_jobs >= (n_jobs - len(jobs) + 1) * n_pieces:
                        jobs.pop(0)()
        for job in jobs:
            job()
        cu = jnp.concatenate(
            [jnp.concatenate([conv[(cb, seg, rb)] for seg in range(nseg) for rb in range(P // CONV_RB)], axis=0)
             for cb in range(D_CONV // LANES)], axis=1)
        mu = jnp.mean(cu, axis=-1, keepdims=True)
        cc = cu - mu
        cn = cc * lax.rsqrt(jnp.mean(cc * cc, axis=-1, keepdims=True) + EPS) * w["lng"][...] + w["lnb"][...]
        ca = (cn * _sigmoid(cn)).astype(BF16)
        ma_s[rows, :] = ma_s[rows, :] * _dot(ca, w["wco"][...])
        return carry

    if n_mt == 1:
        phase1(0, 0)
    else:
        lax.fori_loop(0, n_mt, phase1, 0)

    ones_col = (lax.broadcasted_iota(jnp.int32, (L, HEAD_DIM), 1) == 0).astype(F32).astype(BF16)
    pad_rows = jnp.zeros((LANES - 3 * N_UNITS, L), F32)

    def gate_prep(c, m_vec):
        sc = scan_s[c]
        a, run_max, bsum = sc[0:N_UNITS], sc[N_UNITS:2 * N_UNITS], sc[2 * N_UNITS:3 * N_UNITS]
        tot, a_max = sc[3 * N_UNITS:4 * N_UNITS, 0:1], sc[4 * N_UNITS:5 * N_UNITS, 0:1]
        big_m = jnp.maximum(m_vec, run_max)
        m_end = jnp.maximum(m_vec, a_max)
        cols = jnp.concatenate(
            [big_m, jnp.exp(m_vec - big_m), jnp.exp(-bsum - big_m), pad_rows], axis=0).T
        return a, cols, jnp.exp(a - m_end), jnp.exp(m_vec - m_end), tot + m_end

    qk_cache = {}

    def unit(d, hd, c, prep, first_chunk, want_state):
        a, cols, wk, decay, _ = prep
        rows = slice(c * L, (c + 1) * L)
        hs = slice(hd * HEAD_DIM, (hd + 1) * HEAD_DIM)
        idx = d * N_HEADS + hd
        col = lambda k: cols[:, k * N_UNITS + idx:k * N_UNITS + idx + 1]
        qc = q_s[rows, hs]
        kTc = kT_s[c, hs, :]
        vaug = jnp.concatenate([v_s[rows, hs], ones_col], axis=1)
        if cps == 1 and (hd, c) in qk_cache:
            qk = qk_cache[(hd, c)]
        else:
            qk = _dot(qc, kTc)
            qk_cache[(hd, c)] = qk
        w_intra = jnp.where(lower if d == 0 else upper, jnp.exp(a[idx:idx + 1, :] - col(0)), 0.0)
        nd = _dot((qk * w_intra).astype(BF16), vaug)
        if has_state or not first_chunk:
            nd = nd + col(1) * _dot(qc, cst_s[idx].astype(BF16))
        den = nd[:, HEAD_DIM:HEAD_DIM + 1]
        h = nd[:, :HEAD_DIM] * (1.0 / jnp.maximum(jnp.abs(den), col(2)))
        if d == 0:
            hm_s[rows, hs] = h
        else:
            hm_s[rows, hs] = hm_s[rows, hs] + h
        if want_state:
            kw = (kTc.astype(F32) * wk[idx:idx + 1, :]).astype(BF16)
            upd = _dot(kw, vaug)
            if has_state or not first_chunk:
                upd = upd + decay[idx:idx + 1, :] * cst_s[idx]
            cst_s[idx] = upd

    dir_rows = lax.broadcasted_iota(jnp.int32, (N_UNITS, 1), 0) >= N_HEADS
    for seq in range(n_seq):
        if has_state:
            for idx in range(N_UNITS):
                cst_s[idx] = c0_ref[0, idx]
            m_vec = m0_ref[0, :, 0:1]
        else:
            m_vec = jnp.zeros((N_UNITS, 1), F32)
        prep = None
        for d in range(2):
            order = list(range(cps)) if d == 0 else list(range(cps - 1, -1, -1))
            for pos, c in enumerate(order):
                if cps > 1 or prep is None:
                    prep = gate_prep(seq * cps + c, m_vec)
                for hd in range(N_HEADS):
                    unit(d, hd, seq * cps + c, prep, pos == 0, emit_state or pos < cps - 1)
                m_vec = jnp.where(dir_rows == (d == 1), prep[4], m_vec)
        if emit_state:
            for idx in range(N_UNITS):
                caug = cst_s[idx]
                cout_ref[0, seq * N_UNITS + idx] = caug[:, :HEAD_DIM]
                nout_ref[0, seq * N_UNITS + idx:seq * N_UNITS + idx + 1, :] = caug[:, HEAD_DIM:].T[0:1, :]
            mout_ref[0, seq * N_UNITS:(seq + 1) * N_UNITS, :] = jnp.broadcast_to(m_vec, (N_UNITS, LANES))

    e_iota = lax.broadcasted_iota(jnp.int32, (LANES, MIX_TM), 0)
    g_of_e = lax.shift_right_logical(e_iota, 2)
    j_of_e = lax.bitwise_and(e_iota, EXPERTS_PER_GROUP - 1)
    r8 = lax.broadcasted_iota(jnp.int32, (8, MIX_TM), 0)
    r8_blk = lax.broadcasted_iota(jnp.int32, (8, SUB), 0)
    before_b = (t_idx < s_idx).astype(F32).astype(BF16)

    def phase3(i, carry):
        r0 = pl.multiple_of(i * MIX_TM, MIX_TM)
        rows = pl.ds(r0, MIX_TM)
        hm = hm_s[rows, :]
        heads = []
        for hd in range(N_HEADS):
            hh = hm[:, hd * HEAD_DIM:(hd + 1) * HEAD_DIM]
            heads.append(hh * lax.rsqrt(jnp.mean(hh * hh, axis=-1, keepdims=True) + EPS))
        hn = jnp.concatenate(heads, axis=1) * w["hng"][...]
        hb2 = (so_s[rows, :] * hn).astype(BF16)
        br_b = _dot(hb2, w["wmo"][...])
        mixed = (ma_s[rows, :] + sgb_s[rows, :] * br_b).astype(BF16)
        x1 = x_ref[0, rows, :] + mod_row(2) * _dot(mixed, w["wo"][...])
        x1_ref[0, rows, :] = x1
        xn = x1 * lax.rsqrt(jnp.mean(x1 * x1, axis=-1, keepdims=True) + EPS) * w["g2"][...]
        h2 = xn * (1.0 + mod_row(4)) + mod_row(3)
        h2_ref[0, rows, :] = h2.astype(BF16)

        h2_hi = h2.astype(BF16)
        h2_lo = (h2 - h2_hi.astype(F32)).astype(BF16)
        lg = _dot(h2_hi, w["wrt2"][...])
        lg = lg[:, :LANES] + lg[:, LANES:] + _dot(h2_lo, w["wrt2"][:, :LANES])
        lt = lg.T + w["brtT"][...]
        gl = [lt[N_EXPERTS + g:N_EXPERTS + g + 1, :] for g in range(N_GROUPS)]
        best, gsel = gl[0], jnp.zeros((1, MIX_TM), jnp.int32)
        for g in range(1, N_GROUPS):
            better = gl[g] > best
            gsel = jnp.where(better, g, gsel)
            best = jnp.where(better, gl[g], best)
        gp_sel = 1.0 / sum(jnp.exp(v - best) for v in gl)
        el = []
        for j in range(EXPERTS_PER_GROUP):
            v = lt[j:j + 1, :]
            for g in range(1, N_GROUPS):
                r = g * EXPERTS_PER_GROUP + j
                v = jnp.where(gsel == g, lt[r:r + 1, :], v)
            el.append(v)
        l1, e1 = el[0], jnp.zeros((1, MIX_TM), jnp.int32)
        for j in range(1, EXPERTS_PER_GROUP):
            better = el[j] > l1
            e1 = jnp.where(better, j, e1)
            l1 = jnp.where(better, el[j], l1)
        l2 = jnp.full((1, MIX_TM), -jnp.inf, F32)
        e2 = jnp.zeros((1, MIX_TM), jnp.int32)
        for j in range(EXPERTS_PER_GROUP):
            better = jnp.logical_and(e1 != j, el[j] > l2)
            e2 = jnp.where(better, j, e2)
            l2 = jnp.where(better, el[j], l2)
        r2 = jnp.exp(l2 - l1)
        wt1 = gp_sel / (1.0 + r2)
        wt2 = gp_sel * r2 / (1.0 + r2)
        in_group = g_of_e == gsel
        comb_t = (jnp.where(jnp.logical_and(in_group, j_of_e == e1), wt1, 0.0)
                  + jnp.where(jnp.logical_and(in_group, j_of_e == e2), wt2, 0.0))

        onehot = (r8 == gsel).astype(F32)
        gsel_f = gsel.astype(F32)
        ranks = []
        for j in range(cpm):
            oh = onehot[:, j * SUB:(j + 1) * SUB]
            rank = jnp.sum(oh * _dot(oh.astype(BF16), before_b), axis=0, keepdims=True)
            ranks.append(rank)
            r8rows = pl.ds(pl.multiple_of((i * cpm + j) * 8, 8), 8)
            route_ref[0, r8rows, :] = jnp.where(r8_blk == 0, gsel_f[:, j * SUB:(j + 1) * SUB],
                                                jnp.where(r8_blk == 1, rank, 0.0))
            cnt_ref[0, r8rows, :] = jnp.broadcast_to(jnp.sum(oh, axis=1, keepdims=True), (8, LANES))
        comb_t = jnp.where(e_iota == ROUTE_GROUP_LANE, gsel_f,
                           jnp.where(e_iota == ROUTE_RANK_LANE, jnp.concatenate(ranks, axis=1), comb_t))
        comb_ref[0, rows, :] = comb_t.T
        return carry

    if n_mt == 1:
        phase3(0, 0)
    else:
        lax.fori_loop(0, n_mt, phase3, 0)


def _const_spec(a):
    nd = a.ndim
    return pl.BlockSpec(a.shape, lambda b, _nd=nd: (0,) * _nd, pipeline_mode=pl.Buffered(1))


def _mixer(x, T, mod, mod_index, weights, P, state=None, emit_state=False):
    B, R, _ = x.shape
    n_chunks = R // SUB
    n_seq = R // T
    has_state = state is not None
    seq_mode = {} if R <= MIX_TM else {"pipeline_mode": pl.Buffered(1)}
    in_specs = [
        pl.BlockSpec((1, R, D_MODEL), lambda b: (b, 0, 0), **seq_mode),
        pl.BlockSpec((1, N_ADA, D_MODEL), lambda b: (mod_index(b), 0, 0)),
    ]
    args = [x, mod]
    if has_state:
        caug0, m0 = state
        in_specs += [
            pl.BlockSpec((1, N_UNITS, HEAD_DIM, 2 * HEAD_DIM), lambda b: (b, 0, 0, 0)),
            pl.BlockSpec((1, N_UNITS, LANES), lambda b: (b, 0, 0)),
        ]
        args += [caug0, m0]
    for name in _MIXER_WEIGHTS:
        in_specs.append(_const_spec(weights[name]))
        args.append(weights[name])
    out_shape = [
        jax.ShapeDtypeStruct((B, R, D_MODEL), F32),
        jax.ShapeDtypeStruct((B, R, D_MODEL), BF16),
        jax.ShapeDtypeStruct((B, R, LANES), F32),
        jax.ShapeDtypeStruct((B, n_chunks * 8, SUB), F32),
        jax.ShapeDtypeStruct((B, n_chunks * 8, LANES), F32),
    ]
    out_specs = [
        pl.BlockSpec((1, R, D_MODEL), lambda b: (b, 0, 0), **seq_mode),
        pl.BlockSpec((1, R, D_MODEL), lambda b: (b, 0, 0), **seq_mode),
        pl.BlockSpec((1, R, LANES), lambda b: (b, 0, 0)),
        pl.BlockSpec((1, n_chunks * 8, SUB), lambda b: (b, 0, 0)),
        pl.BlockSpec((1, n_chunks * 8, LANES), lambda b: (b, 0, 0)),
    ]
    if emit_state:
        out_shape += [
            jax.ShapeDtypeStruct((B, n_seq * N_UNITS, HEAD_DIM, HEAD_DIM), F32),
            jax.ShapeDtypeStruct((B, n_seq * N_UNITS, HEAD_DIM), F32),
            jax.ShapeDtypeStruct((B, n_seq * N_UNITS, LANES), F32),
        ]
        out_specs += [
            pl.BlockSpec((1, n_seq * N_UNITS, HEAD_DIM, HEAD_DIM), lambda b: (b, 0, 0, 0)),
            pl.BlockSpec((1, n_seq * N_UNITS, HEAD_DIM), lambda b: (b, 0, 0)),
            pl.BlockSpec((1, n_seq * N_UNITS, LANES), lambda b: (b, 0, 0)),
        ]
    scratch = [
        pltpu.VMEM((R, D_MLSTM), BF16),
        pltpu.VMEM((n_chunks, D_MLSTM, SUB), BF16),
        pltpu.VMEM((R, D_MLSTM), BF16),
        pltpu.VMEM((R, D_MLSTM), F32),
        pltpu.VMEM((n_chunks, 5 * N_UNITS, SUB), F32),
        pltpu.VMEM((R, D_MODEL), F32),
        pltpu.VMEM((R, D_MODEL), F32),
        pltpu.VMEM((R, D_MLSTM), F32),
        pltpu.VMEM((N_UNITS, HEAD_DIM, 2 * HEAD_DIM), F32),
        pltpu.VMEM((MIX_TM // P, P + 2 * CONV_PAD, D_CONV), F32),
    ]
    return pl.pallas_call(
        functools.partial(_mixer_kernel, R, T, P, has_state, emit_state),
        grid=(B,),
        in_specs=in_specs,
        out_specs=out_specs,
        out_shape=out_shape,
        scratch_shapes=scratch,
        compiler_params=pltpu.CompilerParams(
            dimension_semantics=("arbitrary",), vmem_limit_bytes=VMEM_LIMIT),
        name="mixer_T%d" % T,
    )(*args)


def _dest_in_block(group, rank, starts):
    dest = rank
    for g in range(N_GROUPS):
        dest = dest + jnp.where(group == float(g), starts[g], 0.0)
    return dest


def _copy_segments(src_refs, dst_refs, src_starts, dst_starts, n_pieces):
    for g in range(N_GROUPS):
        def body(k, carry, g=g):
            s = pl.multiple_of(src_starts[g] + k * ROW_ALIGN, ROW_ALIGN)
            d = pl.multiple_of(dst_starts[g] + k * ROW_ALIGN, ROW_ALIGN)
            for src, dst in zip(src_refs, dst_refs):
                dst[pl.ds(d, ROW_ALIGN), :] = src[pl.ds(s, ROW_ALIGN), :]
            return carry
        lax.fori_loop(0, n_pieces[g], body, 0)


def _dispatch_kernel(n_ctx_blocks, start_ref, npiece_ref, off_ref,
                     h2c_ref, h2l_ref, cbc_ref, cbl_ref, rtc_ref, rtl_ref,
                     xs_ref, cs_ref, sx_s, sc_s):
    b = pl.program_id(0)
    is_ctx = b < n_ctx_blocks

    @pl.when(b == 0)
    def _():
        xs_ref[...] = jnp.zeros_like(xs_ref)
        cs_ref[...] = jnp.zeros_like(cs_ref)

    h2 = jnp.where(is_ctx, h2c_ref[0], h2l_ref[0])
    cb = jnp.where(is_ctx, cbc_ref[0], cbl_ref[0])
    rt = jnp.where(is_ctx, rtc_ref[0], rtl_ref[0])
    starts = [start_ref[b * N_GROUPS + g] for g in range(N_GROUPS)]
    dest = _dest_in_block(rt[0:1, :], rt[1:2, :], [s.astype(F32) for s in starts])
    row = lax.broadcasted_iota(jnp.int32, (SORT_ROWS, SUB), 0).astype(F32)
    perm = (row == dest).astype(F32).astype(BF16)
    cb_hi = cb.astype(BF16)
    cb_lo = (cb - cb_hi.astype(F32)).astype(BF16)
    sx_s[...] = _dot(perm, h2).astype(BF16)
    sc_s[...] = _dot(perm, jnp.concatenate([cb_hi, cb_lo], axis=1)).astype(BF16)
    _copy_segments((sx_s, sc_s), (xs_ref, cs_ref), starts,
                   [off_ref[b * N_GROUPS + g] for g in range(N_GROUPS)],
                   [npiece_ref[b * N_GROUPS + g] for g in range(N_GROUPS)])


def _experts_kernel(tgroup_ref, tvalid_ref, tfirst_ref, xs_ref, cs_ref, wg_ref, wu_ref, wd_ref, ys_ref,
                    wg_s, wu_s, wd_s):
    i = pl.program_id(0)

    @pl.when(tfirst_ref[i] == 1)
    def _():
        for j in range(EXPERTS_PER_GROUP):
            cols = slice(j * D_EXPERT, (j + 1) * D_EXPERT)
            wg_s[:, cols] = wg_ref[j].astype(BF16)
            wu_s[:, cols] = wu_ref[j].astype(BF16)
            wd_s[cols, :] = wd_ref[j].astype(BF16)

    @pl.when(tvalid_ref[i] == 1)
    def _():
        x = xs_ref[...]
        g = _dot(x, wg_s[...])
        u = _dot(x, wu_s[...])
        comb = cs_ref[:, :LANES].astype(F32) + cs_ref[:, LANES:].astype(F32)
        lane = lax.broadcasted_iota(jnp.int32, comb.shape, 1)
        first = tgroup_ref[i] * EXPERTS_PER_GROUP
        parts = []
        for j in range(EXPERTS_PER_GROUP):
            cols = slice(j * D_EXPERT, (j + 1) * D_EXPERT)
            cw = jnp.sum(jnp.where(lane == first + j, comb, 0.0), axis=1, keepdims=True)
            gj = g[:, cols]
            parts.append((gj * _sigmoid(gj) * u[:, cols] * cw).astype(BF16))
        ys_ref[...] = _dot(jnp.concatenate(parts, axis=1), wd_s[...]).astype(BF16)

    @pl.when(tvalid_ref[i] == 0)
    def _():
        ys_ref[...] = jnp.zeros_like(ys_ref)


def _combine_kernel(n_ctx_blocks, blocks_per_lat_seq, start_ref, npiece_ref, off_ref,
                    x1c_ref, x1l_ref, cbc_ref, cbl_ref, ys_ref, mod_ref, gf_ref, yc_ref, yl_ref, loc_s):
    b = pl.program_id(0)
    is_ctx = b < n_ctx_blocks
    starts = [start_ref[b * N_GROUPS + g] for g in range(N_GROUPS)]
    loc_s[...] = jnp.zeros_like(loc_s)
    _copy_segments((ys_ref,), (loc_s,), [off_ref[b * N_GROUPS + g] for g in range(N_GROUPS)], starts,
                   [npiece_ref[b * N_GROUPS + g] for g in range(N_GROUPS)])
    cb = jnp.where(is_ctx, cbc_ref[0], cbl_ref[0])
    dest = _dest_in_block(cb[:, ROUTE_GROUP_LANE:ROUTE_GROUP_LANE + 1],
                          cb[:, ROUTE_RANK_LANE:ROUTE_RANK_LANE + 1],
                          [s.astype(F32) for s in starts])
    col = lax.broadcasted_iota(jnp.int32, (SUB, SORT_ROWS), 1).astype(F32)
    unperm = (col == dest).astype(F32).astype(BF16)
    moe = _dot(unperm, loc_s[...])
    x1 = jnp.where(is_ctx, x1c_ref[0], x1l_ref[0])
    mrow = jnp.where(is_ctx, 0, 1 + jnp.maximum(b - n_ctx_blocks, 0) // blocks_per_lat_seq)
    x2 = x1 + mod_ref[mrow, N_ADA - 1:N_ADA, :] * moe
    y = x2 * lax.rsqrt(jnp.mean(x2 * x2, axis=-1, keepdims=True) + EPS) * gf_ref[...]

    @pl.when(is_ctx)
    def _():
        yc_ref[0] = y

    @pl.when(jnp.logical_not(is_ctx))
    def _():
        yl_ref[0] = y


def _moe_plan(cnt, n_tiles):
    cnt_al = (cnt + ROW_ALIGN - 1) // ROW_ALIGN * ROW_ALIGN
    start = jnp.cumsum(cnt_al, axis=1) - cnt_al
    gpad = (jnp.sum(cnt_al, axis=0) + MOE_TM - 1) // MOE_TM * MOE_TM
    gbase = jnp.cumsum(gpad) - gpad
    off = gbase[None, :] + jnp.cumsum(cnt_al, axis=0) - cnt_al
    tile_end = jnp.cumsum(gpad // MOE_TM)
    t = jnp.arange(n_tiles, dtype=jnp.int32)
    tgroup = jnp.sum((t[:, None] >= tile_end[None, :]).astype(jnp.int32), axis=1)
    valid = t < tile_end[-1]
    last_group = jnp.sum((tile_end[-1] - 1 >= tile_end).astype(jnp.int32))
    tgroup = jnp.where(valid, tgroup, last_group)
    first = jnp.logical_and(valid, jnp.concatenate([jnp.ones((1,), bool), tgroup[1:] != tgroup[:-1]]))
    i32 = lambda a: a.astype(jnp.int32).reshape(-1)
    return i32(start), i32(cnt_al // ROW_ALIGN), i32(off), i32(tgroup), i32(valid), i32(first)


def _moe(x1c, x1l, h2c, h2l, cbc, cbl, rtc, rtl, cnt, mod, blocks_per_lat_seq, wg, wu, wd, gf):
    nc, nl = x1c.shape[0], x1l.shape[0]
    nb = nc + nl
    n_rows_max = nb * SUB + nb * N_GROUPS * (ROW_ALIGN - 1) + N_GROUPS * (MOE_TM - ROW_ALIGN)
    n_tiles = -(-n_rows_max // MOE_TM)
    ns = n_tiles * MOE_TM
    start, npiece, off, tgroup, tvalid, tfirst = _moe_plan(cnt, n_tiles)

    cmap = lambda b, *_: (jnp.minimum(b, nc - 1), 0, 0)
    lmap = lambda b, *_: (jnp.maximum(b - nc, 0), 0, 0)
    whole = lambda *_: (0, 0)
    once = {"pipeline_mode": pl.Buffered(1)}
    arb = pltpu.CompilerParams(dimension_semantics=("arbitrary",), vmem_limit_bytes=VMEM_LIMIT)

    xs, cs = pl.pallas_call(
        functools.partial(_dispatch_kernel, nc),
        grid_spec=pltpu.PrefetchScalarGridSpec(
            num_scalar_prefetch=3, grid=(nb,),
            in_specs=[
                pl.BlockSpec((1, SUB, D_MODEL), cmap), pl.BlockSpec((1, SUB, D_MODEL), lmap),
                pl.BlockSpec((1, SUB, LANES), cmap), pl.BlockSpec((1, SUB, LANES), lmap),
                pl.BlockSpec((1, 8, SUB), cmap), pl.BlockSpec((1, 8, SUB), lmap),
            ],
            out_specs=[pl.BlockSpec((ns, D_MODEL), whole, **once), pl.BlockSpec((ns, 2 * LANES), whole, **once)],
            scratch_shapes=[pltpu.VMEM((SORT_ROWS, D_MODEL), BF16), pltpu.VMEM((SORT_ROWS, 2 * LANES), BF16)],
        ),
        out_shape=[jax.ShapeDtypeStruct((ns, D_MODEL), BF16), jax.ShapeDtypeStruct((ns, 2 * LANES), BF16)],
        compiler_params=arb,
        name="moe_dispatch",
    )(start, npiece, off, h2c, h2l, cbc, cbl, rtc, rtl)

    wmap = lambda i, tg, tv, tf: (tg[i], 0, 0)
    ys = pl.pallas_call(
        _experts_kernel,
        grid_spec=pltpu.PrefetchScalarGridSpec(
            num_scalar_prefetch=3, grid=(n_tiles,),
            in_specs=[
                pl.BlockSpec((MOE_TM, D_MODEL), lambda i, *_: (i, 0)),
                pl.BlockSpec((MOE_TM, 2 * LANES), lambda i, *_: (i, 0)),
                pl.BlockSpec((EXPERTS_PER_GROUP, D_MODEL, D_EXPERT), wmap),
                pl.BlockSpec((EXPERTS_PER_GROUP, D_MODEL, D_EXPERT), wmap),
                pl.BlockSpec((EXPERTS_PER_GROUP, D_EXPERT, D_MODEL), wmap),
            ],
            out_specs=pl.BlockSpec((MOE_TM, D_MODEL), lambda i, *_: (i, 0)),
            scratch_shapes=[pltpu.VMEM((D_MODEL, EXPERTS_PER_GROUP * D_EXPERT), BF16),
                            pltpu.VMEM((D_MODEL, EXPERTS_PER_GROUP * D_EXPERT), BF16),
                            pltpu.VMEM((EXPERTS_PER_GROUP * D_EXPERT, D_MODEL), BF16)],
        ),
        out_shape=jax.ShapeDtypeStruct((ns, D_MODEL), BF16),
        compiler_params=arb,
        name="moe_experts",
    )(tgroup, tvalid, tfirst, xs, cs, wg, wu, wd)

    yc, yl = pl.pallas_call(
        functools.partial(_combine_kernel, nc, blocks_per_lat_seq),
        grid_spec=pltpu.PrefetchScalarGridSpec(
            num_scalar_prefetch=3, grid=(nb,),
            in_specs=[
                pl.BlockSpec((1, SUB, D_MODEL), cmap), pl.BlockSpec((1, SUB, D_MODEL), lmap),
                pl.BlockSpec((1, SUB, LANES), cmap), pl.BlockSpec((1, SUB, LANES), lmap),
                pl.BlockSpec((ns, D_MODEL), whole, **once),
                pl.BlockSpec(mod.shape, lambda *_: (0, 0, 0)),
                pl.BlockSpec((1, D_MODEL), whole),
            ],
            out_specs=[pl.BlockSpec((1, SUB, D_MODEL), cmap), pl.BlockSpec((1, SUB, D_MODEL), lmap)],
            scratch_shapes=[pltpu.VMEM((SORT_ROWS, D_MODEL), BF16)],
        ),
        out_shape=[jax.ShapeDtypeStruct((nc, SUB, D_MODEL), F32), jax.ShapeDtypeStruct((nl, SUB, D_MODEL), F32)],
        compiler_params=arb,
        name="moe_combine",
    )(start, npiece, off, x1c, x1l, cbc, cbl, ys, mod, gf)
    return yc, yl


def _prep_weights(norm1_g, w_in, b_in, b_gates, w_dw, b_dw, conv_ln_g, conv_ln_b, w_conv_out,
                  mlstm_hn_g, w_mlstm_out, w_o, norm2_g, w_rg, b_rg, w_re, b_re):
    s_a = 2 * D_CONV
    s_q = s_a + D_MLSTM
    s_k = s_q + D_MLSTM
    s_v = s_k + D_MLSTM
    s_o = s_v + D_MLSTM
    s_g = s_o + 4 * N_HEADS
    row = lambda v: v.reshape(1, -1).astype(F32)
    wb = w_in.astype(BF16)
    gperm = [d * 2 * N_HEADS + gate * N_HEADS + hd
             for gate in range(2) for d in range(2) for hd in range(N_HEADS)]
    gperm = jnp.array(gperm, jnp.int32)
    bg = (b_in[s_o:s_g] + b_gates.reshape(-1))[gperm]
    wrt = jnp.concatenate([w_re, w_rg], axis=1)
    wrt = jnp.zeros((D_MODEL, LANES), F32).at[:, :N_EXPERTS + N_GROUPS].set(wrt)
    wrt_hi = wrt.astype(BF16)
    wrt2 = jnp.concatenate([wrt_hi, (wrt - wrt_hi.astype(F32)).astype(BF16)], axis=1)
    brtT = jnp.zeros((LANES, 1), F32).at[:N_EXPERTS + N_GROUPS, 0].set(jnp.concatenate([b_re, b_rg]))
    return {
        "g1": row(norm1_g),
        "wag": wb[:, :s_a], "bag": row(b_in[:s_a]),
        "wq": wb[:, s_a:s_q], "bq": row(b_in[s_a:s_q]),
        "wkT": wb[:, s_q:s_k].T, "bk": b_in[s_q:s_k].reshape(-1, 1),
        "wv": wb[:, s_k:s_v], "bv": row(b_in[s_k:s_v]),
        "wog": wb[:, s_v:s_o], "bog": row(b_in[s_v:s_o]),
        "wgifT": wb[:, s_o:s_g].T[gperm], "bgifT": bg.reshape(-1, 1),
        "wgm": wb[:, s_g:], "bgm": row(b_in[s_g:]),
        "wdw": w_dw.astype(F32), "bdw": row(b_dw), "lng": row(conv_ln_g), "lnb": row(conv_ln_b),
        "wco": w_conv_out.astype(BF16), "hng": row(mlstm_hn_g), "wmo": w_mlstm_out.astype(BF16),
        "wo": w_o.astype(BF16), "g2": row(norm2_g), "wrt2": wrt2, "brtT": brtT,
    }


def kernel(x_prompt, x_sample, state_C, state_n, state_m, c, c_ctx, norm1_g, w_ada, b_ada, w_in, b_in, b_gates, w_dw, b_dw, conv_ln_g, conv_ln_b, w_conv_out, mlstm_hn_g, w_mlstm_out, w_o, norm2_g, w_rg, b_rg, w_re, b_re, w_e_gate, w_e_up, w_e_down, norm_final_g):
    B, S, _ = x_prompt.shape
    Bd, Sd, _ = x_sample.shape
    assert w_ada.shape[0] == 1, "single trunk layer"
    assert S == SUB and Sd % SUB == 0

    cin = jnp.zeros((8, D_MODEL), F32).at[0].set(c_ctx).at[1:1 + Bd].set(c)
    mod = _ada(cin, w_ada[0], b_ada[0].reshape(1, -1)).reshape(8, N_ADA, D_MODEL)

    wts = _prep_weights(norm1_g[0], w_in[0], b_in[0], b_gates[0], w_dw[0], b_dw[0], conv_ln_g[0],
                        conv_ln_b[0], w_conv_out[0], mlstm_hn_g[0], w_mlstm_out[0], w_o[0],
                        norm2_g[0], w_rg[0], b_rg[0], w_re[0], b_re[0])

    x1p, h2p, cbp, rtp, cntp, c_new, n_new, m_new = _mixer(
        x_prompt.reshape(B * S // MIX_TM, MIX_TM, D_MODEL), S, mod, lambda b: 0, wts, P=S, emit_state=True)

    sc = state_C[:, 0].reshape(Bd, N_UNITS, HEAD_DIM, HEAD_DIM)
    sn = state_n[:, 0].reshape(Bd, N_UNITS, HEAD_DIM, 1)
    caug0 = jnp.concatenate([sc, sn, jnp.zeros((Bd, N_UNITS, HEAD_DIM, HEAD_DIM - 1), F32)], axis=-1)
    m0 = jnp.broadcast_to(state_m[:, 0].reshape(Bd, N_UNITS, 1), (Bd, N_UNITS, LANES))
    x1s, h2s, cbs, rts, cnts = _mixer(x_sample, Sd, mod, lambda b: 1 + b, wts, P=GRID_W, state=(caug0, m0))

    nc, nl = B * S // SUB, Bd * Sd // SUB
    blk = lambda a, n: a.reshape(n, SUB, a.shape[-1])
    cnt = jnp.concatenate([cntp.reshape(nc, 8, LANES)[:, :N_GROUPS, 0],
                           cnts.reshape(nl, 8, LANES)[:, :N_GROUPS, 0]], axis=0)
    yp, ys = _moe(blk(x1p, nc), blk(x1s, nl), blk(h2p, nc), blk(h2s, nl), blk(cbp, nc), blk(cbs, nl),
                  rtp.reshape(nc, 8, SUB), rts.reshape(nl, 8, SUB),
                  cnt.astype(jnp.int32), mod, Sd // SUB, w_e_gate[0], w_e_up[0], w_e_down[0],
                  norm_final_g.reshape(1, -1))

    return (yp.reshape(B, S, D_MODEL), ys.reshape(Bd, Sd, D_MODEL),
            c_new.reshape(B, 1, 2, N_HEADS, HEAD_DIM, HEAD_DIM),
            n_new.reshape(B, 1, 2, N_HEADS, HEAD_DIM),
            m_new[:, :, 0].reshape(B, 1, 2, N_HEADS))
```

```python
import functools

import jax
import jax.numpy as jnp
from jax import lax
from jax.experimental import pallas as pl
from jax.experimental.pallas import tpu as pltpu

D_MODEL = 1024
D_CONV = 512
CONV_K = 31
D_MLSTM = 512
N_HEADS = 4
HEAD_DIM = D_MLSTM // N_HEADS
N_GROUPS = 4
EXPERTS_PER_GROUP = 4
N_EXPERTS = N_GROUPS * EXPERTS_PER_GROUP
D_EXPERT = 256
N_ADA = 6
EPS = 1e-6
GRID_W = 64

LANES = 128
SUB = 256
CONV_PAD = 16
CONV_RB = 64
N_UNITS = 2 * N_HEADS
ROW_ALIGN = 16
SORT_ROWS = SUB + N_GROUPS * ROW_ALIGN
MOE_TM = 512
MIX_TM = 512
ROUTE_GROUP_LANE = N_EXPERTS
ROUTE_RANK_LANE = N_EXPERTS + 1
VMEM_LIMIT = 58 * 1024 * 1024

BF16 = jnp.bfloat16
F32 = jnp.float32
NT_DIMS = (((1,), (1,)), ((), ()))


def _dot(a, b):
    return jnp.dot(a, b, preferred_element_type=F32)


def _dot_nt(a, b, precision=None):
    return lax.dot_general(a, b, NT_DIMS, preferred_element_type=F32, precision=precision)


def _sigmoid(x):
    return 0.5 * jnp.tanh(0.5 * x) + 0.5


def _log_sigmoid(x):
    return jnp.minimum(x, 0.0) - jnp.log1p(jnp.exp(-jnp.abs(x)))


def _split3(x):
    hi = x.astype(BF16).astype(F32)
    r1 = x - hi
    mid = r1.astype(BF16).astype(F32)
    lo = (r1 - mid).astype(BF16).astype(F32)
    return hi, mid, lo


def _ada_kernel(c_ref, w_ref, b_ref, o_ref):
    c = c_ref[...]
    s = (c * _sigmoid(c)).astype(BF16)
    o_ref[...] = _dot(s, w_ref[...].astype(BF16)) + b_ref[...]


def _ada(cin, w_ada, b_ada):
    n = w_ada.shape[1]
    bn = 1024
    return pl.pallas_call(
        _ada_kernel,
        grid=(n // bn,),
        in_specs=[
            pl.BlockSpec((8, D_MODEL), lambda j: (0, 0)),
            pl.BlockSpec((D_MODEL, bn), lambda j: (0, j)),
            pl.BlockSpec((1, bn), lambda j: (0, j)),
        ],
        out_specs=pl.BlockSpec((8, bn), lambda j: (0, j)),
        out_shape=jax.ShapeDtypeStruct((8, n), F32),
        compiler_params=pltpu.CompilerParams(dimension_semantics=("arbitrary",)),
        name="ada",
    )(cin, w_ada, b_ada)


_MIXER_WEIGHTS = (
    "g1", "wag", "bag", "wq", "bq", "wkT", "bk", "wv", "bv", "wog", "bog",
    "wgifT", "bgifT", "wgm", "bgm", "wdw", "bdw", "lng", "lnb",
    "wco", "hng", "wmo", "wo", "g2", "wrt2", "brtT",
)


def _conv_block(upad_s, seg, base, cs, wdw_ref, bdw_ref):
    sub = 8
    first = CONV_PAD - CONV_K // 2
    acc = jnp.broadcast_to(bdw_ref[0:1, cs], (CONV_RB, LANES))
    for r in range(sub):
        z = None
        for a in range((CONV_K + first + sub - 1) // sub):
            j = sub * a + r - first
            if 0 <= j < CONV_K:
                lo = base + sub * a
                term = wdw_ref[j:j + 1, cs] * upad_s[seg, lo:lo + CONV_RB + sub, cs]
                z = term if z is None else z + term
        acc = acc + z[r:r + CONV_RB, :]
    return acc


def _mixer_kernel(R, T, P, has_state, emit_state, *refs):
    L = SUB
    n_mt = R // MIX_TM
    cpm = MIX_TM // L
    n_seq = R // T
    cps = T // L
    nseg = MIX_TM // P
    assert not has_state or n_seq == 1
    it = iter(refs)
    x_ref = next(it)
    mod_ref = next(it)
    if has_state:
        c0_ref = next(it)
        m0_ref = next(it)
    w = {name: next(it) for name in _MIXER_WEIGHTS}
    x1_ref = next(it)
    h2_ref = next(it)
    comb_ref = next(it)
    route_ref = next(it)
    cnt_ref = next(it)
    if emit_state:
        cout_ref = next(it)
        nout_ref = next(it)
        mout_ref = next(it)
    (q_s, kT_s, v_s, so_s, scan_s, ma_s, sgb_s, hm_s, cst_s, upad_s) = [next(it) for _ in range(10)]

    def mod_row(i):
        return mod_ref[0, i:i + 1, :]

    zpad = jnp.zeros((CONV_PAD, D_CONV), F32)
    for seg in range(nseg):
        upad_s[seg, 0:CONV_PAD, :] = zpad
        upad_s[seg, CONV_PAD + P:CONV_PAD + P + CONV_PAD, :] = zpad

    t_idx = lax.broadcasted_iota(jnp.int32, (L, L), 0)
    s_idx = lax.broadcasted_iota(jnp.int32, (L, L), 1)
    lower = s_idx <= t_idx
    upper = s_idx >= t_idx
    triu_b = upper.astype(F32).astype(BF16)
    lane_u = lax.broadcasted_iota(jnp.int32, (N_UNITS, L), 1)
    is_bwd = lax.broadcasted_iota(jnp.int32, (N_UNITS, L), 0) >= N_HEADS

    def gate_scan(g):
        gi, lf = g[:N_UNITS], _log_sigmoid(g[N_UNITS:])
        pr = _dot(jnp.concatenate(_split3(lf), axis=0).astype(BF16), triu_b)
        pre = pr[0:N_UNITS] + pr[N_UNITS:2 * N_UNITS] + pr[2 * N_UNITS:]
        tot = pre[:, L - 1:L]
        bsum = jnp.where(is_bwd, tot - pre + lf, pre)
        a = gi - bsum
        pm, sm, k = a, a, 1
        while k < L:
            pm = jnp.where(lane_u >= k, jnp.maximum(pm, pltpu.roll(pm, k, axis=1)), pm)
            sm = jnp.where(lane_u < L - k, jnp.maximum(sm, pltpu.roll(sm, L - k, axis=1)), sm)
            k *= 2
        wide = lambda v: jnp.broadcast_to(v, (N_UNITS, L))
        return jnp.concatenate([a, jnp.where(is_bwd, sm, pm), bsum, wide(tot),
                                wide(jnp.max(a, axis=1, keepdims=True))], axis=0)

    def phase1(i, carry):
        r0 = pl.multiple_of(i * MIX_TM, MIX_TM)
        rows = pl.ds(r0, MIX_TM)
        x = x_ref[0, rows, :]
        xn = x * lax.rsqrt(jnp.mean(x * x, axis=-1, keepdims=True) + EPS) * w["g1"][...]
        hb = (xn * (1.0 + mod_row(1)) + mod_row(0)).astype(BF16)

        gates = _dot_nt(w["wgifT"][...], hb) + w["bgifT"][...]
        for j in range(cpm):
            scan_s[i * cpm + j] = gate_scan(gates[:, j * L:(j + 1) * L])
        ag = _dot(hb, w["wag"][...]) + w["bag"][...]
        u = ag[:, :D_CONV] * _sigmoid(ag[:, D_CONV:])
        for seg in range(nseg):
            upad_s[seg, CONV_PAD:CONV_PAD + P, :] = u[seg * P:(seg + 1) * P, :]

        def proj(name, bias, c0, width=2 * LANES):
            cs = slice(c0, c0 + width)
            return _dot(hb, w[name][:, cs]) + w[bias][:, cs]

        def gm_a(c0):
            ma_s[rows, c0:c0 + 2 * LANES] = _sigmoid(proj("wgm", "bgm", c0))

        def gm_b(c0):
            sgb_s[rows, c0:c0 + 2 * LANES] = _sigmoid(proj("wgm", "bgm", D_MODEL + c0))

        def q_part(c0):
            q_s[rows, c0:c0 + 2 * LANES] = (proj("wq", "bq", c0) * (HEAD_DIM ** -0.5)).astype(BF16)

        def v_part(c0):
            v_s[rows, c0:c0 + 2 * LANES] = proj("wv", "bv", c0).astype(BF16)

        def o_part(c0):
            so_s[rows, c0:c0 + 2 * LANES] = _sigmoid(proj("wog", "bog", c0))

        def k_part(c0):
            rs = slice(c0, c0 + 2 * LANES)
            kt = (_dot_nt(w["wkT"][rs, :], hb) + w["bk"][rs, :]).astype(BF16)
            for j in range(cpm):
                kT_s[i * cpm + j, rs, :] = kt[:, j * L:(j + 1) * L]

        jobs = ([functools.partial(gm_a, c0) for c0 in range(0, D_MODEL, 2 * LANES)]
                + [functools.partial(gm_b, c0) for c0 in range(0, D_MODEL, 2 * LANES)]
                + [functools.partial(f, c0) for f in (q_part, k_part, v_part, o_part)
                   for c0 in range(0, D_MLSTM, 2 * LANES)])
        n_jobs = len(jobs)
        conv = {}
        n_pieces = (D_CONV // LANES) * nseg * (P // CONV_RB)
        for cb in range(D_CONV // LANES):
            cs = slice(cb * LANES, (cb + 1) * LANES)
            for seg in range(nseg):
                for rb in range(P // CONV_RB):
                    conv[(cb, seg, rb)] = _conv_block(upad_s, seg, rb * CONV_RB, cs, w["wdw"], w["bdw"])
                    if jobs and len(conv) * n_jobs >= (n_jobs - len(jobs) + 1) * n_pieces:
                        jobs.pop(0)()
        for job in jobs:
            job()
        cu = jnp.concatenate(
            [jnp.concatenate([conv[(cb, seg, rb)] for seg in range(nseg) for rb in range(P // CONV_RB)], axis=0)
             for cb in range(D_CONV // LANES)], axis=1)
        mu = jnp.mean(cu, axis=-1, keepdims=True)
        cc = cu - mu
        cn = cc * lax.rsqrt(jnp.mean(cc * cc, axis=-1, keepdims=True) + EPS) * w["lng"][...] + w["lnb"][...]
        ca = (cn * _sigmoid(cn)).astype(BF16)
        ma_s[rows, :] = ma_s[rows, :] * _dot(ca, w["wco"][...])
        return carry

    if n_mt == 1:
        phase1(0, 0)
    else:
        lax.fori_loop(0, n_mt, phase1, 0)

    ones_col = (lax.broadcasted_iota(jnp.int32, (L, HEAD_DIM), 1) == 0).astype(F32).astype(BF16)
    pad_rows = jnp.zeros((LANES - 3 * N_UNITS, L), F32)

    def gate_prep(c, m_vec):
        sc = scan_s[c]
        a, run_max, bsum = sc[0:N_UNITS], sc[N_UNITS:2 * N_UNITS], sc[2 * N_UNITS:3 * N_UNITS]
        tot, a_max = sc[3 * N_UNITS:4 * N_UNITS, 0:1], sc[4 * N_UNITS:5 * N_UNITS, 0:1]
        big_m = jnp.maximum(m_vec, run_max)
        m_end = jnp.maximum(m_vec, a_max)
        cols = jnp.concatenate(
            [big_m, jnp.exp(m_vec - big_m), jnp.exp(-bsum - big_m), pad_rows], axis=0).T
        return a, cols, jnp.exp(a - m_end), jnp.exp(m_vec - m_end), tot + m_end

    qk_cache = {}

    def unit(d, hd, c, prep, first_chunk, want_state):
        a, cols, wk, decay, _ = prep
        rows = slice(c * L, (c + 1) * L)
        hs = slice(hd * HEAD_DIM, (hd + 1) * HEAD_DIM)
        idx = d * N_HEADS + hd
        col = lambda k: cols[:, k * N_UNITS + idx:k * N_UNITS + idx + 1]
        qc = q_s[rows, hs]
        kTc = kT_s[c, hs, :]
        vaug = jnp.concatenate([v_s[rows, hs], ones_col], axis=1)
        if cps == 1 and (hd, c) in qk_cache:
            qk = qk_cache[(hd, c)]
        else:
            qk = _dot(qc, kTc)
            qk_cache[(hd, c)] = qk
        w_intra = jnp.where(lower if d == 0 else upper, jnp.exp(a[idx:idx + 1, :] - col(0)), 0.0)
        nd = _dot((qk * w_intra).astype(BF16), vaug)
        if has_state or not first_chunk:
            nd = nd + col(1) * _dot(qc, cst_s[idx].astype(BF16))
        den = nd[:, HEAD_DIM:HEAD_DIM + 1]
        h = nd[:, :HEAD_DIM] * (1.0 / jnp.maximum(jnp.abs(den), col(2)))
        if d == 0:
            hm_s[rows, hs] = h
        else:
            hm_s[rows, hs] = hm_s[rows, hs] + h
        if want_state:
            kw = (kTc.astype(F32) * wk[idx:idx + 1, :]).astype(BF16)
            upd = _dot(kw, vaug)
            if has_state or not first_chunk:
                upd = upd + decay[idx:idx + 1, :] * cst_s[idx]
            cst_s[idx] = upd

    dir_rows = lax.broadcasted_iota(jnp.int32, (N_UNITS, 1), 0) >= N_HEADS
    for seq in range(n_seq):
        if has_state:
            for idx in range(N_UNITS):
                cst_s[idx] = c0_ref[0, idx]
            m_vec = m0_ref[0, :, 0:1]
        else:
            m_vec = jnp.zeros((N_UNITS, 1), F32)
        prep = None
        for d in range(2):
            order = list(range(cps)) if d == 0 else list(range(cps - 1, -1, -1))
            for pos, c in enumerate(order):
                if cps > 1 or prep is None:
                    prep = gate_prep(seq * cps + c, m_vec)
                for hd in range(N_HEADS):
                    unit(d, hd, seq * cps + c, prep, pos == 0, emit_state or pos < cps - 1)
                m_vec = jnp.where(dir_rows == (d == 1), prep[4], m_vec)
        if emit_state:
            for idx in range(N_UNITS):
                caug = cst_s[idx]
                cout_ref[0, seq * N_UNITS + idx] = caug[:, :HEAD_DIM]
                nout_ref[0, seq * N_UNITS + idx:seq * N_UNITS + idx + 1, :] = caug[:, HEAD_DIM:].T[0:1, :]
            mout_ref[0, seq * N_UNITS:(seq + 1) * N_UNITS, :] = jnp.broadcast_to(m_vec, (N_UNITS, LANES))

    e_iota = lax.broadcasted_iota(jnp.int32, (LANES, MIX_TM), 0)
    g_of_e = lax.shift_right_logical(e_iota, 2)
    j_of_e = lax.bitwise_and(e_iota, EXPERTS_PER_GROUP - 1)
    r8 = lax.broadcasted_iota(jnp.int32, (8, MIX_TM), 0)
    r8_blk = lax.broadcasted_iota(jnp.int32, (8, SUB), 0)
    before_b = (t_idx < s_idx).astype(F32).astype(BF16)

    def phase3(i, carry):
        r0 = pl.multiple_of(i * MIX_TM, MIX_TM)
        rows = pl.ds(r0, MIX_TM)
        hm = hm_s[rows, :]
        heads = []
        for hd in range(N_HEADS):
            hh = hm[:, hd * HEAD_DIM:(hd + 1) * HEAD_DIM]
            heads.append(hh * lax.rsqrt(jnp.mean(hh * hh, axis=-1, keepdims=True) + EPS))
        hn = jnp.concatenate(heads, axis=1) * w["hng"][...]
        hb2 = (so_s[rows, :] * hn).astype(BF16)
        br_b = _dot(hb2, w["wmo"][...])
        mixed = (ma_s[rows, :] + sgb_s[rows, :] * br_b).astype(BF16)
        x1 = x_ref[0, rows, :] + mod_row(2) * _dot(mixed, w["wo"][...])
        x1_ref[0, rows, :] = x1
        xn = x1 * lax.rsqrt(jnp.mean(x1 * x1, axis=-1, keepdims=True) + EPS) * w["g2"][...]
        h2 = xn * (1.0 + mod_row(4)) + mod_row(3)
        h2_ref[0, rows, :] = h2.astype(BF16)

        h2_hi = h2.astype(BF16)
        h2_lo = (h2 - h2_hi.astype(F32)).astype(BF16)
        lg = _dot(h2_hi, w["wrt2"][...])
        lg = lg[:, :LANES] + lg[:, LANES:] + _dot(h2_lo, w["wrt2"][:, :LANES])
        lt = lg.T + w["brtT"][...]
        gl = [lt[N_EXPERTS + g:N_EXPERTS + g + 1, :] for g in range(N_GROUPS)]
        best, gsel = gl[0], jnp.zeros((1, MIX_TM), jnp.int32)
        for g in range(1, N_GROUPS):
            better = gl[g] > best
            gsel = jnp.where(better, g, gsel)
            best = jnp.where(better, gl[g], best)
        gp_sel = 1.0 / sum(jnp.exp(v - best) for v in gl)
        el = []
        for j in range(EXPERTS_PER_GROUP):
            v = lt[j:j + 1, :]
            for g in range(1, N_GROUPS):
                r = g * EXPERTS_PER_GROUP + j
                v = jnp.where(gsel == g, lt[r:r + 1, :], v)
            el.append(v)
        l1, e1 = el[0], jnp.zeros((1, MIX_TM), jnp.int32)
        for j in range(1, EXPERTS_PER_GROUP):
            better = el[j] > l1
            e1 = jnp.where(better, j, e1)
            l1 = jnp.where(better, el[j], l1)
        l2 = jnp.full((1, MIX_TM), -jnp.inf, F32)
        e2 = jnp.zeros((1, MIX_TM), jnp.int32)
        for j in range(EXPERTS_PER_GROUP):
            better = jnp.logical_and(e1 != j, el[j] > l2)
            e2 = jnp.where(better, j, e2)
            l2 = jnp.where(better, el[j], l2)
        r2 = jnp.exp(l2 - l1)
        wt1 = gp_sel / (1.0 + r2)
        wt2 = gp_sel * r2 / (1.0 + r2)
        in_group = g_of_e == gsel
        comb_t = (jnp.where(jnp.logical_and(in_group, j_of_e == e1), wt1, 0.0)
                  + jnp.where(jnp.logical_and(in_group, j_of_e == e2), wt2, 0.0))

        onehot = (r8 == gsel).astype(F32)
        gsel_f = gsel.astype(F32)
        ranks = []
        for j in range(cpm):
            oh = onehot[:, j * SUB:(j + 1) * SUB]
            rank = jnp.sum(oh * _dot(oh.astype(BF16), before_b), axis=0, keepdims=True)
            ranks.append(rank)
            r8rows = pl.ds(pl.multiple_of((i * cpm + j) * 8, 8), 8)
            route_ref[0, r8rows, :] = jnp.where(r8_blk == 0, gsel_f[:, j * SUB:(j + 1) * SUB],
                                                jnp.where(r8_blk == 1, rank, 0.0))
            cnt_ref[0, r8rows, :] = jnp.broadcast_to(jnp.sum(oh, axis=1, keepdims=True), (8, LANES))
        comb_t = jnp.where(e_iota == ROUTE_GROUP_LANE, gsel_f,
                           jnp.where(e_iota == ROUTE_RANK_LANE, jnp.concatenate(ranks, axis=1), comb_t))
        comb_ref[0, rows, :] = comb_t.T
        return carry

    if n_mt == 1:
        phase3(0, 0)
    else:
        lax.fori_loop(0, n_mt, phase3, 0)


def _const_spec(a):
    nd = a.ndim
    return pl.BlockSpec(a.shape, lambda b, _nd=nd: (0,) * _nd, pipeline_mode=pl.Buffered(1))


def _mixer(x, T, mod, mod_index, weights, P, state=None, emit_state=False):
    B, R, _ = x.shape
    n_chunks = R // SUB
    n_seq = R // T
    has_state = state is not None
    seq_mode = {} if R <= MIX_TM else {"pipeline_mode": pl.Buffered(1)}
    in_specs = [
        pl.BlockSpec((1, R, D_MODEL), lambda b: (b, 0, 0), **seq_mode),
        pl.BlockSpec((1, N_ADA, D_MODEL), lambda b: (mod_index(b), 0, 0)),
    ]
    args = [x, mod]
    if has_state:
        caug0, m0 = state
        in_specs += [
            pl.BlockSpec((1, N_UNITS, HEAD_DIM, 2 * HEAD_DIM), lambda b: (b, 0, 0, 0)),
            pl.BlockSpec((1, N_UNITS, LANES), lambda b: (b, 0, 0)),
        ]
        args += [caug0, m0]
    for name in _MIXER_WEIGHTS:
        in_specs.append(_const_spec(weights[name]))
        args.append(weights[name])
    out_shape = [
        jax.ShapeDtypeStruct((B, R, D_MODEL), F32),
        jax.ShapeDtypeStruct((B, R, D_MODEL), BF16),
        jax.ShapeDtypeStruct((B, R, LANES), F32),
        jax.ShapeDtypeStruct((B, n_chunks * 8, SUB), F32),
        jax.ShapeDtypeStruct((B, n_chunks * 8, LANES), F32),
    ]
    out_specs = [
        pl.BlockSpec((1, R, D_MODEL), lambda b: (b, 0, 0), **seq_mode),
        pl.BlockSpec((1, R, D_MODEL), lambda b: (b, 0, 0), **seq_mode),
        pl.BlockSpec((1, R, LANES), lambda b: (b, 0, 0)),
        pl.BlockSpec((1, n_chunks * 8, SUB), lambda b: (b, 0, 0)),
        pl.BlockSpec((1, n_chunks * 8, LANES), lambda b: (b, 0, 0)),
    ]
    if emit_state:
        out_shape += [
            jax.ShapeDtypeStruct((B, n_seq * N_UNITS, HEAD_DIM, HEAD_DIM), F32),
            jax.ShapeDtypeStruct((B, n_seq * N_UNITS, HEAD_DIM), F32),
            jax.ShapeDtypeStruct((B, n_seq * N_UNITS, LANES), F32),
        ]
        out_specs += [
            pl.BlockSpec((1, n_seq * N_UNITS, HEAD_DIM, HEAD_DIM), lambda b: (b, 0, 0, 0)),
            pl.BlockSpec((1, n_seq * N_UNITS, HEAD_DIM), lambda b: (b, 0, 0)),
            pl.BlockSpec((1, n_seq * N_UNITS, LANES), lambda b: (b, 0, 0)),
        ]
    scratch = [
        pltpu.VMEM((R, D_MLSTM), BF16),
        pltpu.VMEM((n_chunks, D_MLSTM, SUB), BF16),
        pltpu.VMEM((R, D_MLSTM), BF16),
        pltpu.VMEM((R, D_MLSTM), F32),
        pltpu.VMEM((n_chunks, 5 * N_UNITS, SUB), F32),
        pltpu.VMEM((R, D_MODEL), F32),
        pltpu.VMEM((R, D_MODEL), F32),
        pltpu.VMEM((R, D_MLSTM), F32),
        pltpu.VMEM((N_UNITS, HEAD_DIM, 2 * HEAD_DIM), F32),
        pltpu.VMEM((MIX_TM // P, P + 2 * CONV_PAD, D_CONV), F32),
    ]
    return pl.pallas_call(
        functools.partial(_mixer_kernel, R, T, P, has_state, emit_state),
        grid=(B,),
        in_specs=in_specs,
        out_specs=out_specs,
        out_shape=out_shape,
        scratch_shapes=scratch,
        compiler_params=pltpu.CompilerParams(
            dimension_semantics=("arbitrary",), vmem_limit_bytes=VMEM_LIMIT),
        name="mixer_T%d" % T,
    )(*args)


def _dest_in_block(group, rank, starts):
    dest = rank
    for g in range(N_GROUPS):
        dest = dest + jnp.where(group == float(g), starts[g], 0.0)
    return dest


def _copy_segments(src_refs, dst_refs, src_starts, dst_starts, n_pieces):
    for g in range(N_GROUPS):
        def body(k, carry, g=g):
            s = pl.multiple_of(src_starts[g] + k * ROW_ALIGN, ROW_ALIGN)
            d = pl.multiple_of(dst_starts[g] + k * ROW_ALIGN, ROW_ALIGN)
            for src, dst in zip(src_refs, dst_refs):
                dst[pl.ds(d, ROW_ALIGN), :] = src[pl.ds(s, ROW_ALIGN), :]
            return carry
        lax.fori_loop(0, n_pieces[g], body, 0)


def _dispatch_kernel(n_ctx_blocks, start_ref, npiece_ref, off_ref,
                     h2c_ref, h2l_ref, cbc_ref, cbl_ref, rtc_ref, rtl_ref,
                     xs_ref, cs_ref, sx_s, sc_s):
    b = pl.program_id(0)
    is_ctx = b < n_ctx_blocks

    @pl.when(b == 0)
    def _():
        xs_ref[...] = jnp.zeros_like(xs_ref)
        cs_ref[...] = jnp.zeros_like(cs_ref)

    h2 = jnp.where(is_ctx, h2c_ref[0], h2l_ref[0])
    cb = jnp.where(is_ctx, cbc_ref[0], cbl_ref[0])
    rt = jnp.where(is_ctx, rtc_ref[0], rtl_ref[0])
    starts = [start_ref[b * N_GROUPS + g] for g in range(N_GROUPS)]
    dest = _dest_in_block(rt[0:1, :], rt[1:2, :], [s.astype(F32) for s in starts])
    row = lax.broadcasted_iota(jnp.int32, (SORT_ROWS, SUB), 0).astype(F32)
    perm = (row == dest).astype(F32).astype(BF16)
    cb_hi = cb.astype(BF16)
    cb_lo = (cb - cb_hi.astype(F32)).astype(BF16)
    sx_s[...] = _dot(perm, h2).astype(BF16)
    sc_s[...] = _dot(perm, jnp.concatenate([cb_hi, cb_lo], axis=1)).astype(BF16)
    _copy_segments((sx_s, sc_s), (xs_ref, cs_ref), starts,
                   [off_ref[b * N_GROUPS + g] for g in range(N_GROUPS)],
                   [npiece_ref[b * N_GROUPS + g] for g in range(N_GROUPS)])


def _experts_kernel(tgroup_ref, tvalid_ref, tfirst_ref, xs_ref, cs_ref, wg_ref, wu_ref, wd_ref, ys_ref,
                    wg_s, wu_s, wd_s):
    i = pl.program_id(0)

    @pl.when(tfirst_ref[i] == 1)
    def _():
        for j in range(EXPERTS_PER_GROUP):
            cols = slice(j * D_EXPERT, (j + 1) * D_EXPERT)
            wg_s[:, cols] = wg_ref[j].astype(BF16)
            wu_s[:, cols] = wu_ref[j].astype(BF16)
            wd_s[cols, :] = wd_ref[j].astype(BF16)

    @pl.when(tvalid_ref[i] == 1)
    def _():
        x = xs_ref[...]
        g = _dot(x, wg_s[...])
        u = _dot(x, wu_s[...])
        comb = cs_ref[:, :LANES].astype(F32) + cs_ref[:, LANES:].astype(F32)
        lane = lax.broadcasted_iota(jnp.int32, comb.shape, 1)
        first = tgroup_ref[i] * EXPERTS_PER_GROUP
        parts = []
        for j in range(EXPERTS_PER_GROUP):
            cols = slice(j * D_EXPERT, (j + 1) * D_EXPERT)
            cw = jnp.sum(jnp.where(lane == first + j, comb, 0.0), axis=1, keepdims=True)
            gj = g[:, cols]
            parts.append((gj * _sigmoid(gj) * u[:, cols] * cw).astype(BF16))
        ys_ref[...] = _dot(jnp.concatenate(parts, axis=1), wd_s[...]).astype(BF16)

    @pl.when(tvalid_ref[i] == 0)
    def _():
        ys_ref[...] = jnp.zeros_like(ys_ref)


def _combine_kernel(n_ctx_blocks, blocks_per_lat_seq, start_ref, npiece_ref, off_ref,
                    x1c_ref, x1l_ref, cbc_ref, cbl_ref, ys_ref, mod_ref, gf_ref, yc_ref, yl_ref, loc_s):
    b = pl.program_id(0)
    is_ctx = b < n_ctx_blocks
    starts = [start_ref[b * N_GROUPS + g] for g in range(N_GROUPS)]
    loc_s[...] = jnp.zeros_like(loc_s)
    _copy_segments((ys_ref,), (loc_s,), [off_ref[b * N_GROUPS + g] for g in range(N_GROUPS)], starts,
                   [npiece_ref[b * N_GROUPS + g] for g in range(N_GROUPS)])
    cb = jnp.where(is_ctx, cbc_ref[0], cbl_ref[0])
    dest = _dest_in_block(cb[:, ROUTE_GROUP_LANE:ROUTE_GROUP_LANE + 1],
                          cb[:, ROUTE_RANK_LANE:ROUTE_RANK_LANE + 1],
                          [s.astype(F32) for s in starts])
    col = lax.broadcasted_iota(jnp.int32, (SUB, SORT_ROWS), 1).astype(F32)
    unperm = (col == dest).astype(F32).astype(BF16)
    moe = _dot(unperm, loc_s[...])
    x1 = jnp.where(is_ctx, x1c_ref[0], x1l_ref[0])
    mrow = jnp.where(is_ctx, 0, 1 + jnp.maximum(b - n_ctx_blocks, 0) // blocks_per_lat_seq)
    x2 = x1 + mod_ref[mrow, N_ADA - 1:N_ADA, :] * moe
    y = x2 * lax.rsqrt(jnp.mean(x2 * x2, axis=-1, keepdims=True) + EPS) * gf_ref[...]

    @pl.when(is_ctx)
    def _():
        yc_ref[0] = y

    @pl.when(jnp.logical_not(is_ctx))
    def _():
        yl_ref[0] = y


def _moe_plan(cnt, n_tiles):
    cnt_al = (cnt + ROW_ALIGN - 1) // ROW_ALIGN * ROW_ALIGN
    start = jnp.cumsum(cnt_al, axis=1) - cnt_al
    gpad = (jnp.sum(cnt_al, axis=0) + MOE_TM - 1) // MOE_TM * MOE_TM
    gbase = jnp.cumsum(gpad) - gpad
    off = gbase[None, :] + jnp.cumsum(cnt_al, axis=0) - cnt_al
    tile_end = jnp.cumsum(gpad // MOE_TM)
    t = jnp.arange(n_tiles, dtype=jnp.int32)
    tgroup = jnp.sum((t[:, None] >= tile_end[None, :]).astype(jnp.int32), axis=1)
    valid = t < tile_end[-1]
    last_group = jnp.sum((tile_end[-1] - 1 >= tile_end).astype(jnp.int32))
    tgroup = jnp.where(valid, tgroup, last_group)
    first = jnp.logical_and(valid, jnp.concatenate([jnp.ones((1,), bool), tgroup[1:] != tgroup[:-1]]))
    i32 = lambda a: a.astype(jnp.int32).reshape(-1)
    return i32(start), i32(cnt_al // ROW_ALIGN), i32(off), i32(tgroup), i32(valid), i32(first)


def _moe(x1c, x1l, h2c, h2l, cbc, cbl, rtc, rtl, cnt, mod, blocks_per_lat_seq, wg, wu, wd, gf):
    nc, nl = x1c.shape[0], x1l.shape[0]
    nb = nc + nl
    n_rows_max = nb * SUB + nb * N_GROUPS * (ROW_ALIGN - 1) + N_GROUPS * (MOE_TM - ROW_ALIGN)
    n_tiles = -(-n_rows_max // MOE_TM)
    ns = n_tiles * MOE_TM
    start, npiece, off, tgroup, tvalid, tfirst = _moe_plan(cnt, n_tiles)

    cmap = lambda b, *_: (jnp.minimum(b, nc - 1), 0, 0)
    lmap = lambda b, *_: (jnp.maximum(b - nc, 0), 0, 0)
    whole = lambda *_: (0, 0)
    once = {"pipeline_mode": pl.Buffered(1)}
    arb = pltpu.CompilerParams(dimension_semantics=("arbitrary",), vmem_limit_bytes=VMEM_LIMIT)

    xs, cs = pl.pallas_call(
        functools.partial(_dispatch_kernel, nc),
        grid_spec=pltpu.PrefetchScalarGridSpec(
            num_scalar_prefetch=3, grid=(nb,),
            in_specs=[
                pl.BlockSpec((1, SUB, D_MODEL), cmap), pl.BlockSpec((1, SUB, D_MODEL), lmap),
                pl.BlockSpec((1, SUB, LANES), cmap), pl.BlockSpec((1, SUB, LANES), lmap),
                pl.BlockSpec((1, 8, SUB), cmap), pl.BlockSpec((1, 8, SUB), lmap),
            ],
            out_specs=[pl.BlockSpec((ns, D_MODEL), whole, **once), pl.BlockSpec((ns, 2 * LANES), whole, **once)],
            scratch_shapes=[pltpu.VMEM((SORT_ROWS, D_MODEL), BF16), pltpu.VMEM((SORT_ROWS, 2 * LANES), BF16)],
        ),
        out_shape=[jax.ShapeDtypeStruct((ns, D_MODEL), BF16), jax.ShapeDtypeStruct((ns, 2 * LANES), BF16)],
        compiler_params=arb,
        name="moe_dispatch",
    )(start, npiece, off, h2c, h2l, cbc, cbl, rtc, rtl)

    wmap = lambda i, tg, tv, tf: (tg[i], 0, 0)
    ys = pl.pallas_call(
        _experts_kernel,
        grid_spec=pltpu.PrefetchScalarGridSpec(
            num_scalar_prefetch=3, grid=(n_tiles,),
            in_specs=[
                pl.BlockSpec((MOE_TM, D_MODEL), lambda i, *_: (i, 0)),
                pl.BlockSpec((MOE_TM, 2 * LANES), lambda i, *_: (i, 0)),
                pl.BlockSpec((EXPERTS_PER_GROUP, D_MODEL, D_EXPERT), wmap),
                pl.BlockSpec((EXPERTS_PER_GROUP, D_MODEL, D_EXPERT), wmap),
                pl.BlockSpec((EXPERTS_PER_GROUP, D_EXPERT, D_MODEL), wmap),
            ],
            out_specs=pl.BlockSpec((MOE_TM, D_MODEL), lambda i, *_: (i, 0)),
            scratch_shapes=[pltpu.VMEM((D_MODEL, EXPERTS_PER_GROUP * D_EXPERT), BF16),
                            pltpu.VMEM((D_MODEL, EXPERTS_PER_GROUP * D_EXPERT), BF16),
                            pltpu.VMEM((EXPERTS_PER_GROUP * D_EXPERT, D_MODEL), BF16)],
        ),
        out_shape=jax.ShapeDtypeStruct((ns, D_MODEL), BF16),
        compiler_params=arb,
        name="moe_experts",
    )(tgroup, tvalid, tfirst, xs, cs, wg, wu, wd)

    yc, yl = pl.pallas_call(
        functools.partial(_combine_kernel, nc, blocks_per_lat_seq),
        grid_spec=pltpu.PrefetchScalarGridSpec(
            num_scalar_prefetch=3, grid=(nb,),
            in_specs=[
                pl.BlockSpec((1, SUB, D_MODEL), cmap), pl.BlockSpec((1, SUB, D_MODEL), lmap),
                pl.BlockSpec((1, SUB, LANES), cmap), pl.BlockSpec((1, SUB, LANES), lmap),
                pl.BlockSpec((ns, D_MODEL), whole, **once),
                pl.BlockSpec(mod.shape, lambda *_: (0, 0, 0)),
                pl.BlockSpec((1, D_MODEL), whole),
            ],
            out_specs=[pl.BlockSpec((1, SUB, D_MODEL), cmap), pl.BlockSpec((1, SUB, D_MODEL), lmap)],
            scratch_shapes=[pltpu.VMEM((SORT_ROWS, D_MODEL), BF16)],
        ),
        out_shape=[jax.ShapeDtypeStruct((nc, SUB, D_MODEL), F32), jax.ShapeDtypeStruct((nl, SUB, D_MODEL), F32)],
        compiler_params=arb,
        name="moe_combine",
    )(start, npiece, off, x1c, x1l, cbc, cbl, ys, mod, gf)
    return yc, yl


def _prep_weights(norm1_g, w_in, b_in, b_gates, w_dw, b_dw, conv_ln_g, conv_ln_b, w_conv_out,
                  mlstm_hn_g, w_mlstm_out, w_o, norm2_g, w_rg, b_rg, w_re, b_re):
    s_a = 2 * D_CONV
    s_q = s_a + D_MLSTM
    s_k = s_q + D_MLSTM
    s_v = s_k + D_MLSTM
    s_o = s_v + D_MLSTM
    s_g = s_o + 4 * N_HEADS
    row = lambda v: v.reshape(1, -1).astype(F32)
    wb = lambda a, b: w_in[:, a:b].astype(BF16)
    w_g = w_in[:, s_o:s_g].reshape(D_MODEL, 2, 2, N_HEADS).transpose(2, 1, 3, 0).reshape(4 * N_HEADS, D_MODEL)
    bg = (b_in[s_o:s_g] + b_gates.reshape(-1)).reshape(2, 2, N_HEADS).transpose(1, 0, 2).reshape(-1, 1)
    n_rt = N_EXPERTS + N_GROUPS
    wrt = jnp.pad(jnp.concatenate([w_re, w_rg], axis=1), ((0, 0), (0, LANES - n_rt)))
    wrt_hi = wrt.astype(BF16)
    wrt2 = jnp.concatenate([wrt_hi, (wrt - wrt_hi.astype(F32)).astype(BF16)], axis=1)
    brtT = jnp.pad(jnp.concatenate([b_re, b_rg]), (0, LANES - n_rt)).reshape(LANES, 1)
    return {
        "g1": row(norm1_g),
        "wag": wb(0, s_a), "bag": row(b_in[:s_a]),
        "wq": wb(s_a, s_q), "bq": row(b_in[s_a:s_q]),
        "wkT": w_in[:, s_q:s_k].T.astype(BF16), "bk": b_in[s_q:s_k].reshape(-1, 1),
        "wv": wb(s_k, s_v), "bv": row(b_in[s_k:s_v]),
        "wog": wb(s_v, s_o), "bog": row(b_in[s_v:s_o]),
        "wgifT": w_g.astype(BF16), "bgifT": bg,
        "wgm": wb(s_g, w_in.shape[1]), "bgm": row(b_in[s_g:]),
        "wdw": w_dw.astype(F32), "bdw": row(b_dw), "lng": row(conv_ln_g), "lnb": row(conv_ln_b),
        "wco": w_conv_out.astype(BF16), "hng": row(mlstm_hn_g), "wmo": w_mlstm_out.astype(BF16),
        "wo": w_o.astype(BF16), "g2": row(norm2_g), "wrt2": wrt2, "brtT": brtT,
    }


def kernel(x_prompt, x_sample, state_C, state_n, state_m, c, c_ctx, norm1_g, w_ada, b_ada, w_in, b_in, b_gates, w_dw, b_dw, conv_ln_g, conv_ln_b, w_conv_out, mlstm_hn_g, w_mlstm_out, w_o, norm2_g, w_rg, b_rg, w_re, b_re, w_e_gate, w_e_up, w_e_down, norm_final_g):
    B, S, _ = x_prompt.shape
    Bd, Sd, _ = x_sample.shape
    assert w_ada.shape[0] == 1, "single trunk layer"
    assert S == SUB and Sd % SUB == 0

    cin = jnp.concatenate([c_ctx[None, :], c, jnp.zeros((8 - 1 - Bd, D_MODEL), F32)], axis=0)
    mod = _ada(cin, w_ada[0], b_ada[0].reshape(1, -1)).reshape(8, N_ADA, D_MODEL)

    wts = _prep_weights(norm1_g[0], w_in[0], b_in[0], b_gates[0], w_dw[0], b_dw[0], conv_ln_g[0],
                        conv_ln_b[0], w_conv_out[0], mlstm_hn_g[0], w_mlstm_out[0], w_o[0],
                        norm2_g[0], w_rg[0], b_rg[0], w_re[0], b_re[0])

    x1p, h2p, cbp, rtp, cntp, c_new, n_new, m_new = _mixer(
        x_prompt.reshape(B * S // MIX_TM, MIX_TM, D_MODEL), S, mod, lambda b: 0, wts, P=S, emit_state=True)

    sc = state_C[:, 0].reshape(Bd, N_UNITS, HEAD_DIM, HEAD_DIM)
    sn = state_n[:, 0].reshape(Bd, N_UNITS, HEAD_DIM, 1)
    caug0 = jnp.concatenate([sc, sn, jnp.zeros((Bd, N_UNITS, HEAD_DIM, HEAD_DIM - 1), F32)], axis=-1)
    m0 = jnp.broadcast_to(state_m[:, 0].reshape(Bd, N_UNITS, 1), (Bd, N_UNITS, LANES))
    x1s, h2s, cbs, rts, cnts = _mixer(x_sample, Sd, mod, lambda b: 1 + b, wts, P=GRID_W, state=(caug0, m0))

    nc, nl = B * S // SUB, Bd * Sd // SUB
    blk = lambda a, n: a.reshape(n, SUB, a.shape[-1])
    cnt = jnp.concatenate([cntp.reshape(nc, 8, LANES)[:, :N_GROUPS, 0],
                           cnts.reshape(nl, 8, LANES)[:, :N_GROUPS, 0]], axis=0)
    yp, ys = _moe(blk(x1p, nc), blk(x1s, nl), blk(h2p, nc), blk(h2s, nl), blk(cbp, nc), blk(cbs, nl),
                  rtp.reshape(nc, 8, SUB), rts.reshape(nl, 8, SUB),
                  cnt.astype(jnp.int32), mod, Sd // SUB, w_e_gate[0], w_e_up[0], w_e_down[0],
                  norm_final_g.reshape(1, -1))

    return (yp.reshape(B, S, D_MODEL), ys.reshape(Bd, Sd, D_MODEL),
            c_new.reshape(B, 1, 2, N_HEADS, HEAD_DIM, HEAD_DIM),
            n_new.reshape(B, 1, 2, N_HEADS, HEAD_DIM),
            m_new[:, :, 0].reshape(B, 1, 2, N_HEADS))
```

```python
import functools

import jax
import jax.numpy as jnp
from jax import lax
from jax.experimental import pallas as pl
from jax.experimental.pallas import tpu as pltpu

D_MODEL = 1024
D_CONV = 512
CONV_K = 31
D_MLSTM = 512
N_HEADS = 4
HEAD_DIM = D_MLSTM // N_HEADS
N_GROUPS = 4
EXPERTS_PER_GROUP = 4
N_EXPERTS = N_GROUPS * EXPERTS_PER_GROUP
D_EXPERT = 256
N_ADA = 6
EPS = 1e-6
GRID_W = 64

LANES = 128
SUB = 256
CONV_PAD = 16
CONV_RB = 64
N_UNITS = 2 * N_HEADS
ROW_ALIGN = 16
SORT_ROWS = SUB + N_GROUPS * ROW_ALIGN
MOE_TM = 512
MIX_TM = 512
ROUTE_GROUP_LANE = N_EXPERTS
ROUTE_RANK_LANE = N_EXPERTS + 1
VMEM_LIMIT = 58 * 1024 * 1024

BF16 = jnp.bfloat16
F32 = jnp.float32
NT_DIMS = (((1,), (1,)), ((), ()))


def _dot(a, b):
    return jnp.dot(a, b, preferred_element_type=F32)


def _dot_nt(a, b, precision=None):
    return lax.dot_general(a, b, NT_DIMS, preferred_element_type=F32, precision=precision)


def _sigmoid(x):
    return 0.5 * jnp.tanh(0.5 * x) + 0.5


def _log_sigmoid(x):
    return jnp.minimum(x, 0.0) - jnp.log1p(jnp.exp(-jnp.abs(x)))


def _split3(x):
    hi = x.astype(BF16).astype(F32)
    r1 = x - hi
    mid = r1.astype(BF16).astype(F32)
    lo = (r1 - mid).astype(BF16).astype(F32)
    return hi, mid, lo


def _ada_kernel(c_ref, w_ref, b_ref, o_ref):
    c = c_ref[...]
    s = (c * _sigmoid(c)).astype(BF16)
    o_ref[...] = _dot(s, w_ref[...].astype(BF16)) + b_ref[...]


def _ada(cin, w_ada, b_ada):
    n = w_ada.shape[1]
    bn = 1024
    return pl.pallas_call(
        _ada_kernel,
        grid=(n // bn,),
        in_specs=[
            pl.BlockSpec((8, D_MODEL), lambda j: (0, 0)),
            pl.BlockSpec((D_MODEL, bn), lambda j: (0, j)),
            pl.BlockSpec((1, bn), lambda j: (0, j)),
        ],
        out_specs=pl.BlockSpec((8, bn), lambda j: (0, j)),
        out_shape=jax.ShapeDtypeStruct((8, n), F32),
        compiler_params=pltpu.CompilerParams(dimension_semantics=("arbitrary",)),
        name="ada",
    )(cin, w_ada, b_ada)


_MIXER_WEIGHTS = (
    "g1", "wag", "bag", "wq", "bq", "wkT", "bk", "wv", "bv", "wog", "bog",
    "wgifT", "bgifT", "wgm", "bgm", "wdw", "bdw", "lng", "lnb",
    "wco", "hng", "wmo", "wo", "g2", "wrt2", "brtT",
)


def _conv_block(upad_s, seg, base, cs, wdw_ref, bdw_ref):
    sub = 8
    first = CONV_PAD - CONV_K // 2
    acc = jnp.broadcast_to(bdw_ref[0:1, cs], (CONV_RB, LANES))
    for r in range(sub):
        z = None
        for a in range((CONV_K + first + sub - 1) // sub):
            j = sub * a + r - first
            if 0 <= j < CONV_K:
                lo = base + sub * a
                term = wdw_ref[j:j + 1, cs] * upad_s[seg, lo:lo + CONV_RB + sub, cs]
                z = term if z is None else z + term
        acc = acc + z[r:r + CONV_RB, :]
    return acc


def _mixer_kernel(R, T, P, has_state, emit_state, *refs):
    L = SUB
    n_mt = R // MIX_TM
    cpm = MIX_TM // L
    n_seq = R // T
    cps = T // L
    nseg = MIX_TM // P
    assert not has_state or n_seq == 1
    it = iter(refs)
    x_ref = next(it)
    mod_ref = next(it)
    if has_state:
        c0_ref = next(it)
        n0_ref = next(it)
        m0_ref = next(it)
    w = {name: next(it) for name in _MIXER_WEIGHTS}
    x1_ref = next(it)
    h2_ref = next(it)
    comb_ref = next(it)
    route_ref = next(it)
    cnt_ref = next(it)
    if emit_state:
        cout_ref = next(it)
        nout_ref = next(it)
        mout_ref = next(it)
    (q_s, kT_s, v_s, so_s, scan_s, ma_s, sgb_s, hm_s, cst_s, upad_s) = [next(it) for _ in range(10)]

    def mod_row(i):
        return mod_ref[0, i:i + 1, :]

    zpad = jnp.zeros((CONV_PAD, D_CONV), F32)
    for seg in range(nseg):
        upad_s[seg, 0:CONV_PAD, :] = zpad
        upad_s[seg, CONV_PAD + P:CONV_PAD + P + CONV_PAD, :] = zpad

    t_idx = lax.broadcasted_iota(jnp.int32, (L, L), 0)
    s_idx = lax.broadcasted_iota(jnp.int32, (L, L), 1)
    lower = s_idx <= t_idx
    upper = s_idx >= t_idx
    triu_b = upper.astype(F32).astype(BF16)
    lane_u = lax.broadcasted_iota(jnp.int32, (N_UNITS, L), 1)
    is_bwd = lax.broadcasted_iota(jnp.int32, (N_UNITS, L), 0) >= N_HEADS

    def gate_scan(g):
        gi, lf = g[:N_UNITS], _log_sigmoid(g[N_UNITS:])
        pr = _dot(jnp.concatenate(_split3(lf), axis=0).astype(BF16), triu_b)
        pre = pr[0:N_UNITS] + pr[N_UNITS:2 * N_UNITS] + pr[2 * N_UNITS:]
        tot = pre[:, L - 1:L]
        bsum = jnp.where(is_bwd, tot - pre + lf, pre)
        a = gi - bsum
        pm, sm, k = a, a, 1
        while k < L:
            pm = jnp.where(lane_u >= k, jnp.maximum(pm, pltpu.roll(pm, k, axis=1)), pm)
            sm = jnp.where(lane_u < L - k, jnp.maximum(sm, pltpu.roll(sm, L - k, axis=1)), sm)
            k *= 2
        wide = lambda v: jnp.broadcast_to(v, (N_UNITS, L))
        return jnp.concatenate([a, jnp.where(is_bwd, sm, pm), bsum, wide(tot),
                                wide(jnp.max(a, axis=1, keepdims=True))], axis=0)

    def phase1(i, carry):
        r0 = pl.multiple_of(i * MIX_TM, MIX_TM)
        rows = pl.ds(r0, MIX_TM)
        x = x_ref[0, rows, :]
        xn = x * lax.rsqrt(jnp.mean(x * x, axis=-1, keepdims=True) + EPS) * w["g1"][...]
        hb = (xn * (1.0 + mod_row(1)) + mod_row(0)).astype(BF16)

        gates = _dot_nt(w["wgifT"][...], hb) + w["bgifT"][...]
        for j in range(cpm):
            scan_s[i * cpm + j] = gate_scan(gates[:, j * L:(j + 1) * L])
        ag = _dot(hb, w["wag"][...]) + w["bag"][...]
        u = ag[:, :D_CONV] * _sigmoid(ag[:, D_CONV:])
        for seg in range(nseg):
            upad_s[seg, CONV_PAD:CONV_PAD + P, :] = u[seg * P:(seg + 1) * P, :]

        def proj(name, bias, c0, width=2 * LANES):
            cs = slice(c0, c0 + width)
            return _dot(hb, w[name][:, cs]) + w[bias][:, cs]

        def gm_a(c0):
            ma_s[rows, c0:c0 + 2 * LANES] = _sigmoid(proj("wgm", "bgm", c0))

        def gm_b(c0):
            sgb_s[rows, c0:c0 + 2 * LANES] = _sigmoid(proj("wgm", "bgm", D_MODEL + c0))

        def q_part(c0):
            q_s[rows, c0:c0 + 2 * LANES] = (proj("wq", "bq", c0) * (HEAD_DIM ** -0.5)).astype(BF16)

        def v_part(c0):
            v_s[rows, c0:c0 + 2 * LANES] = proj("wv", "bv", c0).astype(BF16)

        def o_part(c0):
            so_s[rows, c0:c0 + 2 * LANES] = _sigmoid(proj("wog", "bog", c0))

        def k_part(c0):
            rs = slice(c0, c0 + 2 * LANES)
            kt = (_dot_nt(w["wkT"][rs, :], hb) + w["bk"][rs, :]).astype(BF16)
            for j in range(cpm):
                kT_s[i * cpm + j, rs, :] = kt[:, j * L:(j + 1) * L]

        jobs = ([functools.partial(gm_a, c0) for c0 in range(0, D_MODEL, 2 * LANES)]
                + [functools.partial(gm_b, c0) for c0 in range(0, D_MODEL, 2 * LANES)]
                + [functools.partial(f, c0) for f in (q_part, k_part, v_part, o_part)
                   for c0 in range(0, D_MLSTM, 2 * LANES)])
        n_jobs = len(jobs)
        conv = {}
        n_pieces = (D_CONV // LANES) * nseg * (P // CONV_RB)
        for cb in range(D_CONV // LANES):
            cs = slice(cb * LANES, (cb + 1) * LANES)
            for seg in range(nseg):
                for rb in range(P // CONV_RB):
                    conv[(cb, seg, rb)] = _conv_block(upad_s, seg, rb * CONV_RB, cs, w["wdw"], w["bdw"])
                    if jobs and len(conv) * n_jobs >= (n_jobs - len(jobs) + 1) * n_pieces:
                        jobs.pop(0)()
        for job in jobs:
            job()
        cu = jnp.concatenate(
            [jnp.concatenate([conv[(cb, seg, rb)] for seg in range(nseg) for rb in range(P // CONV_RB)], axis=0)
             for cb in range(D_CONV // LANES)], axis=1)
        mu = jnp.mean(cu, axis=-1, keepdims=True)
        cc = cu - mu
        cn = cc * lax.rsqrt(jnp.mean(cc * cc, axis=-1, keepdims=True) + EPS) * w["lng"][...] + w["lnb"][...]
        ca = (cn * _sigmoid(cn)).astype(BF16)
        ma_s[rows, :] = ma_s[rows, :] * _dot(ca, w["wco"][...])
        return carry

    if n_mt == 1:
        phase1(0, 0)
    else:
        lax.fori_loop(0, n_mt, phase1, 0)

    ones_col = (lax.broadcasted_iota(jnp.int32, (L, HEAD_DIM), 1) == 0).astype(F32).astype(BF16)
    pad_rows = jnp.zeros((LANES - 3 * N_UNITS, L), F32)

    def gate_prep(c, m_vec):
        sc = scan_s[c]
        a, run_max, bsum = sc[0:N_UNITS], sc[N_UNITS:2 * N_UNITS], sc[2 * N_UNITS:3 * N_UNITS]
        tot, a_max = sc[3 * N_UNITS:4 * N_UNITS, 0:1], sc[4 * N_UNITS:5 * N_UNITS, 0:1]
        big_m = jnp.maximum(m_vec, run_max)
        m_end = jnp.maximum(m_vec, a_max)
        cols = jnp.concatenate(
            [big_m, jnp.exp(m_vec - big_m), jnp.exp(-bsum - big_m), pad_rows], axis=0).T
        return a, cols, jnp.exp(a - m_end), jnp.exp(m_vec - m_end), tot + m_end

    qk_cache = {}

    def unit(d, hd, c, prep, first_chunk, want_state):
        a, cols, wk, decay, _ = prep
        rows = slice(c * L, (c + 1) * L)
        hs = slice(hd * HEAD_DIM, (hd + 1) * HEAD_DIM)
        idx = d * N_HEADS + hd
        col = lambda k: cols[:, k * N_UNITS + idx:k * N_UNITS + idx + 1]
        qc = q_s[rows, hs]
        kTc = kT_s[c, hs, :]
        vaug = jnp.concatenate([v_s[rows, hs], ones_col], axis=1)
        if cps == 1 and (hd, c) in qk_cache:
            qk = qk_cache[(hd, c)]
        else:
            qk = _dot(qc, kTc)
            qk_cache[(hd, c)] = qk
        w_intra = jnp.where(lower if d == 0 else upper, jnp.exp(a[idx:idx + 1, :] - col(0)), 0.0)
        nd = _dot((qk * w_intra).astype(BF16), vaug)
        if has_state or not first_chunk:
            nd = nd + col(1) * _dot(qc, cst_s[idx].astype(BF16))
        den = nd[:, HEAD_DIM:HEAD_DIM + 1]
        h = nd[:, :HEAD_DIM] * (1.0 / jnp.maximum(jnp.abs(den), col(2)))
        if d == 0:
            hm_s[rows, hs] = h
        else:
            hm_s[rows, hs] = hm_s[rows, hs] + h
        if want_state:
            kw = (kTc.astype(F32) * wk[idx:idx + 1, :]).astype(BF16)
            upd = _dot(kw, vaug)
            if has_state or not first_chunk:
                upd = upd + decay[idx:idx + 1, :] * cst_s[idx]
            cst_s[idx] = upd

    dir_rows = lax.broadcasted_iota(jnp.int32, (N_UNITS, 1), 0) >= N_HEADS
    for seq in range(n_seq):
        if has_state:
            n_cols = jnp.concatenate([n0_ref[0], jnp.zeros((LANES - N_UNITS, HEAD_DIM), F32)], axis=0).T
            first_lane = lax.broadcasted_iota(jnp.int32, (HEAD_DIM, HEAD_DIM), 1) == 0
            for idx in range(N_UNITS):
                cst_s[idx, :, :HEAD_DIM] = c0_ref[0, idx]
                cst_s[idx, :, HEAD_DIM:] = jnp.where(first_lane, n_cols[:, idx:idx + 1], 0.0)
            m_vec = m0_ref[0, :, 0:1]
        else:
            m_vec = jnp.zeros((N_UNITS, 1), F32)
        prep = None
        for d in range(2):
            order = list(range(cps)) if d == 0 else list(range(cps - 1, -1, -1))
            for pos, c in enumerate(order):
                if cps > 1 or prep is None:
                    prep = gate_prep(seq * cps + c, m_vec)
                for hd in range(N_HEADS):
                    unit(d, hd, seq * cps + c, prep, pos == 0, emit_state or pos < cps - 1)
                m_vec = jnp.where(dir_rows == (d == 1), prep[4], m_vec)
        if emit_state:
            for idx in range(N_UNITS):
                caug = cst_s[idx]
                cout_ref[0, seq * N_UNITS + idx] = caug[:, :HEAD_DIM]
                nout_ref[0, seq * N_UNITS + idx:seq * N_UNITS + idx + 1, :] = caug[:, HEAD_DIM:].T[0:1, :]
            mout_ref[0, seq * N_UNITS:(seq + 1) * N_UNITS, :] = jnp.broadcast_to(m_vec, (N_UNITS, LANES))

    e_iota = lax.broadcasted_iota(jnp.int32, (LANES, MIX_TM), 0)
    g_of_e = lax.shift_right_logical(e_iota, 2)
    j_of_e = lax.bitwise_and(e_iota, EXPERTS_PER_GROUP - 1)
    r8 = lax.broadcasted_iota(jnp.int32, (8, MIX_TM), 0)
    r8_blk = lax.broadcasted_iota(jnp.int32, (8, SUB), 0)
    before_b = (t_idx < s_idx).astype(F32).astype(BF16)

    def phase3(i, carry):
        r0 = pl.multiple_of(i * MIX_TM, MIX_TM)
        rows = pl.ds(r0, MIX_TM)
        hm = hm_s[rows, :]
        heads = []
        for hd in range(N_HEADS):
            hh = hm[:, hd * HEAD_DIM:(hd + 1) * HEAD_DIM]
            heads.append(hh * lax.rsqrt(jnp.mean(hh * hh, axis=-1, keepdims=True) + EPS))
        hn = jnp.concatenate(heads, axis=1) * w["hng"][...]
        hb2 = (so_s[rows, :] * hn).astype(BF16)
        br_b = _dot(hb2, w["wmo"][...])
        mixed = (ma_s[rows, :] + sgb_s[rows, :] * br_b).astype(BF16)
        x1 = x_ref[0, rows, :] + mod_row(2) * _dot(mixed, w["wo"][...])
        x1_ref[0, rows, :] = x1
        xn = x1 * lax.rsqrt(jnp.mean(x1 * x1, axis=-1, keepdims=True) + EPS) * w["g2"][...]
        h2 = xn * (1.0 + mod_row(4)) + mod_row(3)
        h2_ref[0, rows, :] = h2.astype(BF16)

        h2_hi = h2.astype(BF16)
        h2_lo = (h2 - h2_hi.astype(F32)).astype(BF16)
        lg = _dot(h2_hi, w["wrt2"][...])
        lg = lg[:, :LANES] + lg[:, LANES:] + _dot(h2_lo, w["wrt2"][:, :LANES])
        lt = lg.T + w["brtT"][...]
        gl = [lt[N_EXPERTS + g:N_EXPERTS + g + 1, :] for g in range(N_GROUPS)]
        best, gsel = gl[0], jnp.zeros((1, MIX_TM), jnp.int32)
        for g in range(1, N_GROUPS):
            better = gl[g] > best
            gsel = jnp.where(better, g, gsel)
            best = jnp.where(better, gl[g], best)
        gp_sel = 1.0 / sum(jnp.exp(v - best) for v in gl)
        el = []
        for j in range(EXPERTS_PER_GROUP):
            v = lt[j:j + 1, :]
            for g in range(1, N_GROUPS):
                r = g * EXPERTS_PER_GROUP + j
                v = jnp.where(gsel == g, lt[r:r + 1, :], v)
            el.append(v)
        l1, e1 = el[0], jnp.zeros((1, MIX_TM), jnp.int32)
        for j in range(1, EXPERTS_PER_GROUP):
            better = el[j] > l1
            e1 = jnp.where(better, j, e1)
            l1 = jnp.where(better, el[j], l1)
        l2 = jnp.full((1, MIX_TM), -jnp.inf, F32)
        e2 = jnp.zeros((1, MIX_TM), jnp.int32)
        for j in range(EXPERTS_PER_GROUP):
            better = jnp.logical_and(e1 != j, el[j] > l2)
            e2 = jnp.where(better, j, e2)
            l2 = jnp.where(better, el[j], l2)
        r2 = jnp.exp(l2 - l1)
        wt1 = gp_sel / (1.0 + r2)
        wt2 = gp_sel * r2 / (1.0 + r2)
        in_group = g_of_e == gsel
        comb_t = (jnp.where(jnp.logical_and(in_group, j_of_e == e1), wt1, 0.0)
                  + jnp.where(jnp.logical_and(in_group, j_of_e == e2), wt2, 0.0))

        onehot = (r8 == gsel).astype(F32)
        gsel_f = gsel.astype(F32)
        ranks = []
        for j in range(cpm):
            oh = onehot[:, j * SUB:(j + 1) * SUB]
            rank = jnp.sum(oh * _dot(oh.astype(BF16), before_b), axis=0, keepdims=True)
            ranks.append(rank)
            r8rows = pl.ds(pl.multiple_of((i * cpm + j) * 8, 8), 8)
            route_ref[0, r8rows, :] = jnp.where(r8_blk == 0, gsel_f[:, j * SUB:(j + 1) * SUB],
                                                jnp.where(r8_blk == 1, rank, 0.0))
            cnt_ref[0, r8rows, :] = jnp.broadcast_to(jnp.sum(oh, axis=1, keepdims=True), (8, LANES))
        comb_t = jnp.where(e_iota == ROUTE_GROUP_LANE, gsel_f,
                           jnp.where(e_iota == ROUTE_RANK_LANE, jnp.concatenate(ranks, axis=1), comb_t))
        comb_ref[0, rows, :] = comb_t.T
        return carry

    if n_mt == 1:
        phase3(0, 0)
    else:
        lax.fori_loop(0, n_mt, phase3, 0)


def _const_spec(a):
    nd = a.ndim
    return pl.BlockSpec(a.shape, lambda b, _nd=nd: (0,) * _nd, pipeline_mode=pl.Buffered(1))


def _mixer(x, T, mod, mod_index, weights, P, state=None, emit_state=False):
    B, R, _ = x.shape
    n_chunks = R // SUB
    n_seq = R // T
    has_state = state is not None
    seq_mode = {} if R <= MIX_TM else {"pipeline_mode": pl.Buffered(1)}
    in_specs = [
        pl.BlockSpec((1, R, D_MODEL), lambda b: (b, 0, 0), **seq_mode),
        pl.BlockSpec((1, N_ADA, D_MODEL), lambda b: (mod_index(b), 0, 0)),
    ]
    args = [x, mod]
    if has_state:
        c0, n0, m0 = state
        in_specs += [
            pl.BlockSpec((1, N_UNITS, HEAD_DIM, HEAD_DIM), lambda b: (b, 0, 0, 0)),
            pl.BlockSpec((1, N_UNITS, HEAD_DIM), lambda b: (b, 0, 0)),
            pl.BlockSpec((1, N_UNITS, LANES), lambda b: (b, 0, 0)),
        ]
        args += [c0, n0, m0]
    for name in _MIXER_WEIGHTS:
        in_specs.append(_const_spec(weights[name]))
        args.append(weights[name])
    out_shape = [
        jax.ShapeDtypeStruct((B, R, D_MODEL), F32),
        jax.ShapeDtypeStruct((B, R, D_MODEL), BF16),
        jax.ShapeDtypeStruct((B, R, LANES), F32),
        jax.ShapeDtypeStruct((B, n_chunks * 8, SUB), F32),
        jax.ShapeDtypeStruct((B, n_chunks * 8, LANES), F32),
    ]
    out_specs = [
        pl.BlockSpec((1, R, D_MODEL), lambda b: (b, 0, 0), **seq_mode),
        pl.BlockSpec((1, R, D_MODEL), lambda b: (b, 0, 0), **seq_mode),
        pl.BlockSpec((1, R, LANES), lambda b: (b, 0, 0)),
        pl.BlockSpec((1, n_chunks * 8, SUB), lambda b: (b, 0, 0)),
        pl.BlockSpec((1, n_chunks * 8, LANES), lambda b: (b, 0, 0)),
    ]
    if emit_state:
        out_shape += [
            jax.ShapeDtypeStruct((B, n_seq * N_UNITS, HEAD_DIM, HEAD_DIM), F32),
            jax.ShapeDtypeStruct((B, n_seq * N_UNITS, HEAD_DIM), F32),
            jax.ShapeDtypeStruct((B, n_seq * N_UNITS, LANES), F32),
        ]
        out_specs += [
            pl.BlockSpec((1, n_seq * N_UNITS, HEAD_DIM, HEAD_DIM), lambda b: (b, 0, 0, 0)),
            pl.BlockSpec((1, n_seq * N_UNITS, HEAD_DIM), lambda b: (b, 0, 0)),
            pl.BlockSpec((1, n_seq * N_UNITS, LANES), lambda b: (b, 0, 0)),
        ]
    scratch = [
        pltpu.VMEM((R, D_MLSTM), BF16),
        pltpu.VMEM((n_chunks, D_MLSTM, SUB), BF16),
        pltpu.VMEM((R, D_MLSTM), BF16),
        pltpu.VMEM((R, D_MLSTM), F32),
        pltpu.VMEM((n_chunks, 5 * N_UNITS, SUB), F32),
        pltpu.VMEM((R, D_MODEL), F32),
        pltpu.VMEM((R, D_MODEL), F32),
        pltpu.VMEM((R, D_MLSTM), F32),
        pltpu.VMEM((N_UNITS, HEAD_DIM, 2 * HEAD_DIM), F32),
        pltpu.VMEM((MIX_TM // P, P + 2 * CONV_PAD, D_CONV), F32),
    ]
    return pl.pallas_call(
        functools.partial(_mixer_kernel, R, T, P, has_state, emit_state),
        grid=(B,),
        in_specs=in_specs,
        out_specs=out_specs,
        out_shape=out_shape,
        scratch_shapes=scratch,
        compiler_params=pltpu.CompilerParams(
            dimension_semantics=("arbitrary",), vmem_limit_bytes=VMEM_LIMIT),
        name="mixer_T%d" % T,
    )(*args)


def _dest_in_block(group, rank, starts):
    dest = rank
    for g in range(N_GROUPS):
        dest = dest + jnp.where(group == float(g), starts[g], 0.0)
    return dest


def _copy_segments(src_refs, dst_refs, src_starts, dst_starts, n_pieces):
    for g in range(N_GROUPS):
        def body(k, carry, g=g):
            s = pl.multiple_of(src_starts[g] + k * ROW_ALIGN, ROW_ALIGN)
            d = pl.multiple_of(dst_starts[g] + k * ROW_ALIGN, ROW_ALIGN)
            for src, dst in zip(src_refs, dst_refs):
                dst[pl.ds(d, ROW_ALIGN), :] = src[pl.ds(s, ROW_ALIGN), :]
            return carry
        lax.fori_loop(0, n_pieces[g], body, 0)


def _plan_segments(n_blocks, n_tiles, cnt_ref, start_ref, npiece_ref, off_ref, tgroup_ref, tvalid_ref, tfirst_ref):
    align_shift = ROW_ALIGN.bit_length() - 1
    tile_shift = MOE_TM.bit_length() - 1

    def block_starts(blk, carry):
        row = jnp.int32(0)
        for g in range(N_GROUPS):
            n = lax.shift_right_logical(cnt_ref[blk * N_GROUPS + g] + (ROW_ALIGN - 1), align_shift)
            npiece_ref[blk * N_GROUPS + g] = n
            start_ref[blk * N_GROUPS + g] = row
            row = row + n * ROW_ALIGN
        return carry

    lax.fori_loop(0, n_blocks, block_starts, 0)

    base_row = jnp.int32(0)
    base_tile = jnp.int32(0)
    last_group = jnp.int32(0)
    for g in range(N_GROUPS):
        def seg_offsets(blk, row, g=g, base_row=base_row):
            off_ref[blk * N_GROUPS + g] = base_row + row
            return row + npiece_ref[blk * N_GROUPS + g] * ROW_ALIGN

        rows = lax.fori_loop(0, n_blocks, seg_offsets, jnp.int32(0))
        tiles = lax.shift_right_logical(rows + (MOE_TM - 1), tile_shift)

        def mark_tiles(t, carry, g=g, base_tile=base_tile):
            tgroup_ref[base_tile + t] = g
            tvalid_ref[base_tile + t] = 1
            tfirst_ref[base_tile + t] = (t == 0).astype(jnp.int32)
            return carry

        lax.fori_loop(0, tiles, mark_tiles, 0)
        last_group = jnp.where(tiles > 0, g, last_group)
        base_row = base_row + tiles * MOE_TM
        base_tile = base_tile + tiles

    def mark_unused(t, carry):
        tgroup_ref[t] = last_group
        tvalid_ref[t] = 0
        tfirst_ref[t] = 0
        return carry

    lax.fori_loop(base_tile, n_tiles, mark_unused, 0)


def _dispatch_kernel(n_ctx_blocks, n_blocks, n_tiles, cnt_ref,
                     h2c_ref, h2l_ref, cbc_ref, cbl_ref, rtc_ref, rtl_ref,
                     xs_ref, cs_ref, start_ref, npiece_ref, off_ref, tgroup_ref, tvalid_ref, tfirst_ref,
                     sx_s, sc_s):
    b = pl.program_id(0)
    is_ctx = b < n_ctx_blocks

    @pl.when(b == 0)
    def _():
        _plan_segments(n_blocks, n_tiles, cnt_ref, start_ref, npiece_ref, off_ref,
                       tgroup_ref, tvalid_ref, tfirst_ref)
        xs_ref[...] = jnp.zeros_like(xs_ref)
        cs_ref[...] = jnp.zeros_like(cs_ref)

    h2 = jnp.where(is_ctx, h2c_ref[0], h2l_ref[0])
    cb = jnp.where(is_ctx, cbc_ref[0], cbl_ref[0])
    rt = jnp.where(is_ctx, rtc_ref[0], rtl_ref[0])
    starts = [start_ref[b * N_GROUPS + g] for g in range(N_GROUPS)]
    dest = _dest_in_block(rt[0:1, :], rt[1:2, :], [s.astype(F32) for s in starts])
    row = lax.broadcasted_iota(jnp.int32, (SORT_ROWS, SUB), 0).astype(F32)
    perm = (row == dest).astype(F32).astype(BF16)
    cb_hi = cb.astype(BF16)
    cb_lo = (cb - cb_hi.astype(F32)).astype(BF16)
    sx_s[...] = _dot(perm, h2).astype(BF16)
    sc_s[...] = _dot(perm, jnp.concatenate([cb_hi, cb_lo], axis=1)).astype(BF16)
    _copy_segments((sx_s, sc_s), (xs_ref, cs_ref), starts,
                   [off_ref[b * N_GROUPS + g] for g in range(N_GROUPS)],
                   [npiece_ref[b * N_GROUPS + g] for g in range(N_GROUPS)])


def _experts_kernel(tgroup_ref, tvalid_ref, tfirst_ref, xs_ref, cs_ref, wg_ref, wu_ref, wd_ref, ys_ref,
                    wg_s, wu_s, wd_s):
    i = pl.program_id(0)

    @pl.when(tfirst_ref[i] == 1)
    def _():
        for j in range(EXPERTS_PER_GROUP):
            cols = slice(j * D_EXPERT, (j + 1) * D_EXPERT)
            wg_s[:, cols] = wg_ref[j].astype(BF16)
            wu_s[:, cols] = wu_ref[j].astype(BF16)
            wd_s[cols, :] = wd_ref[j].astype(BF16)

    @pl.when(tvalid_ref[i] == 1)
    def _():
        x = xs_ref[...]
        g = _dot(x, wg_s[...])
        u = _dot(x, wu_s[...])
        comb = cs_ref[:, :LANES].astype(F32) + cs_ref[:, LANES:].astype(F32)
        lane = lax.broadcasted_iota(jnp.int32, comb.shape, 1)
        first = tgroup_ref[i] * EXPERTS_PER_GROUP
        parts = []
        for j in range(EXPERTS_PER_GROUP):
            cols = slice(j * D_EXPERT, (j + 1) * D_EXPERT)
            cw = jnp.sum(jnp.where(lane == first + j, comb, 0.0), axis=1, keepdims=True)
            gj = g[:, cols]
            parts.append((gj * _sigmoid(gj) * u[:, cols] * cw).astype(BF16))
        ys_ref[...] = _dot(jnp.concatenate(parts, axis=1), wd_s[...]).astype(BF16)

    @pl.when(tvalid_ref[i] == 0)
    def _():
        ys_ref[...] = jnp.zeros_like(ys_ref)


def _combine_kernel(n_ctx_blocks, blocks_per_lat_seq, start_ref, npiece_ref, off_ref,
                    x1c_ref, x1l_ref, cbc_ref, cbl_ref, ys_ref, mod_ref, gf_ref, yc_ref, yl_ref, loc_s):
    b = pl.program_id(0)
    is_ctx = b < n_ctx_blocks
    starts = [start_ref[b * N_GROUPS + g] for g in range(N_GROUPS)]
    loc_s[...] = jnp.zeros_like(loc_s)
    _copy_segments((ys_ref,), (loc_s,), [off_ref[b * N_GROUPS + g] for g in range(N_GROUPS)], starts,
                   [npiece_ref[b * N_GROUPS + g] for g in range(N_GROUPS)])
    cb = jnp.where(is_ctx, cbc_ref[0], cbl_ref[0])
    dest = _dest_in_block(cb[:, ROUTE_GROUP_LANE:ROUTE_GROUP_LANE + 1],
                          cb[:, ROUTE_RANK_LANE:ROUTE_RANK_LANE + 1],
                          [s.astype(F32) for s in starts])
    col = lax.broadcasted_iota(jnp.int32, (SUB, SORT_ROWS), 1).astype(F32)
    unperm = (col == dest).astype(F32).astype(BF16)
    moe = _dot(unperm, loc_s[...])
    x1 = jnp.where(is_ctx, x1c_ref[0], x1l_ref[0])
    mrow = jnp.where(is_ctx, 0, 1 + jnp.maximum(b - n_ctx_blocks, 0) // blocks_per_lat_seq)
    x2 = x1 + mod_ref[mrow, N_ADA - 1:N_ADA, :] * moe
    y = x2 * lax.rsqrt(jnp.mean(x2 * x2, axis=-1, keepdims=True) + EPS) * gf_ref[...]

    @pl.when(is_ctx)
    def _():
        yc_ref[0] = y

    @pl.when(jnp.logical_not(is_ctx))
    def _():
        yl_ref[0] = y


def _moe(x1c, x1l, h2c, h2l, cbc, cbl, rtc, rtl, cnt, mod, blocks_per_lat_seq, wg, wu, wd, gf):
    nc, nl = x1c.shape[0], x1l.shape[0]
    nb = nc + nl
    n_rows_max = nb * SUB + nb * N_GROUPS * (ROW_ALIGN - 1) + N_GROUPS * (MOE_TM - ROW_ALIGN)
    n_tiles = -(-n_rows_max // MOE_TM)
    ns = n_tiles * MOE_TM

    cmap = lambda b, *_: (jnp.minimum(b, nc - 1), 0, 0)
    lmap = lambda b, *_: (jnp.maximum(b - nc, 0), 0, 0)
    whole = lambda *_: (0, 0)
    once = {"pipeline_mode": pl.Buffered(1)}
    arb = pltpu.CompilerParams(dimension_semantics=("arbitrary",), vmem_limit_bytes=VMEM_LIMIT)
    smem = pl.BlockSpec(memory_space=pltpu.SMEM)
    seg_i32 = jax.ShapeDtypeStruct((nb * N_GROUPS,), jnp.int32)
    tile_i32 = jax.ShapeDtypeStruct((n_tiles,), jnp.int32)

    xs, cs, start, npiece, off, tgroup, tvalid, tfirst = pl.pallas_call(
        functools.partial(_dispatch_kernel, nc, nb, n_tiles),
        grid_spec=pltpu.PrefetchScalarGridSpec(
            num_scalar_prefetch=1, grid=(nb,),
            in_specs=[
                pl.BlockSpec((1, SUB, D_MODEL), cmap), pl.BlockSpec((1, SUB, D_MODEL), lmap),
                pl.BlockSpec((1, SUB, LANES), cmap), pl.BlockSpec((1, SUB, LANES), lmap),
                pl.BlockSpec((1, 8, SUB), cmap), pl.BlockSpec((1, 8, SUB), lmap),
            ],
            out_specs=[pl.BlockSpec((ns, D_MODEL), whole, **once), pl.BlockSpec((ns, 2 * LANES), whole, **once),
                       smem, smem, smem, smem, smem, smem],
            scratch_shapes=[pltpu.VMEM((SORT_ROWS, D_MODEL), BF16), pltpu.VMEM((SORT_ROWS, 2 * LANES), BF16)],
        ),
        out_shape=[jax.ShapeDtypeStruct((ns, D_MODEL), BF16), jax.ShapeDtypeStruct((ns, 2 * LANES), BF16),
                   seg_i32, seg_i32, seg_i32, tile_i32, tile_i32, tile_i32],
        compiler_params=arb,
        name="moe_dispatch",
    )(cnt, h2c, h2l, cbc, cbl, rtc, rtl)

    wmap = lambda i, tg, tv, tf: (tg[i], 0, 0)
    ys = pl.pallas_call(
        _experts_kernel,
        grid_spec=pltpu.PrefetchScalarGridSpec(
            num_scalar_prefetch=3, grid=(n_tiles,),
            in_specs=[
                pl.BlockSpec((MOE_TM, D_MODEL), lambda i, *_: (i, 0)),
                pl.BlockSpec((MOE_TM, 2 * LANES), lambda i, *_: (i, 0)),
                pl.BlockSpec((EXPERTS_PER_GROUP, D_MODEL, D_EXPERT), wmap),
                pl.BlockSpec((EXPERTS_PER_GROUP, D_MODEL, D_EXPERT), wmap),
                pl.BlockSpec((EXPERTS_PER_GROUP, D_EXPERT, D_MODEL), wmap),
            ],
            out_specs=pl.BlockSpec((MOE_TM, D_MODEL), lambda i, *_: (i, 0)),
            scratch_shapes=[pltpu.VMEM((D_MODEL, EXPERTS_PER_GROUP * D_EXPERT), BF16),
                            pltpu.VMEM((D_MODEL, EXPERTS_PER_GROUP * D_EXPERT), BF16),
                            pltpu.VMEM((EXPERTS_PER_GROUP * D_EXPERT, D_MODEL), BF16)],
        ),
        out_shape=jax.ShapeDtypeStruct((ns, D_MODEL), BF16),
        compiler_params=arb,
        name="moe_experts",
    )(tgroup, tvalid, tfirst, xs, cs, wg, wu, wd)

    yc, yl = pl.pallas_call(
        functools.partial(_combine_kernel, nc, blocks_per_lat_seq),
        grid_spec=pltpu.PrefetchScalarGridSpec(
            num_scalar_prefetch=3, grid=(nb,),
            in_specs=[
                pl.BlockSpec((1, SUB, D_MODEL), cmap), pl.BlockSpec((1, SUB, D_MODEL), lmap),
                pl.BlockSpec((1, SUB, LANES), cmap), pl.BlockSpec((1, SUB, LANES), lmap),
                pl.BlockSpec((ns, D_MODEL), whole, **once),
                pl.BlockSpec(mod.shape, lambda *_: (0, 0, 0)),
                pl.BlockSpec((1, D_MODEL), whole),
            ],
            out_specs=[pl.BlockSpec((1, SUB, D_MODEL), cmap), pl.BlockSpec((1, SUB, D_MODEL), lmap)],
            scratch_shapes=[pltpu.VMEM((SORT_ROWS, D_MODEL), BF16)],
        ),
        out_shape=[jax.ShapeDtypeStruct((nc, SUB, D_MODEL), F32), jax.ShapeDtypeStruct((nl, SUB, D_MODEL), F32)],
        compiler_params=arb,
        name="moe_combine",
    )(start, npiece, off, x1c, x1l, cbc, cbl, ys, mod, gf)
    return yc, yl


def _prep_weights(norm1_g, w_in, b_in, b_gates, w_dw, b_dw, conv_ln_g, conv_ln_b, w_conv_out,
                  mlstm_hn_g, w_mlstm_out, w_o, norm2_g, w_rg, b_rg, w_re, b_re):
    s_a = 2 * D_CONV
    s_q = s_a + D_MLSTM
    s_k = s_q + D_MLSTM
    s_v = s_k + D_MLSTM
    s_o = s_v + D_MLSTM
    s_g = s_o + 4 * N_HEADS
    row = lambda v: v.reshape(1, -1).astype(F32)
    w_t = w_in.T
    wb = lambda a, b: w_t[a:b].astype(BF16).T
    w_g = w_t[s_o:s_g].reshape(2, 2, N_HEADS, D_MODEL).transpose(1, 0, 2, 3).reshape(4 * N_HEADS, D_MODEL)
    bg = (b_in[s_o:s_g] + b_gates.reshape(-1)).reshape(2, 2, N_HEADS).transpose(1, 0, 2).reshape(-1, 1)
    n_rt = N_EXPERTS + N_GROUPS
    wrt = jnp.pad(jnp.concatenate([w_re, w_rg], axis=1), ((0, 0), (0, LANES - n_rt)))
    wrt_hi = wrt.astype(BF16)
    wrt2 = jnp.concatenate([wrt_hi, (wrt - wrt_hi.astype(F32)).astype(BF16)], axis=1)
    brtT = jnp.pad(jnp.concatenate([b_re, b_rg]), (0, LANES - n_rt)).reshape(LANES, 1)
    return {
        "g1": row(norm1_g),
        "wag": wb(0, s_a), "bag": row(b_in[:s_a]),
        "wq": wb(s_a, s_q), "bq": row(b_in[s_a:s_q]),
        "wkT": w_t[s_q:s_k].astype(BF16), "bk": b_in[s_q:s_k].reshape(-1, 1),
        "wv": wb(s_k, s_v), "bv": row(b_in[s_k:s_v]),
        "wog": wb(s_v, s_o), "bog": row(b_in[s_v:s_o]),
        "wgifT": w_g.astype(BF16), "bgifT": bg,
        "wgm": wb(s_g, w_in.shape[1]), "bgm": row(b_in[s_g:]),
        "wdw": w_dw.astype(F32), "bdw": row(b_dw), "lng": row(conv_ln_g), "lnb": row(conv_ln_b),
        "wco": w_conv_out.astype(BF16), "hng": row(mlstm_hn_g), "wmo": w_mlstm_out.astype(BF16),
        "wo": w_o.astype(BF16), "g2": row(norm2_g), "wrt2": wrt2, "brtT": brtT,
    }


def kernel(x_prompt, x_sample, state_C, state_n, state_m, c, c_ctx, norm1_g, w_ada, b_ada, w_in, b_in, b_gates, w_dw, b_dw, conv_ln_g, conv_ln_b, w_conv_out, mlstm_hn_g, w_mlstm_out, w_o, norm2_g, w_rg, b_rg, w_re, b_re, w_e_gate, w_e_up, w_e_down, norm_final_g):
    B, S, _ = x_prompt.shape
    Bd, Sd, _ = x_sample.shape
    assert w_ada.shape[0] == 1, "single trunk layer"
    assert S == SUB and Sd % SUB == 0

    cin = jnp.concatenate([c_ctx[None, :], c, jnp.zeros((8 - 1 - Bd, D_MODEL), F32)], axis=0)
    mod = _ada(cin, w_ada[0], b_ada[0].reshape(1, -1)).reshape(8, N_ADA, D_MODEL)

    wts = _prep_weights(norm1_g[0], w_in[0], b_in[0], b_gates[0], w_dw[0], b_dw[0], conv_ln_g[0],
                        conv_ln_b[0], w_conv_out[0], mlstm_hn_g[0], w_mlstm_out[0], w_o[0],
                        norm2_g[0], w_rg[0], b_rg[0], w_re[0], b_re[0])

    x1p, h2p, cbp, rtp, cntp, c_new, n_new, m_new = _mixer(
        x_prompt.reshape(B * S // MIX_TM, MIX_TM, D_MODEL), S, mod, lambda b: 0, wts, P=S, emit_state=True)

    m0 = jnp.broadcast_to(state_m[:, 0].reshape(Bd, N_UNITS, 1), (Bd, N_UNITS, LANES))
    state = (state_C[:, 0].reshape(Bd, N_UNITS, HEAD_DIM, HEAD_DIM), state_n[:, 0].reshape(Bd, N_UNITS, HEAD_DIM), m0)
    x1s, h2s, cbs, rts, cnts = _mixer(x_sample, Sd, mod, lambda b: 1 + b, wts, P=GRID_W, state=state)

    nc, nl = B * S // SUB, Bd * Sd // SUB
    blk = lambda a, n: a.reshape(n, SUB, a.shape[-1])
    cnt = jnp.concatenate([cntp.reshape(nc, 8, LANES)[:, :N_GROUPS, 0],
                           cnts.reshape(nl, 8, LANES)[:, :N_GROUPS, 0]], axis=0)
    yp, ys = _moe(blk(x1p, nc), blk(x1s, nl), blk(h2p, nc), blk(h2s, nl), blk(cbp, nc), blk(cbs, nl),
                  rtp.reshape(nc, 8, SUB), rts.reshape(nl, 8, SUB),
                  cnt.astype(jnp.int32).reshape(-1), mod, Sd // SUB, w_e_gate[0], w_e_up[0], w_e_down[0],
                  norm_final_g.reshape(1, -1))

    return (yp.reshape(B, S, D_MODEL), ys.reshape(Bd, Sd, D_MODEL),
            c_new.reshape(B, 1, 2, N_HEADS, HEAD_DIM, HEAD_DIM),
            n_new.reshape(B, 1, 2, N_HEADS, HEAD_DIM),
            m_new[:, :, 0].reshape(B, 1, 2, N_HEADS))
```

```python
import functools

import jax
import jax.numpy as jnp
from jax import lax
from jax.experimental import pallas as pl
from jax.experimental.pallas import tpu as pltpu

D_MODEL = 1024
D_CONV = 512
CONV_K = 31
D_MLSTM = 512
N_HEADS = 4
HEAD_DIM = D_MLSTM // N_HEADS
N_GROUPS = 4
EXPERTS_PER_GROUP = 4
N_EXPERTS = N_GROUPS * EXPERTS_PER_GROUP
D_EXPERT = 256
N_ADA = 6
EPS = 1e-6
GRID_W = 64

LANES = 128
SUB = 256
CONV_PAD = 16
CONV_RB = 64
N_UNITS = 2 * N_HEADS
ROW_ALIGN = 16
SORT_ROWS = SUB + N_GROUPS * ROW_ALIGN
MOE_TM = 512
MIX_TM = 512
WPREP_ROWS = 256
ROUTE_GROUP_LANE = N_EXPERTS
ROUTE_RANK_LANE = N_EXPERTS + 1
VMEM_LIMIT = 58 * 1024 * 1024

BF16 = jnp.bfloat16
F32 = jnp.float32
NT_DIMS = (((1,), (1,)), ((), ()))


def _dot(a, b):
    return jnp.dot(a, b, preferred_element_type=F32)


def _dot_nt(a, b, precision=None):
    return lax.dot_general(a, b, NT_DIMS, preferred_element_type=F32, precision=precision)


def _sigmoid(x):
    return 0.5 * jnp.tanh(0.5 * x) + 0.5


def _log_sigmoid(x):
    return jnp.minimum(x, 0.0) - jnp.log1p(jnp.exp(-jnp.abs(x)))


def _split3(x):
    hi = x.astype(BF16).astype(F32)
    r1 = x - hi
    mid = r1.astype(BF16).astype(F32)
    lo = (r1 - mid).astype(BF16).astype(F32)
    return hi, mid, lo


def _ada_kernel(c_ref, w_ref, b_ref, o_ref):
    c = c_ref[...]
    s = (c * _sigmoid(c)).astype(BF16)
    o_ref[...] = _dot(s, w_ref[...].astype(BF16)) + b_ref[...]


def _ada(cin, w_ada, b_ada):
    n = w_ada.shape[1]
    bn = 1024
    return pl.pallas_call(
        _ada_kernel,
        grid=(n // bn,),
        in_specs=[
            pl.BlockSpec((8, D_MODEL), lambda j: (0, 0)),
            pl.BlockSpec((D_MODEL, bn), lambda j: (0, j)),
            pl.BlockSpec((1, bn), lambda j: (0, j)),
        ],
        out_specs=pl.BlockSpec((8, bn), lambda j: (0, j)),
        out_shape=jax.ShapeDtypeStruct((8, n), F32),
        compiler_params=pltpu.CompilerParams(dimension_semantics=("arbitrary",)),
        name="ada",
    )(cin, w_ada, b_ada)


def _transpose_cast_kernel(starts_ref, wt_ref, o_ref):
    o_ref[...] = wt_ref[...].T.astype(BF16)


def _transpose_cast(w_t, row_starts):
    n, k = len(row_starts), w_t.shape[1]
    return pl.pallas_call(
        _transpose_cast_kernel,
        grid_spec=pltpu.PrefetchScalarGridSpec(
            num_scalar_prefetch=1, grid=(n,),
            in_specs=[pl.BlockSpec((pl.Element(WPREP_ROWS), pl.Element(k)), lambda j, starts: (starts[j] * 8, 0))],
            out_specs=pl.BlockSpec((k, WPREP_ROWS), lambda j, starts: (0, j)),
        ),
        out_shape=jax.ShapeDtypeStruct((k, n * WPREP_ROWS), BF16),
        compiler_params=pltpu.CompilerParams(dimension_semantics=("arbitrary",)),
        name="transpose_cast",
    )(jnp.array([r // 8 for r in row_starts], jnp.int32), w_t)


WROW_OFFSET = {"wq": 2 * D_CONV, "wv": 2 * D_CONV + D_MLSTM, "wog": 2 * D_CONV + 2 * D_MLSTM,
               "wgm": 2 * D_CONV + 3 * D_MLSTM}

_MIXER_WEIGHTS = (
    "g1", "wrow", "bag", "bq", "wkT", "bk", "bv", "bog",
    "wgifT", "bgifT", "bgm", "wdw", "bdw", "lng", "lnb",
    "wco", "hng", "wmo", "wo", "g2", "wrt2", "brtT",
)


def _conv_block(upad_s, seg, base, cs, wdw_ref, bdw_ref):
    sub = 8
    first = CONV_PAD - CONV_K // 2
    acc = jnp.broadcast_to(bdw_ref[0:1, cs], (CONV_RB, LANES))
    for r in range(sub):
        z = None
        for a in range((CONV_K + first + sub - 1) // sub):
            j = sub * a + r - first
            if 0 <= j < CONV_K:
                lo = base + sub * a
                term = wdw_ref[j:j + 1, cs] * upad_s[seg, lo:lo + CONV_RB + sub, cs]
                z = term if z is None else z + term
        acc = acc + z[r:r + CONV_RB, :]
    return acc


def _mixer_kernel(R, T, P, has_state, emit_state, *refs):
    L = SUB
    n_mt = R // MIX_TM
    cpm = MIX_TM // L
    n_seq = R // T
    cps = T // L
    nseg = MIX_TM // P
    assert not has_state or n_seq == 1
    it = iter(refs)
    x_ref = next(it)
    mod_ref = next(it)
    if has_state:
        c0_ref = next(it)
        n0_ref = next(it)
        m0_ref = next(it)
    w = {name: next(it) for name in _MIXER_WEIGHTS}
    x1_ref = next(it)
    h2_ref = next(it)
    comb_ref = next(it)
    route_ref = next(it)
    cnt_ref = next(it)
    if emit_state:
        cout_ref = next(it)
        nout_ref = next(it)
        mout_ref = next(it)
    (q_s, kT_s, v_s, so_s, scan_s, ma_s, sgb_s, hm_s, cst_s, upad_s) = [next(it) for _ in range(10)]

    def mod_row(i):
        return mod_ref[0, i:i + 1, :]

    zpad = jnp.zeros((CONV_PAD, D_CONV), F32)
    for seg in range(nseg):
        upad_s[seg, 0:CONV_PAD, :] = zpad
        upad_s[seg, CONV_PAD + P:CONV_PAD + P + CONV_PAD, :] = zpad

    t_idx = lax.broadcasted_iota(jnp.int32, (L, L), 0)
    s_idx = lax.broadcasted_iota(jnp.int32, (L, L), 1)
    lower = s_idx <= t_idx
    upper = s_idx >= t_idx
    triu_b = upper.astype(F32).astype(BF16)
    lane_u = lax.broadcasted_iota(jnp.int32, (N_UNITS, L), 1)
    is_bwd = lax.broadcasted_iota(jnp.int32, (N_UNITS, L), 0) >= N_HEADS

    def gate_scan(g):
        gi, lf = g[:N_UNITS], _log_sigmoid(g[N_UNITS:])
        pr = _dot(jnp.concatenate(_split3(lf), axis=0).astype(BF16), triu_b)
        pre = pr[0:N_UNITS] + pr[N_UNITS:2 * N_UNITS] + pr[2 * N_UNITS:]
        tot = pre[:, L - 1:L]
        bsum = jnp.where(is_bwd, tot - pre + lf, pre)
        a = gi - bsum
        pm, sm, k = a, a, 1
        while k < L:
            pm = jnp.where(lane_u >= k, jnp.maximum(pm, pltpu.roll(pm, k, axis=1)), pm)
            sm = jnp.where(lane_u < L - k, jnp.maximum(sm, pltpu.roll(sm, L - k, axis=1)), sm)
            k *= 2
        wide = lambda v: jnp.broadcast_to(v, (N_UNITS, L))
        return jnp.concatenate([a, jnp.where(is_bwd, sm, pm), bsum, wide(tot),
                                wide(jnp.max(a, axis=1, keepdims=True))], axis=0)

    def phase1(i, carry):
        r0 = pl.multiple_of(i * MIX_TM, MIX_TM)
        rows = pl.ds(r0, MIX_TM)
        x = x_ref[0, rows, :]
        xn = x * lax.rsqrt(jnp.mean(x * x, axis=-1, keepdims=True) + EPS) * w["g1"][...]
        hb = (xn * (1.0 + mod_row(1)) + mod_row(0)).astype(BF16)

        gates = _dot_nt(w["wgifT"][...], hb) + w["bgifT"][...]
        for j in range(cpm):
            scan_s[i * cpm + j] = gate_scan(gates[:, j * L:(j + 1) * L])
        ag = _dot(hb, w["wrow"][:, :2 * D_CONV]) + w["bag"][...]
        u = ag[:, :D_CONV] * _sigmoid(ag[:, D_CONV:])
        for seg in range(nseg):
            upad_s[seg, CONV_PAD:CONV_PAD + P, :] = u[seg * P:(seg + 1) * P, :]

        def proj(name, bias, c0, width=2 * LANES):
            w0 = WROW_OFFSET[name] + c0
            return _dot(hb, w["wrow"][:, w0:w0 + width]) + w[bias][:, c0:c0 + width]

        def gm_a(c0):
            ma_s[rows, c0:c0 + 2 * LANES] = _sigmoid(proj("wgm", "bgm", c0))

        def gm_b(c0):
            sgb_s[rows, c0:c0 + 2 * LANES] = _sigmoid(proj("wgm", "bgm", D_MODEL + c0))

        def q_part(c0):
            q_s[rows, c0:c0 + 2 * LANES] = (proj("wq", "bq", c0) * (HEAD_DIM ** -0.5)).astype(BF16)

        def v_part(c0):
            v_s[rows, c0:c0 + 2 * LANES] = proj("wv", "bv", c0).astype(BF16)

        def o_part(c0):
            so_s[rows, c0:c0 + 2 * LANES] = _sigmoid(proj("wog", "bog", c0))

        def k_part(c0):
            rs = slice(c0, c0 + 2 * LANES)
            kt = (_dot_nt(w["wkT"][rs, :], hb) + w["bk"][rs, :]).astype(BF16)
            for j in range(cpm):
                kT_s[i * cpm + j, rs, :] = kt[:, j * L:(j + 1) * L]

        jobs = ([functools.partial(gm_a, c0) for c0 in range(0, D_MODEL, 2 * LANES)]
                + [functools.partial(gm_b, c0) for c0 in range(0, D_MODEL, 2 * LANES)]
                + [functools.partial(f, c0) for f in (q_part, k_part, v_part, o_part)
                   for c0 in range(0, D_MLSTM, 2 * LANES)])
        n_jobs = len(jobs)
        conv = {}
        n_pieces = (D_CONV // LANES) * nseg * (P // CONV_RB)
        for cb in range(D_CONV // LANES):
            cs = slice(cb * LANES, (cb + 1) * LANES)
            for seg in range(nseg):
                for rb in range(P // CONV_RB):
                    conv[(cb, seg, rb)] = _conv_block(upad_s, seg, rb * CONV_RB, cs, w["wdw"], w["bdw"])
                    if jobs and len(conv) * n_jobs >= (n_jobs - len(jobs) + 1) * n_pieces:
                        jobs.pop(0)()
        for job in jobs:
            job()
        cu = jnp.concatenate(
            [jnp.concatenate([conv[(cb, seg, rb)] for seg in range(nseg) for rb in range(P // CONV_RB)], axis=0)
             for cb in range(D_CONV // LANES)], axis=1)
        mu = jnp.mean(cu, axis=-1, keepdims=True)
        cc = cu - mu
        cn = cc * lax.rsqrt(jnp.mean(cc * cc, axis=-1, keepdims=True) + EPS) * w["lng"][...] + w["lnb"][...]
        ca = (cn * _sigmoid(cn)).astype(BF16)
        ma_s[rows, :] = ma_s[rows, :] * _dot(ca, w["wco"][...])
        return carry

    if n_mt == 1:
        phase1(0, 0)
    else:
        lax.fori_loop(0, n_mt, phase1, 0)

    ones_col = (lax.broadcasted_iota(jnp.int32, (L, HEAD_DIM), 1) == 0).astype(F32).astype(BF16)
    pad_rows = jnp.zeros((LANES - 3 * N_UNITS, L), F32)

    def gate_prep(c, m_vec):
        sc = scan_s[c]
        a, run_max, bsum = sc[0:N_UNITS], sc[N_UNITS:2 * N_UNITS], sc[2 * N_UNITS:3 * N_UNITS]
        tot, a_max = sc[3 * N_UNITS:4 * N_UNITS, 0:1], sc[4 * N_UNITS:5 * N_UNITS, 0:1]
        big_m = jnp.maximum(m_vec, run_max)
        m_end = jnp.maximum(m_vec, a_max)
        cols = jnp.concatenate(
            [big_m, jnp.exp(m_vec - big_m), jnp.exp(-bsum - big_m), pad_rows], axis=0).T
        return a, cols, jnp.exp(a - m_end), jnp.exp(m_vec - m_end), tot + m_end

    qk_cache = {}

    def unit(d, hd, c, prep, first_chunk, want_state):
        a, cols, wk, decay, _ = prep
        rows = slice(c * L, (c + 1) * L)
        hs = slice(hd * HEAD_DIM, (hd + 1) * HEAD_DIM)
        idx = d * N_HEADS + hd
        col = lambda k: cols[:, k * N_UNITS + idx:k * N_UNITS + idx + 1]
        qc = q_s[rows, hs]
        kTc = kT_s[c, hs, :]
        vaug = jnp.concatenate([v_s[rows, hs], ones_col], axis=1)
        if cps == 1 and (hd, c) in qk_cache:
            qk = qk_cache[(hd, c)]
        else:
            qk = _dot(qc, kTc)
            qk_cache[(hd, c)] = qk
        w_intra = jnp.where(lower if d == 0 else upper, jnp.exp(a[idx:idx + 1, :] - col(0)), 0.0)
        nd = _dot((qk * w_intra).astype(BF16), vaug)
        if has_state or not first_chunk:
            nd = nd + col(1) * _dot(qc, cst_s[idx].astype(BF16))
        den = nd[:, HEAD_DIM:HEAD_DIM + 1]
        h = nd[:, :HEAD_DIM] * (1.0 / jnp.maximum(jnp.abs(den), col(2)))
        if d == 0:
            hm_s[rows, hs] = h
        else:
            hm_s[rows, hs] = hm_s[rows, hs] + h
        if want_state:
            kw = (kTc.astype(F32) * wk[idx:idx + 1, :]).astype(BF16)
            upd = _dot(kw, vaug)
            if has_state or not first_chunk:
                upd = upd + decay[idx:idx + 1, :] * cst_s[idx]
            cst_s[idx] = upd

    dir_rows = lax.broadcasted_iota(jnp.int32, (N_UNITS, 1), 0) >= N_HEADS
    for seq in range(n_seq):
        if has_state:
            n_cols = jnp.concatenate([n0_ref[0], jnp.zeros((LANES - N_UNITS, HEAD_DIM), F32)], axis=0).T
            first_lane = lax.broadcasted_iota(jnp.int32, (HEAD_DIM, HEAD_DIM), 1) == 0
            for idx in range(N_UNITS):
                cst_s[idx, :, :HEAD_DIM] = c0_ref[0, idx]
                cst_s[idx, :, HEAD_DIM:] = jnp.where(first_lane, n_cols[:, idx:idx + 1], 0.0)
            m_vec = m0_ref[0, :, 0:1]
        else:
            m_vec = jnp.zeros((N_UNITS, 1), F32)
        prep = None
        for d in range(2):
            order = list(range(cps)) if d == 0 else list(range(cps - 1, -1, -1))
            for pos, c in enumerate(order):
                if cps > 1 or prep is None:
                    prep = gate_prep(seq * cps + c, m_vec)
                for hd in range(N_HEADS):
                    unit(d, hd, seq * cps + c, prep, pos == 0, emit_state or pos < cps - 1)
                m_vec = jnp.where(dir_rows == (d == 1), prep[4], m_vec)
        if emit_state:
            for idx in range(N_UNITS):
                caug = cst_s[idx]
                cout_ref[0, seq * N_UNITS + idx] = caug[:, :HEAD_DIM]
                nout_ref[0, seq * N_UNITS + idx:seq * N_UNITS + idx + 1, :] = caug[:, HEAD_DIM:].T[0:1, :]
            mout_ref[0, seq * N_UNITS:(seq + 1) * N_UNITS, :] = jnp.broadcast_to(m_vec, (N_UNITS, LANES))

    e_iota = lax.broadcasted_iota(jnp.int32, (LANES, MIX_TM), 0)
    g_of_e = lax.shift_right_logical(e_iota, 2)
    j_of_e = lax.bitwise_and(e_iota, EXPERTS_PER_GROUP - 1)
    r8 = lax.broadcasted_iota(jnp.int32, (8, MIX_TM), 0)
    r8_blk = lax.broadcasted_iota(jnp.int32, (8, SUB), 0)
    before_b = (t_idx < s_idx).astype(F32).astype(BF16)

    def phase3(i, carry):
        r0 = pl.multiple_of(i * MIX_TM, MIX_TM)
        rows = pl.ds(r0, MIX_TM)
        hm = hm_s[rows, :]
        heads = []
        for hd in range(N_HEADS):
            hh = hm[:, hd * HEAD_DIM:(hd + 1) * HEAD_DIM]
            heads.append(hh * lax.rsqrt(jnp.mean(hh * hh, axis=-1, keepdims=True) + EPS))
        hn = jnp.concatenate(heads, axis=1) * w["hng"][...]
        hb2 = (so_s[rows, :] * hn).astype(BF16)
        br_b = _dot(hb2, w["wmo"][...])
        mixed = (ma_s[rows, :] + sgb_s[rows, :] * br_b).astype(BF16)
        x1 = x_ref[0, rows, :] + mod_row(2) * _dot(mixed, w["wo"][...])
        x1_ref[0, rows, :] = x1
        xn = x1 * lax.rsqrt(jnp.mean(x1 * x1, axis=-1, keepdims=True) + EPS) * w["g2"][...]
        h2 = xn * (1.0 + mod_row(4)) + mod_row(3)
        h2_ref[0, rows, :] = h2.astype(BF16)

        h2_hi = h2.astype(BF16)
        h2_lo = (h2 - h2_hi.astype(F32)).astype(BF16)
        lg = _dot(h2_hi, w["wrt2"][...])
        lg = lg[:, :LANES] + lg[:, LANES:] + _dot(h2_lo, w["wrt2"][:, :LANES])
        lt = lg.T + w["brtT"][...]
        gl = [lt[N_EXPERTS + g:N_EXPERTS + g + 1, :] for g in range(N_GROUPS)]
        best, gsel = gl[0], jnp.zeros((1, MIX_TM), jnp.int32)
        for g in range(1, N_GROUPS):
            better = gl[g] > best
            gsel = jnp.where(better, g, gsel)
            best = jnp.where(better, gl[g], best)
        gp_sel = 1.0 / sum(jnp.exp(v - best) for v in gl)
        el = []
        for j in range(EXPERTS_PER_GROUP):
            v = lt[j:j + 1, :]
            for g in range(1, N_GROUPS):
                r = g * EXPERTS_PER_GROUP + j
                v = jnp.where(gsel == g, lt[r:r + 1, :], v)
            el.append(v)
        l1, e1 = el[0], jnp.zeros((1, MIX_TM), jnp.int32)
        for j in range(1, EXPERTS_PER_GROUP):
            better = el[j] > l1
            e1 = jnp.where(better, j, e1)
            l1 = jnp.where(better, el[j], l1)
        l2 = jnp.full((1, MIX_TM), -jnp.inf, F32)
        e2 = jnp.zeros((1, MIX_TM), jnp.int32)
        for j in range(EXPERTS_PER_GROUP):
            better = jnp.logical_and(e1 != j, el[j] > l2)
            e2 = jnp.where(better, j, e2)
            l2 = jnp.where(better, el[j], l2)
        r2 = jnp.exp(l2 - l1)
        wt1 = gp_sel / (1.0 + r2)
        wt2 = gp_sel * r2 / (1.0 + r2)
        in_group = g_of_e == gsel
        comb_t = (jnp.where(jnp.logical_and(in_group, j_of_e == e1), wt1, 0.0)
                  + jnp.where(jnp.logical_and(in_group, j_of_e == e2), wt2, 0.0))

        onehot = (r8 == gsel).astype(F32)
        gsel_f = gsel.astype(F32)
        ranks = []
        for j in range(cpm):
            oh = onehot[:, j * SUB:(j + 1) * SUB]
            rank = jnp.sum(oh * _dot(oh.astype(BF16), before_b), axis=0, keepdims=True)
            ranks.append(rank)
            r8rows = pl.ds(pl.multiple_of((i * cpm + j) * 8, 8), 8)
            route_ref[0, r8rows, :] = jnp.where(r8_blk == 0, gsel_f[:, j * SUB:(j + 1) * SUB],
                                                jnp.where(r8_blk == 1, rank, 0.0))
            cnt_ref[0, r8rows, :] = jnp.broadcast_to(jnp.sum(oh, axis=1, keepdims=True), (8, LANES))
        comb_t = jnp.where(e_iota == ROUTE_GROUP_LANE, gsel_f,
                           jnp.where(e_iota == ROUTE_RANK_LANE, jnp.concatenate(ranks, axis=1), comb_t))
        comb_ref[0, rows, :] = comb_t.T
        return carry

    if n_mt == 1:
        phase3(0, 0)
    else:
        lax.fori_loop(0, n_mt, phase3, 0)


def _const_spec(a):
    nd = a.ndim
    return pl.BlockSpec(a.shape, lambda b, _nd=nd: (0,) * _nd, pipeline_mode=pl.Buffered(1))


def _mixer(x, T, mod, mod_index, weights, P, state=None, emit_state=False):
    B, R, _ = x.shape
    n_chunks = R // SUB
    n_seq = R // T
    has_state = state is not None
    seq_mode = {} if R <= MIX_TM else {"pipeline_mode": pl.Buffered(1)}
    in_specs = [
        pl.BlockSpec((1, R, D_MODEL), lambda b: (b, 0, 0), **seq_mode),
        pl.BlockSpec((1, N_ADA, D_MODEL), lambda b: (mod_index(b), 0, 0)),
    ]
    args = [x, mod]
    if has_state:
        c0, n0, m0 = state
        in_specs += [
            pl.BlockSpec((1, N_UNITS, HEAD_DIM, HEAD_DIM), lambda b: (b, 0, 0, 0)),
            pl.BlockSpec((1, N_UNITS, HEAD_DIM), lambda b: (b, 0, 0)),
            pl.BlockSpec((1, N_UNITS, LANES), lambda b: (b, 0, 0)),
        ]
        args += [c0, n0, m0]
    for name in _MIXER_WEIGHTS:
        in_specs.append(_const_spec(weights[name]))
        args.append(weights[name])
    out_shape = [
        jax.ShapeDtypeStruct((B, R, D_MODEL), F32),
        jax.ShapeDtypeStruct((B, R, D_MODEL), BF16),
        jax.ShapeDtypeStruct((B, R, LANES), F32),
        jax.ShapeDtypeStruct((B, n_chunks * 8, SUB), F32),
        jax.ShapeDtypeStruct((B, n_chunks * 8, LANES), F32),
    ]
    out_specs = [
        pl.BlockSpec((1, R, D_MODEL), lambda b: (b, 0, 0), **seq_mode),
        pl.BlockSpec((1, R, D_MODEL), lambda b: (b, 0, 0), **seq_mode),
        pl.BlockSpec((1, R, LANES), lambda b: (b, 0, 0)),
        pl.BlockSpec((1, n_chunks * 8, SUB), lambda b: (b, 0, 0)),
        pl.BlockSpec((1, n_chunks * 8, LANES), lambda b: (b, 0, 0)),
    ]
    if emit_state:
        out_shape += [
            jax.ShapeDtypeStruct((B, n_seq * N_UNITS, HEAD_DIM, HEAD_DIM), F32),
            jax.ShapeDtypeStruct((B, n_seq * N_UNITS, HEAD_DIM), F32),
            jax.ShapeDtypeStruct((B, n_seq * N_UNITS, LANES), F32),
        ]
        out_specs += [
            pl.BlockSpec((1, n_seq * N_UNITS, HEAD_DIM, HEAD_DIM), lambda b: (b, 0, 0, 0)),
            pl.BlockSpec((1, n_seq * N_UNITS, HEAD_DIM), lambda b: (b, 0, 0)),
            pl.BlockSpec((1, n_seq * N_UNITS, LANES), lambda b: (b, 0, 0)),
        ]
    scratch = [
        pltpu.VMEM((R, D_MLSTM), BF16),
        pltpu.VMEM((n_chunks, D_MLSTM, SUB), BF16),
        pltpu.VMEM((R, D_MLSTM), BF16),
        pltpu.VMEM((R, D_MLSTM), F32),
        pltpu.VMEM((n_chunks, 5 * N_UNITS, SUB), F32),
        pltpu.VMEM((R, D_MODEL), F32),
        pltpu.VMEM((R, D_MODEL), F32),
        pltpu.VMEM((R, D_MLSTM), F32),
        pltpu.VMEM((N_UNITS, HEAD_DIM, 2 * HEAD_DIM), F32),
        pltpu.VMEM((MIX_TM // P, P + 2 * CONV_PAD, D_CONV), F32),
    ]
    return pl.pallas_call(
        functools.partial(_mixer_kernel, R, T, P, has_state, emit_state),
        grid=(B,),
        in_specs=in_specs,
        out_specs=out_specs,
        out_shape=out_shape,
        scratch_shapes=scratch,
        compiler_params=pltpu.CompilerParams(
            dimension_semantics=("arbitrary",), vmem_limit_bytes=VMEM_LIMIT),
        name="mixer_T%d" % T,
    )(*args)


def _dest_in_block(group, rank, starts):
    dest = rank
    for g in range(N_GROUPS):
        dest = dest + jnp.where(group == float(g), starts[g], 0.0)
    return dest


def _copy_segments(src_refs, dst_refs, src_starts, dst_starts, n_pieces):
    for g in range(N_GROUPS):
        def body(k, carry, g=g):
            s = pl.multiple_of(src_starts[g] + k * ROW_ALIGN, ROW_ALIGN)
            d = pl.multiple_of(dst_starts[g] + k * ROW_ALIGN, ROW_ALIGN)
            for src, dst in zip(src_refs, dst_refs):
                dst[pl.ds(d, ROW_ALIGN), :] = src[pl.ds(s, ROW_ALIGN), :]
            return carry
        lax.fori_loop(0, n_pieces[g], body, 0)


def _plan_segments(n_blocks, n_tiles, cnt_ref, start_ref, npiece_ref, off_ref, tgroup_ref, tvalid_ref, tfirst_ref):
    align_shift = ROW_ALIGN.bit_length() - 1
    tile_shift = MOE_TM.bit_length() - 1

    def block_starts(blk, carry):
        row = jnp.int32(0)
        for g in range(N_GROUPS):
            n = lax.shift_right_logical(cnt_ref[blk * N_GROUPS + g] + (ROW_ALIGN - 1), align_shift)
            npiece_ref[blk * N_GROUPS + g] = n
            start_ref[blk * N_GROUPS + g] = row
            row = row + n * ROW_ALIGN
        return carry

    lax.fori_loop(0, n_blocks, block_starts, 0)

    base_row = jnp.int32(0)
    base_tile = jnp.int32(0)
    last_group = jnp.int32(0)
    for g in range(N_GROUPS):
        def seg_offsets(blk, row, g=g, base_row=base_row):
            off_ref[blk * N_GROUPS + g] = base_row + row
            return row + npiece_ref[blk * N_GROUPS + g] * ROW_ALIGN

        rows = lax.fori_loop(0, n_blocks, seg_offsets, jnp.int32(0))
        tiles = lax.shift_right_logical(rows + (MOE_TM - 1), tile_shift)

        def mark_tiles(t, carry, g=g, base_tile=base_tile):
            tgroup_ref[base_tile + t] = g
            tvalid_ref[base_tile + t] = 1
            tfirst_ref[base_tile + t] = (t == 0).astype(jnp.int32)
            return carry

        lax.fori_loop(0, tiles, mark_tiles, 0)
        last_group = jnp.where(tiles > 0, g, last_group)
        base_row = base_row + tiles * MOE_TM
        base_tile = base_tile + tiles

    def mark_unused(t, carry):
        tgroup_ref[t] = last_group
        tvalid_ref[t] = 0
        tfirst_ref[t] = 0
        return carry

    lax.fori_loop(base_tile, n_tiles, mark_unused, 0)


def _dispatch_kernel(n_ctx_blocks, n_blocks, n_tiles, cnt_ref,
                     h2c_ref, h2l_ref, cbc_ref, cbl_ref, rtc_ref, rtl_ref,
                     xs_ref, cs_ref, start_ref, npiece_ref, off_ref, tgroup_ref, tvalid_ref, tfirst_ref,
                     sx_s, sc_s):
    b = pl.program_id(0)
    is_ctx = b < n_ctx_blocks

    @pl.when(b == 0)
    def _():
        _plan_segments(n_blocks, n_tiles, cnt_ref, start_ref, npiece_ref, off_ref,
                       tgroup_ref, tvalid_ref, tfirst_ref)
        xs_ref[...] = jnp.zeros_like(xs_ref)
        cs_ref[...] = jnp.zeros_like(cs_ref)

    h2 = jnp.where(is_ctx, h2c_ref[0], h2l_ref[0])
    cb = jnp.where(is_ctx, cbc_ref[0], cbl_ref[0])
    rt = jnp.where(is_ctx, rtc_ref[0], rtl_ref[0])
    starts = [start_ref[b * N_GROUPS + g] for g in range(N_GROUPS)]
    dest = _dest_in_block(rt[0:1, :], rt[1:2, :], [s.astype(F32) for s in starts])
    row = lax.broadcasted_iota(jnp.int32, (SORT_ROWS, SUB), 0).astype(F32)
    perm = (row == dest).astype(F32).astype(BF16)
    cb_hi = cb.astype(BF16)
    cb_lo = (cb - cb_hi.astype(F32)).astype(BF16)
    sx_s[...] = _dot(perm, h2).astype(BF16)
    sc_s[...] = _dot(perm, jnp.concatenate([cb_hi, cb_lo], axis=1)).astype(BF16)
    _copy_segments((sx_s, sc_s), (xs_ref, cs_ref), starts,
                   [off_ref[b * N_GROUPS + g] for g in range(N_GROUPS)],
                   [npiece_ref[b * N_GROUPS + g] for g in range(N_GROUPS)])


def _experts_kernel(tgroup_ref, tvalid_ref, tfirst_ref, xs_ref, cs_ref, wg_ref, wu_ref, wd_ref, ys_ref,
                    wg_s, wu_s, wd_s):
    i = pl.program_id(0)

    @pl.when(tfirst_ref[i] == 1)
    def _():
        for j in range(EXPERTS_PER_GROUP):
            cols = slice(j * D_EXPERT, (j + 1) * D_EXPERT)
            wg_s[:, cols] = wg_ref[j].astype(BF16)
            wu_s[:, cols] = wu_ref[j].astype(BF16)
            wd_s[cols, :] = wd_ref[j].astype(BF16)

    @pl.when(tvalid_ref[i] == 1)
    def _():
        x = xs_ref[...]
        g = _dot(x, wg_s[...])
        u = _dot(x, wu_s[...])
        comb = cs_ref[:, :LANES].astype(F32) + cs_ref[:, LANES:].astype(F32)
        lane = lax.broadcasted_iota(jnp.int32, comb.shape, 1)
        first = tgroup_ref[i] * EXPERTS_PER_GROUP
        parts = []
        for j in range(EXPERTS_PER_GROUP):
            cols = slice(j * D_EXPERT, (j + 1) * D_EXPERT)
            cw = jnp.sum(jnp.where(lane == first + j, comb, 0.0), axis=1, keepdims=True)
            gj = g[:, cols]
            parts.append((gj * _sigmoid(gj) * u[:, cols] * cw).astype(BF16))
        ys_ref[...] = _dot(jnp.concatenate(parts, axis=1), wd_s[...]).astype(BF16)

    @pl.when(tvalid_ref[i] == 0)
    def _():
        ys_ref[...] = jnp.zeros_like(ys_ref)


def _combine_kernel(n_ctx_blocks, blocks_per_lat_seq, start_ref, npiece_ref, off_ref,
                    x1c_ref, x1l_ref, cbc_ref, cbl_ref, ys_ref, mod_ref, gf_ref, yc_ref, yl_ref, loc_s):
    b = pl.program_id(0)
    is_ctx = b < n_ctx_blocks
    starts = [start_ref[b * N_GROUPS + g] for g in range(N_GROUPS)]
    loc_s[...] = jnp.zeros_like(loc_s)
    _copy_segments((ys_ref,), (loc_s,), [off_ref[b * N_GROUPS + g] for g in range(N_GROUPS)], starts,
                   [npiece_ref[b * N_GROUPS + g] for g in range(N_GROUPS)])
    cb = jnp.where(is_ctx, cbc_ref[0], cbl_ref[0])
    dest = _dest_in_block(cb[:, ROUTE_GROUP_LANE:ROUTE_GROUP_LANE + 1],
                          cb[:, ROUTE_RANK_LANE:ROUTE_RANK_LANE + 1],
                          [s.astype(F32) for s in starts])
    col = lax.broadcasted_iota(jnp.int32, (SUB, SORT_ROWS), 1).astype(F32)
    unperm = (col == dest).astype(F32).astype(BF16)
    moe = _dot(unperm, loc_s[...])
    x1 = jnp.where(is_ctx, x1c_ref[0], x1l_ref[0])
    mrow = jnp.where(is_ctx, 0, 1 + jnp.maximum(b - n_ctx_blocks, 0) // blocks_per_lat_seq)
    x2 = x1 + mod_ref[mrow, N_ADA - 1:N_ADA, :] * moe
    y = x2 * lax.rsqrt(jnp.mean(x2 * x2, axis=-1, keepdims=True) + EPS) * gf_ref[...]

    @pl.when(is_ctx)
    def _():
        yc_ref[0] = y

    @pl.when(jnp.logical_not(is_ctx))
    def _():
        yl_ref[0] = y


def _moe(x1c, x1l, h2c, h2l, cbc, cbl, rtc, rtl, cnt, mod, blocks_per_lat_seq, wg, wu, wd, gf):
    nc, nl = x1c.shape[0], x1l.shape[0]
    nb = nc + nl
    n_rows_max = nb * SUB + nb * N_GROUPS * (ROW_ALIGN - 1) + N_GROUPS * (MOE_TM - ROW_ALIGN)
    n_tiles = -(-n_rows_max // MOE_TM)
    ns = n_tiles * MOE_TM

    cmap = lambda b, *_: (jnp.minimum(b, nc - 1), 0, 0)
    lmap = lambda b, *_: (jnp.maximum(b - nc, 0), 0, 0)
    whole = lambda *_: (0, 0)
    once = {"pipeline_mode": pl.Buffered(1)}
    arb = pltpu.CompilerParams(dimension_semantics=("arbitrary",), vmem_limit_bytes=VMEM_LIMIT)
    smem = pl.BlockSpec(memory_space=pltpu.SMEM)
    seg_i32 = jax.ShapeDtypeStruct((nb * N_GROUPS,), jnp.int32)
    tile_i32 = jax.ShapeDtypeStruct((n_tiles,), jnp.int32)

    xs, cs, start, npiece, off, tgroup, tvalid, tfirst = pl.pallas_call(
        functools.partial(_dispatch_kernel, nc, nb, n_tiles),
        grid_spec=pltpu.PrefetchScalarGridSpec(
            num_scalar_prefetch=1, grid=(nb,),
            in_specs=[
                pl.BlockSpec((1, SUB, D_MODEL), cmap), pl.BlockSpec((1, SUB, D_MODEL), lmap),
                pl.BlockSpec((1, SUB, LANES), cmap), pl.BlockSpec((1, SUB, LANES), lmap),
                pl.BlockSpec((1, 8, SUB), cmap), pl.BlockSpec((1, 8, SUB), lmap),
            ],
            out_specs=[pl.BlockSpec((ns, D_MODEL), whole, **once), pl.BlockSpec((ns, 2 * LANES), whole, **once),
                       smem, smem, smem, smem, smem, smem],
            scratch_shapes=[pltpu.VMEM((SORT_ROWS, D_MODEL), BF16), pltpu.VMEM((SORT_ROWS, 2 * LANES), BF16)],
        ),
        out_shape=[jax.ShapeDtypeStruct((ns, D_MODEL), BF16), jax.ShapeDtypeStruct((ns, 2 * LANES), BF16),
                   seg_i32, seg_i32, seg_i32, tile_i32, tile_i32, tile_i32],
        compiler_params=arb,
        name="moe_dispatch",
    )(cnt, h2c, h2l, cbc, cbl, rtc, rtl)

    wmap = lambda i, tg, tv, tf: (tg[i], 0, 0)
    ys = pl.pallas_call(
        _experts_kernel,
        grid_spec=pltpu.PrefetchScalarGridSpec(
            num_scalar_prefetch=3, grid=(n_tiles,),
            in_specs=[
                pl.BlockSpec((MOE_TM, D_MODEL), lambda i, *_: (i, 0)),
                pl.BlockSpec((MOE_TM, 2 * LANES), lambda i, *_: (i, 0)),
                pl.BlockSpec((EXPERTS_PER_GROUP, D_MODEL, D_EXPERT), wmap),
                pl.BlockSpec((EXPERTS_PER_GROUP, D_MODEL, D_EXPERT), wmap),
                pl.BlockSpec((EXPERTS_PER_GROUP, D_EXPERT, D_MODEL), wmap),
            ],
            out_specs=pl.BlockSpec((MOE_TM, D_MODEL), lambda i, *_: (i, 0)),
            scratch_shapes=[pltpu.VMEM((D_MODEL, EXPERTS_PER_GROUP * D_EXPERT), BF16),
                            pltpu.VMEM((D_MODEL, EXPERTS_PER_GROUP * D_EXPERT), BF16),
                            pltpu.VMEM((EXPERTS_PER_GROUP * D_EXPERT, D_MODEL), BF16)],
        ),
        out_shape=jax.ShapeDtypeStruct((ns, D_MODEL), BF16),
        compiler_params=arb,
        name="moe_experts",
    )(tgroup, tvalid, tfirst, xs, cs, wg, wu, wd)

    yc, yl = pl.pallas_call(
        functools.partial(_combine_kernel, nc, blocks_per_lat_seq),
        grid_spec=pltpu.PrefetchScalarGridSpec(
            num_scalar_prefetch=3, grid=(nb,),
            in_specs=[
                pl.BlockSpec((1, SUB, D_MODEL), cmap), pl.BlockSpec((1, SUB, D_MODEL), lmap),
                pl.BlockSpec((1, SUB, LANES), cmap), pl.BlockSpec((1, SUB, LANES), lmap),
                pl.BlockSpec((ns, D_MODEL), whole, **once),
                pl.BlockSpec(mod.shape, lambda *_: (0, 0, 0)),
                pl.BlockSpec((1, D_MODEL), whole),
            ],
            out_specs=[pl.BlockSpec((1, SUB, D_MODEL), cmap), pl.BlockSpec((1, SUB, D_MODEL), lmap)],
            scratch_shapes=[pltpu.VMEM((SORT_ROWS, D_MODEL), BF16)],
        ),
        out_shape=[jax.ShapeDtypeStruct((nc, SUB, D_MODEL), F32), jax.ShapeDtypeStruct((nl, SUB, D_MODEL), F32)],
        compiler_params=arb,
        name="moe_combine",
    )(start, npiece, off, x1c, x1l, cbc, cbl, ys, mod, gf)
    return yc, yl


def _prep_weights(norm1_g, w_in, b_in, b_gates, w_dw, b_dw, conv_ln_g, conv_ln_b, w_conv_out,
                  mlstm_hn_g, w_mlstm_out, w_o, norm2_g, w_rg, b_rg, w_re, b_re):
    s_a = 2 * D_CONV
    s_q = s_a + D_MLSTM
    s_k = s_q + D_MLSTM
    s_v = s_k + D_MLSTM
    s_o = s_v + D_MLSTM
    s_g = s_o + 4 * N_HEADS
    row = lambda v: v.reshape(1, -1).astype(F32)
    w_t = w_in.T
    keep = [(0, s_q), (s_k, s_o), (s_g, w_in.shape[1])]
    wrow = _transpose_cast(w_t, [r for a, b in keep for r in range(a, b, WPREP_ROWS)])
    w_g = w_t[s_o:s_g].reshape(2, 2, N_HEADS, D_MODEL).transpose(1, 0, 2, 3).reshape(4 * N_HEADS, D_MODEL)
    bg = (b_in[s_o:s_g] + b_gates.reshape(-1)).reshape(2, 2, N_HEADS).transpose(1, 0, 2).reshape(-1, 1)
    n_rt = N_EXPERTS + N_GROUPS
    wrt = jnp.pad(jnp.concatenate([w_re, w_rg], axis=1), ((0, 0), (0, LANES - n_rt)))
    wrt_hi = wrt.astype(BF16)
    wrt2 = jnp.concatenate([wrt_hi, (wrt - wrt_hi.astype(F32)).astype(BF16)], axis=1)
    brtT = jnp.pad(jnp.concatenate([b_re, b_rg]), (0, LANES - n_rt)).reshape(LANES, 1)
    return {
        "g1": row(norm1_g),
        "wrow": wrow, "bag": row(b_in[:s_a]), "bq": row(b_in[s_a:s_q]),
        "wkT": w_t[s_q:s_k].astype(BF16), "bk": b_in[s_q:s_k].reshape(-1, 1),
        "bv": row(b_in[s_k:s_v]), "bog": row(b_in[s_v:s_o]),
        "wgifT": w_g.astype(BF16), "bgifT": bg, "bgm": row(b_in[s_g:]),
        "wdw": w_dw.astype(F32), "bdw": row(b_dw), "lng": row(conv_ln_g), "lnb": row(conv_ln_b),
        "wco": w_conv_out.astype(BF16), "hng": row(mlstm_hn_g), "wmo": w_mlstm_out.astype(BF16),
        "wo": w_o.astype(BF16), "g2": row(norm2_g), "wrt2": wrt2, "brtT": brtT,
    }


def kernel(x_prompt, x_sample, state_C, state_n, state_m, c, c_ctx, norm1_g, w_ada, b_ada, w_in, b_in, b_gates, w_dw, b_dw, conv_ln_g, conv_ln_b, w_conv_out, mlstm_hn_g, w_mlstm_out, w_o, norm2_g, w_rg, b_rg, w_re, b_re, w_e_gate, w_e_up, w_e_down, norm_final_g):
    B, S, _ = x_prompt.shape
    Bd, Sd, _ = x_sample.shape
    assert w_ada.shape[0] == 1, "single trunk layer"
    assert S == SUB and Sd % SUB == 0

    cin = jnp.concatenate([c_ctx[None, :], c, jnp.zeros((8 - 1 - Bd, D_MODEL), F32)], axis=0)
    mod = _ada(cin, w_ada[0], b_ada[0].reshape(1, -1)).reshape(8, N_ADA, D_MODEL)

    wts = _prep_weights(norm1_g[0], w_in[0], b_in[0], b_gates[0], w_dw[0], b_dw[0], conv_ln_g[0],
                        conv_ln_b[0], w_conv_out[0], mlstm_hn_g[0], w_mlstm_out[0], w_o[0],
                        norm2_g[0], w_rg[0], b_rg[0], w_re[0], b_re[0])

    x1p, h2p, cbp, rtp, cntp, c_new, n_new, m_new = _mixer(
        x_prompt.reshape(B * S // MIX_TM, MIX_TM, D_MODEL), S, mod, lambda b: 0, wts, P=S, emit_state=True)

    m0 = jnp.broadcast_to(state_m[:, 0].reshape(Bd, N_UNITS, 1), (Bd, N_UNITS, LANES))
    state = (state_C[:, 0].reshape(Bd, N_UNITS, HEAD_DIM, HEAD_DIM), state_n[:, 0].reshape(Bd, N_UNITS, HEAD_DIM), m0)
    x1s, h2s, cbs, rts, cnts = _mixer(x_sample, Sd, mod, lambda b: 1 + b, wts, P=GRID_W, state=state)

    nc, nl = B * S // SUB, Bd * Sd // SUB
    blk = lambda a, n: a.reshape(n, SUB, a.shape[-1])
    cnt = jnp.concatenate([cntp.reshape(nc, 8, LANES)[:, :N_GROUPS, 0],
                           cnts.reshape(nl, 8, LANES)[:, :N_GROUPS, 0]], axis=0)
    yp, ys = _moe(blk(x1p, nc), blk(x1s, nl), blk(h2p, nc), blk(h2s, nl), blk(cbp, nc), blk(cbs, nl),
                  rtp.reshape(nc, 8, SUB), rts.reshape(nl, 8, SUB),
                  cnt.astype(jnp.int32).reshape(-1), mod, Sd // SUB, w_e_gate[0], w_e_up[0], w_e_down[0],
                  norm_final_g.reshape(1, -1))

    return (yp.reshape(B, S, D_MODEL), ys.reshape(Bd, Sd, D_MODEL),
            c_new.reshape(B, 1, 2, N_HEADS, HEAD_DIM, HEAD_DIM),
            n_new.reshape(B, 1, 2, N_HEADS, HEAD_DIM),
            m_new[:, :, 0].reshape(B, 1, 2, N_HEADS))
```

```python
import functools

import jax
import jax.numpy as jnp
from jax import lax
from jax.experimental import pallas as pl
from jax.experimental.pallas import tpu as pltpu

D_MODEL = 1024
D_CONV = 512
CONV_K = 31
D_MLSTM = 512
N_HEADS = 4
HEAD_DIM = D_MLSTM // N_HEADS
N_GROUPS = 4
EXPERTS_PER_GROUP = 4
N_EXPERTS = N_GROUPS * EXPERTS_PER_GROUP
D_EXPERT = 256
N_ADA = 6
EPS = 1e-6
GRID_W = 64

LANES = 128
SUB = 256
CONV_PAD = 16
CONV_RB = 64
N_UNITS = 2 * N_HEADS
ROW_ALIGN = 16
SORT_ROWS = SUB + N_GROUPS * ROW_ALIGN
MOE_TM = 512
MIX_TM = 512
WPREP_ROWS = 512
ROUTE_GROUP_LANE = N_EXPERTS
ROUTE_RANK_LANE = N_EXPERTS + 1
VMEM_LIMIT = 58 * 1024 * 1024

BF16 = jnp.bfloat16
F32 = jnp.float32
NT_DIMS = (((1,), (1,)), ((), ()))


def _dot(a, b):
    return jnp.dot(a, b, preferred_element_type=F32)


def _dot_nt(a, b, precision=None):
    return lax.dot_general(a, b, NT_DIMS, preferred_element_type=F32, precision=precision)


def _sigmoid(x):
    return 0.5 * jnp.tanh(0.5 * x) + 0.5


def _log_sigmoid(x):
    return jnp.minimum(x, 0.0) - jnp.log1p(jnp.exp(-jnp.abs(x)))


def _split3(x):
    hi = x.astype(BF16).astype(F32)
    r1 = x - hi
    mid = r1.astype(BF16).astype(F32)
    lo = (r1 - mid).astype(BF16).astype(F32)
    return hi, mid, lo


def _ada_kernel(c_ref, w_ref, b_ref, o_ref):
    c = c_ref[...]
    s = (c * _sigmoid(c)).astype(BF16)
    o_ref[0] = _dot(s, w_ref[...].astype(BF16)) + b_ref[...]


def _ada(cin, w_ada, b_ada):
    return pl.pallas_call(
        _ada_kernel,
        grid=(N_ADA,),
        in_specs=[
            pl.BlockSpec((8, D_MODEL), lambda j: (0, 0)),
            pl.BlockSpec((D_MODEL, D_MODEL), lambda j: (0, j)),
            pl.BlockSpec((1, D_MODEL), lambda j: (0, j)),
        ],
        out_specs=pl.BlockSpec((1, 8, D_MODEL), lambda j: (j, 0, 0)),
        out_shape=jax.ShapeDtypeStruct((N_ADA, 8, D_MODEL), F32),
        compiler_params=pltpu.CompilerParams(dimension_semantics=("arbitrary",)),
        name="ada",
    )(cin, w_ada, b_ada)


def _transpose_cast_kernel(starts_ref, wt_ref, o_ref):
    o_ref[...] = wt_ref[...].T.astype(BF16)


def _transpose_cast(w_t, row_starts):
    n, k = len(row_starts), w_t.shape[1]
    return pl.pallas_call(
        _transpose_cast_kernel,
        grid_spec=pltpu.PrefetchScalarGridSpec(
            num_scalar_prefetch=1, grid=(n,),
            in_specs=[pl.BlockSpec((pl.Element(WPREP_ROWS), pl.Element(k)), lambda j, starts: (starts[j] * 8, 0))],
            out_specs=pl.BlockSpec((k, WPREP_ROWS), lambda j, starts: (0, j)),
        ),
        out_shape=jax.ShapeDtypeStruct((k, n * WPREP_ROWS), BF16),
        compiler_params=pltpu.CompilerParams(dimension_semantics=("arbitrary",)),
        name="transpose_cast",
    )(jnp.array([r // 8 for r in row_starts], jnp.int32), w_t)


WROW_OFFSET = {"wq": 2 * D_CONV, "wv": 2 * D_CONV + D_MLSTM, "wog": 2 * D_CONV + 2 * D_MLSTM,
               "wgm": 2 * D_CONV + 3 * D_MLSTM}

_MIXER_WEIGHTS = (
    "g1", "wrow", "bag", "bq", "wkT", "bk", "bv", "bog",
    "wgifT", "bgifT", "bgm", "wdw", "bdw", "lng", "lnb",
    "wco", "hng", "wmo", "wo", "g2", "wrt2", "brtT",
)


def _conv_block(upad_s, seg, base, cs, wdw_ref, bdw_ref):
    sub = 8
    first = CONV_PAD - CONV_K // 2
    acc = jnp.broadcast_to(bdw_ref[0:1, cs], (CONV_RB, LANES))
    for r in range(sub):
        z = None
        for a in range((CONV_K + first + sub - 1) // sub):
            j = sub * a + r - first
            if 0 <= j < CONV_K:
                lo = base + sub * a
                term = wdw_ref[j:j + 1, cs] * upad_s[seg, lo:lo + CONV_RB + sub, cs]
                z = term if z is None else z + term
        acc = acc + z[r:r + CONV_RB, :]
    return acc


def _mixer_kernel(R, T, P, has_state, emit_state, mod_index, *refs):
    L = SUB
    n_mt = R // MIX_TM
    cpm = MIX_TM // L
    n_seq = R // T
    cps = T // L
    nseg = MIX_TM // P
    assert not has_state or n_seq == 1
    it = iter(refs)
    x_ref = next(it)
    mod_ref = next(it)
    if has_state:
        c0_ref = next(it)
        n0_ref = next(it)
        m0_ref = next(it)
    w = {name: next(it) for name in _MIXER_WEIGHTS}
    x1_ref = next(it)
    h2_ref = next(it)
    comb_ref = next(it)
    route_ref = next(it)
    cnt_ref = next(it)
    if emit_state:
        cout_ref = next(it)
        nout_ref = next(it)
        mout_ref = next(it)
    (q_s, kT_s, v_s, so_s, scan_s, ma_s, sgb_s, hm_s, cst_s, upad_s) = [next(it) for _ in range(10)]

    cond_row = mod_index(pl.program_id(0))

    def mod_row(i):
        return mod_ref[i, pl.ds(cond_row, 1), :]

    zpad = jnp.zeros((CONV_PAD, D_CONV), F32)
    for seg in range(nseg):
        upad_s[seg, 0:CONV_PAD, :] = zpad
        upad_s[seg, CONV_PAD + P:CONV_PAD + P + CONV_PAD, :] = zpad

    t_idx = lax.broadcasted_iota(jnp.int32, (L, L), 0)
    s_idx = lax.broadcasted_iota(jnp.int32, (L, L), 1)
    lower = s_idx <= t_idx
    upper = s_idx >= t_idx
    triu_b = upper.astype(F32).astype(BF16)
    lane_u = lax.broadcasted_iota(jnp.int32, (N_UNITS, L), 1)
    is_bwd = lax.broadcasted_iota(jnp.int32, (N_UNITS, L), 0) >= N_HEADS

    def gate_scan(g):
        gi, lf = g[:N_UNITS], _log_sigmoid(g[N_UNITS:])
        pr = _dot(jnp.concatenate(_split3(lf), axis=0).astype(BF16), triu_b)
        pre = pr[0:N_UNITS] + pr[N_UNITS:2 * N_UNITS] + pr[2 * N_UNITS:]
        tot = pre[:, L - 1:L]
        bsum = jnp.where(is_bwd, tot - pre + lf, pre)
        a = gi - bsum
        pm, sm, k = a, a, 1
        while k < L:
            pm = jnp.where(lane_u >= k, jnp.maximum(pm, pltpu.roll(pm, k, axis=1)), pm)
            sm = jnp.where(lane_u < L - k, jnp.maximum(sm, pltpu.roll(sm, L - k, axis=1)), sm)
            k *= 2
        wide = lambda v: jnp.broadcast_to(v, (N_UNITS, L))
        return jnp.concatenate([a, jnp.where(is_bwd, sm, pm), bsum, wide(tot),
                                wide(jnp.max(a, axis=1, keepdims=True))], axis=0)

    def phase1(i, carry):
        r0 = pl.multiple_of(i * MIX_TM, MIX_TM)
        rows = pl.ds(r0, MIX_TM)
        x = x_ref[0, rows, :]
        xn = x * lax.rsqrt(jnp.mean(x * x, axis=-1, keepdims=True) + EPS) * w["g1"][...]
        hb = (xn * (1.0 + mod_row(1)) + mod_row(0)).astype(BF16)

        gates = _dot_nt(w["wgifT"][...].astype(BF16), hb) + w["bgifT"][...]
        for j in range(cpm):
            scan_s[i * cpm + j] = gate_scan(gates[:, j * L:(j + 1) * L])
        ag = _dot(hb, w["wrow"][:, :2 * D_CONV]) + w["bag"][...]
        u = ag[:, :D_CONV] * _sigmoid(ag[:, D_CONV:])
        for seg in range(nseg):
            upad_s[seg, CONV_PAD:CONV_PAD + P, :] = u[seg * P:(seg + 1) * P, :]

        def proj(name, bias, c0, width=2 * LANES):
            w0 = WROW_OFFSET[name] + c0
            return _dot(hb, w["wrow"][:, w0:w0 + width]) + w[bias][:, c0:c0 + width]

        def gm_a(c0):
            ma_s[rows, c0:c0 + 2 * LANES] = _sigmoid(proj("wgm", "bgm", c0))

        def gm_b(c0):
            sgb_s[rows, c0:c0 + 2 * LANES] = _sigmoid(proj("wgm", "bgm", D_MODEL + c0))

        def q_part(c0):
            q_s[rows, c0:c0 + 2 * LANES] = (proj("wq", "bq", c0) * (HEAD_DIM ** -0.5)).astype(BF16)

        def v_part(c0):
            v_s[rows, c0:c0 + 2 * LANES] = proj("wv", "bv", c0).astype(BF16)

        def o_part(c0):
            so_s[rows, c0:c0 + 2 * LANES] = _sigmoid(proj("wog", "bog", c0))

        def k_part(c0):
            rs = slice(c0, c0 + 2 * LANES)
            kt = (_dot_nt(w["wkT"][rs, :].astype(BF16), hb) + w["bk"][rs, :]).astype(BF16)
            for j in range(cpm):
                kT_s[i * cpm + j, rs, :] = kt[:, j * L:(j + 1) * L]

        jobs = ([functools.partial(gm_a, c0) for c0 in range(0, D_MODEL, 2 * LANES)]
                + [functools.partial(gm_b, c0) for c0 in range(0, D_MODEL, 2 * LANES)]
                + [functools.partial(f, c0) for f in (q_part, k_part, v_part, o_part)
                   for c0 in range(0, D_MLSTM, 2 * LANES)])
        n_jobs = len(jobs)
        conv = {}
        n_pieces = (D_CONV // LANES) * nseg * (P // CONV_RB)
        for cb in range(D_CONV // LANES):
            cs = slice(cb * LANES, (cb + 1) * LANES)
            for seg in range(nseg):
                for rb in range(P // CONV_RB):
                    conv[(cb, seg, rb)] = _conv_block(upad_s, seg, rb * CONV_RB, cs, w["wdw"], w["bdw"])
                    if jobs and len(conv) * n_jobs >= (n_jobs - len(jobs) + 1) * n_pieces:
                        jobs.pop(0)()
        for job in jobs:
            job()
        cu = jnp.concatenate(
            [jnp.concatenate([conv[(cb, seg, rb)] for seg in range(nseg) for rb in range(P // CONV_RB)], axis=0)
             for cb in range(D_CONV // LANES)], axis=1)
        mu = jnp.mean(cu, axis=-1, keepdims=True)
        cc = cu - mu
        cn = cc * lax.rsqrt(jnp.mean(cc * cc, axis=-1, keepdims=True) + EPS) * w["lng"][...] + w["lnb"][...]
        ca = (cn * _sigmoid(cn)).astype(BF16)
        ma_s[rows, :] = ma_s[rows, :] * _dot(ca, w["wco"][...])
        return carry

    if n_mt == 1:
        phase1(0, 0)
    else:
        lax.fori_loop(0, n_mt, phase1, 0)

    ones_col = (lax.broadcasted_iota(jnp.int32, (L, HEAD_DIM), 1) == 0).astype(F32).astype(BF16)
    pad_rows = jnp.zeros((LANES - 3 * N_UNITS, L), F32)

    def gate_prep(c, m_vec):
        sc = scan_s[c]
        a, run_max, bsum = sc[0:N_UNITS], sc[N_UNITS:2 * N_UNITS], sc[2 * N_UNITS:3 * N_UNITS]
        tot, a_max = sc[3 * N_UNITS:4 * N_UNITS, 0:1], sc[4 * N_UNITS:5 * N_UNITS, 0:1]
        big_m = jnp.maximum(m_vec, run_max)
        m_end = jnp.maximum(m_vec, a_max)
        cols = jnp.concatenate(
            [big_m, jnp.exp(m_vec - big_m), jnp.exp(-bsum - big_m), pad_rows], axis=0).T
        return a, cols, jnp.exp(a - m_end), jnp.exp(m_vec - m_end), tot + m_end

    qk_cache = {}

    def unit(d, hd, c, prep, first_chunk, want_state):
        a, cols, wk, decay, _ = prep
        rows = slice(c * L, (c + 1) * L)
        hs = slice(hd * HEAD_DIM, (hd + 1) * HEAD_DIM)
        idx = d * N_HEADS + hd
        col = lambda k: cols[:, k * N_UNITS + idx:k * N_UNITS + idx + 1]
        qc = q_s[rows, hs]
        kTc = kT_s[c, hs, :]
        vaug = jnp.concatenate([v_s[rows, hs], ones_col], axis=1)
        if cps == 1 and (hd, c) in qk_cache:
            qk = qk_cache[(hd, c)]
        else:
            qk = _dot(qc, kTc)
            qk_cache[(hd, c)] = qk
        w_intra = jnp.where(lower if d == 0 else upper, jnp.exp(a[idx:idx + 1, :] - col(0)), 0.0)
        nd = _dot((qk * w_intra).astype(BF16), vaug)
        if has_state or not first_chunk:
            nd = nd + col(1) * _dot(qc, cst_s[idx].astype(BF16))
        den = nd[:, HEAD_DIM:HEAD_DIM + 1]
        h = nd[:, :HEAD_DIM] * (1.0 / jnp.maximum(jnp.abs(den), col(2)))
        if d == 0:
            hm_s[rows, hs] = h
        else:
            hm_s[rows, hs] = hm_s[rows, hs] + h
        if want_state:
            kw = (kTc.astype(F32) * wk[idx:idx + 1, :]).astype(BF16)
            upd = _dot(kw, vaug)
            if has_state or not first_chunk:
                upd = upd + decay[idx:idx + 1, :] * cst_s[idx]
            cst_s[idx] = upd

    dir_rows = lax.broadcasted_iota(jnp.int32, (N_UNITS, 1), 0) >= N_HEADS
    for seq in range(n_seq):
        if has_state:
            n_cols = jnp.concatenate([n0_ref[0], jnp.zeros((LANES - N_UNITS, HEAD_DIM), F32)], axis=0).T
            first_lane = lax.broadcasted_iota(jnp.int32, (HEAD_DIM, HEAD_DIM), 1) == 0
            for idx in range(N_UNITS):
                cst_s[idx, :, :HEAD_DIM] = c0_ref[0, idx]
                cst_s[idx, :, HEAD_DIM:] = jnp.where(first_lane, n_cols[:, idx:idx + 1], 0.0)
            m_vec = m0_ref[0, :, 0:1]
        else:
            m_vec = jnp.zeros((N_UNITS, 1), F32)
        prep = None
        for d in range(2):
            order = list(range(cps)) if d == 0 else list(range(cps - 1, -1, -1))
            for pos, c in enumerate(order):
                if cps > 1 or prep is None:
                    prep = gate_prep(seq * cps + c, m_vec)
                for hd in range(N_HEADS):
                    unit(d, hd, seq * cps + c, prep, pos == 0, emit_state or pos < cps - 1)
                m_vec = jnp.where(dir_rows == (d == 1), prep[4], m_vec)
        if emit_state:
            for idx in range(N_UNITS):
                caug = cst_s[idx]
                cout_ref[0, seq * N_UNITS + idx] = caug[:, :HEAD_DIM]
                nout_ref[0, seq * N_UNITS + idx:seq * N_UNITS + idx + 1, :] = caug[:, HEAD_DIM:].T[0:1, :]
            mout_ref[0, seq * N_UNITS:(seq + 1) * N_UNITS, :] = jnp.broadcast_to(m_vec, (N_UNITS, LANES))

    e_iota = lax.broadcasted_iota(jnp.int32, (LANES, MIX_TM), 0)
    g_of_e = lax.shift_right_logical(e_iota, 2)
    j_of_e = lax.bitwise_and(e_iota, EXPERTS_PER_GROUP - 1)
    r8 = lax.broadcasted_iota(jnp.int32, (8, MIX_TM), 0)
    r8_blk = lax.broadcasted_iota(jnp.int32, (8, SUB), 0)
    before_b = (t_idx < s_idx).astype(F32).astype(BF16)

    def phase3(i, carry):
        r0 = pl.multiple_of(i * MIX_TM, MIX_TM)
        rows = pl.ds(r0, MIX_TM)
        hm = hm_s[rows, :]
        heads = []
        for hd in range(N_HEADS):
            hh = hm[:, hd * HEAD_DIM:(hd + 1) * HEAD_DIM]
            heads.append(hh * lax.rsqrt(jnp.mean(hh * hh, axis=-1, keepdims=True) + EPS))
        hn = jnp.concatenate(heads, axis=1) * w["hng"][...]
        hb2 = (so_s[rows, :] * hn).astype(BF16)
        br_b = _dot(hb2, w["wmo"][...])
        mixed = (ma_s[rows, :] + sgb_s[rows, :] * br_b).astype(BF16)
        x1 = x_ref[0, rows, :] + mod_row(2) * _dot(mixed, w["wo"][...])
        x1_ref[0, rows, :] = x1
        xn = x1 * lax.rsqrt(jnp.mean(x1 * x1, axis=-1, keepdims=True) + EPS) * w["g2"][...]
        h2 = xn * (1.0 + mod_row(4)) + mod_row(3)
        h2_ref[0, rows, :] = h2.astype(BF16)

        h2_hi = h2.astype(BF16)
        h2_lo = (h2 - h2_hi.astype(F32)).astype(BF16)
        lg = _dot(h2_hi, w["wrt2"][...])
        lg = lg[:, :LANES] + lg[:, LANES:] + _dot(h2_lo, w["wrt2"][:, :LANES])
        lt = lg.T + w["brtT"][...]
        gl = [lt[N_EXPERTS + g:N_EXPERTS + g + 1, :] for g in range(N_GROUPS)]
        best, gsel = gl[0], jnp.zeros((1, MIX_TM), jnp.int32)
        for g in range(1, N_GROUPS):
            better = gl[g] > best
            gsel = jnp.where(better, g, gsel)
            best = jnp.where(better, gl[g], best)
        gp_sel = 1.0 / sum(jnp.exp(v - best) for v in gl)
        el = []
        for j in range(EXPERTS_PER_GROUP):
            v = lt[j:j + 1, :]
            for g in range(1, N_GROUPS):
                r = g * EXPERTS_PER_GROUP + j
                v = jnp.where(gsel == g, lt[r:r + 1, :], v)
            el.append(v)
        l1, e1 = el[0], jnp.zeros((1, MIX_TM), jnp.int32)
        for j in range(1, EXPERTS_PER_GROUP):
            better = el[j] > l1
            e1 = jnp.where(better, j, e1)
            l1 = jnp.where(better, el[j], l1)
        l2 = jnp.full((1, MIX_TM), -jnp.inf, F32)
        e2 = jnp.zeros((1, MIX_TM), jnp.int32)
        for j in range(EXPERTS_PER_GROUP):
            better = jnp.logical_and(e1 != j, el[j] > l2)
            e2 = jnp.where(better, j, e2)
            l2 = jnp.where(better, el[j], l2)
        r2 = jnp.exp(l2 - l1)
        wt1 = gp_sel / (1.0 + r2)
        wt2 = gp_sel * r2 / (1.0 + r2)
        in_group = g_of_e == gsel
        comb_t = (jnp.where(jnp.logical_and(in_group, j_of_e == e1), wt1, 0.0)
                  + jnp.where(jnp.logical_and(in_group, j_of_e == e2), wt2, 0.0))

        onehot = (r8 == gsel).astype(F32)
        gsel_f = gsel.astype(F32)
        ranks = []
        for j in range(cpm):
            oh = onehot[:, j * SUB:(j + 1) * SUB]
            rank = jnp.sum(oh * _dot(oh.astype(BF16), before_b), axis=0, keepdims=True)
            ranks.append(rank)
            r8rows = pl.ds(pl.multiple_of((i * cpm + j) * 8, 8), 8)
            route_ref[0, r8rows, :] = jnp.where(r8_blk == 0, gsel_f[:, j * SUB:(j + 1) * SUB],
                                                jnp.where(r8_blk == 1, rank, 0.0))
            cnt_ref[0, r8rows, :] = jnp.broadcast_to(jnp.sum(oh, axis=1, keepdims=True), (8, LANES))
        comb_t = jnp.where(e_iota == ROUTE_GROUP_LANE, gsel_f,
                           jnp.where(e_iota == ROUTE_RANK_LANE, jnp.concatenate(ranks, axis=1), comb_t))
        comb_ref[0, rows, :] = comb_t.T
        return carry

    if n_mt == 1:
        phase3(0, 0)
    else:
        lax.fori_loop(0, n_mt, phase3, 0)


def _const_spec(a):
    nd = a.ndim
    return pl.BlockSpec(a.shape, lambda b, _nd=nd: (0,) * _nd, pipeline_mode=pl.Buffered(1))


def _mixer(x, T, mod, mod_index, weights, P, state=None, emit_state=False):
    B, R, _ = x.shape
    n_chunks = R // SUB
    n_seq = R // T
    has_state = state is not None
    seq_mode = {} if R <= MIX_TM else {"pipeline_mode": pl.Buffered(1)}
    in_specs = [
        pl.BlockSpec((1, R, D_MODEL), lambda b: (b, 0, 0), **seq_mode),
        pl.BlockSpec(mod.shape, lambda b: (0, 0, 0)),
    ]
    args = [x, mod]
    if has_state:
        c0, n0, m0 = state
        in_specs += [
            pl.BlockSpec((1, N_UNITS, HEAD_DIM, HEAD_DIM), lambda b: (b, 0, 0, 0)),
            pl.BlockSpec((1, N_UNITS, HEAD_DIM), lambda b: (b, 0, 0)),
            pl.BlockSpec((1, N_UNITS, LANES), lambda b: (b, 0, 0)),
        ]
        args += [c0, n0, m0]
    for name in _MIXER_WEIGHTS:
        in_specs.append(_const_spec(weights[name]))
        args.append(weights[name])
    out_shape = [
        jax.ShapeDtypeStruct((B, R, D_MODEL), F32),
        jax.ShapeDtypeStruct((B, R, D_MODEL), BF16),
        jax.ShapeDtypeStruct((B, R, LANES), F32),
        jax.ShapeDtypeStruct((B, n_chunks * 8, SUB), F32),
        jax.ShapeDtypeStruct((B, n_chunks * 8, LANES), F32),
    ]
    out_specs = [
        pl.BlockSpec((1, R, D_MODEL), lambda b: (b, 0, 0), **seq_mode),
        pl.BlockSpec((1, R, D_MODEL), lambda b: (b, 0, 0), **seq_mode),
        pl.BlockSpec((1, R, LANES), lambda b: (b, 0, 0)),
        pl.BlockSpec((1, n_chunks * 8, SUB), lambda b: (b, 0, 0)),
        pl.BlockSpec((1, n_chunks * 8, LANES), lambda b: (b, 0, 0)),
    ]
    if emit_state:
        out_shape += [
            jax.ShapeDtypeStruct((B, n_seq * N_UNITS, HEAD_DIM, HEAD_DIM), F32),
            jax.ShapeDtypeStruct((B, n_seq * N_UNITS, HEAD_DIM), F32),
            jax.ShapeDtypeStruct((B, n_seq * N_UNITS, LANES), F32),
        ]
        out_specs += [
            pl.BlockSpec((1, n_seq * N_UNITS, HEAD_DIM, HEAD_DIM), lambda b: (b, 0, 0, 0)),
            pl.BlockSpec((1, n_seq * N_UNITS, HEAD_DIM), lambda b: (b, 0, 0)),
            pl.BlockSpec((1, n_seq * N_UNITS, LANES), lambda b: (b, 0, 0)),
        ]
    scratch = [
        pltpu.VMEM((R, D_MLSTM), BF16),
        pltpu.VMEM((n_chunks, D_MLSTM, SUB), BF16),
        pltpu.VMEM((R, D_MLSTM), BF16),
        pltpu.VMEM((R, D_MLSTM), F32),
        pltpu.VMEM((n_chunks, 5 * N_UNITS, SUB), F32),
        pltpu.VMEM((R, D_MODEL), F32),
        pltpu.VMEM((R, D_MODEL), F32),
        pltpu.VMEM((R, D_MLSTM), F32),
        pltpu.VMEM((N_UNITS, HEAD_DIM, 2 * HEAD_DIM), F32),
        pltpu.VMEM((MIX_TM // P, P + 2 * CONV_PAD, D_CONV), F32),
    ]
    return pl.pallas_call(
        functools.partial(_mixer_kernel, R, T, P, has_state, emit_state, mod_index),
        grid=(B,),
        in_specs=in_specs,
        out_specs=out_specs,
        out_shape=out_shape,
        scratch_shapes=scratch,
        compiler_params=pltpu.CompilerParams(
            dimension_semantics=("arbitrary",), vmem_limit_bytes=VMEM_LIMIT),
        name="mixer_T%d" % T,
    )(*args)


def _dest_in_block(group, rank, starts):
    dest = rank
    for g in range(N_GROUPS):
        dest = dest + jnp.where(group == float(g), starts[g], 0.0)
    return dest


def _copy_segments(src_refs, dst_refs, src_starts, dst_starts, n_pieces):
    for g in range(N_GROUPS):
        def body(k, carry, g=g):
            s = pl.multiple_of(src_starts[g] + k * ROW_ALIGN, ROW_ALIGN)
            d = pl.multiple_of(dst_starts[g] + k * ROW_ALIGN, ROW_ALIGN)
            for src, dst in zip(src_refs, dst_refs):
                dst[pl.ds(d, ROW_ALIGN), :] = src[pl.ds(s, ROW_ALIGN), :]
            return carry
        lax.fori_loop(0, n_pieces[g], body, 0)


def _plan_segments(n_blocks, n_tiles, cnt_ref, start_ref, npiece_ref, off_ref, tgroup_ref, tvalid_ref, tfirst_ref):
    align_shift = ROW_ALIGN.bit_length() - 1
    tile_shift = MOE_TM.bit_length() - 1

    def block_starts(blk, carry):
        row = jnp.int32(0)
        for g in range(N_GROUPS):
            n = lax.shift_right_logical(cnt_ref[blk * N_GROUPS + g] + (ROW_ALIGN - 1), align_shift)
            npiece_ref[blk * N_GROUPS + g] = n
            start_ref[blk * N_GROUPS + g] = row
            row = row + n * ROW_ALIGN
        return carry

    lax.fori_loop(0, n_blocks, block_starts, 0)

    base_row = jnp.int32(0)
    base_tile = jnp.int32(0)
    last_group = jnp.int32(0)
    for g in range(N_GROUPS):
        def seg_offsets(blk, row, g=g, base_row=base_row):
            off_ref[blk * N_GROUPS + g] = base_row + row
            return row + npiece_ref[blk * N_GROUPS + g] * ROW_ALIGN

        rows = lax.fori_loop(0, n_blocks, seg_offsets, jnp.int32(0))
        tiles = lax.shift_right_logical(rows + (MOE_TM - 1), tile_shift)

        def mark_tiles(t, carry, g=g, base_tile=base_tile):
            tgroup_ref[base_tile + t] = g
            tvalid_ref[base_tile + t] = 1
            tfirst_ref[base_tile + t] = (t == 0).astype(jnp.int32)
            return carry

        lax.fori_loop(0, tiles, mark_tiles, 0)
        last_group = jnp.where(tiles > 0, g, last_group)
        base_row = base_row + tiles * MOE_TM
        base_tile = base_tile + tiles

    def mark_unused(t, carry):
        tgroup_ref[t] = last_group
        tvalid_ref[t] = 0
        tfirst_ref[t] = 0
        return carry

    lax.fori_loop(base_tile, n_tiles, mark_unused, 0)


def _dispatch_kernel(n_ctx_blocks, n_blocks, n_tiles, cnt_ref,
                     h2c_ref, h2l_ref, cbc_ref, cbl_ref, rtc_ref, rtl_ref,
                     xs_ref, cs_ref, start_ref, npiece_ref, off_ref, tgroup_ref, tvalid_ref, tfirst_ref,
                     sx_s, sc_s):
    b = pl.program_id(0)
    is_ctx = b < n_ctx_blocks

    @pl.when(b == 0)
    def _():
        _plan_segments(n_blocks, n_tiles, cnt_ref, start_ref, npiece_ref, off_ref,
                       tgroup_ref, tvalid_ref, tfirst_ref)
        xs_ref[...] = jnp.zeros_like(xs_ref)
        cs_ref[...] = jnp.zeros_like(cs_ref)

    h2 = jnp.where(is_ctx, h2c_ref[0], h2l_ref[0])
    cb = jnp.where(is_ctx, cbc_ref[0], cbl_ref[0])
    rt = jnp.where(is_ctx, rtc_ref[0], rtl_ref[0])
    starts = [start_ref[b * N_GROUPS + g] for g in range(N_GROUPS)]
    dest = _dest_in_block(rt[0:1, :], rt[1:2, :], [s.astype(F32) for s in starts])
    row = lax.broadcasted_iota(jnp.int32, (SORT_ROWS, SUB), 0).astype(F32)
    perm = (row == dest).astype(F32).astype(BF16)
    cb_hi = cb.astype(BF16)
    cb_lo = (cb - cb_hi.astype(F32)).astype(BF16)
    sx_s[...] = _dot(perm, h2).astype(BF16)
    sc_s[...] = _dot(perm, jnp.concatenate([cb_hi, cb_lo], axis=1)).astype(BF16)
    _copy_segments((sx_s, sc_s), (xs_ref, cs_ref), starts,
                   [off_ref[b * N_GROUPS + g] for g in range(N_GROUPS)],
                   [npiece_ref[b * N_GROUPS + g] for g in range(N_GROUPS)])


def _experts_kernel(tgroup_ref, tvalid_ref, tfirst_ref, xs_ref, cs_ref, wg_ref, wu_ref, wd_ref, ys_ref,
                    wg_s, wu_s, wd_s):
    i = pl.program_id(0)

    @pl.when(tfirst_ref[i] == 1)
    def _():
        for j in range(EXPERTS_PER_GROUP):
            cols = slice(j * D_EXPERT, (j + 1) * D_EXPERT)
            wg_s[:, cols] = wg_ref[j].astype(BF16)
            wu_s[:, cols] = wu_ref[j].astype(BF16)
            wd_s[cols, :] = wd_ref[j].astype(BF16)

    @pl.when(tvalid_ref[i] == 1)
    def _():
        x = xs_ref[...]
        g = _dot(x, wg_s[...])
        u = _dot(x, wu_s[...])
        comb = cs_ref[:, :LANES].astype(F32) + cs_ref[:, LANES:].astype(F32)
        lane = lax.broadcasted_iota(jnp.int32, comb.shape, 1)
        first = tgroup_ref[i] * EXPERTS_PER_GROUP
        parts = []
        for j in range(EXPERTS_PER_GROUP):
            cols = slice(j * D_EXPERT, (j + 1) * D_EXPERT)
            cw = jnp.sum(jnp.where(lane == first + j, comb, 0.0), axis=1, keepdims=True)
            gj = g[:, cols]
            parts.append((gj * _sigmoid(gj) * u[:, cols] * cw).astype(BF16))
        ys_ref[...] = _dot(jnp.concatenate(parts, axis=1), wd_s[...]).astype(BF16)

    @pl.when(tvalid_ref[i] == 0)
    def _():
        ys_ref[...] = jnp.zeros_like(ys_ref)


def _combine_kernel(n_ctx_blocks, blocks_per_lat_seq, start_ref, npiece_ref, off_ref,
                    x1c_ref, x1l_ref, cbc_ref, cbl_ref, ys_ref, mod_ref, gf_ref, yc_ref, yl_ref, loc_s):
    b = pl.program_id(0)
    is_ctx = b < n_ctx_blocks
    starts = [start_ref[b * N_GROUPS + g] for g in range(N_GROUPS)]
    loc_s[...] = jnp.zeros_like(loc_s)
    _copy_segments((ys_ref,), (loc_s,), [off_ref[b * N_GROUPS + g] for g in range(N_GROUPS)], starts,
                   [npiece_ref[b * N_GROUPS + g] for g in range(N_GROUPS)])
    cb = jnp.where(is_ctx, cbc_ref[0], cbl_ref[0])
    dest = _dest_in_block(cb[:, ROUTE_GROUP_LANE:ROUTE_GROUP_LANE + 1],
                          cb[:, ROUTE_RANK_LANE:ROUTE_RANK_LANE + 1],
                          [s.astype(F32) for s in starts])
    col = lax.broadcasted_iota(jnp.int32, (SUB, SORT_ROWS), 1).astype(F32)
    unperm = (col == dest).astype(F32).astype(BF16)
    moe = _dot(unperm, loc_s[...])
    x1 = jnp.where(is_ctx, x1c_ref[0], x1l_ref[0])
    mrow = jnp.where(is_ctx, 0, 1 + jnp.maximum(b - n_ctx_blocks, 0) // blocks_per_lat_seq)
    x2 = x1 + mod_ref[N_ADA - 1, pl.ds(mrow, 1), :] * moe
    y = x2 * lax.rsqrt(jnp.mean(x2 * x2, axis=-1, keepdims=True) + EPS) * gf_ref[...]

    @pl.when(is_ctx)
    def _():
        yc_ref[0] = y

    @pl.when(jnp.logical_not(is_ctx))
    def _():
        yl_ref[0] = y


def _moe(x1c, x1l, h2c, h2l, cbc, cbl, rtc, rtl, cnt, mod, blocks_per_lat_seq, wg, wu, wd, gf):
    nc, nl = x1c.shape[0], x1l.shape[0]
    nb = nc + nl
    n_rows_max = nb * SUB + nb * N_GROUPS * (ROW_ALIGN - 1) + N_GROUPS * (MOE_TM - ROW_ALIGN)
    n_tiles = -(-n_rows_max // MOE_TM)
    ns = n_tiles * MOE_TM

    cmap = lambda b, *_: (jnp.minimum(b, nc - 1), 0, 0)
    lmap = lambda b, *_: (jnp.maximum(b - nc, 0), 0, 0)
    whole = lambda *_: (0, 0)
    once = {"pipeline_mode": pl.Buffered(1)}
    arb = pltpu.CompilerParams(dimension_semantics=("arbitrary",), vmem_limit_bytes=VMEM_LIMIT)
    smem = pl.BlockSpec(memory_space=pltpu.SMEM)
    seg_i32 = jax.ShapeDtypeStruct((nb * N_GROUPS,), jnp.int32)
    tile_i32 = jax.ShapeDtypeStruct((n_tiles,), jnp.int32)

    xs, cs, start, npiece, off, tgroup, tvalid, tfirst = pl.pallas_call(
        functools.partial(_dispatch_kernel, nc, nb, n_tiles),
        grid_spec=pltpu.PrefetchScalarGridSpec(
            num_scalar_prefetch=1, grid=(nb,),
            in_specs=[
                pl.BlockSpec((1, SUB, D_MODEL), cmap), pl.BlockSpec((1, SUB, D_MODEL), lmap),
                pl.BlockSpec((1, SUB, LANES), cmap), pl.BlockSpec((1, SUB, LANES), lmap),
                pl.BlockSpec((1, 8, SUB), cmap), pl.BlockSpec((1, 8, SUB), lmap),
            ],
            out_specs=[pl.BlockSpec((ns, D_MODEL), whole, **once), pl.BlockSpec((ns, 2 * LANES), whole, **once),
                       smem, smem, smem, smem, smem, smem],
            scratch_shapes=[pltpu.VMEM((SORT_ROWS, D_MODEL), BF16), pltpu.VMEM((SORT_ROWS, 2 * LANES), BF16)],
        ),
        out_shape=[jax.ShapeDtypeStruct((ns, D_MODEL), BF16), jax.ShapeDtypeStruct((ns, 2 * LANES), BF16),
                   seg_i32, seg_i32, seg_i32, tile_i32, tile_i32, tile_i32],
        compiler_params=arb,
        name="moe_dispatch",
    )(cnt, h2c, h2l, cbc, cbl, rtc, rtl)

    wmap = lambda i, tg, tv, tf: (tg[i], 0, 0)
    ys = pl.pallas_call(
        _experts_kernel,
        grid_spec=pltpu.PrefetchScalarGridSpec(
            num_scalar_prefetch=3, grid=(n_tiles,),
            in_specs=[
                pl.BlockSpec((MOE_TM, D_MODEL), lambda i, *_: (i, 0)),
                pl.BlockSpec((MOE_TM, 2 * LANES), lambda i, *_: (i, 0)),
                pl.BlockSpec((EXPERTS_PER_GROUP, D_MODEL, D_EXPERT), wmap),
                pl.BlockSpec((EXPERTS_PER_GROUP, D_MODEL, D_EXPERT), wmap),
                pl.BlockSpec((EXPERTS_PER_GROUP, D_EXPERT, D_MODEL), wmap),
            ],
            out_specs=pl.BlockSpec((MOE_TM, D_MODEL), lambda i, *_: (i, 0)),
            scratch_shapes=[pltpu.VMEM((D_MODEL, EXPERTS_PER_GROUP * D_EXPERT), BF16),
                            pltpu.VMEM((D_MODEL, EXPERTS_PER_GROUP * D_EXPERT), BF16),
                            pltpu.VMEM((EXPERTS_PER_GROUP * D_EXPERT, D_MODEL), BF16)],
        ),
        out_shape=jax.ShapeDtypeStruct((ns, D_MODEL), BF16),
        compiler_params=arb,
        name="moe_experts",
    )(tgroup, tvalid, tfirst, xs, cs, wg, wu, wd)

    yc, yl = pl.pallas_call(
        functools.partial(_combine_kernel, nc, blocks_per_lat_seq),
        grid_spec=pltpu.PrefetchScalarGridSpec(
            num_scalar_prefetch=3, grid=(nb,),
            in_specs=[
                pl.BlockSpec((1, SUB, D_MODEL), cmap), pl.BlockSpec((1, SUB, D_MODEL), lmap),
                pl.BlockSpec((1, SUB, LANES), cmap), pl.BlockSpec((1, SUB, LANES), lmap),
                pl.BlockSpec((ns, D_MODEL), whole, **once),
                pl.BlockSpec(mod.shape, lambda *_: (0, 0, 0)),
                pl.BlockSpec((1, D_MODEL), whole),
            ],
            out_specs=[pl.BlockSpec((1, SUB, D_MODEL), cmap), pl.BlockSpec((1, SUB, D_MODEL), lmap)],
            scratch_shapes=[pltpu.VMEM((SORT_ROWS, D_MODEL), BF16)],
        ),
        out_shape=[jax.ShapeDtypeStruct((nc, SUB, D_MODEL), F32), jax.ShapeDtypeStruct((nl, SUB, D_MODEL), F32)],
        compiler_params=arb,
        name="moe_combine",
    )(start, npiece, off, x1c, x1l, cbc, cbl, ys, mod, gf)
    return yc, yl


def _prep_weights(norm1_g, w_in, b_in, b_gates, w_dw, b_dw, conv_ln_g, conv_ln_b, w_conv_out,
                  mlstm_hn_g, w_mlstm_out, w_o, norm2_g, w_rg, b_rg, w_re, b_re):
    s_a = 2 * D_CONV
    s_q = s_a + D_MLSTM
    s_k = s_q + D_MLSTM
    s_v = s_k + D_MLSTM
    s_o = s_v + D_MLSTM
    s_g = s_o + 4 * N_HEADS
    row = lambda v: v.reshape(1, -1).astype(F32)
    w_t = w_in.T
    keep = [(0, s_q), (s_k, s_o), (s_g, w_in.shape[1])]
    wrow = _transpose_cast(w_t, [r for a, b in keep for r in range(a, b, WPREP_ROWS)])
    w_g = w_t[s_o:s_g].reshape(2, 2, N_HEADS, D_MODEL).transpose(1, 0, 2, 3).reshape(4 * N_HEADS, D_MODEL)
    bg = (b_in[s_o:s_g] + b_gates.reshape(-1)).reshape(2, 2, N_HEADS).transpose(1, 0, 2).reshape(-1, 1)
    n_rt = N_EXPERTS + N_GROUPS
    wrt = jnp.pad(jnp.concatenate([w_re, w_rg], axis=1), ((0, 0), (0, LANES - n_rt)))
    wrt_hi = wrt.astype(BF16)
    wrt2 = jnp.concatenate([wrt_hi, (wrt - wrt_hi.astype(F32)).astype(BF16)], axis=1)
    brtT = jnp.pad(jnp.concatenate([b_re, b_rg]), (0, LANES - n_rt)).reshape(LANES, 1)
    return {
        "g1": row(norm1_g),
        "wrow": wrow, "bag": row(b_in[:s_a]), "bq": row(b_in[s_a:s_q]),
        "wkT": w_t[s_q:s_k], "bk": b_in[s_q:s_k].reshape(-1, 1),
        "bv": row(b_in[s_k:s_v]), "bog": row(b_in[s_v:s_o]),
        "wgifT": w_g, "bgifT": bg, "bgm": row(b_in[s_g:]),
        "wdw": w_dw.astype(F32), "bdw": row(b_dw), "lng": row(conv_ln_g), "lnb": row(conv_ln_b),
        "wco": w_conv_out.astype(BF16), "hng": row(mlstm_hn_g), "wmo": w_mlstm_out.astype(BF16),
        "wo": w_o.astype(BF16), "g2": row(norm2_g), "wrt2": wrt2, "brtT": brtT,
    }


def kernel(x_prompt, x_sample, state_C, state_n, state_m, c, c_ctx, norm1_g, w_ada, b_ada, w_in, b_in, b_gates, w_dw, b_dw, conv_ln_g, conv_ln_b, w_conv_out, mlstm_hn_g, w_mlstm_out, w_o, norm2_g, w_rg, b_rg, w_re, b_re, w_e_gate, w_e_up, w_e_down, norm_final_g):
    B, S, _ = x_prompt.shape
    Bd, Sd, _ = x_sample.shape
    assert w_ada.shape[0] == 1, "single trunk layer"
    assert S == SUB and Sd % SUB == 0

    cin = jnp.concatenate([c_ctx[None, :], c, jnp.zeros((8 - 1 - Bd, D_MODEL), F32)], axis=0)
    mod = _ada(cin, w_ada[0], b_ada[0].reshape(1, -1))

    wts = _prep_weights(norm1_g[0], w_in[0], b_in[0], b_gates[0], w_dw[0], b_dw[0], conv_ln_g[0],
                        conv_ln_b[0], w_conv_out[0], mlstm_hn_g[0], w_mlstm_out[0], w_o[0],
                        norm2_g[0], w_rg[0], b_rg[0], w_re[0], b_re[0])

    x1p, h2p, cbp, rtp, cntp, c_new, n_new, m_new = _mixer(
        x_prompt.reshape(B * S // MIX_TM, MIX_TM, D_MODEL), S, mod, lambda b: 0, wts, P=S, emit_state=True)

    m0 = jnp.broadcast_to(state_m[:, 0].reshape(Bd, N_UNITS, 1), (Bd, N_UNITS, LANES))
    state = (state_C[:, 0].reshape(Bd, N_UNITS, HEAD_DIM, HEAD_DIM), state_n[:, 0].reshape(Bd, N_UNITS, HEAD_DIM), m0)
    x1s, h2s, cbs, rts, cnts = _mixer(x_sample, Sd, mod, lambda b: 1 + b, wts, P=GRID_W, state=state)

    nc, nl = B * S // SUB, Bd * Sd // SUB
    blk = lambda a, n: a.reshape(n, SUB, a.shape[-1])
    cnt = jnp.concatenate([cntp.reshape(nc, 8, LANES)[:, :N_GROUPS, 0],
                           cnts.reshape(nl, 8, LANES)[:, :N_GROUPS, 0]], axis=0)
    yp, ys = _moe(blk(x1p, nc), blk(x1s, nl), blk(h2p, nc), blk(h2s, nl), blk(cbp, nc), blk(cbs, nl),
                  rtp.reshape(nc, 8, SUB), rts.reshape(nl, 8, SUB),
                  cnt.astype(jnp.int32).reshape(-1), mod, Sd // SUB, w_e_gate[0], w_e_up[0], w_e_down[0],
                  norm_final_g.reshape(1, -1))

    return (yp.reshape(B, S, D_MODEL), ys.reshape(Bd, Sd, D_MODEL),
            c_new.reshape(B, 1, 2, N_HEADS, HEAD_DIM, HEAD_DIM),
            n_new.reshape(B, 1, 2, N_HEADS, HEAD_DIM),
            m_new[:, :, 0].reshape(B, 1, 2, N_HEADS))
```

```python
import functools

import jax
import jax.numpy as jnp
from jax import lax
from jax.experimental import pallas as pl
from jax.experimental.pallas import tpu as pltpu

D_MODEL = 1024
D_CONV = 512
CONV_K = 31
D_MLSTM = 512
N_HEADS = 4
HEAD_DIM = D_MLSTM // N_HEADS
N_GROUPS = 4
EXPERTS_PER_GROUP = 4
N_EXPERTS = N_GROUPS * EXPERTS_PER_GROUP
D_EXPERT = 256
N_ADA = 6
EPS = 1e-6
GRID_W = 64

LANES = 128
SUB = 256
CONV_PAD = 16
CONV_RB = 64
N_UNITS = 2 * N_HEADS
ROW_ALIGN = 16
SORT_ROWS = SUB + N_GROUPS * ROW_ALIGN
MOE_TM = 512
MIX_TM = 512
WPREP_ROWS = 512
ROUTE_GROUP_LANE = N_EXPERTS
ROUTE_RANK_LANE = N_EXPERTS + 1
VMEM_LIMIT = 58 * 1024 * 1024

BF16 = jnp.bfloat16
F32 = jnp.float32
NT_DIMS = (((1,), (1,)), ((), ()))


def _dot(a, b):
    return jnp.dot(a, b, preferred_element_type=F32)


def _dot_nt(a, b, precision=None):
    return lax.dot_general(a, b, NT_DIMS, preferred_element_type=F32, precision=precision)


def _sigmoid(x):
    return 0.5 * jnp.tanh(0.5 * x) + 0.5


def _log_sigmoid(x):
    return jnp.minimum(x, 0.0) - jnp.log1p(jnp.exp(-jnp.abs(x)))


def _split3(x):
    hi = x.astype(BF16).astype(F32)
    r1 = x - hi
    mid = r1.astype(BF16).astype(F32)
    lo = (r1 - mid).astype(BF16).astype(F32)
    return hi, mid, lo


def _ada_kernel(c_ref, w_ref, b_ref, o_ref):
    c = c_ref[...]
    s = (c * _sigmoid(c)).astype(BF16)
    o_ref[0] = _dot(s, w_ref[...].astype(BF16)) + b_ref[...]


def _ada(cin, w_ada, b_ada):
    return pl.pallas_call(
        _ada_kernel,
        grid=(N_ADA,),
        in_specs=[
            pl.BlockSpec((8, D_MODEL), lambda j: (0, 0)),
            pl.BlockSpec((D_MODEL, D_MODEL), lambda j: (0, j)),
            pl.BlockSpec((1, D_MODEL), lambda j: (0, j)),
        ],
        out_specs=pl.BlockSpec((1, 8, D_MODEL), lambda j: (j, 0, 0)),
        out_shape=jax.ShapeDtypeStruct((N_ADA, 8, D_MODEL), F32),
        compiler_params=pltpu.CompilerParams(dimension_semantics=("arbitrary",)),
        name="ada",
    )(cin, w_ada, b_ada)


def _transpose_cast_kernel(starts_ref, wt_ref, o_ref):
    o_ref[...] = wt_ref[...].T.astype(BF16)


def _transpose_cast(w_t, row_starts):
    n, k = len(row_starts), w_t.shape[1]
    return pl.pallas_call(
        _transpose_cast_kernel,
        grid_spec=pltpu.PrefetchScalarGridSpec(
            num_scalar_prefetch=1, grid=(n,),
            in_specs=[pl.BlockSpec((pl.Element(WPREP_ROWS), pl.Element(k)), lambda j, starts: (starts[j] * 8, 0))],
            out_specs=pl.BlockSpec((k, WPREP_ROWS), lambda j, starts: (0, j)),
        ),
        out_shape=jax.ShapeDtypeStruct((k, n * WPREP_ROWS), BF16),
        compiler_params=pltpu.CompilerParams(dimension_semantics=("arbitrary",)),
        name="transpose_cast",
    )(jnp.array([r // 8 for r in row_starts], jnp.int32), w_t)


WROW_OFFSET = {"wq": 2 * D_CONV, "wv": 2 * D_CONV + D_MLSTM, "wog": 2 * D_CONV + 2 * D_MLSTM,
               "wgm": 2 * D_CONV + 3 * D_MLSTM}

_MIXER_WEIGHTS = (
    "g1", "wrow", "bag", "bq", "wkT", "bk", "bv", "bog",
    "wgifT", "bgifT", "bgm", "wdw", "bdw", "lng", "lnb",
    "wco", "hng", "wmo", "wo", "g2", "wrt2", "brtT",
)


def _zero_after(x):
    bits = lax.bitcast_convert_type(x, jnp.uint32)
    bits = lax.shift_right_logical(lax.shift_right_logical(bits, jnp.uint32(16)), jnp.uint32(16))
    return lax.bitcast_convert_type(bits, F32)[0:1, :]


def _conv_block(upad_s, seg, base, cs, wdw_ref, bdw_ref, after=None):
    sub = 8
    first = CONV_PAD - CONV_K // 2
    acc = jnp.broadcast_to(bdw_ref[0:1, cs], (CONV_RB, LANES))
    for r in range(sub):
        z = None
        for a in range((CONV_K + first + sub - 1) // sub):
            j = sub * a + r - first
            if 0 <= j < CONV_K:
                lo = base + sub * a
                tap = wdw_ref[j:j + 1, cs] if after is None else wdw_ref[j:j + 1, cs] + after
                term = tap * upad_s[seg, lo:lo + CONV_RB + sub, cs]
                z = term if z is None else z + term
        acc = acc + z[r:r + CONV_RB, :]
    return acc


def _mixer_kernel(R, T, P, has_state, emit_state, mod_index, *refs):
    L = SUB
    n_mt = R // MIX_TM
    cpm = MIX_TM // L
    n_seq = R // T
    cps = T // L
    nseg = MIX_TM // P
    assert not has_state or n_seq == 1
    it = iter(refs)
    x_ref = next(it)
    mod_ref = next(it)
    if has_state:
        c0_ref = next(it)
        n0_ref = next(it)
        m0_ref = next(it)
    w = {name: next(it) for name in _MIXER_WEIGHTS}
    x1_ref = next(it)
    h2_ref = next(it)
    comb_ref = next(it)
    route_ref = next(it)
    cnt_ref = next(it)
    if emit_state:
        cout_ref = next(it)
        nout_ref = next(it)
        mout_ref = next(it)
    (q_s, kT_s, v_s, so_s, scan_s, ma_s, sgb_s, hm_s, cst_s, upad_s) = [next(it) for _ in range(10)]

    cond_row = mod_index(pl.program_id(0))

    def mod_row(i):
        return mod_ref[i, pl.ds(cond_row, 1), :]

    zpad = jnp.zeros((CONV_PAD, D_CONV), F32)
    for seg in range(nseg):
        upad_s[seg, 0:CONV_PAD, :] = zpad
        upad_s[seg, CONV_PAD + P:CONV_PAD + P + CONV_PAD, :] = zpad

    t_idx = lax.broadcasted_iota(jnp.int32, (L, L), 0)
    s_idx = lax.broadcasted_iota(jnp.int32, (L, L), 1)
    lower = s_idx <= t_idx
    upper = s_idx >= t_idx
    triu_b = upper.astype(F32).astype(BF16)
    lane_u = lax.broadcasted_iota(jnp.int32, (N_UNITS, L), 1)
    is_bwd = lax.broadcasted_iota(jnp.int32, (N_UNITS, L), 0) >= N_HEADS

    def gate_scan(g):
        gi, lf = g[:N_UNITS], _log_sigmoid(g[N_UNITS:])
        pr = _dot(jnp.concatenate(_split3(lf), axis=0).astype(BF16), triu_b)
        pre = pr[0:N_UNITS] + pr[N_UNITS:2 * N_UNITS] + pr[2 * N_UNITS:]
        tot = pre[:, L - 1:L]
        bsum = jnp.where(is_bwd, tot - pre + lf, pre)
        a = gi - bsum
        pm, sm, k = a, a, 1
        while k < L:
            pm = jnp.where(lane_u >= k, jnp.maximum(pm, pltpu.roll(pm, k, axis=1)), pm)
            sm = jnp.where(lane_u < L - k, jnp.maximum(sm, pltpu.roll(sm, L - k, axis=1)), sm)
            k *= 2
        wide = lambda v: jnp.broadcast_to(v, (N_UNITS, L))
        return jnp.concatenate([a, jnp.where(is_bwd, sm, pm), bsum, wide(tot),
                                wide(jnp.max(a, axis=1, keepdims=True))], axis=0)

    def phase1(i, carry):
        r0 = pl.multiple_of(i * MIX_TM, MIX_TM)
        rows = pl.ds(r0, MIX_TM)
        x = x_ref[0, rows, :]
        xn = x * lax.rsqrt(jnp.mean(x * x, axis=-1, keepdims=True) + EPS) * w["g1"][...]
        hb = (xn * (1.0 + mod_row(1)) + mod_row(0)).astype(BF16)

        gates = _dot_nt(w["wgifT"][...].astype(BF16), hb) + w["bgifT"][...]
        for j in range(cpm):
            scan_s[i * cpm + j] = gate_scan(gates[:, j * L:(j + 1) * L])
        ag = _dot(hb, w["wrow"][:, :2 * D_CONV]) + w["bag"][...]
        u = ag[:, :D_CONV] * _sigmoid(ag[:, D_CONV:])
        for seg in range(nseg):
            upad_s[seg, CONV_PAD:CONV_PAD + P, :] = u[seg * P:(seg + 1) * P, :]

        def proj(name, bias, c0, gate, width=2 * LANES):
            w0 = WROW_OFFSET[name] + c0
            b = w[bias][:, c0:c0 + width]
            if gate is not None:
                b = b + jnp.concatenate([gate] * (width // LANES), axis=1)
            return _dot(hb, w["wrow"][:, w0:w0 + width]) + b

        last = lambda z: z[-8:, -LANES:]

        def gm_a(c0, gate):
            z = proj("wgm", "bgm", c0, gate)
            ma_s[rows, c0:c0 + 2 * LANES] = _sigmoid(z)
            return last(z)

        def gm_b(c0, gate):
            z = proj("wgm", "bgm", D_MODEL + c0, gate)
            sgb_s[rows, c0:c0 + 2 * LANES] = _sigmoid(z)
            return last(z)

        def q_part(c0, gate):
            z = proj("wq", "bq", c0, gate)
            q_s[rows, c0:c0 + 2 * LANES] = (z * (HEAD_DIM ** -0.5)).astype(BF16)
            return last(z)

        def v_part(c0, gate):
            z = proj("wv", "bv", c0, gate)
            v_s[rows, c0:c0 + 2 * LANES] = z.astype(BF16)
            return last(z)

        def o_part(c0, gate):
            z = proj("wog", "bog", c0, gate)
            so_s[rows, c0:c0 + 2 * LANES] = _sigmoid(z)
            return last(z)

        def k_part(c0, gate):
            rs = slice(c0, c0 + 2 * LANES)
            b = w["bk"][rs, :] if gate is None else w["bk"][rs, :] + gate[:, 0:1]
            z = _dot_nt(w["wkT"][rs, :].astype(BF16), hb) + b
            kt = z.astype(BF16)
            for j in range(cpm):
                kT_s[i * cpm + j, rs, :] = kt[:, j * L:(j + 1) * L]
            return last(z)

        jobs = ([functools.partial(gm_a, c0) for c0 in range(0, D_MODEL, 2 * LANES)]
                + [functools.partial(gm_b, c0) for c0 in range(0, D_MODEL, 2 * LANES)]
                + [functools.partial(f, c0) for f in (q_part, k_part, v_part, o_part)
                   for c0 in range(0, D_MLSTM, 2 * LANES)])
        n_jobs = len(jobs)
        conv = {}
        after = None
        n_pieces = (D_CONV // LANES) * nseg * (P // CONV_RB)
        for cb in range(D_CONV // LANES):
            cs = slice(cb * LANES, (cb + 1) * LANES)
            for seg in range(nseg):
                for rb in range(P // CONV_RB):
                    blk = _conv_block(upad_s, seg, rb * CONV_RB, cs, w["wdw"], w["bdw"], after)
                    conv[(cb, seg, rb)] = blk
                    if jobs and len(conv) * n_jobs >= (n_jobs - len(jobs) + 1) * n_pieces:
                        after = _zero_after(jobs.pop(0)(_zero_after(blk[-8:, :])))
        for job in jobs:
            job(None)
        cu = jnp.concatenate(
            [jnp.concatenate([conv[(cb, seg, rb)] for seg in range(nseg) for rb in range(P // CONV_RB)], axis=0)
             for cb in range(D_CONV // LANES)], axis=1)
        mu = jnp.mean(cu, axis=-1, keepdims=True)
        cc = cu - mu
        cn = cc * lax.rsqrt(jnp.mean(cc * cc, axis=-1, keepdims=True) + EPS) * w["lng"][...] + w["lnb"][...]
        ca = (cn * _sigmoid(cn)).astype(BF16)
        ma_s[rows, :] = ma_s[rows, :] * _dot(ca, w["wco"][...])
        return carry

    if n_mt == 1:
        phase1(0, 0)
    else:
        lax.fori_loop(0, n_mt, phase1, 0)

    ones_col = (lax.broadcasted_iota(jnp.int32, (L, HEAD_DIM), 1) == 0).astype(F32).astype(BF16)
    pad_rows = jnp.zeros((LANES - 3 * N_UNITS, L), F32)

    def gate_prep(c, m_vec):
        sc = scan_s[c]
        a, run_max, bsum = sc[0:N_UNITS], sc[N_UNITS:2 * N_UNITS], sc[2 * N_UNITS:3 * N_UNITS]
        tot, a_max = sc[3 * N_UNITS:4 * N_UNITS, 0:1], sc[4 * N_UNITS:5 * N_UNITS, 0:1]
        big_m = jnp.maximum(m_vec, run_max)
        m_end = jnp.maximum(m_vec, a_max)
        cols = jnp.concatenate(
            [big_m, jnp.exp(m_vec - big_m), jnp.exp(-bsum - big_m), pad_rows], axis=0).T
        return a, cols, jnp.exp(a - m_end), jnp.exp(m_vec - m_end), tot + m_end

    qk_cache = {}

    def unit(d, hd, c, prep, first_chunk, want_state):
        a, cols, wk, decay, _ = prep
        rows = slice(c * L, (c + 1) * L)
        hs = slice(hd * HEAD_DIM, (hd + 1) * HEAD_DIM)
        idx = d * N_HEADS + hd
        col = lambda k: cols[:, k * N_UNITS + idx:k * N_UNITS + idx + 1]
        qc = q_s[rows, hs]
        kTc = kT_s[c, hs, :]
        vaug = jnp.concatenate([v_s[rows, hs], ones_col], axis=1)
        if cps == 1 and (hd, c) in qk_cache:
            qk = qk_cache[(hd, c)]
        else:
            qk = _dot(qc, kTc)
            qk_cache[(hd, c)] = qk
        w_intra = jnp.where(lower if d == 0 else upper, jnp.exp(a[idx:idx + 1, :] - col(0)), 0.0)
        nd = _dot((qk * w_intra).astype(BF16), vaug)
        if has_state or not first_chunk:
            nd = nd + col(1) * _dot(qc, cst_s[idx].astype(BF16))
        den = nd[:, HEAD_DIM:HEAD_DIM + 1]
        h = nd[:, :HEAD_DIM] * (1.0 / jnp.maximum(jnp.abs(den), col(2)))
        if d == 0:
            hm_s[rows, hs] = h
        else:
            hm_s[rows, hs] = hm_s[rows, hs] + h
        if want_state:
            kw = (kTc.astype(F32) * wk[idx:idx + 1, :]).astype(BF16)
            upd = _dot(kw, vaug)
            if has_state or not first_chunk:
                upd = upd + decay[idx:idx + 1, :] * cst_s[idx]
            cst_s[idx] = upd

    dir_rows = lax.broadcasted_iota(jnp.int32, (N_UNITS, 1), 0) >= N_HEADS
    for seq in range(n_seq):
        if has_state:
            n_cols = jnp.concatenate([n0_ref[0], jnp.zeros((LANES - N_UNITS, HEAD_DIM), F32)], axis=0).T
            first_lane = lax.broadcasted_iota(jnp.int32, (HEAD_DIM, HEAD_DIM), 1) == 0
            for idx in range(N_UNITS):
                cst_s[idx, :, :HEAD_DIM] = c0_ref[0, idx]
                cst_s[idx, :, HEAD_DIM:] = jnp.where(first_lane, n_cols[:, idx:idx + 1], 0.0)
            m_vec = m0_ref[0, :, 0:1]
        else:
            m_vec = jnp.zeros((N_UNITS, 1), F32)
        prep = None
        for d in range(2):
            order = list(range(cps)) if d == 0 else list(range(cps - 1, -1, -1))
            for pos, c in enumerate(order):
                if cps > 1 or prep is None:
                    prep = gate_prep(seq * cps + c, m_vec)
                for hd in range(N_HEADS):
                    unit(d, hd, seq * cps + c, prep, pos == 0, emit_state or pos < cps - 1)
                m_vec = jnp.where(dir_rows == (d == 1), prep[4], m_vec)
        if emit_state:
            for idx in range(N_UNITS):
                caug = cst_s[idx]
                cout_ref[0, seq * N_UNITS + idx] = caug[:, :HEAD_DIM]
                nout_ref[0, seq * N_UNITS + idx:seq * N_UNITS + idx + 1, :] = caug[:, HEAD_DIM:].T[0:1, :]
            mout_ref[0, seq * N_UNITS:(seq + 1) * N_UNITS, :] = jnp.broadcast_to(m_vec, (N_UNITS, LANES))

    e_iota = lax.broadcasted_iota(jnp.int32, (LANES, MIX_TM), 0)
    g_of_e = lax.shift_right_logical(e_iota, 2)
    j_of_e = lax.bitwise_and(e_iota, EXPERTS_PER_GROUP - 1)
    r8 = lax.broadcasted_iota(jnp.int32, (8, MIX_TM), 0)
    r8_blk = lax.broadcasted_iota(jnp.int32, (8, SUB), 0)
    before_b = (t_idx < s_idx).astype(F32).astype(BF16)

    def phase3(i, carry):
        r0 = pl.multiple_of(i * MIX_TM, MIX_TM)
        rows = pl.ds(r0, MIX_TM)
        hm = hm_s[rows, :]
        heads = []
        for hd in range(N_HEADS):
            hh = hm[:, hd * HEAD_DIM:(hd + 1) * HEAD_DIM]
            heads.append(hh * lax.rsqrt(jnp.mean(hh * hh, axis=-1, keepdims=True) + EPS))
        hn = jnp.concatenate(heads, axis=1) * w["hng"][...]
        hb2 = (so_s[rows, :] * hn).astype(BF16)
        br_b = _dot(hb2, w["wmo"][...])
        mixed = (ma_s[rows, :] + sgb_s[rows, :] * br_b).astype(BF16)
        x1 = x_ref[0, rows, :] + mod_row(2) * _dot(mixed, w["wo"][...])
        x1_ref[0, rows, :] = x1
        xn = x1 * lax.rsqrt(jnp.mean(x1 * x1, axis=-1, keepdims=True) + EPS) * w["g2"][...]
        h2 = xn * (1.0 + mod_row(4)) + mod_row(3)
        h2_ref[0, rows, :] = h2.astype(BF16)

        h2_hi = h2.astype(BF16)
        h2_lo = (h2 - h2_hi.astype(F32)).astype(BF16)
        lg = _dot(h2_hi, w["wrt2"][...])
        lg = lg[:, :LANES] + lg[:, LANES:] + _dot(h2_lo, w["wrt2"][:, :LANES])
        lt = lg.T + w["brtT"][...]
        gl = [lt[N_EXPERTS + g:N_EXPERTS + g + 1, :] for g in range(N_GROUPS)]
        best, gsel = gl[0], jnp.zeros((1, MIX_TM), jnp.int32)
        for g in range(1, N_GROUPS):
            better = gl[g] > best
            gsel = jnp.where(better, g, gsel)
            best = jnp.where(better, gl[g], best)
        gp_sel = 1.0 / sum(jnp.exp(v - best) for v in gl)
        el = []
        for j in range(EXPERTS_PER_GROUP):
            v = lt[j:j + 1, :]
            for g in range(1, N_GROUPS):
                r = g * EXPERTS_PER_GROUP + j
                v = jnp.where(gsel == g, lt[r:r + 1, :], v)
            el.append(v)
        l1, e1 = el[0], jnp.zeros((1, MIX_TM), jnp.int32)
        for j in range(1, EXPERTS_PER_GROUP):
            better = el[j] > l1
            e1 = jnp.where(better, j, e1)
            l1 = jnp.where(better, el[j], l1)
        l2 = jnp.full((1, MIX_TM), -jnp.inf, F32)
        e2 = jnp.zeros((1, MIX_TM), jnp.int32)
        for j in range(EXPERTS_PER_GROUP):
            better = jnp.logical_and(e1 != j, el[j] > l2)
            e2 = jnp.where(better, j, e2)
            l2 = jnp.where(better, el[j], l2)
        r2 = jnp.exp(l2 - l1)
        wt1 = gp_sel / (1.0 + r2)
        wt2 = gp_sel * r2 / (1.0 + r2)
        in_group = g_of_e == gsel
        comb_t = (jnp.where(jnp.logical_and(in_group, j_of_e == e1), wt1, 0.0)
                  + jnp.where(jnp.logical_and(in_group, j_of_e == e2), wt2, 0.0))

        onehot = (r8 == gsel).astype(F32)
        gsel_f = gsel.astype(F32)
        ranks = []
        for j in range(cpm):
            oh = onehot[:, j * SUB:(j + 1) * SUB]
            rank = jnp.sum(oh * _dot(oh.astype(BF16), before_b), axis=0, keepdims=True)
            ranks.append(rank)
            r8rows = pl.ds(pl.multiple_of((i * cpm + j) * 8, 8), 8)
            route_ref[0, r8rows, :] = jnp.where(r8_blk == 0, gsel_f[:, j * SUB:(j + 1) * SUB],
                                                jnp.where(r8_blk == 1, rank, 0.0))
            cnt_ref[0, r8rows, :] = jnp.broadcast_to(jnp.sum(oh, axis=1, keepdims=True), (8, LANES))
        comb_t = jnp.where(e_iota == ROUTE_GROUP_LANE, gsel_f,
                           jnp.where(e_iota == ROUTE_RANK_LANE, jnp.concatenate(ranks, axis=1), comb_t))
        comb_ref[0, rows, :] = comb_t.T
        return carry

    if n_mt == 1:
        phase3(0, 0)
    else:
        lax.fori_loop(0, n_mt, phase3, 0)


def _const_spec(a):
    nd = a.ndim
    return pl.BlockSpec(a.shape, lambda b, _nd=nd: (0,) * _nd, pipeline_mode=pl.Buffered(1))


def _mixer(x, T, mod, mod_index, weights, P, state=None, emit_state=False):
    B, R, _ = x.shape
    n_chunks = R // SUB
    n_seq = R // T
    has_state = state is not None
    seq_mode = {} if R <= MIX_TM else {"pipeline_mode": pl.Buffered(1)}
    in_specs = [
        pl.BlockSpec((1, R, D_MODEL), lambda b: (b, 0, 0), **seq_mode),
        pl.BlockSpec(mod.shape, lambda b: (0, 0, 0)),
    ]
    args = [x, mod]
    if has_state:
        c0, n0, m0 = state
        in_specs += [
            pl.BlockSpec((1, N_UNITS, HEAD_DIM, HEAD_DIM), lambda b: (b, 0, 0, 0)),
            pl.BlockSpec((1, N_UNITS, HEAD_DIM), lambda b: (b, 0, 0)),
            pl.BlockSpec((1, N_UNITS, LANES), lambda b: (b, 0, 0)),
        ]
        args += [c0, n0, m0]
    for name in _MIXER_WEIGHTS:
        in_specs.append(_const_spec(weights[name]))
        args.append(weights[name])
    out_shape = [
        jax.ShapeDtypeStruct((B, R, D_MODEL), F32),
        jax.ShapeDtypeStruct((B, R, D_MODEL), BF16),
        jax.ShapeDtypeStruct((B, R, LANES), F32),
        jax.ShapeDtypeStruct((B, n_chunks * 8, SUB), F32),
        jax.ShapeDtypeStruct((B, n_chunks * 8, LANES), F32),
    ]
    out_specs = [
        pl.BlockSpec((1, R, D_MODEL), lambda b: (b, 0, 0), **seq_mode),
        pl.BlockSpec((1, R, D_MODEL), lambda b: (b, 0, 0), **seq_mode),
        pl.BlockSpec((1, R, LANES), lambda b: (b, 0, 0)),
        pl.BlockSpec((1, n_chunks * 8, SUB), lambda b: (b, 0, 0)),
        pl.BlockSpec((1, n_chunks * 8, LANES), lambda b: (b, 0, 0)),
    ]
    if emit_state:
        out_shape += [
            jax.ShapeDtypeStruct((B, n_seq * N_UNITS, HEAD_DIM, HEAD_DIM), F32),
            jax.ShapeDtypeStruct((B, n_seq * N_UNITS, HEAD_DIM), F32),
            jax.ShapeDtypeStruct((B, n_seq * N_UNITS, LANES), F32),
        ]
        out_specs += [
            pl.BlockSpec((1, n_seq * N_UNITS, HEAD_DIM, HEAD_DIM), lambda b: (b, 0, 0, 0)),
            pl.BlockSpec((1, n_seq * N_UNITS, HEAD_DIM), lambda b: (b, 0, 0)),
            pl.BlockSpec((1, n_seq * N_UNITS, LANES), lambda b: (b, 0, 0)),
        ]
    scratch = [
        pltpu.VMEM((R, D_MLSTM), BF16),
        pltpu.VMEM((n_chunks, D_MLSTM, SUB), BF16),
        pltpu.VMEM((R, D_MLSTM), BF16),
        pltpu.VMEM((R, D_MLSTM), F32),
        pltpu.VMEM((n_chunks, 5 * N_UNITS, SUB), F32),
        pltpu.VMEM((R, D_MODEL), F32),
        pltpu.VMEM((R, D_MODEL), F32),
        pltpu.VMEM((R, D_MLSTM), F32),
        pltpu.VMEM((N_UNITS, HEAD_DIM, 2 * HEAD_DIM), F32),
        pltpu.VMEM((MIX_TM // P, P + 2 * CONV_PAD, D_CONV), F32),
    ]
    return pl.pallas_call(
        functools.partial(_mixer_kernel, R, T, P, has_state, emit_state, mod_index),
        grid=(B,),
        in_specs=in_specs,
        out_specs=out_specs,
        out_shape=out_shape,
        scratch_shapes=scratch,
        compiler_params=pltpu.CompilerParams(
            dimension_semantics=("arbitrary",), vmem_limit_bytes=VMEM_LIMIT),
        name="mixer_T%d" % T,
    )(*args)


def _dest_in_block(group, rank, starts):
    dest = rank
    for g in range(N_GROUPS):
        dest = dest + jnp.where(group == float(g), starts[g], 0.0)
    return dest


def _copy_segments(src_refs, dst_refs, src_starts, dst_starts, n_pieces):
    for g in range(N_GROUPS):
        def body(k, carry, g=g):
            s = pl.multiple_of(src_starts[g] + k * ROW_ALIGN, ROW_ALIGN)
            d = pl.multiple_of(dst_starts[g] + k * ROW_ALIGN, ROW_ALIGN)
            for src, dst in zip(src_refs, dst_refs):
                dst[pl.ds(d, ROW_ALIGN), :] = src[pl.ds(s, ROW_ALIGN), :]
            return carry
        lax.fori_loop(0, n_pieces[g], body, 0)


def _plan_segments(n_blocks, n_tiles, cnt_ref, start_ref, npiece_ref, off_ref, tgroup_ref, tvalid_ref, tfirst_ref):
    align_shift = ROW_ALIGN.bit_length() - 1
    tile_shift = MOE_TM.bit_length() - 1

    def block_starts(blk, carry):
        row = jnp.int32(0)
        for g in range(N_GROUPS):
            n = lax.shift_right_logical(cnt_ref[blk * N_GROUPS + g] + (ROW_ALIGN - 1), align_shift)
            npiece_ref[blk * N_GROUPS + g] = n
            start_ref[blk * N_GROUPS + g] = row
            row = row + n * ROW_ALIGN
        return carry

    lax.fori_loop(0, n_blocks, block_starts, 0)

    base_row = jnp.int32(0)
    base_tile = jnp.int32(0)
    last_group = jnp.int32(0)
    for g in range(N_GROUPS):
        def seg_offsets(blk, row, g=g, base_row=base_row):
            off_ref[blk * N_GROUPS + g] = base_row + row
            return row + npiece_ref[blk * N_GROUPS + g] * ROW_ALIGN

        rows = lax.fori_loop(0, n_blocks, seg_offsets, jnp.int32(0))
        tiles = lax.shift_right_logical(rows + (MOE_TM - 1), tile_shift)

        def mark_tiles(t, carry, g=g, base_tile=base_tile):
            tgroup_ref[base_tile + t] = g
            tvalid_ref[base_tile + t] = 1
            tfirst_ref[base_tile + t] = (t == 0).astype(jnp.int32)
            return carry

        lax.fori_loop(0, tiles, mark_tiles, 0)
        last_group = jnp.where(tiles > 0, g, last_group)
        base_row = base_row + tiles * MOE_TM
        base_tile = base_tile + tiles

    def mark_unused(t, carry):
        tgroup_ref[t] = last_group
        tvalid_ref[t] = 0
        tfirst_ref[t] = 0
        return carry

    lax.fori_loop(base_tile, n_tiles, mark_unused, 0)


def _dispatch_kernel(n_ctx_blocks, n_blocks, n_tiles, cnt_ref,
                     h2c_ref, h2l_ref, cbc_ref, cbl_ref, rtc_ref, rtl_ref,
                     xs_ref, cs_ref, start_ref, npiece_ref, off_ref, tgroup_ref, tvalid_ref, tfirst_ref,
                     sx_s, sc_s):
    b = pl.program_id(0)
    is_ctx = b < n_ctx_blocks

    @pl.when(b == 0)
    def _():
        _plan_segments(n_blocks, n_tiles, cnt_ref, start_ref, npiece_ref, off_ref,
                       tgroup_ref, tvalid_ref, tfirst_ref)
        xs_ref[...] = jnp.zeros_like(xs_ref)
        cs_ref[...] = jnp.zeros_like(cs_ref)

    h2 = jnp.where(is_ctx, h2c_ref[0], h2l_ref[0])
    cb = jnp.where(is_ctx, cbc_ref[0], cbl_ref[0])
    rt = jnp.where(is_ctx, rtc_ref[0], rtl_ref[0])
    starts = [start_ref[b * N_GROUPS + g] for g in range(N_GROUPS)]
    dest = _dest_in_block(rt[0:1, :], rt[1:2, :], [s.astype(F32) for s in starts])
    row = lax.broadcasted_iota(jnp.int32, (SORT_ROWS, SUB), 0).astype(F32)
    perm = (row == dest).astype(F32).astype(BF16)
    cb_hi = cb.astype(BF16)
    cb_lo = (cb - cb_hi.astype(F32)).astype(BF16)
    sx_s[...] = _dot(perm, h2).astype(BF16)
    sc_s[...] = _dot(perm, jnp.concatenate([cb_hi, cb_lo], axis=1)).astype(BF16)
    _copy_segments((sx_s, sc_s), (xs_ref, cs_ref), starts,
                   [off_ref[b * N_GROUPS + g] for g in range(N_GROUPS)],
                   [npiece_ref[b * N_GROUPS + g] for g in range(N_GROUPS)])


def _experts_kernel(tgroup_ref, tvalid_ref, tfirst_ref, xs_ref, cs_ref, wg_ref, wu_ref, wd_ref, ys_ref,
                    wg_s, wu_s, wd_s):
    i = pl.program_id(0)

    @pl.when(tfirst_ref[i] == 1)
    def _():
        for j in range(EXPERTS_PER_GROUP):
            cols = slice(j * D_EXPERT, (j + 1) * D_EXPERT)
            wg_s[:, cols] = wg_ref[j].astype(BF16)
            wu_s[:, cols] = wu_ref[j].astype(BF16)
            wd_s[cols, :] = wd_ref[j].astype(BF16)

    @pl.when(tvalid_ref[i] == 1)
    def _():
        x = xs_ref[...]
        g = _dot(x, wg_s[...])
        u = _dot(x, wu_s[...])
        comb = cs_ref[:, :LANES].astype(F32) + cs_ref[:, LANES:].astype(F32)
        lane = lax.broadcasted_iota(jnp.int32, comb.shape, 1)
        first = tgroup_ref[i] * EXPERTS_PER_GROUP
        parts = []
        for j in range(EXPERTS_PER_GROUP):
            cols = slice(j * D_EXPERT, (j + 1) * D_EXPERT)
            cw = jnp.sum(jnp.where(lane == first + j, comb, 0.0), axis=1, keepdims=True)
            gj = g[:, cols]
            parts.append((gj * _sigmoid(gj) * u[:, cols] * cw).astype(BF16))
        ys_ref[...] = _dot(jnp.concatenate(parts, axis=1), wd_s[...]).astype(BF16)

    @pl.when(tvalid_ref[i] == 0)
    def _():
        ys_ref[...] = jnp.zeros_like(ys_ref)


def _combine_kernel(n_ctx_blocks, blocks_per_lat_seq, start_ref, npiece_ref, off_ref,
                    x1c_ref, x1l_ref, cbc_ref, cbl_ref, ys_ref, mod_ref, gf_ref, yc_ref, yl_ref, loc_s):
    b = pl.program_id(0)
    is_ctx = b < n_ctx_blocks
    starts = [start_ref[b * N_GROUPS + g] for g in range(N_GROUPS)]
    loc_s[...] = jnp.zeros_like(loc_s)
    _copy_segments((ys_ref,), (loc_s,), [off_ref[b * N_GROUPS + g] for g in range(N_GROUPS)], starts,
                   [npiece_ref[b * N_GROUPS + g] for g in range(N_GROUPS)])
    cb = jnp.where(is_ctx, cbc_ref[0], cbl_ref[0])
    dest = _dest_in_block(cb[:, ROUTE_GROUP_LANE:ROUTE_GROUP_LANE + 1],
                          cb[:, ROUTE_RANK_LANE:ROUTE_RANK_LANE + 1],
                          [s.astype(F32) for s in starts])
    col = lax.broadcasted_iota(jnp.int32, (SUB, SORT_ROWS), 1).astype(F32)
    unperm = (col == dest).astype(F32).astype(BF16)
    moe = _dot(unperm, loc_s[...])
    x1 = jnp.where(is_ctx, x1c_ref[0], x1l_ref[0])
    mrow = jnp.where(is_ctx, 0, 1 + jnp.maximum(b - n_ctx_blocks, 0) // blocks_per_lat_seq)
    x2 = x1 + mod_ref[N_ADA - 1, pl.ds(mrow, 1), :] * moe
    y = x2 * lax.rsqrt(jnp.mean(x2 * x2, axis=-1, keepdims=True) + EPS) * gf_ref[...]

    @pl.when(is_ctx)
    def _():
        yc_ref[0] = y

    @pl.when(jnp.logical_not(is_ctx))
    def _():
        yl_ref[0] = y


def _moe(x1c, x1l, h2c, h2l, cbc, cbl, rtc, rtl, cnt, mod, blocks_per_lat_seq, wg, wu, wd, gf):
    nc, nl = x1c.shape[0], x1l.shape[0]
    nb = nc + nl
    n_rows_max = nb * SUB + nb * N_GROUPS * (ROW_ALIGN - 1) + N_GROUPS * (MOE_TM - ROW_ALIGN)
    n_tiles = -(-n_rows_max // MOE_TM)
    ns = n_tiles * MOE_TM

    cmap = lambda b, *_: (jnp.minimum(b, nc - 1), 0, 0)
    lmap = lambda b, *_: (jnp.maximum(b - nc, 0), 0, 0)
    whole = lambda *_: (0, 0)
    once = {"pipeline_mode": pl.Buffered(1)}
    arb = pltpu.CompilerParams(dimension_semantics=("arbitrary",), vmem_limit_bytes=VMEM_LIMIT)
    smem = pl.BlockSpec(memory_space=pltpu.SMEM)
    seg_i32 = jax.ShapeDtypeStruct((nb * N_GROUPS,), jnp.int32)
    tile_i32 = jax.ShapeDtypeStruct((n_tiles,), jnp.int32)

    xs, cs, start, npiece, off, tgroup, tvalid, tfirst = pl.pallas_call(
        functools.partial(_dispatch_kernel, nc, nb, n_tiles),
        grid_spec=pltpu.PrefetchScalarGridSpec(
            num_scalar_prefetch=1, grid=(nb,),
            in_specs=[
                pl.BlockSpec((1, SUB, D_MODEL), cmap), pl.BlockSpec((1, SUB, D_MODEL), lmap),
                pl.BlockSpec((1, SUB, LANES), cmap), pl.BlockSpec((1, SUB, LANES), lmap),
                pl.BlockSpec((1, 8, SUB), cmap), pl.BlockSpec((1, 8, SUB), lmap),
            ],
            out_specs=[pl.BlockSpec((ns, D_MODEL), whole, **once), pl.BlockSpec((ns, 2 * LANES), whole, **once),
                       smem, smem, smem, smem, smem, smem],
            scratch_shapes=[pltpu.VMEM((SORT_ROWS, D_MODEL), BF16), pltpu.VMEM((SORT_ROWS, 2 * LANES), BF16)],
        ),
        out_shape=[jax.ShapeDtypeStruct((ns, D_MODEL), BF16), jax.ShapeDtypeStruct((ns, 2 * LANES), BF16),
                   seg_i32, seg_i32, seg_i32, tile_i32, tile_i32, tile_i32],
        compiler_params=arb,
        name="moe_dispatch",
    )(cnt, h2c, h2l, cbc, cbl, rtc, rtl)

    wmap = lambda i, tg, tv, tf: (tg[i], 0, 0)
    ys = pl.pallas_call(
        _experts_kernel,
        grid_spec=pltpu.PrefetchScalarGridSpec(
            num_scalar_prefetch=3, grid=(n_tiles,),
            in_specs=[
                pl.BlockSpec((MOE_TM, D_MODEL), lambda i, *_: (i, 0)),
                pl.BlockSpec((MOE_TM, 2 * LANES), lambda i, *_: (i, 0)),
                pl.BlockSpec((EXPERTS_PER_GROUP, D_MODEL, D_EXPERT), wmap),
                pl.BlockSpec((EXPERTS_PER_GROUP, D_MODEL, D_EXPERT), wmap),
                pl.BlockSpec((EXPERTS_PER_GROUP, D_EXPERT, D_MODEL), wmap),
            ],
            out_specs=pl.BlockSpec((MOE_TM, D_MODEL), lambda i, *_: (i, 0)),
            scratch_shapes=[pltpu.VMEM((D_MODEL, EXPERTS_PER_GROUP * D_EXPERT), BF16),
                            pltpu.VMEM((D_MODEL, EXPERTS_PER_GROUP * D_EXPERT), BF16),
                            pltpu.VMEM((EXPERTS_PER_GROUP * D_EXPERT, D_MODEL), BF16)],
        ),
        out_shape=jax.ShapeDtypeStruct((ns, D_MODEL), BF16),
        compiler_params=arb,
        name="moe_experts",
    )(tgroup, tvalid, tfirst, xs, cs, wg, wu, wd)

    yc, yl = pl.pallas_call(
        functools.partial(_combine_kernel, nc, blocks_per_lat_seq),
        grid_spec=pltpu.PrefetchScalarGridSpec(
            num_scalar_prefetch=3, grid=(nb,),
            in_specs=[
                pl.BlockSpec((1, SUB, D_MODEL), cmap), pl.BlockSpec((1, SUB, D_MODEL), lmap),
                pl.BlockSpec((1, SUB, LANES), cmap), pl.BlockSpec((1, SUB, LANES), lmap),
                pl.BlockSpec((ns, D_MODEL), whole, **once),
                pl.BlockSpec(mod.shape, lambda *_: (0, 0, 0)),
                pl.BlockSpec((1, D_MODEL), whole),
            ],
            out_specs=[pl.BlockSpec((1, SUB, D_MODEL), cmap), pl.BlockSpec((1, SUB, D_MODEL), lmap)],
            scratch_shapes=[pltpu.VMEM((SORT_ROWS, D_MODEL), BF16)],
        ),
        out_shape=[jax.ShapeDtypeStruct((nc, SUB, D_MODEL), F32), jax.ShapeDtypeStruct((nl, SUB, D_MODEL), F32)],
        compiler_params=arb,
        name="moe_combine",
    )(start, npiece, off, x1c, x1l, cbc, cbl, ys, mod, gf)
    return yc, yl


def _prep_weights(norm1_g, w_in, b_in, b_gates, w_dw, b_dw, conv_ln_g, conv_ln_b, w_conv_out,
                  mlstm_hn_g, w_mlstm_out, w_o, norm2_g, w_rg, b_rg, w_re, b_re):
    s_a = 2 * D_CONV
    s_q = s_a + D_MLSTM
    s_k = s_q + D_MLSTM
    s_v = s_k + D_MLSTM
    s_o = s_v + D_MLSTM
    s_g = s_o + 4 * N_HEADS
    row = lambda v: v.reshape(1, -1).astype(F32)
    w_t = w_in.T
    keep = [(0, s_q), (s_k, s_o), (s_g, w_in.shape[1])]
    wrow = _transpose_cast(w_t, [r for a, b in keep for r in range(a, b, WPREP_ROWS)])
    w_g = w_t[s_o:s_g].reshape(2, 2, N_HEADS, D_MODEL).transpose(1, 0, 2, 3).reshape(4 * N_HEADS, D_MODEL)
    bg = (b_in[s_o:s_g] + b_gates.reshape(-1)).reshape(2, 2, N_HEADS).transpose(1, 0, 2).reshape(-1, 1)
    n_rt = N_EXPERTS + N_GROUPS
    wrt = jnp.pad(jnp.concatenate([w_re, w_rg], axis=1), ((0, 0), (0, LANES - n_rt)))
    wrt_hi = wrt.astype(BF16)
    wrt2 = jnp.concatenate([wrt_hi, (wrt - wrt_hi.astype(F32)).astype(BF16)], axis=1)
    brtT = jnp.pad(jnp.concatenate([b_re, b_rg]), (0, LANES - n_rt)).reshape(LANES, 1)
    return {
        "g1": row(norm1_g),
        "wrow": wrow, "bag": row(b_in[:s_a]), "bq": row(b_in[s_a:s_q]),
        "wkT": w_t[s_q:s_k], "bk": b_in[s_q:s_k].reshape(-1, 1),
        "bv": row(b_in[s_k:s_v]), "bog": row(b_in[s_v:s_o]),
        "wgifT": w_g, "bgifT": bg, "bgm": row(b_in[s_g:]),
        "wdw": w_dw.astype(F32), "bdw": row(b_dw), "lng": row(conv_ln_g), "lnb": row(conv_ln_b),
        "wco": w_conv_out.astype(BF16), "hng": row(mlstm_hn_g), "wmo": w_mlstm_out.astype(BF16),
        "wo": w_o.astype(BF16), "g2": row(norm2_g), "wrt2": wrt2, "brtT": brtT,
    }


def kernel(x_prompt, x_sample, state_C, state_n, state_m, c, c_ctx, norm1_g, w_ada, b_ada, w_in, b_in, b_gates, w_dw, b_dw, conv_ln_g, conv_ln_b, w_conv_out, mlstm_hn_g, w_mlstm_out, w_o, norm2_g, w_rg, b_rg, w_re, b_re, w_e_gate, w_e_up, w_e_down, norm_final_g):
    B, S, _ = x_prompt.shape
    Bd, Sd, _ = x_sample.shape
    assert w_ada.shape[0] == 1, "single trunk layer"
    assert S == SUB and Sd % SUB == 0

    cin = jnp.concatenate([c_ctx[None, :], c, jnp.zeros((8 - 1 - Bd, D_MODEL), F32)], axis=0)
    mod = _ada(cin, w_ada[0], b_ada[0].reshape(1, -1))

    wts = _prep_weights(norm1_g[0], w_in[0], b_in[0], b_gates[0], w_dw[0], b_dw[0], conv_ln_g[0],
                        conv_ln_b[0], w_conv_out[0], mlstm_hn_g[0], w_mlstm_out[0], w_o[0],
                        norm2_g[0], w_rg[0], b_rg[0], w_re[0], b_re[0])

    x1p, h2p, cbp, rtp, cntp, c_new, n_new, m_new = _mixer(
        x_prompt.reshape(B * S // MIX_TM, MIX_TM, D_MODEL), S, mod, lambda b: 0, wts, P=S, emit_state=True)

    m0 = jnp.broadcast_to(state_m[:, 0].reshape(Bd, N_UNITS, 1), (Bd, N_UNITS, LANES))
    state = (state_C[:, 0].reshape(Bd, N_UNITS, HEAD_DIM, HEAD_DIM), state_n[:, 0].reshape(Bd, N_UNITS, HEAD_DIM), m0)
    x1s, h2s, cbs, rts, cnts = _mixer(x_sample, Sd, mod, lambda b: 1 + b, wts, P=GRID_W, state=state)

    nc, nl = B * S // SUB, Bd * Sd // SUB
    blk = lambda a, n: a.reshape(n, SUB, a.shape[-1])
    cnt = jnp.concatenate([cntp.reshape(nc, 8, LANES)[:, :N_GROUPS, 0],
                           cnts.reshape(nl, 8, LANES)[:, :N_GROUPS, 0]], axis=0)
    yp, ys = _moe(blk(x1p, nc), blk(x1s, nl), blk(h2p, nc), blk(h2s, nl), blk(cbp, nc), blk(cbs, nl),
                  rtp.reshape(nc, 8, SUB), rts.reshape(nl, 8, SUB),
                  cnt.astype(jnp.int32).reshape(-1), mod, Sd // SUB, w_e_gate[0], w_e_up[0], w_e_down[0],
                  norm_final_g.reshape(1, -1))

    return (yp.reshape(B, S, D_MODEL), ys.reshape(Bd, Sd, D_MODEL),
            c_new.reshape(B, 1, 2, N_HEADS, HEAD_DIM, HEAD_DIM),
            n_new.reshape(B, 1, 2, N_HEADS, HEAD_DIM),
            m_new[:, :, 0].reshape(B, 1, 2, N_HEADS))
```

```python
import functools

import jax
import jax.numpy as jnp
from jax import lax
from jax.experimental import pallas as pl
from jax.experimental.pallas import tpu as pltpu

D_MODEL = 1024
D_CONV = 512
CONV_K = 31
D_MLSTM = 512
N_HEADS = 4
HEAD_DIM = D_MLSTM // N_HEADS
N_GROUPS = 4
EXPERTS_PER_GROUP = 4
N_EXPERTS = N_GROUPS * EXPERTS_PER_GROUP
D_EXPERT = 256
N_ADA = 6
EPS = 1e-6
GRID_W = 64

LANES = 128
SUB = 256
CONV_PAD = 16
CONV_RB = 64
N_UNITS = 2 * N_HEADS
ROW_ALIGN = 16
SORT_ROWS = SUB + N_GROUPS * ROW_ALIGN
MOE_TM = 512
MIX_TM = 512
WPREP_ROWS = 512
ROUTE_GROUP_LANE = N_EXPERTS
ROUTE_RANK_LANE = N_EXPERTS + 1
VMEM_LIMIT = 58 * 1024 * 1024

BF16 = jnp.bfloat16
F32 = jnp.float32
NT_DIMS = (((1,), (1,)), ((), ()))


def _dot(a, b):
    return jnp.dot(a, b, preferred_element_type=F32)


def _dot_nt(a, b, precision=None):
    return lax.dot_general(a, b, NT_DIMS, preferred_element_type=F32, precision=precision)


def _sigmoid(x):
    return 0.5 * jnp.tanh(0.5 * x) + 0.5


def _log_sigmoid(x):
    return jnp.minimum(x, 0.0) - jnp.log1p(jnp.exp(-jnp.abs(x)))


def _mean_sq(x):
    k = x.shape[1]
    return _dot((x * x).astype(BF16), jnp.ones((k, LANES), BF16)) * (1.0 / k)


def _scale_rows(x, r):
    return jnp.concatenate([x[:, c:c + LANES] * r for c in range(0, x.shape[1], LANES)], axis=1)


def _split3(x):
    hi = x.astype(BF16).astype(F32)
    r1 = x - hi
    mid = r1.astype(BF16).astype(F32)
    lo = (r1 - mid).astype(BF16).astype(F32)
    return hi, mid, lo


def _ada_kernel(c_ref, w_ref, b_ref, o_ref):
    c = c_ref[...]
    s = (c * _sigmoid(c)).astype(BF16)
    o_ref[0] = _dot(s, w_ref[...].astype(BF16)) + b_ref[...]


def _ada(cin, w_ada, b_ada):
    return pl.pallas_call(
        _ada_kernel,
        grid=(N_ADA,),
        in_specs=[
            pl.BlockSpec((8, D_MODEL), lambda j: (0, 0)),
            pl.BlockSpec((D_MODEL, D_MODEL), lambda j: (0, j)),
            pl.BlockSpec((1, D_MODEL), lambda j: (0, j)),
        ],
        out_specs=pl.BlockSpec((1, 8, D_MODEL), lambda j: (j, 0, 0)),
        out_shape=jax.ShapeDtypeStruct((N_ADA, 8, D_MODEL), F32),
        compiler_params=pltpu.CompilerParams(dimension_semantics=("arbitrary",)),
        name="ada",
    )(cin, w_ada, b_ada)


def _transpose_cast_kernel(starts_ref, wt_ref, o_ref):
    o_ref[...] = wt_ref[...].T.astype(BF16)


def _transpose_cast(w_t, row_starts):
    n, k = len(row_starts), w_t.shape[1]
    return pl.pallas_call(
        _transpose_cast_kernel,
        grid_spec=pltpu.PrefetchScalarGridSpec(
            num_scalar_prefetch=1, grid=(n,),
            in_specs=[pl.BlockSpec((pl.Element(WPREP_ROWS), pl.Element(k)), lambda j, starts: (starts[j] * 8, 0))],
            out_specs=pl.BlockSpec((k, WPREP_ROWS), lambda j, starts: (0, j)),
        ),
        out_shape=jax.ShapeDtypeStruct((k, n * WPREP_ROWS), BF16),
        compiler_params=pltpu.CompilerParams(dimension_semantics=("arbitrary",)),
        name="transpose_cast",
    )(jnp.array([r // 8 for r in row_starts], jnp.int32), w_t)


WROW_OFFSET = {"wq": 2 * D_CONV, "wv": 2 * D_CONV + D_MLSTM, "wog": 2 * D_CONV + 2 * D_MLSTM,
               "wgm": 2 * D_CONV + 3 * D_MLSTM}

_MIXER_WEIGHTS = (
    "g1", "wrow", "bag", "bq", "wkT", "bk", "bv", "bog",
    "wgifT", "bgifT", "bgm", "wdw", "bdw", "lng", "lnb",
    "wco", "hng", "wmo", "wo", "g2", "wrt2", "brtT",
)


def _zero_after(x):
    bits = lax.bitcast_convert_type(x, jnp.uint32)
    bits = lax.shift_right_logical(lax.shift_right_logical(bits, jnp.uint32(16)), jnp.uint32(16))
    return lax.bitcast_convert_type(bits, F32)[0:1, :]


def _conv_block(upad_s, seg, base, cs, wdw_ref, bdw_ref, after=None):
    sub = 8
    first = CONV_PAD - CONV_K // 2
    acc = jnp.broadcast_to(bdw_ref[0:1, cs], (CONV_RB, LANES))
    for r in range(sub):
        z = None
        for a in range((CONV_K + first + sub - 1) // sub):
            j = sub * a + r - first
            if 0 <= j < CONV_K:
                lo = base + sub * a
                tap = wdw_ref[j:j + 1, cs] if after is None else wdw_ref[j:j + 1, cs] + after
                term = tap * upad_s[seg, lo:lo + CONV_RB + sub, cs]
                z = term if z is None else z + term
        acc = acc + z[r:r + CONV_RB, :]
    return acc


def _mixer_kernel(R, T, P, has_state, emit_state, mod_index, *refs):
    L = SUB
    n_mt = R // MIX_TM
    cpm = MIX_TM // L
    n_seq = R // T
    cps = T // L
    nseg = MIX_TM // P
    assert not has_state or n_seq == 1
    it = iter(refs)
    x_ref = next(it)
    mod_ref = next(it)
    if has_state:
        c0_ref = next(it)
        n0_ref = next(it)
        m0_ref = next(it)
    w = {name: next(it) for name in _MIXER_WEIGHTS}
    x1_ref = next(it)
    h2_ref = next(it)
    comb_ref = next(it)
    route_ref = next(it)
    cnt_ref = next(it)
    if emit_state:
        cout_ref = next(it)
        nout_ref = next(it)
        mout_ref = next(it)
    (q_s, kT_s, v_s, so_s, scan_s, ma_s, sgb_s, hm_s, cst_s, upad_s) = [next(it) for _ in range(10)]

    cond_row = mod_index(pl.program_id(0))

    def mod_row(i):
        return mod_ref[i, pl.ds(cond_row, 1), :]

    zpad = jnp.zeros((CONV_PAD, D_CONV), F32)
    for seg in range(nseg):
        upad_s[seg, 0:CONV_PAD, :] = zpad
        upad_s[seg, CONV_PAD + P:CONV_PAD + P + CONV_PAD, :] = zpad

    t_idx = lax.broadcasted_iota(jnp.int32, (L, L), 0)
    s_idx = lax.broadcasted_iota(jnp.int32, (L, L), 1)
    lower = s_idx <= t_idx
    upper = s_idx >= t_idx
    triu_b = upper.astype(F32).astype(BF16)
    lane_u = lax.broadcasted_iota(jnp.int32, (N_UNITS, L), 1)
    is_bwd = lax.broadcasted_iota(jnp.int32, (N_UNITS, L), 0) >= N_HEADS

    def gate_scan(g):
        gi, lf = g[:N_UNITS], _log_sigmoid(g[N_UNITS:])
        pr = _dot(jnp.concatenate(_split3(lf), axis=0).astype(BF16), triu_b)
        pre = pr[0:N_UNITS] + pr[N_UNITS:2 * N_UNITS] + pr[2 * N_UNITS:]
        tot = pre[:, L - 1:L]
        bsum = jnp.where(is_bwd, tot - pre + lf, pre)
        a = gi - bsum
        pm, sm, k = a, a, 1
        while k < L:
            pm = jnp.where(lane_u >= k, jnp.maximum(pm, pltpu.roll(pm, k, axis=1)), pm)
            sm = jnp.where(lane_u < L - k, jnp.maximum(sm, pltpu.roll(sm, L - k, axis=1)), sm)
            k *= 2
        wide = lambda v: jnp.broadcast_to(v, (N_UNITS, L))
        return jnp.concatenate([a, jnp.where(is_bwd, sm, pm), bsum, wide(tot),
                                wide(jnp.max(a, axis=1, keepdims=True))], axis=0)

    def phase1(i, carry):
        r0 = pl.multiple_of(i * MIX_TM, MIX_TM)
        rows = pl.ds(r0, MIX_TM)
        x = x_ref[0, rows, :]
        xn = _scale_rows(x, lax.rsqrt(_mean_sq(x) + EPS))
        hb = (xn * (w["g1"][...] * (1.0 + mod_row(1))) + mod_row(0)).astype(BF16)

        gates = _dot_nt(w["wgifT"][...].astype(BF16), hb) + w["bgifT"][...]
        for j in range(cpm):
            scan_s[i * cpm + j] = gate_scan(gates[:, j * L:(j + 1) * L])
        ag = _dot(hb, w["wrow"][:, :2 * D_CONV]) + w["bag"][...]
        u = ag[:, :D_CONV] * _sigmoid(ag[:, D_CONV:])
        for seg in range(nseg):
            upad_s[seg, CONV_PAD:CONV_PAD + P, :] = u[seg * P:(seg + 1) * P, :]

        def proj(name, bias, c0, gate, width=2 * LANES):
            w0 = WROW_OFFSET[name] + c0
            b = w[bias][:, c0:c0 + width]
            if gate is not None:
                b = b + jnp.concatenate([gate] * (width // LANES), axis=1)
            return _dot(hb, w["wrow"][:, w0:w0 + width]) + b

        last = lambda z: z[-8:, -LANES:]

        def gm_a(c0, gate):
            z = proj("wgm", "bgm", c0, gate)
            ma_s[rows, c0:c0 + 2 * LANES] = _sigmoid(z)
            return last(z)

        def gm_b(c0, gate):
            z = proj("wgm", "bgm", D_MODEL + c0, gate)
            sgb_s[rows, c0:c0 + 2 * LANES] = _sigmoid(z)
            return last(z)

        def q_part(c0, gate):
            z = proj("wq", "bq", c0, gate)
            q_s[rows, c0:c0 + 2 * LANES] = (z * (HEAD_DIM ** -0.5)).astype(BF16)
            return last(z)

        def v_part(c0, gate):
            z = proj("wv", "bv", c0, gate)
            v_s[rows, c0:c0 + 2 * LANES] = z.astype(BF16)
            return last(z)

        def o_part(c0, gate):
            z = proj("wog", "bog", c0, gate)
            so_s[rows, c0:c0 + 2 * LANES] = _sigmoid(z)
            return last(z)

        def k_part(c0, gate):
            rs = slice(c0, c0 + 2 * LANES)
            b = w["bk"][rs, :] if gate is None else w["bk"][rs, :] + gate[:, 0:1]
            z = _dot_nt(w["wkT"][rs, :].astype(BF16), hb) + b
            kt = z.astype(BF16)
            for j in range(cpm):
                kT_s[i * cpm + j, rs, :] = kt[:, j * L:(j + 1) * L]
            return last(z)

        jobs = ([functools.partial(gm_a, c0) for c0 in range(0, D_MODEL, 2 * LANES)]
                + [functools.partial(gm_b, c0) for c0 in range(0, D_MODEL, 2 * LANES)]
                + [functools.partial(f, c0) for f in (q_part, k_part, v_part, o_part)
                   for c0 in range(0, D_MLSTM, 2 * LANES)])
        n_jobs = len(jobs)
        conv = {}
        after = None
        n_pieces = (D_CONV // LANES) * nseg * (P // CONV_RB)
        for cb in range(D_CONV // LANES):
            cs = slice(cb * LANES, (cb + 1) * LANES)
            for seg in range(nseg):
                for rb in range(P // CONV_RB):
                    blk = _conv_block(upad_s, seg, rb * CONV_RB, cs, w["wdw"], w["bdw"], after)
                    conv[(cb, seg, rb)] = blk
                    if jobs and len(conv) * n_jobs >= (n_jobs - len(jobs) + 1) * n_pieces:
                        after = _zero_after(jobs.pop(0)(_zero_after(blk[-8:, :])))
        for job in jobs:
            job(None)
        cu = jnp.concatenate(
            [jnp.concatenate([conv[(cb, seg, rb)] for seg in range(nseg) for rb in range(P // CONV_RB)], axis=0)
             for cb in range(D_CONV // LANES)], axis=1)
        mu = jnp.mean(cu, axis=-1, keepdims=True)
        cc = cu - mu
        cn = _scale_rows(cc, lax.rsqrt(_mean_sq(cc) + EPS)) * w["lng"][...] + w["lnb"][...]
        ca = (cn * _sigmoid(cn)).astype(BF16)
        ma_s[rows, :] = ma_s[rows, :] * _dot(ca, w["wco"][...])
        return carry

    if n_mt == 1:
        phase1(0, 0)
    else:
        lax.fori_loop(0, n_mt, phase1, 0)

    ones_col = (lax.broadcasted_iota(jnp.int32, (L, HEAD_DIM), 1) == 0).astype(F32).astype(BF16)
    pad_rows = jnp.zeros((LANES - 3 * N_UNITS, L), F32)

    def gate_prep(c, m_vec):
        sc = scan_s[c]
        a, run_max, bsum = sc[0:N_UNITS], sc[N_UNITS:2 * N_UNITS], sc[2 * N_UNITS:3 * N_UNITS]
        tot, a_max = sc[3 * N_UNITS:4 * N_UNITS, 0:1], sc[4 * N_UNITS:5 * N_UNITS, 0:1]
        big_m = jnp.maximum(m_vec, run_max)
        m_end = jnp.maximum(m_vec, a_max)
        cols = jnp.concatenate(
            [big_m, jnp.exp(m_vec - big_m), jnp.exp(-bsum - big_m), pad_rows], axis=0).T
        return a, cols, jnp.exp(a - m_end), jnp.exp(m_vec - m_end), tot + m_end

    qk_cache = {}

    def unit(d, hd, c, prep, first_chunk, want_state):
        a, cols, wk, decay, _ = prep
        rows = slice(c * L, (c + 1) * L)
        hs = slice(hd * HEAD_DIM, (hd + 1) * HEAD_DIM)
        idx = d * N_HEADS + hd
        col = lambda k: cols[:, k * N_UNITS + idx:k * N_UNITS + idx + 1]
        qc = q_s[rows, hs]
        kTc = kT_s[c, hs, :]
        vaug = jnp.concatenate([v_s[rows, hs], ones_col], axis=1)
        if cps == 1 and (hd, c) in qk_cache:
            qk = qk_cache[(hd, c)]
        else:
            qk = _dot(qc, kTc)
            qk_cache[(hd, c)] = qk
        w_intra = jnp.where(lower if d == 0 else upper, jnp.exp(a[idx:idx + 1, :] - col(0)), 0.0)
        nd = _dot((qk * w_intra).astype(BF16), vaug)
        if has_state or not first_chunk:
            nd = nd + col(1) * _dot(qc, cst_s[idx].astype(BF16))
        den = nd[:, HEAD_DIM:HEAD_DIM + 1]
        h = nd[:, :HEAD_DIM] * (1.0 / jnp.maximum(jnp.abs(den), col(2)))
        if d == 0:
            hm_s[rows, hs] = h
        else:
            hm_s[rows, hs] = hm_s[rows, hs] + h
        if want_state:
            kw = (kTc.astype(F32) * wk[idx:idx + 1, :]).astype(BF16)
            upd = _dot(kw, vaug)
            if has_state or not first_chunk:
                upd = upd + decay[idx:idx + 1, :] * cst_s[idx]
            cst_s[idx] = upd

    dir_rows = lax.broadcasted_iota(jnp.int32, (N_UNITS, 1), 0) >= N_HEADS
    for seq in range(n_seq):
        if has_state:
            n_cols = jnp.concatenate([n0_ref[0], jnp.zeros((LANES - N_UNITS, HEAD_DIM), F32)], axis=0).T
            first_lane = lax.broadcasted_iota(jnp.int32, (HEAD_DIM, HEAD_DIM), 1) == 0
            for idx in range(N_UNITS):
                cst_s[idx, :, :HEAD_DIM] = c0_ref[0, idx]
                cst_s[idx, :, HEAD_DIM:] = jnp.where(first_lane, n_cols[:, idx:idx + 1], 0.0)
            m_vec = m0_ref[0, :, 0:1]
        else:
            m_vec = jnp.zeros((N_UNITS, 1), F32)
        prep = None
        for d in range(2):
            order = list(range(cps)) if d == 0 else list(range(cps - 1, -1, -1))
            for pos, c in enumerate(order):
                if cps > 1 or prep is None:
                    prep = gate_prep(seq * cps + c, m_vec)
                for hd in range(N_HEADS):
                    unit(d, hd, seq * cps + c, prep, pos == 0, emit_state or pos < cps - 1)
                m_vec = jnp.where(dir_rows == (d == 1), prep[4], m_vec)
        if emit_state:
            for idx in range(N_UNITS):
                caug = cst_s[idx]
                cout_ref[0, seq * N_UNITS + idx] = caug[:, :HEAD_DIM]
                nout_ref[0, seq * N_UNITS + idx:seq * N_UNITS + idx + 1, :] = caug[:, HEAD_DIM:].T[0:1, :]
            mout_ref[0, seq * N_UNITS:(seq + 1) * N_UNITS, :] = jnp.broadcast_to(m_vec, (N_UNITS, LANES))

    e_iota = lax.broadcasted_iota(jnp.int32, (LANES, MIX_TM), 0)
    g_of_e = lax.shift_right_logical(e_iota, 2)
    j_of_e = lax.bitwise_and(e_iota, EXPERTS_PER_GROUP - 1)
    r8 = lax.broadcasted_iota(jnp.int32, (8, MIX_TM), 0)
    r8_blk = lax.broadcasted_iota(jnp.int32, (8, SUB), 0)
    before_b = (t_idx < s_idx).astype(F32).astype(BF16)

    def phase3(i, carry):
        r0 = pl.multiple_of(i * MIX_TM, MIX_TM)
        rows = pl.ds(r0, MIX_TM)
        hm = hm_s[rows, :]
        heads = []
        for hd in range(N_HEADS):
            hh = hm[:, hd * HEAD_DIM:(hd + 1) * HEAD_DIM]
            heads.append(hh * lax.rsqrt(_mean_sq(hh) + EPS))
        hn = jnp.concatenate(heads, axis=1) * w["hng"][...]
        hb2 = (so_s[rows, :] * hn).astype(BF16)
        br_b = _dot(hb2, w["wmo"][...])
        mixed = (ma_s[rows, :] + sgb_s[rows, :] * br_b).astype(BF16)
        x1 = x_ref[0, rows, :] + mod_row(2) * _dot(mixed, w["wo"][...])
        x1_ref[0, rows, :] = x1
        xn = _scale_rows(x1, lax.rsqrt(_mean_sq(x1) + EPS))
        h2 = xn * (w["g2"][...] * (1.0 + mod_row(4))) + mod_row(3)
        h2_ref[0, rows, :] = h2.astype(BF16)

        h2_hi = h2.astype(BF16)
        h2_lo = (h2 - h2_hi.astype(F32)).astype(BF16)
        lg = _dot(h2_hi, w["wrt2"][...])
        lg = lg[:, :LANES] + lg[:, LANES:] + _dot(h2_lo, w["wrt2"][:, :LANES])
        lt = lg.T + w["brtT"][...]
        gl = [lt[N_EXPERTS + g:N_EXPERTS + g + 1, :] for g in range(N_GROUPS)]
        best, gsel = gl[0], jnp.zeros((1, MIX_TM), jnp.int32)
        for g in range(1, N_GROUPS):
            better = gl[g] > best
            gsel = jnp.where(better, g, gsel)
            best = jnp.where(better, gl[g], best)
        gp_sel = 1.0 / sum(jnp.exp(v - best) for v in gl)
        el = []
        for j in range(EXPERTS_PER_GROUP):
            v = lt[j:j + 1, :]
            for g in range(1, N_GROUPS):
                r = g * EXPERTS_PER_GROUP + j
                v = jnp.where(gsel == g, lt[r:r + 1, :], v)
            el.append(v)
        l1, e1 = el[0], jnp.zeros((1, MIX_TM), jnp.int32)
        for j in range(1, EXPERTS_PER_GROUP):
            better = el[j] > l1
            e1 = jnp.where(better, j, e1)
            l1 = jnp.where(better, el[j], l1)
        l2 = jnp.full((1, MIX_TM), -jnp.inf, F32)
        e2 = jnp.zeros((1, MIX_TM), jnp.int32)
        for j in range(EXPERTS_PER_GROUP):
            better = jnp.logical_and(e1 != j, el[j] > l2)
            e2 = jnp.where(better, j, e2)
            l2 = jnp.where(better, el[j], l2)
        r2 = jnp.exp(l2 - l1)
        wt1 = gp_sel / (1.0 + r2)
        wt2 = gp_sel * r2 / (1.0 + r2)
        in_group = g_of_e == gsel
        comb_t = (jnp.where(jnp.logical_and(in_group, j_of_e == e1), wt1, 0.0)
                  + jnp.where(jnp.logical_and(in_group, j_of_e == e2), wt2, 0.0))

        onehot = (r8 == gsel).astype(F32)
        gsel_f = gsel.astype(F32)
        ranks = []
        for j in range(cpm):
            oh = onehot[:, j * SUB:(j + 1) * SUB]
            rank = jnp.sum(oh * _dot(oh.astype(BF16), before_b), axis=0, keepdims=True)
            ranks.append(rank)
            r8rows = pl.ds(pl.multiple_of((i * cpm + j) * 8, 8), 8)
            route_ref[0, r8rows, :] = jnp.where(r8_blk == 0, gsel_f[:, j * SUB:(j + 1) * SUB],
                                                jnp.where(r8_blk == 1, rank, 0.0))
            cnt_ref[0, r8rows, :] = jnp.broadcast_to(jnp.sum(oh, axis=1, keepdims=True), (8, LANES))
        comb_t = jnp.where(e_iota == ROUTE_GROUP_LANE, gsel_f,
                           jnp.where(e_iota == ROUTE_RANK_LANE, jnp.concatenate(ranks, axis=1), comb_t))
        comb_ref[0, rows, :] = comb_t.T
        return carry

    if n_mt == 1:
        phase3(0, 0)
    else:
        lax.fori_loop(0, n_mt, phase3, 0)


def _const_spec(a):
    nd = a.ndim
    return pl.BlockSpec(a.shape, lambda b, _nd=nd: (0,) * _nd, pipeline_mode=pl.Buffered(1))


def _mixer(x, T, mod, mod_index, weights, P, state=None, emit_state=False):
    B, R, _ = x.shape
    n_chunks = R // SUB
    n_seq = R // T
    has_state = state is not None
    seq_mode = {} if R <= MIX_TM else {"pipeline_mode": pl.Buffered(1)}
    in_specs = [
        pl.BlockSpec((1, R, D_MODEL), lambda b: (b, 0, 0), **seq_mode),
        pl.BlockSpec(mod.shape, lambda b: (0, 0, 0)),
    ]
    args = [x, mod]
    if has_state:
        c0, n0, m0 = state
        in_specs += [
            pl.BlockSpec((1, N_UNITS, HEAD_DIM, HEAD_DIM), lambda b: (b, 0, 0, 0)),
            pl.BlockSpec((1, N_UNITS, HEAD_DIM), lambda b: (b, 0, 0)),
            pl.BlockSpec((1, N_UNITS, LANES), lambda b: (b, 0, 0)),
        ]
        args += [c0, n0, m0]
    for name in _MIXER_WEIGHTS:
        in_specs.append(_const_spec(weights[name]))
        args.append(weights[name])
    out_shape = [
        jax.ShapeDtypeStruct((B, R, D_MODEL), F32),
        jax.ShapeDtypeStruct((B, R, D_MODEL), BF16),
        jax.ShapeDtypeStruct((B, R, LANES), F32),
        jax.ShapeDtypeStruct((B, n_chunks * 8, SUB), F32),
        jax.ShapeDtypeStruct((B, n_chunks * 8, LANES), F32),
    ]
    out_specs = [
        pl.BlockSpec((1, R, D_MODEL), lambda b: (b, 0, 0), **seq_mode),
        pl.BlockSpec((1, R, D_MODEL), lambda b: (b, 0, 0), **seq_mode),
        pl.BlockSpec((1, R, LANES), lambda b: (b, 0, 0)),
        pl.BlockSpec((1, n_chunks * 8, SUB), lambda b: (b, 0, 0)),
        pl.BlockSpec((1, n_chunks * 8, LANES), lambda b: (b, 0, 0)),
    ]
    if emit_state:
        out_shape += [
            jax.ShapeDtypeStruct((B, n_seq * N_UNITS, HEAD_DIM, HEAD_DIM), F32),
            jax.ShapeDtypeStruct((B, n_seq * N_UNITS, HEAD_DIM), F32),
            jax.ShapeDtypeStruct((B, n_seq * N_UNITS, LANES), F32),
        ]
        out_specs += [
            pl.BlockSpec((1, n_seq * N_UNITS, HEAD_DIM, HEAD_DIM), lambda b: (b, 0, 0, 0)),
            pl.BlockSpec((1, n_seq * N_UNITS, HEAD_DIM), lambda b: (b, 0, 0)),
            pl.BlockSpec((1, n_seq * N_UNITS, LANES), lambda b: (b, 0, 0)),
        ]
    scratch = [
        pltpu.VMEM((R, D_MLSTM), BF16),
        pltpu.VMEM((n_chunks, D_MLSTM, SUB), BF16),
        pltpu.VMEM((R, D_MLSTM), BF16),
        pltpu.VMEM((R, D_MLSTM), F32),
        pltpu.VMEM((n_chunks, 5 * N_UNITS, SUB), F32),
        pltpu.VMEM((R, D_MODEL), F32),
        pltpu.VMEM((R, D_MODEL), F32),
        pltpu.VMEM((R, D_MLSTM), F32),
        pltpu.VMEM((N_UNITS, HEAD_DIM, 2 * HEAD_DIM), F32),
        pltpu.VMEM((MIX_TM // P, P + 2 * CONV_PAD, D_CONV), F32),
    ]
    return pl.pallas_call(
        functools.partial(_mixer_kernel, R, T, P, has_state, emit_state, mod_index),
        grid=(B,),
        in_specs=in_specs,
        out_specs=out_specs,
        out_shape=out_shape,
        scratch_shapes=scratch,
        compiler_params=pltpu.CompilerParams(
            dimension_semantics=("arbitrary",), vmem_limit_bytes=VMEM_LIMIT),
        name="mixer_T%d" % T,
    )(*args)


def _dest_in_block(group, rank, starts):
    dest = rank
    for g in range(N_GROUPS):
        dest = dest + jnp.where(group == float(g), starts[g], 0.0)
    return dest


def _copy_segments(src_refs, dst_refs, src_starts, dst_starts, n_pieces):
    for g in range(N_GROUPS):
        def body(k, carry, g=g):
            s = pl.multiple_of(src_starts[g] + k * ROW_ALIGN, ROW_ALIGN)
            d = pl.multiple_of(dst_starts[g] + k * ROW_ALIGN, ROW_ALIGN)
            for src, dst in zip(src_refs, dst_refs):
                dst[pl.ds(d, ROW_ALIGN), :] = src[pl.ds(s, ROW_ALIGN), :]
            return carry
        lax.fori_loop(0, n_pieces[g], body, 0)


def _plan_segments(n_blocks, n_tiles, cnt_ref, start_ref, npiece_ref, off_ref, tgroup_ref, tvalid_ref, tfirst_ref):
    align_shift = ROW_ALIGN.bit_length() - 1
    tile_shift = MOE_TM.bit_length() - 1

    def block_starts(blk, carry):
        row = jnp.int32(0)
        for g in range(N_GROUPS):
            n = lax.shift_right_logical(cnt_ref[blk * N_GROUPS + g] + (ROW_ALIGN - 1), align_shift)
            npiece_ref[blk * N_GROUPS + g] = n
            start_ref[blk * N_GROUPS + g] = row
            row = row + n * ROW_ALIGN
        return carry

    lax.fori_loop(0, n_blocks, block_starts, 0)

    base_row = jnp.int32(0)
    base_tile = jnp.int32(0)
    last_group = jnp.int32(0)
    for g in range(N_GROUPS):
        def seg_offsets(blk, row, g=g, base_row=base_row):
            off_ref[blk * N_GROUPS + g] = base_row + row
            return row + npiece_ref[blk * N_GROUPS + g] * ROW_ALIGN

        rows = lax.fori_loop(0, n_blocks, seg_offsets, jnp.int32(0))
        tiles = lax.shift_right_logical(rows + (MOE_TM - 1), tile_shift)

        def mark_tiles(t, carry, g=g, base_tile=base_tile):
            tgroup_ref[base_tile + t] = g
            tvalid_ref[base_tile + t] = 1
            tfirst_ref[base_tile + t] = (t == 0).astype(jnp.int32)
            return carry

        lax.fori_loop(0, tiles, mark_tiles, 0)
        last_group = jnp.where(tiles > 0, g, last_group)
        base_row = base_row + tiles * MOE_TM
        base_tile = base_tile + tiles

    def mark_unused(t, carry):
        tgroup_ref[t] = last_group
        tvalid_ref[t] = 0
        tfirst_ref[t] = 0
        return carry

    lax.fori_loop(base_tile, n_tiles, mark_unused, 0)


def _dispatch_kernel(n_ctx_blocks, n_blocks, n_tiles, cnt_ref,
                     h2c_ref, h2l_ref, cbc_ref, cbl_ref, rtc_ref, rtl_ref,
                     xs_ref, cs_ref, start_ref, npiece_ref, off_ref, tgroup_ref, tvalid_ref, tfirst_ref,
                     sx_s, sc_s):
    b = pl.program_id(0)
    is_ctx = b < n_ctx_blocks

    @pl.when(b == 0)
    def _():
        _plan_segments(n_blocks, n_tiles, cnt_ref, start_ref, npiece_ref, off_ref,
                       tgroup_ref, tvalid_ref, tfirst_ref)
        xs_ref[...] = jnp.zeros_like(xs_ref)
        cs_ref[...] = jnp.zeros_like(cs_ref)

    h2 = jnp.where(is_ctx, h2c_ref[0], h2l_ref[0])
    cb = jnp.where(is_ctx, cbc_ref[0], cbl_ref[0])
    rt = jnp.where(is_ctx, rtc_ref[0], rtl_ref[0])
    starts = [start_ref[b * N_GROUPS + g] for g in range(N_GROUPS)]
    dest = _dest_in_block(rt[0:1, :], rt[1:2, :], [s.astype(F32) for s in starts])
    row = lax.broadcasted_iota(jnp.int32, (SORT_ROWS, SUB), 0).astype(F32)
    perm = (row == dest).astype(F32).astype(BF16)
    cb_hi = cb.astype(BF16)
    cb_lo = (cb - cb_hi.astype(F32)).astype(BF16)
    sx_s[...] = _dot(perm, h2).astype(BF16)
    sc_s[...] = _dot(perm, jnp.concatenate([cb_hi, cb_lo], axis=1)).astype(BF16)
    _copy_segments((sx_s, sc_s), (xs_ref, cs_ref), starts,
                   [off_ref[b * N_GROUPS + g] for g in range(N_GROUPS)],
                   [npiece_ref[b * N_GROUPS + g] for g in range(N_GROUPS)])


def _experts_kernel(tgroup_ref, tvalid_ref, tfirst_ref, xs_ref, cs_ref, wg_ref, wu_ref, wd_ref, ys_ref,
                    wg_s, wu_s, wd_s):
    i = pl.program_id(0)

    @pl.when(tfirst_ref[i] == 1)
    def _():
        for j in range(EXPERTS_PER_GROUP):
            cols = slice(j * D_EXPERT, (j + 1) * D_EXPERT)
            wg_s[:, cols] = wg_ref[j].astype(BF16)
            wu_s[:, cols] = wu_ref[j].astype(BF16)
            wd_s[cols, :] = wd_ref[j].astype(BF16)

    @pl.when(tvalid_ref[i] == 1)
    def _():
        x = xs_ref[...]
        g = _dot(x, wg_s[...])
        u = _dot(x, wu_s[...])
        comb = cs_ref[:, :LANES].astype(F32) + cs_ref[:, LANES:].astype(F32)
        lane = lax.broadcasted_iota(jnp.int32, comb.shape, 1)
        first = tgroup_ref[i] * EXPERTS_PER_GROUP
        parts = []
        for j in range(EXPERTS_PER_GROUP):
            cols = slice(j * D_EXPERT, (j + 1) * D_EXPERT)
            cw = jnp.sum(jnp.where(lane == first + j, comb, 0.0), axis=1, keepdims=True)
            gj = g[:, cols]
            parts.append((gj * _sigmoid(gj) * u[:, cols] * cw).astype(BF16))
        ys_ref[...] = _dot(jnp.concatenate(parts, axis=1), wd_s[...]).astype(BF16)

    @pl.when(tvalid_ref[i] == 0)
    def _():
        ys_ref[...] = jnp.zeros_like(ys_ref)


def _combine_kernel(n_ctx_blocks, blocks_per_lat_seq, start_ref, npiece_ref, off_ref,
                    x1c_ref, x1l_ref, cbc_ref, cbl_ref, ys_ref, mod_ref, gf_ref, yc_ref, yl_ref, loc_s):
    b = pl.program_id(0)
    is_ctx = b < n_ctx_blocks
    starts = [start_ref[b * N_GROUPS + g] for g in range(N_GROUPS)]
    loc_s[...] = jnp.zeros_like(loc_s)
    _copy_segments((ys_ref,), (loc_s,), [off_ref[b * N_GROUPS + g] for g in range(N_GROUPS)], starts,
                   [npiece_ref[b * N_GROUPS + g] for g in range(N_GROUPS)])
    cb = jnp.where(is_ctx, cbc_ref[0], cbl_ref[0])
    dest = _dest_in_block(cb[:, ROUTE_GROUP_LANE:ROUTE_GROUP_LANE + 1],
                          cb[:, ROUTE_RANK_LANE:ROUTE_RANK_LANE + 1],
                          [s.astype(F32) for s in starts])
    col = lax.broadcasted_iota(jnp.int32, (SUB, SORT_ROWS), 1).astype(F32)
    unperm = (col == dest).astype(F32).astype(BF16)
    moe = _dot(unperm, loc_s[...])
    x1 = jnp.where(is_ctx, x1c_ref[0], x1l_ref[0])
    mrow = jnp.where(is_ctx, 0, 1 + jnp.maximum(b - n_ctx_blocks, 0) // blocks_per_lat_seq)
    x2 = x1 + mod_ref[N_ADA - 1, pl.ds(mrow, 1), :] * moe
    y = _scale_rows(x2, lax.rsqrt(_mean_sq(x2) + EPS)) * gf_ref[...]

    @pl.when(is_ctx)
    def _():
        yc_ref[0] = y

    @pl.when(jnp.logical_not(is_ctx))
    def _():
        yl_ref[0] = y


def _moe(x1c, x1l, h2c, h2l, cbc, cbl, rtc, rtl, cnt, mod, blocks_per_lat_seq, wg, wu, wd, gf):
    nc, nl = x1c.shape[0], x1l.shape[0]
    nb = nc + nl
    n_rows_max = nb * SUB + nb * N_GROUPS * (ROW_ALIGN - 1) + N_GROUPS * (MOE_TM - ROW_ALIGN)
    n_tiles = -(-n_rows_max // MOE_TM)
    ns = n_tiles * MOE_TM

    cmap = lambda b, *_: (jnp.minimum(b, nc - 1), 0, 0)
    lmap = lambda b, *_: (jnp.maximum(b - nc, 0), 0, 0)
    whole = lambda *_: (0, 0)
    once = {"pipeline_mode": pl.Buffered(1)}
    arb = pltpu.CompilerParams(dimension_semantics=("arbitrary",), vmem_limit_bytes=VMEM_LIMIT)
    smem = pl.BlockSpec(memory_space=pltpu.SMEM)
    seg_i32 = jax.ShapeDtypeStruct((nb * N_GROUPS,), jnp.int32)
    tile_i32 = jax.ShapeDtypeStruct((n_tiles,), jnp.int32)

    xs, cs, start, npiece, off, tgroup, tvalid, tfirst = pl.pallas_call(
        functools.partial(_dispatch_kernel, nc, nb, n_tiles),
        grid_spec=pltpu.PrefetchScalarGridSpec(
            num_scalar_prefetch=1, grid=(nb,),
            in_specs=[
                pl.BlockSpec((1, SUB, D_MODEL), cmap), pl.BlockSpec((1, SUB, D_MODEL), lmap),
                pl.BlockSpec((1, SUB, LANES), cmap), pl.BlockSpec((1, SUB, LANES), lmap),
                pl.BlockSpec((1, 8, SUB), cmap), pl.BlockSpec((1, 8, SUB), lmap),
            ],
            out_specs=[pl.BlockSpec((ns, D_MODEL), whole, **once), pl.BlockSpec((ns, 2 * LANES), whole, **once),
                       smem, smem, smem, smem, smem, smem],
            scratch_shapes=[pltpu.VMEM((SORT_ROWS, D_MODEL), BF16), pltpu.VMEM((SORT_ROWS, 2 * LANES), BF16)],
        ),
        out_shape=[jax.ShapeDtypeStruct((ns, D_MODEL), BF16), jax.ShapeDtypeStruct((ns, 2 * LANES), BF16),
                   seg_i32, seg_i32, seg_i32, tile_i32, tile_i32, tile_i32],
        compiler_params=arb,
        name="moe_dispatch",
    )(cnt, h2c, h2l, cbc, cbl, rtc, rtl)

    wmap = lambda i, tg, tv, tf: (tg[i], 0, 0)
    ys = pl.pallas_call(
        _experts_kernel,
        grid_spec=pltpu.PrefetchScalarGridSpec(
            num_scalar_prefetch=3, grid=(n_tiles,),
            in_specs=[
                pl.BlockSpec((MOE_TM, D_MODEL), lambda i, *_: (i, 0)),
                pl.BlockSpec((MOE_TM, 2 * LANES), lambda i, *_: (i, 0)),
                pl.BlockSpec((EXPERTS_PER_GROUP, D_MODEL, D_EXPERT), wmap),
                pl.BlockSpec((EXPERTS_PER_GROUP, D_MODEL, D_EXPERT), wmap),
                pl.BlockSpec((EXPERTS_PER_GROUP, D_EXPERT, D_MODEL), wmap),
            ],
            out_specs=pl.BlockSpec((MOE_TM, D_MODEL), lambda i, *_: (i, 0)),
            scratch_shapes=[pltpu.VMEM((D_MODEL, EXPERTS_PER_GROUP * D_EXPERT), BF16),
                            pltpu.VMEM((D_MODEL, EXPERTS_PER_GROUP * D_EXPERT), BF16),
                            pltpu.VMEM((EXPERTS_PER_GROUP * D_EXPERT, D_MODEL), BF16)],
        ),
        out_shape=jax.ShapeDtypeStruct((ns, D_MODEL), BF16),
        compiler_params=arb,
        name="moe_experts",
    )(tgroup, tvalid, tfirst, xs, cs, wg, wu, wd)

    yc, yl = pl.pallas_call(
        functools.partial(_combine_kernel, nc, blocks_per_lat_seq),
        grid_spec=pltpu.PrefetchScalarGridSpec(
            num_scalar_prefetch=3, grid=(nb,),
            in_specs=[
                pl.BlockSpec((1, SUB, D_MODEL), cmap), pl.BlockSpec((1, SUB, D_MODEL), lmap),
                pl.BlockSpec((1, SUB, LANES), cmap), pl.BlockSpec((1, SUB, LANES), lmap),
                pl.BlockSpec((ns, D_MODEL), whole, **once),
                pl.BlockSpec(mod.shape, lambda *_: (0, 0, 0)),
                pl.BlockSpec((1, D_MODEL), whole),
            ],
            out_specs=[pl.BlockSpec((1, SUB, D_MODEL), cmap), pl.BlockSpec((1, SUB, D_MODEL), lmap)],
            scratch_shapes=[pltpu.VMEM((SORT_ROWS, D_MODEL), BF16)],
        ),
        out_shape=[jax.ShapeDtypeStruct((nc, SUB, D_MODEL), F32), jax.ShapeDtypeStruct((nl, SUB, D_MODEL), F32)],
        compiler_params=arb,
        name="moe_combine",
    )(start, npiece, off, x1c, x1l, cbc, cbl, ys, mod, gf)
    return yc, yl


def _prep_weights(norm1_g, w_in, b_in, b_gates, w_dw, b_dw, conv_ln_g, conv_ln_b, w_conv_out,
                  mlstm_hn_g, w_mlstm_out, w_o, norm2_g, w_rg, b_rg, w_re, b_re):
    s_a = 2 * D_CONV
    s_q = s_a + D_MLSTM
    s_k = s_q + D_MLSTM
    s_v = s_k + D_MLSTM
    s_o = s_v + D_MLSTM
    s_g = s_o + 4 * N_HEADS
    row = lambda v: v.reshape(1, -1).astype(F32)
    w_t = w_in.T
    keep = [(0, s_q), (s_k, s_o), (s_g, w_in.shape[1])]
    wrow = _transpose_cast(w_t, [r for a, b in keep for r in range(a, b, WPREP_ROWS)])
    w_g = w_t[s_o:s_g].reshape(2, 2, N_HEADS, D_MODEL).transpose(1, 0, 2, 3).reshape(4 * N_HEADS, D_MODEL)
    bg = (b_in[s_o:s_g] + b_gates.reshape(-1)).reshape(2, 2, N_HEADS).transpose(1, 0, 2).reshape(-1, 1)
    n_rt = N_EXPERTS + N_GROUPS
    wrt = jnp.pad(jnp.concatenate([w_re, w_rg], axis=1), ((0, 0), (0, LANES - n_rt)))
    wrt_hi = wrt.astype(BF16)
    wrt2 = jnp.concatenate([wrt_hi, (wrt - wrt_hi.astype(F32)).astype(BF16)], axis=1)
    brtT = jnp.pad(jnp.concatenate([b_re, b_rg]), (0, LANES - n_rt)).reshape(LANES, 1)
    return {
        "g1": row(norm1_g),
        "wrow": wrow, "bag": row(b_in[:s_a]), "bq": row(b_in[s_a:s_q]),
        "wkT": w_t[s_q:s_k], "bk": b_in[s_q:s_k].reshape(-1, 1),
        "bv": row(b_in[s_k:s_v]), "bog": row(b_in[s_v:s_o]),
        "wgifT": w_g, "bgifT": bg, "bgm": row(b_in[s_g:]),
        "wdw": w_dw.astype(F32), "bdw": row(b_dw), "lng": row(conv_ln_g), "lnb": row(conv_ln_b),
        "wco": w_conv_out.astype(BF16), "hng": row(mlstm_hn_g), "wmo": w_mlstm_out.astype(BF16),
        "wo": w_o.astype(BF16), "g2": row(norm2_g), "wrt2": wrt2, "brtT": brtT,
    }


def kernel(x_prompt, x_sample, state_C, state_n, state_m, c, c_ctx, norm1_g, w_ada, b_ada, w_in, b_in, b_gates, w_dw, b_dw, conv_ln_g, conv_ln_b, w_conv_out, mlstm_hn_g, w_mlstm_out, w_o, norm2_g, w_rg, b_rg, w_re, b_re, w_e_gate, w_e_up, w_e_down, norm_final_g):
    B, S, _ = x_prompt.shape
    Bd, Sd, _ = x_sample.shape
    assert w_ada.shape[0] == 1, "single trunk layer"
    assert S == SUB and Sd % SUB == 0

    cin = jnp.concatenate([c_ctx[None, :], c, jnp.zeros((8 - 1 - Bd, D_MODEL), F32)], axis=0)
    mod = _ada(cin, w_ada[0], b_ada[0].reshape(1, -1))

    wts = _prep_weights(norm1_g[0], w_in[0], b_in[0], b_gates[0], w_dw[0], b_dw[0], conv_ln_g[0],
                        conv_ln_b[0], w_conv_out[0], mlstm_hn_g[0], w_mlstm_out[0], w_o[0],
                        norm2_g[0], w_rg[0], b_rg[0], w_re[0], b_re[0])

    x1p, h2p, cbp, rtp, cntp, c_new, n_new, m_new = _mixer(
        x_prompt.reshape(B * S // MIX_TM, MIX_TM, D_MODEL), S, mod, lambda b: 0, wts, P=S, emit_state=True)

    m0 = jnp.broadcast_to(state_m[:, 0].reshape(Bd, N_UNITS, 1), (Bd, N_UNITS, LANES))
    state = (state_C[:, 0].reshape(Bd, N_UNITS, HEAD_DIM, HEAD_DIM), state_n[:, 0].reshape(Bd, N_UNITS, HEAD_DIM), m0)
    x1s, h2s, cbs, rts, cnts = _mixer(x_sample, Sd, mod, lambda b: 1 + b, wts, P=GRID_W, state=state)

    nc, nl = B * S // SUB, Bd * Sd // SUB
    blk = lambda a, n: a.reshape(n, SUB, a.shape[-1])
    cnt = jnp.concatenate([cntp.reshape(nc, 8, LANES)[:, :N_GROUPS, 0],
                           cnts.reshape(nl, 8, LANES)[:, :N_GROUPS, 0]], axis=0)
    yp, ys = _moe(blk(x1p, nc), blk(x1s, nl), blk(h2p, nc), blk(h2s, nl), blk(cbp, nc), blk(cbs, nl),
                  rtp.reshape(nc, 8, SUB), rts.reshape(nl, 8, SUB),
                  cnt.astype(jnp.int32).reshape(-1), mod, Sd // SUB, w_e_gate[0], w_e_up[0], w_e_down[0],
                  norm_final_g.reshape(1, -1))

    return (yp.reshape(B, S, D_MODEL), ys.reshape(Bd, Sd, D_MODEL),
            c_new.reshape(B, 1, 2, N_HEADS, HEAD_DIM, HEAD_DIM),
            n_new.reshape(B, 1, 2, N_HEADS, HEAD_DIM),
            m_new[:, :, 0].reshape(B, 1, 2, N_HEADS))
```

```python
import functools

import jax
import jax.numpy as jnp
from jax import lax
from jax.experimental import pallas as pl
from jax.experimental.pallas import tpu as pltpu

D_MODEL = 1024
D_CONV = 512
CONV_K = 31
D_MLSTM = 512
N_HEADS = 4
HEAD_DIM = D_MLSTM // N_HEADS
N_GROUPS = 4
EXPERTS_PER_GROUP = 4
N_EXPERTS = N_GROUPS * EXPERTS_PER_GROUP
D_EXPERT = 256
N_ADA = 6
EPS = 1e-6
GRID_W = 64

LANES = 128
SUB = 256
CONV_PAD = 16
CONV_RB = 64
N_UNITS = 2 * N_HEADS
ROW_ALIGN = 16
MOE_TM = 512
MIX_TM = 512
MOE_BLK = MIX_TM
SORT_ROWS = MOE_BLK + N_GROUPS * ROW_ALIGN
WPREP_ROWS = 512
ROUTE_GROUP_LANE = N_EXPERTS
ROUTE_RANK_LANE = N_EXPERTS + 1
VMEM_LIMIT = 58 * 1024 * 1024

BF16 = jnp.bfloat16
F32 = jnp.float32
NT_DIMS = (((1,), (1,)), ((), ()))


def _dot(a, b):
    return jnp.dot(a, b, preferred_element_type=F32)


def _dot_nt(a, b, precision=None):
    return lax.dot_general(a, b, NT_DIMS, preferred_element_type=F32, precision=precision)


def _sigmoid(x):
    return 0.5 * jnp.tanh(0.5 * x) + 0.5


def _log_sigmoid(x):
    return jnp.minimum(x, 0.0) - jnp.log1p(jnp.exp(-jnp.abs(x)))


def _split3(x):
    hi = x.astype(BF16).astype(F32)
    r1 = x - hi
    mid = r1.astype(BF16).astype(F32)
    lo = (r1 - mid).astype(BF16).astype(F32)
    return hi, mid, lo


def _ada_kernel(c_ref, w_ref, b_ref, o_ref):
    c = c_ref[...]
    s = (c * _sigmoid(c)).astype(BF16)
    o_ref[0] = _dot(s, w_ref[...].astype(BF16)) + b_ref[...]


def _ada(cin, w_ada, b_ada):
    return pl.pallas_call(
        _ada_kernel,
        grid=(N_ADA,),
        in_specs=[
            pl.BlockSpec((8, D_MODEL), lambda j: (0, 0)),
            pl.BlockSpec((D_MODEL, D_MODEL), lambda j: (0, j)),
            pl.BlockSpec((1, D_MODEL), lambda j: (0, j)),
        ],
        out_specs=pl.BlockSpec((1, 8, D_MODEL), lambda j: (j, 0, 0)),
        out_shape=jax.ShapeDtypeStruct((N_ADA, 8, D_MODEL), F32),
        compiler_params=pltpu.CompilerParams(dimension_semantics=("arbitrary",)),
        name="ada",
    )(cin, w_ada, b_ada)


def _transpose_cast_kernel(starts_ref, wt_ref, o_ref):
    o_ref[...] = wt_ref[...].T.astype(BF16)


def _transpose_cast(w_t, row_starts):
    n, k = len(row_starts), w_t.shape[1]
    return pl.pallas_call(
        _transpose_cast_kernel,
        grid_spec=pltpu.PrefetchScalarGridSpec(
            num_scalar_prefetch=1, grid=(n,),
            in_specs=[pl.BlockSpec((pl.Element(WPREP_ROWS), pl.Element(k)), lambda j, starts: (starts[j] * 8, 0))],
            out_specs=pl.BlockSpec((k, WPREP_ROWS), lambda j, starts: (0, j)),
        ),
        out_shape=jax.ShapeDtypeStruct((k, n * WPREP_ROWS), BF16),
        compiler_params=pltpu.CompilerParams(dimension_semantics=("arbitrary",)),
        name="transpose_cast",
    )(jnp.array([r // 8 for r in row_starts], jnp.int32), w_t)


WROW_OFFSET = {"wq": 2 * D_CONV, "wv": 2 * D_CONV + D_MLSTM, "wog": 2 * D_CONV + 2 * D_MLSTM,
               "wgm": 2 * D_CONV + 3 * D_MLSTM}

_MIXER_WEIGHTS = (
    "g1", "wrow", "bag", "bq", "wkT", "bk", "bv", "bog",
    "wgifT", "bgifT", "bgm", "wdw", "bdw", "lng", "lnb",
    "wco", "hng", "wmo", "wo", "g2", "wrt2", "brtT",
)


def _zero_after(x):
    bits = lax.bitcast_convert_type(x, jnp.uint32)
    bits = lax.shift_right_logical(lax.shift_right_logical(bits, jnp.uint32(16)), jnp.uint32(16))
    return lax.bitcast_convert_type(bits, F32)[0:1, :]


def _conv_block(upad_s, seg, base, cs, wdw_ref, bdw_ref, after=None):
    sub = 8
    first = CONV_PAD - CONV_K // 2
    acc = jnp.broadcast_to(bdw_ref[0:1, cs], (CONV_RB, LANES))
    for r in range(sub):
        z = None
        for a in range((CONV_K + first + sub - 1) // sub):
            j = sub * a + r - first
            if 0 <= j < CONV_K:
                lo = base + sub * a
                tap = wdw_ref[j:j + 1, cs] if after is None else wdw_ref[j:j + 1, cs] + after
                term = tap * upad_s[seg, lo:lo + CONV_RB + sub, cs]
                z = term if z is None else z + term
        acc = acc + z[r:r + CONV_RB, :]
    return acc


def _mixer_kernel(R, T, P, has_state, emit_state, mod_index, *refs):
    L = SUB
    n_mt = R // MIX_TM
    cpm = MIX_TM // L
    n_seq = R // T
    cps = T // L
    nseg = MIX_TM // P
    assert not has_state or n_seq == 1
    it = iter(refs)
    x_ref = next(it)
    mod_ref = next(it)
    if has_state:
        c0_ref = next(it)
        n0_ref = next(it)
        m0_ref = next(it)
    w = {name: next(it) for name in _MIXER_WEIGHTS}
    x1_ref = next(it)
    h2_ref = next(it)
    comb_ref = next(it)
    route_ref = next(it)
    cnt_ref = next(it)
    if emit_state:
        cout_ref = next(it)
        nout_ref = next(it)
        mout_ref = next(it)
    (q_s, kT_s, v_s, so_s, scan_s, ma_s, sgb_s, hm_s, cst_s, upad_s) = [next(it) for _ in range(10)]

    cond_row = mod_index(pl.program_id(0))

    def mod_row(i):
        return mod_ref[i, pl.ds(cond_row, 1), :]

    zpad = jnp.zeros((CONV_PAD, D_CONV), F32)
    for seg in range(nseg):
        upad_s[seg, 0:CONV_PAD, :] = zpad
        upad_s[seg, CONV_PAD + P:CONV_PAD + P + CONV_PAD, :] = zpad

    t_idx = lax.broadcasted_iota(jnp.int32, (L, L), 0)
    s_idx = lax.broadcasted_iota(jnp.int32, (L, L), 1)
    lower = s_idx <= t_idx
    upper = s_idx >= t_idx
    triu_b = upper.astype(F32).astype(BF16)
    lane_u = lax.broadcasted_iota(jnp.int32, (N_UNITS, L), 1)
    is_bwd = lax.broadcasted_iota(jnp.int32, (N_UNITS, L), 0) >= N_HEADS

    def gate_scan(g):
        gi, lf = g[:N_UNITS], _log_sigmoid(g[N_UNITS:])
        pr = _dot(jnp.concatenate(_split3(lf), axis=0).astype(BF16), triu_b)
        pre = pr[0:N_UNITS] + pr[N_UNITS:2 * N_UNITS] + pr[2 * N_UNITS:]
        tot = pre[:, L - 1:L]
        bsum = jnp.where(is_bwd, tot - pre + lf, pre)
        a = gi - bsum
        pm, sm, k = a, a, 1
        while k < L:
            pm = jnp.where(lane_u >= k, jnp.maximum(pm, pltpu.roll(pm, k, axis=1)), pm)
            sm = jnp.where(lane_u < L - k, jnp.maximum(sm, pltpu.roll(sm, L - k, axis=1)), sm)
            k *= 2
        wide = lambda v: jnp.broadcast_to(v, (N_UNITS, L))
        return jnp.concatenate([a, jnp.where(is_bwd, sm, pm), bsum, wide(tot),
                                wide(jnp.max(a, axis=1, keepdims=True))], axis=0)

    def phase1(i, carry):
        r0 = pl.multiple_of(i * MIX_TM, MIX_TM)
        rows = pl.ds(r0, MIX_TM)
        x = x_ref[0, rows, :]
        xn = x * lax.rsqrt(jnp.mean(x * x, axis=-1, keepdims=True) + EPS) * w["g1"][...]
        hb = (xn * (1.0 + mod_row(1)) + mod_row(0)).astype(BF16)

        gates = _dot_nt(w["wgifT"][...].astype(BF16), hb) + w["bgifT"][...]
        for j in range(cpm):
            scan_s[i * cpm + j] = gate_scan(gates[:, j * L:(j + 1) * L])
        ag = _dot(hb, w["wrow"][:, :2 * D_CONV]) + w["bag"][...]
        u = ag[:, :D_CONV] * _sigmoid(ag[:, D_CONV:])
        for seg in range(nseg):
            upad_s[seg, CONV_PAD:CONV_PAD + P, :] = u[seg * P:(seg + 1) * P, :]

        def proj(name, bias, c0, gate, width=2 * LANES):
            w0 = WROW_OFFSET[name] + c0
            b = w[bias][:, c0:c0 + width]
            if gate is not None:
                b = b + jnp.concatenate([gate] * (width // LANES), axis=1)
            return _dot(hb, w["wrow"][:, w0:w0 + width]) + b

        last = lambda z: z[-8:, -LANES:]

        def gm_a(c0, gate):
            z = proj("wgm", "bgm", c0, gate)
            ma_s[rows, c0:c0 + 2 * LANES] = _sigmoid(z)
            return last(z)

        def gm_b(c0, gate):
            z = proj("wgm", "bgm", D_MODEL + c0, gate)
            sgb_s[rows, c0:c0 + 2 * LANES] = _sigmoid(z)
            return last(z)

        def q_part(c0, gate):
            z = proj("wq", "bq", c0, gate)
            q_s[rows, c0:c0 + 2 * LANES] = (z * (HEAD_DIM ** -0.5)).astype(BF16)
            return last(z)

        def v_part(c0, gate):
            z = proj("wv", "bv", c0, gate)
            v_s[rows, c0:c0 + 2 * LANES] = z.astype(BF16)
            return last(z)

        def o_part(c0, gate):
            z = proj("wog", "bog", c0, gate)
            so_s[rows, c0:c0 + 2 * LANES] = _sigmoid(z)
            return last(z)

        def k_part(c0, gate):
            rs = slice(c0, c0 + 2 * LANES)
            b = w["bk"][rs, :] if gate is None else w["bk"][rs, :] + gate[:, 0:1]
            z = _dot_nt(w["wkT"][rs, :].astype(BF16), hb) + b
            kt = z.astype(BF16)
            for j in range(cpm):
                kT_s[i * cpm + j, rs, :] = kt[:, j * L:(j + 1) * L]
            return last(z)

        jobs = ([functools.partial(gm_a, c0) for c0 in range(0, D_MODEL, 2 * LANES)]
                + [functools.partial(gm_b, c0) for c0 in range(0, D_MODEL, 2 * LANES)]
                + [functools.partial(f, c0) for f in (q_part, k_part, v_part, o_part)
                   for c0 in range(0, D_MLSTM, 2 * LANES)])
        n_jobs = len(jobs)
        conv = {}
        after = None
        n_pieces = (D_CONV // LANES) * nseg * (P // CONV_RB)
        for cb in range(D_CONV // LANES):
            cs = slice(cb * LANES, (cb + 1) * LANES)
            for seg in range(nseg):
                for rb in range(P // CONV_RB):
                    blk = _conv_block(upad_s, seg, rb * CONV_RB, cs, w["wdw"], w["bdw"], after)
                    conv[(cb, seg, rb)] = blk
                    if jobs and len(conv) * n_jobs >= (n_jobs - len(jobs) + 1) * n_pieces:
                        after = _zero_after(jobs.pop(0)(_zero_after(blk[-8:, :])))
        for job in jobs:
            job(None)
        cu = jnp.concatenate(
            [jnp.concatenate([conv[(cb, seg, rb)] for seg in range(nseg) for rb in range(P // CONV_RB)], axis=0)
             for cb in range(D_CONV // LANES)], axis=1)
        mu = jnp.mean(cu, axis=-1, keepdims=True)
        cc = cu - mu
        cn = cc * lax.rsqrt(jnp.mean(cc * cc, axis=-1, keepdims=True) + EPS) * w["lng"][...] + w["lnb"][...]
        ca = (cn * _sigmoid(cn)).astype(BF16)
        ma_s[rows, :] = ma_s[rows, :] * _dot(ca, w["wco"][...])
        return carry

    if n_mt == 1:
        phase1(0, 0)
    else:
        lax.fori_loop(0, n_mt, phase1, 0)

    ones_col = (lax.broadcasted_iota(jnp.int32, (L, HEAD_DIM), 1) == 0).astype(F32).astype(BF16)
    pad_rows = jnp.zeros((LANES - 3 * N_UNITS, L), F32)

    def gate_prep(c, m_vec):
        sc = scan_s[c]
        a, run_max, bsum = sc[0:N_UNITS], sc[N_UNITS:2 * N_UNITS], sc[2 * N_UNITS:3 * N_UNITS]
        tot, a_max = sc[3 * N_UNITS:4 * N_UNITS, 0:1], sc[4 * N_UNITS:5 * N_UNITS, 0:1]
        big_m = jnp.maximum(m_vec, run_max)
        m_end = jnp.maximum(m_vec, a_max)
        cols = jnp.concatenate(
            [big_m, jnp.exp(m_vec - big_m), jnp.exp(-bsum - big_m), pad_rows], axis=0).T
        return a, cols, jnp.exp(a - m_end), jnp.exp(m_vec - m_end), tot + m_end

    qk_cache = {}

    def unit(d, hd, c, prep, first_chunk, want_state):
        a, cols, wk, decay, _ = prep
        rows = slice(c * L, (c + 1) * L)
        hs = slice(hd * HEAD_DIM, (hd + 1) * HEAD_DIM)
        idx = d * N_HEADS + hd
        col = lambda k: cols[:, k * N_UNITS + idx:k * N_UNITS + idx + 1]
        qc = q_s[rows, hs]
        kTc = kT_s[c, hs, :]
        vaug = jnp.concatenate([v_s[rows, hs], ones_col], axis=1)
        if cps == 1 and (hd, c) in qk_cache:
            qk = qk_cache[(hd, c)]
        else:
            qk = _dot(qc, kTc)
            qk_cache[(hd, c)] = qk
        w_intra = jnp.where(lower if d == 0 else upper, jnp.exp(a[idx:idx + 1, :] - col(0)), 0.0)
        nd = _dot((qk * w_intra).astype(BF16), vaug)
        if has_state or not first_chunk:
            nd = nd + col(1) * _dot(qc, cst_s[idx].astype(BF16))
        den = nd[:, HEAD_DIM:HEAD_DIM + 1]
        h = nd[:, :HEAD_DIM] * (1.0 / jnp.maximum(jnp.abs(den), col(2)))
        if d == 0:
            hm_s[rows, hs] = h
        else:
            hm_s[rows, hs] = hm_s[rows, hs] + h
        if want_state:
            kw = (kTc.astype(F32) * wk[idx:idx + 1, :]).astype(BF16)
            upd = _dot(kw, vaug)
            if has_state or not first_chunk:
                upd = upd + decay[idx:idx + 1, :] * cst_s[idx]
            cst_s[idx] = upd

    dir_rows = lax.broadcasted_iota(jnp.int32, (N_UNITS, 1), 0) >= N_HEADS
    for seq in range(n_seq):
        if has_state:
            n_cols = jnp.concatenate([n0_ref[0], jnp.zeros((LANES - N_UNITS, HEAD_DIM), F32)], axis=0).T
            first_lane = lax.broadcasted_iota(jnp.int32, (HEAD_DIM, HEAD_DIM), 1) == 0
            for idx in range(N_UNITS):
                cst_s[idx, :, :HEAD_DIM] = c0_ref[0, idx]
                cst_s[idx, :, HEAD_DIM:] = jnp.where(first_lane, n_cols[:, idx:idx + 1], 0.0)
            m_vec = m0_ref[0, :, 0:1]
        else:
            m_vec = jnp.zeros((N_UNITS, 1), F32)
        prep = None
        for d in range(2):
            order = list(range(cps)) if d == 0 else list(range(cps - 1, -1, -1))
            for pos, c in enumerate(order):
                if cps > 1 or prep is None:
                    prep = gate_prep(seq * cps + c, m_vec)
                for hd in range(N_HEADS):
                    unit(d, hd, seq * cps + c, prep, pos == 0, emit_state or pos < cps - 1)
                m_vec = jnp.where(dir_rows == (d == 1), prep[4], m_vec)
        if emit_state:
            for idx in range(N_UNITS):
                caug = cst_s[idx]
                cout_ref[0, seq * N_UNITS + idx] = caug[:, :HEAD_DIM]
                nout_ref[0, seq * N_UNITS + idx:seq * N_UNITS + idx + 1, :] = caug[:, HEAD_DIM:].T[0:1, :]
            mout_ref[0, seq * N_UNITS:(seq + 1) * N_UNITS, :] = jnp.broadcast_to(m_vec, (N_UNITS, LANES))

    e_iota = lax.broadcasted_iota(jnp.int32, (LANES, MIX_TM), 0)
    g_of_e = lax.shift_right_logical(e_iota, 2)
    j_of_e = lax.bitwise_and(e_iota, EXPERTS_PER_GROUP - 1)
    r8 = lax.broadcasted_iota(jnp.int32, (8, MIX_TM), 0)
    before_b = (lax.broadcasted_iota(jnp.int32, (MOE_BLK, MOE_BLK), 0)
                < lax.broadcasted_iota(jnp.int32, (MOE_BLK, MOE_BLK), 1)).astype(F32).astype(BF16)

    def phase3(i, carry):
        r0 = pl.multiple_of(i * MIX_TM, MIX_TM)
        rows = pl.ds(r0, MIX_TM)
        hm = hm_s[rows, :]
        heads = []
        for hd in range(N_HEADS):
            hh = hm[:, hd * HEAD_DIM:(hd + 1) * HEAD_DIM]
            heads.append(hh * lax.rsqrt(jnp.mean(hh * hh, axis=-1, keepdims=True) + EPS))
        hn = jnp.concatenate(heads, axis=1) * w["hng"][...]
        hb2 = (so_s[rows, :] * hn).astype(BF16)
        br_b = _dot(hb2, w["wmo"][...])
        mixed = (ma_s[rows, :] + sgb_s[rows, :] * br_b).astype(BF16)
        x1 = x_ref[0, rows, :] + mod_row(2) * _dot(mixed, w["wo"][...])
        x1_ref[0, rows, :] = x1
        xn = x1 * lax.rsqrt(jnp.mean(x1 * x1, axis=-1, keepdims=True) + EPS) * w["g2"][...]
        h2 = xn * (1.0 + mod_row(4)) + mod_row(3)
        h2_ref[0, rows, :] = h2.astype(BF16)

        h2_hi = h2.astype(BF16)
        h2_lo = (h2 - h2_hi.astype(F32)).astype(BF16)
        lg = _dot(h2_hi, w["wrt2"][...])
        lg = lg[:, :LANES] + lg[:, LANES:] + _dot(h2_lo, w["wrt2"][:, :LANES])
        lt = lg.T + w["brtT"][...]
        gl = [lt[N_EXPERTS + g:N_EXPERTS + g + 1, :] for g in range(N_GROUPS)]
        best, gsel = gl[0], jnp.zeros((1, MIX_TM), jnp.int32)
        for g in range(1, N_GROUPS):
            better = gl[g] > best
            gsel = jnp.where(better, g, gsel)
            best = jnp.where(better, gl[g], best)
        gp_sel = 1.0 / sum(jnp.exp(v - best) for v in gl)
        el = []
        for j in range(EXPERTS_PER_GROUP):
            v = lt[j:j + 1, :]
            for g in range(1, N_GROUPS):
                r = g * EXPERTS_PER_GROUP + j
                v = jnp.where(gsel == g, lt[r:r + 1, :], v)
            el.append(v)
        l1, e1 = el[0], jnp.zeros((1, MIX_TM), jnp.int32)
        for j in range(1, EXPERTS_PER_GROUP):
            better = el[j] > l1
            e1 = jnp.where(better, j, e1)
            l1 = jnp.where(better, el[j], l1)
        l2 = jnp.full((1, MIX_TM), -jnp.inf, F32)
        e2 = jnp.zeros((1, MIX_TM), jnp.int32)
        for j in range(EXPERTS_PER_GROUP):
            better = jnp.logical_and(e1 != j, el[j] > l2)
            e2 = jnp.where(better, j, e2)
            l2 = jnp.where(better, el[j], l2)
        r2 = jnp.exp(l2 - l1)
        wt1 = gp_sel / (1.0 + r2)
        wt2 = gp_sel * r2 / (1.0 + r2)
        in_group = g_of_e == gsel
        comb_t = (jnp.where(jnp.logical_and(in_group, j_of_e == e1), wt1, 0.0)
                  + jnp.where(jnp.logical_and(in_group, j_of_e == e2), wt2, 0.0))

        onehot = (r8 == gsel).astype(F32)
        gsel_f = gsel.astype(F32)
        rank = jnp.sum(onehot * _dot(onehot.astype(BF16), before_b), axis=0, keepdims=True)
        r8rows = pl.ds(pl.multiple_of(i * 8, 8), 8)
        route_ref[0, r8rows, :] = jnp.where(r8 == 0, gsel_f, jnp.where(r8 == 1, rank, 0.0))
        cnt_ref[0, r8rows, :] = jnp.broadcast_to(jnp.sum(onehot, axis=1, keepdims=True), (8, LANES))
        comb_t = jnp.where(e_iota == ROUTE_GROUP_LANE, gsel_f,
                           jnp.where(e_iota == ROUTE_RANK_LANE, rank, comb_t))
        comb_ref[0, rows, :] = comb_t.T
        return carry

    if n_mt == 1:
        phase3(0, 0)
    else:
        lax.fori_loop(0, n_mt, phase3, 0)


def _const_spec(a):
    nd = a.ndim
    return pl.BlockSpec(a.shape, lambda b, _nd=nd: (0,) * _nd, pipeline_mode=pl.Buffered(1))


def _mixer(x, T, mod, mod_index, weights, P, state=None, emit_state=False):
    B, R, _ = x.shape
    n_chunks = R // SUB
    n_blk = R // MOE_BLK
    n_seq = R // T
    has_state = state is not None
    seq_mode = {} if R <= MIX_TM else {"pipeline_mode": pl.Buffered(1)}
    in_specs = [
        pl.BlockSpec((1, R, D_MODEL), lambda b: (b, 0, 0), **seq_mode),
        pl.BlockSpec(mod.shape, lambda b: (0, 0, 0)),
    ]
    args = [x, mod]
    if has_state:
        c0, n0, m0 = state
        in_specs += [
            pl.BlockSpec((1, N_UNITS, HEAD_DIM, HEAD_DIM), lambda b: (b, 0, 0, 0)),
            pl.BlockSpec((1, N_UNITS, HEAD_DIM), lambda b: (b, 0, 0)),
            pl.BlockSpec((1, N_UNITS, LANES), lambda b: (b, 0, 0)),
        ]
        args += [c0, n0, m0]
    for name in _MIXER_WEIGHTS:
        in_specs.append(_const_spec(weights[name]))
        args.append(weights[name])
    out_shape = [
        jax.ShapeDtypeStruct((B, R, D_MODEL), F32),
        jax.ShapeDtypeStruct((B, R, D_MODEL), BF16),
        jax.ShapeDtypeStruct((B, R, LANES), F32),
        jax.ShapeDtypeStruct((B, n_blk * 8, MOE_BLK), F32),
        jax.ShapeDtypeStruct((B, n_blk * 8, LANES), F32),
    ]
    out_specs = [
        pl.BlockSpec((1, R, D_MODEL), lambda b: (b, 0, 0), **seq_mode),
        pl.BlockSpec((1, R, D_MODEL), lambda b: (b, 0, 0), **seq_mode),
        pl.BlockSpec((1, R, LANES), lambda b: (b, 0, 0)),
        pl.BlockSpec((1, n_blk * 8, MOE_BLK), lambda b: (b, 0, 0)),
        pl.BlockSpec((1, n_blk * 8, LANES), lambda b: (b, 0, 0)),
    ]
    if emit_state:
        out_shape += [
            jax.ShapeDtypeStruct((B, n_seq * N_UNITS, HEAD_DIM, HEAD_DIM), F32),
            jax.ShapeDtypeStruct((B, n_seq * N_UNITS, HEAD_DIM), F32),
            jax.ShapeDtypeStruct((B, n_seq * N_UNITS, LANES), F32),
        ]
        out_specs += [
            pl.BlockSpec((1, n_seq * N_UNITS, HEAD_DIM, HEAD_DIM), lambda b: (b, 0, 0, 0)),
            pl.BlockSpec((1, n_seq * N_UNITS, HEAD_DIM), lambda b: (b, 0, 0)),
            pl.BlockSpec((1, n_seq * N_UNITS, LANES), lambda b: (b, 0, 0)),
        ]
    scratch = [
        pltpu.VMEM((R, D_MLSTM), BF16),
        pltpu.VMEM((n_chunks, D_MLSTM, SUB), BF16),
        pltpu.VMEM((R, D_MLSTM), BF16),
        pltpu.VMEM((R, D_MLSTM), F32),
        pltpu.VMEM((n_chunks, 5 * N_UNITS, SUB), F32),
        pltpu.VMEM((R, D_MODEL), F32),
        pltpu.VMEM((R, D_MODEL), F32),
        pltpu.VMEM((R, D_MLSTM), F32),
        pltpu.VMEM((N_UNITS, HEAD_DIM, 2 * HEAD_DIM), F32),
        pltpu.VMEM((MIX_TM // P, P + 2 * CONV_PAD, D_CONV), F32),
    ]
    return pl.pallas_call(
        functools.partial(_mixer_kernel, R, T, P, has_state, emit_state, mod_index),
        grid=(B,),
        in_specs=in_specs,
        out_specs=out_specs,
        out_shape=out_shape,
        scratch_shapes=scratch,
        compiler_params=pltpu.CompilerParams(
            dimension_semantics=("arbitrary",), vmem_limit_bytes=VMEM_LIMIT),
        name="mixer_T%d" % T,
    )(*args)


def _dest_in_block(group, rank, starts):
    dest = rank
    for g in range(N_GROUPS):
        dest = dest + jnp.where(group == float(g), starts[g], 0.0)
    return dest


def _copy_segments(src_refs, dst_refs, src_starts, dst_starts, n_pieces):
    for g in range(N_GROUPS):
        def body(k, carry, g=g):
            s = pl.multiple_of(src_starts[g] + k * ROW_ALIGN, ROW_ALIGN)
            d = pl.multiple_of(dst_starts[g] + k * ROW_ALIGN, ROW_ALIGN)
            for src, dst in zip(src_refs, dst_refs):
                dst[pl.ds(d, ROW_ALIGN), :] = src[pl.ds(s, ROW_ALIGN), :]
            return carry
        lax.fori_loop(0, n_pieces[g], body, 0)


def _plan_segments(n_blocks, n_tiles, cnt_ref, start_ref, npiece_ref, off_ref, tgroup_ref, tvalid_ref, tfirst_ref):
    align_shift = ROW_ALIGN.bit_length() - 1
    tile_shift = MOE_TM.bit_length() - 1

    def block_starts(blk, carry):
        row = jnp.int32(0)
        for g in range(N_GROUPS):
            n = lax.shift_right_logical(cnt_ref[blk * N_GROUPS + g] + (ROW_ALIGN - 1), align_shift)
            npiece_ref[blk * N_GROUPS + g] = n
            start_ref[blk * N_GROUPS + g] = row
            row = row + n * ROW_ALIGN
        return carry

    lax.fori_loop(0, n_blocks, block_starts, 0)

    base_row = jnp.int32(0)
    base_tile = jnp.int32(0)
    last_group = jnp.int32(0)
    for g in range(N_GROUPS):
        def seg_offsets(blk, row, g=g, base_row=base_row):
            off_ref[blk * N_GROUPS + g] = base_row + row
            return row + npiece_ref[blk * N_GROUPS + g] * ROW_ALIGN

        rows = lax.fori_loop(0, n_blocks, seg_offsets, jnp.int32(0))
        tiles = lax.shift_right_logical(rows + (MOE_TM - 1), tile_shift)

        def mark_tiles(t, carry, g=g, base_tile=base_tile):
            tgroup_ref[base_tile + t] = g
            tvalid_ref[base_tile + t] = 1
            tfirst_ref[base_tile + t] = (t == 0).astype(jnp.int32)
            return carry

        lax.fori_loop(0, tiles, mark_tiles, 0)
        last_group = jnp.where(tiles > 0, g, last_group)
        base_row = base_row + tiles * MOE_TM
        base_tile = base_tile + tiles

    def mark_unused(t, carry):
        tgroup_ref[t] = last_group
        tvalid_ref[t] = 0
        tfirst_ref[t] = 0
        return carry

    lax.fori_loop(base_tile, n_tiles, mark_unused, 0)


def _dispatch_kernel(n_ctx_blocks, n_blocks, n_tiles, cnt_ref,
                     h2c_ref, h2l_ref, cbc_ref, cbl_ref, rtc_ref, rtl_ref,
                     xs_ref, cs_ref, start_ref, npiece_ref, off_ref, tgroup_ref, tvalid_ref, tfirst_ref,
                     sx_s, sc_s):
    b = pl.program_id(0)
    is_ctx = b < n_ctx_blocks

    @pl.when(b == 0)
    def _():
        _plan_segments(n_blocks, n_tiles, cnt_ref, start_ref, npiece_ref, off_ref,
                       tgroup_ref, tvalid_ref, tfirst_ref)
        xs_ref[...] = jnp.zeros_like(xs_ref)
        cs_ref[...] = jnp.zeros_like(cs_ref)

    h2 = jnp.where(is_ctx, h2c_ref[0], h2l_ref[0])
    cb = jnp.where(is_ctx, cbc_ref[0], cbl_ref[0])
    rt = jnp.where(is_ctx, rtc_ref[0], rtl_ref[0])
    starts = [start_ref[b * N_GROUPS + g] for g in range(N_GROUPS)]
    dest = _dest_in_block(rt[0:1, :], rt[1:2, :], [s.astype(F32) for s in starts])
    row = lax.broadcasted_iota(jnp.int32, (SORT_ROWS, MOE_BLK), 0).astype(F32)
    perm = (row == dest).astype(F32).astype(BF16)
    cb_hi = cb.astype(BF16)
    cb_lo = (cb - cb_hi.astype(F32)).astype(BF16)
    sx_s[...] = _dot(perm, h2).astype(BF16)
    sc_s[...] = _dot(perm, jnp.concatenate([cb_hi, cb_lo], axis=1)).astype(BF16)
    _copy_segments((sx_s, sc_s), (xs_ref, cs_ref), starts,
                   [off_ref[b * N_GROUPS + g] for g in range(N_GROUPS)],
                   [npiece_ref[b * N_GROUPS + g] for g in range(N_GROUPS)])


def _experts_kernel(tgroup_ref, tvalid_ref, tfirst_ref, xs_ref, cs_ref, wg_ref, wu_ref, wd_ref, ys_ref,
                    wg_s, wu_s, wd_s):
    i = pl.program_id(0)

    @pl.when(tfirst_ref[i] == 1)
    def _():
        for j in range(EXPERTS_PER_GROUP):
            cols = slice(j * D_EXPERT, (j + 1) * D_EXPERT)
            wg_s[:, cols] = wg_ref[j].astype(BF16)
            wu_s[:, cols] = wu_ref[j].astype(BF16)
            wd_s[cols, :] = wd_ref[j].astype(BF16)

    @pl.when(tvalid_ref[i] == 1)
    def _():
        x = xs_ref[...]
        g = _dot(x, wg_s[...])
        u = _dot(x, wu_s[...])
        comb = cs_ref[:, :LANES].astype(F32) + cs_ref[:, LANES:].astype(F32)
        lane = lax.broadcasted_iota(jnp.int32, comb.shape, 1)
        first = tgroup_ref[i] * EXPERTS_PER_GROUP
        parts = []
        for j in range(EXPERTS_PER_GROUP):
            cols = slice(j * D_EXPERT, (j + 1) * D_EXPERT)
            cw = jnp.sum(jnp.where(lane == first + j, comb, 0.0), axis=1, keepdims=True)
            gj = g[:, cols]
            parts.append((gj * _sigmoid(gj) * u[:, cols] * cw).astype(BF16))
        ys_ref[...] = _dot(jnp.concatenate(parts, axis=1), wd_s[...]).astype(BF16)

    @pl.when(tvalid_ref[i] == 0)
    def _():
        ys_ref[...] = jnp.zeros_like(ys_ref)


def _combine_kernel(n_ctx_blocks, blocks_per_lat_seq, start_ref, npiece_ref, off_ref,
                    x1c_ref, x1l_ref, cbc_ref, cbl_ref, ys_ref, mod_ref, gf_ref, yc_ref, yl_ref, loc_s):
    b = pl.program_id(0)
    is_ctx = b < n_ctx_blocks
    starts = [start_ref[b * N_GROUPS + g] for g in range(N_GROUPS)]
    loc_s[...] = jnp.zeros_like(loc_s)
    _copy_segments((ys_ref,), (loc_s,), [off_ref[b * N_GROUPS + g] for g in range(N_GROUPS)], starts,
                   [npiece_ref[b * N_GROUPS + g] for g in range(N_GROUPS)])
    cb = jnp.where(is_ctx, cbc_ref[0], cbl_ref[0])
    dest = _dest_in_block(cb[:, ROUTE_GROUP_LANE:ROUTE_GROUP_LANE + 1],
                          cb[:, ROUTE_RANK_LANE:ROUTE_RANK_LANE + 1],
                          [s.astype(F32) for s in starts])
    col = lax.broadcasted_iota(jnp.int32, (MOE_BLK, SORT_ROWS), 1).astype(F32)
    unperm = (col == dest).astype(F32).astype(BF16)
    moe = _dot(unperm, loc_s[...])
    x1 = jnp.where(is_ctx, x1c_ref[0], x1l_ref[0])
    mrow = jnp.where(is_ctx, 0, 1 + jnp.maximum(b - n_ctx_blocks, 0) // blocks_per_lat_seq)
    x2 = x1 + mod_ref[N_ADA - 1, pl.ds(mrow, 1), :] * moe
    y = x2 * lax.rsqrt(jnp.mean(x2 * x2, axis=-1, keepdims=True) + EPS) * gf_ref[...]

    @pl.when(is_ctx)
    def _():
        yc_ref[0] = y

    @pl.when(jnp.logical_not(is_ctx))
    def _():
        yl_ref[0] = y


def _moe(x1c, x1l, h2c, h2l, cbc, cbl, rtc, rtl, cnt, mod, blocks_per_lat_seq, wg, wu, wd, gf):
    nc, nl = x1c.shape[0], x1l.shape[0]
    nb = nc + nl
    n_rows_max = nb * MOE_BLK + nb * N_GROUPS * (ROW_ALIGN - 1) + N_GROUPS * (MOE_TM - ROW_ALIGN)
    n_tiles = -(-n_rows_max // MOE_TM)
    ns = n_tiles * MOE_TM

    cmap = lambda b, *_: (jnp.minimum(b, nc - 1), 0, 0)
    lmap = lambda b, *_: (jnp.maximum(b - nc, 0), 0, 0)
    whole = lambda *_: (0, 0)
    once = {"pipeline_mode": pl.Buffered(1)}
    arb = pltpu.CompilerParams(dimension_semantics=("arbitrary",), vmem_limit_bytes=VMEM_LIMIT)
    smem = pl.BlockSpec(memory_space=pltpu.SMEM)
    seg_i32 = jax.ShapeDtypeStruct((nb * N_GROUPS,), jnp.int32)
    tile_i32 = jax.ShapeDtypeStruct((n_tiles,), jnp.int32)

    xs, cs, start, npiece, off, tgroup, tvalid, tfirst = pl.pallas_call(
        functools.partial(_dispatch_kernel, nc, nb, n_tiles),
        grid_spec=pltpu.PrefetchScalarGridSpec(
            num_scalar_prefetch=1, grid=(nb,),
            in_specs=[
                pl.BlockSpec((1, MOE_BLK, D_MODEL), cmap), pl.BlockSpec((1, MOE_BLK, D_MODEL), lmap),
                pl.BlockSpec((1, MOE_BLK, LANES), cmap), pl.BlockSpec((1, MOE_BLK, LANES), lmap),
                pl.BlockSpec((1, 8, MOE_BLK), cmap), pl.BlockSpec((1, 8, MOE_BLK), lmap),
            ],
            out_specs=[pl.BlockSpec((ns, D_MODEL), whole, **once), pl.BlockSpec((ns, 2 * LANES), whole, **once),
                       smem, smem, smem, smem, smem, smem],
            scratch_shapes=[pltpu.VMEM((SORT_ROWS, D_MODEL), BF16), pltpu.VMEM((SORT_ROWS, 2 * LANES), BF16)],
        ),
        out_shape=[jax.ShapeDtypeStruct((ns, D_MODEL), BF16), jax.ShapeDtypeStruct((ns, 2 * LANES), BF16),
                   seg_i32, seg_i32, seg_i32, tile_i32, tile_i32, tile_i32],
        compiler_params=arb,
        name="moe_dispatch",
    )(cnt, h2c, h2l, cbc, cbl, rtc, rtl)

    wmap = lambda i, tg, tv, tf: (tg[i], 0, 0)
    ys = pl.pallas_call(
        _experts_kernel,
        grid_spec=pltpu.PrefetchScalarGridSpec(
            num_scalar_prefetch=3, grid=(n_tiles,),
            in_specs=[
                pl.BlockSpec((MOE_TM, D_MODEL), lambda i, *_: (i, 0)),
                pl.BlockSpec((MOE_TM, 2 * LANES), lambda i, *_: (i, 0)),
                pl.BlockSpec((EXPERTS_PER_GROUP, D_MODEL, D_EXPERT), wmap),
                pl.BlockSpec((EXPERTS_PER_GROUP, D_MODEL, D_EXPERT), wmap),
                pl.BlockSpec((EXPERTS_PER_GROUP, D_EXPERT, D_MODEL), wmap),
            ],
            out_specs=pl.BlockSpec((MOE_TM, D_MODEL), lambda i, *_: (i, 0)),
            scratch_shapes=[pltpu.VMEM((D_MODEL, EXPERTS_PER_GROUP * D_EXPERT), BF16),
                            pltpu.VMEM((D_MODEL, EXPERTS_PER_GROUP * D_EXPERT), BF16),
                            pltpu.VMEM((EXPERTS_PER_GROUP * D_EXPERT, D_MODEL), BF16)],
        ),
        out_shape=jax.ShapeDtypeStruct((ns, D_MODEL), BF16),
        compiler_params=arb,
        name="moe_experts",
    )(tgroup, tvalid, tfirst, xs, cs, wg, wu, wd)

    yc, yl = pl.pallas_call(
        functools.partial(_combine_kernel, nc, blocks_per_lat_seq),
        grid_spec=pltpu.PrefetchScalarGridSpec(
            num_scalar_prefetch=3, grid=(nb,),
            in_specs=[
                pl.BlockSpec((1, MOE_BLK, D_MODEL), cmap), pl.BlockSpec((1, MOE_BLK, D_MODEL), lmap),
                pl.BlockSpec((1, MOE_BLK, LANES), cmap), pl.BlockSpec((1, MOE_BLK, LANES), lmap),
                pl.BlockSpec((ns, D_MODEL), whole, **once),
                pl.BlockSpec(mod.shape, lambda *_: (0, 0, 0)),
                pl.BlockSpec((1, D_MODEL), whole),
            ],
            out_specs=[pl.BlockSpec((1, MOE_BLK, D_MODEL), cmap), pl.BlockSpec((1, MOE_BLK, D_MODEL), lmap)],
            scratch_shapes=[pltpu.VMEM((SORT_ROWS, D_MODEL), BF16)],
        ),
        out_shape=[jax.ShapeDtypeStruct((nc, MOE_BLK, D_MODEL), F32),
                   jax.ShapeDtypeStruct((nl, MOE_BLK, D_MODEL), F32)],
        compiler_params=arb,
        name="moe_combine",
    )(start, npiece, off, x1c, x1l, cbc, cbl, ys, mod, gf)
    return yc, yl


def _prep_weights(norm1_g, w_in, b_in, b_gates, w_dw, b_dw, conv_ln_g, conv_ln_b, w_conv_out,
                  mlstm_hn_g, w_mlstm_out, w_o, norm2_g, w_rg, b_rg, w_re, b_re):
    s_a = 2 * D_CONV
    s_q = s_a + D_MLSTM
    s_k = s_q + D_MLSTM
    s_v = s_k + D_MLSTM
    s_o = s_v + D_MLSTM
    s_g = s_o + 4 * N_HEADS
    row = lambda v: v.reshape(1, -1).astype(F32)
    w_t = w_in.T
    keep = [(0, s_q), (s_k, s_o), (s_g, w_in.shape[1])]
    wrow = _transpose_cast(w_t, [r for a, b in keep for r in range(a, b, WPREP_ROWS)])
    w_g = w_t[s_o:s_g].reshape(2, 2, N_HEADS, D_MODEL).transpose(1, 0, 2, 3).reshape(4 * N_HEADS, D_MODEL)
    bg = (b_in[s_o:s_g] + b_gates.reshape(-1)).reshape(2, 2, N_HEADS).transpose(1, 0, 2).reshape(-1, 1)
    n_rt = N_EXPERTS + N_GROUPS
    wrt = jnp.pad(jnp.concatenate([w_re, w_rg], axis=1), ((0, 0), (0, LANES - n_rt)))
    wrt_hi = wrt.astype(BF16)
    wrt2 = jnp.concatenate([wrt_hi, (wrt - wrt_hi.astype(F32)).astype(BF16)], axis=1)
    brtT = jnp.pad(jnp.concatenate([b_re, b_rg]), (0, LANES - n_rt)).reshape(LANES, 1)
    return {
        "g1": row(norm1_g),
        "wrow": wrow, "bag": row(b_in[:s_a]), "bq": row(b_in[s_a:s_q]),
        "wkT": w_t[s_q:s_k], "bk": b_in[s_q:s_k].reshape(-1, 1),
        "bv": row(b_in[s_k:s_v]), "bog": row(b_in[s_v:s_o]),
        "wgifT": w_g, "bgifT": bg, "bgm": row(b_in[s_g:]),
        "wdw": w_dw.astype(F32), "bdw": row(b_dw), "lng": row(conv_ln_g), "lnb": row(conv_ln_b),
        "wco": w_conv_out.astype(BF16), "hng": row(mlstm_hn_g), "wmo": w_mlstm_out.astype(BF16),
        "wo": w_o.astype(BF16), "g2": row(norm2_g), "wrt2": wrt2, "brtT": brtT,
    }


def kernel(x_prompt, x_sample, state_C, state_n, state_m, c, c_ctx, norm1_g, w_ada, b_ada, w_in, b_in, b_gates, w_dw, b_dw, conv_ln_g, conv_ln_b, w_conv_out, mlstm_hn_g, w_mlstm_out, w_o, norm2_g, w_rg, b_rg, w_re, b_re, w_e_gate, w_e_up, w_e_down, norm_final_g):
    B, S, _ = x_prompt.shape
    Bd, Sd, _ = x_sample.shape
    assert w_ada.shape[0] == 1, "single trunk layer"
    assert S == SUB and Sd % SUB == 0

    cin = jnp.concatenate([c_ctx[None, :], c, jnp.zeros((8 - 1 - Bd, D_MODEL), F32)], axis=0)
    mod = _ada(cin, w_ada[0], b_ada[0].reshape(1, -1))

    wts = _prep_weights(norm1_g[0], w_in[0], b_in[0], b_gates[0], w_dw[0], b_dw[0], conv_ln_g[0],
                        conv_ln_b[0], w_conv_out[0], mlstm_hn_g[0], w_mlstm_out[0], w_o[0],
                        norm2_g[0], w_rg[0], b_rg[0], w_re[0], b_re[0])

    x1p, h2p, cbp, rtp, cntp, c_new, n_new, m_new = _mixer(
        x_prompt.reshape(B * S // MIX_TM, MIX_TM, D_MODEL), S, mod, lambda b: 0, wts, P=S, emit_state=True)

    m0 = jnp.broadcast_to(state_m[:, 0].reshape(Bd, N_UNITS, 1), (Bd, N_UNITS, LANES))
    state = (state_C[:, 0].reshape(Bd, N_UNITS, HEAD_DIM, HEAD_DIM), state_n[:, 0].reshape(Bd, N_UNITS, HEAD_DIM), m0)
    x1s, h2s, cbs, rts, cnts = _mixer(x_sample, Sd, mod, lambda b: 1 + b, wts, P=GRID_W, state=state)

    nc, nl = B * S // MOE_BLK, Bd * Sd // MOE_BLK
    blk = lambda a, n: a.reshape(n, MOE_BLK, a.shape[-1])
    cnt = jnp.concatenate([cntp.reshape(nc, 8, LANES)[:, :N_GROUPS, 0],
                           cnts.reshape(nl, 8, LANES)[:, :N_GROUPS, 0]], axis=0)
    yp, ys = _moe(blk(x1p, nc), blk(x1s, nl), blk(h2p, nc), blk(h2s, nl), blk(cbp, nc), blk(cbs, nl),
                  rtp.reshape(nc, 8, MOE_BLK), rts.reshape(nl, 8, MOE_BLK),
                  cnt.astype(jnp.int32).reshape(-1), mod, Sd // MOE_BLK, w_e_gate[0], w_e_up[0], w_e_down[0],
                  norm_final_g.reshape(1, -1))

    return (yp.reshape(B, S, D_MODEL), ys.reshape(Bd, Sd, D_MODEL),
            c_new.reshape(B, 1, 2, N_HEADS, HEAD_DIM, HEAD_DIM),
            n_new.reshape(B, 1, 2, N_HEADS, HEAD_DIM),
            m_new[:, :, 0].reshape(B, 1, 2, N_HEADS))
```

```python
import functools
from typing import NamedTuple

import jax
import jax.numpy as jnp
from jax import lax
from jax.experimental import pallas as pl
from jax.experimental.pallas import tpu as pltpu

D_MODEL = 1024
D_CONV = 512
CONV_K = 31
D_MLSTM = 512
N_HEADS = 4
HEAD_DIM = D_MLSTM // N_HEADS
N_GROUPS = 4
EXPERTS_PER_GROUP = 4
N_EXPERTS = N_GROUPS * EXPERTS_PER_GROUP
D_EXPERT = 256
N_ADA = 6
EPS = 1e-6
GRID_W = 64

LANES = 128
SUB = 256
CONV_PAD = 16
CONV_RB = 64
N_UNITS = 2 * N_HEADS
ROW_ALIGN = 16
MOE_TM = 512
MIX_TM = 512
MOE_BLK = MIX_TM
SORT_ROWS = MOE_BLK + N_GROUPS * ROW_ALIGN
WPREP_ROWS = 512
ROUTE_GROUP_LANE = N_EXPERTS
ROUTE_RANK_LANE = N_EXPERTS + 1
VMEM_LIMIT = 58 * 1024 * 1024

BF16 = jnp.bfloat16
F32 = jnp.float32
NT_DIMS = (((1,), (1,)), ((), ()))


def _dot(a, b):
    return jnp.dot(a, b, preferred_element_type=F32)


def _dot_nt(a, b, precision=None):
    return lax.dot_general(a, b, NT_DIMS, preferred_element_type=F32, precision=precision)


def _sigmoid(x):
    return 0.5 * jnp.tanh(0.5 * x) + 0.5


def _log_sigmoid(x):
    return jnp.minimum(x, 0.0) - jnp.log1p(jnp.exp(-jnp.abs(x)))


def _split3(x):
    hi = x.astype(BF16).astype(F32)
    r1 = x - hi
    mid = r1.astype(BF16).astype(F32)
    lo = (r1 - mid).astype(BF16).astype(F32)
    return hi, mid, lo


def _ada_kernel(n_rows, c_ref, w_ref, b_ref, o_ref, sb_s):
    @pl.when(pl.program_id(0) == 0)
    def _():
        c = c_ref[...]
        s = jnp.concatenate([c * _sigmoid(c), jnp.zeros((LANES - 8, D_MODEL), F32)], axis=0).T
        for r in range(n_rows):
            sb_s[r] = jnp.broadcast_to(s[:, r:r + 1], (D_MODEL, LANES))

    rows = []
    for r in range(n_rows):
        cols = []
        for c0 in range(0, D_MODEL, LANES):
            prod = w_ref[:, c0:c0 + LANES] * sb_s[r]
            acc = prod[0:8]
            for k0 in range(8, D_MODEL, 8):
                acc = acc + prod[k0:k0 + 8]
            cols.append(jnp.sum(acc, axis=0, keepdims=True))
        rows.append(jnp.concatenate(cols, axis=1) + b_ref[...])
    o_ref[0] = jnp.concatenate(rows + [jnp.zeros((8 - n_rows, D_MODEL), F32)], axis=0)


def _ada(cin, n_rows, w_ada, b_ada):
    return pl.pallas_call(
        functools.partial(_ada_kernel, n_rows),
        grid=(N_ADA,),
        in_specs=[
            pl.BlockSpec((8, D_MODEL), lambda j: (0, 0)),
            pl.BlockSpec((D_MODEL, D_MODEL), lambda j: (0, j)),
            pl.BlockSpec((1, D_MODEL), lambda j: (0, j)),
        ],
        out_specs=pl.BlockSpec((1, 8, D_MODEL), lambda j: (j, 0, 0)),
        out_shape=jax.ShapeDtypeStruct((N_ADA, 8, D_MODEL), F32),
        scratch_shapes=[pltpu.VMEM((n_rows, D_MODEL, LANES), F32)],
        compiler_params=pltpu.CompilerParams(dimension_semantics=("arbitrary",)),
        name="ada",
    )(cin, w_ada, b_ada)


def _transpose_cast_kernel(starts_ref, wt_ref, o_ref):
    o_ref[...] = wt_ref[...].T.astype(BF16)


def _transpose_cast(w_t, row_starts):
    n, k = len(row_starts), w_t.shape[1]
    return pl.pallas_call(
        _transpose_cast_kernel,
        grid_spec=pltpu.PrefetchScalarGridSpec(
            num_scalar_prefetch=1, grid=(n,),
            in_specs=[pl.BlockSpec((pl.Element(WPREP_ROWS), pl.Element(k)), lambda j, starts: (starts[j] * 8, 0))],
            out_specs=pl.BlockSpec((k, WPREP_ROWS), lambda j, starts: (0, j)),
        ),
        out_shape=jax.ShapeDtypeStruct((k, n * WPREP_ROWS), BF16),
        compiler_params=pltpu.CompilerParams(dimension_semantics=("arbitrary",)),
        name="transpose_cast",
    )(jnp.array([r // 8 for r in row_starts], jnp.int32), w_t)


WROW_OFFSET = {"wq": 2 * D_CONV, "wv": 2 * D_CONV + D_MLSTM, "wog": 2 * D_CONV + 2 * D_MLSTM,
               "wgm": 2 * D_CONV + 3 * D_MLSTM}

_MIXER_WEIGHTS = (
    "g1", "wrow", "bag", "bq", "wkT", "bk", "bv", "bog",
    "wgifT", "bgifT", "bgm", "wdw", "bdw", "lng", "lnb",
    "wco", "hng", "wmo", "wo", "g2", "wrt2", "brtT",
)


def _zero_after(x):
    bits = lax.bitcast_convert_type(x, jnp.uint32)
    bits = lax.shift_right_logical(lax.shift_right_logical(bits, jnp.uint32(16)), jnp.uint32(16))
    return lax.bitcast_convert_type(bits, F32)[0:1, :]


def _conv_block(upad_s, seg, base, cs, wdw_ref, bdw_ref, after=None):
    sub = 8
    first = CONV_PAD - CONV_K // 2
    acc = jnp.broadcast_to(bdw_ref[0:1, cs], (CONV_RB, LANES))
    for r in range(sub):
        z = None
        for a in range((CONV_K + first + sub - 1) // sub):
            j = sub * a + r - first
            if 0 <= j < CONV_K:
                lo = base + sub * a
                tap = wdw_ref[j:j + 1, cs] if after is None else wdw_ref[j:j + 1, cs] + after
                term = tap * upad_s[seg, lo:lo + CONV_RB + sub, cs]
                z = term if z is None else z + term
        acc = acc + z[r:r + CONV_RB, :]
    return acc


def _mixer_kernel(R, T, P, has_state, emit_state, mod_index, *refs):
    L = SUB
    n_mt = R // MIX_TM
    cpm = MIX_TM // L
    n_seq = R // T
    cps = T // L
    nseg = MIX_TM // P
    assert not has_state or n_seq == 1
    it = iter(refs)
    x_ref = next(it)
    mod_ref = next(it)
    if has_state:
        c0_ref = next(it)
        n0_ref = next(it)
        m0_ref = next(it)
    w = {name: next(it) for name in _MIXER_WEIGHTS}
    x1_ref = next(it)
    h2_ref = next(it)
    comb_ref = next(it)
    route_ref = next(it)
    cnt_ref = next(it)
    if emit_state:
        cout_ref = next(it)
        nout_ref = next(it)
        mout_ref = next(it)
    (q_s, kT_s, v_s, so_s, scan_s, ma_s, sgb_s, hm_s, cst_s, upad_s) = [next(it) for _ in range(10)]

    cond_row = mod_index(pl.program_id(0))

    def mod_row(i):
        return mod_ref[i, pl.ds(cond_row, 1), :]

    zpad = jnp.zeros((CONV_PAD, D_CONV), F32)
    for seg in range(nseg):
        upad_s[seg, 0:CONV_PAD, :] = zpad
        upad_s[seg, CONV_PAD + P:CONV_PAD + P + CONV_PAD, :] = zpad

    t_idx = lax.broadcasted_iota(jnp.int32, (L, L), 0)
    s_idx = lax.broadcasted_iota(jnp.int32, (L, L), 1)
    lower = s_idx <= t_idx
    upper = s_idx >= t_idx
    triu_b = upper.astype(F32).astype(BF16)
    lane_u = lax.broadcasted_iota(jnp.int32, (N_UNITS, L), 1)
    is_bwd = lax.broadcasted_iota(jnp.int32, (N_UNITS, L), 0) >= N_HEADS

    def gate_scan(g):
        gi, lf = g[:N_UNITS], _log_sigmoid(g[N_UNITS:])
        pr = _dot(jnp.concatenate(_split3(lf), axis=0).astype(BF16), triu_b)
        pre = pr[0:N_UNITS] + pr[N_UNITS:2 * N_UNITS] + pr[2 * N_UNITS:]
        tot = pre[:, L - 1:L]
        bsum = jnp.where(is_bwd, tot - pre + lf, pre)
        a = gi - bsum
        pm, sm, k = a, a, 1
        while k < L:
            pm = jnp.where(lane_u >= k, jnp.maximum(pm, pltpu.roll(pm, k, axis=1)), pm)
            sm = jnp.where(lane_u < L - k, jnp.maximum(sm, pltpu.roll(sm, L - k, axis=1)), sm)
            k *= 2
        wide = lambda v: jnp.broadcast_to(v, (N_UNITS, L))
        return jnp.concatenate([a, jnp.where(is_bwd, sm, pm), bsum, wide(tot),
                                wide(jnp.max(a, axis=1, keepdims=True))], axis=0)

    def phase1(i, carry):
        r0 = pl.multiple_of(i * MIX_TM, MIX_TM)
        rows = pl.ds(r0, MIX_TM)
        x = x_ref[0, rows, :]
        xn = x * lax.rsqrt(jnp.mean(x * x, axis=-1, keepdims=True) + EPS) * w["g1"][...]
        hb = (xn * (1.0 + mod_row(1)) + mod_row(0)).astype(BF16)

        gates = _dot_nt(w["wgifT"][...].astype(BF16), hb)
        gates = jnp.concatenate([gates[d * 2 * N_HEADS + g * N_HEADS:d * 2 * N_HEADS + (g + 1) * N_HEADS]
                                 for g in range(2) for d in range(2)], axis=0) + w["bgifT"][...]
        for j in range(cpm):
            scan_s[i * cpm + j] = gate_scan(gates[:, j * L:(j + 1) * L])
        ag = _dot(hb, w["wrow"][:, :2 * D_CONV]) + w["bag"][...]
        u = ag[:, :D_CONV] * _sigmoid(ag[:, D_CONV:])
        for seg in range(nseg):
            upad_s[seg, CONV_PAD:CONV_PAD + P, :] = u[seg * P:(seg + 1) * P, :]

        def proj(name, bias, c0, gate, width=2 * LANES):
            w0 = WROW_OFFSET[name] + c0
            b = w[bias][:, c0:c0 + width]
            if gate is not None:
                b = b + jnp.concatenate([gate] * (width // LANES), axis=1)
            return _dot(hb, w["wrow"][:, w0:w0 + width]) + b

        last = lambda z: z[-8:, -LANES:]

        def gm_a(c0, gate):
            z = proj("wgm", "bgm", c0, gate)
            ma_s[rows, c0:c0 + 2 * LANES] = _sigmoid(z)
            return last(z)

        def gm_b(c0, gate):
            z = proj("wgm", "bgm", D_MODEL + c0, gate)
            sgb_s[rows, c0:c0 + 2 * LANES] = _sigmoid(z)
            return last(z)

        def q_part(c0, gate):
            z = proj("wq", "bq", c0, gate)
            q_s[rows, c0:c0 + 2 * LANES] = (z * (HEAD_DIM ** -0.5)).astype(BF16)
            return last(z)

        def v_part(c0, gate):
            z = proj("wv", "bv", c0, gate)
            v_s[rows, c0:c0 + 2 * LANES] = z.astype(BF16)
            return last(z)

        def o_part(c0, gate):
            z = proj("wog", "bog", c0, gate)
            so_s[rows, c0:c0 + 2 * LANES] = _sigmoid(z)
            return last(z)

        def k_part(c0, gate):
            rs = slice(c0, c0 + 2 * LANES)
            b = w["bk"][rs, :] if gate is None else w["bk"][rs, :] + gate[:, 0:1]
            z = _dot_nt(w["wkT"][rs, :].astype(BF16), hb) + b
            kt = z.astype(BF16)
            for j in range(cpm):
                kT_s[i * cpm + j, rs, :] = kt[:, j * L:(j + 1) * L]
            return last(z)

        jobs = ([functools.partial(gm_a, c0) for c0 in range(0, D_MODEL, 2 * LANES)]
                + [functools.partial(gm_b, c0) for c0 in range(0, D_MODEL, 2 * LANES)]
                + [functools.partial(f, c0) for f in (q_part, k_part, v_part, o_part)
                   for c0 in range(0, D_MLSTM, 2 * LANES)])
        n_jobs = len(jobs)
        conv = {}
        after = None
        n_pieces = (D_CONV // LANES) * nseg * (P // CONV_RB)
        for cb in range(D_CONV // LANES):
            cs = slice(cb * LANES, (cb + 1) * LANES)
            for seg in range(nseg):
                for rb in range(P // CONV_RB):
                    blk = _conv_block(upad_s, seg, rb * CONV_RB, cs, w["wdw"], w["bdw"], after)
                    conv[(cb, seg, rb)] = blk
                    if jobs and len(conv) * n_jobs >= (n_jobs - len(jobs) + 1) * n_pieces:
                        after = _zero_after(jobs.pop(0)(_zero_after(blk[-8:, :])))
        for job in jobs:
            job(None)
        cu = jnp.concatenate(
            [jnp.concatenate([conv[(cb, seg, rb)] for seg in range(nseg) for rb in range(P // CONV_RB)], axis=0)
             for cb in range(D_CONV // LANES)], axis=1)
        mu = jnp.mean(cu, axis=-1, keepdims=True)
        cc = cu - mu
        cn = cc * lax.rsqrt(jnp.mean(cc * cc, axis=-1, keepdims=True) + EPS) * w["lng"][...] + w["lnb"][...]
        ca = (cn * _sigmoid(cn)).astype(BF16)
        ma_s[rows, :] = ma_s[rows, :] * _dot(ca, w["wco"][...])
        return carry

    if n_mt == 1:
        phase1(0, 0)
    else:
        lax.fori_loop(0, n_mt, phase1, 0)

    ones_col = (lax.broadcasted_iota(jnp.int32, (L, HEAD_DIM), 1) == 0).astype(F32).astype(BF16)
    pad_rows = jnp.zeros((LANES - 3 * N_UNITS, L), F32)

    def gate_prep(c, m_vec):
        sc = scan_s[c]
        a, run_max, bsum = sc[0:N_UNITS], sc[N_UNITS:2 * N_UNITS], sc[2 * N_UNITS:3 * N_UNITS]
        tot, a_max = sc[3 * N_UNITS:4 * N_UNITS, 0:1], sc[4 * N_UNITS:5 * N_UNITS, 0:1]
        big_m = jnp.maximum(m_vec, run_max)
        m_end = jnp.maximum(m_vec, a_max)
        cols = jnp.concatenate(
            [big_m, jnp.exp(m_vec - big_m), jnp.exp(-bsum - big_m), pad_rows], axis=0).T
        return a, cols, jnp.exp(a - m_end), jnp.exp(m_vec - m_end), tot + m_end

    qk_cache = {}

    def unit(d, hd, c, prep, first_chunk, want_state):
        a, cols, wk, decay, _ = prep
        rows = slice(c * L, (c + 1) * L)
        hs = slice(hd * HEAD_DIM, (hd + 1) * HEAD_DIM)
        idx = d * N_HEADS + hd
        col = lambda k: cols[:, k * N_UNITS + idx:k * N_UNITS + idx + 1]
        qc = q_s[rows, hs]
        kTc = kT_s[c, hs, :]
        vaug = jnp.concatenate([v_s[rows, hs], ones_col], axis=1)
        if cps == 1 and (hd, c) in qk_cache:
            qk = qk_cache[(hd, c)]
        else:
            qk = _dot(qc, kTc)
            qk_cache[(hd, c)] = qk
        w_intra = jnp.where(lower if d == 0 else upper, jnp.exp(a[idx:idx + 1, :] - col(0)), 0.0)
        nd = _dot((qk * w_intra).astype(BF16), vaug)
        if has_state or not first_chunk:
            nd = nd + col(1) * _dot(qc, cst_s[idx].astype(BF16))
        den = nd[:, HEAD_DIM:HEAD_DIM + 1]
        h = nd[:, :HEAD_DIM] * (1.0 / jnp.maximum(jnp.abs(den), col(2)))
        if d == 0:
            hm_s[rows, hs] = h
        else:
            hm_s[rows, hs] = hm_s[rows, hs] + h
        if want_state:
            kw = (kTc.astype(F32) * wk[idx:idx + 1, :]).astype(BF16)
            upd = _dot(kw, vaug)
            if has_state or not first_chunk:
                upd = upd + decay[idx:idx + 1, :] * cst_s[idx]
            cst_s[idx] = upd

    dir_rows = lax.broadcasted_iota(jnp.int32, (N_UNITS, 1), 0) >= N_HEADS
    for seq in range(n_seq):
        if has_state:
            n_cols = jnp.concatenate([n0_ref[0], jnp.zeros((LANES - N_UNITS, HEAD_DIM), F32)], axis=0).T
            first_lane = lax.broadcasted_iota(jnp.int32, (HEAD_DIM, HEAD_DIM), 1) == 0
            for idx in range(N_UNITS):
                cst_s[idx, :, :HEAD_DIM] = c0_ref[0, idx]
                cst_s[idx, :, HEAD_DIM:] = jnp.where(first_lane, n_cols[:, idx:idx + 1], 0.0)
            m_vec = m0_ref[0, :, 0:1]
        else:
            m_vec = jnp.zeros((N_UNITS, 1), F32)
        prep = None
        for d in range(2):
            order = list(range(cps)) if d == 0 else list(range(cps - 1, -1, -1))
            for pos, c in enumerate(order):
                if cps > 1 or prep is None:
                    prep = gate_prep(seq * cps + c, m_vec)
                for hd in range(N_HEADS):
                    unit(d, hd, seq * cps + c, prep, pos == 0, emit_state or pos < cps - 1)
                m_vec = jnp.where(dir_rows == (d == 1), prep[4], m_vec)
        if emit_state:
            for idx in range(N_UNITS):
                caug = cst_s[idx]
                cout_ref[0, seq * N_UNITS + idx] = caug[:, :HEAD_DIM]
                nout_ref[0, seq * N_UNITS + idx:seq * N_UNITS + idx + 1, :] = caug[:, HEAD_DIM:].T[0:1, :]
            mout_ref[0, seq * N_UNITS:(seq + 1) * N_UNITS, :] = jnp.broadcast_to(m_vec, (N_UNITS, LANES))

    e_iota = lax.broadcasted_iota(jnp.int32, (LANES, MIX_TM), 0)
    g_of_e = lax.shift_right_logical(e_iota, 2)
    j_of_e = lax.bitwise_and(e_iota, EXPERTS_PER_GROUP - 1)
    r8 = lax.broadcasted_iota(jnp.int32, (8, MIX_TM), 0)
    before_b = (lax.broadcasted_iota(jnp.int32, (MOE_BLK, MOE_BLK), 0)
                < lax.broadcasted_iota(jnp.int32, (MOE_BLK, MOE_BLK), 1)).astype(F32).astype(BF16)

    def phase3(i, carry):
        r0 = pl.multiple_of(i * MIX_TM, MIX_TM)
        rows = pl.ds(r0, MIX_TM)
        hm = hm_s[rows, :]
        heads = []
        for hd in range(N_HEADS):
            hh = hm[:, hd * HEAD_DIM:(hd + 1) * HEAD_DIM]
            heads.append(hh * lax.rsqrt(jnp.mean(hh * hh, axis=-1, keepdims=True) + EPS))
        hn = jnp.concatenate(heads, axis=1) * w["hng"][...]
        hb2 = (so_s[rows, :] * hn).astype(BF16)
        br_b = _dot(hb2, w["wmo"][...])
        mixed = (ma_s[rows, :] + sgb_s[rows, :] * br_b).astype(BF16)
        x1 = x_ref[0, rows, :] + mod_row(2) * _dot(mixed, w["wo"][...])
        x1_ref[0, rows, :] = x1
        xn = x1 * lax.rsqrt(jnp.mean(x1 * x1, axis=-1, keepdims=True) + EPS) * w["g2"][...]
        h2 = xn * (1.0 + mod_row(4)) + mod_row(3)
        h2_ref[0, rows, :] = h2.astype(BF16)

        h2_hi = h2.astype(BF16)
        h2_lo = (h2 - h2_hi.astype(F32)).astype(BF16)
        lg = _dot(h2_hi, w["wrt2"][...])
        lg = lg[:, :LANES] + lg[:, LANES:] + _dot(h2_lo, w["wrt2"][:, :LANES])
        lt = lg.T + w["brtT"][...]
        gl = [lt[N_EXPERTS + g:N_EXPERTS + g + 1, :] for g in range(N_GROUPS)]
        best, gsel = gl[0], jnp.zeros((1, MIX_TM), jnp.int32)
        for g in range(1, N_GROUPS):
            better = gl[g] > best
            gsel = jnp.where(better, g, gsel)
            best = jnp.where(better, gl[g], best)
        gp_sel = 1.0 / sum(jnp.exp(v - best) for v in gl)
        el = []
        for j in range(EXPERTS_PER_GROUP):
            v = lt[j:j + 1, :]
            for g in range(1, N_GROUPS):
                r = g * EXPERTS_PER_GROUP + j
                v = jnp.where(gsel == g, lt[r:r + 1, :], v)
            el.append(v)
        l1, e1 = el[0], jnp.zeros((1, MIX_TM), jnp.int32)
        for j in range(1, EXPERTS_PER_GROUP):
            better = el[j] > l1
            e1 = jnp.where(better, j, e1)
            l1 = jnp.where(better, el[j], l1)
        l2 = jnp.full((1, MIX_TM), -jnp.inf, F32)
        e2 = jnp.zeros((1, MIX_TM), jnp.int32)
        for j in range(EXPERTS_PER_GROUP):
            better = jnp.logical_and(e1 != j, el[j] > l2)
            e2 = jnp.where(better, j, e2)
            l2 = jnp.where(better, el[j], l2)
        r2 = jnp.exp(l2 - l1)
        wt1 = gp_sel / (1.0 + r2)
        wt2 = gp_sel * r2 / (1.0 + r2)
        in_group = g_of_e == gsel
        comb_t = (jnp.where(jnp.logical_and(in_group, j_of_e == e1), wt1, 0.0)
                  + jnp.where(jnp.logical_and(in_group, j_of_e == e2), wt2, 0.0))

        onehot = (r8 == gsel).astype(F32)
        gsel_f = gsel.astype(F32)
        rank = jnp.sum(onehot * _dot(onehot.astype(BF16), before_b), axis=0, keepdims=True)
        r8rows = pl.ds(pl.multiple_of(i * 8, 8), 8)
        route_ref[0, r8rows, :] = jnp.where(r8 == 0, gsel_f, jnp.where(r8 == 1, rank, 0.0))
        cnt_ref[0, r8rows, :] = jnp.broadcast_to(jnp.sum(onehot, axis=1, keepdims=True), (8, LANES))
        comb_t = jnp.where(e_iota == ROUTE_GROUP_LANE, gsel_f,
                           jnp.where(e_iota == ROUTE_RANK_LANE, rank, comb_t))
        comb_ref[0, rows, :] = comb_t.T
        return carry

    if n_mt == 1:
        phase3(0, 0)
    else:
        lax.fori_loop(0, n_mt, phase3, 0)


class _RowWindow(NamedTuple):
    array: jax.Array
    start: int
    n: int


def _const_spec(a):
    if isinstance(a, _RowWindow):
        assert a.start % a.n == 0
        return a.array, pl.BlockSpec((a.n, a.array.shape[1]), lambda b: (a.start // a.n, 0),
                                     pipeline_mode=pl.Buffered(1))
    nd = a.ndim
    return a, pl.BlockSpec(a.shape, lambda b, _nd=nd: (0,) * _nd, pipeline_mode=pl.Buffered(1))


def _mixer(x, T, mod, mod_index, weights, P, state=None, emit_state=False):
    B, R, _ = x.shape
    n_chunks = R // SUB
    n_blk = R // MOE_BLK
    n_seq = R // T
    has_state = state is not None
    seq_mode = {} if R <= MIX_TM else {"pipeline_mode": pl.Buffered(1)}
    in_specs = [
        pl.BlockSpec((1, R, D_MODEL), lambda b: (b, 0, 0), **seq_mode),
        pl.BlockSpec(mod.shape, lambda b: (0, 0, 0)),
    ]
    args = [x, mod]
    if has_state:
        c0, n0, m0 = state
        in_specs += [
            pl.BlockSpec((1, N_UNITS, HEAD_DIM, HEAD_DIM), lambda b: (b, 0, 0, 0)),
            pl.BlockSpec((1, N_UNITS, HEAD_DIM), lambda b: (b, 0, 0)),
            pl.BlockSpec((1, N_UNITS, LANES), lambda b: (b, 0, 0)),
        ]
        args += [c0, n0, m0]
    for name in _MIXER_WEIGHTS:
        operand, spec = _const_spec(weights[name])
        in_specs.append(spec)
        args.append(operand)
    out_shape = [
        jax.ShapeDtypeStruct((B, R, D_MODEL), F32),
        jax.ShapeDtypeStruct((B, R, D_MODEL), BF16),
        jax.ShapeDtypeStruct((B, R, LANES), F32),
        jax.ShapeDtypeStruct((B, n_blk * 8, MOE_BLK), F32),
        jax.ShapeDtypeStruct((B, n_blk * 8, LANES), F32),
    ]
    out_specs = [
        pl.BlockSpec((1, R, D_MODEL), lambda b: (b, 0, 0), **seq_mode),
        pl.BlockSpec((1, R, D_MODEL), lambda b: (b, 0, 0), **seq_mode),
        pl.BlockSpec((1, R, LANES), lambda b: (b, 0, 0)),
        pl.BlockSpec((1, n_blk * 8, MOE_BLK), lambda b: (b, 0, 0)),
        pl.BlockSpec((1, n_blk * 8, LANES), lambda b: (b, 0, 0)),
    ]
    if emit_state:
        out_shape += [
            jax.ShapeDtypeStruct((B, n_seq * N_UNITS, HEAD_DIM, HEAD_DIM), F32),
            jax.ShapeDtypeStruct((B, n_seq * N_UNITS, HEAD_DIM), F32),
            jax.ShapeDtypeStruct((B, n_seq * N_UNITS, LANES), F32),
        ]
        out_specs += [
            pl.BlockSpec((1, n_seq * N_UNITS, HEAD_DIM, HEAD_DIM), lambda b: (b, 0, 0, 0)),
            pl.BlockSpec((1, n_seq * N_UNITS, HEAD_DIM), lambda b: (b, 0, 0)),
            pl.BlockSpec((1, n_seq * N_UNITS, LANES), lambda b: (b, 0, 0)),
        ]
    scratch = [
        pltpu.VMEM((R, D_MLSTM), BF16),
        pltpu.VMEM((n_chunks, D_MLSTM, SUB), BF16),
        pltpu.VMEM((R, D_MLSTM), BF16),
        pltpu.VMEM((R, D_MLSTM), F32),
        pltpu.VMEM((n_chunks, 5 * N_UNITS, SUB), F32),
        pltpu.VMEM((R, D_MODEL), F32),
        pltpu.VMEM((R, D_MODEL), F32),
        pltpu.VMEM((R, D_MLSTM), F32),
        pltpu.VMEM((N_UNITS, HEAD_DIM, 2 * HEAD_DIM), F32),
        pltpu.VMEM((MIX_TM // P, P + 2 * CONV_PAD, D_CONV), F32),
    ]
    return pl.pallas_call(
        functools.partial(_mixer_kernel, R, T, P, has_state, emit_state, mod_index),
        grid=(B,),
        in_specs=in_specs,
        out_specs=out_specs,
        out_shape=out_shape,
        scratch_shapes=scratch,
        compiler_params=pltpu.CompilerParams(
            dimension_semantics=("arbitrary",), vmem_limit_bytes=VMEM_LIMIT),
        name="mixer_T%d" % T,
    )(*args)


def _dest_in_block(group, rank, starts):
    dest = rank
    for g in range(N_GROUPS):
        dest = dest + jnp.where(group == float(g), starts[g], 0.0)
    return dest


def _copy_segments(src_refs, dst_refs, src_starts, dst_starts, n_pieces):
    for g in range(N_GROUPS):
        def body(k, carry, g=g):
            s = pl.multiple_of(src_starts[g] + k * ROW_ALIGN, ROW_ALIGN)
            d = pl.multiple_of(dst_starts[g] + k * ROW_ALIGN, ROW_ALIGN)
            for src, dst in zip(src_refs, dst_refs):
                dst[pl.ds(d, ROW_ALIGN), :] = src[pl.ds(s, ROW_ALIGN), :]
            return carry
        lax.fori_loop(0, n_pieces[g], body, 0)


def _plan_segments(n_blocks, n_tiles, cnt_ref, start_ref, npiece_ref, off_ref, tgroup_ref, tvalid_ref, tfirst_ref):
    align_shift = ROW_ALIGN.bit_length() - 1
    tile_shift = MOE_TM.bit_length() - 1

    def block_starts(blk, carry):
        row = jnp.int32(0)
        for g in range(N_GROUPS):
            n = lax.shift_right_logical(cnt_ref[blk * N_GROUPS + g] + (ROW_ALIGN - 1), align_shift)
            npiece_ref[blk * N_GROUPS + g] = n
            start_ref[blk * N_GROUPS + g] = row
            row = row + n * ROW_ALIGN
        return carry

    lax.fori_loop(0, n_blocks, block_starts, 0)

    base_row = jnp.int32(0)
    base_tile = jnp.int32(0)
    last_group = jnp.int32(0)
    for g in range(N_GROUPS):
        def seg_offsets(blk, row, g=g, base_row=base_row):
            off_ref[blk * N_GROUPS + g] = base_row + row
            return row + npiece_ref[blk * N_GROUPS + g] * ROW_ALIGN

        rows = lax.fori_loop(0, n_blocks, seg_offsets, jnp.int32(0))
        tiles = lax.shift_right_logical(rows + (MOE_TM - 1), tile_shift)

        def mark_tiles(t, carry, g=g, base_tile=base_tile):
            tgroup_ref[base_tile + t] = g
            tvalid_ref[base_tile + t] = 1
            tfirst_ref[base_tile + t] = (t == 0).astype(jnp.int32)
            return carry

        lax.fori_loop(0, tiles, mark_tiles, 0)
        last_group = jnp.where(tiles > 0, g, last_group)
        base_row = base_row + tiles * MOE_TM
        base_tile = base_tile + tiles

    def mark_unused(t, carry):
        tgroup_ref[t] = last_group
        tvalid_ref[t] = 0
        tfirst_ref[t] = 0
        return carry

    lax.fori_loop(base_tile, n_tiles, mark_unused, 0)


def _dispatch_kernel(n_ctx_blocks, n_blocks, n_tiles, cnt_ref,
                     h2c_ref, h2l_ref, cbc_ref, cbl_ref, rtc_ref, rtl_ref,
                     xs_ref, cs_ref, start_ref, npiece_ref, off_ref, tgroup_ref, tvalid_ref, tfirst_ref,
                     sx_s, sc_s):
    b = pl.program_id(0)
    is_ctx = b < n_ctx_blocks

    @pl.when(b == 0)
    def _():
        _plan_segments(n_blocks, n_tiles, cnt_ref, start_ref, npiece_ref, off_ref,
                       tgroup_ref, tvalid_ref, tfirst_ref)
        xs_ref[...] = jnp.zeros_like(xs_ref)
        cs_ref[...] = jnp.zeros_like(cs_ref)

    h2 = jnp.where(is_ctx, h2c_ref[0], h2l_ref[0])
    cb = jnp.where(is_ctx, cbc_ref[0], cbl_ref[0])
    rt = jnp.where(is_ctx, rtc_ref[0], rtl_ref[0])
    starts = [start_ref[b * N_GROUPS + g] for g in range(N_GROUPS)]
    dest = _dest_in_block(rt[0:1, :], rt[1:2, :], [s.astype(F32) for s in starts])
    row = lax.broadcasted_iota(jnp.int32, (SORT_ROWS, MOE_BLK), 0).astype(F32)
    perm = (row == dest).astype(F32).astype(BF16)
    cb_hi = cb.astype(BF16)
    cb_lo = (cb - cb_hi.astype(F32)).astype(BF16)
    sx_s[...] = _dot(perm, h2).astype(BF16)
    sc_s[...] = _dot(perm, jnp.concatenate([cb_hi, cb_lo], axis=1)).astype(BF16)
    _copy_segments((sx_s, sc_s), (xs_ref, cs_ref), starts,
                   [off_ref[b * N_GROUPS + g] for g in range(N_GROUPS)],
                   [npiece_ref[b * N_GROUPS + g] for g in range(N_GROUPS)])


def _experts_kernel(tgroup_ref, tvalid_ref, tfirst_ref, xs_ref, cs_ref, wg_ref, wu_ref, wd_ref, ys_ref,
                    wg_s, wu_s, wd_s):
    i = pl.program_id(0)

    @pl.when(tfirst_ref[i] == 1)
    def _():
        for j in range(EXPERTS_PER_GROUP):
            cols = slice(j * D_EXPERT, (j + 1) * D_EXPERT)
            wg_s[:, cols] = wg_ref[j].astype(BF16)
            wu_s[:, cols] = wu_ref[j].astype(BF16)
            wd_s[cols, :] = wd_ref[j].astype(BF16)

    @pl.when(tvalid_ref[i] == 1)
    def _():
        x = xs_ref[...]
        g = _dot(x, wg_s[...])
        u = _dot(x, wu_s[...])
        comb = cs_ref[:, :LANES].astype(F32) + cs_ref[:, LANES:].astype(F32)
        lane = lax.broadcasted_iota(jnp.int32, comb.shape, 1)
        first = tgroup_ref[i] * EXPERTS_PER_GROUP
        parts = []
        for j in range(EXPERTS_PER_GROUP):
            cols = slice(j * D_EXPERT, (j + 1) * D_EXPERT)
            cw = jnp.sum(jnp.where(lane == first + j, comb, 0.0), axis=1, keepdims=True)
            gj = g[:, cols]
            parts.append((gj * _sigmoid(gj) * u[:, cols] * cw).astype(BF16))
        ys_ref[...] = _dot(jnp.concatenate(parts, axis=1), wd_s[...]).astype(BF16)

    @pl.when(tvalid_ref[i] == 0)
    def _():
        ys_ref[...] = jnp.zeros_like(ys_ref)


def _combine_kernel(n_ctx_blocks, blocks_per_lat_seq, start_ref, npiece_ref, off_ref,
                    x1c_ref, x1l_ref, cbc_ref, cbl_ref, ys_ref, mod_ref, gf_ref, yc_ref, yl_ref, loc_s):
    b = pl.program_id(0)
    is_ctx = b < n_ctx_blocks
    starts = [start_ref[b * N_GROUPS + g] for g in range(N_GROUPS)]
    loc_s[...] = jnp.zeros_like(loc_s)
    _copy_segments((ys_ref,), (loc_s,), [off_ref[b * N_GROUPS + g] for g in range(N_GROUPS)], starts,
                   [npiece_ref[b * N_GROUPS + g] for g in range(N_GROUPS)])
    cb = jnp.where(is_ctx, cbc_ref[0], cbl_ref[0])
    dest = _dest_in_block(cb[:, ROUTE_GROUP_LANE:ROUTE_GROUP_LANE + 1],
                          cb[:, ROUTE_RANK_LANE:ROUTE_RANK_LANE + 1],
                          [s.astype(F32) for s in starts])
    col = lax.broadcasted_iota(jnp.int32, (MOE_BLK, SORT_ROWS), 1).astype(F32)
    unperm = (col == dest).astype(F32).astype(BF16)
    moe = _dot(unperm, loc_s[...])
    x1 = jnp.where(is_ctx, x1c_ref[0], x1l_ref[0])
    mrow = jnp.where(is_ctx, 0, 1 + jnp.maximum(b - n_ctx_blocks, 0) // blocks_per_lat_seq)
    x2 = x1 + mod_ref[N_ADA - 1, pl.ds(mrow, 1), :] * moe
    y = x2 * lax.rsqrt(jnp.mean(x2 * x2, axis=-1, keepdims=True) + EPS) * gf_ref[...]

    @pl.when(is_ctx)
    def _():
        yc_ref[0] = y

    @pl.when(jnp.logical_not(is_ctx))
    def _():
        yl_ref[0] = y


def _moe(x1c, x1l, h2c, h2l, cbc, cbl, rtc, rtl, cnt, mod, blocks_per_lat_seq, wg, wu, wd, gf):
    nc, nl = x1c.shape[0], x1l.shape[0]
    nb = nc + nl
    n_rows_max = nb * MOE_BLK + nb * N_GROUPS * (ROW_ALIGN - 1) + N_GROUPS * (MOE_TM - ROW_ALIGN)
    n_tiles = -(-n_rows_max // MOE_TM)
    ns = n_tiles * MOE_TM

    cmap = lambda b, *_: (jnp.minimum(b, nc - 1), 0, 0)
    lmap = lambda b, *_: (jnp.maximum(b - nc, 0), 0, 0)
    whole = lambda *_: (0, 0)
    once = {"pipeline_mode": pl.Buffered(1)}
    arb = pltpu.CompilerParams(dimension_semantics=("arbitrary",), vmem_limit_bytes=VMEM_LIMIT)
    smem = pl.BlockSpec(memory_space=pltpu.SMEM)
    seg_i32 = jax.ShapeDtypeStruct((nb * N_GROUPS,), jnp.int32)
    tile_i32 = jax.ShapeDtypeStruct((n_tiles,), jnp.int32)

    xs, cs, start, npiece, off, tgroup, tvalid, tfirst = pl.pallas_call(
        functools.partial(_dispatch_kernel, nc, nb, n_tiles),
        grid_spec=pltpu.PrefetchScalarGridSpec(
            num_scalar_prefetch=1, grid=(nb,),
            in_specs=[
                pl.BlockSpec((1, MOE_BLK, D_MODEL), cmap), pl.BlockSpec((1, MOE_BLK, D_MODEL), lmap),
                pl.BlockSpec((1, MOE_BLK, LANES), cmap), pl.BlockSpec((1, MOE_BLK, LANES), lmap),
                pl.BlockSpec((1, 8, MOE_BLK), cmap), pl.BlockSpec((1, 8, MOE_BLK), lmap),
            ],
            out_specs=[pl.BlockSpec((ns, D_MODEL), whole, **once), pl.BlockSpec((ns, 2 * LANES), whole, **once),
                       smem, smem, smem, smem, smem, smem],
            scratch_shapes=[pltpu.VMEM((SORT_ROWS, D_MODEL), BF16), pltpu.VMEM((SORT_ROWS, 2 * LANES), BF16)],
        ),
        out_shape=[jax.ShapeDtypeStruct((ns, D_MODEL), BF16), jax.ShapeDtypeStruct((ns, 2 * LANES), BF16),
                   seg_i32, seg_i32, seg_i32, tile_i32, tile_i32, tile_i32],
        compiler_params=arb,
        name="moe_dispatch",
    )(cnt, h2c, h2l, cbc, cbl, rtc, rtl)

    wmap = lambda i, tg, tv, tf: (tg[i], 0, 0)
    ys = pl.pallas_call(
        _experts_kernel,
        grid_spec=pltpu.PrefetchScalarGridSpec(
            num_scalar_prefetch=3, grid=(n_tiles,),
            in_specs=[
                pl.BlockSpec((MOE_TM, D_MODEL), lambda i, *_: (i, 0)),
                pl.BlockSpec((MOE_TM, 2 * LANES), lambda i, *_: (i, 0)),
                pl.BlockSpec((EXPERTS_PER_GROUP, D_MODEL, D_EXPERT), wmap),
                pl.BlockSpec((EXPERTS_PER_GROUP, D_MODEL, D_EXPERT), wmap),
                pl.BlockSpec((EXPERTS_PER_GROUP, D_EXPERT, D_MODEL), wmap),
            ],
            out_specs=pl.BlockSpec((MOE_TM, D_MODEL), lambda i, *_: (i, 0)),
            scratch_shapes=[pltpu.VMEM((D_MODEL, EXPERTS_PER_GROUP * D_EXPERT), BF16),
                            pltpu.VMEM((D_MODEL, EXPERTS_PER_GROUP * D_EXPERT), BF16),
                            pltpu.VMEM((EXPERTS_PER_GROUP * D_EXPERT, D_MODEL), BF16)],
        ),
        out_shape=jax.ShapeDtypeStruct((ns, D_MODEL), BF16),
        compiler_params=arb,
        name="moe_experts",
    )(tgroup, tvalid, tfirst, xs, cs, wg, wu, wd)

    yc, yl = pl.pallas_call(
        functools.partial(_combine_kernel, nc, blocks_per_lat_seq),
        grid_spec=pltpu.PrefetchScalarGridSpec(
            num_scalar_prefetch=3, grid=(nb,),
            in_specs=[
                pl.BlockSpec((1, MOE_BLK, D_MODEL), cmap), pl.BlockSpec((1, MOE_BLK, D_MODEL), lmap),
                pl.BlockSpec((1, MOE_BLK, LANES), cmap), pl.BlockSpec((1, MOE_BLK, LANES), lmap),
                pl.BlockSpec((ns, D_MODEL), whole, **once),
                pl.BlockSpec(mod.shape, lambda *_: (0, 0, 0)),
                pl.BlockSpec((1, D_MODEL), whole),
            ],
            out_specs=[pl.BlockSpec((1, MOE_BLK, D_MODEL), cmap), pl.BlockSpec((1, MOE_BLK, D_MODEL), lmap)],
            scratch_shapes=[pltpu.VMEM((SORT_ROWS, D_MODEL), BF16)],
        ),
        out_shape=[jax.ShapeDtypeStruct((nc, MOE_BLK, D_MODEL), F32),
                   jax.ShapeDtypeStruct((nl, MOE_BLK, D_MODEL), F32)],
        compiler_params=arb,
        name="moe_combine",
    )(start, npiece, off, x1c, x1l, cbc, cbl, ys, mod, gf)
    return yc, yl


def _prep_weights(norm1_g, w_in, b_in, b_gates, w_dw, b_dw, conv_ln_g, conv_ln_b, w_conv_out,
                  mlstm_hn_g, w_mlstm_out, w_o, norm2_g, w_rg, b_rg, w_re, b_re):
    s_a = 2 * D_CONV
    s_q = s_a + D_MLSTM
    s_k = s_q + D_MLSTM
    s_v = s_k + D_MLSTM
    s_o = s_v + D_MLSTM
    s_g = s_o + 4 * N_HEADS
    row = lambda v: v.reshape(1, -1).astype(F32)
    w_t = w_in.T
    keep = [(0, s_q), (s_k, s_o), (s_g, w_in.shape[1])]
    wrow = _transpose_cast(w_t, [r for a, b in keep for r in range(a, b, WPREP_ROWS)])
    bg = (b_in[s_o:s_g] + b_gates.reshape(-1)).reshape(2, 2, N_HEADS).transpose(1, 0, 2).reshape(-1, 1)
    row_window = lambda start, n: _RowWindow(w_t, start, n)
    n_rt = N_EXPERTS + N_GROUPS
    wrt = jnp.pad(jnp.concatenate([w_re, w_rg], axis=1), ((0, 0), (0, LANES - n_rt)))
    wrt_hi = wrt.astype(BF16)
    wrt2 = jnp.concatenate([wrt_hi, (wrt - wrt_hi.astype(F32)).astype(BF16)], axis=1)
    brtT = jnp.pad(jnp.concatenate([b_re, b_rg]), (0, LANES - n_rt)).reshape(LANES, 1)
    return {
        "g1": row(norm1_g),
        "wrow": wrow, "bag": row(b_in[:s_a]), "bq": row(b_in[s_a:s_q]),
        "wkT": row_window(s_q, D_MLSTM), "bk": b_in[s_q:s_k].reshape(-1, 1),
        "bv": row(b_in[s_k:s_v]), "bog": row(b_in[s_v:s_o]),
        "wgifT": row_window(s_o, 4 * N_HEADS), "bgifT": bg, "bgm": row(b_in[s_g:]),
        "wdw": w_dw.astype(F32), "bdw": row(b_dw), "lng": row(conv_ln_g), "lnb": row(conv_ln_b),
        "wco": w_conv_out.astype(BF16), "hng": row(mlstm_hn_g), "wmo": w_mlstm_out.astype(BF16),
        "wo": w_o.astype(BF16), "g2": row(norm2_g), "wrt2": wrt2, "brtT": brtT,
    }


def kernel(x_prompt, x_sample, state_C, state_n, state_m, c, c_ctx, norm1_g, w_ada, b_ada, w_in, b_in, b_gates, w_dw, b_dw, conv_ln_g, conv_ln_b, w_conv_out, mlstm_hn_g, w_mlstm_out, w_o, norm2_g, w_rg, b_rg, w_re, b_re, w_e_gate, w_e_up, w_e_down, norm_final_g):
    B, S, _ = x_prompt.shape
    Bd, Sd, _ = x_sample.shape
    assert w_ada.shape[0] == 1, "single trunk layer"
    assert S == SUB and Sd % SUB == 0

    cin = jnp.concatenate([c_ctx[None, :], c, jnp.zeros((8 - 1 - Bd, D_MODEL), F32)], axis=0)
    mod = _ada(cin, 1 + Bd, w_ada[0], b_ada[0].reshape(1, -1))

    wts = _prep_weights(norm1_g[0], w_in[0], b_in[0], b_gates[0], w_dw[0], b_dw[0], conv_ln_g[0],
                        conv_ln_b[0], w_conv_out[0], mlstm_hn_g[0], w_mlstm_out[0], w_o[0],
                        norm2_g[0], w_rg[0], b_rg[0], w_re[0], b_re[0])

    x1p, h2p, cbp, rtp, cntp, c_new, n_new, m_new = _mixer(
        x_prompt.reshape(B * S // MIX_TM, MIX_TM, D_MODEL), S, mod, lambda b: 0, wts, P=S, emit_state=True)

    m0 = jnp.broadcast_to(state_m[:, 0].reshape(Bd, N_UNITS, 1), (Bd, N_UNITS, LANES))
    state = (state_C[:, 0].reshape(Bd, N_UNITS, HEAD_DIM, HEAD_DIM), state_n[:, 0].reshape(Bd, N_UNITS, HEAD_DIM), m0)
    x1s, h2s, cbs, rts, cnts = _mixer(x_sample, Sd, mod, lambda b: 1 + b, wts, P=GRID_W, state=state)

    nc, nl = B * S // MOE_BLK, Bd * Sd // MOE_BLK
    blk = lambda a, n: a.reshape(n, MOE_BLK, a.shape[-1])
    cnt = jnp.concatenate([cntp.reshape(nc, 8, LANES)[:, :N_GROUPS, 0],
                           cnts.reshape(nl, 8, LANES)[:, :N_GROUPS, 0]], axis=0)
    yp, ys = _moe(blk(x1p, nc), blk(x1s, nl), blk(h2p, nc), blk(h2s, nl), blk(cbp, nc), blk(cbs, nl),
                  rtp.reshape(nc, 8, MOE_BLK), rts.reshape(nl, 8, MOE_BLK),
                  cnt.astype(jnp.int32).reshape(-1), mod, Sd // MOE_BLK, w_e_gate[0], w_e_up[0], w_e_down[0],
                  norm_final_g.reshape(1, -1))

    return (yp.reshape(B, S, D_MODEL), ys.reshape(Bd, Sd, D_MODEL),
            c_new.reshape(B, 1, 2, N_HEADS, HEAD_DIM, HEAD_DIM),
            n_new.reshape(B, 1, 2, N_HEADS, HEAD_DIM),
            m_new[:, :, 0].reshape(B, 1, 2, N_HEADS))
```

```python
import functools
from typing import NamedTuple

import jax
import jax.numpy as jnp
from jax import lax
from jax.experimental import pallas as pl
from jax.experimental.pallas import tpu as pltpu

D_MODEL = 1024
D_CONV = 512
CONV_K = 31
D_MLSTM = 512
N_HEADS = 4
HEAD_DIM = D_MLSTM // N_HEADS
N_GROUPS = 4
EXPERTS_PER_GROUP = 4
N_EXPERTS = N_GROUPS * EXPERTS_PER_GROUP
D_EXPERT = 256
N_ADA = 6
EPS = 1e-6
GRID_W = 64

LANES = 128
SUB = 256
CONV_PAD = 16
CONV_RB = 64
N_UNITS = 2 * N_HEADS
ROW_ALIGN = 16
MOE_TM = 512
MIX_TM = 512
MOE_BLK = MIX_TM
SORT_ROWS = MOE_BLK + N_GROUPS * ROW_ALIGN
WPREP_ROWS = 512
ROUTE_GROUP_LANE = N_EXPERTS
ROUTE_RANK_LANE = N_EXPERTS + 1
VMEM_LIMIT = 58 * 1024 * 1024

BF16 = jnp.bfloat16
F32 = jnp.float32
NT_DIMS = (((1,), (1,)), ((), ()))


def _dot(a, b):
    return jnp.dot(a, b, preferred_element_type=F32)


def _dot_nt(a, b, precision=None):
    return lax.dot_general(a, b, NT_DIMS, preferred_element_type=F32, precision=precision)


def _sigmoid(x):
    return 0.5 * jnp.tanh(0.5 * x) + 0.5


def _log_sigmoid(x):
    return jnp.minimum(x, 0.0) - jnp.log1p(jnp.exp(-jnp.abs(x)))


def _split3(x):
    hi = x.astype(BF16).astype(F32)
    r1 = x - hi
    mid = r1.astype(BF16).astype(F32)
    lo = (r1 - mid).astype(BF16).astype(F32)
    return hi, mid, lo


def _ada_kernel(c_ref, w_ref, b_ref, o_ref):
    c = c_ref[...]
    s = (c * _sigmoid(c)).astype(BF16)
    o_ref[0] = _dot(s, w_ref[...].astype(BF16)) + b_ref[...]


def _ada(cin, w_ada, b_ada):
    return pl.pallas_call(
        _ada_kernel,
        grid=(N_ADA,),
        in_specs=[
            pl.BlockSpec((8, D_MODEL), lambda j: (0, 0)),
            pl.BlockSpec((D_MODEL, D_MODEL), lambda j: (0, j)),
            pl.BlockSpec((1, D_MODEL), lambda j: (0, j)),
        ],
        out_specs=pl.BlockSpec((1, 8, D_MODEL), lambda j: (j, 0, 0)),
        out_shape=jax.ShapeDtypeStruct((N_ADA, 8, D_MODEL), F32),
        compiler_params=pltpu.CompilerParams(dimension_semantics=("arbitrary",)),
        name="ada",
    )(cin, w_ada, b_ada)


def _transpose_cast_kernel(starts_ref, wt_ref, o_ref):
    o_ref[...] = wt_ref[...].astype(BF16).T


def _transpose_cast(w_t, row_starts):
    n, k = len(row_starts), w_t.shape[1]
    return pl.pallas_call(
        _transpose_cast_kernel,
        grid_spec=pltpu.PrefetchScalarGridSpec(
            num_scalar_prefetch=1, grid=(n,),
            in_specs=[pl.BlockSpec((pl.Element(WPREP_ROWS), pl.Element(k)), lambda j, starts: (starts[j] * 8, 0))],
            out_specs=pl.BlockSpec((k, WPREP_ROWS), lambda j, starts: (0, j)),
        ),
        out_shape=jax.ShapeDtypeStruct((k, n * WPREP_ROWS), BF16),
        compiler_params=pltpu.CompilerParams(dimension_semantics=("arbitrary",)),
        name="transpose_cast",
    )(jnp.array([r // 8 for r in row_starts], jnp.int32), w_t)


WROW_OFFSET = {"wq": 2 * D_CONV, "wv": 2 * D_CONV + D_MLSTM, "wog": 2 * D_CONV + 2 * D_MLSTM,
               "wgm": 2 * D_CONV + 3 * D_MLSTM}

_MIXER_WEIGHTS = (
    "g1", "wrow", "bag", "bq", "wkT", "bk", "bv", "bog",
    "wgifT", "bgifT", "bgm", "wdw", "bdw", "lng", "lnb",
    "wco", "hng", "wmo", "wo", "g2", "wrt2", "brtT",
)


def _zero_after(x):
    bits = lax.bitcast_convert_type(x, jnp.uint32)
    bits = lax.shift_right_logical(lax.shift_right_logical(bits, jnp.uint32(16)), jnp.uint32(16))
    return lax.bitcast_convert_type(bits, F32)[0:1, :]


def _conv_block(upad_s, seg, base, cs, wdw_ref, bdw_ref, after=None):
    sub = 8
    first = CONV_PAD - CONV_K // 2
    acc = jnp.broadcast_to(bdw_ref[0:1, cs], (CONV_RB, LANES))
    for r in range(sub):
        z = None
        for a in range((CONV_K + first + sub - 1) // sub):
            j = sub * a + r - first
            if 0 <= j < CONV_K:
                lo = base + sub * a
                tap = wdw_ref[j:j + 1, cs] if after is None else wdw_ref[j:j + 1, cs] + after
                term = tap * upad_s[seg, lo:lo + CONV_RB + sub, cs]
                z = term if z is None else z + term
        acc = acc + z[r:r + CONV_RB, :]
    return acc


def _mixer_kernel(R, T, P, has_state, emit_state, mod_index, *refs):
    L = SUB
    n_mt = R // MIX_TM
    cpm = MIX_TM // L
    n_seq = R // T
    cps = T // L
    nseg = MIX_TM // P
    assert not has_state or n_seq == 1
    it = iter(refs)
    x_ref = next(it)
    mod_ref = next(it)
    if has_state:
        c0_ref = next(it)
        n0_ref = next(it)
        m0_ref = next(it)
    w = {name: next(it) for name in _MIXER_WEIGHTS}
    x1_ref = next(it)
    h2_ref = next(it)
    comb_ref = next(it)
    route_ref = next(it)
    cnt_ref = next(it)
    if emit_state:
        cout_ref = next(it)
        nout_ref = next(it)
        mout_ref = next(it)
    (q_s, kT_s, v_s, so_s, scan_s, ma_s, sgb_s, hm_s, cst_s, upad_s) = [next(it) for _ in range(10)]

    cond_row = mod_index(pl.program_id(0))

    def mod_row(i):
        return mod_ref[i, pl.ds(cond_row, 1), :]

    zpad = jnp.zeros((CONV_PAD, D_CONV), F32)
    for seg in range(nseg):
        upad_s[seg, 0:CONV_PAD, :] = zpad
        upad_s[seg, CONV_PAD + P:CONV_PAD + P + CONV_PAD, :] = zpad

    t_idx = lax.broadcasted_iota(jnp.int32, (L, L), 0)
    s_idx = lax.broadcasted_iota(jnp.int32, (L, L), 1)
    lower = s_idx <= t_idx
    upper = s_idx >= t_idx
    triu_b = upper.astype(F32).astype(BF16)
    lane_u = lax.broadcasted_iota(jnp.int32, (N_UNITS, L), 1)
    is_bwd = lax.broadcasted_iota(jnp.int32, (N_UNITS, L), 0) >= N_HEADS

    def gate_scan(g):
        gi, lf = g[:N_UNITS], _log_sigmoid(g[N_UNITS:])
        pr = _dot(jnp.concatenate(_split3(lf), axis=0).astype(BF16), triu_b)
        pre = pr[0:N_UNITS] + pr[N_UNITS:2 * N_UNITS] + pr[2 * N_UNITS:]
        tot = pre[:, L - 1:L]
        bsum = jnp.where(is_bwd, tot - pre + lf, pre)
        a = gi - bsum
        pm, sm, k = a, a, 1
        while k < L:
            pm = jnp.where(lane_u >= k, jnp.maximum(pm, pltpu.roll(pm, k, axis=1)), pm)
            sm = jnp.where(lane_u < L - k, jnp.maximum(sm, pltpu.roll(sm, L - k, axis=1)), sm)
            k *= 2
        wide = lambda v: jnp.broadcast_to(v, (N_UNITS, L))
        return jnp.concatenate([a, jnp.where(is_bwd, sm, pm), bsum, wide(tot),
                                wide(jnp.max(a, axis=1, keepdims=True))], axis=0)

    def phase1(i, carry):
        r0 = pl.multiple_of(i * MIX_TM, MIX_TM)
        rows = pl.ds(r0, MIX_TM)
        x = x_ref[0, rows, :]
        xn = x * lax.rsqrt(jnp.mean(x * x, axis=-1, keepdims=True) + EPS) * w["g1"][...]
        hb = (xn * (1.0 + mod_row(1)) + mod_row(0)).astype(BF16)

        gates = _dot_nt(w["wgifT"][...].astype(BF16), hb)
        gates = jnp.concatenate([gates[d * 2 * N_HEADS + g * N_HEADS:d * 2 * N_HEADS + (g + 1) * N_HEADS]
                                 for g in range(2) for d in range(2)], axis=0) + w["bgifT"][...]
        for j in range(cpm):
            scan_s[i * cpm + j] = gate_scan(gates[:, j * L:(j + 1) * L])
        ag = _dot(hb, w["wrow"][:, :2 * D_CONV]) + w["bag"][...]
        u = ag[:, :D_CONV] * _sigmoid(ag[:, D_CONV:])
        for seg in range(nseg):
            upad_s[seg, CONV_PAD:CONV_PAD + P, :] = u[seg * P:(seg + 1) * P, :]

        def proj(name, bias, c0, gate, width=2 * LANES):
            w0 = WROW_OFFSET[name] + c0
            b = w[bias][:, c0:c0 + width]
            if gate is not None:
                b = b + jnp.concatenate([gate] * (width // LANES), axis=1)
            return _dot(hb, w["wrow"][:, w0:w0 + width]) + b

        last = lambda z: z[-8:, -LANES:]

        def gm_a(c0, gate):
            z = proj("wgm", "bgm", c0, gate)
            ma_s[rows, c0:c0 + 2 * LANES] = _sigmoid(z)
            return last(z)

        def gm_b(c0, gate):
            z = proj("wgm", "bgm", D_MODEL + c0, gate)
            sgb_s[rows, c0:c0 + 2 * LANES] = _sigmoid(z)
            return last(z)

        def q_part(c0, gate):
            z = proj("wq", "bq", c0, gate)
            q_s[rows, c0:c0 + 2 * LANES] = (z * (HEAD_DIM ** -0.5)).astype(BF16)
            return last(z)

        def v_part(c0, gate):
            z = proj("wv", "bv", c0, gate)
            v_s[rows, c0:c0 + 2 * LANES] = z.astype(BF16)
            return last(z)

        def o_part(c0, gate):
            z = proj("wog", "bog", c0, gate)
            so_s[rows, c0:c0 + 2 * LANES] = _sigmoid(z)
            return last(z)

        def k_part(c0, gate):
            rs = slice(c0, c0 + 2 * LANES)
            b = w["bk"][rs, :] if gate is None else w["bk"][rs, :] + gate[:, 0:1]
            z = _dot_nt(w["wkT"][rs, :].astype(BF16), hb) + b
            kt = z.astype(BF16)
            for j in range(cpm):
                kT_s[i * cpm + j, rs, :] = kt[:, j * L:(j + 1) * L]
            return last(z)

        jobs = ([functools.partial(gm_a, c0) for c0 in range(0, D_MODEL, 2 * LANES)]
                + [functools.partial(gm_b, c0) for c0 in range(0, D_MODEL, 2 * LANES)]
                + [functools.partial(f, c0) for f in (q_part, k_part, v_part, o_part)
                   for c0 in range(0, D_MLSTM, 2 * LANES)])
        n_jobs = len(jobs)
        conv = {}
        after = None
        n_pieces = (D_CONV // LANES) * nseg * (P // CONV_RB)
        for cb in range(D_CONV // LANES):
            cs = slice(cb * LANES, (cb + 1) * LANES)
            for seg in range(nseg):
                for rb in range(P // CONV_RB):
                    blk = _conv_block(upad_s, seg, rb * CONV_RB, cs, w["wdw"], w["bdw"], after)
                    conv[(cb, seg, rb)] = blk
                    if jobs and len(conv) * n_jobs >= (n_jobs - len(jobs) + 1) * n_pieces:
                        after = _zero_after(jobs.pop(0)(_zero_after(blk[-8:, :])))
        for job in jobs:
            job(None)
        cu = jnp.concatenate(
            [jnp.concatenate([conv[(cb, seg, rb)] for seg in range(nseg) for rb in range(P // CONV_RB)], axis=0)
             for cb in range(D_CONV // LANES)], axis=1)
        mu = jnp.mean(cu, axis=-1, keepdims=True)
        cc = cu - mu
        cn = cc * lax.rsqrt(jnp.mean(cc * cc, axis=-1, keepdims=True) + EPS) * w["lng"][...] + w["lnb"][...]
        ca = (cn * _sigmoid(cn)).astype(BF16)
        ma_s[rows, :] = ma_s[rows, :] * _dot(ca, w["wco"][...])
        return carry

    if n_mt == 1:
        phase1(0, 0)
    else:
        lax.fori_loop(0, n_mt, phase1, 0)

    ones_col = (lax.broadcasted_iota(jnp.int32, (L, HEAD_DIM), 1) == 0).astype(F32).astype(BF16)
    pad_rows = jnp.zeros((LANES - 3 * N_UNITS, L), F32)

    def gate_prep(c, m_vec):
        sc = scan_s[c]
        a, run_max, bsum = sc[0:N_UNITS], sc[N_UNITS:2 * N_UNITS], sc[2 * N_UNITS:3 * N_UNITS]
        tot, a_max = sc[3 * N_UNITS:4 * N_UNITS, 0:1], sc[4 * N_UNITS:5 * N_UNITS, 0:1]
        big_m = jnp.maximum(m_vec, run_max)
        m_end = jnp.maximum(m_vec, a_max)
        cols = jnp.concatenate(
            [big_m, jnp.exp(m_vec - big_m), jnp.exp(-bsum - big_m), pad_rows], axis=0).T
        return a, cols, jnp.exp(a - m_end), jnp.exp(m_vec - m_end), tot + m_end

    qk_cache = {}

    def unit(d, hd, c, prep, first_chunk, want_state):
        a, cols, wk, decay, _ = prep
        rows = slice(c * L, (c + 1) * L)
        hs = slice(hd * HEAD_DIM, (hd + 1) * HEAD_DIM)
        idx = d * N_HEADS + hd
        col = lambda k: cols[:, k * N_UNITS + idx:k * N_UNITS + idx + 1]
        qc = q_s[rows, hs]
        kTc = kT_s[c, hs, :]
        vaug = jnp.concatenate([v_s[rows, hs], ones_col], axis=1)
        if cps == 1 and (hd, c) in qk_cache:
            qk = qk_cache[(hd, c)]
        else:
            qk = _dot(qc, kTc)
            qk_cache[(hd, c)] = qk
        w_intra = jnp.where(lower if d == 0 else upper, jnp.exp(a[idx:idx + 1, :] - col(0)), 0.0)
        nd = _dot((qk * w_intra).astype(BF16), vaug)
        if has_state or not first_chunk:
            nd = nd + col(1) * _dot(qc, cst_s[idx].astype(BF16))
        den = nd[:, HEAD_DIM:HEAD_DIM + 1]
        h = nd[:, :HEAD_DIM] * (1.0 / jnp.maximum(jnp.abs(den), col(2)))
        if d == 0:
            hm_s[rows, hs] = h
        else:
            hm_s[rows, hs] = hm_s[rows, hs] + h
        if want_state:
            kw = (kTc.astype(F32) * wk[idx:idx + 1, :]).astype(BF16)
            upd = _dot(kw, vaug)
            if has_state or not first_chunk:
                upd = upd + decay[idx:idx + 1, :] * cst_s[idx]
            cst_s[idx] = upd

    dir_rows = lax.broadcasted_iota(jnp.int32, (N_UNITS, 1), 0) >= N_HEADS
    for seq in range(n_seq):
        if has_state:
            n_cols = jnp.concatenate([n0_ref[0], jnp.zeros((LANES - N_UNITS, HEAD_DIM), F32)], axis=0).T
            first_lane = lax.broadcasted_iota(jnp.int32, (HEAD_DIM, HEAD_DIM), 1) == 0
            for idx in range(N_UNITS):
                cst_s[idx, :, :HEAD_DIM] = c0_ref[0, idx]
                cst_s[idx, :, HEAD_DIM:] = jnp.where(first_lane, n_cols[:, idx:idx + 1], 0.0)
            m_vec = m0_ref[0, :, 0:1]
        else:
            m_vec = jnp.zeros((N_UNITS, 1), F32)
        prep = None
        for d in range(2):
            order = list(range(cps)) if d == 0 else list(range(cps - 1, -1, -1))
            for pos, c in enumerate(order):
                if cps > 1 or prep is None:
                    prep = gate_prep(seq * cps + c, m_vec)
                for hd in range(N_HEADS):
                    unit(d, hd, seq * cps + c, prep, pos == 0, emit_state or pos < cps - 1)
                m_vec = jnp.where(dir_rows == (d == 1), prep[4], m_vec)
        if emit_state:
            for idx in range(N_UNITS):
                caug = cst_s[idx]
                cout_ref[0, seq * N_UNITS + idx] = caug[:, :HEAD_DIM]
                nout_ref[0, seq * N_UNITS + idx:seq * N_UNITS + idx + 1, :] = caug[:, HEAD_DIM:].T[0:1, :]
            mout_ref[0, seq * N_UNITS:(seq + 1) * N_UNITS, :] = jnp.broadcast_to(m_vec, (N_UNITS, LANES))

    e_iota = lax.broadcasted_iota(jnp.int32, (LANES, MIX_TM), 0)
    g_of_e = lax.shift_right_logical(e_iota, 2)
    j_of_e = lax.bitwise_and(e_iota, EXPERTS_PER_GROUP - 1)
    r8 = lax.broadcasted_iota(jnp.int32, (8, MIX_TM), 0)
    before_b = (lax.broadcasted_iota(jnp.int32, (MOE_BLK, MOE_BLK), 0)
                < lax.broadcasted_iota(jnp.int32, (MOE_BLK, MOE_BLK), 1)).astype(F32).astype(BF16)

    def phase3(i, carry):
        r0 = pl.multiple_of(i * MIX_TM, MIX_TM)
        rows = pl.ds(r0, MIX_TM)
        hm = hm_s[rows, :]
        heads = []
        for hd in range(N_HEADS):
            hh = hm[:, hd * HEAD_DIM:(hd + 1) * HEAD_DIM]
            heads.append(hh * lax.rsqrt(jnp.mean(hh * hh, axis=-1, keepdims=True) + EPS))
        hn = jnp.concatenate(heads, axis=1) * w["hng"][...]
        hb2 = (so_s[rows, :] * hn).astype(BF16)
        br_b = _dot(hb2, w["wmo"][...])
        mixed = (ma_s[rows, :] + sgb_s[rows, :] * br_b).astype(BF16)
        x1 = x_ref[0, rows, :] + mod_row(2) * _dot(mixed, w["wo"][...])
        x1_ref[0, rows, :] = x1
        xn = x1 * lax.rsqrt(jnp.mean(x1 * x1, axis=-1, keepdims=True) + EPS) * w["g2"][...]
        h2 = xn * (1.0 + mod_row(4)) + mod_row(3)
        h2_ref[0, rows, :] = h2.astype(BF16)

        h2_hi = h2.astype(BF16)
        h2_lo = (h2 - h2_hi.astype(F32)).astype(BF16)
        lg = _dot(h2_hi, w["wrt2"][...])
        lg = lg[:, :LANES] + lg[:, LANES:] + _dot(h2_lo, w["wrt2"][:, :LANES])
        lt = lg.T + w["brtT"][...]
        gl = [lt[N_EXPERTS + g:N_EXPERTS + g + 1, :] for g in range(N_GROUPS)]
        best, gsel = gl[0], jnp.zeros((1, MIX_TM), jnp.int32)
        for g in range(1, N_GROUPS):
            better = gl[g] > best
            gsel = jnp.where(better, g, gsel)
            best = jnp.where(better, gl[g], best)
        gp_sel = 1.0 / sum(jnp.exp(v - best) for v in gl)
        el = []
        for j in range(EXPERTS_PER_GROUP):
            v = lt[j:j + 1, :]
            for g in range(1, N_GROUPS):
                r = g * EXPERTS_PER_GROUP + j
                v = jnp.where(gsel == g, lt[r:r + 1, :], v)
            el.append(v)
        l1, e1 = el[0], jnp.zeros((1, MIX_TM), jnp.int32)
        for j in range(1, EXPERTS_PER_GROUP):
            better = el[j] > l1
            e1 = jnp.where(better, j, e1)
            l1 = jnp.where(better, el[j], l1)
        l2 = jnp.full((1, MIX_TM), -jnp.inf, F32)
        e2 = jnp.zeros((1, MIX_TM), jnp.int32)
        for j in range(EXPERTS_PER_GROUP):
            better = jnp.logical_and(e1 != j, el[j] > l2)
            e2 = jnp.where(better, j, e2)
            l2 = jnp.where(better, el[j], l2)
        r2 = jnp.exp(l2 - l1)
        wt1 = gp_sel / (1.0 + r2)
        wt2 = gp_sel * r2 / (1.0 + r2)
        in_group = g_of_e == gsel
        comb_t = (jnp.where(jnp.logical_and(in_group, j_of_e == e1), wt1, 0.0)
                  + jnp.where(jnp.logical_and(in_group, j_of_e == e2), wt2, 0.0))

        onehot = (r8 == gsel).astype(F32)
        gsel_f = gsel.astype(F32)
        rank = jnp.sum(onehot * _dot(onehot.astype(BF16), before_b), axis=0, keepdims=True)
        r8rows = pl.ds(pl.multiple_of(i * 8, 8), 8)
        route_ref[0, r8rows, :] = jnp.where(r8 == 0, gsel_f, jnp.where(r8 == 1, rank, 0.0))
        cnt_ref[0, r8rows, :] = jnp.broadcast_to(jnp.sum(onehot, axis=1, keepdims=True), (8, LANES))
        comb_t = jnp.where(e_iota == ROUTE_GROUP_LANE, gsel_f,
                           jnp.where(e_iota == ROUTE_RANK_LANE, rank, comb_t))
        comb_ref[0, rows, :] = comb_t.T
        return carry

    if n_mt == 1:
        phase3(0, 0)
    else:
        lax.fori_loop(0, n_mt, phase3, 0)


class _RowWindow(NamedTuple):
    array: jax.Array
    start: int
    n: int


def _const_spec(a):
    if isinstance(a, _RowWindow):
        assert a.start % a.n == 0
        return a.array, pl.BlockSpec((a.n, a.array.shape[1]), lambda b: (a.start // a.n, 0),
                                     pipeline_mode=pl.Buffered(1))
    nd = a.ndim
    return a, pl.BlockSpec(a.shape, lambda b, _nd=nd: (0,) * _nd, pipeline_mode=pl.Buffered(1))


def _mixer(x, T, mod, mod_index, weights, P, state=None, emit_state=False):
    B, R, _ = x.shape
    n_chunks = R // SUB
    n_blk = R // MOE_BLK
    n_seq = R // T
    has_state = state is not None
    seq_mode = {} if R <= MIX_TM else {"pipeline_mode": pl.Buffered(1)}
    in_specs = [
        pl.BlockSpec((1, R, D_MODEL), lambda b: (b, 0, 0), **seq_mode),
        pl.BlockSpec(mod.shape, lambda b: (0, 0, 0)),
    ]
    args = [x, mod]
    if has_state:
        c0, n0, m0 = state
        in_specs += [
            pl.BlockSpec((1, N_UNITS, HEAD_DIM, HEAD_DIM), lambda b: (b, 0, 0, 0)),
            pl.BlockSpec((1, N_UNITS, HEAD_DIM), lambda b: (b, 0, 0)),
            pl.BlockSpec((1, N_UNITS, LANES), lambda b: (b, 0, 0)),
        ]
        args += [c0, n0, m0]
    for name in _MIXER_WEIGHTS:
        operand, spec = _const_spec(weights[name])
        in_specs.append(spec)
        args.append(operand)
    out_shape = [
        jax.ShapeDtypeStruct((B, R, D_MODEL), F32),
        jax.ShapeDtypeStruct((B, R, D_MODEL), BF16),
        jax.ShapeDtypeStruct((B, R, LANES), F32),
        jax.ShapeDtypeStruct((B, n_blk * 8, MOE_BLK), F32),
        jax.ShapeDtypeStruct((B, n_blk * 8, LANES), F32),
    ]
    out_specs = [
        pl.BlockSpec((1, R, D_MODEL), lambda b: (b, 0, 0), **seq_mode),
        pl.BlockSpec((1, R, D_MODEL), lambda b: (b, 0, 0), **seq_mode),
        pl.BlockSpec((1, R, LANES), lambda b: (b, 0, 0)),
        pl.BlockSpec((1, n_blk * 8, MOE_BLK), lambda b: (b, 0, 0)),
        pl.BlockSpec((1, n_blk * 8, LANES), lambda b: (b, 0, 0)),
    ]
    if emit_state:
        out_shape += [
            jax.ShapeDtypeStruct((B, n_seq * N_UNITS, HEAD_DIM, HEAD_DIM), F32),
            jax.ShapeDtypeStruct((B, n_seq * N_UNITS, HEAD_DIM), F32),
            jax.ShapeDtypeStruct((B, n_seq * N_UNITS, LANES), F32),
        ]
        out_specs += [
            pl.BlockSpec((1, n_seq * N_UNITS, HEAD_DIM, HEAD_DIM), lambda b: (b, 0, 0, 0)),
            pl.BlockSpec((1, n_seq * N_UNITS, HEAD_DIM), lambda b: (b, 0, 0)),
            pl.BlockSpec((1, n_seq * N_UNITS, LANES), lambda b: (b, 0, 0)),
        ]
    scratch = [
        pltpu.VMEM((R, D_MLSTM), BF16),
        pltpu.VMEM((n_chunks, D_MLSTM, SUB), BF16),
        pltpu.VMEM((R, D_MLSTM), BF16),
        pltpu.VMEM((R, D_MLSTM), F32),
        pltpu.VMEM((n_chunks, 5 * N_UNITS, SUB), F32),
        pltpu.VMEM((R, D_MODEL), F32),
        pltpu.VMEM((R, D_MODEL), F32),
        pltpu.VMEM((R, D_MLSTM), F32),
        pltpu.VMEM((N_UNITS, HEAD_DIM, 2 * HEAD_DIM), F32),
        pltpu.VMEM((MIX_TM // P, P + 2 * CONV_PAD, D_CONV), F32),
    ]
    return pl.pallas_call(
        functools.partial(_mixer_kernel, R, T, P, has_state, emit_state, mod_index),
        grid=(B,),
        in_specs=in_specs,
        out_specs=out_specs,
        out_shape=out_shape,
        scratch_shapes=scratch,
        compiler_params=pltpu.CompilerParams(
            dimension_semantics=("arbitrary",), vmem_limit_bytes=VMEM_LIMIT),
        name="mixer_T%d" % T,
    )(*args)


def _dest_in_block(group, rank, starts):
    dest = rank
    for g in range(N_GROUPS):
        dest = dest + jnp.where(group == float(g), starts[g], 0.0)
    return dest


def _copy_segments(src_refs, dst_refs, src_starts, dst_starts, n_pieces):
    for g in range(N_GROUPS):
        def body(k, carry, g=g):
            s = pl.multiple_of(src_starts[g] + k * ROW_ALIGN, ROW_ALIGN)
            d = pl.multiple_of(dst_starts[g] + k * ROW_ALIGN, ROW_ALIGN)
            for src, dst in zip(src_refs, dst_refs):
                dst[pl.ds(d, ROW_ALIGN), :] = src[pl.ds(s, ROW_ALIGN), :]
            return carry
        lax.fori_loop(0, n_pieces[g], body, 0)


def _plan_segments(n_blocks, n_tiles, cnt_ref, start_ref, npiece_ref, off_ref, tgroup_ref, tvalid_ref, tfirst_ref):
    align_shift = ROW_ALIGN.bit_length() - 1
    tile_shift = MOE_TM.bit_length() - 1

    def block_starts(blk, carry):
        row = jnp.int32(0)
        for g in range(N_GROUPS):
            n = lax.shift_right_logical(cnt_ref[blk * N_GROUPS + g] + (ROW_ALIGN - 1), align_shift)
            npiece_ref[blk * N_GROUPS + g] = n
            start_ref[blk * N_GROUPS + g] = row
            row = row + n * ROW_ALIGN
        return carry

    lax.fori_loop(0, n_blocks, block_starts, 0)

    base_row = jnp.int32(0)
    base_tile = jnp.int32(0)
    last_group = jnp.int32(0)
    for g in range(N_GROUPS):
        def seg_offsets(blk, row, g=g, base_row=base_row):
            off_ref[blk * N_GROUPS + g] = base_row + row
            return row + npiece_ref[blk * N_GROUPS + g] * ROW_ALIGN

        rows = lax.fori_loop(0, n_blocks, seg_offsets, jnp.int32(0))
        tiles = lax.shift_right_logical(rows + (MOE_TM - 1), tile_shift)

        def mark_tiles(t, carry, g=g, base_tile=base_tile):
            tgroup_ref[base_tile + t] = g
            tvalid_ref[base_tile + t] = 1
            tfirst_ref[base_tile + t] = (t == 0).astype(jnp.int32)
            return carry

        lax.fori_loop(0, tiles, mark_tiles, 0)
        last_group = jnp.where(tiles > 0, g, last_group)
        base_row = base_row + tiles * MOE_TM
        base_tile = base_tile + tiles

    def mark_unused(t, carry):
        tgroup_ref[t] = last_group
        tvalid_ref[t] = 0
        tfirst_ref[t] = 0
        return carry

    lax.fori_loop(base_tile, n_tiles, mark_unused, 0)


def _dispatch_kernel(n_ctx_blocks, n_blocks, n_tiles, cnt_ref,
                     h2c_ref, h2l_ref, cbc_ref, cbl_ref, rtc_ref, rtl_ref,
                     xs_ref, cs_ref, start_ref, npiece_ref, off_ref, tgroup_ref, tvalid_ref, tfirst_ref,
                     sx_s, sc_s):
    b = pl.program_id(0)
    is_ctx = b < n_ctx_blocks

    @pl.when(b == 0)
    def _():
        _plan_segments(n_blocks, n_tiles, cnt_ref, start_ref, npiece_ref, off_ref,
                       tgroup_ref, tvalid_ref, tfirst_ref)
        xs_ref[...] = jnp.zeros_like(xs_ref)
        cs_ref[...] = jnp.zeros_like(cs_ref)

    h2 = jnp.where(is_ctx, h2c_ref[0], h2l_ref[0])
    cb = jnp.where(is_ctx, cbc_ref[0], cbl_ref[0])
    rt = jnp.where(is_ctx, rtc_ref[0], rtl_ref[0])
    starts = [start_ref[b * N_GROUPS + g] for g in range(N_GROUPS)]
    dest = _dest_in_block(rt[0:1, :], rt[1:2, :], [s.astype(F32) for s in starts])
    row = lax.broadcasted_iota(jnp.int32, (SORT_ROWS, MOE_BLK), 0).astype(F32)
    perm = (row == dest).astype(F32).astype(BF16)
    cb_hi = cb.astype(BF16)
    cb_lo = (cb - cb_hi.astype(F32)).astype(BF16)
    sx_s[...] = _dot(perm, h2).astype(BF16)
    sc_s[...] = _dot(perm, jnp.concatenate([cb_hi, cb_lo], axis=1)).astype(BF16)
    _copy_segments((sx_s, sc_s), (xs_ref, cs_ref), starts,
                   [off_ref[b * N_GROUPS + g] for g in range(N_GROUPS)],
                   [npiece_ref[b * N_GROUPS + g] for g in range(N_GROUPS)])


def _experts_kernel(tgroup_ref, tvalid_ref, tfirst_ref, xs_ref, cs_ref, wg_ref, wu_ref, wd_ref, ys_ref,
                    wg_s, wu_s, wd_s):
    i = pl.program_id(0)

    @pl.when(tfirst_ref[i] == 1)
    def _():
        for j in range(EXPERTS_PER_GROUP):
            cols = slice(j * D_EXPERT, (j + 1) * D_EXPERT)
            wg_s[:, cols] = wg_ref[j].astype(BF16)
            wu_s[:, cols] = wu_ref[j].astype(BF16)
            wd_s[cols, :] = wd_ref[j].astype(BF16)

    @pl.when(tvalid_ref[i] == 1)
    def _():
        x = xs_ref[...]
        g = _dot(x, wg_s[...])
        u = _dot(x, wu_s[...])
        comb = cs_ref[:, :LANES].astype(F32) + cs_ref[:, LANES:].astype(F32)
        lane = lax.broadcasted_iota(jnp.int32, comb.shape, 1)
        first = tgroup_ref[i] * EXPERTS_PER_GROUP
        parts = []
        for j in range(EXPERTS_PER_GROUP):
            cols = slice(j * D_EXPERT, (j + 1) * D_EXPERT)
            cw = jnp.sum(jnp.where(lane == first + j, comb, 0.0), axis=1, keepdims=True)
            gj = g[:, cols]
            parts.append((gj * _sigmoid(gj) * u[:, cols] * cw).astype(BF16))
        ys_ref[...] = _dot(jnp.concatenate(parts, axis=1), wd_s[...]).astype(BF16)

    @pl.when(tvalid_ref[i] == 0)
    def _():
        ys_ref[...] = jnp.zeros_like(ys_ref)


def _combine_kernel(n_ctx_blocks, blocks_per_lat_seq, start_ref, npiece_ref, off_ref,
                    x1c_ref, x1l_ref, cbc_ref, cbl_ref, ys_ref, mod_ref, gf_ref, yc_ref, yl_ref, loc_s):
    b = pl.program_id(0)
    is_ctx = b < n_ctx_blocks
    starts = [start_ref[b * N_GROUPS + g] for g in range(N_GROUPS)]
    loc_s[...] = jnp.zeros_like(loc_s)
    _copy_segments((ys_ref,), (loc_s,), [off_ref[b * N_GROUPS + g] for g in range(N_GROUPS)], starts,
                   [npiece_ref[b * N_GROUPS + g] for g in range(N_GROUPS)])
    cb = jnp.where(is_ctx, cbc_ref[0], cbl_ref[0])
    dest = _dest_in_block(cb[:, ROUTE_GROUP_LANE:ROUTE_GROUP_LANE + 1],
                          cb[:, ROUTE_RANK_LANE:ROUTE_RANK_LANE + 1],
                          [s.astype(F32) for s in starts])
    col = lax.broadcasted_iota(jnp.int32, (MOE_BLK, SORT_ROWS), 1).astype(F32)
    unperm = (col == dest).astype(F32).astype(BF16)
    moe = _dot(unperm, loc_s[...])
    x1 = jnp.where(is_ctx, x1c_ref[0], x1l_ref[0])
    mrow = jnp.where(is_ctx, 0, 1 + jnp.maximum(b - n_ctx_blocks, 0) // blocks_per_lat_seq)
    x2 = x1 + mod_ref[N_ADA - 1, pl.ds(mrow, 1), :] * moe
    y = x2 * lax.rsqrt(jnp.mean(x2 * x2, axis=-1, keepdims=True) + EPS) * gf_ref[...]

    @pl.when(is_ctx)
    def _():
        yc_ref[0] = y

    @pl.when(jnp.logical_not(is_ctx))
    def _():
        yl_ref[0] = y


def _moe(x1c, x1l, h2c, h2l, cbc, cbl, rtc, rtl, cnt, mod, blocks_per_lat_seq, wg, wu, wd, gf):
    nc, nl = x1c.shape[0], x1l.shape[0]
    nb = nc + nl
    n_rows_max = nb * MOE_BLK + nb * N_GROUPS * (ROW_ALIGN - 1) + N_GROUPS * (MOE_TM - ROW_ALIGN)
    n_tiles = -(-n_rows_max // MOE_TM)
    ns = n_tiles * MOE_TM

    cmap = lambda b, *_: (jnp.minimum(b, nc - 1), 0, 0)
    lmap = lambda b, *_: (jnp.maximum(b - nc, 0), 0, 0)
    whole = lambda *_: (0, 0)
    once = {"pipeline_mode": pl.Buffered(1)}
    arb = pltpu.CompilerParams(dimension_semantics=("arbitrary",), vmem_limit_bytes=VMEM_LIMIT)
    smem = pl.BlockSpec(memory_space=pltpu.SMEM)
    seg_i32 = jax.ShapeDtypeStruct((nb * N_GROUPS,), jnp.int32)
    tile_i32 = jax.ShapeDtypeStruct((n_tiles,), jnp.int32)

    xs, cs, start, npiece, off, tgroup, tvalid, tfirst = pl.pallas_call(
        functools.partial(_dispatch_kernel, nc, nb, n_tiles),
        grid_spec=pltpu.PrefetchScalarGridSpec(
            num_scalar_prefetch=1, grid=(nb,),
            in_specs=[
                pl.BlockSpec((1, MOE_BLK, D_MODEL), cmap), pl.BlockSpec((1, MOE_BLK, D_MODEL), lmap),
                pl.BlockSpec((1, MOE_BLK, LANES), cmap), pl.BlockSpec((1, MOE_BLK, LANES), lmap),
                pl.BlockSpec((1, 8, MOE_BLK), cmap), pl.BlockSpec((1, 8, MOE_BLK), lmap),
            ],
            out_specs=[pl.BlockSpec((ns, D_MODEL), whole, **once), pl.BlockSpec((ns, 2 * LANES), whole, **once),
                       smem, smem, smem, smem, smem, smem],
            scratch_shapes=[pltpu.VMEM((SORT_ROWS, D_MODEL), BF16), pltpu.VMEM((SORT_ROWS, 2 * LANES), BF16)],
        ),
        out_shape=[jax.ShapeDtypeStruct((ns, D_MODEL), BF16), jax.ShapeDtypeStruct((ns, 2 * LANES), BF16),
                   seg_i32, seg_i32, seg_i32, tile_i32, tile_i32, tile_i32],
        compiler_params=arb,
        name="moe_dispatch",
    )(cnt, h2c, h2l, cbc, cbl, rtc, rtl)

    wmap = lambda i, tg, tv, tf: (tg[i], 0, 0)
    ys = pl.pallas_call(
        _experts_kernel,
        grid_spec=pltpu.PrefetchScalarGridSpec(
            num_scalar_prefetch=3, grid=(n_tiles,),
            in_specs=[
                pl.BlockSpec((MOE_TM, D_MODEL), lambda i, *_: (i, 0)),
                pl.BlockSpec((MOE_TM, 2 * LANES), lambda i, *_: (i, 0)),
                pl.BlockSpec((EXPERTS_PER_GROUP, D_MODEL, D_EXPERT), wmap),
                pl.BlockSpec((EXPERTS_PER_GROUP, D_MODEL, D_EXPERT), wmap),
                pl.BlockSpec((EXPERTS_PER_GROUP, D_EXPERT, D_MODEL), wmap),
            ],
            out_specs=pl.BlockSpec((MOE_TM, D_MODEL), lambda i, *_: (i, 0)),
            scratch_shapes=[pltpu.VMEM((D_MODEL, EXPERTS_PER_GROUP * D_EXPERT), BF16),
                            pltpu.VMEM((D_MODEL, EXPERTS_PER_GROUP * D_EXPERT), BF16),
                            pltpu.VMEM((EXPERTS_PER_GROUP * D_EXPERT, D_MODEL), BF16)],
        ),
        out_shape=jax.ShapeDtypeStruct((ns, D_MODEL), BF16),
        compiler_params=arb,
        name="moe_experts",
    )(tgroup, tvalid, tfirst, xs, cs, wg, wu, wd)

    yc, yl = pl.pallas_call(
        functools.partial(_combine_kernel, nc, blocks_per_lat_seq),
        grid_spec=pltpu.PrefetchScalarGridSpec(
            num_scalar_prefetch=3, grid=(nb,),
            in_specs=[
                pl.BlockSpec((1, MOE_BLK, D_MODEL), cmap), pl.BlockSpec((1, MOE_BLK, D_MODEL), lmap),
                pl.BlockSpec((1, MOE_BLK, LANES), cmap), pl.BlockSpec((1, MOE_BLK, LANES), lmap),
                pl.BlockSpec((ns, D_MODEL), whole, **once),
                pl.BlockSpec(mod.shape, lambda *_: (0, 0, 0)),
                pl.BlockSpec((1, D_MODEL), whole),
            ],
            out_specs=[pl.BlockSpec((1, MOE_BLK, D_MODEL), cmap), pl.BlockSpec((1, MOE_BLK, D_MODEL), lmap)],
            scratch_shapes=[pltpu.VMEM((SORT_ROWS, D_MODEL), BF16)],
        ),
        out_shape=[jax.ShapeDtypeStruct((nc, MOE_BLK, D_MODEL), F32),
                   jax.ShapeDtypeStruct((nl, MOE_BLK, D_MODEL), F32)],
        compiler_params=arb,
        name="moe_combine",
    )(start, npiece, off, x1c, x1l, cbc, cbl, ys, mod, gf)
    return yc, yl


def _prep_weights(norm1_g, w_in, b_in, b_gates, w_dw, b_dw, conv_ln_g, conv_ln_b, w_conv_out,
                  mlstm_hn_g, w_mlstm_out, w_o, norm2_g, w_rg, b_rg, w_re, b_re):
    s_a = 2 * D_CONV
    s_q = s_a + D_MLSTM
    s_k = s_q + D_MLSTM
    s_v = s_k + D_MLSTM
    s_o = s_v + D_MLSTM
    s_g = s_o + 4 * N_HEADS
    row = lambda v: v.reshape(1, -1).astype(F32)
    w_t = w_in.T
    keep = [(0, s_q), (s_k, s_o), (s_g, w_in.shape[1])]
    wrow = _transpose_cast(w_t, [r for a, b in keep for r in range(a, b, WPREP_ROWS)])
    bg = (b_in[s_o:s_g] + b_gates.reshape(-1)).reshape(2, 2, N_HEADS).transpose(1, 0, 2).reshape(-1, 1)
    row_window = lambda start, n: _RowWindow(w_t, start, n)
    n_rt = N_EXPERTS + N_GROUPS
    wrt = jnp.pad(jnp.concatenate([w_re, w_rg], axis=1), ((0, 0), (0, LANES - n_rt)))
    wrt_hi = wrt.astype(BF16)
    wrt2 = jnp.concatenate([wrt_hi, (wrt - wrt_hi.astype(F32)).astype(BF16)], axis=1)
    brtT = jnp.pad(jnp.concatenate([b_re, b_rg]), (0, LANES - n_rt)).reshape(LANES, 1)
    return {
        "g1": row(norm1_g),
        "wrow": wrow, "bag": row(b_in[:s_a]), "bq": row(b_in[s_a:s_q]),
        "wkT": row_window(s_q, D_MLSTM), "bk": b_in[s_q:s_k].reshape(-1, 1),
        "bv": row(b_in[s_k:s_v]), "bog": row(b_in[s_v:s_o]),
        "wgifT": row_window(s_o, 4 * N_HEADS), "bgifT": bg, "bgm": row(b_in[s_g:]),
        "wdw": w_dw.astype(F32), "bdw": row(b_dw), "lng": row(conv_ln_g), "lnb": row(conv_ln_b),
        "wco": w_conv_out.astype(BF16), "hng": row(mlstm_hn_g), "wmo": w_mlstm_out.astype(BF16),
        "wo": w_o.astype(BF16), "g2": row(norm2_g), "wrt2": wrt2, "brtT": brtT,
    }


def kernel(x_prompt, x_sample, state_C, state_n, state_m, c, c_ctx, norm1_g, w_ada, b_ada, w_in, b_in, b_gates, w_dw, b_dw, conv_ln_g, conv_ln_b, w_conv_out, mlstm_hn_g, w_mlstm_out, w_o, norm2_g, w_rg, b_rg, w_re, b_re, w_e_gate, w_e_up, w_e_down, norm_final_g):
    B, S, _ = x_prompt.shape
    Bd, Sd, _ = x_sample.shape
    assert w_ada.shape[0] == 1, "single trunk layer"
    assert S == SUB and Sd % SUB == 0

    cin = jnp.concatenate([c_ctx[None, :], c, jnp.zeros((8 - 1 - Bd, D_MODEL), F32)], axis=0)
    mod = _ada(cin, w_ada[0], b_ada[0].reshape(1, -1))

    wts = _prep_weights(norm1_g[0], w_in[0], b_in[0], b_gates[0], w_dw[0], b_dw[0], conv_ln_g[0],
                        conv_ln_b[0], w_conv_out[0], mlstm_hn_g[0], w_mlstm_out[0], w_o[0],
                        norm2_g[0], w_rg[0], b_rg[0], w_re[0], b_re[0])

    x1p, h2p, cbp, rtp, cntp, c_new, n_new, m_new = _mixer(
        x_prompt.reshape(B * S // MIX_TM, MIX_TM, D_MODEL), S, mod, lambda b: 0, wts, P=S, emit_state=True)

    m0 = jnp.broadcast_to(state_m[:, 0].reshape(Bd, N_UNITS, 1), (Bd, N_UNITS, LANES))
    state = (state_C[:, 0].reshape(Bd, N_UNITS, HEAD_DIM, HEAD_DIM), state_n[:, 0].reshape(Bd, N_UNITS, HEAD_DIM), m0)
    x1s, h2s, cbs, rts, cnts = _mixer(x_sample, Sd, mod, lambda b: 1 + b, wts, P=GRID_W, state=state)

    nc, nl = B * S // MOE_BLK, Bd * Sd // MOE_BLK
    blk = lambda a, n: a.reshape(n, MOE_BLK, a.shape[-1])
    cnt = jnp.concatenate([cntp.reshape(nc, 8, LANES)[:, :N_GROUPS, 0],
                           cnts.reshape(nl, 8, LANES)[:, :N_GROUPS, 0]], axis=0)
    yp, ys = _moe(blk(x1p, nc), blk(x1s, nl), blk(h2p, nc), blk(h2s, nl), blk(cbp, nc), blk(cbs, nl),
                  rtp.reshape(nc, 8, MOE_BLK), rts.reshape(nl, 8, MOE_BLK),
                  cnt.astype(jnp.int32).reshape(-1), mod, Sd // MOE_BLK, w_e_gate[0], w_e_up[0], w_e_down[0],
                  norm_final_g.reshape(1, -1))

    return (yp.reshape(B, S, D_MODEL), ys.reshape(Bd, Sd, D_MODEL),
            c_new.reshape(B, 1, 2, N_HEADS, HEAD_DIM, HEAD_DIM),
            n_new.reshape(B, 1, 2, N_HEADS, HEAD_DIM),
            m_new[:, :, 0].reshape(B, 1, 2, N_HEADS))
```

```python
import functools
from typing import NamedTuple

import jax
import jax.numpy as jnp
from jax import lax
from jax.experimental import pallas as pl
from jax.experimental.pallas import tpu as pltpu

D_MODEL = 1024
D_CONV = 512
CONV_K = 31
D_MLSTM = 512
N_HEADS = 4
HEAD_DIM = D_MLSTM // N_HEADS
N_GROUPS = 4
EXPERTS_PER_GROUP = 4
N_EXPERTS = N_GROUPS * EXPERTS_PER_GROUP
D_EXPERT = 256
N_ADA = 6
EPS = 1e-6
GRID_W = 64

LANES = 128
SUB = 256
CONV_PAD = 16
CONV_RB = 64
N_UNITS = 2 * N_HEADS
ROW_ALIGN = 16
MOE_TM = 512
MIX_TM = 512
MOE_BLK = MIX_TM
SORT_ROWS = MOE_BLK + N_GROUPS * ROW_ALIGN
WPREP_ROWS = 512
ROUTE_GROUP_LANE = N_EXPERTS
ROUTE_RANK_LANE = N_EXPERTS + 1
VMEM_LIMIT = 58 * 1024 * 1024

BF16 = jnp.bfloat16
F32 = jnp.float32
NT_DIMS = (((1,), (1,)), ((), ()))


def _dot(a, b):
    return jnp.dot(a, b, preferred_element_type=F32)


def _dot_nt(a, b, precision=None):
    return lax.dot_general(a, b, NT_DIMS, preferred_element_type=F32, precision=precision)


def _sigmoid(x):
    return 0.5 * jnp.tanh(0.5 * x) + 0.5


def _log_sigmoid(x):
    return jnp.minimum(x, 0.0) - jnp.log1p(jnp.exp(-jnp.abs(x)))


def _split3(x):
    hi = x.astype(BF16).astype(F32)
    r1 = x - hi
    mid = r1.astype(BF16).astype(F32)
    lo = (r1 - mid).astype(BF16).astype(F32)
    return hi, mid, lo


def _ada_kernel(c_ref, w_ref, b_ref, o_ref):
    c = c_ref[...]
    s = (c * _sigmoid(c)).astype(BF16)
    o_ref[0] = _dot(s, w_ref[...].astype(BF16)) + b_ref[...]


def _ada(cin, w_ada, b_ada):
    return pl.pallas_call(
        _ada_kernel,
        grid=(N_ADA,),
        in_specs=[
            pl.BlockSpec((8, D_MODEL), lambda j: (0, 0)),
            pl.BlockSpec((D_MODEL, D_MODEL), lambda j: (0, j)),
            pl.BlockSpec((1, D_MODEL), lambda j: (0, j)),
        ],
        out_specs=pl.BlockSpec((1, 8, D_MODEL), lambda j: (j, 0, 0)),
        out_shape=jax.ShapeDtypeStruct((N_ADA, 8, D_MODEL), F32),
        compiler_params=pltpu.CompilerParams(dimension_semantics=("arbitrary",)),
        name="ada",
    )(cin, w_ada, b_ada)


def _transpose_cast_kernel(starts_ref, wt_ref, o_ref):
    o_ref[...] = wt_ref[...].astype(BF16).T


def _transpose_cast(w_t, row_starts):
    n, k = len(row_starts), w_t.shape[1]
    return pl.pallas_call(
        _transpose_cast_kernel,
        grid_spec=pltpu.PrefetchScalarGridSpec(
            num_scalar_prefetch=1, grid=(n,),
            in_specs=[pl.BlockSpec((pl.Element(WPREP_ROWS), pl.Element(k)), lambda j, starts: (starts[j] * 8, 0))],
            out_specs=pl.BlockSpec((k, WPREP_ROWS), lambda j, starts: (0, j)),
        ),
        out_shape=jax.ShapeDtypeStruct((k, n * WPREP_ROWS), BF16),
        compiler_params=pltpu.CompilerParams(dimension_semantics=("arbitrary",)),
        name="transpose_cast",
    )(jnp.array([r // 8 for r in row_starts], jnp.int32), w_t)


WROW_OFFSET = {"wq": 2 * D_CONV, "wv": 2 * D_CONV + D_MLSTM, "wog": 2 * D_CONV + 2 * D_MLSTM,
               "wgm": 2 * D_CONV + 3 * D_MLSTM}

_MIXER_WEIGHTS = (
    "g1", "wrow", "bag", "bq", "wkT", "bk", "bv", "bog",
    "wgifT", "bgifT", "bgm", "wdw", "bdw", "lng", "lnb",
    "wco", "hng", "wmo", "wo", "g2", "wrt2", "brtT",
)


def _zero_after(x):
    bits = lax.bitcast_convert_type(x, jnp.uint32)
    bits = lax.shift_right_logical(lax.shift_right_logical(bits, jnp.uint32(16)), jnp.uint32(16))
    return lax.bitcast_convert_type(bits, F32)[0:1, :]


def _conv_block(upad_s, seg, base, cs, wdw_ref, bdw_ref, after=None):
    sub = 8
    first = CONV_PAD - CONV_K // 2
    acc = jnp.broadcast_to(bdw_ref[0:1, cs], (CONV_RB, LANES))
    for r in range(sub):
        z = None
        for a in range((CONV_K + first + sub - 1) // sub):
            j = sub * a + r - first
            if 0 <= j < CONV_K:
                lo = base + sub * a
                tap = wdw_ref[j:j + 1, cs] if after is None else wdw_ref[j:j + 1, cs] + after
                term = tap * upad_s[seg, lo:lo + CONV_RB + sub, cs]
                z = term if z is None else z + term
        acc = acc + z[r:r + CONV_RB, :]
    return acc


def _mixer_kernel(R, T, P, has_state, emit_state, mod_index, *refs):
    L = SUB
    n_mt = R // MIX_TM
    cpm = MIX_TM // L
    n_seq = R // T
    cps = T // L
    nseg = MIX_TM // P
    assert not has_state or n_seq == 1
    it = iter(refs)
    x_ref = next(it)
    mod_ref = next(it)
    if has_state:
        c0_ref = next(it)
        n0_ref = next(it)
        m0_ref = next(it)
    w = {name: next(it) for name in _MIXER_WEIGHTS}
    x1_ref = next(it)
    h2_ref = next(it)
    comb_ref = next(it)
    route_ref = next(it)
    cnt_ref = next(it)
    if emit_state:
        cout_ref = next(it)
        nout_ref = next(it)
        mout_ref = next(it)
    (q_s, kT_s, v_s, so_s, scan_s, ma_s, sgb_s, hm_s, cst_s, upad_s) = [next(it) for _ in range(10)]

    cond_row = mod_index(pl.program_id(0))

    def mod_row(i):
        return mod_ref[i, pl.ds(cond_row, 1), :]

    zpad = jnp.zeros((CONV_PAD, D_CONV), F32)
    for seg in range(nseg):
        upad_s[seg, 0:CONV_PAD, :] = zpad
        upad_s[seg, CONV_PAD + P:CONV_PAD + P + CONV_PAD, :] = zpad

    t_idx = lax.broadcasted_iota(jnp.int32, (L, L), 0)
    s_idx = lax.broadcasted_iota(jnp.int32, (L, L), 1)
    lower = s_idx <= t_idx
    upper = s_idx >= t_idx
    triu_b = upper.astype(F32).astype(BF16)
    lane_u = lax.broadcasted_iota(jnp.int32, (N_UNITS, L), 1)
    is_bwd = lax.broadcasted_iota(jnp.int32, (N_UNITS, L), 0) >= N_HEADS

    def gate_scan(g):
        gi, lf = g[:N_UNITS], _log_sigmoid(g[N_UNITS:])
        pr = _dot(jnp.concatenate(_split3(lf), axis=0).astype(BF16), triu_b)
        pre = pr[0:N_UNITS] + pr[N_UNITS:2 * N_UNITS] + pr[2 * N_UNITS:]
        tot = pre[:, L - 1:L]
        bsum = jnp.where(is_bwd, tot - pre + lf, pre)
        a = gi - bsum
        pm, sm, k = a, a, 1
        while k < L:
            pm = jnp.where(lane_u >= k, jnp.maximum(pm, pltpu.roll(pm, k, axis=1)), pm)
            sm = jnp.where(lane_u < L - k, jnp.maximum(sm, pltpu.roll(sm, L - k, axis=1)), sm)
            k *= 2
        wide = lambda v: jnp.broadcast_to(v, (N_UNITS, L))
        return jnp.concatenate([a, jnp.where(is_bwd, sm, pm), bsum, wide(tot),
                                wide(jnp.max(a, axis=1, keepdims=True))], axis=0)

    def phase1(i, carry):
        r0 = pl.multiple_of(i * MIX_TM, MIX_TM)
        rows = pl.ds(r0, MIX_TM)
        x = x_ref[0, rows, :]
        xn = x * lax.rsqrt(jnp.mean(x * x, axis=-1, keepdims=True) + EPS) * w["g1"][...]
        hb = (xn * (1.0 + mod_row(1)) + mod_row(0)).astype(BF16)

        gates = _dot_nt(w["wgifT"][...].astype(BF16), hb)
        gates = jnp.concatenate([gates[d * 2 * N_HEADS + g * N_HEADS:d * 2 * N_HEADS + (g + 1) * N_HEADS]
                                 for g in range(2) for d in range(2)], axis=0) + w["bgifT"][...]
        for j in range(cpm):
            scan_s[i * cpm + j] = gate_scan(gates[:, j * L:(j + 1) * L])
        ag = _dot(hb, w["wrow"][:, :2 * D_CONV]) + w["bag"][...]
        u = ag[:, :D_CONV] * _sigmoid(ag[:, D_CONV:])
        for seg in range(nseg):
            upad_s[seg, CONV_PAD:CONV_PAD + P, :] = u[seg * P:(seg + 1) * P, :]

        def proj(name, bias, c0, gate, width=2 * LANES):
            w0 = WROW_OFFSET[name] + c0
            b = w[bias][:, c0:c0 + width]
            if gate is not None:
                b = b + jnp.concatenate([gate] * (width // LANES), axis=1)
            return _dot(hb, w["wrow"][:, w0:w0 + width]) + b

        last = lambda z: z[-8:, -LANES:]

        def gm_a(c0, gate):
            z = proj("wgm", "bgm", c0, gate)
            ma_s[rows, c0:c0 + 2 * LANES] = _sigmoid(z)
            return last(z)

        def gm_b(c0, gate):
            z = proj("wgm", "bgm", D_MODEL + c0, gate)
            sgb_s[rows, c0:c0 + 2 * LANES] = _sigmoid(z)
            return last(z)

        def q_part(c0, gate):
            z = proj("wq", "bq", c0, gate)
            q_s[rows, c0:c0 + 2 * LANES] = (z * (HEAD_DIM ** -0.5)).astype(BF16)
            return last(z)

        def v_part(c0, gate):
            z = proj("wv", "bv", c0, gate)
            v_s[rows, c0:c0 + 2 * LANES] = z.astype(BF16)
            return last(z)

        def o_part(c0, gate):
            z = proj("wog", "bog", c0, gate)
            so_s[rows, c0:c0 + 2 * LANES] = _sigmoid(z)
            return last(z)

        def k_part(c0, gate):
            rs = slice(c0, c0 + 2 * LANES)
            b = w["bk"][rs, :] if gate is None else w["bk"][rs, :] + gate[:, 0:1]
            z = _dot_nt(w["wkT"][rs, :].astype(BF16), hb) + b
            kt = z.astype(BF16)
            for j in range(cpm):
                kT_s[i * cpm + j, rs, :] = kt[:, j * L:(j + 1) * L]
            return last(z)

        jobs = ([functools.partial(gm_a, c0) for c0 in range(0, D_MODEL, 2 * LANES)]
                + [functools.partial(gm_b, c0) for c0 in range(0, D_MODEL, 2 * LANES)]
                + [functools.partial(f, c0) for f in (q_part, k_part, v_part, o_part)
                   for c0 in range(0, D_MLSTM, 2 * LANES)])
        n_jobs = len(jobs)
        conv = {}
        after = None
        n_pieces = (D_CONV // LANES) * nseg * (P // CONV_RB)
        for cb in range(D_CONV // LANES):
            cs = slice(cb * LANES, (cb + 1) * LANES)
            for seg in range(nseg):
                for rb in range(P // CONV_RB):
                    blk = _conv_block(upad_s, seg, rb * CONV_RB, cs, w["wdw"], w["bdw"], after)
                    conv[(cb, seg, rb)] = blk
                    if jobs and len(conv) * n_jobs >= (n_jobs - len(jobs) + 1) * n_pieces:
                        after = _zero_after(jobs.pop(0)(_zero_after(blk[-8:, :])))
        for job in jobs:
            job(None)
        cu = jnp.concatenate(
            [jnp.concatenate([conv[(cb, seg, rb)] for seg in range(nseg) for rb in range(P // CONV_RB)], axis=0)
             for cb in range(D_CONV // LANES)], axis=1)
        mu = jnp.mean(cu, axis=-1, keepdims=True)
        cc = cu - mu
        cn = cc * lax.rsqrt(jnp.mean(cc * cc, axis=-1, keepdims=True) + EPS) * w["lng"][...] + w["lnb"][...]
        ca = (cn * _sigmoid(cn)).astype(BF16)
        ma_s[rows, :] = ma_s[rows, :] * _dot(ca, w["wco"][...])
        return carry

    if n_mt == 1:
        phase1(0, 0)
    else:
        lax.fori_loop(0, n_mt, phase1, 0)

    ones_col = (lax.broadcasted_iota(jnp.int32, (L, HEAD_DIM), 1) == 0).astype(F32).astype(BF16)
    pad_rows = jnp.zeros((LANES - 3 * N_UNITS, L), F32)

    def gate_prep(c, m_vec):
        sc = scan_s[c]
        a, run_max, bsum = sc[0:N_UNITS], sc[N_UNITS:2 * N_UNITS], sc[2 * N_UNITS:3 * N_UNITS]
        tot, a_max = sc[3 * N_UNITS:4 * N_UNITS, 0:1], sc[4 * N_UNITS:5 * N_UNITS, 0:1]
        big_m = jnp.maximum(m_vec, run_max)
        m_end = jnp.maximum(m_vec, a_max)
        cols = jnp.concatenate(
            [big_m, jnp.exp(m_vec - big_m), jnp.exp(-bsum - big_m), pad_rows], axis=0).T
        return a, cols, jnp.exp(a - m_end), jnp.exp(m_vec - m_end), tot + m_end

    qk_cache = {}

    def unit_group(d, c, prep, first_chunk, want_state):
        a, cols, wk, decay, _ = prep
        rows = slice(c * L, (c + 1) * L)
        heads = range(N_HEADS)
        hs = [slice(hd * HEAD_DIM, (hd + 1) * HEAD_DIM) for hd in heads]
        idx = [d * N_HEADS + hd for hd in heads]
        col = lambda k, hd: cols[:, k * N_UNITS + idx[hd]:k * N_UNITS + idx[hd] + 1]
        chained = has_state or not first_chunk
        qc = [q_s[rows, hs[hd]] for hd in heads]
        kTc = [kT_s[c, hs[hd], :] for hd in heads]
        vaug = [jnp.concatenate([v_s[rows, hs[hd]], ones_col], axis=1) for hd in heads]
        for hd in heads:
            if not (cps == 1 and (hd, c) in qk_cache):
                qk_cache[(hd, c)] = _dot(qc[hd], kTc[hd])
        mask = lower if d == 0 else upper
        s_mat = [(qk_cache[(hd, c)] * jnp.where(mask, jnp.exp(a[idx[hd]:idx[hd] + 1, :] - col(0, hd)), 0.0)
                  ).astype(BF16) for hd in heads]
        nd = [_dot(s_mat[hd], vaug[hd]) for hd in heads]
        if chained:
            nd = [nd[hd] + col(1, hd) * _dot(qc[hd], cst_s[idx[hd]].astype(BF16)) for hd in heads]
        for hd in heads:
            den = nd[hd][:, HEAD_DIM:HEAD_DIM + 1]
            h = nd[hd][:, :HEAD_DIM] * (1.0 / jnp.maximum(jnp.abs(den), col(2, hd)))
            if d == 0:
                hm_s[rows, hs[hd]] = h
            else:
                hm_s[rows, hs[hd]] = hm_s[rows, hs[hd]] + h
        if want_state:
            kw = [(kTc[hd].astype(F32) * wk[idx[hd]:idx[hd] + 1, :]).astype(BF16) for hd in heads]
            upd = [_dot(kw[hd], vaug[hd]) for hd in heads]
            for hd in heads:
                cst_s[idx[hd]] = (upd[hd] + decay[idx[hd]:idx[hd] + 1, :] * cst_s[idx[hd]]) if chained else upd[hd]

    dir_rows = lax.broadcasted_iota(jnp.int32, (N_UNITS, 1), 0) >= N_HEADS
    for seq in range(n_seq):
        if has_state:
            n_cols = jnp.concatenate([n0_ref[0], jnp.zeros((LANES - N_UNITS, HEAD_DIM), F32)], axis=0).T
            first_lane = lax.broadcasted_iota(jnp.int32, (HEAD_DIM, HEAD_DIM), 1) == 0
            for idx in range(N_UNITS):
                cst_s[idx, :, :HEAD_DIM] = c0_ref[0, idx]
                cst_s[idx, :, HEAD_DIM:] = jnp.where(first_lane, n_cols[:, idx:idx + 1], 0.0)
            m_vec = m0_ref[0, :, 0:1]
        else:
            m_vec = jnp.zeros((N_UNITS, 1), F32)
        prep = None
        for d in range(2):
            order = list(range(cps)) if d == 0 else list(range(cps - 1, -1, -1))
            for pos, c in enumerate(order):
                if cps > 1 or prep is None:
                    prep = gate_prep(seq * cps + c, m_vec)
                unit_group(d, seq * cps + c, prep, pos == 0, emit_state or pos < cps - 1)
                m_vec = jnp.where(dir_rows == (d == 1), prep[4], m_vec)
        if emit_state:
            for idx in range(N_UNITS):
                caug = cst_s[idx]
                cout_ref[0, seq * N_UNITS + idx] = caug[:, :HEAD_DIM]
                nout_ref[0, seq * N_UNITS + idx:seq * N_UNITS + idx + 1, :] = caug[:, HEAD_DIM:].T[0:1, :]
            mout_ref[0, seq * N_UNITS:(seq + 1) * N_UNITS, :] = jnp.broadcast_to(m_vec, (N_UNITS, LANES))

    e_iota = lax.broadcasted_iota(jnp.int32, (LANES, MIX_TM), 0)
    g_of_e = lax.shift_right_logical(e_iota, 2)
    j_of_e = lax.bitwise_and(e_iota, EXPERTS_PER_GROUP - 1)
    r8 = lax.broadcasted_iota(jnp.int32, (8, MIX_TM), 0)
    before_b = (lax.broadcasted_iota(jnp.int32, (MOE_BLK, MOE_BLK), 0)
                < lax.broadcasted_iota(jnp.int32, (MOE_BLK, MOE_BLK), 1)).astype(F32).astype(BF16)

    def phase3(i, carry):
        r0 = pl.multiple_of(i * MIX_TM, MIX_TM)
        rows = pl.ds(r0, MIX_TM)
        hm = hm_s[rows, :]
        heads = []
        for hd in range(N_HEADS):
            hh = hm[:, hd * HEAD_DIM:(hd + 1) * HEAD_DIM]
            heads.append(hh * lax.rsqrt(jnp.mean(hh * hh, axis=-1, keepdims=True) + EPS))
        hn = jnp.concatenate(heads, axis=1) * w["hng"][...]
        hb2 = (so_s[rows, :] * hn).astype(BF16)
        br_b = _dot(hb2, w["wmo"][...])
        mixed = (ma_s[rows, :] + sgb_s[rows, :] * br_b).astype(BF16)
        x1 = x_ref[0, rows, :] + mod_row(2) * _dot(mixed, w["wo"][...])
        x1_ref[0, rows, :] = x1
        xn = x1 * lax.rsqrt(jnp.mean(x1 * x1, axis=-1, keepdims=True) + EPS) * w["g2"][...]
        h2 = xn * (1.0 + mod_row(4)) + mod_row(3)
        h2_ref[0, rows, :] = h2.astype(BF16)

        h2_hi = h2.astype(BF16)
        h2_lo = (h2 - h2_hi.astype(F32)).astype(BF16)
        lg = _dot(h2_hi, w["wrt2"][...])
        lg = lg[:, :LANES] + lg[:, LANES:] + _dot(h2_lo, w["wrt2"][:, :LANES])
        lt = lg.T + w["brtT"][...]
        gl = [lt[N_EXPERTS + g:N_EXPERTS + g + 1, :] for g in range(N_GROUPS)]
        best, gsel = gl[0], jnp.zeros((1, MIX_TM), jnp.int32)
        for g in range(1, N_GROUPS):
            better = gl[g] > best
            gsel = jnp.where(better, g, gsel)
            best = jnp.where(better, gl[g], best)
        gp_sel = 1.0 / sum(jnp.exp(v - best) for v in gl)
        el = []
        for j in range(EXPERTS_PER_GROUP):
            v = lt[j:j + 1, :]
            for g in range(1, N_GROUPS):
                r = g * EXPERTS_PER_GROUP + j
                v = jnp.where(gsel == g, lt[r:r + 1, :], v)
            el.append(v)
        l1, e1 = el[0], jnp.zeros((1, MIX_TM), jnp.int32)
        for j in range(1, EXPERTS_PER_GROUP):
            better = el[j] > l1
            e1 = jnp.where(better, j, e1)
            l1 = jnp.where(better, el[j], l1)
        l2 = jnp.full((1, MIX_TM), -jnp.inf, F32)
        e2 = jnp.zeros((1, MIX_TM), jnp.int32)
        for j in range(EXPERTS_PER_GROUP):
            better = jnp.logical_and(e1 != j, el[j] > l2)
            e2 = jnp.where(better, j, e2)
            l2 = jnp.where(better, el[j], l2)
        r2 = jnp.exp(l2 - l1)
        wt1 = gp_sel / (1.0 + r2)
        wt2 = gp_sel * r2 / (1.0 + r2)
        in_group = g_of_e == gsel
        comb_t = (jnp.where(jnp.logical_and(in_group, j_of_e == e1), wt1, 0.0)
                  + jnp.where(jnp.logical_and(in_group, j_of_e == e2), wt2, 0.0))

        onehot = (r8 == gsel).astype(F32)
        gsel_f = gsel.astype(F32)
        rank = jnp.sum(onehot * _dot(onehot.astype(BF16), before_b), axis=0, keepdims=True)
        r8rows = pl.ds(pl.multiple_of(i * 8, 8), 8)
        route_ref[0, r8rows, :] = jnp.where(r8 == 0, gsel_f, jnp.where(r8 == 1, rank, 0.0))
        cnt_ref[0, r8rows, :] = jnp.broadcast_to(jnp.sum(onehot, axis=1, keepdims=True), (8, LANES))
        comb_t = jnp.where(e_iota == ROUTE_GROUP_LANE, gsel_f,
                           jnp.where(e_iota == ROUTE_RANK_LANE, rank, comb_t))
        comb_ref[0, rows, :] = comb_t.T
        return carry

    if n_mt == 1:
        phase3(0, 0)
    else:
        lax.fori_loop(0, n_mt, phase3, 0)


class _RowWindow(NamedTuple):
    array: jax.Array
    start: int
    n: int


def _const_spec(a):
    if isinstance(a, _RowWindow):
        assert a.start % a.n == 0
        return a.array, pl.BlockSpec((a.n, a.array.shape[1]), lambda b: (a.start // a.n, 0),
                                     pipeline_mode=pl.Buffered(1))
    nd = a.ndim
    return a, pl.BlockSpec(a.shape, lambda b, _nd=nd: (0,) * _nd, pipeline_mode=pl.Buffered(1))


def _mixer(x, T, mod, mod_index, weights, P, state=None, emit_state=False):
    B, R, _ = x.shape
    n_chunks = R // SUB
    n_blk = R // MOE_BLK
    n_seq = R // T
    has_state = state is not None
    seq_mode = {} if R <= MIX_TM else {"pipeline_mode": pl.Buffered(1)}
    in_specs = [
        pl.BlockSpec((1, R, D_MODEL), lambda b: (b, 0, 0), **seq_mode),
        pl.BlockSpec(mod.shape, lambda b: (0, 0, 0)),
    ]
    args = [x, mod]
    if has_state:
        c0, n0, m0 = state
        in_specs += [
            pl.BlockSpec((1, N_UNITS, HEAD_DIM, HEAD_DIM), lambda b: (b, 0, 0, 0)),
            pl.BlockSpec((1, N_UNITS, HEAD_DIM), lambda b: (b, 0, 0)),
            pl.BlockSpec((1, N_UNITS, LANES), lambda b: (b, 0, 0)),
        ]
        args += [c0, n0, m0]
    for name in _MIXER_WEIGHTS:
        operand, spec = _const_spec(weights[name])
        in_specs.append(spec)
        args.append(operand)
    out_shape = [
        jax.ShapeDtypeStruct((B, R, D_MODEL), F32),
        jax.ShapeDtypeStruct((B, R, D_MODEL), BF16),
        jax.ShapeDtypeStruct((B, R, LANES), F32),
        jax.ShapeDtypeStruct((B, n_blk * 8, MOE_BLK), F32),
        jax.ShapeDtypeStruct((B, n_blk * 8, LANES), F32),
    ]
    out_specs = [
        pl.BlockSpec((1, R, D_MODEL), lambda b: (b, 0, 0), **seq_mode),
        pl.BlockSpec((1, R, D_MODEL), lambda b: (b, 0, 0), **seq_mode),
        pl.BlockSpec((1, R, LANES), lambda b: (b, 0, 0)),
        pl.BlockSpec((1, n_blk * 8, MOE_BLK), lambda b: (b, 0, 0)),
        pl.BlockSpec((1, n_blk * 8, LANES), lambda b: (b, 0, 0)),
    ]
    if emit_state:
        out_shape += [
            jax.ShapeDtypeStruct((B, n_seq * N_UNITS, HEAD_DIM, HEAD_DIM), F32),
            jax.ShapeDtypeStruct((B, n_seq * N_UNITS, HEAD_DIM), F32),
            jax.ShapeDtypeStruct((B, n_seq * N_UNITS, LANES), F32),
        ]
        out_specs += [
            pl.BlockSpec((1, n_seq * N_UNITS, HEAD_DIM, HEAD_DIM), lambda b: (b, 0, 0, 0)),
            pl.BlockSpec((1, n_seq * N_UNITS, HEAD_DIM), lambda b: (b, 0, 0)),
            pl.BlockSpec((1, n_seq * N_UNITS, LANES), lambda b: (b, 0, 0)),
        ]
    scratch = [
        pltpu.VMEM((R, D_MLSTM), BF16),
        pltpu.VMEM((n_chunks, D_MLSTM, SUB), BF16),
        pltpu.VMEM((R, D_MLSTM), BF16),
        pltpu.VMEM((R, D_MLSTM), F32),
        pltpu.VMEM((n_chunks, 5 * N_UNITS, SUB), F32),
        pltpu.VMEM((R, D_MODEL), F32),
        pltpu.VMEM((R, D_MODEL), F32),
        pltpu.VMEM((R, D_MLSTM), F32),
        pltpu.VMEM((N_UNITS, HEAD_DIM, 2 * HEAD_DIM), F32),
        pltpu.VMEM((MIX_TM // P, P + 2 * CONV_PAD, D_CONV), F32),
    ]
    return pl.pallas_call(
        functools.partial(_mixer_kernel, R, T, P, has_state, emit_state, mod_index),
        grid=(B,),
        in_specs=in_specs,
        out_specs=out_specs,
        out_shape=out_shape,
        scratch_shapes=scratch,
        compiler_params=pltpu.CompilerParams(
            dimension_semantics=("arbitrary",), vmem_limit_bytes=VMEM_LIMIT),
        name="mixer_T%d" % T,
    )(*args)


def _dest_in_block(group, rank, starts):
    dest = rank
    for g in range(N_GROUPS):
        dest = dest + jnp.where(group == float(g), starts[g], 0.0)
    return dest


def _copy_segments(src_refs, dst_refs, src_starts, dst_starts, n_pieces):
    for g in range(N_GROUPS):
        def body(k, carry, g=g):
            s = pl.multiple_of(src_starts[g] + k * ROW_ALIGN, ROW_ALIGN)
            d = pl.multiple_of(dst_starts[g] + k * ROW_ALIGN, ROW_ALIGN)
            for src, dst in zip(src_refs, dst_refs):
                dst[pl.ds(d, ROW_ALIGN), :] = src[pl.ds(s, ROW_ALIGN), :]
            return carry
        lax.fori_loop(0, n_pieces[g], body, 0)


def _plan_segments(n_blocks, n_tiles, cnt_ref, start_ref, npiece_ref, off_ref, tgroup_ref, tvalid_ref, tfirst_ref):
    align_shift = ROW_ALIGN.bit_length() - 1
    tile_shift = MOE_TM.bit_length() - 1

    def block_starts(blk, carry):
        row = jnp.int32(0)
        for g in range(N_GROUPS):
            n = lax.shift_right_logical(cnt_ref[blk * N_GROUPS + g] + (ROW_ALIGN - 1), align_shift)
            npiece_ref[blk * N_GROUPS + g] = n
            start_ref[blk * N_GROUPS + g] = row
            row = row + n * ROW_ALIGN
        return carry

    lax.fori_loop(0, n_blocks, block_starts, 0)

    base_row = jnp.int32(0)
    base_tile = jnp.int32(0)
    last_group = jnp.int32(0)
    for g in range(N_GROUPS):
        def seg_offsets(blk, row, g=g, base_row=base_row):
            off_ref[blk * N_GROUPS + g] = base_row + row
            return row + npiece_ref[blk * N_GROUPS + g] * ROW_ALIGN

        rows = lax.fori_loop(0, n_blocks, seg_offsets, jnp.int32(0))
        tiles = lax.shift_right_logical(rows + (MOE_TM - 1), tile_shift)

        def mark_tiles(t, carry, g=g, base_tile=base_tile):
            tgroup_ref[base_tile + t] = g
            tvalid_ref[base_tile + t] = 1
            tfirst_ref[base_tile + t] = (t == 0).astype(jnp.int32)
            return carry

        lax.fori_loop(0, tiles, mark_tiles, 0)
        last_group = jnp.where(tiles > 0, g, last_group)
        base_row = base_row + tiles * MOE_TM
        base_tile = base_tile + tiles

    def mark_unused(t, carry):
        tgroup_ref[t] = last_group
        tvalid_ref[t] = 0
        tfirst_ref[t] = 0
        return carry

    lax.fori_loop(base_tile, n_tiles, mark_unused, 0)


def _dispatch_kernel(n_ctx_blocks, n_blocks, n_tiles, cnt_ref,
                     h2c_ref, h2l_ref, cbc_ref, cbl_ref, rtc_ref, rtl_ref,
                     xs_ref, cs_ref, start_ref, npiece_ref, off_ref, tgroup_ref, tvalid_ref, tfirst_ref,
                     sx_s, sc_s):
    b = pl.program_id(0)
    is_ctx = b < n_ctx_blocks

    @pl.when(b == 0)
    def _():
        _plan_segments(n_blocks, n_tiles, cnt_ref, start_ref, npiece_ref, off_ref,
                       tgroup_ref, tvalid_ref, tfirst_ref)
        xs_ref[...] = jnp.zeros_like(xs_ref)
        cs_ref[...] = jnp.zeros_like(cs_ref)

    h2 = jnp.where(is_ctx, h2c_ref[0], h2l_ref[0])
    cb = jnp.where(is_ctx, cbc_ref[0], cbl_ref[0])
    rt = jnp.where(is_ctx, rtc_ref[0], rtl_ref[0])
    starts = [start_ref[b * N_GROUPS + g] for g in range(N_GROUPS)]
    dest = _dest_in_block(rt[0:1, :], rt[1:2, :], [s.astype(F32) for s in starts])
    row = lax.broadcasted_iota(jnp.int32, (SORT_ROWS, MOE_BLK), 0).astype(F32)
    perm = (row == dest).astype(F32).astype(BF16)
    cb_hi = cb.astype(BF16)
    cb_lo = (cb - cb_hi.astype(F32)).astype(BF16)
    sx_s[...] = _dot(perm, h2).astype(BF16)
    sc_s[...] = _dot(perm, jnp.concatenate([cb_hi, cb_lo], axis=1)).astype(BF16)
    _copy_segments((sx_s, sc_s), (xs_ref, cs_ref), starts,
                   [off_ref[b * N_GROUPS + g] for g in range(N_GROUPS)],
                   [npiece_ref[b * N_GROUPS + g] for g in range(N_GROUPS)])


def _experts_kernel(tgroup_ref, tvalid_ref, tfirst_ref, xs_ref, cs_ref, wg_ref, wu_ref, wd_ref, ys_ref,
                    wg_s, wu_s, wd_s):
    i = pl.program_id(0)

    @pl.when(tfirst_ref[i] == 1)
    def _():
        for j in range(EXPERTS_PER_GROUP):
            cols = slice(j * D_EXPERT, (j + 1) * D_EXPERT)
            wg_s[:, cols] = wg_ref[j].astype(BF16)
            wu_s[:, cols] = wu_ref[j].astype(BF16)
            wd_s[cols, :] = wd_ref[j].astype(BF16)

    @pl.when(tvalid_ref[i] == 1)
    def _():
        x = xs_ref[...]
        g = _dot(x, wg_s[...])
        u = _dot(x, wu_s[...])
        comb = cs_ref[:, :LANES].astype(F32) + cs_ref[:, LANES:].astype(F32)
        lane = lax.broadcasted_iota(jnp.int32, comb.shape, 1)
        first = tgroup_ref[i] * EXPERTS_PER_GROUP
        parts = []
        for j in range(EXPERTS_PER_GROUP):
            cols = slice(j * D_EXPERT, (j + 1) * D_EXPERT)
            cw = jnp.sum(jnp.where(lane == first + j, comb, 0.0), axis=1, keepdims=True)
            gj = g[:, cols]
            parts.append((gj * _sigmoid(gj) * u[:, cols] * cw).astype(BF16))
        ys_ref[...] = _dot(jnp.concatenate(parts, axis=1), wd_s[...]).astype(BF16)

    @pl.when(tvalid_ref[i] == 0)
    def _():
        ys_ref[...] = jnp.zeros_like(ys_ref)


def _combine_kernel(n_ctx_blocks, blocks_per_lat_seq, start_ref, npiece_ref, off_ref,
                    x1c_ref, x1l_ref, cbc_ref, cbl_ref, ys_ref, mod_ref, gf_ref, yc_ref, yl_ref, loc_s):
    b = pl.program_id(0)
    is_ctx = b < n_ctx_blocks
    starts = [start_ref[b * N_GROUPS + g] for g in range(N_GROUPS)]
    loc_s[...] = jnp.zeros_like(loc_s)
    _copy_segments((ys_ref,), (loc_s,), [off_ref[b * N_GROUPS + g] for g in range(N_GROUPS)], starts,
                   [npiece_ref[b * N_GROUPS + g] for g in range(N_GROUPS)])
    cb = jnp.where(is_ctx, cbc_ref[0], cbl_ref[0])
    dest = _dest_in_block(cb[:, ROUTE_GROUP_LANE:ROUTE_GROUP_LANE + 1],
                          cb[:, ROUTE_RANK_LANE:ROUTE_RANK_LANE + 1],
                          [s.astype(F32) for s in starts])
    col = lax.broadcasted_iota(jnp.int32, (MOE_BLK, SORT_ROWS), 1).astype(F32)
    unperm = (col == dest).astype(F32).astype(BF16)
    moe = _dot(unperm, loc_s[...])
    x1 = jnp.where(is_ctx, x1c_ref[0], x1l_ref[0])
    mrow = jnp.where(is_ctx, 0, 1 + jnp.maximum(b - n_ctx_blocks, 0) // blocks_per_lat_seq)
    x2 = x1 + mod_ref[N_ADA - 1, pl.ds(mrow, 1), :] * moe
    y = x2 * lax.rsqrt(jnp.mean(x2 * x2, axis=-1, keepdims=True) + EPS) * gf_ref[...]

    @pl.when(is_ctx)
    def _():
        yc_ref[0] = y

    @pl.when(jnp.logical_not(is_ctx))
    def _():
        yl_ref[0] = y


def _moe(x1c, x1l, h2c, h2l, cbc, cbl, rtc, rtl, cnt, mod, blocks_per_lat_seq, wg, wu, wd, gf):
    nc, nl = x1c.shape[0], x1l.shape[0]
    nb = nc + nl
    n_rows_max = nb * MOE_BLK + nb * N_GROUPS * (ROW_ALIGN - 1) + N_GROUPS * (MOE_TM - ROW_ALIGN)
    n_tiles = -(-n_rows_max // MOE_TM)
    ns = n_tiles * MOE_TM

    cmap = lambda b, *_: (jnp.minimum(b, nc - 1), 0, 0)
    lmap = lambda b, *_: (jnp.maximum(b - nc, 0), 0, 0)
    whole = lambda *_: (0, 0)
    once = {"pipeline_mode": pl.Buffered(1)}
    arb = pltpu.CompilerParams(dimension_semantics=("arbitrary",), vmem_limit_bytes=VMEM_LIMIT)
    smem = pl.BlockSpec(memory_space=pltpu.SMEM)
    seg_i32 = jax.ShapeDtypeStruct((nb * N_GROUPS,), jnp.int32)
    tile_i32 = jax.ShapeDtypeStruct((n_tiles,), jnp.int32)

    xs, cs, start, npiece, off, tgroup, tvalid, tfirst = pl.pallas_call(
        functools.partial(_dispatch_kernel, nc, nb, n_tiles),
        grid_spec=pltpu.PrefetchScalarGridSpec(
            num_scalar_prefetch=1, grid=(nb,),
            in_specs=[
                pl.BlockSpec((1, MOE_BLK, D_MODEL), cmap), pl.BlockSpec((1, MOE_BLK, D_MODEL), lmap),
                pl.BlockSpec((1, MOE_BLK, LANES), cmap), pl.BlockSpec((1, MOE_BLK, LANES), lmap),
                pl.BlockSpec((1, 8, MOE_BLK), cmap), pl.BlockSpec((1, 8, MOE_BLK), lmap),
            ],
            out_specs=[pl.BlockSpec((ns, D_MODEL), whole, **once), pl.BlockSpec((ns, 2 * LANES), whole, **once),
                       smem, smem, smem, smem, smem, smem],
            scratch_shapes=[pltpu.VMEM((SORT_ROWS, D_MODEL), BF16), pltpu.VMEM((SORT_ROWS, 2 * LANES), BF16)],
        ),
        out_shape=[jax.ShapeDtypeStruct((ns, D_MODEL), BF16), jax.ShapeDtypeStruct((ns, 2 * LANES), BF16),
                   seg_i32, seg_i32, seg_i32, tile_i32, tile_i32, tile_i32],
        compiler_params=arb,
        name="moe_dispatch",
    )(cnt, h2c, h2l, cbc, cbl, rtc, rtl)

    wmap = lambda i, tg, tv, tf: (tg[i], 0, 0)
    ys = pl.pallas_call(
        _experts_kernel,
        grid_spec=pltpu.PrefetchScalarGridSpec(
            num_scalar_prefetch=3, grid=(n_tiles,),
            in_specs=[
                pl.BlockSpec((MOE_TM, D_MODEL), lambda i, *_: (i, 0)),
                pl.BlockSpec((MOE_TM, 2 * LANES), lambda i, *_: (i, 0)),
                pl.BlockSpec((EXPERTS_PER_GROUP, D_MODEL, D_EXPERT), wmap),
                pl.BlockSpec((EXPERTS_PER_GROUP, D_MODEL, D_EXPERT), wmap),
                pl.BlockSpec((EXPERTS_PER_GROUP, D_EXPERT, D_MODEL), wmap),
            ],
            out_specs=pl.BlockSpec((MOE_TM, D_MODEL), lambda i, *_: (i, 0)),
            scratch_shapes=[pltpu.VMEM((D_MODEL, EXPERTS_PER_GROUP * D_EXPERT), BF16),
                            pltpu.VMEM((D_MODEL, EXPERTS_PER_GROUP * D_EXPERT), BF16),
                            pltpu.VMEM((EXPERTS_PER_GROUP * D_EXPERT, D_MODEL), BF16)],
        ),
        out_shape=jax.ShapeDtypeStruct((ns, D_MODEL), BF16),
        compiler_params=arb,
        name="moe_experts",
    )(tgroup, tvalid, tfirst, xs, cs, wg, wu, wd)

    yc, yl = pl.pallas_call(
        functools.partial(_combine_kernel, nc, blocks_per_lat_seq),
        grid_spec=pltpu.PrefetchScalarGridSpec(
            num_scalar_prefetch=3, grid=(nb,),
            in_specs=[
                pl.BlockSpec((1, MOE_BLK, D_MODEL), cmap), pl.BlockSpec((1, MOE_BLK, D_MODEL), lmap),
                pl.BlockSpec((1, MOE_BLK, LANES), cmap), pl.BlockSpec((1, MOE_BLK, LANES), lmap),
                pl.BlockSpec((ns, D_MODEL), whole, **once),
                pl.BlockSpec(mod.shape, lambda *_: (0, 0, 0)),
                pl.BlockSpec((1, D_MODEL), whole),
            ],
            out_specs=[pl.BlockSpec((1, MOE_BLK, D_MODEL), cmap), pl.BlockSpec((1, MOE_BLK, D_MODEL), lmap)],
            scratch_shapes=[pltpu.VMEM((SORT_ROWS, D_MODEL), BF16)],
        ),
        out_shape=[jax.ShapeDtypeStruct((nc, MOE_BLK, D_MODEL), F32),
                   jax.ShapeDtypeStruct((nl, MOE_BLK, D_MODEL), F32)],
        compiler_params=arb,
        name="moe_combine",
    )(start, npiece, off, x1c, x1l, cbc, cbl, ys, mod, gf)
    return yc, yl


def _prep_weights(norm1_g, w_in, b_in, b_gates, w_dw, b_dw, conv_ln_g, conv_ln_b, w_conv_out,
                  mlstm_hn_g, w_mlstm_out, w_o, norm2_g, w_rg, b_rg, w_re, b_re):
    s_a = 2 * D_CONV
    s_q = s_a + D_MLSTM
    s_k = s_q + D_MLSTM
    s_v = s_k + D_MLSTM
    s_o = s_v + D_MLSTM
    s_g = s_o + 4 * N_HEADS
    row = lambda v: v.reshape(1, -1).astype(F32)
    w_t = w_in.T
    keep = [(0, s_q), (s_k, s_o), (s_g, w_in.shape[1])]
    wrow = _transpose_cast(w_t, [r for a, b in keep for r in range(a, b, WPREP_ROWS)])
    bg = (b_in[s_o:s_g] + b_gates.reshape(-1)).reshape(2, 2, N_HEADS).transpose(1, 0, 2).reshape(-1, 1)
    row_window = lambda start, n: _RowWindow(w_t, start, n)
    n_rt = N_EXPERTS + N_GROUPS
    wrt = jnp.pad(jnp.concatenate([w_re, w_rg], axis=1), ((0, 0), (0, LANES - n_rt)))
    wrt_hi = wrt.astype(BF16)
    wrt2 = jnp.concatenate([wrt_hi, (wrt - wrt_hi.astype(F32)).astype(BF16)], axis=1)
    brtT = jnp.pad(jnp.concatenate([b_re, b_rg]), (0, LANES - n_rt)).reshape(LANES, 1)
    return {
        "g1": row(norm1_g),
        "wrow": wrow, "bag": row(b_in[:s_a]), "bq": row(b_in[s_a:s_q]),
        "wkT": row_window(s_q, D_MLSTM), "bk": b_in[s_q:s_k].reshape(-1, 1),
        "bv": row(b_in[s_k:s_v]), "bog": row(b_in[s_v:s_o]),
        "wgifT": row_window(s_o, 4 * N_HEADS), "bgifT": bg, "bgm": row(b_in[s_g:]),
        "wdw": w_dw.astype(F32), "bdw": row(b_dw), "lng": row(conv_ln_g), "lnb": row(conv_ln_b),
        "wco": w_conv_out.astype(BF16), "hng": row(mlstm_hn_g), "wmo": w_mlstm_out.astype(BF16),
        "wo": w_o.astype(BF16), "g2": row(norm2_g), "wrt2": wrt2, "brtT": brtT,
    }


def kernel(x_prompt, x_sample, state_C, state_n, state_m, c, c_ctx, norm1_g, w_ada, b_ada, w_in, b_in, b_gates, w_dw, b_dw, conv_ln_g, conv_ln_b, w_conv_out, mlstm_hn_g, w_mlstm_out, w_o, norm2_g, w_rg, b_rg, w_re, b_re, w_e_gate, w_e_up, w_e_down, norm_final_g):
    B, S, _ = x_prompt.shape
    Bd, Sd, _ = x_sample.shape
    assert w_ada.shape[0] == 1, "single trunk layer"
    assert S == SUB and Sd % SUB == 0

    cin = jnp.concatenate([c_ctx[None, :], c, jnp.zeros((8 - 1 - Bd, D_MODEL), F32)], axis=0)
    mod = _ada(cin, w_ada[0], b_ada[0].reshape(1, -1))

    wts = _prep_weights(norm1_g[0], w_in[0], b_in[0], b_gates[0], w_dw[0], b_dw[0], conv_ln_g[0],
                        conv_ln_b[0], w_conv_out[0], mlstm_hn_g[0], w_mlstm_out[0], w_o[0],
                        norm2_g[0], w_rg[0], b_rg[0], w_re[0], b_re[0])

    x1p, h2p, cbp, rtp, cntp, c_new, n_new, m_new = _mixer(
        x_prompt.reshape(B * S // MIX_TM, MIX_TM, D_MODEL), S, mod, lambda b: 0, wts, P=S, emit_state=True)

    m0 = jnp.broadcast_to(state_m[:, 0].reshape(Bd, N_UNITS, 1), (Bd, N_UNITS, LANES))
    state = (state_C[:, 0].reshape(Bd, N_UNITS, HEAD_DIM, HEAD_DIM), state_n[:, 0].reshape(Bd, N_UNITS, HEAD_DIM), m0)
    x1s, h2s, cbs, rts, cnts = _mixer(x_sample, Sd, mod, lambda b: 1 + b, wts, P=GRID_W, state=state)

    nc, nl = B * S // MOE_BLK, Bd * Sd // MOE_BLK
    blk = lambda a, n: a.reshape(n, MOE_BLK, a.shape[-1])
    cnt = jnp.concatenate([cntp.reshape(nc, 8, LANES)[:, :N_GROUPS, 0],
                           cnts.reshape(nl, 8, LANES)[:, :N_GROUPS, 0]], axis=0)
    yp, ys = _moe(blk(x1p, nc), blk(x1s, nl), blk(h2p, nc), blk(h2s, nl), blk(cbp, nc), blk(cbs, nl),
                  rtp.reshape(nc, 8, MOE_BLK), rts.reshape(nl, 8, MOE_BLK),
                  cnt.astype(jnp.int32).reshape(-1), mod, Sd // MOE_BLK, w_e_gate[0], w_e_up[0], w_e_down[0],
                  norm_final_g.reshape(1, -1))

    return (yp.reshape(B, S, D_MODEL), ys.reshape(Bd, Sd, D_MODEL),
            c_new.reshape(B, 1, 2, N_HEADS, HEAD_DIM, HEAD_DIM),
            n_new.reshape(B, 1, 2, N_HEADS, HEAD_DIM),
            m_new[:, :, 0].reshape(B, 1, 2, N_HEADS))
```

```python
import functools
from typing import NamedTuple

import jax
import jax.numpy as jnp
from jax import lax
from jax.experimental import pallas as pl
from jax.experimental.pallas import tpu as pltpu

D_MODEL = 1024
D_CONV = 512
CONV_K = 31
D_MLSTM = 512
N_HEADS = 4
HEAD_DIM = D_MLSTM // N_HEADS
N_GROUPS = 4
EXPERTS_PER_GROUP = 4
N_EXPERTS = N_GROUPS * EXPERTS_PER_GROUP
D_EXPERT = 256
N_ADA = 6
EPS = 1e-6
GRID_W = 64

LANES = 128
SUB = 256
CONV_PAD = 16
CONV_RB = 64
N_UNITS = 2 * N_HEADS
ROW_ALIGN = 16
MOE_TM = 512
MIX_TM = 512
MOE_BLK = MIX_TM
SORT_ROWS = MOE_BLK + N_GROUPS * ROW_ALIGN
WPREP_ROWS = 512
ROUTE_GROUP_LANE = N_EXPERTS
ROUTE_RANK_LANE = N_EXPERTS + 1
VMEM_LIMIT = 58 * 1024 * 1024

BF16 = jnp.bfloat16
F32 = jnp.float32
NT_DIMS = (((1,), (1,)), ((), ()))


def _dot(a, b):
    return jnp.dot(a, b, preferred_element_type=F32)


def _dot_nt(a, b, precision=None):
    return lax.dot_general(a, b, NT_DIMS, preferred_element_type=F32, precision=precision)


def _sigmoid(x):
    return 0.5 * jnp.tanh(0.5 * x) + 0.5


def _log_sigmoid(x):
    return jnp.minimum(x, 0.0) - jnp.log1p(jnp.exp(-jnp.abs(x)))


def _split3(x):
    hi = x.astype(BF16).astype(F32)
    r1 = x - hi
    mid = r1.astype(BF16).astype(F32)
    lo = (r1 - mid).astype(BF16).astype(F32)
    return hi, mid, lo


def _ada_kernel(c_ref, w_ref, b_ref, o_ref):
    c = c_ref[...]
    s = (c * _sigmoid(c)).astype(BF16)
    o_ref[0] = _dot(s, w_ref[...].astype(BF16)) + b_ref[...]


def _ada(cin, w_ada, b_ada):
    return pl.pallas_call(
        _ada_kernel,
        grid=(N_ADA,),
        in_specs=[
            pl.BlockSpec((8, D_MODEL), lambda j: (0, 0)),
            pl.BlockSpec((D_MODEL, D_MODEL), lambda j: (0, j)),
            pl.BlockSpec((1, D_MODEL), lambda j: (0, j)),
        ],
        out_specs=pl.BlockSpec((1, 8, D_MODEL), lambda j: (j, 0, 0)),
        out_shape=jax.ShapeDtypeStruct((N_ADA, 8, D_MODEL), F32),
        compiler_params=pltpu.CompilerParams(dimension_semantics=("arbitrary",)),
        name="ada",
    )(cin, w_ada, b_ada)


def _transpose_cast_kernel(starts_ref, wt_ref, o_ref):
    o_ref[...] = wt_ref[...].astype(BF16).T


def _transpose_cast(w_t, row_starts):
    n, k = len(row_starts), w_t.shape[1]
    return pl.pallas_call(
        _transpose_cast_kernel,
        grid_spec=pltpu.PrefetchScalarGridSpec(
            num_scalar_prefetch=1, grid=(n,),
            in_specs=[pl.BlockSpec((pl.Element(WPREP_ROWS), pl.Element(k)), lambda j, starts: (starts[j] * 8, 0))],
            out_specs=pl.BlockSpec((k, WPREP_ROWS), lambda j, starts: (0, j)),
        ),
        out_shape=jax.ShapeDtypeStruct((k, n * WPREP_ROWS), BF16),
        compiler_params=pltpu.CompilerParams(dimension_semantics=("arbitrary",)),
        name="transpose_cast",
    )(jnp.array([r // 8 for r in row_starts], jnp.int32), w_t)


WROW_OFFSET = {"wq": 2 * D_CONV, "wv": 2 * D_CONV + D_MLSTM, "wog": 2 * D_CONV + 2 * D_MLSTM,
               "wgm": 2 * D_CONV + 3 * D_MLSTM}

_MIXER_WEIGHTS = (
    "g1", "wrow", "bag", "bq", "wkT", "bk", "bv", "bog",
    "wgifT", "bgifT", "bgm", "wdw", "bdw", "lng", "lnb",
    "wco", "hng", "wmo", "wo", "g2", "wrt2", "brtT",
)


def _zero_after(x):
    bits = lax.bitcast_convert_type(x, jnp.uint32)
    bits = lax.shift_right_logical(lax.shift_right_logical(bits, jnp.uint32(16)), jnp.uint32(16))
    return lax.bitcast_convert_type(bits, F32)[0:1, :]


def _conv_block(upad_s, seg, base, cs, wdw_ref, bdw_ref, after=None):
    sub = 8
    first = CONV_PAD - CONV_K // 2
    acc = jnp.broadcast_to(bdw_ref[0:1, cs], (CONV_RB, LANES))
    for r in range(sub):
        z = None
        for a in range((CONV_K + first + sub - 1) // sub):
            j = sub * a + r - first
            if 0 <= j < CONV_K:
                lo = base + sub * a
                tap = wdw_ref[j:j + 1, cs] if after is None else wdw_ref[j:j + 1, cs] + after
                term = tap * upad_s[seg, lo:lo + CONV_RB + sub, cs]
                z = term if z is None else z + term
        acc = acc + z[r:r + CONV_RB, :]
    return acc


def _mixer_kernel(R, T, P, has_state, emit_state, mod_index, *refs):
    L = SUB
    n_mt = R // MIX_TM
    cpm = MIX_TM // L
    n_seq = R // T
    cps = T // L
    nseg = MIX_TM // P
    assert not has_state or n_seq == 1
    it = iter(refs)
    x_ref = next(it)
    mod_ref = next(it)
    if has_state:
        c0_ref = next(it)
        n0_ref = next(it)
        m0_ref = next(it)
    w = {name: next(it) for name in _MIXER_WEIGHTS}
    x1_ref = next(it)
    h2_ref = next(it)
    comb_ref = next(it)
    route_ref = next(it)
    cnt_ref = next(it)
    if emit_state:
        cout_ref = next(it)
        nout_ref = next(it)
        mout_ref = next(it)
    (q_s, kT_s, v_s, so_s, scan_s, ma_s, sgb_s, hm_s, cst_s, upad_s) = [next(it) for _ in range(10)]

    cond_row = mod_index(pl.program_id(0))

    def mod_row(i):
        return mod_ref[i, pl.ds(cond_row, 1), :]

    zpad = jnp.zeros((CONV_PAD, D_CONV), F32)
    for seg in range(nseg):
        upad_s[seg, 0:CONV_PAD, :] = zpad
        upad_s[seg, CONV_PAD + P:CONV_PAD + P + CONV_PAD, :] = zpad

    t_idx = lax.broadcasted_iota(jnp.int32, (L, L), 0)
    s_idx = lax.broadcasted_iota(jnp.int32, (L, L), 1)
    lower = s_idx <= t_idx
    upper = s_idx >= t_idx
    triu_b = upper.astype(F32).astype(BF16)
    lane_u = lax.broadcasted_iota(jnp.int32, (N_UNITS, L), 1)
    is_bwd = lax.broadcasted_iota(jnp.int32, (N_UNITS, L), 0) >= N_HEADS

    def gate_scan(g):
        gi, lf = g[:N_UNITS], _log_sigmoid(g[N_UNITS:])
        pr = _dot(jnp.concatenate(_split3(lf), axis=0).astype(BF16), triu_b)
        pre = pr[0:N_UNITS] + pr[N_UNITS:2 * N_UNITS] + pr[2 * N_UNITS:]
        tot = pre[:, L - 1:L]
        bsum = jnp.where(is_bwd, tot - pre + lf, pre)
        a = gi - bsum
        pm, sm, k = a, a, 1
        while k < L:
            pm = jnp.where(lane_u >= k, jnp.maximum(pm, pltpu.roll(pm, k, axis=1)), pm)
            sm = jnp.where(lane_u < L - k, jnp.maximum(sm, pltpu.roll(sm, L - k, axis=1)), sm)
            k *= 2
        wide = lambda v: jnp.broadcast_to(v, (N_UNITS, L))
        return jnp.concatenate([a, jnp.where(is_bwd, sm, pm), bsum, wide(tot),
                                wide(jnp.max(a, axis=1, keepdims=True))], axis=0)

    def phase1(i, carry):
        r0 = pl.multiple_of(i * MIX_TM, MIX_TM)
        rows = pl.ds(r0, MIX_TM)
        x = x_ref[0, rows, :]
        xn = x * lax.rsqrt(jnp.mean(x * x, axis=-1, keepdims=True) + EPS) * w["g1"][...]
        hb = (xn * (1.0 + mod_row(1)) + mod_row(0)).astype(BF16)

        gates = _dot_nt(w["wgifT"][...].astype(BF16), hb)
        gates = jnp.concatenate([gates[d * 2 * N_HEADS + g * N_HEADS:d * 2 * N_HEADS + (g + 1) * N_HEADS]
                                 for g in range(2) for d in range(2)], axis=0) + w["bgifT"][...]
        for j in range(cpm):
            scan_s[i * cpm + j] = gate_scan(gates[:, j * L:(j + 1) * L])
        ag = _dot(hb, w["wrow"][:, :2 * D_CONV]) + w["bag"][...]
        u = ag[:, :D_CONV] * _sigmoid(ag[:, D_CONV:])
        for seg in range(nseg):
            upad_s[seg, CONV_PAD:CONV_PAD + P, :] = u[seg * P:(seg + 1) * P, :]

        def proj(name, bias, c0, gate, width=2 * LANES):
            w0 = WROW_OFFSET[name] + c0
            b = w[bias][:, c0:c0 + width]
            if gate is not None:
                b = b + jnp.concatenate([gate] * (width // LANES), axis=1)
            return _dot(hb, w["wrow"][:, w0:w0 + width]) + b

        last = lambda z: z[-8:, -LANES:]

        def gm_a(c0, gate):
            z = proj("wgm", "bgm", c0, gate)
            ma_s[rows, c0:c0 + 2 * LANES] = _sigmoid(z)
            return last(z)

        def gm_b(c0, gate):
            z = proj("wgm", "bgm", D_MODEL + c0, gate)
            sgb_s[rows, c0:c0 + 2 * LANES] = _sigmoid(z)
            return last(z)

        def q_part(c0, gate):
            z = proj("wq", "bq", c0, gate)
            q_s[rows, c0:c0 + 2 * LANES] = (z * (HEAD_DIM ** -0.5)).astype(BF16)
            return last(z)

        def v_part(c0, gate):
            z = proj("wv", "bv", c0, gate)
            v_s[rows, c0:c0 + 2 * LANES] = z.astype(BF16)
            return last(z)

        def o_part(c0, gate):
            z = proj("wog", "bog", c0, gate)
            so_s[rows, c0:c0 + 2 * LANES] = _sigmoid(z)
            return last(z)

        def k_part(c0, gate):
            rs = slice(c0, c0 + 2 * LANES)
            b = w["bk"][rs, :] if gate is None else w["bk"][rs, :] + gate[:, 0:1]
            z = _dot_nt(w["wkT"][rs, :].astype(BF16), hb) + b
            kt = z.astype(BF16)
            for j in range(cpm):
                kT_s[i * cpm + j, rs, :] = kt[:, j * L:(j + 1) * L]
            return last(z)

        jobs = ([functools.partial(gm_a, c0) for c0 in range(0, D_MODEL, 2 * LANES)]
                + [functools.partial(gm_b, c0) for c0 in range(0, D_MODEL, 2 * LANES)]
                + [functools.partial(f, c0) for f in (q_part, k_part, v_part, o_part)
                   for c0 in range(0, D_MLSTM, 2 * LANES)])
        n_jobs = len(jobs)
        conv = {}
        after = None
        n_pieces = (D_CONV // LANES) * nseg * (P // CONV_RB)
        for cb in range(D_CONV // LANES):
            cs = slice(cb * LANES, (cb + 1) * LANES)
            for seg in range(nseg):
                for rb in range(P // CONV_RB):
                    blk = _conv_block(upad_s, seg, rb * CONV_RB, cs, w["wdw"], w["bdw"], after)
                    conv[(cb, seg, rb)] = blk
                    if jobs and len(conv) * n_jobs >= (n_jobs - len(jobs) + 1) * n_pieces:
                        after = _zero_after(jobs.pop(0)(_zero_after(blk[-8:, :])))
        for job in jobs:
            job(None)
        cu = jnp.concatenate(
            [jnp.concatenate([conv[(cb, seg, rb)] for seg in range(nseg) for rb in range(P // CONV_RB)], axis=0)
             for cb in range(D_CONV // LANES)], axis=1)
        mu = jnp.mean(cu, axis=-1, keepdims=True)
        cc = cu - mu
        cn = cc * lax.rsqrt(jnp.mean(cc * cc, axis=-1, keepdims=True) + EPS) * w["lng"][...] + w["lnb"][...]
        ca = (cn * _sigmoid(cn)).astype(BF16)
        ma_s[rows, :] = ma_s[rows, :] * _dot(ca, w["wco"][...])
        return carry

    if n_mt == 1:
        phase1(0, 0)
    else:
        lax.fori_loop(0, n_mt, phase1, 0)

    ones_col = (lax.broadcasted_iota(jnp.int32, (L, HEAD_DIM), 1) == 0).astype(F32).astype(BF16)
    pad_rows = jnp.zeros((LANES - 3 * N_UNITS, L), F32)

    def gate_prep(c, m_vec):
        sc = scan_s[c]
        a, run_max, bsum = sc[0:N_UNITS], sc[N_UNITS:2 * N_UNITS], sc[2 * N_UNITS:3 * N_UNITS]
        tot, a_max = sc[3 * N_UNITS:4 * N_UNITS, 0:1], sc[4 * N_UNITS:5 * N_UNITS, 0:1]
        big_m = jnp.maximum(m_vec, run_max)
        m_end = jnp.maximum(m_vec, a_max)
        cols = jnp.concatenate(
            [big_m, jnp.exp(m_vec - big_m), jnp.exp(-bsum - big_m), pad_rows], axis=0).T
        return a, cols, jnp.exp(a - m_end), jnp.exp(m_vec - m_end), tot + m_end

    def unit_group(dirs, c, prep, first_chunk, want_state):
        a, cols, wk, decay, _ = prep
        rows = slice(c * L, (c + 1) * L)
        heads = range(N_HEADS)
        units = [(d, hd) for d in dirs for hd in heads]
        hs = [slice(hd * HEAD_DIM, (hd + 1) * HEAD_DIM) for hd in heads]
        idx = {u: u[0] * N_HEADS + u[1] for u in units}
        col = lambda k, u: cols[:, k * N_UNITS + idx[u]:k * N_UNITS + idx[u] + 1]
        row = lambda arr, u: arr[idx[u]:idx[u] + 1, :]
        chained = has_state or not first_chunk
        qc = [q_s[rows, hs[hd]] for hd in heads]
        kTc = [kT_s[c, hs[hd], :] for hd in heads]
        vaug = [jnp.concatenate([v_s[rows, hs[hd]], ones_col], axis=1) for hd in heads]
        qk = [_dot(qc[hd], kTc[hd]) for hd in heads]
        s_mat = {u: (qk[u[1]] * jnp.where(lower if u[0] == 0 else upper, jnp.exp(row(a, u) - col(0, u)), 0.0)
                     ).astype(BF16) for u in units}
        nd = {u: _dot(s_mat[u], vaug[u[1]]) for u in units}
        if chained:
            nd = {u: nd[u] + col(1, u) * _dot(qc[u[1]], cst_s[idx[u]].astype(BF16)) for u in units}
        h = {u: nd[u][:, :HEAD_DIM] * (1.0 / jnp.maximum(jnp.abs(nd[u][:, HEAD_DIM:HEAD_DIM + 1]), col(2, u)))
             for u in units}
        for hd in heads:
            total = h[(dirs[0], hd)]
            for d in dirs[1:]:
                total = total + h[(d, hd)]
            if dirs[0] == 0:
                hm_s[rows, hs[hd]] = total
            else:
                hm_s[rows, hs[hd]] = hm_s[rows, hs[hd]] + total
        if want_state:
            kw = {u: (kTc[u[1]].astype(F32) * row(wk, u)).astype(BF16) for u in units}
            upd = {u: _dot(kw[u], vaug[u[1]]) for u in units}
            for u in units:
                cst_s[idx[u]] = (upd[u] + row(decay, u) * cst_s[idx[u]]) if chained else upd[u]

    dir_rows = lax.broadcasted_iota(jnp.int32, (N_UNITS, 1), 0) >= N_HEADS
    for seq in range(n_seq):
        if has_state:
            n_cols = jnp.concatenate([n0_ref[0], jnp.zeros((LANES - N_UNITS, HEAD_DIM), F32)], axis=0).T
            first_lane = lax.broadcasted_iota(jnp.int32, (HEAD_DIM, HEAD_DIM), 1) == 0
            for idx in range(N_UNITS):
                cst_s[idx, :, :HEAD_DIM] = c0_ref[0, idx]
                cst_s[idx, :, HEAD_DIM:] = jnp.where(first_lane, n_cols[:, idx:idx + 1], 0.0)
            m_vec = m0_ref[0, :, 0:1]
        else:
            m_vec = jnp.zeros((N_UNITS, 1), F32)
        if cps == 1:
            prep = gate_prep(seq, m_vec)
            unit_group([0, 1], seq, prep, True, emit_state)
            m_vec = prep[4]
        else:
            for d in range(2):
                order = list(range(cps)) if d == 0 else list(range(cps - 1, -1, -1))
                for pos, c in enumerate(order):
                    prep = gate_prep(seq * cps + c, m_vec)
                    unit_group([d], seq * cps + c, prep, pos == 0, emit_state or pos < cps - 1)
                    m_vec = jnp.where(dir_rows == (d == 1), prep[4], m_vec)
        if emit_state:
            for idx in range(N_UNITS):
                caug = cst_s[idx]
                cout_ref[0, seq * N_UNITS + idx] = caug[:, :HEAD_DIM]
                nout_ref[0, seq * N_UNITS + idx:seq * N_UNITS + idx + 1, :] = caug[:, HEAD_DIM:].T[0:1, :]
            mout_ref[0, seq * N_UNITS:(seq + 1) * N_UNITS, :] = jnp.broadcast_to(m_vec, (N_UNITS, LANES))

    e_iota = lax.broadcasted_iota(jnp.int32, (LANES, MIX_TM), 0)
    g_of_e = lax.shift_right_logical(e_iota, 2)
    j_of_e = lax.bitwise_and(e_iota, EXPERTS_PER_GROUP - 1)
    r8 = lax.broadcasted_iota(jnp.int32, (8, MIX_TM), 0)
    before_b = (lax.broadcasted_iota(jnp.int32, (MOE_BLK, MOE_BLK), 0)
                < lax.broadcasted_iota(jnp.int32, (MOE_BLK, MOE_BLK), 1)).astype(F32).astype(BF16)

    def phase3(i, carry):
        r0 = pl.multiple_of(i * MIX_TM, MIX_TM)
        rows = pl.ds(r0, MIX_TM)
        hm = hm_s[rows, :]
        heads = []
        for hd in range(N_HEADS):
            hh = hm[:, hd * HEAD_DIM:(hd + 1) * HEAD_DIM]
            heads.append(hh * lax.rsqrt(jnp.mean(hh * hh, axis=-1, keepdims=True) + EPS))
        hn = jnp.concatenate(heads, axis=1) * w["hng"][...]
        hb2 = (so_s[rows, :] * hn).astype(BF16)
        br_b = _dot(hb2, w["wmo"][...])
        mixed = (ma_s[rows, :] + sgb_s[rows, :] * br_b).astype(BF16)
        x1 = x_ref[0, rows, :] + mod_row(2) * _dot(mixed, w["wo"][...])
        x1_ref[0, rows, :] = x1
        xn = x1 * lax.rsqrt(jnp.mean(x1 * x1, axis=-1, keepdims=True) + EPS) * w["g2"][...]
        h2 = xn * (1.0 + mod_row(4)) + mod_row(3)
        h2_ref[0, rows, :] = h2.astype(BF16)

        h2_hi = h2.astype(BF16)
        h2_lo = (h2 - h2_hi.astype(F32)).astype(BF16)
        lg = _dot(h2_hi, w["wrt2"][...])
        lg = lg[:, :LANES] + lg[:, LANES:] + _dot(h2_lo, w["wrt2"][:, :LANES])
        lt = lg.T + w["brtT"][...]
        gl = [lt[N_EXPERTS + g:N_EXPERTS + g + 1, :] for g in range(N_GROUPS)]
        best, gsel = gl[0], jnp.zeros((1, MIX_TM), jnp.int32)
        for g in range(1, N_GROUPS):
            better = gl[g] > best
            gsel = jnp.where(better, g, gsel)
            best = jnp.where(better, gl[g], best)
        gp_sel = 1.0 / sum(jnp.exp(v - best) for v in gl)
        el = []
        for j in range(EXPERTS_PER_GROUP):
            v = lt[j:j + 1, :]
            for g in range(1, N_GROUPS):
                r = g * EXPERTS_PER_GROUP + j
                v = jnp.where(gsel == g, lt[r:r + 1, :], v)
            el.append(v)
        l1, e1 = el[0], jnp.zeros((1, MIX_TM), jnp.int32)
        for j in range(1, EXPERTS_PER_GROUP):
            better = el[j] > l1
            e1 = jnp.where(better, j, e1)
            l1 = jnp.where(better, el[j], l1)
        l2 = jnp.full((1, MIX_TM), -jnp.inf, F32)
        e2 = jnp.zeros((1, MIX_TM), jnp.int32)
        for j in range(EXPERTS_PER_GROUP):
            better = jnp.logical_and(e1 != j, el[j] > l2)
            e2 = jnp.where(better, j, e2)
            l2 = jnp.where(better, el[j], l2)
        r2 = jnp.exp(l2 - l1)
        wt1 = gp_sel / (1.0 + r2)
        wt2 = gp_sel * r2 / (1.0 + r2)
        in_group = g_of_e == gsel
        comb_t = (jnp.where(jnp.logical_and(in_group, j_of_e == e1), wt1, 0.0)
                  + jnp.where(jnp.logical_and(in_group, j_of_e == e2), wt2, 0.0))

        onehot = (r8 == gsel).astype(F32)
        gsel_f = gsel.astype(F32)
        rank = jnp.sum(onehot * _dot(onehot.astype(BF16), before_b), axis=0, keepdims=True)
        r8rows = pl.ds(pl.multiple_of(i * 8, 8), 8)
        route_ref[0, r8rows, :] = jnp.where(r8 == 0, gsel_f, jnp.where(r8 == 1, rank, 0.0))
        cnt_ref[0, r8rows, :] = jnp.broadcast_to(jnp.sum(onehot, axis=1, keepdims=True), (8, LANES))
        comb_t = jnp.where(e_iota == ROUTE_GROUP_LANE, gsel_f,
                           jnp.where(e_iota == ROUTE_RANK_LANE, rank, comb_t))
        comb_ref[0, rows, :] = comb_t.T
        return carry

    if n_mt == 1:
        phase3(0, 0)
    else:
        lax.fori_loop(0, n_mt, phase3, 0)


class _RowWindow(NamedTuple):
    array: jax.Array
    start: int
    n: int


def _const_spec(a):
    if isinstance(a, _RowWindow):
        assert a.start % a.n == 0
        return a.array, pl.BlockSpec((a.n, a.array.shape[1]), lambda b: (a.start // a.n, 0),
                                     pipeline_mode=pl.Buffered(1))
    nd = a.ndim
    return a, pl.BlockSpec(a.shape, lambda b, _nd=nd: (0,) * _nd, pipeline_mode=pl.Buffered(1))


def _mixer(x, T, mod, mod_index, weights, P, state=None, emit_state=False):
    B, R, _ = x.shape
    n_chunks = R // SUB
    n_blk = R // MOE_BLK
    n_seq = R // T
    has_state = state is not None
    seq_mode = {} if R <= MIX_TM else {"pipeline_mode": pl.Buffered(1)}
    in_specs = [
        pl.BlockSpec((1, R, D_MODEL), lambda b: (b, 0, 0), **seq_mode),
        pl.BlockSpec(mod.shape, lambda b: (0, 0, 0)),
    ]
    args = [x, mod]
    if has_state:
        c0, n0, m0 = state
        in_specs += [
            pl.BlockSpec((1, N_UNITS, HEAD_DIM, HEAD_DIM), lambda b: (b, 0, 0, 0)),
            pl.BlockSpec((1, N_UNITS, HEAD_DIM), lambda b: (b, 0, 0)),
            pl.BlockSpec((1, N_UNITS, LANES), lambda b: (b, 0, 0)),
        ]
        args += [c0, n0, m0]
    for name in _MIXER_WEIGHTS:
        operand, spec = _const_spec(weights[name])
        in_specs.append(spec)
        args.append(operand)
    out_shape = [
        jax.ShapeDtypeStruct((B, R, D_MODEL), F32),
        jax.ShapeDtypeStruct((B, R, D_MODEL), BF16),
        jax.ShapeDtypeStruct((B, R, LANES), F32),
        jax.ShapeDtypeStruct((B, n_blk * 8, MOE_BLK), F32),
        jax.ShapeDtypeStruct((B, n_blk * 8, LANES), F32),
    ]
    out_specs = [
        pl.BlockSpec((1, R, D_MODEL), lambda b: (b, 0, 0), **seq_mode),
        pl.BlockSpec((1, R, D_MODEL), lambda b: (b, 0, 0), **seq_mode),
        pl.BlockSpec((1, R, LANES), lambda b: (b, 0, 0)),
        pl.BlockSpec((1, n_blk * 8, MOE_BLK), lambda b: (b, 0, 0)),
        pl.BlockSpec((1, n_blk * 8, LANES), lambda b: (b, 0, 0)),
    ]
    if emit_state:
        out_shape += [
            jax.ShapeDtypeStruct((B, n_seq * N_UNITS, HEAD_DIM, HEAD_DIM), F32),
            jax.ShapeDtypeStruct((B, n_seq * N_UNITS, HEAD_DIM), F32),
            jax.ShapeDtypeStruct((B, n_seq * N_UNITS, LANES), F32),
        ]
        out_specs += [
            pl.BlockSpec((1, n_seq * N_UNITS, HEAD_DIM, HEAD_DIM), lambda b: (b, 0, 0, 0)),
            pl.BlockSpec((1, n_seq * N_UNITS, HEAD_DIM), lambda b: (b, 0, 0)),
            pl.BlockSpec((1, n_seq * N_UNITS, LANES), lambda b: (b, 0, 0)),
        ]
    scratch = [
        pltpu.VMEM((R, D_MLSTM), BF16),
        pltpu.VMEM((n_chunks, D_MLSTM, SUB), BF16),
        pltpu.VMEM((R, D_MLSTM), BF16),
        pltpu.VMEM((R, D_MLSTM), F32),
        pltpu.VMEM((n_chunks, 5 * N_UNITS, SUB), F32),
        pltpu.VMEM((R, D_MODEL), F32),
        pltpu.VMEM((R, D_MODEL), F32),
        pltpu.VMEM((R, D_MLSTM), F32),
        pltpu.VMEM((N_UNITS, HEAD_DIM, 2 * HEAD_DIM), F32),
        pltpu.VMEM((MIX_TM // P, P + 2 * CONV_PAD, D_CONV), F32),
    ]
    return pl.pallas_call(
        functools.partial(_mixer_kernel, R, T, P, has_state, emit_state, mod_index),
        grid=(B,),
        in_specs=in_specs,
        out_specs=out_specs,
        out_shape=out_shape,
        scratch_shapes=scratch,
        compiler_params=pltpu.CompilerParams(
            dimension_semantics=("arbitrary",), vmem_limit_bytes=VMEM_LIMIT),
        name="mixer_T%d" % T,
    )(*args)


def _dest_in_block(group, rank, starts):
    dest = rank
    for g in range(N_GROUPS):
        dest = dest + jnp.where(group == float(g), starts[g], 0.0)
    return dest


def _copy_segments(src_refs, dst_refs, src_starts, dst_starts, n_pieces):
    for g in range(N_GROUPS):
        def body(k, carry, g=g):
            s = pl.multiple_of(src_starts[g] + k * ROW_ALIGN, ROW_ALIGN)
            d = pl.multiple_of(dst_starts[g] + k * ROW_ALIGN, ROW_ALIGN)
            for src, dst in zip(src_refs, dst_refs):
                dst[pl.ds(d, ROW_ALIGN), :] = src[pl.ds(s, ROW_ALIGN), :]
            return carry
        lax.fori_loop(0, n_pieces[g], body, 0)


def _plan_segments(n_blocks, n_tiles, cnt_ref, start_ref, npiece_ref, off_ref, tgroup_ref, tvalid_ref, tfirst_ref):
    align_shift = ROW_ALIGN.bit_length() - 1
    tile_shift = MOE_TM.bit_length() - 1

    def block_starts(blk, carry):
        row = jnp.int32(0)
        for g in range(N_GROUPS):
            n = lax.shift_right_logical(cnt_ref[blk * N_GROUPS + g] + (ROW_ALIGN - 1), align_shift)
            npiece_ref[blk * N_GROUPS + g] = n
            start_ref[blk * N_GROUPS + g] = row
            row = row + n * ROW_ALIGN
        return carry

    lax.fori_loop(0, n_blocks, block_starts, 0)

    base_row = jnp.int32(0)
    base_tile = jnp.int32(0)
    last_group = jnp.int32(0)
    for g in range(N_GROUPS):
        def seg_offsets(blk, row, g=g, base_row=base_row):
            off_ref[blk * N_GROUPS + g] = base_row + row
            return row + npiece_ref[blk * N_GROUPS + g] * ROW_ALIGN

        rows = lax.fori_loop(0, n_blocks, seg_offsets, jnp.int32(0))
        tiles = lax.shift_right_logical(rows + (MOE_TM - 1), tile_shift)

        def mark_tiles(t, carry, g=g, base_tile=base_tile):
            tgroup_ref[base_tile + t] = g
            tvalid_ref[base_tile + t] = 1
            tfirst_ref[base_tile + t] = (t == 0).astype(jnp.int32)
            return carry

        lax.fori_loop(0, tiles, mark_tiles, 0)
        last_group = jnp.where(tiles > 0, g, last_group)
        base_row = base_row + tiles * MOE_TM
        base_tile = base_tile + tiles

    def mark_unused(t, carry):
        tgroup_ref[t] = last_group
        tvalid_ref[t] = 0
        tfirst_ref[t] = 0
        return carry

    lax.fori_loop(base_tile, n_tiles, mark_unused, 0)


def _dispatch_kernel(n_ctx_blocks, n_blocks, n_tiles, cnt_ref,
                     h2c_ref, h2l_ref, cbc_ref, cbl_ref, rtc_ref, rtl_ref,
                     xs_ref, cs_ref, start_ref, npiece_ref, off_ref, tgroup_ref, tvalid_ref, tfirst_ref,
                     sx_s, sc_s):
    b = pl.program_id(0)
    is_ctx = b < n_ctx_blocks

    @pl.when(b == 0)
    def _():
        _plan_segments(n_blocks, n_tiles, cnt_ref, start_ref, npiece_ref, off_ref,
                       tgroup_ref, tvalid_ref, tfirst_ref)
        xs_ref[...] = jnp.zeros_like(xs_ref)
        cs_ref[...] = jnp.zeros_like(cs_ref)

    h2 = jnp.where(is_ctx, h2c_ref[0], h2l_ref[0])
    cb = jnp.where(is_ctx, cbc_ref[0], cbl_ref[0])
    rt = jnp.where(is_ctx, rtc_ref[0], rtl_ref[0])
    starts = [start_ref[b * N_GROUPS + g] for g in range(N_GROUPS)]
    dest = _dest_in_block(rt[0:1, :], rt[1:2, :], [s.astype(F32) for s in starts])
    row = lax.broadcasted_iota(jnp.int32, (SORT_ROWS, MOE_BLK), 0).astype(F32)
    perm = (row == dest).astype(F32).astype(BF16)
    cb_hi = cb.astype(BF16)
    cb_lo = (cb - cb_hi.astype(F32)).astype(BF16)
    sx_s[...] = _dot(perm, h2).astype(BF16)
    sc_s[...] = _dot(perm, jnp.concatenate([cb_hi, cb_lo], axis=1)).astype(BF16)
    _copy_segments((sx_s, sc_s), (xs_ref, cs_ref), starts,
                   [off_ref[b * N_GROUPS + g] for g in range(N_GROUPS)],
                   [npiece_ref[b * N_GROUPS + g] for g in range(N_GROUPS)])


def _experts_kernel(tgroup_ref, tvalid_ref, tfirst_ref, xs_ref, cs_ref, wg_ref, wu_ref, wd_ref, ys_ref,
                    wg_s, wu_s, wd_s):
    i = pl.program_id(0)

    @pl.when(tfirst_ref[i] == 1)
    def _():
        for j in range(EXPERTS_PER_GROUP):
            cols = slice(j * D_EXPERT, (j + 1) * D_EXPERT)
            wg_s[:, cols] = wg_ref[j].astype(BF16)
            wu_s[:, cols] = wu_ref[j].astype(BF16)
            wd_s[cols, :] = wd_ref[j].astype(BF16)

    @pl.when(tvalid_ref[i] == 1)
    def _():
        x = xs_ref[...]
        g = _dot(x, wg_s[...])
        u = _dot(x, wu_s[...])
        comb = cs_ref[:, :LANES].astype(F32) + cs_ref[:, LANES:].astype(F32)
        lane = lax.broadcasted_iota(jnp.int32, comb.shape, 1)
        first = tgroup_ref[i] * EXPERTS_PER_GROUP
        parts = []
        for j in range(EXPERTS_PER_GROUP):
            cols = slice(j * D_EXPERT, (j + 1) * D_EXPERT)
            cw = jnp.sum(jnp.where(lane == first + j, comb, 0.0), axis=1, keepdims=True)
            gj = g[:, cols]
            parts.append((gj * _sigmoid(gj) * u[:, cols] * cw).astype(BF16))
        ys_ref[...] = _dot(jnp.concatenate(parts, axis=1), wd_s[...]).astype(BF16)

    @pl.when(tvalid_ref[i] == 0)
    def _():
        ys_ref[...] = jnp.zeros_like(ys_ref)


def _combine_kernel(n_ctx_blocks, blocks_per_lat_seq, start_ref, npiece_ref, off_ref,
                    x1c_ref, x1l_ref, cbc_ref, cbl_ref, ys_ref, mod_ref, gf_ref, yc_ref, yl_ref, loc_s):
    b = pl.program_id(0)
    is_ctx = b < n_ctx_blocks
    starts = [start_ref[b * N_GROUPS + g] for g in range(N_GROUPS)]
    loc_s[...] = jnp.zeros_like(loc_s)
    _copy_segments((ys_ref,), (loc_s,), [off_ref[b * N_GROUPS + g] for g in range(N_GROUPS)], starts,
                   [npiece_ref[b * N_GROUPS + g] for g in range(N_GROUPS)])
    cb = jnp.where(is_ctx, cbc_ref[0], cbl_ref[0])
    dest = _dest_in_block(cb[:, ROUTE_GROUP_LANE:ROUTE_GROUP_LANE + 1],
                          cb[:, ROUTE_RANK_LANE:ROUTE_RANK_LANE + 1],
                          [s.astype(F32) for s in starts])
    col = lax.broadcasted_iota(jnp.int32, (MOE_BLK, SORT_ROWS), 1).astype(F32)
    unperm = (col == dest).astype(F32).astype(BF16)
    moe = _dot(unperm, loc_s[...])
    x1 = jnp.where(is_ctx, x1c_ref[0], x1l_ref[0])
    mrow = jnp.where(is_ctx, 0, 1 + jnp.maximum(b - n_ctx_blocks, 0) // blocks_per_lat_seq)
    x2 = x1 + mod_ref[N_ADA - 1, pl.ds(mrow, 1), :] * moe
    y = x2 * lax.rsqrt(jnp.mean(x2 * x2, axis=-1, keepdims=True) + EPS) * gf_ref[...]

    @pl.when(is_ctx)
    def _():
        yc_ref[0] = y

    @pl.when(jnp.logical_not(is_ctx))
    def _():
        yl_ref[0] = y


def _moe(x1c, x1l, h2c, h2l, cbc, cbl, rtc, rtl, cnt, mod, blocks_per_lat_seq, wg, wu, wd, gf):
    nc, nl = x1c.shape[0], x1l.shape[0]
    nb = nc + nl
    n_rows_max = nb * MOE_BLK + nb * N_GROUPS * (ROW_ALIGN - 1) + N_GROUPS * (MOE_TM - ROW_ALIGN)
    n_tiles = -(-n_rows_max // MOE_TM)
    ns = n_tiles * MOE_TM

    cmap = lambda b, *_: (jnp.minimum(b, nc - 1), 0, 0)
    lmap = lambda b, *_: (jnp.maximum(b - nc, 0), 0, 0)
    whole = lambda *_: (0, 0)
    once = {"pipeline_mode": pl.Buffered(1)}
    arb = pltpu.CompilerParams(dimension_semantics=("arbitrary",), vmem_limit_bytes=VMEM_LIMIT)
    smem = pl.BlockSpec(memory_space=pltpu.SMEM)
    seg_i32 = jax.ShapeDtypeStruct((nb * N_GROUPS,), jnp.int32)
    tile_i32 = jax.ShapeDtypeStruct((n_tiles,), jnp.int32)

    xs, cs, start, npiece, off, tgroup, tvalid, tfirst = pl.pallas_call(
        functools.partial(_dispatch_kernel, nc, nb, n_tiles),
        grid_spec=pltpu.PrefetchScalarGridSpec(
            num_scalar_prefetch=1, grid=(nb,),
            in_specs=[
                pl.BlockSpec((1, MOE_BLK, D_MODEL), cmap), pl.BlockSpec((1, MOE_BLK, D_MODEL), lmap),
                pl.BlockSpec((1, MOE_BLK, LANES), cmap), pl.BlockSpec((1, MOE_BLK, LANES), lmap),
                pl.BlockSpec((1, 8, MOE_BLK), cmap), pl.BlockSpec((1, 8, MOE_BLK), lmap),
            ],
            out_specs=[pl.BlockSpec((ns, D_MODEL), whole, **once), pl.BlockSpec((ns, 2 * LANES), whole, **once),
                       smem, smem, smem, smem, smem, smem],
            scratch_shapes=[pltpu.VMEM((SORT_ROWS, D_MODEL), BF16), pltpu.VMEM((SORT_ROWS, 2 * LANES), BF16)],
        ),
        out_shape=[jax.ShapeDtypeStruct((ns, D_MODEL), BF16), jax.ShapeDtypeStruct((ns, 2 * LANES), BF16),
                   seg_i32, seg_i32, seg_i32, tile_i32, tile_i32, tile_i32],
        compiler_params=arb,
        name="moe_dispatch",
    )(cnt, h2c, h2l, cbc, cbl, rtc, rtl)

    wmap = lambda i, tg, tv, tf: (tg[i], 0, 0)
    ys = pl.pallas_call(
        _experts_kernel,
        grid_spec=pltpu.PrefetchScalarGridSpec(
            num_scalar_prefetch=3, grid=(n_tiles,),
            in_specs=[
                pl.BlockSpec((MOE_TM, D_MODEL), lambda i, *_: (i, 0)),
                pl.BlockSpec((MOE_TM, 2 * LANES), lambda i, *_: (i, 0)),
                pl.BlockSpec((EXPERTS_PER_GROUP, D_MODEL, D_EXPERT), wmap),
                pl.BlockSpec((EXPERTS_PER_GROUP, D_MODEL, D_EXPERT), wmap),
                pl.BlockSpec((EXPERTS_PER_GROUP, D_EXPERT, D_MODEL), wmap),
            ],
            out_specs=pl.BlockSpec((MOE_TM, D_MODEL), lambda i, *_: (i, 0)),
            scratch_shapes=[pltpu.VMEM((D_MODEL, EXPERTS_PER_GROUP * D_EXPERT), BF16),
                            pltpu.VMEM((D_MODEL, EXPERTS_PER_GROUP * D_EXPERT), BF16),
                            pltpu.VMEM((EXPERTS_PER_GROUP * D_EXPERT, D_MODEL), BF16)],
        ),
        out_shape=jax.ShapeDtypeStruct((ns, D_MODEL), BF16),
        compiler_params=arb,
        name="moe_experts",
    )(tgroup, tvalid, tfirst, xs, cs, wg, wu, wd)

    yc, yl = pl.pallas_call(
        functools.partial(_combine_kernel, nc, blocks_per_lat_seq),
        grid_spec=pltpu.PrefetchScalarGridSpec(
            num_scalar_prefetch=3, grid=(nb,),
            in_specs=[
                pl.BlockSpec((1, MOE_BLK, D_MODEL), cmap), pl.BlockSpec((1, MOE_BLK, D_MODEL), lmap),
                pl.BlockSpec((1, MOE_BLK, LANES), cmap), pl.BlockSpec((1, MOE_BLK, LANES), lmap),
                pl.BlockSpec((ns, D_MODEL), whole, **once),
                pl.BlockSpec(mod.shape, lambda *_: (0, 0, 0)),
                pl.BlockSpec((1, D_MODEL), whole),
            ],
            out_specs=[pl.BlockSpec((1, MOE_BLK, D_MODEL), cmap), pl.BlockSpec((1, MOE_BLK, D_MODEL), lmap)],
            scratch_shapes=[pltpu.VMEM((SORT_ROWS, D_MODEL), BF16)],
        ),
        out_shape=[jax.ShapeDtypeStruct((nc, MOE_BLK, D_MODEL), F32),
                   jax.ShapeDtypeStruct((nl, MOE_BLK, D_MODEL), F32)],
        compiler_params=arb,
        name="moe_combine",
    )(start, npiece, off, x1c, x1l, cbc, cbl, ys, mod, gf)
    return yc, yl


def _prep_weights(norm1_g, w_in, b_in, b_gates, w_dw, b_dw, conv_ln_g, conv_ln_b, w_conv_out,
                  mlstm_hn_g, w_mlstm_out, w_o, norm2_g, w_rg, b_rg, w_re, b_re):
    s_a = 2 * D_CONV
    s_q = s_a + D_MLSTM
    s_k = s_q + D_MLSTM
    s_v = s_k + D_MLSTM
    s_o = s_v + D_MLSTM
    s_g = s_o + 4 * N_HEADS
    row = lambda v: v.reshape(1, -1).astype(F32)
    w_t = w_in.T
    keep = [(0, s_q), (s_k, s_o), (s_g, w_in.shape[1])]
    wrow = _transpose_cast(w_t, [r for a, b in keep for r in range(a, b, WPREP_ROWS)])
    bg = (b_in[s_o:s_g] + b_gates.reshape(-1)).reshape(2, 2, N_HEADS).transpose(1, 0, 2).reshape(-1, 1)
    row_window = lambda start, n: _RowWindow(w_t, start, n)
    n_rt = N_EXPERTS + N_GROUPS
    wrt = jnp.pad(jnp.concatenate([w_re, w_rg], axis=1), ((0, 0), (0, LANES - n_rt)))
    wrt_hi = wrt.astype(BF16)
    wrt2 = jnp.concatenate([wrt_hi, (wrt - wrt_hi.astype(F32)).astype(BF16)], axis=1)
    brtT = jnp.pad(jnp.concatenate([b_re, b_rg]), (0, LANES - n_rt)).reshape(LANES, 1)
    return {
        "g1": row(norm1_g),
        "wrow": wrow, "bag": row(b_in[:s_a]), "bq": row(b_in[s_a:s_q]),
        "wkT": row_window(s_q, D_MLSTM), "bk": b_in[s_q:s_k].reshape(-1, 1),
        "bv": row(b_in[s_k:s_v]), "bog": row(b_in[s_v:s_o]),
        "wgifT": row_window(s_o, 4 * N_HEADS), "bgifT": bg, "bgm": row(b_in[s_g:]),
        "wdw": w_dw.astype(F32), "bdw": row(b_dw), "lng": row(conv_ln_g), "lnb": row(conv_ln_b),
        "wco": w_conv_out.astype(BF16), "hng": row(mlstm_hn_g), "wmo": w_mlstm_out.astype(BF16),
        "wo": w_o.astype(BF16), "g2": row(norm2_g), "wrt2": wrt2, "brtT": brtT,
    }


def kernel(x_prompt, x_sample, state_C, state_n, state_m, c, c_ctx, norm1_g, w_ada, b_ada, w_in, b_in, b_gates, w_dw, b_dw, conv_ln_g, conv_ln_b, w_conv_out, mlstm_hn_g, w_mlstm_out, w_o, norm2_g, w_rg, b_rg, w_re, b_re, w_e_gate, w_e_up, w_e_down, norm_final_g):
    B, S, _ = x_prompt.shape
    Bd, Sd, _ = x_sample.shape
    assert w_ada.shape[0] == 1, "single trunk layer"
    assert S == SUB and Sd % SUB == 0

    cin = jnp.concatenate([c_ctx[None, :], c, jnp.zeros((8 - 1 - Bd, D_MODEL), F32)], axis=0)
    mod = _ada(cin, w_ada[0], b_ada[0].reshape(1, -1))

    wts = _prep_weights(norm1_g[0], w_in[0], b_in[0], b_gates[0], w_dw[0], b_dw[0], conv_ln_g[0],
                        conv_ln_b[0], w_conv_out[0], mlstm_hn_g[0], w_mlstm_out[0], w_o[0],
                        norm2_g[0], w_rg[0], b_rg[0], w_re[0], b_re[0])

    x1p, h2p, cbp, rtp, cntp, c_new, n_new, m_new = _mixer(
        x_prompt.reshape(B * S // MIX_TM, MIX_TM, D_MODEL), S, mod, lambda b: 0, wts, P=S, emit_state=True)

    m0 = jnp.broadcast_to(state_m[:, 0].reshape(Bd, N_UNITS, 1), (Bd, N_UNITS, LANES))
    state = (state_C[:, 0].reshape(Bd, N_UNITS, HEAD_DIM, HEAD_DIM), state_n[:, 0].reshape(Bd, N_UNITS, HEAD_DIM), m0)
    x1s, h2s, cbs, rts, cnts = _mixer(x_sample, Sd, mod, lambda b: 1 + b, wts, P=GRID_W, state=state)

    nc, nl = B * S // MOE_BLK, Bd * Sd // MOE_BLK
    blk = lambda a, n: a.reshape(n, MOE_BLK, a.shape[-1])
    cnt = jnp.concatenate([cntp.reshape(nc, 8, LANES)[:, :N_GROUPS, 0],
                           cnts.reshape(nl, 8, LANES)[:, :N_GROUPS, 0]], axis=0)
    yp, ys = _moe(blk(x1p, nc), blk(x1s, nl), blk(h2p, nc), blk(h2s, nl), blk(cbp, nc), blk(cbs, nl),
                  rtp.reshape(nc, 8, MOE_BLK), rts.reshape(nl, 8, MOE_BLK),
                  cnt.astype(jnp.int32).reshape(-1), mod, Sd // MOE_BLK, w_e_gate[0], w_e_up[0], w_e_down[0],
                  norm_final_g.reshape(1, -1))

    return (yp.reshape(B, S, D_MODEL), ys.reshape(Bd, Sd, D_MODEL),
            c_new.reshape(B, 1, 2, N_HEADS, HEAD_DIM, HEAD_DIM),
            n_new.reshape(B, 1, 2, N_HEADS, HEAD_DIM),
            m_new[:, :, 0].reshape(B, 1, 2, N_HEADS))
```

```python
import functools
from typing import NamedTuple

import jax
import jax.numpy as jnp
from jax import lax
from jax.experimental import pallas as pl
from jax.experimental.pallas import tpu as pltpu

D_MODEL = 1024
D_CONV = 512
CONV_K = 31
D_MLSTM = 512
N_HEADS = 4
HEAD_DIM = D_MLSTM // N_HEADS
N_GROUPS = 4
EXPERTS_PER_GROUP = 4
N_EXPERTS = N_GROUPS * EXPERTS_PER_GROUP
D_EXPERT = 256
N_ADA = 6
EPS = 1e-6
GRID_W = 64

LANES = 128
SUB = 256
CONV_PAD = 16
CONV_RB = 64
N_UNITS = 2 * N_HEADS
ROW_ALIGN = 16
MOE_TM = 512
MIX_TM = 512
MOE_BLK = MIX_TM
SORT_ROWS = MOE_BLK + N_GROUPS * ROW_ALIGN
WPREP_ROWS = 512
ROUTE_GROUP_LANE = N_EXPERTS
ROUTE_RANK_LANE = N_EXPERTS + 1
VMEM_LIMIT = 58 * 1024 * 1024

BF16 = jnp.bfloat16
F32 = jnp.float32
NT_DIMS = (((1,), (1,)), ((), ()))


def _dot(a, b):
    return jnp.dot(a, b, preferred_element_type=F32)


def _dot_nt(a, b, precision=None):
    return lax.dot_general(a, b, NT_DIMS, preferred_element_type=F32, precision=precision)


def _sigmoid(x):
    return 0.5 * jnp.tanh(0.5 * x) + 0.5


def _log_sigmoid(x):
    return jnp.minimum(x, 0.0) - jnp.log1p(jnp.exp(-jnp.abs(x)))


def _split3(x):
    hi = x.astype(BF16).astype(F32)
    r1 = x - hi
    mid = r1.astype(BF16).astype(F32)
    lo = (r1 - mid).astype(BF16).astype(F32)
    return hi, mid, lo


def _ada_kernel(c_ref, w_ref, b_ref, o_ref):
    c = c_ref[...]
    s = (c * _sigmoid(c)).astype(BF16)
    o_ref[0] = _dot(s, w_ref[...].astype(BF16)) + b_ref[...]


def _ada(cin, w_ada, b_ada):
    return pl.pallas_call(
        _ada_kernel,
        grid=(N_ADA,),
        in_specs=[
            pl.BlockSpec((8, D_MODEL), lambda j: (0, 0)),
            pl.BlockSpec((D_MODEL, D_MODEL), lambda j: (0, j)),
            pl.BlockSpec((1, D_MODEL), lambda j: (0, j)),
        ],
        out_specs=pl.BlockSpec((1, 8, D_MODEL), lambda j: (j, 0, 0)),
        out_shape=jax.ShapeDtypeStruct((N_ADA, 8, D_MODEL), F32),
        compiler_params=pltpu.CompilerParams(dimension_semantics=("arbitrary",)),
        name="ada",
    )(cin, w_ada, b_ada)


def _transpose_cast_kernel(starts_ref, wt_ref, o_ref):
    o_ref[...] = wt_ref[...].astype(BF16).T


def _transpose_cast(w_t, row_starts):
    n, k = len(row_starts), w_t.shape[1]
    return pl.pallas_call(
        _transpose_cast_kernel,
        grid_spec=pltpu.PrefetchScalarGridSpec(
            num_scalar_prefetch=1, grid=(n,),
            in_specs=[pl.BlockSpec((pl.Element(WPREP_ROWS), pl.Element(k)), lambda j, starts: (starts[j] * 8, 0))],
            out_specs=pl.BlockSpec((k, WPREP_ROWS), lambda j, starts: (0, j)),
        ),
        out_shape=jax.ShapeDtypeStruct((k, n * WPREP_ROWS), BF16),
        compiler_params=pltpu.CompilerParams(dimension_semantics=("arbitrary",)),
        name="transpose_cast",
    )(jnp.array([r // 8 for r in row_starts], jnp.int32), w_t)


WROW_OFFSET = {"wq": 2 * D_CONV, "wv": 2 * D_CONV + D_MLSTM, "wog": 2 * D_CONV + 2 * D_MLSTM,
               "wgm": 2 * D_CONV + 3 * D_MLSTM}

_MIXER_WEIGHTS = (
    "g1", "wrow", "bag", "bq", "wkT", "bk", "bv", "bog",
    "wgifT", "bgifT", "bgm", "wdw", "bdw", "lng", "lnb",
    "wco", "hng", "wmo", "wo", "g2", "wrt2", "brtT",
)


def _zero_after(x):
    bits = lax.bitcast_convert_type(x, jnp.uint32)
    bits = lax.shift_right_logical(lax.shift_right_logical(bits, jnp.uint32(16)), jnp.uint32(16))
    return lax.bitcast_convert_type(bits, F32)[0:1, :]


def _conv_block(upad_s, seg, base, cs, wdw_ref, bdw_ref, after=None):
    sub = 8
    first = CONV_PAD - CONV_K // 2
    acc = jnp.broadcast_to(bdw_ref[0:1, cs], (CONV_RB, LANES))
    for r in range(sub):
        z = None
        for a in range((CONV_K + first + sub - 1) // sub):
            j = sub * a + r - first
            if 0 <= j < CONV_K:
                lo = base + sub * a
                tap = wdw_ref[j:j + 1, cs] if after is None else wdw_ref[j:j + 1, cs] + after
                term = tap * upad_s[seg, lo:lo + CONV_RB + sub, cs]
                z = term if z is None else z + term
        acc = acc + z[r:r + CONV_RB, :]
    return acc


def _mixer_kernel(R, T, P, has_state, emit_state, mod_index, *refs):
    L = SUB
    n_mt = R // MIX_TM
    cpm = MIX_TM // L
    n_seq = R // T
    cps = T // L
    nseg = MIX_TM // P
    assert not has_state or n_seq == 1
    it = iter(refs)
    x_ref = next(it)
    mod_ref = next(it)
    if has_state:
        c0_ref = next(it)
        n0_ref = next(it)
        m0_ref = next(it)
    w = {name: next(it) for name in _MIXER_WEIGHTS}
    x1_ref = next(it)
    h2_ref = next(it)
    comb_ref = next(it)
    route_ref = next(it)
    cnt_ref = next(it)
    if emit_state:
        cout_ref = next(it)
        nout_ref = next(it)
        mout_ref = next(it)
    (q_s, kT_s, v_s, so_s, scan_s, ma_s, sgb_s, hm_s, cst_s, upad_s) = [next(it) for _ in range(10)]

    cond_row = mod_index(pl.program_id(0))

    def mod_row(i):
        return mod_ref[i, pl.ds(cond_row, 1), :]

    zpad = jnp.zeros((CONV_PAD, D_CONV), F32)
    for seg in range(nseg):
        upad_s[seg, 0:CONV_PAD, :] = zpad
        upad_s[seg, CONV_PAD + P:CONV_PAD + P + CONV_PAD, :] = zpad

    t_idx = lax.broadcasted_iota(jnp.int32, (L, L), 0)
    s_idx = lax.broadcasted_iota(jnp.int32, (L, L), 1)
    lower = s_idx <= t_idx
    upper = s_idx >= t_idx
    triu_b = upper.astype(F32).astype(BF16)
    lane_u = lax.broadcasted_iota(jnp.int32, (N_UNITS, L), 1)
    is_bwd = lax.broadcasted_iota(jnp.int32, (N_UNITS, L), 0) >= N_HEADS

    def gate_scan(g):
        gi, lf = g[:N_UNITS], _log_sigmoid(g[N_UNITS:])
        pr = _dot(jnp.concatenate(_split3(lf), axis=0).astype(BF16), triu_b)
        pre = pr[0:N_UNITS] + pr[N_UNITS:2 * N_UNITS] + pr[2 * N_UNITS:]
        tot = pre[:, L - 1:L]
        bsum = jnp.where(is_bwd, tot - pre + lf, pre)
        a = gi - bsum
        pm, sm, k = a, a, 1
        while k < L:
            pm = jnp.where(lane_u >= k, jnp.maximum(pm, pltpu.roll(pm, k, axis=1)), pm)
            sm = jnp.where(lane_u < L - k, jnp.maximum(sm, pltpu.roll(sm, L - k, axis=1)), sm)
            k *= 2
        wide = lambda v: jnp.broadcast_to(v, (N_UNITS, L))
        return jnp.concatenate([a, jnp.where(is_bwd, sm, pm), bsum, wide(tot),
                                wide(jnp.max(a, axis=1, keepdims=True))], axis=0)

    def phase1(i, carry):
        r0 = pl.multiple_of(i * MIX_TM, MIX_TM)
        rows = pl.ds(r0, MIX_TM)
        x = x_ref[0, rows, :]
        xn = x * lax.rsqrt(jnp.mean(x * x, axis=-1, keepdims=True) + EPS) * w["g1"][...]
        hb = (xn * (1.0 + mod_row(1)) + mod_row(0)).astype(BF16)

        gates = _dot_nt(w["wgifT"][...].astype(BF16), hb)
        gates = jnp.concatenate([gates[d * 2 * N_HEADS + g * N_HEADS:d * 2 * N_HEADS + (g + 1) * N_HEADS]
                                 for g in range(2) for d in range(2)], axis=0) + w["bgifT"][...]
        for j in range(cpm):
            scan_s[i * cpm + j] = gate_scan(gates[:, j * L:(j + 1) * L])
        ag = _dot(hb, w["wrow"][:, :2 * D_CONV]) + w["bag"][...]
        u = ag[:, :D_CONV] * _sigmoid(ag[:, D_CONV:])
        for seg in range(nseg):
            upad_s[seg, CONV_PAD:CONV_PAD + P, :] = u[seg * P:(seg + 1) * P, :]

        def proj(name, bias, c0, gate, width=2 * LANES):
            w0 = WROW_OFFSET[name] + c0
            b = w[bias][:, c0:c0 + width]
            if gate is not None:
                b = b + jnp.concatenate([gate] * (width // LANES), axis=1)
            return _dot(hb, w["wrow"][:, w0:w0 + width]) + b

        last = lambda z: z[-8:, -LANES:]

        def gm_a(c0, gate):
            z = proj("wgm", "bgm", c0, gate)
            ma_s[rows, c0:c0 + 2 * LANES] = _sigmoid(z)
            return last(z)

        def gm_b(c0, gate):
            z = proj("wgm", "bgm", D_MODEL + c0, gate)
            sgb_s[rows, c0:c0 + 2 * LANES] = _sigmoid(z)
            return last(z)

        def q_part(c0, gate):
            z = proj("wq", "bq", c0, gate)
            q_s[rows, c0:c0 + 2 * LANES] = (z * (HEAD_DIM ** -0.5)).astype(BF16)
            return last(z)

        def v_part(c0, gate):
            z = proj("wv", "bv", c0, gate)
            v_s[rows, c0:c0 + 2 * LANES] = z.astype(BF16)
            return last(z)

        def o_part(c0, gate):
            z = proj("wog", "bog", c0, gate)
            so_s[rows, c0:c0 + 2 * LANES] = _sigmoid(z)
            return last(z)

        def k_part(c0, gate):
            rs = slice(c0, c0 + 2 * LANES)
            b = w["bk"][rs, :] if gate is None else w["bk"][rs, :] + gate[:, 0:1]
            z = _dot_nt(w["wkT"][rs, :].astype(BF16), hb) + b
            kt = z.astype(BF16)
            for j in range(cpm):
                kT_s[i * cpm + j, rs, :] = kt[:, j * L:(j + 1) * L]
            return last(z)

        jobs = ([functools.partial(gm_a, c0) for c0 in range(0, D_MODEL, 2 * LANES)]
                + [functools.partial(gm_b, c0) for c0 in range(0, D_MODEL, 2 * LANES)]
                + [functools.partial(f, c0) for f in (q_part, k_part, v_part, o_part)
                   for c0 in range(0, D_MLSTM, 2 * LANES)])
        n_jobs = len(jobs)
        conv = {}
        after = None
        n_pieces = (D_CONV // LANES) * nseg * (P // CONV_RB)
        for cb in range(D_CONV // LANES):
            cs = slice(cb * LANES, (cb + 1) * LANES)
            for seg in range(nseg):
                for rb in range(P // CONV_RB):
                    blk = _conv_block(upad_s, seg, rb * CONV_RB, cs, w["wdw"], w["bdw"], after)
                    conv[(cb, seg, rb)] = blk
                    if jobs and len(conv) * n_jobs >= (n_jobs - len(jobs) + 1) * n_pieces:
                        after = _zero_after(jobs.pop(0)(_zero_after(blk[-8:, :])))
        for job in jobs:
            job(None)
        cu = jnp.concatenate(
            [jnp.concatenate([conv[(cb, seg, rb)] for seg in range(nseg) for rb in range(P // CONV_RB)], axis=0)
             for cb in range(D_CONV // LANES)], axis=1)
        mu = jnp.mean(cu, axis=-1, keepdims=True)
        cc = cu - mu
        cn = cc * lax.rsqrt(jnp.mean(cc * cc, axis=-1, keepdims=True) + EPS) * w["lng"][...] + w["lnb"][...]
        ca = (cn * _sigmoid(cn)).astype(BF16)
        ma_s[rows, :] = ma_s[rows, :] * _dot(ca, w["wco"][...])
        return carry

    if n_mt == 1:
        phase1(0, 0)
    else:
        lax.fori_loop(0, n_mt, phase1, 0)

    ones_col = (lax.broadcasted_iota(jnp.int32, (L, HEAD_DIM), 1) == 0).astype(F32).astype(BF16)
    pad_rows = jnp.zeros((LANES - 3 * N_UNITS, L), F32)

    def gate_prep(c, m_vec):
        sc = scan_s[c]
        a, run_max, bsum = sc[0:N_UNITS], sc[N_UNITS:2 * N_UNITS], sc[2 * N_UNITS:3 * N_UNITS]
        tot, a_max = sc[3 * N_UNITS:4 * N_UNITS, 0:1], sc[4 * N_UNITS:5 * N_UNITS, 0:1]
        big_m = jnp.maximum(m_vec, run_max)
        m_end = jnp.maximum(m_vec, a_max)
        cols = jnp.concatenate(
            [big_m, jnp.exp(m_vec - big_m), jnp.exp(-bsum - big_m), pad_rows], axis=0).T
        return a, cols, jnp.exp(a - m_end), jnp.exp(m_vec - m_end), tot + m_end

    def unit_group(dirs, c, prep, first_chunk, want_state):
        a, cols, wk, decay, _ = prep
        rows = slice(c * L, (c + 1) * L)
        heads = range(N_HEADS)
        units = [(d, hd) for d in dirs for hd in heads]
        hs = [slice(hd * HEAD_DIM, (hd + 1) * HEAD_DIM) for hd in heads]
        idx = {u: u[0] * N_HEADS + u[1] for u in units}
        col = lambda k, u: cols[:, k * N_UNITS + idx[u]:k * N_UNITS + idx[u] + 1]
        row = lambda arr, u: arr[idx[u]:idx[u] + 1, :]
        chained = has_state or not first_chunk
        qc = [q_s[rows, hs[hd]] for hd in heads]
        kTc = [kT_s[c, hs[hd], :] for hd in heads]
        vaug = [jnp.concatenate([v_s[rows, hs[hd]], ones_col], axis=1) for hd in heads]
        qk = [_dot(qc[hd], kTc[hd]) for hd in heads]
        s_mat = {u: (qk[u[1]] * jnp.where(lower if u[0] == 0 else upper, jnp.exp(row(a, u) - col(0, u)), 0.0)
                     ).astype(BF16) for u in units}
        nd = {u: _dot(s_mat[u], vaug[u[1]]) for u in units}
        if chained:
            nd = {u: nd[u] + col(1, u) * _dot(qc[u[1]], cst_s[idx[u]].astype(BF16)) for u in units}
        h = {u: nd[u][:, :HEAD_DIM] * (1.0 / jnp.maximum(jnp.abs(nd[u][:, HEAD_DIM:HEAD_DIM + 1]), col(2, u)))
             for u in units}
        for hd in heads:
            total = h[(dirs[0], hd)]
            for d in dirs[1:]:
                total = total + h[(d, hd)]
            if dirs[0] == 0:
                hm_s[rows, hs[hd]] = total
            else:
                hm_s[rows, hs[hd]] = hm_s[rows, hs[hd]] + total
        if want_state:
            kw = {u: (kTc[u[1]].astype(F32) * row(wk, u)).astype(BF16) for u in units}
            upd = {u: _dot(kw[u], vaug[u[1]]) for u in units}
            for u in units:
                cst_s[idx[u]] = (upd[u] + row(decay, u) * cst_s[idx[u]]) if chained else upd[u]

    dir_rows = lax.broadcasted_iota(jnp.int32, (N_UNITS, 1), 0) >= N_HEADS
    for seq in range(n_seq):
        if has_state:
            n_cols = jnp.concatenate([n0_ref[0], jnp.zeros((LANES - N_UNITS, HEAD_DIM), F32)], axis=0).T
            first_lane = lax.broadcasted_iota(jnp.int32, (HEAD_DIM, HEAD_DIM), 1) == 0
            for idx in range(N_UNITS):
                cst_s[idx, :, :HEAD_DIM] = c0_ref[0, idx]
                cst_s[idx, :, HEAD_DIM:] = jnp.where(first_lane, n_cols[:, idx:idx + 1], 0.0)
            m_vec = m0_ref[0, :, 0:1]
        else:
            m_vec = jnp.zeros((N_UNITS, 1), F32)
        if cps == 1:
            prep = gate_prep(seq, m_vec)
            unit_group([0, 1], seq, prep, True, emit_state)
            m_vec = prep[4]
        else:
            for d in range(2):
                order = list(range(cps)) if d == 0 else list(range(cps - 1, -1, -1))
                for pos, c in enumerate(order):
                    prep = gate_prep(seq * cps + c, m_vec)
                    unit_group([d], seq * cps + c, prep, pos == 0, emit_state or pos < cps - 1)
                    m_vec = jnp.where(dir_rows == (d == 1), prep[4], m_vec)
        if emit_state:
            for idx in range(N_UNITS):
                caug = cst_s[idx]
                cout_ref[0, seq * N_UNITS + idx] = caug[:, :HEAD_DIM]
                nout_ref[0, seq * N_UNITS + idx:seq * N_UNITS + idx + 1, :] = caug[:, HEAD_DIM:].T[0:1, :]
            mout_ref[0, seq * N_UNITS:(seq + 1) * N_UNITS, :] = jnp.broadcast_to(m_vec, (N_UNITS, LANES))

    e_iota = lax.broadcasted_iota(jnp.int32, (LANES, MIX_TM), 0)
    g_of_e = lax.shift_right_logical(e_iota, 2)
    j_of_e = lax.bitwise_and(e_iota, EXPERTS_PER_GROUP - 1)
    r8 = lax.broadcasted_iota(jnp.int32, (8, MIX_TM), 0)
    before_b = (lax.broadcasted_iota(jnp.int32, (MOE_BLK, MOE_BLK), 0)
                < lax.broadcasted_iota(jnp.int32, (MOE_BLK, MOE_BLK), 1)).astype(F32).astype(BF16)

    def phase3(i, carry):
        r0 = pl.multiple_of(i * MIX_TM, MIX_TM)
        rows = pl.ds(r0, MIX_TM)
        hm = hm_s[rows, :]
        heads = []
        for hd in range(N_HEADS):
            hh = hm[:, hd * HEAD_DIM:(hd + 1) * HEAD_DIM]
            heads.append(hh * lax.rsqrt(jnp.mean(hh * hh, axis=-1, keepdims=True) + EPS))
        hn = jnp.concatenate(heads, axis=1) * w["hng"][...]
        hb2 = (so_s[rows, :] * hn).astype(BF16)
        br_b = _dot(hb2, w["wmo"][...])
        mixed = (ma_s[rows, :] + sgb_s[rows, :] * br_b).astype(BF16)
        x1 = x_ref[0, rows, :] + mod_row(2) * _dot(mixed, w["wo"][...])
        x1_ref[0, rows, :] = x1
        xn = x1 * lax.rsqrt(jnp.mean(x1 * x1, axis=-1, keepdims=True) + EPS) * w["g2"][...]
        h2 = xn * (1.0 + mod_row(4)) + mod_row(3)
        h2_ref[0, rows, :] = h2.astype(BF16)

        h2_hi = h2.astype(BF16)
        h2_lo = (h2 - h2_hi.astype(F32)).astype(BF16)
        lg = _dot(h2_hi, w["wrt2"][...])
        lg = lg[:, :LANES] + lg[:, LANES:] + _dot(h2_lo, w["wrt2"][:, :LANES])
        lt = lg.T + w["brtT"][...]
        gl = [lt[N_EXPERTS + g:N_EXPERTS + g + 1, :] for g in range(N_GROUPS)]
        best, gsel = gl[0], jnp.zeros((1, MIX_TM), jnp.int32)
        for g in range(1, N_GROUPS):
            better = gl[g] > best
            gsel = jnp.where(better, g, gsel)
            best = jnp.where(better, gl[g], best)
        gp_sel = 1.0 / sum(jnp.exp(v - best) for v in gl)
        el = []
        for j in range(EXPERTS_PER_GROUP):
            v = lt[j:j + 1, :]
            for g in range(1, N_GROUPS):
                r = g * EXPERTS_PER_GROUP + j
                v = jnp.where(gsel == g, lt[r:r + 1, :], v)
            el.append(v)
        l1, e1 = el[0], jnp.zeros((1, MIX_TM), jnp.int32)
        for j in range(1, EXPERTS_PER_GROUP):
            better = el[j] > l1
            e1 = jnp.where(better, j, e1)
            l1 = jnp.where(better, el[j], l1)
        l2 = jnp.full((1, MIX_TM), -jnp.inf, F32)
        e2 = jnp.zeros((1, MIX_TM), jnp.int32)
        for j in range(EXPERTS_PER_GROUP):
            better = jnp.logical_and(e1 != j, el[j] > l2)
            e2 = jnp.where(better, j, e2)
            l2 = jnp.where(better, el[j], l2)
        r2 = jnp.exp(l2 - l1)
        wt1 = gp_sel / (1.0 + r2)
        wt2 = gp_sel * r2 / (1.0 + r2)
        in_group = g_of_e == gsel
        comb_t = (jnp.where(jnp.logical_and(in_group, j_of_e == e1), wt1, 0.0)
                  + jnp.where(jnp.logical_and(in_group, j_of_e == e2), wt2, 0.0))

        onehot = (r8 == gsel).astype(F32)
        gsel_f = gsel.astype(F32)
        rank = jnp.sum(onehot * _dot(onehot.astype(BF16), before_b), axis=0, keepdims=True)
        r8rows = pl.ds(pl.multiple_of(i * 8, 8), 8)
        route_ref[0, r8rows, :] = jnp.where(r8 == 0, gsel_f, jnp.where(r8 == 1, rank, 0.0))
        cnt_ref[0, r8rows, :] = jnp.broadcast_to(jnp.sum(onehot, axis=1, keepdims=True), (8, LANES))
        comb_t = jnp.where(e_iota == ROUTE_GROUP_LANE, gsel_f,
                           jnp.where(e_iota == ROUTE_RANK_LANE, rank, comb_t))
        comb_ref[0, rows, :] = comb_t.T
        return carry

    if n_mt == 1:
        phase3(0, 0)
    else:
        lax.fori_loop(0, n_mt, phase3, 0)


class _RowWindow(NamedTuple):
    array: jax.Array
    start: int
    n: int


def _const_spec(a):
    if isinstance(a, _RowWindow):
        assert a.start % a.n == 0
        return a.array, pl.BlockSpec((a.n, a.array.shape[1]), lambda b: (a.start // a.n, 0),
                                     pipeline_mode=pl.Buffered(1))
    nd = a.ndim
    return a, pl.BlockSpec(a.shape, lambda b, _nd=nd: (0,) * _nd, pipeline_mode=pl.Buffered(1))


def _mixer(x, T, mod, mod_index, weights, P, state=None, emit_state=False):
    B, R, _ = x.shape
    n_chunks = R // SUB
    n_blk = R // MOE_BLK
    n_seq = R // T
    has_state = state is not None
    seq_mode = {} if R <= MIX_TM else {"pipeline_mode": pl.Buffered(1)}
    in_specs = [
        pl.BlockSpec((1, R, D_MODEL), lambda b: (b, 0, 0), **seq_mode),
        pl.BlockSpec(mod.shape, lambda b: (0, 0, 0)),
    ]
    args = [x, mod]
    if has_state:
        c0, n0, m0 = state
        in_specs += [
            pl.BlockSpec((1, N_UNITS, HEAD_DIM, HEAD_DIM), lambda b: (b, 0, 0, 0)),
            pl.BlockSpec((1, N_UNITS, HEAD_DIM), lambda b: (b, 0, 0)),
            pl.BlockSpec((1, N_UNITS, LANES), lambda b: (b, 0, 0)),
        ]
        args += [c0, n0, m0]
    for name in _MIXER_WEIGHTS:
        operand, spec = _const_spec(weights[name])
        in_specs.append(spec)
        args.append(operand)
    out_shape = [
        jax.ShapeDtypeStruct((B, R, D_MODEL), F32),
        jax.ShapeDtypeStruct((B, R, D_MODEL), BF16),
        jax.ShapeDtypeStruct((B, R, LANES), F32),
        jax.ShapeDtypeStruct((B, n_blk * 8, MOE_BLK), F32),
        jax.ShapeDtypeStruct((B, n_blk * 8, LANES), F32),
    ]
    out_specs = [
        pl.BlockSpec((1, R, D_MODEL), lambda b: (b, 0, 0), **seq_mode),
        pl.BlockSpec((1, R, D_MODEL), lambda b: (b, 0, 0), **seq_mode),
        pl.BlockSpec((1, R, LANES), lambda b: (b, 0, 0)),
        pl.BlockSpec((1, n_blk * 8, MOE_BLK), lambda b: (b, 0, 0)),
        pl.BlockSpec((1, n_blk * 8, LANES), lambda b: (b, 0, 0)),
    ]
    if emit_state:
        out_shape += [
            jax.ShapeDtypeStruct((B, n_seq * N_UNITS, HEAD_DIM, HEAD_DIM), F32),
            jax.ShapeDtypeStruct((B, n_seq * N_UNITS, HEAD_DIM), F32),
            jax.ShapeDtypeStruct((B, n_seq * N_UNITS, LANES), F32),
        ]
        out_specs += [
            pl.BlockSpec((1, n_seq * N_UNITS, HEAD_DIM, HEAD_DIM), lambda b: (b, 0, 0, 0)),
            pl.BlockSpec((1, n_seq * N_UNITS, HEAD_DIM), lambda b: (b, 0, 0)),
            pl.BlockSpec((1, n_seq * N_UNITS, LANES), lambda b: (b, 0, 0)),
        ]
    scratch = [
        pltpu.VMEM((R, D_MLSTM), BF16),
        pltpu.VMEM((n_chunks, D_MLSTM, SUB), BF16),
        pltpu.VMEM((R, D_MLSTM), BF16),
        pltpu.VMEM((R, D_MLSTM), F32),
        pltpu.VMEM((n_chunks, 5 * N_UNITS, SUB), F32),
        pltpu.VMEM((R, D_MODEL), F32),
        pltpu.VMEM((R, D_MODEL), F32),
        pltpu.VMEM((R, D_MLSTM), F32),
        pltpu.VMEM((N_UNITS, HEAD_DIM, 2 * HEAD_DIM), F32),
        pltpu.VMEM((MIX_TM // P, P + 2 * CONV_PAD, D_CONV), F32),
    ]
    return pl.pallas_call(
        functools.partial(_mixer_kernel, R, T, P, has_state, emit_state, mod_index),
        grid=(B,),
        in_specs=in_specs,
        out_specs=out_specs,
        out_shape=out_shape,
        scratch_shapes=scratch,
        compiler_params=pltpu.CompilerParams(
            dimension_semantics=("arbitrary",), vmem_limit_bytes=VMEM_LIMIT),
        name="mixer_T%d" % T,
    )(*args)


def _dest_in_block(group, rank, starts):
    dest = rank
    for g in range(N_GROUPS):
        dest = dest + jnp.where(group == float(g), starts[g], 0.0)
    return dest


def _copy_segments(src_refs, dst_refs, src_starts, dst_starts, n_pieces):
    for g in range(N_GROUPS):
        def body(k, carry, g=g):
            s = pl.multiple_of(src_starts[g] + k * ROW_ALIGN, ROW_ALIGN)
            d = pl.multiple_of(dst_starts[g] + k * ROW_ALIGN, ROW_ALIGN)
            for src, dst in zip(src_refs, dst_refs):
                dst[pl.ds(d, ROW_ALIGN), :] = src[pl.ds(s, ROW_ALIGN), :]
            return carry
        lax.fori_loop(0, n_pieces[g], body, 0)


def _plan_segments(n_blocks, n_tiles, cnt_ref, start_ref, npiece_ref, off_ref, tgroup_ref, tvalid_ref):
    align_shift = ROW_ALIGN.bit_length() - 1
    tile_shift = MOE_TM.bit_length() - 1

    def block_starts(blk, carry):
        row = jnp.int32(0)
        for g in range(N_GROUPS):
            n = lax.shift_right_logical(cnt_ref[blk * N_GROUPS + g] + (ROW_ALIGN - 1), align_shift)
            npiece_ref[blk * N_GROUPS + g] = n
            start_ref[blk * N_GROUPS + g] = row
            row = row + n * ROW_ALIGN
        return carry

    lax.fori_loop(0, n_blocks, block_starts, 0)

    base_row = jnp.int32(0)
    base_tile = jnp.int32(0)
    last_group = jnp.int32(0)
    for g in range(N_GROUPS):
        def seg_offsets(blk, row, g=g, base_row=base_row):
            off_ref[blk * N_GROUPS + g] = base_row + row
            return row + npiece_ref[blk * N_GROUPS + g] * ROW_ALIGN

        rows = lax.fori_loop(0, n_blocks, seg_offsets, jnp.int32(0))
        tiles = lax.shift_right_logical(rows + (MOE_TM - 1), tile_shift)

        def mark_tiles(t, carry, g=g, base_tile=base_tile):
            tgroup_ref[base_tile + t] = g
            tvalid_ref[base_tile + t] = 1
            return carry

        lax.fori_loop(0, tiles, mark_tiles, 0)
        last_group = jnp.where(tiles > 0, g, last_group)
        base_row = base_row + tiles * MOE_TM
        base_tile = base_tile + tiles

    def mark_unused(t, carry):
        tgroup_ref[t] = last_group
        tvalid_ref[t] = 0
        return carry

    lax.fori_loop(base_tile, n_tiles, mark_unused, 0)


def _dispatch_kernel(n_ctx_blocks, n_blocks, n_tiles, cnt_ref,
                     h2c_ref, h2l_ref, cbc_ref, cbl_ref, rtc_ref, rtl_ref,
                     xs_ref, cs_ref, start_ref, npiece_ref, off_ref, tgroup_ref, tvalid_ref,
                     sx_s, sc_s):
    b = pl.program_id(0)
    is_ctx = b < n_ctx_blocks

    @pl.when(b == 0)
    def _():
        _plan_segments(n_blocks, n_tiles, cnt_ref, start_ref, npiece_ref, off_ref, tgroup_ref, tvalid_ref)
        xs_ref[...] = jnp.zeros_like(xs_ref)
        cs_ref[...] = jnp.zeros_like(cs_ref)

    h2 = jnp.where(is_ctx, h2c_ref[0], h2l_ref[0])
    cb = jnp.where(is_ctx, cbc_ref[0], cbl_ref[0])
    rt = jnp.where(is_ctx, rtc_ref[0], rtl_ref[0])
    starts = [start_ref[b * N_GROUPS + g] for g in range(N_GROUPS)]
    dest = _dest_in_block(rt[0:1, :], rt[1:2, :], [s.astype(F32) for s in starts])
    row = lax.broadcasted_iota(jnp.int32, (SORT_ROWS, MOE_BLK), 0).astype(F32)
    perm = (row == dest).astype(F32).astype(BF16)
    cb_hi = cb.astype(BF16)
    cb_lo = (cb - cb_hi.astype(F32)).astype(BF16)
    sx_s[...] = _dot(perm, h2).astype(BF16)
    sc_s[...] = _dot(perm, jnp.concatenate([cb_hi, cb_lo], axis=1)).astype(BF16)
    _copy_segments((sx_s, sc_s), (xs_ref, cs_ref), starts,
                   [off_ref[b * N_GROUPS + g] for g in range(N_GROUPS)],
                   [npiece_ref[b * N_GROUPS + g] for g in range(N_GROUPS)])


def _experts_kernel(tgroup_ref, tvalid_ref, xs_ref, cs_ref, wg_ref, wu_ref, wd_ref, ys_ref):
    i = pl.program_id(0)

    @pl.when(tvalid_ref[i] == 1)
    def _():
        x = xs_ref[...]
        comb = cs_ref[:, :LANES].astype(F32) + cs_ref[:, LANES:].astype(F32)
        lane = lax.broadcasted_iota(jnp.int32, comb.shape, 1)
        first = tgroup_ref[i] * EXPERTS_PER_GROUP
        acc = None
        for j in range(EXPERTS_PER_GROUP):
            gj = _dot(x, wg_ref[j].astype(BF16))
            uj = _dot(x, wu_ref[j].astype(BF16))
            cw = jnp.sum(jnp.where(lane == first + j, comb, 0.0), axis=1, keepdims=True)
            out = _dot((gj * _sigmoid(gj) * uj * cw).astype(BF16), wd_ref[j].astype(BF16))
            acc = out if acc is None else acc + out
        ys_ref[...] = acc.astype(BF16)

    @pl.when(tvalid_ref[i] == 0)
    def _():
        ys_ref[...] = jnp.zeros_like(ys_ref)


def _combine_kernel(n_ctx_blocks, blocks_per_lat_seq, start_ref, npiece_ref, off_ref,
                    x1c_ref, x1l_ref, cbc_ref, cbl_ref, ys_ref, mod_ref, gf_ref, yc_ref, yl_ref, loc_s):
    b = pl.program_id(0)
    is_ctx = b < n_ctx_blocks
    starts = [start_ref[b * N_GROUPS + g] for g in range(N_GROUPS)]
    loc_s[...] = jnp.zeros_like(loc_s)
    _copy_segments((ys_ref,), (loc_s,), [off_ref[b * N_GROUPS + g] for g in range(N_GROUPS)], starts,
                   [npiece_ref[b * N_GROUPS + g] for g in range(N_GROUPS)])
    cb = jnp.where(is_ctx, cbc_ref[0], cbl_ref[0])
    dest = _dest_in_block(cb[:, ROUTE_GROUP_LANE:ROUTE_GROUP_LANE + 1],
                          cb[:, ROUTE_RANK_LANE:ROUTE_RANK_LANE + 1],
                          [s.astype(F32) for s in starts])
    col = lax.broadcasted_iota(jnp.int32, (MOE_BLK, SORT_ROWS), 1).astype(F32)
    unperm = (col == dest).astype(F32).astype(BF16)
    moe = _dot(unperm, loc_s[...])
    x1 = jnp.where(is_ctx, x1c_ref[0], x1l_ref[0])
    mrow = jnp.where(is_ctx, 0, 1 + jnp.maximum(b - n_ctx_blocks, 0) // blocks_per_lat_seq)
    x2 = x1 + mod_ref[N_ADA - 1, pl.ds(mrow, 1), :] * moe
    y = x2 * lax.rsqrt(jnp.mean(x2 * x2, axis=-1, keepdims=True) + EPS) * gf_ref[...]

    @pl.when(is_ctx)
    def _():
        yc_ref[0] = y

    @pl.when(jnp.logical_not(is_ctx))
    def _():
        yl_ref[0] = y


def _moe(x1c, x1l, h2c, h2l, cbc, cbl, rtc, rtl, cnt, mod, blocks_per_lat_seq, wg, wu, wd, gf):
    nc, nl = x1c.shape[0], x1l.shape[0]
    nb = nc + nl
    n_rows_max = nb * MOE_BLK + nb * N_GROUPS * (ROW_ALIGN - 1) + N_GROUPS * (MOE_TM - ROW_ALIGN)
    n_tiles = -(-n_rows_max // MOE_TM)
    ns = n_tiles * MOE_TM

    cmap = lambda b, *_: (jnp.minimum(b, nc - 1), 0, 0)
    lmap = lambda b, *_: (jnp.maximum(b - nc, 0), 0, 0)
    whole = lambda *_: (0, 0)
    once = {"pipeline_mode": pl.Buffered(1)}
    arb = pltpu.CompilerParams(dimension_semantics=("arbitrary",), vmem_limit_bytes=VMEM_LIMIT)
    smem = pl.BlockSpec(memory_space=pltpu.SMEM)
    seg_i32 = jax.ShapeDtypeStruct((nb * N_GROUPS,), jnp.int32)
    tile_i32 = jax.ShapeDtypeStruct((n_tiles,), jnp.int32)

    xs, cs, start, npiece, off, tgroup, tvalid = pl.pallas_call(
        functools.partial(_dispatch_kernel, nc, nb, n_tiles),
        grid_spec=pltpu.PrefetchScalarGridSpec(
            num_scalar_prefetch=1, grid=(nb,),
            in_specs=[
                pl.BlockSpec((1, MOE_BLK, D_MODEL), cmap), pl.BlockSpec((1, MOE_BLK, D_MODEL), lmap),
                pl.BlockSpec((1, MOE_BLK, LANES), cmap), pl.BlockSpec((1, MOE_BLK, LANES), lmap),
                pl.BlockSpec((1, 8, MOE_BLK), cmap), pl.BlockSpec((1, 8, MOE_BLK), lmap),
            ],
            out_specs=[pl.BlockSpec((ns, D_MODEL), whole, **once), pl.BlockSpec((ns, 2 * LANES), whole, **once),
                       smem, smem, smem, smem, smem],
            scratch_shapes=[pltpu.VMEM((SORT_ROWS, D_MODEL), BF16), pltpu.VMEM((SORT_ROWS, 2 * LANES), BF16)],
        ),
        out_shape=[jax.ShapeDtypeStruct((ns, D_MODEL), BF16), jax.ShapeDtypeStruct((ns, 2 * LANES), BF16),
                   seg_i32, seg_i32, seg_i32, tile_i32, tile_i32],
        compiler_params=arb,
        name="moe_dispatch",
    )(cnt, h2c, h2l, cbc, cbl, rtc, rtl)

    wmap = lambda i, tg, tv: (tg[i], 0, 0)
    ys = pl.pallas_call(
        _experts_kernel,
        grid_spec=pltpu.PrefetchScalarGridSpec(
            num_scalar_prefetch=2, grid=(n_tiles,),
            in_specs=[
                pl.BlockSpec((MOE_TM, D_MODEL), lambda i, *_: (i, 0)),
                pl.BlockSpec((MOE_TM, 2 * LANES), lambda i, *_: (i, 0)),
                pl.BlockSpec((EXPERTS_PER_GROUP, D_MODEL, D_EXPERT), wmap),
                pl.BlockSpec((EXPERTS_PER_GROUP, D_MODEL, D_EXPERT), wmap),
                pl.BlockSpec((EXPERTS_PER_GROUP, D_EXPERT, D_MODEL), wmap),
            ],
            out_specs=pl.BlockSpec((MOE_TM, D_MODEL), lambda i, *_: (i, 0)),
        ),
        out_shape=jax.ShapeDtypeStruct((ns, D_MODEL), BF16),
        compiler_params=arb,
        name="moe_experts",
    )(tgroup, tvalid, xs, cs, wg, wu, wd)

    yc, yl = pl.pallas_call(
        functools.partial(_combine_kernel, nc, blocks_per_lat_seq),
        grid_spec=pltpu.PrefetchScalarGridSpec(
            num_scalar_prefetch=3, grid=(nb,),
            in_specs=[
                pl.BlockSpec((1, MOE_BLK, D_MODEL), cmap), pl.BlockSpec((1, MOE_BLK, D_MODEL), lmap),
                pl.BlockSpec((1, MOE_BLK, LANES), cmap), pl.BlockSpec((1, MOE_BLK, LANES), lmap),
                pl.BlockSpec((ns, D_MODEL), whole, **once),
                pl.BlockSpec(mod.shape, lambda *_: (0, 0, 0)),
                pl.BlockSpec((1, D_MODEL), whole),
            ],
            out_specs=[pl.BlockSpec((1, MOE_BLK, D_MODEL), cmap), pl.BlockSpec((1, MOE_BLK, D_MODEL), lmap)],
            scratch_shapes=[pltpu.VMEM((SORT_ROWS, D_MODEL), BF16)],
        ),
        out_shape=[jax.ShapeDtypeStruct((nc, MOE_BLK, D_MODEL), F32),
                   jax.ShapeDtypeStruct((nl, MOE_BLK, D_MODEL), F32)],
        compiler_params=arb,
        name="moe_combine",
    )(start, npiece, off, x1c, x1l, cbc, cbl, ys, mod, gf)
    return yc, yl


def _prep_weights(norm1_g, w_in, b_in, b_gates, w_dw, b_dw, conv_ln_g, conv_ln_b, w_conv_out,
                  mlstm_hn_g, w_mlstm_out, w_o, norm2_g, w_rg, b_rg, w_re, b_re):
    s_a = 2 * D_CONV
    s_q = s_a + D_MLSTM
    s_k = s_q + D_MLSTM
    s_v = s_k + D_MLSTM
    s_o = s_v + D_MLSTM
    s_g = s_o + 4 * N_HEADS
    row = lambda v: v.reshape(1, -1).astype(F32)
    w_t = w_in.T
    keep = [(0, s_q), (s_k, s_o), (s_g, w_in.shape[1])]
    wrow = _transpose_cast(w_t, [r for a, b in keep for r in range(a, b, WPREP_ROWS)])
    bg = (b_in[s_o:s_g] + b_gates.reshape(-1)).reshape(2, 2, N_HEADS).transpose(1, 0, 2).reshape(-1, 1)
    row_window = lambda start, n: _RowWindow(w_t, start, n)
    n_rt = N_EXPERTS + N_GROUPS
    wrt = jnp.pad(jnp.concatenate([w_re, w_rg], axis=1), ((0, 0), (0, LANES - n_rt)))
    wrt_hi = wrt.astype(BF16)
    wrt2 = jnp.concatenate([wrt_hi, (wrt - wrt_hi.astype(F32)).astype(BF16)], axis=1)
    brtT = jnp.pad(jnp.concatenate([b_re, b_rg]), (0, LANES - n_rt)).reshape(LANES, 1)
    return {
        "g1": row(norm1_g),
        "wrow": wrow, "bag": row(b_in[:s_a]), "bq": row(b_in[s_a:s_q]),
        "wkT": row_window(s_q, D_MLSTM), "bk": b_in[s_q:s_k].reshape(-1, 1),
        "bv": row(b_in[s_k:s_v]), "bog": row(b_in[s_v:s_o]),
        "wgifT": row_window(s_o, 4 * N_HEADS), "bgifT": bg, "bgm": row(b_in[s_g:]),
        "wdw": w_dw.astype(F32), "bdw": row(b_dw), "lng": row(conv_ln_g), "lnb": row(conv_ln_b),
        "wco": w_conv_out.astype(BF16), "hng": row(mlstm_hn_g), "wmo": w_mlstm_out.astype(BF16),
        "wo": w_o.astype(BF16), "g2": row(norm2_g), "wrt2": wrt2, "brtT": brtT,
    }


def kernel(x_prompt, x_sample, state_C, state_n, state_m, c, c_ctx, norm1_g, w_ada, b_ada, w_in, b_in, b_gates, w_dw, b_dw, conv_ln_g, conv_ln_b, w_conv_out, mlstm_hn_g, w_mlstm_out, w_o, norm2_g, w_rg, b_rg, w_re, b_re, w_e_gate, w_e_up, w_e_down, norm_final_g):
    B, S, _ = x_prompt.shape
    Bd, Sd, _ = x_sample.shape
    assert w_ada.shape[0] == 1, "single trunk layer"
    assert S == SUB and Sd % SUB == 0

    cin = jnp.concatenate([c_ctx[None, :], c, jnp.zeros((8 - 1 - Bd, D_MODEL), F32)], axis=0)
    mod = _ada(cin, w_ada[0], b_ada[0].reshape(1, -1))

    wts = _prep_weights(norm1_g[0], w_in[0], b_in[0], b_gates[0], w_dw[0], b_dw[0], conv_ln_g[0],
                        conv_ln_b[0], w_conv_out[0], mlstm_hn_g[0], w_mlstm_out[0], w_o[0],
                        norm2_g[0], w_rg[0], b_rg[0], w_re[0], b_re[0])

    x1p, h2p, cbp, rtp, cntp, c_new, n_new, m_new = _mixer(
        x_prompt.reshape(B * S // MIX_TM, MIX_TM, D_MODEL), S, mod, lambda b: 0, wts, P=S, emit_state=True)

    m0 = jnp.broadcast_to(state_m[:, 0].reshape(Bd, N_UNITS, 1), (Bd, N_UNITS, LANES))
    state = (state_C[:, 0].reshape(Bd, N_UNITS, HEAD_DIM, HEAD_DIM), state_n[:, 0].reshape(Bd, N_UNITS, HEAD_DIM), m0)
    x1s, h2s, cbs, rts, cnts = _mixer(x_sample, Sd, mod, lambda b: 1 + b, wts, P=GRID_W, state=state)

    nc, nl = B * S // MOE_BLK, Bd * Sd // MOE_BLK
    blk = lambda a, n: a.reshape(n, MOE_BLK, a.shape[-1])
    cnt = jnp.concatenate([cntp.reshape(nc, 8, LANES)[:, :N_GROUPS, 0],
                           cnts.reshape(nl, 8, LANES)[:, :N_GROUPS, 0]], axis=0)
    yp, ys = _moe(blk(x1p, nc), blk(x1s, nl), blk(h2p, nc), blk(h2s, nl), blk(cbp, nc), blk(cbs, nl),
                  rtp.reshape(nc, 8, MOE_BLK), rts.reshape(nl, 8, MOE_BLK),
                  cnt.astype(jnp.int32).reshape(-1), mod, Sd // MOE_BLK, w_e_gate[0], w_e_up[0], w_e_down[0],
                  norm_final_g.reshape(1, -1))

    return (yp.reshape(B, S, D_MODEL), ys.reshape(Bd, Sd, D_MODEL),
            c_new.reshape(B, 1, 2, N_HEADS, HEAD_DIM, HEAD_DIM),
            n_new.reshape(B, 1, 2, N_HEADS, HEAD_DIM),
            m_new[:, :, 0].reshape(B, 1, 2, N_HEADS))
```

```python
import functools
from typing import NamedTuple

import jax
import jax.numpy as jnp
from jax import lax
from jax.experimental import pallas as pl
from jax.experimental.pallas import tpu as pltpu

D_MODEL = 1024
D_CONV = 512
CONV_K = 31
D_MLSTM = 512
N_HEADS = 4
HEAD_DIM = D_MLSTM // N_HEADS
N_GROUPS = 4
EXPERTS_PER_GROUP = 4
N_EXPERTS = N_GROUPS * EXPERTS_PER_GROUP
D_EXPERT = 256
N_ADA = 6
EPS = 1e-6
GRID_W = 64

LANES = 128
SUB = 256
CONV_PAD = 16
CONV_RB = 64
N_UNITS = 2 * N_HEADS
ROW_ALIGN = 16
MOE_TM = 512
MIX_TM = 512
MOE_BLK = MIX_TM
SORT_ROWS = MOE_BLK + N_GROUPS * ROW_ALIGN
WPREP_ROWS = 512
ROUTE_GROUP_LANE = N_EXPERTS
ROUTE_RANK_LANE = N_EXPERTS + 1
VMEM_LIMIT = 58 * 1024 * 1024

BF16 = jnp.bfloat16
F32 = jnp.float32
NT_DIMS = (((1,), (1,)), ((), ()))


def _dot(a, b):
    return jnp.dot(a, b, preferred_element_type=F32)


def _dot_nt(a, b, precision=None):
    return lax.dot_general(a, b, NT_DIMS, preferred_element_type=F32, precision=precision)


def _sigmoid(x):
    return 0.5 * jnp.tanh(0.5 * x) + 0.5


def _log_sigmoid(x):
    return jnp.minimum(x, 0.0) - jnp.log1p(jnp.exp(-jnp.abs(x)))


def _split3(x):
    hi = x.astype(BF16).astype(F32)
    r1 = x - hi
    mid = r1.astype(BF16).astype(F32)
    lo = (r1 - mid).astype(BF16).astype(F32)
    return hi, mid, lo


def _ada_kernel(cctx_ref, c_ref, w_ref, b_ref, o_ref):
    n = 1 + c_ref.shape[0]
    c = jnp.concatenate([cctx_ref[...], c_ref[...], jnp.zeros((8 - n, D_MODEL), F32)], axis=0)
    s = (c * _sigmoid(c)).astype(BF16)
    o_ref[0] = _dot(s, w_ref[...].astype(BF16)) + b_ref[...]


def _ada(c_ctx, c, w_ada, b_ada):
    return pl.pallas_call(
        _ada_kernel,
        grid=(N_ADA,),
        in_specs=[
            pl.BlockSpec(c_ctx.shape, lambda j: (0, 0)),
            pl.BlockSpec(c.shape, lambda j: (0, 0)),
            pl.BlockSpec((D_MODEL, D_MODEL), lambda j: (0, j)),
            pl.BlockSpec((1, D_MODEL), lambda j: (0, j)),
        ],
        out_specs=pl.BlockSpec((1, 8, D_MODEL), lambda j: (j, 0, 0)),
        out_shape=jax.ShapeDtypeStruct((N_ADA, 8, D_MODEL), F32),
        compiler_params=pltpu.CompilerParams(dimension_semantics=("arbitrary",)),
        name="ada",
    )(c_ctx, c, w_ada, b_ada)


def _transpose_cast_kernel(starts_ref, wt_ref, o_ref):
    o_ref[...] = wt_ref[...].astype(BF16).T


def _transpose_cast(w_t, row_starts):
    n, k = len(row_starts), w_t.shape[1]
    return pl.pallas_call(
        _transpose_cast_kernel,
        grid_spec=pltpu.PrefetchScalarGridSpec(
            num_scalar_prefetch=1, grid=(n,),
            in_specs=[pl.BlockSpec((pl.Element(WPREP_ROWS), pl.Element(k)), lambda j, starts: (starts[j] * 8, 0))],
            out_specs=pl.BlockSpec((k, WPREP_ROWS), lambda j, starts: (0, j)),
        ),
        out_shape=jax.ShapeDtypeStruct((k, n * WPREP_ROWS), BF16),
        compiler_params=pltpu.CompilerParams(dimension_semantics=("arbitrary",)),
        name="transpose_cast",
    )(jnp.array([r // 8 for r in row_starts], jnp.int32), w_t)


WROW_OFFSET = {"wq": 2 * D_CONV, "wv": 2 * D_CONV + D_MLSTM, "wog": 2 * D_CONV + 2 * D_MLSTM,
               "wgm": 2 * D_CONV + 3 * D_MLSTM}

_MIXER_WEIGHTS = (
    "g1", "wrow", "bag", "bq", "wkT", "bk", "bv", "bog",
    "wgifT", "bgifT", "bgm", "wdw", "bdw", "lng", "lnb",
    "wco", "hng", "wmo", "wo", "g2", "wrt2", "brtT",
)


def _zero_after(x):
    bits = lax.bitcast_convert_type(x, jnp.uint32)
    bits = lax.shift_right_logical(lax.shift_right_logical(bits, jnp.uint32(16)), jnp.uint32(16))
    return lax.bitcast_convert_type(bits, F32)[0:1, :]


def _conv_block(upad_s, seg, base, cs, wdw_ref, bdw_ref, after=None):
    sub = 8
    first = CONV_PAD - CONV_K // 2
    acc = jnp.broadcast_to(bdw_ref[0:1, cs], (CONV_RB, LANES))
    for r in range(sub):
        z = None
        for a in range((CONV_K + first + sub - 1) // sub):
            j = sub * a + r - first
            if 0 <= j < CONV_K:
                lo = base + sub * a
                tap = wdw_ref[j:j + 1, cs] if after is None else wdw_ref[j:j + 1, cs] + after
                term = tap * upad_s[seg, lo:lo + CONV_RB + sub, cs]
                z = term if z is None else z + term
        acc = acc + z[r:r + CONV_RB, :]
    return acc


def _mixer_kernel(R, T, P, has_state, emit_state, mod_index, *refs):
    L = SUB
    n_mt = R // MIX_TM
    cpm = MIX_TM // L
    n_seq = R // T
    cps = T // L
    nseg = MIX_TM // P
    assert not has_state or n_seq == 1
    it = iter(refs)
    x_ref = next(it)
    mod_ref = next(it)
    if has_state:
        c0_ref = next(it)
        n0_ref = next(it)
        m0_ref = next(it)
    w = {name: next(it) for name in _MIXER_WEIGHTS}
    x1_ref = next(it)
    h2_ref = next(it)
    comb_ref = next(it)
    route_ref = next(it)
    cnt_ref = next(it)
    if emit_state:
        cout_ref = next(it)
        nout_ref = next(it)
        mout_ref = next(it)
    (q_s, kT_s, v_s, so_s, scan_s, ma_s, sgb_s, hm_s, cst_s, upad_s) = [next(it) for _ in range(10)]

    cond_row = mod_index(pl.program_id(0))

    def mod_row(i):
        return mod_ref[i, pl.ds(cond_row, 1), :]

    zpad = jnp.zeros((CONV_PAD, D_CONV), F32)
    for seg in range(nseg):
        upad_s[seg, 0:CONV_PAD, :] = zpad
        upad_s[seg, CONV_PAD + P:CONV_PAD + P + CONV_PAD, :] = zpad

    t_idx = lax.broadcasted_iota(jnp.int32, (L, L), 0)
    s_idx = lax.broadcasted_iota(jnp.int32, (L, L), 1)
    lower = s_idx <= t_idx
    upper = s_idx >= t_idx
    triu_b = upper.astype(F32).astype(BF16)
    lane_u = lax.broadcasted_iota(jnp.int32, (N_UNITS, L), 1)
    is_bwd = lax.broadcasted_iota(jnp.int32, (N_UNITS, L), 0) >= N_HEADS

    def gate_scan(g):
        gi, lf = g[:N_UNITS], _log_sigmoid(g[N_UNITS:])
        pr = _dot(jnp.concatenate(_split3(lf), axis=0).astype(BF16), triu_b)
        pre = pr[0:N_UNITS] + pr[N_UNITS:2 * N_UNITS] + pr[2 * N_UNITS:]
        tot = pre[:, L - 1:L]
        bsum = jnp.where(is_bwd, tot - pre + lf, pre)
        a = gi - bsum
        pm, sm, k = a, a, 1
        while k < L:
            pm = jnp.where(lane_u >= k, jnp.maximum(pm, pltpu.roll(pm, k, axis=1)), pm)
            sm = jnp.where(lane_u < L - k, jnp.maximum(sm, pltpu.roll(sm, L - k, axis=1)), sm)
            k *= 2
        wide = lambda v: jnp.broadcast_to(v, (N_UNITS, L))
        return jnp.concatenate([a, jnp.where(is_bwd, sm, pm), bsum, wide(tot),
                                wide(jnp.max(a, axis=1, keepdims=True))], axis=0)

    def phase1(i, carry):
        r0 = pl.multiple_of(i * MIX_TM, MIX_TM)
        rows = pl.ds(r0, MIX_TM)
        x = x_ref[0, rows, :]
        xn = x * lax.rsqrt(jnp.mean(x * x, axis=-1, keepdims=True) + EPS) * w["g1"][...]
        hb = (xn * (1.0 + mod_row(1)) + mod_row(0)).astype(BF16)

        gates = _dot_nt(w["wgifT"][...].astype(BF16), hb)
        gates = jnp.concatenate([gates[d * 2 * N_HEADS + g * N_HEADS:d * 2 * N_HEADS + (g + 1) * N_HEADS]
                                 for g in range(2) for d in range(2)], axis=0) + w["bgifT"][...]
        for j in range(cpm):
            scan_s[i * cpm + j] = gate_scan(gates[:, j * L:(j + 1) * L])
        ag = _dot(hb, w["wrow"][:, :2 * D_CONV]) + w["bag"][...]
        u = ag[:, :D_CONV] * _sigmoid(ag[:, D_CONV:])
        for seg in range(nseg):
            upad_s[seg, CONV_PAD:CONV_PAD + P, :] = u[seg * P:(seg + 1) * P, :]

        def proj(name, bias, c0, gate, width=2 * LANES):
            w0 = WROW_OFFSET[name] + c0
            b = w[bias][:, c0:c0 + width]
            if gate is not None:
                b = b + jnp.concatenate([gate] * (width // LANES), axis=1)
            return _dot(hb, w["wrow"][:, w0:w0 + width]) + b

        last = lambda z: z[-8:, -LANES:]

        def gm_a(c0, gate):
            z = proj("wgm", "bgm", c0, gate)
            ma_s[rows, c0:c0 + 2 * LANES] = _sigmoid(z)
            return last(z)

        def gm_b(c0, gate):
            z = proj("wgm", "bgm", D_MODEL + c0, gate)
            sgb_s[rows, c0:c0 + 2 * LANES] = _sigmoid(z)
            return last(z)

        def q_part(c0, gate):
            z = proj("wq", "bq", c0, gate)
            q_s[rows, c0:c0 + 2 * LANES] = (z * (HEAD_DIM ** -0.5)).astype(BF16)
            return last(z)

        def v_part(c0, gate):
            z = proj("wv", "bv", c0, gate)
            v_s[rows, c0:c0 + 2 * LANES] = z.astype(BF16)
            return last(z)

        def o_part(c0, gate):
            z = proj("wog", "bog", c0, gate)
            so_s[rows, c0:c0 + 2 * LANES] = _sigmoid(z)
            return last(z)

        def k_part(c0, gate):
            rs = slice(c0, c0 + 2 * LANES)
            b = w["bk"][rs, :] if gate is None else w["bk"][rs, :] + gate[:, 0:1]
            z = _dot_nt(w["wkT"][rs, :].astype(BF16), hb) + b
            kt = z.astype(BF16)
            for j in range(cpm):
                kT_s[i * cpm + j, rs, :] = kt[:, j * L:(j + 1) * L]
            return last(z)

        jobs = ([functools.partial(gm_a, c0) for c0 in range(0, D_MODEL, 2 * LANES)]
                + [functools.partial(gm_b, c0) for c0 in range(0, D_MODEL, 2 * LANES)]
                + [functools.partial(f, c0) for f in (q_part, k_part, v_part, o_part)
                   for c0 in range(0, D_MLSTM, 2 * LANES)])
        n_jobs = len(jobs)
        conv = {}
        after = None
        n_pieces = (D_CONV // LANES) * nseg * (P // CONV_RB)
        for cb in range(D_CONV // LANES):
            cs = slice(cb * LANES, (cb + 1) * LANES)
            for seg in range(nseg):
                for rb in range(P // CONV_RB):
                    blk = _conv_block(upad_s, seg, rb * CONV_RB, cs, w["wdw"], w["bdw"], after)
                    conv[(cb, seg, rb)] = blk
                    if jobs and len(conv) * n_jobs >= (n_jobs - len(jobs) + 1) * n_pieces:
                        after = _zero_after(jobs.pop(0)(_zero_after(blk[-8:, :])))
        for job in jobs:
            job(None)
        cu = jnp.concatenate(
            [jnp.concatenate([conv[(cb, seg, rb)] for seg in range(nseg) for rb in range(P // CONV_RB)], axis=0)
             for cb in range(D_CONV // LANES)], axis=1)
        mu = jnp.mean(cu, axis=-1, keepdims=True)
        cc = cu - mu
        cn = cc * lax.rsqrt(jnp.mean(cc * cc, axis=-1, keepdims=True) + EPS) * w["lng"][...] + w["lnb"][...]
        ca = (cn * _sigmoid(cn)).astype(BF16)
        ma_s[rows, :] = ma_s[rows, :] * _dot(ca, w["wco"][...])
        return carry

    if n_mt == 1:
        phase1(0, 0)
    else:
        lax.fori_loop(0, n_mt, phase1, 0)

    ones_col = (lax.broadcasted_iota(jnp.int32, (L, HEAD_DIM), 1) == 0).astype(F32).astype(BF16)
    pad_rows = jnp.zeros((LANES - 3 * N_UNITS, L), F32)

    def gate_prep(c, m_vec):
        sc = scan_s[c]
        a, run_max, bsum = sc[0:N_UNITS], sc[N_UNITS:2 * N_UNITS], sc[2 * N_UNITS:3 * N_UNITS]
        tot, a_max = sc[3 * N_UNITS:4 * N_UNITS, 0:1], sc[4 * N_UNITS:5 * N_UNITS, 0:1]
        big_m = jnp.maximum(m_vec, run_max)
        m_end = jnp.maximum(m_vec, a_max)
        cols = jnp.concatenate(
            [big_m, jnp.exp(m_vec - big_m), jnp.exp(-bsum - big_m), pad_rows], axis=0).T
        return a, cols, jnp.exp(a - m_end), jnp.exp(m_vec - m_end), tot + m_end

    def unit_group(dirs, c, prep, first_chunk, want_state):
        a, cols, wk, decay, _ = prep
        rows = slice(c * L, (c + 1) * L)
        heads = range(N_HEADS)
        units = [(d, hd) for d in dirs for hd in heads]
        hs = [slice(hd * HEAD_DIM, (hd + 1) * HEAD_DIM) for hd in heads]
        idx = {u: u[0] * N_HEADS + u[1] for u in units}
        col = lambda k, u: cols[:, k * N_UNITS + idx[u]:k * N_UNITS + idx[u] + 1]
        row = lambda arr, u: arr[idx[u]:idx[u] + 1, :]
        chained = has_state or not first_chunk
        qc = [q_s[rows, hs[hd]] for hd in heads]
        kTc = [kT_s[c, hs[hd], :] for hd in heads]
        vaug = [jnp.concatenate([v_s[rows, hs[hd]], ones_col], axis=1) for hd in heads]
        qk = [_dot(qc[hd], kTc[hd]) for hd in heads]
        s_mat = {u: (qk[u[1]] * jnp.where(lower if u[0] == 0 else upper, jnp.exp(row(a, u) - col(0, u)), 0.0)
                     ).astype(BF16) for u in units}
        nd = {u: _dot(s_mat[u], vaug[u[1]]) for u in units}
        if chained:
            nd = {u: nd[u] + col(1, u) * _dot(qc[u[1]], cst_s[idx[u]].astype(BF16)) for u in units}
        h = {u: nd[u][:, :HEAD_DIM] * (1.0 / jnp.maximum(jnp.abs(nd[u][:, HEAD_DIM:HEAD_DIM + 1]), col(2, u)))
             for u in units}
        for hd in heads:
            total = h[(dirs[0], hd)]
            for d in dirs[1:]:
                total = total + h[(d, hd)]
            if dirs[0] == 0:
                hm_s[rows, hs[hd]] = total
            else:
                hm_s[rows, hs[hd]] = hm_s[rows, hs[hd]] + total
        if want_state:
            kw = {u: (kTc[u[1]].astype(F32) * row(wk, u)).astype(BF16) for u in units}
            upd = {u: _dot(kw[u], vaug[u[1]]) for u in units}
            for u in units:
                cst_s[idx[u]] = (upd[u] + row(decay, u) * cst_s[idx[u]]) if chained else upd[u]

    dir_rows = lax.broadcasted_iota(jnp.int32, (N_UNITS, 1), 0) >= N_HEADS
    for seq in range(n_seq):
        if has_state:
            n_cols = jnp.concatenate([n0_ref[0], jnp.zeros((LANES - N_UNITS, HEAD_DIM), F32)], axis=0).T
            first_lane = lax.broadcasted_iota(jnp.int32, (HEAD_DIM, HEAD_DIM), 1) == 0
            for idx in range(N_UNITS):
                cst_s[idx, :, :HEAD_DIM] = c0_ref[0, idx]
                cst_s[idx, :, HEAD_DIM:] = jnp.where(first_lane, n_cols[:, idx:idx + 1], 0.0)
            unit_row = lax.broadcasted_iota(jnp.int32, (N_UNITS, 1), 0)
            m_vec = jnp.zeros((N_UNITS, 1), F32)
            for idx in range(N_UNITS):
                m_vec = jnp.where(unit_row == idx, m0_ref[pl.program_id(0), idx], m_vec)
        else:
            m_vec = jnp.zeros((N_UNITS, 1), F32)
        if cps == 1:
            prep = gate_prep(seq, m_vec)
            unit_group([0, 1], seq, prep, True, emit_state)
            m_vec = prep[4]
        else:
            for d in range(2):
                order = list(range(cps)) if d == 0 else list(range(cps - 1, -1, -1))
                for pos, c in enumerate(order):
                    prep = gate_prep(seq * cps + c, m_vec)
                    unit_group([d], seq * cps + c, prep, pos == 0, emit_state or pos < cps - 1)
                    m_vec = jnp.where(dir_rows == (d == 1), prep[4], m_vec)
        if emit_state:
            for idx in range(N_UNITS):
                caug = cst_s[idx]
                cout_ref[0, seq * N_UNITS + idx] = caug[:, :HEAD_DIM]
                nout_ref[0, seq * N_UNITS + idx:seq * N_UNITS + idx + 1, :] = caug[:, HEAD_DIM:].T[0:1, :]
            mout_ref[0, seq * N_UNITS:(seq + 1) * N_UNITS, :] = jnp.broadcast_to(m_vec, (N_UNITS, LANES))

    e_iota = lax.broadcasted_iota(jnp.int32, (LANES, MIX_TM), 0)
    g_of_e = lax.shift_right_logical(e_iota, 2)
    j_of_e = lax.bitwise_and(e_iota, EXPERTS_PER_GROUP - 1)
    r8 = lax.broadcasted_iota(jnp.int32, (8, MIX_TM), 0)
    before_b = (lax.broadcasted_iota(jnp.int32, (MOE_BLK, MOE_BLK), 0)
                < lax.broadcasted_iota(jnp.int32, (MOE_BLK, MOE_BLK), 1)).astype(F32).astype(BF16)

    def phase3(i, carry):
        r0 = pl.multiple_of(i * MIX_TM, MIX_TM)
        rows = pl.ds(r0, MIX_TM)
        hm = hm_s[rows, :]
        heads = []
        for hd in range(N_HEADS):
            hh = hm[:, hd * HEAD_DIM:(hd + 1) * HEAD_DIM]
            heads.append(hh * lax.rsqrt(jnp.mean(hh * hh, axis=-1, keepdims=True) + EPS))
        hn = jnp.concatenate(heads, axis=1) * w["hng"][...]
        hb2 = (so_s[rows, :] * hn).astype(BF16)
        br_b = _dot(hb2, w["wmo"][...])
        mixed = (ma_s[rows, :] + sgb_s[rows, :] * br_b).astype(BF16)
        x1 = x_ref[0, rows, :] + mod_row(2) * _dot(mixed, w["wo"][...])
        x1_ref[0, rows, :] = x1
        xn = x1 * lax.rsqrt(jnp.mean(x1 * x1, axis=-1, keepdims=True) + EPS) * w["g2"][...]
        h2 = xn * (1.0 + mod_row(4)) + mod_row(3)
        h2_ref[0, rows, :] = h2.astype(BF16)

        h2_hi = h2.astype(BF16)
        h2_lo = (h2 - h2_hi.astype(F32)).astype(BF16)
        lg = _dot(h2_hi, w["wrt2"][...])
        lg = lg[:, :LANES] + lg[:, LANES:] + _dot(h2_lo, w["wrt2"][:, :LANES])
        lt = lg.T + w["brtT"][...]
        gl = [lt[N_EXPERTS + g:N_EXPERTS + g + 1, :] for g in range(N_GROUPS)]
        best, gsel = gl[0], jnp.zeros((1, MIX_TM), jnp.int32)
        for g in range(1, N_GROUPS):
            better = gl[g] > best
            gsel = jnp.where(better, g, gsel)
            best = jnp.where(better, gl[g], best)
        gp_sel = 1.0 / sum(jnp.exp(v - best) for v in gl)
        el = []
        for j in range(EXPERTS_PER_GROUP):
            v = lt[j:j + 1, :]
            for g in range(1, N_GROUPS):
                r = g * EXPERTS_PER_GROUP + j
                v = jnp.where(gsel == g, lt[r:r + 1, :], v)
            el.append(v)
        l1, e1 = el[0], jnp.zeros((1, MIX_TM), jnp.int32)
        for j in range(1, EXPERTS_PER_GROUP):
            better = el[j] > l1
            e1 = jnp.where(better, j, e1)
            l1 = jnp.where(better, el[j], l1)
        l2 = jnp.full((1, MIX_TM), -jnp.inf, F32)
        e2 = jnp.zeros((1, MIX_TM), jnp.int32)
        for j in range(EXPERTS_PER_GROUP):
            better = jnp.logical_and(e1 != j, el[j] > l2)
            e2 = jnp.where(better, j, e2)
            l2 = jnp.where(better, el[j], l2)
        r2 = jnp.exp(l2 - l1)
        wt1 = gp_sel / (1.0 + r2)
        wt2 = gp_sel * r2 / (1.0 + r2)
        in_group = g_of_e == gsel
        comb_t = (jnp.where(jnp.logical_and(in_group, j_of_e == e1), wt1, 0.0)
                  + jnp.where(jnp.logical_and(in_group, j_of_e == e2), wt2, 0.0))

        onehot = (r8 == gsel).astype(F32)
        gsel_f = gsel.astype(F32)
        rank = jnp.sum(onehot * _dot(onehot.astype(BF16), before_b), axis=0, keepdims=True)
        r8rows = pl.ds(pl.multiple_of(i * 8, 8), 8)
        route_ref[0, r8rows, :] = jnp.where(r8 == 0, gsel_f, jnp.where(r8 == 1, rank, 0.0))
        cnt_ref[0, r8rows, :] = jnp.broadcast_to(jnp.sum(onehot, axis=1, keepdims=True), (8, LANES))
        comb_t = jnp.where(e_iota == ROUTE_GROUP_LANE, gsel_f,
                           jnp.where(e_iota == ROUTE_RANK_LANE, rank, comb_t))
        comb_ref[0, rows, :] = comb_t.T
        return carry

    if n_mt == 1:
        phase3(0, 0)
    else:
        lax.fori_loop(0, n_mt, phase3, 0)


class _RowWindow(NamedTuple):
    array: jax.Array
    start: int
    n: int


def _const_spec(a):
    if isinstance(a, _RowWindow):
        assert a.start % a.n == 0
        return a.array, pl.BlockSpec((a.n, a.array.shape[1]), lambda b: (a.start // a.n, 0),
                                     pipeline_mode=pl.Buffered(1))
    nd = a.ndim
    return a, pl.BlockSpec(a.shape, lambda b, _nd=nd: (0,) * _nd, pipeline_mode=pl.Buffered(1))


def _mixer(x, T, mod, mod_index, weights, P, state=None, emit_state=False):
    B, R, _ = x.shape
    n_chunks = R // SUB
    n_blk = R // MOE_BLK
    n_seq = R // T
    has_state = state is not None
    seq_mode = {} if R <= MIX_TM else {"pipeline_mode": pl.Buffered(1)}
    in_specs = [
        pl.BlockSpec((1, R, D_MODEL), lambda b: (b, 0, 0), **seq_mode),
        pl.BlockSpec(mod.shape, lambda b: (0, 0, 0)),
    ]
    args = [x, mod]
    if has_state:
        c0, n0, m0 = state
        in_specs += [
            pl.BlockSpec((1, N_UNITS, HEAD_DIM, HEAD_DIM), lambda b: (b, 0, 0, 0)),
            pl.BlockSpec((1, N_UNITS, HEAD_DIM), lambda b: (b, 0, 0)),
            pl.BlockSpec(memory_space=pltpu.SMEM),
        ]
        args += [c0, n0, m0]
    for name in _MIXER_WEIGHTS:
        operand, spec = _const_spec(weights[name])
        in_specs.append(spec)
        args.append(operand)
    out_shape = [
        jax.ShapeDtypeStruct((B, R, D_MODEL), F32),
        jax.ShapeDtypeStruct((B, R, D_MODEL), BF16),
        jax.ShapeDtypeStruct((B, R, LANES), F32),
        jax.ShapeDtypeStruct((B, n_blk * 8, MOE_BLK), F32),
        jax.ShapeDtypeStruct((B, n_blk * 8, LANES), F32),
    ]
    out_specs = [
        pl.BlockSpec((1, R, D_MODEL), lambda b: (b, 0, 0), **seq_mode),
        pl.BlockSpec((1, R, D_MODEL), lambda b: (b, 0, 0), **seq_mode),
        pl.BlockSpec((1, R, LANES), lambda b: (b, 0, 0)),
        pl.BlockSpec((1, n_blk * 8, MOE_BLK), lambda b: (b, 0, 0)),
        pl.BlockSpec((1, n_blk * 8, LANES), lambda b: (b, 0, 0)),
    ]
    if emit_state:
        out_shape += [
            jax.ShapeDtypeStruct((B, n_seq * N_UNITS, HEAD_DIM, HEAD_DIM), F32),
            jax.ShapeDtypeStruct((B, n_seq * N_UNITS, HEAD_DIM), F32),
            jax.ShapeDtypeStruct((B, n_seq * N_UNITS, LANES), F32),
        ]
        out_specs += [
            pl.BlockSpec((1, n_seq * N_UNITS, HEAD_DIM, HEAD_DIM), lambda b: (b, 0, 0, 0)),
            pl.BlockSpec((1, n_seq * N_UNITS, HEAD_DIM), lambda b: (b, 0, 0)),
            pl.BlockSpec((1, n_seq * N_UNITS, LANES), lambda b: (b, 0, 0)),
        ]
    scratch = [
        pltpu.VMEM((R, D_MLSTM), BF16),
        pltpu.VMEM((n_chunks, D_MLSTM, SUB), BF16),
        pltpu.VMEM((R, D_MLSTM), BF16),
        pltpu.VMEM((R, D_MLSTM), F32),
        pltpu.VMEM((n_chunks, 5 * N_UNITS, SUB), F32),
        pltpu.VMEM((R, D_MODEL), F32),
        pltpu.VMEM((R, D_MODEL), F32),
        pltpu.VMEM((R, D_MLSTM), F32),
        pltpu.VMEM((N_UNITS, HEAD_DIM, 2 * HEAD_DIM), F32),
        pltpu.VMEM((MIX_TM // P, P + 2 * CONV_PAD, D_CONV), F32),
    ]
    return pl.pallas_call(
        functools.partial(_mixer_kernel, R, T, P, has_state, emit_state, mod_index),
        grid=(B,),
        in_specs=in_specs,
        out_specs=out_specs,
        out_shape=out_shape,
        scratch_shapes=scratch,
        compiler_params=pltpu.CompilerParams(
            dimension_semantics=("arbitrary",), vmem_limit_bytes=VMEM_LIMIT),
        name="mixer_T%d" % T,
    )(*args)


def _dest_in_block(group, rank, starts):
    dest = rank
    for g in range(N_GROUPS):
        dest = dest + jnp.where(group == float(g), starts[g], 0.0)
    return dest


def _copy_segments(src_refs, dst_refs, src_starts, dst_starts, n_pieces):
    for g in range(N_GROUPS):
        def body(k, carry, g=g):
            s = pl.multiple_of(src_starts[g] + k * ROW_ALIGN, ROW_ALIGN)
            d = pl.multiple_of(dst_starts[g] + k * ROW_ALIGN, ROW_ALIGN)
            for src, dst in zip(src_refs, dst_refs):
                dst[pl.ds(d, ROW_ALIGN), :] = src[pl.ds(s, ROW_ALIGN), :]
            return carry
        lax.fori_loop(0, n_pieces[g], body, 0)


def _plan_segments(n_blocks, n_tiles, count, start_ref, npiece_ref, off_ref, tgroup_ref, tvalid_ref):
    align_shift = ROW_ALIGN.bit_length() - 1
    tile_shift = MOE_TM.bit_length() - 1

    def block_starts(blk, carry):
        row = jnp.int32(0)
        for g in range(N_GROUPS):
            n = lax.shift_right_logical(count(blk, g) + (ROW_ALIGN - 1), align_shift)
            npiece_ref[blk * N_GROUPS + g] = n
            start_ref[blk * N_GROUPS + g] = row
            row = row + n * ROW_ALIGN
        return carry

    lax.fori_loop(0, n_blocks, block_starts, 0)

    base_row = jnp.int32(0)
    base_tile = jnp.int32(0)
    last_group = jnp.int32(0)
    for g in range(N_GROUPS):
        def seg_offsets(blk, row, g=g, base_row=base_row):
            off_ref[blk * N_GROUPS + g] = base_row + row
            return row + npiece_ref[blk * N_GROUPS + g] * ROW_ALIGN

        rows = lax.fori_loop(0, n_blocks, seg_offsets, jnp.int32(0))
        tiles = lax.shift_right_logical(rows + (MOE_TM - 1), tile_shift)

        def mark_tiles(t, carry, g=g, base_tile=base_tile):
            tgroup_ref[base_tile + t] = g
            tvalid_ref[base_tile + t] = 1
            return carry

        lax.fori_loop(0, tiles, mark_tiles, 0)
        last_group = jnp.where(tiles > 0, g, last_group)
        base_row = base_row + tiles * MOE_TM
        base_tile = base_tile + tiles

    def mark_unused(t, carry):
        tgroup_ref[t] = last_group
        tvalid_ref[t] = 0
        return carry

    lax.fori_loop(base_tile, n_tiles, mark_unused, 0)


def _dispatch_kernel(n_ctx_blocks, n_blocks, n_tiles,
                     h2c_ref, h2l_ref, cbc_ref, cbl_ref, rtc_ref, rtl_ref, cntc_ref, cntl_ref,
                     xs_ref, cs_ref, start_ref, npiece_ref, off_ref, tgroup_ref, tvalid_ref,
                     sx_s, sc_s):
    b = pl.program_id(0)
    is_ctx = b < n_ctx_blocks

    def count(blk, g):
        vc = cntc_ref[jnp.minimum(blk, n_ctx_blocks - 1), pl.ds(g, 1), pl.ds(0, 1)]
        vl = cntl_ref[jnp.maximum(blk - n_ctx_blocks, 0), pl.ds(g, 1), pl.ds(0, 1)]
        return jnp.where(blk < n_ctx_blocks, vc, vl)[0, 0].astype(jnp.int32)

    @pl.when(b == 0)
    def _():
        _plan_segments(n_blocks, n_tiles, count, start_ref, npiece_ref, off_ref, tgroup_ref, tvalid_ref)
        xs_ref[...] = jnp.zeros_like(xs_ref)
        cs_ref[...] = jnp.zeros_like(cs_ref)

    h2 = jnp.where(is_ctx, h2c_ref[0], h2l_ref[0])
    cb = jnp.where(is_ctx, cbc_ref[0], cbl_ref[0])
    rt = jnp.where(is_ctx, rtc_ref[0], rtl_ref[0])
    starts = [start_ref[b * N_GROUPS + g] for g in range(N_GROUPS)]
    dest = _dest_in_block(rt[0:1, :], rt[1:2, :], [s.astype(F32) for s in starts])
    row = lax.broadcasted_iota(jnp.int32, (SORT_ROWS, MOE_BLK), 0).astype(F32)
    perm = (row == dest).astype(F32).astype(BF16)
    cb_hi = cb.astype(BF16)
    cb_lo = (cb - cb_hi.astype(F32)).astype(BF16)
    sx_s[...] = _dot(perm, h2).astype(BF16)
    sc_s[...] = _dot(perm, jnp.concatenate([cb_hi, cb_lo], axis=1)).astype(BF16)
    _copy_segments((sx_s, sc_s), (xs_ref, cs_ref), starts,
                   [off_ref[b * N_GROUPS + g] for g in range(N_GROUPS)],
                   [npiece_ref[b * N_GROUPS + g] for g in range(N_GROUPS)])


def _experts_kernel(tgroup_ref, tvalid_ref, xs_ref, cs_ref, wg_ref, wu_ref, wd_ref, ys_ref):
    i = pl.program_id(0)

    @pl.when(tvalid_ref[i] == 1)
    def _():
        x = xs_ref[...]
        comb = cs_ref[:, :LANES].astype(F32) + cs_ref[:, LANES:].astype(F32)
        lane = lax.broadcasted_iota(jnp.int32, comb.shape, 1)
        first = tgroup_ref[i] * EXPERTS_PER_GROUP
        acc = None
        for j in range(EXPERTS_PER_GROUP):
            gj = _dot(x, wg_ref[j].astype(BF16))
            uj = _dot(x, wu_ref[j].astype(BF16))
            cw = jnp.sum(jnp.where(lane == first + j, comb, 0.0), axis=1, keepdims=True)
            out = _dot((gj * _sigmoid(gj) * uj * cw).astype(BF16), wd_ref[j].astype(BF16))
            acc = out if acc is None else acc + out
        ys_ref[...] = acc.astype(BF16)

    @pl.when(tvalid_ref[i] == 0)
    def _():
        ys_ref[...] = jnp.zeros_like(ys_ref)


def _combine_kernel(n_ctx_blocks, blocks_per_lat_seq, start_ref, npiece_ref, off_ref,
                    x1c_ref, x1l_ref, cbc_ref, cbl_ref, ys_ref, mod_ref, gf_ref, yc_ref, yl_ref, loc_s):
    b = pl.program_id(0)
    is_ctx = b < n_ctx_blocks
    starts = [start_ref[b * N_GROUPS + g] for g in range(N_GROUPS)]
    loc_s[...] = jnp.zeros_like(loc_s)
    _copy_segments((ys_ref,), (loc_s,), [off_ref[b * N_GROUPS + g] for g in range(N_GROUPS)], starts,
                   [npiece_ref[b * N_GROUPS + g] for g in range(N_GROUPS)])
    cb = jnp.where(is_ctx, cbc_ref[0], cbl_ref[0])
    dest = _dest_in_block(cb[:, ROUTE_GROUP_LANE:ROUTE_GROUP_LANE + 1],
                          cb[:, ROUTE_RANK_LANE:ROUTE_RANK_LANE + 1],
                          [s.astype(F32) for s in starts])
    col = lax.broadcasted_iota(jnp.int32, (MOE_BLK, SORT_ROWS), 1).astype(F32)
    unperm = (col == dest).astype(F32).astype(BF16)
    moe = _dot(unperm, loc_s[...])
    x1 = jnp.where(is_ctx, x1c_ref[0], x1l_ref[0])
    mrow = jnp.where(is_ctx, 0, 1 + jnp.maximum(b - n_ctx_blocks, 0) // blocks_per_lat_seq)
    x2 = x1 + mod_ref[N_ADA - 1, pl.ds(mrow, 1), :] * moe
    y = x2 * lax.rsqrt(jnp.mean(x2 * x2, axis=-1, keepdims=True) + EPS) * gf_ref[...]

    @pl.when(is_ctx)
    def _():
        yc_ref[0] = y

    @pl.when(jnp.logical_not(is_ctx))
    def _():
        yl_ref[0] = y


def _moe(x1c, x1l, h2c, h2l, cbc, cbl, rtc, rtl, cntc, cntl, mod, blocks_per_lat_seq, wg, wu, wd, gf):
    nc, nl = x1c.shape[0], x1l.shape[0]
    nb = nc + nl
    n_rows_max = nb * MOE_BLK + nb * N_GROUPS * (ROW_ALIGN - 1) + N_GROUPS * (MOE_TM - ROW_ALIGN)
    n_tiles = -(-n_rows_max // MOE_TM)
    ns = n_tiles * MOE_TM

    cmap = lambda b, *_: (jnp.minimum(b, nc - 1), 0, 0)
    lmap = lambda b, *_: (jnp.maximum(b - nc, 0), 0, 0)
    whole = lambda *_: (0, 0)
    once = {"pipeline_mode": pl.Buffered(1)}
    arb = pltpu.CompilerParams(dimension_semantics=("arbitrary",), vmem_limit_bytes=VMEM_LIMIT)
    smem = pl.BlockSpec(memory_space=pltpu.SMEM)
    seg_i32 = jax.ShapeDtypeStruct((nb * N_GROUPS,), jnp.int32)
    tile_i32 = jax.ShapeDtypeStruct((n_tiles,), jnp.int32)

    xs, cs, start, npiece, off, tgroup, tvalid = pl.pallas_call(
        functools.partial(_dispatch_kernel, nc, nb, n_tiles),
        grid_spec=pltpu.PrefetchScalarGridSpec(
            num_scalar_prefetch=0, grid=(nb,),
            in_specs=[
                pl.BlockSpec((1, MOE_BLK, D_MODEL), cmap), pl.BlockSpec((1, MOE_BLK, D_MODEL), lmap),
                pl.BlockSpec((1, MOE_BLK, LANES), cmap), pl.BlockSpec((1, MOE_BLK, LANES), lmap),
                pl.BlockSpec((1, 8, MOE_BLK), cmap), pl.BlockSpec((1, 8, MOE_BLK), lmap),
                pl.BlockSpec(cntc.shape, lambda b: (0, 0, 0)), pl.BlockSpec(cntl.shape, lambda b: (0, 0, 0)),
            ],
            out_specs=[pl.BlockSpec((ns, D_MODEL), whole, **once), pl.BlockSpec((ns, 2 * LANES), whole, **once),
                       smem, smem, smem, smem, smem],
            scratch_shapes=[pltpu.VMEM((SORT_ROWS, D_MODEL), BF16), pltpu.VMEM((SORT_ROWS, 2 * LANES), BF16)],
        ),
        out_shape=[jax.ShapeDtypeStruct((ns, D_MODEL), BF16), jax.ShapeDtypeStruct((ns, 2 * LANES), BF16),
                   seg_i32, seg_i32, seg_i32, tile_i32, tile_i32],
        compiler_params=arb,
        name="moe_dispatch",
    )(h2c, h2l, cbc, cbl, rtc, rtl, cntc, cntl)

    wmap = lambda i, tg, tv: (tg[i], 0, 0)
    ys = pl.pallas_call(
        _experts_kernel,
        grid_spec=pltpu.PrefetchScalarGridSpec(
            num_scalar_prefetch=2, grid=(n_tiles,),
            in_specs=[
                pl.BlockSpec((MOE_TM, D_MODEL), lambda i, *_: (i, 0)),
                pl.BlockSpec((MOE_TM, 2 * LANES), lambda i, *_: (i, 0)),
                pl.BlockSpec((EXPERTS_PER_GROUP, D_MODEL, D_EXPERT), wmap),
                pl.BlockSpec((EXPERTS_PER_GROUP, D_MODEL, D_EXPERT), wmap),
                pl.BlockSpec((EXPERTS_PER_GROUP, D_EXPERT, D_MODEL), wmap),
            ],
            out_specs=pl.BlockSpec((MOE_TM, D_MODEL), lambda i, *_: (i, 0)),
        ),
        out_shape=jax.ShapeDtypeStruct((ns, D_MODEL), BF16),
        compiler_params=arb,
        name="moe_experts",
    )(tgroup, tvalid, xs, cs, wg, wu, wd)

    yc, yl = pl.pallas_call(
        functools.partial(_combine_kernel, nc, blocks_per_lat_seq),
        grid_spec=pltpu.PrefetchScalarGridSpec(
            num_scalar_prefetch=3, grid=(nb,),
            in_specs=[
                pl.BlockSpec((1, MOE_BLK, D_MODEL), cmap), pl.BlockSpec((1, MOE_BLK, D_MODEL), lmap),
                pl.BlockSpec((1, MOE_BLK, LANES), cmap), pl.BlockSpec((1, MOE_BLK, LANES), lmap),
                pl.BlockSpec((ns, D_MODEL), whole, **once),
                pl.BlockSpec(mod.shape, lambda *_: (0, 0, 0)),
                pl.BlockSpec((1, D_MODEL), whole),
            ],
            out_specs=[pl.BlockSpec((1, MOE_BLK, D_MODEL), cmap), pl.BlockSpec((1, MOE_BLK, D_MODEL), lmap)],
            scratch_shapes=[pltpu.VMEM((SORT_ROWS, D_MODEL), BF16)],
        ),
        out_shape=[jax.ShapeDtypeStruct((nc, MOE_BLK, D_MODEL), F32),
                   jax.ShapeDtypeStruct((nl, MOE_BLK, D_MODEL), F32)],
        compiler_params=arb,
        name="moe_combine",
    )(start, npiece, off, x1c, x1l, cbc, cbl, ys, mod, gf)
    return yc, yl


def _prep_weights(norm1_g, w_in, b_in, b_gates, w_dw, b_dw, conv_ln_g, conv_ln_b, w_conv_out,
                  mlstm_hn_g, w_mlstm_out, w_o, norm2_g, w_rg, b_rg, w_re, b_re):
    s_a = 2 * D_CONV
    s_q = s_a + D_MLSTM
    s_k = s_q + D_MLSTM
    s_v = s_k + D_MLSTM
    s_o = s_v + D_MLSTM
    s_g = s_o + 4 * N_HEADS
    row = lambda v: v.reshape(1, -1).astype(F32)
    w_t = w_in.T
    keep = [(0, s_q), (s_k, s_o), (s_g, w_in.shape[1])]
    wrow = _transpose_cast(w_t, [r for a, b in keep for r in range(a, b, WPREP_ROWS)])
    bg = (b_in[s_o:s_g] + b_gates.reshape(-1)).reshape(2, 2, N_HEADS).transpose(1, 0, 2).reshape(-1, 1)
    row_window = lambda start, n: _RowWindow(w_t, start, n)
    n_rt = N_EXPERTS + N_GROUPS
    wrt = jnp.pad(jnp.concatenate([w_re, w_rg], axis=1), ((0, 0), (0, LANES - n_rt)))
    wrt_hi = wrt.astype(BF16)
    wrt2 = jnp.concatenate([wrt_hi, (wrt - wrt_hi.astype(F32)).astype(BF16)], axis=1)
    brtT = jnp.pad(jnp.concatenate([b_re, b_rg]), (0, LANES - n_rt)).reshape(LANES, 1)
    return {
        "g1": row(norm1_g),
        "wrow": wrow, "bag": row(b_in[:s_a]), "bq": row(b_in[s_a:s_q]),
        "wkT": row_window(s_q, D_MLSTM), "bk": b_in[s_q:s_k].reshape(-1, 1),
        "bv": row(b_in[s_k:s_v]), "bog": row(b_in[s_v:s_o]),
        "wgifT": row_window(s_o, 4 * N_HEADS), "bgifT": bg, "bgm": row(b_in[s_g:]),
        "wdw": w_dw.astype(F32), "bdw": row(b_dw), "lng": row(conv_ln_g), "lnb": row(conv_ln_b),
        "wco": w_conv_out.astype(BF16), "hng": row(mlstm_hn_g), "wmo": w_mlstm_out.astype(BF16),
        "wo": w_o.astype(BF16), "g2": row(norm2_g), "wrt2": wrt2, "brtT": brtT,
    }


def kernel(x_prompt, x_sample, state_C, state_n, state_m, c, c_ctx, norm1_g, w_ada, b_ada, w_in, b_in, b_gates, w_dw, b_dw, conv_ln_g, conv_ln_b, w_conv_out, mlstm_hn_g, w_mlstm_out, w_o, norm2_g, w_rg, b_rg, w_re, b_re, w_e_gate, w_e_up, w_e_down, norm_final_g):
    B, S, _ = x_prompt.shape
    Bd, Sd, _ = x_sample.shape
    assert w_ada.shape[0] == 1, "single trunk layer"
    assert S == SUB and Sd % SUB == 0

    mod = _ada(c_ctx.reshape(1, -1), c, w_ada[0], b_ada[0].reshape(1, -1))

    wts = _prep_weights(norm1_g[0], w_in[0], b_in[0], b_gates[0], w_dw[0], b_dw[0], conv_ln_g[0],
                        conv_ln_b[0], w_conv_out[0], mlstm_hn_g[0], w_mlstm_out[0], w_o[0],
                        norm2_g[0], w_rg[0], b_rg[0], w_re[0], b_re[0])

    x1p, h2p, cbp, rtp, cntp, c_new, n_new, m_new = _mixer(
        x_prompt.reshape(B * S // MIX_TM, MIX_TM, D_MODEL), S, mod, lambda b: 0, wts, P=S, emit_state=True)

    state = (state_C[:, 0].reshape(Bd, N_UNITS, HEAD_DIM, HEAD_DIM), state_n[:, 0].reshape(Bd, N_UNITS, HEAD_DIM),
             state_m[:, 0].reshape(Bd, N_UNITS))
    x1s, h2s, cbs, rts, cnts = _mixer(x_sample, Sd, mod, lambda b: 1 + b, wts, P=GRID_W, state=state)

    nc, nl = B * S // MOE_BLK, Bd * Sd // MOE_BLK
    blk = lambda a, n: a.reshape(n, MOE_BLK, a.shape[-1])
    yp, ys = _moe(blk(x1p, nc), blk(x1s, nl), blk(h2p, nc), blk(h2s, nl), blk(cbp, nc), blk(cbs, nl),
                  rtp.reshape(nc, 8, MOE_BLK), rts.reshape(nl, 8, MOE_BLK),
                  cntp.reshape(nc, 8, LANES), cnts.reshape(nl, 8, LANES),
                  mod, Sd // MOE_BLK, w_e_gate[0], w_e_up[0], w_e_down[0], norm_final_g.reshape(1, -1))

    return (yp.reshape(B, S, D_MODEL), ys.reshape(Bd, Sd, D_MODEL),
            c_new.reshape(B, 1, 2, N_HEADS, HEAD_DIM, HEAD_DIM),
            n_new.reshape(B, 1, 2, N_HEADS, HEAD_DIM),
            m_new[:, :, 0].reshape(B, 1, 2, N_HEADS))
```

```python
import functools
from typing import NamedTuple

import jax
import jax.numpy as jnp
from jax import lax
from jax.experimental import pallas as pl
from jax.experimental.pallas import tpu as pltpu

D_MODEL = 1024
D_CONV = 512
CONV_K = 31
D_MLSTM = 512
N_HEADS = 4
HEAD_DIM = D_MLSTM // N_HEADS
N_GROUPS = 4
EXPERTS_PER_GROUP = 4
N_EXPERTS = N_GROUPS * EXPERTS_PER_GROUP
D_EXPERT = 256
N_ADA = 6
EPS = 1e-6
GRID_W = 64

LANES = 128
SUB = 256
CONV_PAD = 16
CONV_RB = 64
N_UNITS = 2 * N_HEADS
ROW_ALIGN = 16
MOE_TM = 512
MIX_TM = 512
MOE_BLK = MIX_TM
SORT_ROWS = MOE_BLK + N_GROUPS * ROW_ALIGN
WPREP_ROWS = 512
ROUTE_GROUP_LANE = N_EXPERTS
ROUTE_RANK_LANE = N_EXPERTS + 1
VMEM_LIMIT = 58 * 1024 * 1024

BF16 = jnp.bfloat16
F32 = jnp.float32
NT_DIMS = (((1,), (1,)), ((), ()))


def _dot(a, b):
    return jnp.dot(a, b, preferred_element_type=F32)


def _dot_nt(a, b, precision=None):
    return lax.dot_general(a, b, NT_DIMS, preferred_element_type=F32, precision=precision)


def _sigmoid(x):
    return 0.5 * jnp.tanh(0.5 * x) + 0.5


def _log_sigmoid(x):
    return jnp.minimum(x, 0.0) - jnp.log1p(jnp.exp(-jnp.abs(x)))


def _split3(x):
    hi = x.astype(BF16).astype(F32)
    r1 = x - hi
    mid = r1.astype(BF16).astype(F32)
    lo = (r1 - mid).astype(BF16).astype(F32)
    return hi, mid, lo


def _ada_kernel(cctx_ref, c_ref, w_ref, b_ref, o_ref):
    n = 1 + c_ref.shape[0]
    c = jnp.concatenate([cctx_ref[...], c_ref[...], jnp.zeros((8 - n, D_MODEL), F32)], axis=0)
    s = (c * _sigmoid(c)).astype(BF16)
    o_ref[0] = _dot(s, w_ref[...].astype(BF16)) + b_ref[...]


def _ada(c_ctx, c, w_ada, b_ada):
    return pl.pallas_call(
        _ada_kernel,
        grid=(N_ADA,),
        in_specs=[
            pl.BlockSpec(c_ctx.shape, lambda j: (0, 0)),
            pl.BlockSpec(c.shape, lambda j: (0, 0)),
            pl.BlockSpec((D_MODEL, D_MODEL), lambda j: (0, j)),
            pl.BlockSpec((1, D_MODEL), lambda j: (0, j)),
        ],
        out_specs=pl.BlockSpec((1, 8, D_MODEL), lambda j: (j, 0, 0)),
        out_shape=jax.ShapeDtypeStruct((N_ADA, 8, D_MODEL), F32),
        compiler_params=pltpu.CompilerParams(dimension_semantics=("arbitrary",)),
        name="ada",
    )(c_ctx, c, w_ada, b_ada)


def _transpose_cast_kernel(starts_ref, wt_ref, o_ref):
    o_ref[...] = wt_ref[...].astype(BF16).T


def _transpose_cast(w_t, row_starts):
    n, k = len(row_starts), w_t.shape[1]
    return pl.pallas_call(
        _transpose_cast_kernel,
        grid_spec=pltpu.PrefetchScalarGridSpec(
            num_scalar_prefetch=1, grid=(n,),
            in_specs=[pl.BlockSpec((pl.Element(WPREP_ROWS), pl.Element(k)), lambda j, starts: (starts[j] * 8, 0))],
            out_specs=pl.BlockSpec((k, WPREP_ROWS), lambda j, starts: (0, j)),
        ),
        out_shape=jax.ShapeDtypeStruct((k, n * WPREP_ROWS), BF16),
        compiler_params=pltpu.CompilerParams(dimension_semantics=("arbitrary",)),
        name="transpose_cast",
    )(jnp.array([r // 8 for r in row_starts], jnp.int32), w_t)


WROW_OFFSET = {"wq": 2 * D_CONV, "wv": 2 * D_CONV + D_MLSTM, "wog": 2 * D_CONV + 2 * D_MLSTM,
               "wgm": 2 * D_CONV + 3 * D_MLSTM}
BROW_K_OFFSET = 2 * D_CONV + 3 * D_MLSTM + 2 * D_MODEL

_MIXER_WEIGHTS = (
    "g1", "wrow", "brow", "wkT", "wgifT", "bgifT", "wdw", "bdw", "lng", "lnb",
    "wco", "hng", "wmo", "wo", "g2", "wrt2", "brtT",
)


def _zero_after(x):
    bits = lax.bitcast_convert_type(x, jnp.uint32)
    bits = lax.shift_right_logical(lax.shift_right_logical(bits, jnp.uint32(16)), jnp.uint32(16))
    return lax.bitcast_convert_type(bits, F32)[0:1, :]


def _conv_block(upad_s, seg, base, cs, wdw_ref, bdw_ref, after=None):
    sub = 8
    first = CONV_PAD - CONV_K // 2
    acc = jnp.broadcast_to(bdw_ref[0:1, cs], (CONV_RB, LANES))
    for r in range(sub):
        z = None
        for a in range((CONV_K + first + sub - 1) // sub):
            j = sub * a + r - first
            if 0 <= j < CONV_K:
                lo = base + sub * a
                tap = wdw_ref[j:j + 1, cs] if after is None else wdw_ref[j:j + 1, cs] + after
                term = tap * upad_s[seg, lo:lo + CONV_RB + sub, cs]
                z = term if z is None else z + term
        acc = acc + z[r:r + CONV_RB, :]
    return acc


def _mixer_kernel(R, T, P, has_state, emit_state, mod_index, *refs):
    L = SUB
    n_mt = R // MIX_TM
    cpm = MIX_TM // L
    n_seq = R // T
    cps = T // L
    nseg = MIX_TM // P
    assert not has_state or n_seq == 1
    it = iter(refs)
    x_ref = next(it)
    mod_ref = next(it)
    if has_state:
        c0_ref = next(it)
        n0_ref = next(it)
        m0_ref = next(it)
    w = {name: next(it) for name in _MIXER_WEIGHTS}
    x1_ref = next(it)
    h2_ref = next(it)
    comb_ref = next(it)
    route_ref = next(it)
    cnt_ref = next(it)
    if emit_state:
        cout_ref = next(it)
        nout_ref = next(it)
        mout_ref = next(it)
    (q_s, kT_s, v_s, so_s, scan_s, ma_s, sgb_s, hm_s, cst_s, upad_s) = [next(it) for _ in range(10)]

    cond_row = mod_index(pl.program_id(0))

    def mod_row(i):
        return mod_ref[i, pl.ds(cond_row, 1), :]

    zpad = jnp.zeros((CONV_PAD, D_CONV), F32)
    for seg in range(nseg):
        upad_s[seg, 0:CONV_PAD, :] = zpad
        upad_s[seg, CONV_PAD + P:CONV_PAD + P + CONV_PAD, :] = zpad

    t_idx = lax.broadcasted_iota(jnp.int32, (L, L), 0)
    s_idx = lax.broadcasted_iota(jnp.int32, (L, L), 1)
    lower = s_idx <= t_idx
    upper = s_idx >= t_idx
    triu_b = upper.astype(F32).astype(BF16)
    lane_u = lax.broadcasted_iota(jnp.int32, (N_UNITS, L), 1)
    is_bwd = lax.broadcasted_iota(jnp.int32, (N_UNITS, L), 0) >= N_HEADS

    def gate_scan(g):
        gi, lf = g[:N_UNITS], _log_sigmoid(g[N_UNITS:])
        pr = _dot(jnp.concatenate(_split3(lf), axis=0).astype(BF16), triu_b)
        pre = pr[0:N_UNITS] + pr[N_UNITS:2 * N_UNITS] + pr[2 * N_UNITS:]
        tot = pre[:, L - 1:L]
        bsum = jnp.where(is_bwd, tot - pre + lf, pre)
        a = gi - bsum
        pm, sm, k = a, a, 1
        while k < L:
            pm = jnp.where(lane_u >= k, jnp.maximum(pm, pltpu.roll(pm, k, axis=1)), pm)
            sm = jnp.where(lane_u < L - k, jnp.maximum(sm, pltpu.roll(sm, L - k, axis=1)), sm)
            k *= 2
        wide = lambda v: jnp.broadcast_to(v, (N_UNITS, L))
        return jnp.concatenate([a, jnp.where(is_bwd, sm, pm), bsum, wide(tot),
                                wide(jnp.max(a, axis=1, keepdims=True))], axis=0)

    def phase1(i, carry):
        r0 = pl.multiple_of(i * MIX_TM, MIX_TM)
        rows = pl.ds(r0, MIX_TM)
        x = x_ref[0, rows, :]
        xn = x * lax.rsqrt(jnp.mean(x * x, axis=-1, keepdims=True) + EPS) * w["g1"][...]
        hb = (xn * (1.0 + mod_row(1)) + mod_row(0)).astype(BF16)

        gates = _dot_nt(w["wgifT"][...].astype(BF16), hb)
        gates = jnp.concatenate([gates[d * 2 * N_HEADS + g * N_HEADS:d * 2 * N_HEADS + (g + 1) * N_HEADS]
                                 for g in range(2) for d in range(2)], axis=0) + w["bgifT"][...]
        for j in range(cpm):
            scan_s[i * cpm + j] = gate_scan(gates[:, j * L:(j + 1) * L])
        ag = _dot(hb, w["wrow"][:, :2 * D_CONV]) + w["brow"][:, :2 * D_CONV]
        u = ag[:, :D_CONV] * _sigmoid(ag[:, D_CONV:])
        for seg in range(nseg):
            upad_s[seg, CONV_PAD:CONV_PAD + P, :] = u[seg * P:(seg + 1) * P, :]

        def proj(name, c0, gate, width=2 * LANES):
            w0 = WROW_OFFSET[name] + c0
            b = w["brow"][:, w0:w0 + width]
            if gate is not None:
                b = b + jnp.concatenate([gate] * (width // LANES), axis=1)
            return _dot(hb, w["wrow"][:, w0:w0 + width]) + b

        last = lambda z: z[-8:, -LANES:]
        bk_row = w["brow"][:, BROW_K_OFFSET:BROW_K_OFFSET + D_MLSTM]
        bk_col = jnp.concatenate([bk_row, jnp.zeros((LANES - 1, D_MLSTM), F32)], axis=0).T[:, 0:1]

        def gm_a(c0, gate):
            z = proj("wgm", c0, gate)
            ma_s[rows, c0:c0 + 2 * LANES] = _sigmoid(z)
            return last(z)

        def gm_b(c0, gate):
            z = proj("wgm", D_MODEL + c0, gate)
            sgb_s[rows, c0:c0 + 2 * LANES] = _sigmoid(z)
            return last(z)

        def q_part(c0, gate):
            z = proj("wq", c0, gate)
            q_s[rows, c0:c0 + 2 * LANES] = (z * (HEAD_DIM ** -0.5)).astype(BF16)
            return last(z)

        def v_part(c0, gate):
            z = proj("wv", c0, gate)
            v_s[rows, c0:c0 + 2 * LANES] = z.astype(BF16)
            return last(z)

        def o_part(c0, gate):
            z = proj("wog", c0, gate)
            so_s[rows, c0:c0 + 2 * LANES] = _sigmoid(z)
            return last(z)

        def k_part(c0, gate):
            rs = slice(c0, c0 + 2 * LANES)
            b = bk_col[rs, :] if gate is None else bk_col[rs, :] + gate[:, 0:1]
            z = _dot_nt(w["wkT"][rs, :].astype(BF16), hb) + b
            kt = z.astype(BF16)
            for j in range(cpm):
                kT_s[i * cpm + j, rs, :] = kt[:, j * L:(j + 1) * L]
            return last(z)

        jobs = ([functools.partial(gm_a, c0) for c0 in range(0, D_MODEL, 2 * LANES)]
                + [functools.partial(gm_b, c0) for c0 in range(0, D_MODEL, 2 * LANES)]
                + [functools.partial(f, c0) for f in (q_part, k_part, v_part, o_part)
                   for c0 in range(0, D_MLSTM, 2 * LANES)])
        n_jobs = len(jobs)
        conv = {}
        after = None
        n_pieces = (D_CONV // LANES) * nseg * (P // CONV_RB)
        for cb in range(D_CONV // LANES):
            cs = slice(cb * LANES, (cb + 1) * LANES)
            for seg in range(nseg):
                for rb in range(P // CONV_RB):
                    blk = _conv_block(upad_s, seg, rb * CONV_RB, cs, w["wdw"], w["bdw"], after)
                    conv[(cb, seg, rb)] = blk
                    if jobs and len(conv) * n_jobs >= (n_jobs - len(jobs) + 1) * n_pieces:
                        after = _zero_after(jobs.pop(0)(_zero_after(blk[-8:, :])))
        for job in jobs:
            job(None)
        cu = jnp.concatenate(
            [jnp.concatenate([conv[(cb, seg, rb)] for seg in range(nseg) for rb in range(P // CONV_RB)], axis=0)
             for cb in range(D_CONV // LANES)], axis=1)
        mu = jnp.mean(cu, axis=-1, keepdims=True)
        cc = cu - mu
        cn = cc * lax.rsqrt(jnp.mean(cc * cc, axis=-1, keepdims=True) + EPS) * w["lng"][...] + w["lnb"][...]
        ca = (cn * _sigmoid(cn)).astype(BF16)
        ma_s[rows, :] = ma_s[rows, :] * _dot(ca, w["wco"][...])
        return carry

    if n_mt == 1:
        phase1(0, 0)
    else:
        lax.fori_loop(0, n_mt, phase1, 0)

    ones_col = (lax.broadcasted_iota(jnp.int32, (L, HEAD_DIM), 1) == 0).astype(F32).astype(BF16)
    pad_rows = jnp.zeros((LANES - 3 * N_UNITS, L), F32)

    def gate_prep(c, m_vec):
        sc = scan_s[c]
        a, run_max, bsum = sc[0:N_UNITS], sc[N_UNITS:2 * N_UNITS], sc[2 * N_UNITS:3 * N_UNITS]
        tot, a_max = sc[3 * N_UNITS:4 * N_UNITS, 0:1], sc[4 * N_UNITS:5 * N_UNITS, 0:1]
        big_m = jnp.maximum(m_vec, run_max)
        m_end = jnp.maximum(m_vec, a_max)
        cols = jnp.concatenate(
            [big_m, jnp.exp(m_vec - big_m), jnp.exp(-bsum - big_m), pad_rows], axis=0).T
        return a, cols, jnp.exp(a - m_end), jnp.exp(m_vec - m_end), tot + m_end

    def unit_group(dirs, c, prep, first_chunk, want_state):
        a, cols, wk, decay, _ = prep
        rows = slice(c * L, (c + 1) * L)
        heads = range(N_HEADS)
        units = [(d, hd) for d in dirs for hd in heads]
        hs = [slice(hd * HEAD_DIM, (hd + 1) * HEAD_DIM) for hd in heads]
        idx = {u: u[0] * N_HEADS + u[1] for u in units}
        col = lambda k, u: cols[:, k * N_UNITS + idx[u]:k * N_UNITS + idx[u] + 1]
        row = lambda arr, u: arr[idx[u]:idx[u] + 1, :]
        chained = has_state or not first_chunk
        qc = [q_s[rows, hs[hd]] for hd in heads]
        kTc = [kT_s[c, hs[hd], :] for hd in heads]
        vaug = [jnp.concatenate([v_s[rows, hs[hd]], ones_col], axis=1) for hd in heads]
        qk = [_dot(qc[hd], kTc[hd]) for hd in heads]
        s_mat = {u: (qk[u[1]] * jnp.where(lower if u[0] == 0 else upper, jnp.exp(row(a, u) - col(0, u)), 0.0)
                     ).astype(BF16) for u in units}
        nd = {u: _dot(s_mat[u], vaug[u[1]]) for u in units}
        if chained:
            nd = {u: nd[u] + col(1, u) * _dot(qc[u[1]], cst_s[idx[u]].astype(BF16)) for u in units}
        h = {u: nd[u][:, :HEAD_DIM] * (1.0 / jnp.maximum(jnp.abs(nd[u][:, HEAD_DIM:HEAD_DIM + 1]), col(2, u)))
             for u in units}
        for hd in heads:
            total = h[(dirs[0], hd)]
            for d in dirs[1:]:
                total = total + h[(d, hd)]
            if dirs[0] == 0:
                hm_s[rows, hs[hd]] = total
            else:
                hm_s[rows, hs[hd]] = hm_s[rows, hs[hd]] + total
        if want_state:
            kw = {u: (kTc[u[1]].astype(F32) * row(wk, u)).astype(BF16) for u in units}
            upd = {u: _dot(kw[u], vaug[u[1]]) for u in units}
            for u in units:
                cst_s[idx[u]] = (upd[u] + row(decay, u) * cst_s[idx[u]]) if chained else upd[u]

    dir_rows = lax.broadcasted_iota(jnp.int32, (N_UNITS, 1), 0) >= N_HEADS
    for seq in range(n_seq):
        if has_state:
            n_cols = jnp.concatenate([n0_ref[0], jnp.zeros((LANES - N_UNITS, HEAD_DIM), F32)], axis=0).T
            first_lane = lax.broadcasted_iota(jnp.int32, (HEAD_DIM, HEAD_DIM), 1) == 0
            for idx in range(N_UNITS):
                cst_s[idx, :, :HEAD_DIM] = c0_ref[0, idx]
                cst_s[idx, :, HEAD_DIM:] = jnp.where(first_lane, n_cols[:, idx:idx + 1], 0.0)
            unit_row = lax.broadcasted_iota(jnp.int32, (N_UNITS, 1), 0)
            m_vec = jnp.zeros((N_UNITS, 1), F32)
            for idx in range(N_UNITS):
                m_vec = jnp.where(unit_row == idx, m0_ref[pl.program_id(0), idx], m_vec)
        else:
            m_vec = jnp.zeros((N_UNITS, 1), F32)
        if cps == 1:
            prep = gate_prep(seq, m_vec)
            unit_group([0, 1], seq, prep, True, emit_state)
            m_vec = prep[4]
        else:
            for d in range(2):
                order = list(range(cps)) if d == 0 else list(range(cps - 1, -1, -1))
                for pos, c in enumerate(order):
                    prep = gate_prep(seq * cps + c, m_vec)
                    unit_group([d], seq * cps + c, prep, pos == 0, emit_state or pos < cps - 1)
                    m_vec = jnp.where(dir_rows == (d == 1), prep[4], m_vec)
        if emit_state:
            for idx in range(N_UNITS):
                caug = cst_s[idx]
                cout_ref[0, seq * N_UNITS + idx] = caug[:, :HEAD_DIM]
                nout_ref[0, seq * N_UNITS + idx:seq * N_UNITS + idx + 1, :] = caug[:, HEAD_DIM:].T[0:1, :]
            mout_ref[0, seq * N_UNITS:(seq + 1) * N_UNITS, :] = jnp.broadcast_to(m_vec, (N_UNITS, LANES))

    e_iota = lax.broadcasted_iota(jnp.int32, (LANES, MIX_TM), 0)
    g_of_e = lax.shift_right_logical(e_iota, 2)
    j_of_e = lax.bitwise_and(e_iota, EXPERTS_PER_GROUP - 1)
    r8 = lax.broadcasted_iota(jnp.int32, (8, MIX_TM), 0)
    before_b = (lax.broadcasted_iota(jnp.int32, (MOE_BLK, MOE_BLK), 0)
                < lax.broadcasted_iota(jnp.int32, (MOE_BLK, MOE_BLK), 1)).astype(F32).astype(BF16)

    def phase3(i, carry):
        r0 = pl.multiple_of(i * MIX_TM, MIX_TM)
        rows = pl.ds(r0, MIX_TM)
        hm = hm_s[rows, :]
        heads = []
        for hd in range(N_HEADS):
            hh = hm[:, hd * HEAD_DIM:(hd + 1) * HEAD_DIM]
            heads.append(hh * lax.rsqrt(jnp.mean(hh * hh, axis=-1, keepdims=True) + EPS))
        hn = jnp.concatenate(heads, axis=1) * w["hng"][...]
        hb2 = (so_s[rows, :] * hn).astype(BF16)
        br_b = _dot(hb2, w["wmo"][...])
        mixed = (ma_s[rows, :] + sgb_s[rows, :] * br_b).astype(BF16)
        x1 = x_ref[0, rows, :] + mod_row(2) * _dot(mixed, w["wo"][...])
        x1_ref[0, rows, :] = x1
        xn = x1 * lax.rsqrt(jnp.mean(x1 * x1, axis=-1, keepdims=True) + EPS) * w["g2"][...]
        h2 = xn * (1.0 + mod_row(4)) + mod_row(3)
        h2_ref[0, rows, :] = h2.astype(BF16)

        h2_hi = h2.astype(BF16)
        h2_lo = (h2 - h2_hi.astype(F32)).astype(BF16)
        lg = _dot(h2_hi, w["wrt2"][...])
        lg = lg[:, :LANES] + lg[:, LANES:] + _dot(h2_lo, w["wrt2"][:, :LANES])
        lt = lg.T + w["brtT"][...]
        gl = [lt[N_EXPERTS + g:N_EXPERTS + g + 1, :] for g in range(N_GROUPS)]
        best, gsel = gl[0], jnp.zeros((1, MIX_TM), jnp.int32)
        for g in range(1, N_GROUPS):
            better = gl[g] > best
            gsel = jnp.where(better, g, gsel)
            best = jnp.where(better, gl[g], best)
        gp_sel = 1.0 / sum(jnp.exp(v - best) for v in gl)
        el = []
        for j in range(EXPERTS_PER_GROUP):
            v = lt[j:j + 1, :]
            for g in range(1, N_GROUPS):
                r = g * EXPERTS_PER_GROUP + j
                v = jnp.where(gsel == g, lt[r:r + 1, :], v)
            el.append(v)
        l1, e1 = el[0], jnp.zeros((1, MIX_TM), jnp.int32)
        for j in range(1, EXPERTS_PER_GROUP):
            better = el[j] > l1
            e1 = jnp.where(better, j, e1)
            l1 = jnp.where(better, el[j], l1)
        l2 = jnp.full((1, MIX_TM), -jnp.inf, F32)
        e2 = jnp.zeros((1, MIX_TM), jnp.int32)
        for j in range(EXPERTS_PER_GROUP):
            better = jnp.logical_and(e1 != j, el[j] > l2)
            e2 = jnp.where(better, j, e2)
            l2 = jnp.where(better, el[j], l2)
        r2 = jnp.exp(l2 - l1)
        wt1 = gp_sel / (1.0 + r2)
        wt2 = gp_sel * r2 / (1.0 + r2)
        in_group = g_of_e == gsel
        comb_t = (jnp.where(jnp.logical_and(in_group, j_of_e == e1), wt1, 0.0)
                  + jnp.where(jnp.logical_and(in_group, j_of_e == e2), wt2, 0.0))

        onehot = (r8 == gsel).astype(F32)
        gsel_f = gsel.astype(F32)
        rank = jnp.sum(onehot * _dot(onehot.astype(BF16), before_b), axis=0, keepdims=True)
        r8rows = pl.ds(pl.multiple_of(i * 8, 8), 8)
        route_ref[0, r8rows, :] = jnp.where(r8 == 0, gsel_f, jnp.where(r8 == 1, rank, 0.0))
        cnt_ref[0, r8rows, :] = jnp.broadcast_to(jnp.sum(onehot, axis=1, keepdims=True), (8, LANES))
        comb_t = jnp.where(e_iota == ROUTE_GROUP_LANE, gsel_f,
                           jnp.where(e_iota == ROUTE_RANK_LANE, rank, comb_t))
        comb_ref[0, rows, :] = comb_t.T
        return carry

    if n_mt == 1:
        phase3(0, 0)
    else:
        lax.fori_loop(0, n_mt, phase3, 0)


class _RowWindow(NamedTuple):
    array: jax.Array
    start: int
    n: int


def _const_spec(a):
    if isinstance(a, _RowWindow):
        assert a.start % a.n == 0
        return a.array, pl.BlockSpec((a.n, a.array.shape[1]), lambda b: (a.start // a.n, 0),
                                     pipeline_mode=pl.Buffered(1))
    nd = a.ndim
    return a, pl.BlockSpec(a.shape, lambda b, _nd=nd: (0,) * _nd, pipeline_mode=pl.Buffered(1))


def _mixer(x, T, mod, mod_index, weights, P, state=None, emit_state=False):
    B, R, _ = x.shape
    n_chunks = R // SUB
    n_blk = R // MOE_BLK
    n_seq = R // T
    has_state = state is not None
    seq_mode = {} if R <= MIX_TM else {"pipeline_mode": pl.Buffered(1)}
    in_specs = [
        pl.BlockSpec((1, R, D_MODEL), lambda b: (b, 0, 0), **seq_mode),
        pl.BlockSpec(mod.shape, lambda b: (0, 0, 0)),
    ]
    args = [x, mod]
    if has_state:
        c0, n0, m0 = state
        in_specs += [
            pl.BlockSpec((1, N_UNITS, HEAD_DIM, HEAD_DIM), lambda b: (b, 0, 0, 0)),
            pl.BlockSpec((1, N_UNITS, HEAD_DIM), lambda b: (b, 0, 0)),
            pl.BlockSpec(memory_space=pltpu.SMEM),
        ]
        args += [c0, n0, m0]
    for name in _MIXER_WEIGHTS:
        operand, spec = _const_spec(weights[name])
        in_specs.append(spec)
        args.append(operand)
    out_shape = [
        jax.ShapeDtypeStruct((B, R, D_MODEL), F32),
        jax.ShapeDtypeStruct((B, R, D_MODEL), BF16),
        jax.ShapeDtypeStruct((B, R, LANES), F32),
        jax.ShapeDtypeStruct((B, n_blk * 8, MOE_BLK), F32),
        jax.ShapeDtypeStruct((B, n_blk * 8, LANES), F32),
    ]
    out_specs = [
        pl.BlockSpec((1, R, D_MODEL), lambda b: (b, 0, 0), **seq_mode),
        pl.BlockSpec((1, R, D_MODEL), lambda b: (b, 0, 0), **seq_mode),
        pl.BlockSpec((1, R, LANES), lambda b: (b, 0, 0)),
        pl.BlockSpec((1, n_blk * 8, MOE_BLK), lambda b: (b, 0, 0)),
        pl.BlockSpec((1, n_blk * 8, LANES), lambda b: (b, 0, 0)),
    ]
    if emit_state:
        out_shape += [
            jax.ShapeDtypeStruct((B, n_seq * N_UNITS, HEAD_DIM, HEAD_DIM), F32),
            jax.ShapeDtypeStruct((B, n_seq * N_UNITS, HEAD_DIM), F32),
            jax.ShapeDtypeStruct((B, n_seq * N_UNITS, LANES), F32),
        ]
        out_specs += [
            pl.BlockSpec((1, n_seq * N_UNITS, HEAD_DIM, HEAD_DIM), lambda b: (b, 0, 0, 0)),
            pl.BlockSpec((1, n_seq * N_UNITS, HEAD_DIM), lambda b: (b, 0, 0)),
            pl.BlockSpec((1, n_seq * N_UNITS, LANES), lambda b: (b, 0, 0)),
        ]
    scratch = [
        pltpu.VMEM((R, D_MLSTM), BF16),
        pltpu.VMEM((n_chunks, D_MLSTM, SUB), BF16),
        pltpu.VMEM((R, D_MLSTM), BF16),
        pltpu.VMEM((R, D_MLSTM), F32),
        pltpu.VMEM((n_chunks, 5 * N_UNITS, SUB), F32),
        pltpu.VMEM((R, D_MODEL), F32),
        pltpu.VMEM((R, D_MODEL), F32),
        pltpu.VMEM((R, D_MLSTM), F32),
        pltpu.VMEM((N_UNITS, HEAD_DIM, 2 * HEAD_DIM), F32),
        pltpu.VMEM((MIX_TM // P, P + 2 * CONV_PAD, D_CONV), F32),
    ]
    return pl.pallas_call(
        functools.partial(_mixer_kernel, R, T, P, has_state, emit_state, mod_index),
        grid=(B,),
        in_specs=in_specs,
        out_specs=out_specs,
        out_shape=out_shape,
        scratch_shapes=scratch,
        compiler_params=pltpu.CompilerParams(
            dimension_semantics=("arbitrary",), vmem_limit_bytes=VMEM_LIMIT),
        name="mixer_T%d" % T,
    )(*args)


def _dest_in_block(group, rank, starts):
    dest = rank
    for g in range(N_GROUPS):
        dest = dest + jnp.where(group == float(g), starts[g], 0.0)
    return dest


def _copy_segments(src_refs, dst_refs, src_starts, dst_starts, n_pieces):
    for g in range(N_GROUPS):
        def body(k, carry, g=g):
            s = pl.multiple_of(src_starts[g] + k * ROW_ALIGN, ROW_ALIGN)
            d = pl.multiple_of(dst_starts[g] + k * ROW_ALIGN, ROW_ALIGN)
            for src, dst in zip(src_refs, dst_refs):
                dst[pl.ds(d, ROW_ALIGN), :] = src[pl.ds(s, ROW_ALIGN), :]
            return carry
        lax.fori_loop(0, n_pieces[g], body, 0)


def _plan_segments(n_blocks, n_tiles, count, start_ref, npiece_ref, off_ref, tgroup_ref, tvalid_ref):
    align_shift = ROW_ALIGN.bit_length() - 1
    tile_shift = MOE_TM.bit_length() - 1

    def block_starts(blk, carry):
        row = jnp.int32(0)
        for g in range(N_GROUPS):
            n = lax.shift_right_logical(count(blk, g) + (ROW_ALIGN - 1), align_shift)
            npiece_ref[blk * N_GROUPS + g] = n
            start_ref[blk * N_GROUPS + g] = row
            row = row + n * ROW_ALIGN
        return carry

    lax.fori_loop(0, n_blocks, block_starts, 0)

    base_row = jnp.int32(0)
    base_tile = jnp.int32(0)
    last_group = jnp.int32(0)
    for g in range(N_GROUPS):
        def seg_offsets(blk, row, g=g, base_row=base_row):
            off_ref[blk * N_GROUPS + g] = base_row + row
            return row + npiece_ref[blk * N_GROUPS + g] * ROW_ALIGN

        rows = lax.fori_loop(0, n_blocks, seg_offsets, jnp.int32(0))
        tiles = lax.shift_right_logical(rows + (MOE_TM - 1), tile_shift)

        def mark_tiles(t, carry, g=g, base_tile=base_tile):
            tgroup_ref[base_tile + t] = g
            tvalid_ref[base_tile + t] = 1
            return carry

        lax.fori_loop(0, tiles, mark_tiles, 0)
        last_group = jnp.where(tiles > 0, g, last_group)
        base_row = base_row + tiles * MOE_TM
        base_tile = base_tile + tiles

    def mark_unused(t, carry):
        tgroup_ref[t] = last_group
        tvalid_ref[t] = 0
        return carry

    lax.fori_loop(base_tile, n_tiles, mark_unused, 0)


def _dispatch_kernel(n_ctx_blocks, n_blocks, n_tiles,
                     h2c_ref, h2l_ref, cbc_ref, cbl_ref, rtc_ref, rtl_ref, cntc_ref, cntl_ref,
                     xs_ref, cs_ref, start_ref, npiece_ref, off_ref, tgroup_ref, tvalid_ref,
                     sx_s, sc_s):
    b = pl.program_id(0)
    is_ctx = b < n_ctx_blocks

    def count(blk, g):
        vc = cntc_ref[jnp.minimum(blk, n_ctx_blocks - 1), pl.ds(g, 1), pl.ds(0, 1)]
        vl = cntl_ref[jnp.maximum(blk - n_ctx_blocks, 0), pl.ds(g, 1), pl.ds(0, 1)]
        return jnp.where(blk < n_ctx_blocks, vc, vl)[0, 0].astype(jnp.int32)

    @pl.when(b == 0)
    def _():
        _plan_segments(n_blocks, n_tiles, count, start_ref, npiece_ref, off_ref, tgroup_ref, tvalid_ref)
        xs_ref[...] = jnp.zeros_like(xs_ref)
        cs_ref[...] = jnp.zeros_like(cs_ref)

    h2 = jnp.where(is_ctx, h2c_ref[0], h2l_ref[0])
    cb = jnp.where(is_ctx, cbc_ref[0], cbl_ref[0])
    rt = jnp.where(is_ctx, rtc_ref[0], rtl_ref[0])
    starts = [start_ref[b * N_GROUPS + g] for g in range(N_GROUPS)]
    dest = _dest_in_block(rt[0:1, :], rt[1:2, :], [s.astype(F32) for s in starts])
    row = lax.broadcasted_iota(jnp.int32, (SORT_ROWS, MOE_BLK), 0).astype(F32)
    perm = (row == dest).astype(F32).astype(BF16)
    cb_hi = cb.astype(BF16)
    cb_lo = (cb - cb_hi.astype(F32)).astype(BF16)
    sx_s[...] = _dot(perm, h2).astype(BF16)
    sc_s[...] = _dot(perm, jnp.concatenate([cb_hi, cb_lo], axis=1)).astype(BF16)
    _copy_segments((sx_s, sc_s), (xs_ref, cs_ref), starts,
                   [off_ref[b * N_GROUPS + g] for g in range(N_GROUPS)],
                   [npiece_ref[b * N_GROUPS + g] for g in range(N_GROUPS)])


def _experts_kernel(tgroup_ref, tvalid_ref, xs_ref, cs_ref, wg_ref, wu_ref, wd_ref, ys_ref):
    i = pl.program_id(0)

    @pl.when(tvalid_ref[i] == 1)
    def _():
        x = xs_ref[...]
        comb = cs_ref[:, :LANES].astype(F32) + cs_ref[:, LANES:].astype(F32)
        lane = lax.broadcasted_iota(jnp.int32, comb.shape, 1)
        first = tgroup_ref[i] * EXPERTS_PER_GROUP
        acc = None
        for j in range(EXPERTS_PER_GROUP):
            gj = _dot(x, wg_ref[j].astype(BF16))
            uj = _dot(x, wu_ref[j].astype(BF16))
            cw = jnp.sum(jnp.where(lane == first + j, comb, 0.0), axis=1, keepdims=True)
            out = _dot((gj * _sigmoid(gj) * uj * cw).astype(BF16), wd_ref[j].astype(BF16))
            acc = out if acc is None else acc + out
        ys_ref[...] = acc.astype(BF16)

    @pl.when(tvalid_ref[i] == 0)
    def _():
        ys_ref[...] = jnp.zeros_like(ys_ref)


def _combine_kernel(n_ctx_blocks, blocks_per_lat_seq, start_ref, npiece_ref, off_ref,
                    x1c_ref, x1l_ref, cbc_ref, cbl_ref, ys_ref, mod_ref, gf_ref, yc_ref, yl_ref, loc_s):
    b = pl.program_id(0)
    is_ctx = b < n_ctx_blocks
    starts = [start_ref[b * N_GROUPS + g] for g in range(N_GROUPS)]
    loc_s[...] = jnp.zeros_like(loc_s)
    _copy_segments((ys_ref,), (loc_s,), [off_ref[b * N_GROUPS + g] for g in range(N_GROUPS)], starts,
                   [npiece_ref[b * N_GROUPS + g] for g in range(N_GROUPS)])
    cb = jnp.where(is_ctx, cbc_ref[0], cbl_ref[0])
    dest = _dest_in_block(cb[:, ROUTE_GROUP_LANE:ROUTE_GROUP_LANE + 1],
                          cb[:, ROUTE_RANK_LANE:ROUTE_RANK_LANE + 1],
                          [s.astype(F32) for s in starts])
    col = lax.broadcasted_iota(jnp.int32, (MOE_BLK, SORT_ROWS), 1).astype(F32)
    unperm = (col == dest).astype(F32).astype(BF16)
    moe = _dot(unperm, loc_s[...])
    x1 = jnp.where(is_ctx, x1c_ref[0], x1l_ref[0])
    mrow = jnp.where(is_ctx, 0, 1 + jnp.maximum(b - n_ctx_blocks, 0) // blocks_per_lat_seq)
    x2 = x1 + mod_ref[N_ADA - 1, pl.ds(mrow, 1), :] * moe
    y = x2 * lax.rsqrt(jnp.mean(x2 * x2, axis=-1, keepdims=True) + EPS) * gf_ref[...]

    @pl.when(is_ctx)
    def _():
        yc_ref[0] = y

    @pl.when(jnp.logical_not(is_ctx))
    def _():
        yl_ref[0] = y


def _moe(x1c, x1l, h2c, h2l, cbc, cbl, rtc, rtl, cntc, cntl, mod, blocks_per_lat_seq, wg, wu, wd, gf):
    nc, nl = x1c.shape[0], x1l.shape[0]
    nb = nc + nl
    n_rows_max = nb * MOE_BLK + nb * N_GROUPS * (ROW_ALIGN - 1) + N_GROUPS * (MOE_TM - ROW_ALIGN)
    n_tiles = -(-n_rows_max // MOE_TM)
    ns = n_tiles * MOE_TM

    cmap = lambda b, *_: (jnp.minimum(b, nc - 1), 0, 0)
    lmap = lambda b, *_: (jnp.maximum(b - nc, 0), 0, 0)
    whole = lambda *_: (0, 0)
    once = {"pipeline_mode": pl.Buffered(1)}
    arb = pltpu.CompilerParams(dimension_semantics=("arbitrary",), vmem_limit_bytes=VMEM_LIMIT)
    smem = pl.BlockSpec(memory_space=pltpu.SMEM)
    seg_i32 = jax.ShapeDtypeStruct((nb * N_GROUPS,), jnp.int32)
    tile_i32 = jax.ShapeDtypeStruct((n_tiles,), jnp.int32)

    xs, cs, start, npiece, off, tgroup, tvalid = pl.pallas_call(
        functools.partial(_dispatch_kernel, nc, nb, n_tiles),
        grid_spec=pltpu.PrefetchScalarGridSpec(
            num_scalar_prefetch=0, grid=(nb,),
            in_specs=[
                pl.BlockSpec((1, MOE_BLK, D_MODEL), cmap), pl.BlockSpec((1, MOE_BLK, D_MODEL), lmap),
                pl.BlockSpec((1, MOE_BLK, LANES), cmap), pl.BlockSpec((1, MOE_BLK, LANES), lmap),
                pl.BlockSpec((1, 8, MOE_BLK), cmap), pl.BlockSpec((1, 8, MOE_BLK), lmap),
                pl.BlockSpec(cntc.shape, lambda b: (0, 0, 0)), pl.BlockSpec(cntl.shape, lambda b: (0, 0, 0)),
            ],
            out_specs=[pl.BlockSpec((ns, D_MODEL), whole, **once), pl.BlockSpec((ns, 2 * LANES), whole, **once),
                       smem, smem, smem, smem, smem],
            scratch_shapes=[pltpu.VMEM((SORT_ROWS, D_MODEL), BF16), pltpu.VMEM((SORT_ROWS, 2 * LANES), BF16)],
        ),
        out_shape=[jax.ShapeDtypeStruct((ns, D_MODEL), BF16), jax.ShapeDtypeStruct((ns, 2 * LANES), BF16),
                   seg_i32, seg_i32, seg_i32, tile_i32, tile_i32],
        compiler_params=arb,
        name="moe_dispatch",
    )(h2c, h2l, cbc, cbl, rtc, rtl, cntc, cntl)

    wmap = lambda i, tg, tv: (tg[i], 0, 0)
    ys = pl.pallas_call(
        _experts_kernel,
        grid_spec=pltpu.PrefetchScalarGridSpec(
            num_scalar_prefetch=2, grid=(n_tiles,),
            in_specs=[
                pl.BlockSpec((MOE_TM, D_MODEL), lambda i, *_: (i, 0)),
                pl.BlockSpec((MOE_TM, 2 * LANES), lambda i, *_: (i, 0)),
                pl.BlockSpec((EXPERTS_PER_GROUP, D_MODEL, D_EXPERT), wmap),
                pl.BlockSpec((EXPERTS_PER_GROUP, D_MODEL, D_EXPERT), wmap),
                pl.BlockSpec((EXPERTS_PER_GROUP, D_EXPERT, D_MODEL), wmap),
            ],
            out_specs=pl.BlockSpec((MOE_TM, D_MODEL), lambda i, *_: (i, 0)),
        ),
        out_shape=jax.ShapeDtypeStruct((ns, D_MODEL), BF16),
        compiler_params=arb,
        name="moe_experts",
    )(tgroup, tvalid, xs, cs, wg, wu, wd)

    yc, yl = pl.pallas_call(
        functools.partial(_combine_kernel, nc, blocks_per_lat_seq),
        grid_spec=pltpu.PrefetchScalarGridSpec(
            num_scalar_prefetch=3, grid=(nb,),
            in_specs=[
                pl.BlockSpec((1, MOE_BLK, D_MODEL), cmap), pl.BlockSpec((1, MOE_BLK, D_MODEL), lmap),
                pl.BlockSpec((1, MOE_BLK, LANES), cmap), pl.BlockSpec((1, MOE_BLK, LANES), lmap),
                pl.BlockSpec((ns, D_MODEL), whole, **once),
                pl.BlockSpec(mod.shape, lambda *_: (0, 0, 0)),
                pl.BlockSpec((1, D_MODEL), whole),
            ],
            out_specs=[pl.BlockSpec((1, MOE_BLK, D_MODEL), cmap), pl.BlockSpec((1, MOE_BLK, D_MODEL), lmap)],
            scratch_shapes=[pltpu.VMEM((SORT_ROWS, D_MODEL), BF16)],
        ),
        out_shape=[jax.ShapeDtypeStruct((nc, MOE_BLK, D_MODEL), F32),
                   jax.ShapeDtypeStruct((nl, MOE_BLK, D_MODEL), F32)],
        compiler_params=arb,
        name="moe_combine",
    )(start, npiece, off, x1c, x1l, cbc, cbl, ys, mod, gf)
    return yc, yl


def _prep_weights(norm1_g, w_in, b_in, b_gates, w_dw, b_dw, conv_ln_g, conv_ln_b, w_conv_out,
                  mlstm_hn_g, w_mlstm_out, w_o, norm2_g, w_rg, b_rg, w_re, b_re):
    s_a = 2 * D_CONV
    s_q = s_a + D_MLSTM
    s_k = s_q + D_MLSTM
    s_v = s_k + D_MLSTM
    s_o = s_v + D_MLSTM
    s_g = s_o + 4 * N_HEADS
    row = lambda v: v.reshape(1, -1).astype(F32)
    w_t = w_in.T
    keep = [(0, s_q), (s_k, s_o), (s_g, w_in.shape[1])]
    wrow = _transpose_cast(w_t, [r for a, b in keep for r in range(a, b, WPREP_ROWS)])
    bg = (b_in[s_o:s_g] + b_gates.reshape(-1)).reshape(2, 2, N_HEADS).transpose(1, 0, 2).reshape(-1, 1)
    row_window = lambda start, n: _RowWindow(w_t, start, n)
    n_rt = N_EXPERTS + N_GROUPS
    wrt = jnp.pad(jnp.concatenate([w_re, w_rg], axis=1), ((0, 0), (0, LANES - n_rt)))
    wrt_hi = wrt.astype(BF16)
    wrt2 = jnp.concatenate([wrt_hi, (wrt - wrt_hi.astype(F32)).astype(BF16)], axis=1)
    brtT = jnp.pad(jnp.concatenate([b_re, b_rg]), (0, LANES - n_rt)).reshape(LANES, 1)
    return {
        "g1": row(norm1_g),
        "wrow": wrow, "brow": row(jnp.concatenate([b_in[a:b] for a, b in keep] + [b_in[s_q:s_k]])),
        "wkT": row_window(s_q, D_MLSTM), "wgifT": row_window(s_o, 4 * N_HEADS), "bgifT": bg,
        "wdw": w_dw.astype(F32), "bdw": row(b_dw), "lng": row(conv_ln_g), "lnb": row(conv_ln_b),
        "wco": w_conv_out.astype(BF16), "hng": row(mlstm_hn_g), "wmo": w_mlstm_out.astype(BF16),
        "wo": w_o.astype(BF16), "g2": row(norm2_g), "wrt2": wrt2, "brtT": brtT,
    }


def kernel(x_prompt, x_sample, state_C, state_n, state_m, c, c_ctx, norm1_g, w_ada, b_ada, w_in, b_in, b_gates, w_dw, b_dw, conv_ln_g, conv_ln_b, w_conv_out, mlstm_hn_g, w_mlstm_out, w_o, norm2_g, w_rg, b_rg, w_re, b_re, w_e_gate, w_e_up, w_e_down, norm_final_g):
    B, S, _ = x_prompt.shape
    Bd, Sd, _ = x_sample.shape
    assert w_ada.shape[0] == 1, "single trunk layer"
    assert S == SUB and Sd % SUB == 0

    mod = _ada(c_ctx.reshape(1, -1), c, w_ada[0], b_ada[0].reshape(1, -1))

    wts = _prep_weights(norm1_g[0], w_in[0], b_in[0], b_gates[0], w_dw[0], b_dw[0], conv_ln_g[0],
                        conv_ln_b[0], w_conv_out[0], mlstm_hn_g[0], w_mlstm_out[0], w_o[0],
                        norm2_g[0], w_rg[0], b_rg[0], w_re[0], b_re[0])

    x1p, h2p, cbp, rtp, cntp, c_new, n_new, m_new = _mixer(
        x_prompt.reshape(B * S // MIX_TM, MIX_TM, D_MODEL), S, mod, lambda b: 0, wts, P=S, emit_state=True)

    state = (state_C[:, 0].reshape(Bd, N_UNITS, HEAD_DIM, HEAD_DIM), state_n[:, 0].reshape(Bd, N_UNITS, HEAD_DIM),
             state_m[:, 0].reshape(Bd, N_UNITS))
    x1s, h2s, cbs, rts, cnts = _mixer(x_sample, Sd, mod, lambda b: 1 + b, wts, P=GRID_W, state=state)

    nc, nl = B * S // MOE_BLK, Bd * Sd // MOE_BLK
    blk = lambda a, n: a.reshape(n, MOE_BLK, a.shape[-1])
    yp, ys = _moe(blk(x1p, nc), blk(x1s, nl), blk(h2p, nc), blk(h2s, nl), blk(cbp, nc), blk(cbs, nl),
                  rtp.reshape(nc, 8, MOE_BLK), rts.reshape(nl, 8, MOE_BLK),
                  cntp.reshape(nc, 8, LANES), cnts.reshape(nl, 8, LANES),
                  mod, Sd // MOE_BLK, w_e_gate[0], w_e_up[0], w_e_down[0], norm_final_g.reshape(1, -1))

    return (yp.reshape(B, S, D_MODEL), ys.reshape(Bd, Sd, D_MODEL),
            c_new.reshape(B, 1, 2, N_HEADS, HEAD_DIM, HEAD_DIM),
            n_new.reshape(B, 1, 2, N_HEADS, HEAD_DIM),
            m_new[:, :, 0].reshape(B, 1, 2, N_HEADS))
```

```python
import functools
from typing import NamedTuple

import jax
import jax.numpy as jnp
from jax import lax
from jax.experimental import pallas as pl
from jax.experimental.pallas import tpu as pltpu

D_MODEL = 1024
D_CONV = 512
CONV_K = 31
D_MLSTM = 512
N_HEADS = 4
HEAD_DIM = D_MLSTM // N_HEADS
N_GROUPS = 4
EXPERTS_PER_GROUP = 4
N_EXPERTS = N_GROUPS * EXPERTS_PER_GROUP
D_EXPERT = 256
N_ADA = 6
EPS = 1e-6
GRID_W = 64

LANES = 128
SUB = 256
CONV_PAD = 16
CONV_RB = 64
N_UNITS = 2 * N_HEADS
ROW_ALIGN = 16
MOE_TM = 512
MIX_TM = 512
MOE_BLK = MIX_TM
SORT_ROWS = MOE_BLK + N_GROUPS * ROW_ALIGN
ADA_PER_STEP = 2
WPREP_ROWS = 512
ROUTE_GROUP_LANE = N_EXPERTS
ROUTE_RANK_LANE = N_EXPERTS + 1
VMEM_LIMIT = 58 * 1024 * 1024

BF16 = jnp.bfloat16
F32 = jnp.float32
NT_DIMS = (((1,), (1,)), ((), ()))


def _dot(a, b):
    return jnp.dot(a, b, preferred_element_type=F32)


def _dot_nt(a, b, precision=None):
    return lax.dot_general(a, b, NT_DIMS, preferred_element_type=F32, precision=precision)


def _sigmoid(x):
    return 0.5 * jnp.tanh(0.5 * x) + 0.5


def _log_sigmoid(x):
    return jnp.minimum(x, 0.0) - jnp.log1p(jnp.exp(-jnp.abs(x)))


def _split3(x):
    hi = x.astype(BF16).astype(F32)
    r1 = x - hi
    mid = r1.astype(BF16).astype(F32)
    lo = (r1 - mid).astype(BF16).astype(F32)
    return hi, mid, lo


def _ada_kernel(cctx_ref, c_ref, w_ref, b_ref, o_ref):
    n = 1 + c_ref.shape[0]
    c = jnp.concatenate([cctx_ref[...], c_ref[...], jnp.zeros((8 - n, D_MODEL), F32)], axis=0)
    s = (c * _sigmoid(c)).astype(BF16)
    out = _dot(s, w_ref[...].astype(BF16)) + b_ref[...]
    for v in range(ADA_PER_STEP):
        o_ref[v] = out[:, v * D_MODEL:(v + 1) * D_MODEL]


def _ada(c_ctx, c, w_ada, b_ada):
    return pl.pallas_call(
        _ada_kernel,
        grid=(N_ADA // ADA_PER_STEP,),
        in_specs=[
            pl.BlockSpec(c_ctx.shape, lambda j: (0, 0)),
            pl.BlockSpec(c.shape, lambda j: (0, 0)),
            pl.BlockSpec((D_MODEL, ADA_PER_STEP * D_MODEL), lambda j: (0, j)),
            pl.BlockSpec((1, ADA_PER_STEP * D_MODEL), lambda j: (0, j)),
        ],
        out_specs=pl.BlockSpec((ADA_PER_STEP, 8, D_MODEL), lambda j: (j, 0, 0)),
        out_shape=jax.ShapeDtypeStruct((N_ADA, 8, D_MODEL), F32),
        compiler_params=pltpu.CompilerParams(dimension_semantics=("arbitrary",)),
        name="ada",
    )(c_ctx, c, w_ada, b_ada)


def _transpose_cast_kernel(starts_ref, wt_ref, o_ref):
    o_ref[...] = wt_ref[...].astype(BF16).T


def _transpose_cast(w_t, row_starts):
    n, k = len(row_starts), w_t.shape[1]
    return pl.pallas_call(
        _transpose_cast_kernel,
        grid_spec=pltpu.PrefetchScalarGridSpec(
            num_scalar_prefetch=1, grid=(n,),
            in_specs=[pl.BlockSpec((pl.Element(WPREP_ROWS), pl.Element(k)), lambda j, starts: (starts[j] * 8, 0))],
            out_specs=pl.BlockSpec((k, WPREP_ROWS), lambda j, starts: (0, j)),
        ),
        out_shape=jax.ShapeDtypeStruct((k, n * WPREP_ROWS), BF16),
        compiler_params=pltpu.CompilerParams(dimension_semantics=("arbitrary",)),
        name="transpose_cast",
    )(jnp.array([r // 8 for r in row_starts], jnp.int32), w_t)


WROW_OFFSET = {"wq": 2 * D_CONV, "wv": 2 * D_CONV + D_MLSTM, "wog": 2 * D_CONV + 2 * D_MLSTM,
               "wgm": 2 * D_CONV + 3 * D_MLSTM}
BROW_K_OFFSET = 2 * D_CONV + 3 * D_MLSTM + 2 * D_MODEL

_MIXER_WEIGHTS = (
    "g1", "wrow", "brow", "wkT", "wgifT", "bgifT", "wdw", "bdw", "lng", "lnb",
    "wco", "hng", "wmo", "wo", "g2", "wrt2", "brtT",
)


def _zero_after(x):
    bits = lax.bitcast_convert_type(x, jnp.uint32)
    bits = lax.shift_right_logical(lax.shift_right_logical(bits, jnp.uint32(16)), jnp.uint32(16))
    return lax.bitcast_convert_type(bits, F32)[0:1, :]


def _conv_block(upad_s, seg, base, cs, wdw_ref, bdw_ref, after=None):
    sub = 8
    first = CONV_PAD - CONV_K // 2
    acc = jnp.broadcast_to(bdw_ref[0:1, cs], (CONV_RB, LANES))
    for r in range(sub):
        z = None
        for a in range((CONV_K + first + sub - 1) // sub):
            j = sub * a + r - first
            if 0 <= j < CONV_K:
                lo = base + sub * a
                tap = wdw_ref[j:j + 1, cs] if after is None else wdw_ref[j:j + 1, cs] + after
                term = tap * upad_s[seg, lo:lo + CONV_RB + sub, cs]
                z = term if z is None else z + term
        acc = acc + z[r:r + CONV_RB, :]
    return acc


def _mixer_kernel(R, T, P, has_state, emit_state, mod_index, *refs):
    L = SUB
    n_mt = R // MIX_TM
    cpm = MIX_TM // L
    n_seq = R // T
    cps = T // L
    nseg = MIX_TM // P
    assert not has_state or n_seq == 1
    it = iter(refs)
    x_ref = next(it)
    mod_ref = next(it)
    if has_state:
        c0_ref = next(it)
        n0_ref = next(it)
        m0_ref = next(it)
    w = {name: next(it) for name in _MIXER_WEIGHTS}
    x1_ref = next(it)
    h2_ref = next(it)
    comb_ref = next(it)
    route_ref = next(it)
    cnt_ref = next(it)
    if emit_state:
        cout_ref = next(it)
        nout_ref = next(it)
        mout_ref = next(it)
    (q_s, kT_s, v_s, so_s, scan_s, ma_s, sgb_s, hm_s, cst_s, upad_s) = [next(it) for _ in range(10)]

    cond_row = mod_index(pl.program_id(0))

    def mod_row(i):
        return mod_ref[i, pl.ds(cond_row, 1), :]

    zpad = jnp.zeros((CONV_PAD, D_CONV), F32)
    for seg in range(nseg):
        upad_s[seg, 0:CONV_PAD, :] = zpad
        upad_s[seg, CONV_PAD + P:CONV_PAD + P + CONV_PAD, :] = zpad

    t_idx = lax.broadcasted_iota(jnp.int32, (L, L), 0)
    s_idx = lax.broadcasted_iota(jnp.int32, (L, L), 1)
    lower = s_idx <= t_idx
    upper = s_idx >= t_idx
    triu_b = upper.astype(F32).astype(BF16)
    lane_u = lax.broadcasted_iota(jnp.int32, (N_UNITS, L), 1)
    is_bwd = lax.broadcasted_iota(jnp.int32, (N_UNITS, L), 0) >= N_HEADS

    def gate_scan(g):
        gi, lf = g[:N_UNITS], _log_sigmoid(g[N_UNITS:])
        pr = _dot(jnp.concatenate(_split3(lf), axis=0).astype(BF16), triu_b)
        pre = pr[0:N_UNITS] + pr[N_UNITS:2 * N_UNITS] + pr[2 * N_UNITS:]
        tot = pre[:, L - 1:L]
        bsum = jnp.where(is_bwd, tot - pre + lf, pre)
        a = gi - bsum
        pm, sm, k = a, a, 1
        while k < L:
            pm = jnp.where(lane_u >= k, jnp.maximum(pm, pltpu.roll(pm, k, axis=1)), pm)
            sm = jnp.where(lane_u < L - k, jnp.maximum(sm, pltpu.roll(sm, L - k, axis=1)), sm)
            k *= 2
        wide = lambda v: jnp.broadcast_to(v, (N_UNITS, L))
        return jnp.concatenate([a, jnp.where(is_bwd, sm, pm), bsum, wide(tot),
                                wide(jnp.max(a, axis=1, keepdims=True))], axis=0)

    def phase1(i, carry):
        r0 = pl.multiple_of(i * MIX_TM, MIX_TM)
        rows = pl.ds(r0, MIX_TM)
        x = x_ref[0, rows, :]
        xn = x * lax.rsqrt(jnp.mean(x * x, axis=-1, keepdims=True) + EPS) * w["g1"][...]
        hb = (xn * (1.0 + mod_row(1)) + mod_row(0)).astype(BF16)

        gates = _dot_nt(w["wgifT"][...].astype(BF16), hb)
        gates = jnp.concatenate([gates[d * 2 * N_HEADS + g * N_HEADS:d * 2 * N_HEADS + (g + 1) * N_HEADS]
                                 for g in range(2) for d in range(2)], axis=0) + w["bgifT"][...]
        for j in range(cpm):
            scan_s[i * cpm + j] = gate_scan(gates[:, j * L:(j + 1) * L])
        ag = _dot(hb, w["wrow"][:, :2 * D_CONV]) + w["brow"][:, :2 * D_CONV]
        u = ag[:, :D_CONV] * _sigmoid(ag[:, D_CONV:])
        for seg in range(nseg):
            upad_s[seg, CONV_PAD:CONV_PAD + P, :] = u[seg * P:(seg + 1) * P, :]

        def proj(name, c0, gate, width=2 * LANES):
            w0 = WROW_OFFSET[name] + c0
            b = w["brow"][:, w0:w0 + width]
            if gate is not None:
                b = b + jnp.concatenate([gate] * (width // LANES), axis=1)
            return _dot(hb, w["wrow"][:, w0:w0 + width]) + b

        last = lambda z: z[-8:, -LANES:]
        bk_row = w["brow"][:, BROW_K_OFFSET:BROW_K_OFFSET + D_MLSTM]
        bk_col = jnp.concatenate([bk_row, jnp.zeros((LANES - 1, D_MLSTM), F32)], axis=0).T[:, 0:1]

        def gm_a(c0, gate):
            z = proj("wgm", c0, gate)
            ma_s[rows, c0:c0 + 2 * LANES] = _sigmoid(z)
            return last(z)

        def gm_b(c0, gate):
            z = proj("wgm", D_MODEL + c0, gate)
            sgb_s[rows, c0:c0 + 2 * LANES] = _sigmoid(z)
            return last(z)

        def q_part(c0, gate):
            z = proj("wq", c0, gate)
            q_s[rows, c0:c0 + 2 * LANES] = (z * (HEAD_DIM ** -0.5)).astype(BF16)
            return last(z)

        def v_part(c0, gate):
            z = proj("wv", c0, gate)
            v_s[rows, c0:c0 + 2 * LANES] = z.astype(BF16)
            return last(z)

        def o_part(c0, gate):
            z = proj("wog", c0, gate)
            so_s[rows, c0:c0 + 2 * LANES] = _sigmoid(z)
            return last(z)

        def k_part(c0, gate):
            rs = slice(c0, c0 + 2 * LANES)
            b = bk_col[rs, :] if gate is None else bk_col[rs, :] + gate[:, 0:1]
            z = _dot_nt(w["wkT"][rs, :].astype(BF16), hb) + b
            kt = z.astype(BF16)
            for j in range(cpm):
                kT_s[i * cpm + j, rs, :] = kt[:, j * L:(j + 1) * L]
            return last(z)

        jobs = ([functools.partial(gm_a, c0) for c0 in range(0, D_MODEL, 2 * LANES)]
                + [functools.partial(gm_b, c0) for c0 in range(0, D_MODEL, 2 * LANES)]
                + [functools.partial(f, c0) for f in (q_part, k_part, v_part, o_part)
                   for c0 in range(0, D_MLSTM, 2 * LANES)])
        n_jobs = len(jobs)
        conv = {}
        after = None
        n_pieces = (D_CONV // LANES) * nseg * (P // CONV_RB)
        for cb in range(D_CONV // LANES):
            cs = slice(cb * LANES, (cb + 1) * LANES)
            for seg in range(nseg):
                for rb in range(P // CONV_RB):
                    blk = _conv_block(upad_s, seg, rb * CONV_RB, cs, w["wdw"], w["bdw"], after)
                    conv[(cb, seg, rb)] = blk
                    if jobs and len(conv) * n_jobs >= (n_jobs - len(jobs) + 1) * n_pieces:
                        after = _zero_after(jobs.pop(0)(_zero_after(blk[-8:, :])))
        for job in jobs:
            job(None)
        cu = jnp.concatenate(
            [jnp.concatenate([conv[(cb, seg, rb)] for seg in range(nseg) for rb in range(P // CONV_RB)], axis=0)
             for cb in range(D_CONV // LANES)], axis=1)
        mu = jnp.mean(cu, axis=-1, keepdims=True)
        cc = cu - mu
        cn = cc * lax.rsqrt(jnp.mean(cc * cc, axis=-1, keepdims=True) + EPS) * w["lng"][...] + w["lnb"][...]
        ca = (cn * _sigmoid(cn)).astype(BF16)
        ma_s[rows, :] = ma_s[rows, :] * _dot(ca, w["wco"][...])
        return carry

    if n_mt == 1:
        phase1(0, 0)
    else:
        lax.fori_loop(0, n_mt, phase1, 0)

    ones_col = (lax.broadcasted_iota(jnp.int32, (L, HEAD_DIM), 1) == 0).astype(F32).astype(BF16)
    pad_rows = jnp.zeros((LANES - 3 * N_UNITS, L), F32)

    def gate_prep(c, m_vec):
        sc = scan_s[c]
        a, run_max, bsum = sc[0:N_UNITS], sc[N_UNITS:2 * N_UNITS], sc[2 * N_UNITS:3 * N_UNITS]
        tot, a_max = sc[3 * N_UNITS:4 * N_UNITS, 0:1], sc[4 * N_UNITS:5 * N_UNITS, 0:1]
        big_m = jnp.maximum(m_vec, run_max)
        m_end = jnp.maximum(m_vec, a_max)
        cols = jnp.concatenate(
            [big_m, jnp.exp(m_vec - big_m), jnp.exp(-bsum - big_m), pad_rows], axis=0).T
        return a, cols, jnp.exp(a - m_end), jnp.exp(m_vec - m_end), tot + m_end

    def unit_group(dirs, c, prep, first_chunk, want_state):
        a, cols, wk, decay, _ = prep
        rows = slice(c * L, (c + 1) * L)
        heads = range(N_HEADS)
        units = [(d, hd) for d in dirs for hd in heads]
        hs = [slice(hd * HEAD_DIM, (hd + 1) * HEAD_DIM) for hd in heads]
        idx = {u: u[0] * N_HEADS + u[1] for u in units}
        col = lambda k, u: cols[:, k * N_UNITS + idx[u]:k * N_UNITS + idx[u] + 1]
        row = lambda arr, u: arr[idx[u]:idx[u] + 1, :]
        chained = has_state or not first_chunk
        qc = [q_s[rows, hs[hd]] for hd in heads]
        kTc = [kT_s[c, hs[hd], :] for hd in heads]
        vaug = [jnp.concatenate([v_s[rows, hs[hd]], ones_col], axis=1) for hd in heads]
        qk = [_dot(qc[hd], kTc[hd]) for hd in heads]
        s_mat = {u: (qk[u[1]] * jnp.where(lower if u[0] == 0 else upper, jnp.exp(row(a, u) - col(0, u)), 0.0)
                     ).astype(BF16) for u in units}
        nd = {u: _dot(s_mat[u], vaug[u[1]]) for u in units}
        if chained:
            nd = {u: nd[u] + col(1, u) * _dot(qc[u[1]], cst_s[idx[u]].astype(BF16)) for u in units}
        h = {u: nd[u][:, :HEAD_DIM] * (1.0 / jnp.maximum(jnp.abs(nd[u][:, HEAD_DIM:HEAD_DIM + 1]), col(2, u)))
             for u in units}
        for hd in heads:
            total = h[(dirs[0], hd)]
            for d in dirs[1:]:
                total = total + h[(d, hd)]
            if dirs[0] == 0:
                hm_s[rows, hs[hd]] = total
            else:
                hm_s[rows, hs[hd]] = hm_s[rows, hs[hd]] + total
        if want_state:
            kw = {u: (kTc[u[1]].astype(F32) * row(wk, u)).astype(BF16) for u in units}
            upd = {u: _dot(kw[u], vaug[u[1]]) for u in units}
            for u in units:
                cst_s[idx[u]] = (upd[u] + row(decay, u) * cst_s[idx[u]]) if chained else upd[u]

    dir_rows = lax.broadcasted_iota(jnp.int32, (N_UNITS, 1), 0) >= N_HEADS
    for seq in range(n_seq):
        if has_state:
            n_cols = jnp.concatenate([n0_ref[0], jnp.zeros((LANES - N_UNITS, HEAD_DIM), F32)], axis=0).T
            first_lane = lax.broadcasted_iota(jnp.int32, (HEAD_DIM, HEAD_DIM), 1) == 0
            for idx in range(N_UNITS):
                cst_s[idx, :, :HEAD_DIM] = c0_ref[0, idx]
                cst_s[idx, :, HEAD_DIM:] = jnp.where(first_lane, n_cols[:, idx:idx + 1], 0.0)
            unit_row = lax.broadcasted_iota(jnp.int32, (N_UNITS, 1), 0)
            m_vec = jnp.zeros((N_UNITS, 1), F32)
            for idx in range(N_UNITS):
                m_vec = jnp.where(unit_row == idx, m0_ref[pl.program_id(0), idx], m_vec)
        else:
            m_vec = jnp.zeros((N_UNITS, 1), F32)
        if cps == 1:
            prep = gate_prep(seq, m_vec)
            unit_group([0, 1], seq, prep, True, emit_state)
            m_vec = prep[4]
        else:
            for d in range(2):
                order = list(range(cps)) if d == 0 else list(range(cps - 1, -1, -1))
                for pos, c in enumerate(order):
                    prep = gate_prep(seq * cps + c, m_vec)
                    unit_group([d], seq * cps + c, prep, pos == 0, emit_state or pos < cps - 1)
                    m_vec = jnp.where(dir_rows == (d == 1), prep[4], m_vec)
        if emit_state:
            for idx in range(N_UNITS):
                caug = cst_s[idx]
                cout_ref[0, seq * N_UNITS + idx] = caug[:, :HEAD_DIM]
                nout_ref[0, seq * N_UNITS + idx:seq * N_UNITS + idx + 1, :] = caug[:, HEAD_DIM:].T[0:1, :]
            mout_ref[0, seq * N_UNITS:(seq + 1) * N_UNITS, :] = jnp.broadcast_to(m_vec, (N_UNITS, LANES))

    e_iota = lax.broadcasted_iota(jnp.int32, (LANES, MIX_TM), 0)
    g_of_e = lax.shift_right_logical(e_iota, 2)
    j_of_e = lax.bitwise_and(e_iota, EXPERTS_PER_GROUP - 1)
    r8 = lax.broadcasted_iota(jnp.int32, (8, MIX_TM), 0)
    before_b = (lax.broadcasted_iota(jnp.int32, (MOE_BLK, MOE_BLK), 0)
                < lax.broadcasted_iota(jnp.int32, (MOE_BLK, MOE_BLK), 1)).astype(F32).astype(BF16)

    def phase3(i, carry):
        r0 = pl.multiple_of(i * MIX_TM, MIX_TM)
        rows = pl.ds(r0, MIX_TM)
        hm = hm_s[rows, :]
        heads = []
        for hd in range(N_HEADS):
            hh = hm[:, hd * HEAD_DIM:(hd + 1) * HEAD_DIM]
            heads.append(hh * lax.rsqrt(jnp.mean(hh * hh, axis=-1, keepdims=True) + EPS))
        hn = jnp.concatenate(heads, axis=1) * w["hng"][...]
        hb2 = (so_s[rows, :] * hn).astype(BF16)
        br_b = _dot(hb2, w["wmo"][...])
        mixed = (ma_s[rows, :] + sgb_s[rows, :] * br_b).astype(BF16)
        x1 = x_ref[0, rows, :] + mod_row(2) * _dot(mixed, w["wo"][...])
        x1_ref[0, rows, :] = x1
        xn = x1 * lax.rsqrt(jnp.mean(x1 * x1, axis=-1, keepdims=True) + EPS) * w["g2"][...]
        h2 = xn * (1.0 + mod_row(4)) + mod_row(3)
        h2_ref[0, rows, :] = h2.astype(BF16)

        h2_hi = h2.astype(BF16)
        h2_lo = (h2 - h2_hi.astype(F32)).astype(BF16)
        lg = _dot(h2_hi, w["wrt2"][...])
        lg = lg[:, :LANES] + lg[:, LANES:] + _dot(h2_lo, w["wrt2"][:, :LANES])
        lt = lg.T + w["brtT"][...]
        gl = [lt[N_EXPERTS + g:N_EXPERTS + g + 1, :] for g in range(N_GROUPS)]
        best, gsel = gl[0], jnp.zeros((1, MIX_TM), jnp.int32)
        for g in range(1, N_GROUPS):
            better = gl[g] > best
            gsel = jnp.where(better, g, gsel)
            best = jnp.where(better, gl[g], best)
        gp_sel = 1.0 / sum(jnp.exp(v - best) for v in gl)
        el = []
        for j in range(EXPERTS_PER_GROUP):
            v = lt[j:j + 1, :]
            for g in range(1, N_GROUPS):
                r = g * EXPERTS_PER_GROUP + j
                v = jnp.where(gsel == g, lt[r:r + 1, :], v)
            el.append(v)
        l1, e1 = el[0], jnp.zeros((1, MIX_TM), jnp.int32)
        for j in range(1, EXPERTS_PER_GROUP):
            better = el[j] > l1
            e1 = jnp.where(better, j, e1)
            l1 = jnp.where(better, el[j], l1)
        l2 = jnp.full((1, MIX_TM), -jnp.inf, F32)
        e2 = jnp.zeros((1, MIX_TM), jnp.int32)
        for j in range(EXPERTS_PER_GROUP):
            better = jnp.logical_and(e1 != j, el[j] > l2)
            e2 = jnp.where(better, j, e2)
            l2 = jnp.where(better, el[j], l2)
        r2 = jnp.exp(l2 - l1)
        wt1 = gp_sel / (1.0 + r2)
        wt2 = gp_sel * r2 / (1.0 + r2)
        in_group = g_of_e == gsel
        comb_t = (jnp.where(jnp.logical_and(in_group, j_of_e == e1), wt1, 0.0)
                  + jnp.where(jnp.logical_and(in_group, j_of_e == e2), wt2, 0.0))

        onehot = (r8 == gsel).astype(F32)
        gsel_f = gsel.astype(F32)
        rank = jnp.sum(onehot * _dot(onehot.astype(BF16), before_b), axis=0, keepdims=True)
        r8rows = pl.ds(pl.multiple_of(i * 8, 8), 8)
        route_ref[0, r8rows, :] = jnp.where(r8 == 0, gsel_f, jnp.where(r8 == 1, rank, 0.0))
        cnt_ref[0, r8rows, :] = jnp.broadcast_to(jnp.sum(onehot, axis=1, keepdims=True), (8, LANES))
        comb_t = jnp.where(e_iota == ROUTE_GROUP_LANE, gsel_f,
                           jnp.where(e_iota == ROUTE_RANK_LANE, rank, comb_t))
        comb_ref[0, rows, :] = comb_t.T
        return carry

    if n_mt == 1:
        phase3(0, 0)
    else:
        lax.fori_loop(0, n_mt, phase3, 0)


class _RowWindow(NamedTuple):
    array: jax.Array
    start: int
    n: int


def _const_spec(a):
    if isinstance(a, _RowWindow):
        assert a.start % a.n == 0
        return a.array, pl.BlockSpec((a.n, a.array.shape[1]), lambda b: (a.start // a.n, 0),
                                     pipeline_mode=pl.Buffered(1))
    nd = a.ndim
    return a, pl.BlockSpec(a.shape, lambda b, _nd=nd: (0,) * _nd, pipeline_mode=pl.Buffered(1))


def _mixer(x, T, mod, mod_index, weights, P, state=None, emit_state=False):
    B, R, _ = x.shape
    n_chunks = R // SUB
    n_blk = R // MOE_BLK
    n_seq = R // T
    has_state = state is not None
    seq_mode = {} if R <= MIX_TM else {"pipeline_mode": pl.Buffered(1)}
    in_specs = [
        pl.BlockSpec((1, R, D_MODEL), lambda b: (b, 0, 0), **seq_mode),
        pl.BlockSpec(mod.shape, lambda b: (0, 0, 0)),
    ]
    args = [x, mod]
    if has_state:
        c0, n0, m0 = state
        in_specs += [
            pl.BlockSpec((1, N_UNITS, HEAD_DIM, HEAD_DIM), lambda b: (b, 0, 0, 0)),
            pl.BlockSpec((1, N_UNITS, HEAD_DIM), lambda b: (b, 0, 0)),
            pl.BlockSpec(memory_space=pltpu.SMEM),
        ]
        args += [c0, n0, m0]
    for name in _MIXER_WEIGHTS:
        operand, spec = _const_spec(weights[name])
        in_specs.append(spec)
        args.append(operand)
    out_shape = [
        jax.ShapeDtypeStruct((B, R, D_MODEL), F32),
        jax.ShapeDtypeStruct((B, R, D_MODEL), BF16),
        jax.ShapeDtypeStruct((B, R, LANES), F32),
        jax.ShapeDtypeStruct((B, n_blk * 8, MOE_BLK), F32),
        jax.ShapeDtypeStruct((B, n_blk * 8, LANES), F32),
    ]
    out_specs = [
        pl.BlockSpec((1, R, D_MODEL), lambda b: (b, 0, 0), **seq_mode),
        pl.BlockSpec((1, R, D_MODEL), lambda b: (b, 0, 0), **seq_mode),
        pl.BlockSpec((1, R, LANES), lambda b: (b, 0, 0)),
        pl.BlockSpec((1, n_blk * 8, MOE_BLK), lambda b: (b, 0, 0)),
        pl.BlockSpec((1, n_blk * 8, LANES), lambda b: (b, 0, 0)),
    ]
    if emit_state:
        out_shape += [
            jax.ShapeDtypeStruct((B, n_seq * N_UNITS, HEAD_DIM, HEAD_DIM), F32),
            jax.ShapeDtypeStruct((B, n_seq * N_UNITS, HEAD_DIM), F32),
            jax.ShapeDtypeStruct((B, n_seq * N_UNITS, LANES), F32),
        ]
        out_specs += [
            pl.BlockSpec((1, n_seq * N_UNITS, HEAD_DIM, HEAD_DIM), lambda b: (b, 0, 0, 0)),
            pl.BlockSpec((1, n_seq * N_UNITS, HEAD_DIM), lambda b: (b, 0, 0)),
            pl.BlockSpec((1, n_seq * N_UNITS, LANES), lambda b: (b, 0, 0)),
        ]
    scratch = [
        pltpu.VMEM((R, D_MLSTM), BF16),
        pltpu.VMEM((n_chunks, D_MLSTM, SUB), BF16),
        pltpu.VMEM((R, D_MLSTM), BF16),
        pltpu.VMEM((R, D_MLSTM), F32),
        pltpu.VMEM((n_chunks, 5 * N_UNITS, SUB), F32),
        pltpu.VMEM((R, D_MODEL), F32),
        pltpu.VMEM((R, D_MODEL), F32),
        pltpu.VMEM((R, D_MLSTM), F32),
        pltpu.VMEM((N_UNITS, HEAD_DIM, 2 * HEAD_DIM), F32),
        pltpu.VMEM((MIX_TM // P, P + 2 * CONV_PAD, D_CONV), F32),
    ]
    return pl.pallas_call(
        functools.partial(_mixer_kernel, R, T, P, has_state, emit_state, mod_index),
        grid=(B,),
        in_specs=in_specs,
        out_specs=out_specs,
        out_shape=out_shape,
        scratch_shapes=scratch,
        compiler_params=pltpu.CompilerParams(
            dimension_semantics=("arbitrary",), vmem_limit_bytes=VMEM_LIMIT),
        name="mixer_T%d" % T,
    )(*args)


def _dest_in_block(group, rank, starts):
    dest = rank
    for g in range(N_GROUPS):
        dest = dest + jnp.where(group == float(g), starts[g], 0.0)
    return dest


def _copy_segments(src_refs, dst_refs, src_starts, dst_starts, n_pieces):
    for g in range(N_GROUPS):
        def body(k, carry, g=g):
            s = pl.multiple_of(src_starts[g] + k * ROW_ALIGN, ROW_ALIGN)
            d = pl.multiple_of(dst_starts[g] + k * ROW_ALIGN, ROW_ALIGN)
            for src, dst in zip(src_refs, dst_refs):
                dst[pl.ds(d, ROW_ALIGN), :] = src[pl.ds(s, ROW_ALIGN), :]
            return carry
        lax.fori_loop(0, n_pieces[g], body, 0)


def _plan_segments(n_blocks, n_tiles, count, start_ref, npiece_ref, off_ref, tgroup_ref, tvalid_ref):
    align_shift = ROW_ALIGN.bit_length() - 1
    tile_shift = MOE_TM.bit_length() - 1

    def block_starts(blk, carry):
        row = jnp.int32(0)
        for g in range(N_GROUPS):
            n = lax.shift_right_logical(count(blk, g) + (ROW_ALIGN - 1), align_shift)
            npiece_ref[blk * N_GROUPS + g] = n
            start_ref[blk * N_GROUPS + g] = row
            row = row + n * ROW_ALIGN
        return carry

    lax.fori_loop(0, n_blocks, block_starts, 0)

    base_row = jnp.int32(0)
    base_tile = jnp.int32(0)
    last_group = jnp.int32(0)
    for g in range(N_GROUPS):
        def seg_offsets(blk, row, g=g, base_row=base_row):
            off_ref[blk * N_GROUPS + g] = base_row + row
            return row + npiece_ref[blk * N_GROUPS + g] * ROW_ALIGN

        rows = lax.fori_loop(0, n_blocks, seg_offsets, jnp.int32(0))
        tiles = lax.shift_right_logical(rows + (MOE_TM - 1), tile_shift)

        def mark_tiles(t, carry, g=g, base_tile=base_tile):
            tgroup_ref[base_tile + t] = g
            tvalid_ref[base_tile + t] = 1
            return carry

        lax.fori_loop(0, tiles, mark_tiles, 0)
        last_group = jnp.where(tiles > 0, g, last_group)
        base_row = base_row + tiles * MOE_TM
        base_tile = base_tile + tiles

    def mark_unused(t, carry):
        tgroup_ref[t] = last_group
        tvalid_ref[t] = 0
        return carry

    lax.fori_loop(base_tile, n_tiles, mark_unused, 0)


def _dispatch_kernel(n_ctx_blocks, n_blocks, n_tiles,
                     h2c_ref, h2l_ref, cbc_ref, cbl_ref, rtc_ref, rtl_ref, cntc_ref, cntl_ref,
                     xs_ref, cs_ref, start_ref, npiece_ref, off_ref, tgroup_ref, tvalid_ref,
                     sx_s, sc_s):
    b = pl.program_id(0)
    is_ctx = b < n_ctx_blocks

    def count(blk, g):
        vc = cntc_ref[jnp.minimum(blk, n_ctx_blocks - 1), pl.ds(g, 1), pl.ds(0, 1)]
        vl = cntl_ref[jnp.maximum(blk - n_ctx_blocks, 0), pl.ds(g, 1), pl.ds(0, 1)]
        return jnp.where(blk < n_ctx_blocks, vc, vl)[0, 0].astype(jnp.int32)

    @pl.when(b == 0)
    def _():
        _plan_segments(n_blocks, n_tiles, count, start_ref, npiece_ref, off_ref, tgroup_ref, tvalid_ref)
        xs_ref[...] = jnp.zeros_like(xs_ref)
        cs_ref[...] = jnp.zeros_like(cs_ref)

    h2 = jnp.where(is_ctx, h2c_ref[0], h2l_ref[0])
    cb = jnp.where(is_ctx, cbc_ref[0], cbl_ref[0])
    rt = jnp.where(is_ctx, rtc_ref[0], rtl_ref[0])
    starts = [start_ref[b * N_GROUPS + g] for g in range(N_GROUPS)]
    dest = _dest_in_block(rt[0:1, :], rt[1:2, :], [s.astype(F32) for s in starts])
    row = lax.broadcasted_iota(jnp.int32, (SORT_ROWS, MOE_BLK), 0).astype(F32)
    perm = (row == dest).astype(F32).astype(BF16)
    cb_hi = cb.astype(BF16)
    cb_lo = (cb - cb_hi.astype(F32)).astype(BF16)
    sx_s[...] = _dot(perm, h2).astype(BF16)
    sc_s[...] = _dot(perm, jnp.concatenate([cb_hi, cb_lo], axis=1)).astype(BF16)
    _copy_segments((sx_s, sc_s), (xs_ref, cs_ref), starts,
                   [off_ref[b * N_GROUPS + g] for g in range(N_GROUPS)],
                   [npiece_ref[b * N_GROUPS + g] for g in range(N_GROUPS)])


def _experts_kernel(tgroup_ref, tvalid_ref, xs_ref, cs_ref, wg_ref, wu_ref, wd_ref, ys_ref):
    i = pl.program_id(0)

    @pl.when(tvalid_ref[i] == 1)
    def _():
        x = xs_ref[...]
        comb = cs_ref[:, :LANES].astype(F32) + cs_ref[:, LANES:].astype(F32)
        lane = lax.broadcasted_iota(jnp.int32, comb.shape, 1)
        first = tgroup_ref[i] * EXPERTS_PER_GROUP
        acc = None
        for j in range(EXPERTS_PER_GROUP):
            gj = _dot(x, wg_ref[j].astype(BF16))
            uj = _dot(x, wu_ref[j].astype(BF16))
            cw = jnp.sum(jnp.where(lane == first + j, comb, 0.0), axis=1, keepdims=True)
            out = _dot((gj * _sigmoid(gj) * uj * cw).astype(BF16), wd_ref[j].astype(BF16))
            acc = out if acc is None else acc + out
        ys_ref[...] = acc.astype(BF16)

    @pl.when(tvalid_ref[i] == 0)
    def _():
        ys_ref[...] = jnp.zeros_like(ys_ref)


def _combine_kernel(n_ctx_blocks, blocks_per_lat_seq, start_ref, npiece_ref, off_ref,
                    x1c_ref, x1l_ref, cbc_ref, cbl_ref, ys_ref, mod_ref, gf_ref, yc_ref, yl_ref, loc_s):
    b = pl.program_id(0)
    is_ctx = b < n_ctx_blocks
    starts = [start_ref[b * N_GROUPS + g] for g in range(N_GROUPS)]
    loc_s[...] = jnp.zeros_like(loc_s)
    _copy_segments((ys_ref,), (loc_s,), [off_ref[b * N_GROUPS + g] for g in range(N_GROUPS)], starts,
                   [npiece_ref[b * N_GROUPS + g] for g in range(N_GROUPS)])
    cb = jnp.where(is_ctx, cbc_ref[0], cbl_ref[0])
    dest = _dest_in_block(cb[:, ROUTE_GROUP_LANE:ROUTE_GROUP_LANE + 1],
                          cb[:, ROUTE_RANK_LANE:ROUTE_RANK_LANE + 1],
                          [s.astype(F32) for s in starts])
    col = lax.broadcasted_iota(jnp.int32, (MOE_BLK, SORT_ROWS), 1).astype(F32)
    unperm = (col == dest).astype(F32).astype(BF16)
    moe = _dot(unperm, loc_s[...])
    x1 = jnp.where(is_ctx, x1c_ref[0], x1l_ref[0])
    mrow = jnp.where(is_ctx, 0, 1 + jnp.maximum(b - n_ctx_blocks, 0) // blocks_per_lat_seq)
    x2 = x1 + mod_ref[N_ADA - 1, pl.ds(mrow, 1), :] * moe
    y = x2 * lax.rsqrt(jnp.mean(x2 * x2, axis=-1, keepdims=True) + EPS) * gf_ref[...]

    @pl.when(is_ctx)
    def _():
        yc_ref[0] = y

    @pl.when(jnp.logical_not(is_ctx))
    def _():
        yl_ref[0] = y


def _moe(x1c, x1l, h2c, h2l, cbc, cbl, rtc, rtl, cntc, cntl, mod, blocks_per_lat_seq, wg, wu, wd, gf):
    nc, nl = x1c.shape[0], x1l.shape[0]
    nb = nc + nl
    n_rows_max = nb * MOE_BLK + nb * N_GROUPS * (ROW_ALIGN - 1) + N_GROUPS * (MOE_TM - ROW_ALIGN)
    n_tiles = -(-n_rows_max // MOE_TM)
    ns = n_tiles * MOE_TM

    cmap = lambda b, *_: (jnp.minimum(b, nc - 1), 0, 0)
    lmap = lambda b, *_: (jnp.maximum(b - nc, 0), 0, 0)
    whole = lambda *_: (0, 0)
    once = {"pipeline_mode": pl.Buffered(1)}
    arb = pltpu.CompilerParams(dimension_semantics=("arbitrary",), vmem_limit_bytes=VMEM_LIMIT)
    smem = pl.BlockSpec(memory_space=pltpu.SMEM)
    seg_i32 = jax.ShapeDtypeStruct((nb * N_GROUPS,), jnp.int32)
    tile_i32 = jax.ShapeDtypeStruct((n_tiles,), jnp.int32)

    xs, cs, start, npiece, off, tgroup, tvalid = pl.pallas_call(
        functools.partial(_dispatch_kernel, nc, nb, n_tiles),
        grid_spec=pltpu.PrefetchScalarGridSpec(
            num_scalar_prefetch=0, grid=(nb,),
            in_specs=[
                pl.BlockSpec((1, MOE_BLK, D_MODEL), cmap), pl.BlockSpec((1, MOE_BLK, D_MODEL), lmap),
                pl.BlockSpec((1, MOE_BLK, LANES), cmap), pl.BlockSpec((1, MOE_BLK, LANES), lmap),
                pl.BlockSpec((1, 8, MOE_BLK), cmap), pl.BlockSpec((1, 8, MOE_BLK), lmap),
                pl.BlockSpec(cntc.shape, lambda b: (0, 0, 0)), pl.BlockSpec(cntl.shape, lambda b: (0, 0, 0)),
            ],
            out_specs=[pl.BlockSpec((ns, D_MODEL), whole, **once), pl.BlockSpec((ns, 2 * LANES), whole, **once),
                       smem, smem, smem, smem, smem],
            scratch_shapes=[pltpu.VMEM((SORT_ROWS, D_MODEL), BF16), pltpu.VMEM((SORT_ROWS, 2 * LANES), BF16)],
        ),
        out_shape=[jax.ShapeDtypeStruct((ns, D_MODEL), BF16), jax.ShapeDtypeStruct((ns, 2 * LANES), BF16),
                   seg_i32, seg_i32, seg_i32, tile_i32, tile_i32],
        compiler_params=arb,
        name="moe_dispatch",
    )(h2c, h2l, cbc, cbl, rtc, rtl, cntc, cntl)

    wmap = lambda i, tg, tv: (tg[i], 0, 0)
    ys = pl.pallas_call(
        _experts_kernel,
        grid_spec=pltpu.PrefetchScalarGridSpec(
            num_scalar_prefetch=2, grid=(n_tiles,),
            in_specs=[
                pl.BlockSpec((MOE_TM, D_MODEL), lambda i, *_: (i, 0)),
                pl.BlockSpec((MOE_TM, 2 * LANES), lambda i, *_: (i, 0)),
                pl.BlockSpec((EXPERTS_PER_GROUP, D_MODEL, D_EXPERT), wmap),
                pl.BlockSpec((EXPERTS_PER_GROUP, D_MODEL, D_EXPERT), wmap),
                pl.BlockSpec((EXPERTS_PER_GROUP, D_EXPERT, D_MODEL), wmap),
            ],
            out_specs=pl.BlockSpec((MOE_TM, D_MODEL), lambda i, *_: (i, 0)),
        ),
        out_shape=jax.ShapeDtypeStruct((ns, D_MODEL), BF16),
        compiler_params=arb,
        name="moe_experts",
    )(tgroup, tvalid, xs, cs, wg, wu, wd)

    yc, yl = pl.pallas_call(
        functools.partial(_combine_kernel, nc, blocks_per_lat_seq),
        grid_spec=pltpu.PrefetchScalarGridSpec(
            num_scalar_prefetch=3, grid=(nb,),
            in_specs=[
                pl.BlockSpec((1, MOE_BLK, D_MODEL), cmap), pl.BlockSpec((1, MOE_BLK, D_MODEL), lmap),
                pl.BlockSpec((1, MOE_BLK, LANES), cmap), pl.BlockSpec((1, MOE_BLK, LANES), lmap),
                pl.BlockSpec((ns, D_MODEL), whole, **once),
                pl.BlockSpec(mod.shape, lambda *_: (0, 0, 0)),
                pl.BlockSpec((1, D_MODEL), whole),
            ],
            out_specs=[pl.BlockSpec((1, MOE_BLK, D_MODEL), cmap), pl.BlockSpec((1, MOE_BLK, D_MODEL), lmap)],
            scratch_shapes=[pltpu.VMEM((SORT_ROWS, D_MODEL), BF16)],
        ),
        out_shape=[jax.ShapeDtypeStruct((nc, MOE_BLK, D_MODEL), F32),
                   jax.ShapeDtypeStruct((nl, MOE_BLK, D_MODEL), F32)],
        compiler_params=arb,
        name="moe_combine",
    )(start, npiece, off, x1c, x1l, cbc, cbl, ys, mod, gf)
    return yc, yl


def _prep_weights(norm1_g, w_in, b_in, b_gates, w_dw, b_dw, conv_ln_g, conv_ln_b, w_conv_out,
                  mlstm_hn_g, w_mlstm_out, w_o, norm2_g, w_rg, b_rg, w_re, b_re):
    s_a = 2 * D_CONV
    s_q = s_a + D_MLSTM
    s_k = s_q + D_MLSTM
    s_v = s_k + D_MLSTM
    s_o = s_v + D_MLSTM
    s_g = s_o + 4 * N_HEADS
    row = lambda v: v.reshape(1, -1).astype(F32)
    w_t = w_in.T
    keep = [(0, s_q), (s_k, s_o), (s_g, w_in.shape[1])]
    wrow = _transpose_cast(w_t, [r for a, b in keep for r in range(a, b, WPREP_ROWS)])
    bg = (b_in[s_o:s_g] + b_gates.reshape(-1)).reshape(2, 2, N_HEADS).transpose(1, 0, 2).reshape(-1, 1)
    row_window = lambda start, n: _RowWindow(w_t, start, n)
    n_rt = N_EXPERTS + N_GROUPS
    wrt = jnp.pad(jnp.concatenate([w_re, w_rg], axis=1), ((0, 0), (0, LANES - n_rt)))
    wrt_hi = wrt.astype(BF16)
    wrt2 = jnp.concatenate([wrt_hi, (wrt - wrt_hi.astype(F32)).astype(BF16)], axis=1)
    brtT = jnp.pad(jnp.concatenate([b_re, b_rg]), (0, LANES - n_rt)).reshape(LANES, 1)
    return {
        "g1": row(norm1_g),
        "wrow": wrow, "brow": row(jnp.concatenate([b_in[a:b] for a, b in keep] + [b_in[s_q:s_k]])),
        "wkT": row_window(s_q, D_MLSTM), "wgifT": row_window(s_o, 4 * N_HEADS), "bgifT": bg,
        "wdw": w_dw.astype(F32), "bdw": row(b_dw), "lng": row(conv_ln_g), "lnb": row(conv_ln_b),
        "wco": w_conv_out.astype(BF16), "hng": row(mlstm_hn_g), "wmo": w_mlstm_out.astype(BF16),
        "wo": w_o.astype(BF16), "g2": row(norm2_g), "wrt2": wrt2, "brtT": brtT,
    }


def kernel(x_prompt, x_sample, state_C, state_n, state_m, c, c_ctx, norm1_g, w_ada, b_ada, w_in, b_in, b_gates, w_dw, b_dw, conv_ln_g, conv_ln_b, w_conv_out, mlstm_hn_g, w_mlstm_out, w_o, norm2_g, w_rg, b_rg, w_re, b_re, w_e_gate, w_e_up, w_e_down, norm_final_g):
    B, S, _ = x_prompt.shape
    Bd, Sd, _ = x_sample.shape
    assert w_ada.shape[0] == 1, "single trunk layer"
    assert S == SUB and Sd % SUB == 0

    mod = _ada(c_ctx.reshape(1, -1), c, w_ada[0], b_ada[0].reshape(1, -1))

    wts = _prep_weights(norm1_g[0], w_in[0], b_in[0], b_gates[0], w_dw[0], b_dw[0], conv_ln_g[0],
                        conv_ln_b[0], w_conv_out[0], mlstm_hn_g[0], w_mlstm_out[0], w_o[0],
                        norm2_g[0], w_rg[0], b_rg[0], w_re[0], b_re[0])

    x1p, h2p, cbp, rtp, cntp, c_new, n_new, m_new = _mixer(
        x_prompt.reshape(B * S // MIX_TM, MIX_TM, D_MODEL), S, mod, lambda b: 0, wts, P=S, emit_state=True)

    state = (state_C[:, 0].reshape(Bd, N_UNITS, HEAD_DIM, HEAD_DIM), state_n[:, 0].reshape(Bd, N_UNITS, HEAD_DIM),
             state_m[:, 0].reshape(Bd, N_UNITS))
    x1s, h2s, cbs, rts, cnts = _mixer(x_sample, Sd, mod, lambda b: 1 + b, wts, P=GRID_W, state=state)

    nc, nl = B * S // MOE_BLK, Bd * Sd // MOE_BLK
    blk = lambda a, n: a.reshape(n, MOE_BLK, a.shape[-1])
    yp, ys = _moe(blk(x1p, nc), blk(x1s, nl), blk(h2p, nc), blk(h2s, nl), blk(cbp, nc), blk(cbs, nl),
                  rtp.reshape(nc, 8, MOE_BLK), rts.reshape(nl, 8, MOE_BLK),
                  cntp.reshape(nc, 8, LANES), cnts.reshape(nl, 8, LANES),
                  mod, Sd // MOE_BLK, w_e_gate[0], w_e_up[0], w_e_down[0], norm_final_g.reshape(1, -1))

    return (yp.reshape(B, S, D_MODEL), ys.reshape(Bd, Sd, D_MODEL),
            c_new.reshape(B, 1, 2, N_HEADS, HEAD_DIM, HEAD_DIM),
            n_new.reshape(B, 1, 2, N_HEADS, HEAD_DIM),
            m_new[:, :, 0].reshape(B, 1, 2, N_HEADS))
```

```python
import functools
from typing import NamedTuple

import jax
import jax.numpy as jnp
from jax import lax
from jax.experimental import pallas as pl
from jax.experimental.pallas import tpu as pltpu

D_MODEL = 1024
D_CONV = 512
CONV_K = 31
D_MLSTM = 512
N_HEADS = 4
HEAD_DIM = D_MLSTM // N_HEADS
N_GROUPS = 4
EXPERTS_PER_GROUP = 4
N_EXPERTS = N_GROUPS * EXPERTS_PER_GROUP
D_EXPERT = 256
N_ADA = 6
EPS = 1e-6
GRID_W = 64

LANES = 128
SUB = 256
CONV_PAD = 16
CONV_RB = 64
N_UNITS = 2 * N_HEADS
ROW_ALIGN = 16
COPY_RUN = 4
MOE_TM = 512
MIX_TM = 512
MOE_BLK = MIX_TM
SORT_ROWS = MOE_BLK + N_GROUPS * ROW_ALIGN
ADA_PER_STEP = 2
WPREP_ROWS = 512
ROUTE_GROUP_LANE = N_EXPERTS
ROUTE_RANK_LANE = N_EXPERTS + 1
VMEM_LIMIT = 58 * 1024 * 1024

BF16 = jnp.bfloat16
F32 = jnp.float32
NT_DIMS = (((1,), (1,)), ((), ()))


def _dot(a, b):
    return jnp.dot(a, b, preferred_element_type=F32)


def _dot_nt(a, b, precision=None):
    return lax.dot_general(a, b, NT_DIMS, preferred_element_type=F32, precision=precision)


def _sigmoid(x):
    return 0.5 * jnp.tanh(0.5 * x) + 0.5


def _log_sigmoid(x):
    return jnp.minimum(x, 0.0) - jnp.log1p(jnp.exp(-jnp.abs(x)))


def _split3(x):
    hi = x.astype(BF16).astype(F32)
    r1 = x - hi
    mid = r1.astype(BF16).astype(F32)
    lo = (r1 - mid).astype(BF16).astype(F32)
    return hi, mid, lo


def _ada_kernel(cctx_ref, c_ref, w_ref, b_ref, o_ref):
    n = 1 + c_ref.shape[0]
    c = jnp.concatenate([cctx_ref[...], c_ref[...], jnp.zeros((8 - n, D_MODEL), F32)], axis=0)
    s = (c * _sigmoid(c)).astype(BF16)
    out = _dot(s, w_ref[...].astype(BF16)) + b_ref[...]
    for v in range(ADA_PER_STEP):
        o_ref[v] = out[:, v * D_MODEL:(v + 1) * D_MODEL]


def _ada(c_ctx, c, w_ada, b_ada):
    return pl.pallas_call(
        _ada_kernel,
        grid=(N_ADA // ADA_PER_STEP,),
        in_specs=[
            pl.BlockSpec(c_ctx.shape, lambda j: (0, 0)),
            pl.BlockSpec(c.shape, lambda j: (0, 0)),
            pl.BlockSpec((D_MODEL, ADA_PER_STEP * D_MODEL), lambda j: (0, j)),
            pl.BlockSpec((1, ADA_PER_STEP * D_MODEL), lambda j: (0, j)),
        ],
        out_specs=pl.BlockSpec((ADA_PER_STEP, 8, D_MODEL), lambda j: (j, 0, 0)),
        out_shape=jax.ShapeDtypeStruct((N_ADA, 8, D_MODEL), F32),
        compiler_params=pltpu.CompilerParams(dimension_semantics=("arbitrary",)),
        name="ada",
    )(c_ctx, c, w_ada, b_ada)


def _transpose_cast_kernel(starts_ref, wt_ref, o_ref):
    o_ref[...] = wt_ref[...].astype(BF16).T


def _transpose_cast(w_t, row_starts):
    n, k = len(row_starts), w_t.shape[1]
    return pl.pallas_call(
        _transpose_cast_kernel,
        grid_spec=pltpu.PrefetchScalarGridSpec(
            num_scalar_prefetch=1, grid=(n,),
            in_specs=[pl.BlockSpec((pl.Element(WPREP_ROWS), pl.Element(k)), lambda j, starts: (starts[j] * 8, 0))],
            out_specs=pl.BlockSpec((k, WPREP_ROWS), lambda j, starts: (0, j)),
        ),
        out_shape=jax.ShapeDtypeStruct((k, n * WPREP_ROWS), BF16),
        compiler_params=pltpu.CompilerParams(dimension_semantics=("arbitrary",)),
        name="transpose_cast",
    )(jnp.array([r // 8 for r in row_starts], jnp.int32), w_t)


WROW_OFFSET = {"wq": 2 * D_CONV, "wv": 2 * D_CONV + D_MLSTM, "wog": 2 * D_CONV + 2 * D_MLSTM,
               "wgm": 2 * D_CONV + 3 * D_MLSTM}
BROW_K_OFFSET = 2 * D_CONV + 3 * D_MLSTM + 2 * D_MODEL

_MIXER_WEIGHTS = (
    "g1", "wrow", "brow", "wkT", "wgifT", "bgifT", "wdw", "bdw", "lng", "lnb",
    "wco", "hng", "wmo", "wo", "g2", "wrt2", "brtT",
)


def _zero_after(x):
    bits = lax.bitcast_convert_type(x, jnp.uint32)
    bits = lax.shift_right_logical(lax.shift_right_logical(bits, jnp.uint32(16)), jnp.uint32(16))
    return lax.bitcast_convert_type(bits, F32)[0:1, :]


def _conv_block(upad_s, seg, base, cs, wdw_ref, bdw_ref, after=None):
    sub = 8
    first = CONV_PAD - CONV_K // 2
    acc = jnp.broadcast_to(bdw_ref[0:1, cs], (CONV_RB, LANES))
    for r in range(sub):
        z = None
        for a in range((CONV_K + first + sub - 1) // sub):
            j = sub * a + r - first
            if 0 <= j < CONV_K:
                lo = base + sub * a
                tap = wdw_ref[j:j + 1, cs] if after is None else wdw_ref[j:j + 1, cs] + after
                term = tap * upad_s[seg, lo:lo + CONV_RB + sub, cs]
                z = term if z is None else z + term
        acc = acc + z[r:r + CONV_RB, :]
    return acc


def _mixer_kernel(R, T, P, has_state, emit_state, mod_index, *refs):
    L = SUB
    n_mt = R // MIX_TM
    cpm = MIX_TM // L
    n_seq = R // T
    cps = T // L
    nseg = MIX_TM // P
    assert not has_state or n_seq == 1
    it = iter(refs)
    x_ref = next(it)
    mod_ref = next(it)
    if has_state:
        c0_ref = next(it)
        n0_ref = next(it)
        m0_ref = next(it)
    w = {name: next(it) for name in _MIXER_WEIGHTS}
    x1_ref = next(it)
    h2_ref = next(it)
    comb_ref = next(it)
    route_ref = next(it)
    cnt_ref = next(it)
    if emit_state:
        cout_ref = next(it)
        nout_ref = next(it)
        mout_ref = next(it)
    (q_s, kT_s, v_s, so_s, scan_s, ma_s, sgb_s, hm_s, cst_s, upad_s) = [next(it) for _ in range(10)]

    cond_row = mod_index(pl.program_id(0))

    def mod_row(i):
        return mod_ref[i, pl.ds(cond_row, 1), :]

    zpad = jnp.zeros((CONV_PAD, D_CONV), F32)
    for seg in range(nseg):
        upad_s[seg, 0:CONV_PAD, :] = zpad
        upad_s[seg, CONV_PAD + P:CONV_PAD + P + CONV_PAD, :] = zpad

    t_idx = lax.broadcasted_iota(jnp.int32, (L, L), 0)
    s_idx = lax.broadcasted_iota(jnp.int32, (L, L), 1)
    lower = s_idx <= t_idx
    upper = s_idx >= t_idx
    triu_b = upper.astype(F32).astype(BF16)
    lane_u = lax.broadcasted_iota(jnp.int32, (N_UNITS, L), 1)
    is_bwd = lax.broadcasted_iota(jnp.int32, (N_UNITS, L), 0) >= N_HEADS

    def gate_scan(g):
        gi, lf = g[:N_UNITS], _log_sigmoid(g[N_UNITS:])
        pr = _dot(jnp.concatenate(_split3(lf), axis=0).astype(BF16), triu_b)
        pre = pr[0:N_UNITS] + pr[N_UNITS:2 * N_UNITS] + pr[2 * N_UNITS:]
        tot = pre[:, L - 1:L]
        bsum = jnp.where(is_bwd, tot - pre + lf, pre)
        a = gi - bsum
        pm, sm, k = a, a, 1
        while k < L:
            pm = jnp.where(lane_u >= k, jnp.maximum(pm, pltpu.roll(pm, k, axis=1)), pm)
            sm = jnp.where(lane_u < L - k, jnp.maximum(sm, pltpu.roll(sm, L - k, axis=1)), sm)
            k *= 2
        wide = lambda v: jnp.broadcast_to(v, (N_UNITS, L))
        return jnp.concatenate([a, jnp.where(is_bwd, sm, pm), bsum, wide(tot),
                                wide(jnp.max(a, axis=1, keepdims=True))], axis=0)

    def phase1(i, carry):
        r0 = pl.multiple_of(i * MIX_TM, MIX_TM)
        rows = pl.ds(r0, MIX_TM)
        x = x_ref[0, rows, :]
        xn = x * lax.rsqrt(jnp.mean(x * x, axis=-1, keepdims=True) + EPS) * w["g1"][...]
        hb = (xn * (1.0 + mod_row(1)) + mod_row(0)).astype(BF16)

        gates = _dot_nt(w["wgifT"][...].astype(BF16), hb)
        gates = jnp.concatenate([gates[d * 2 * N_HEADS + g * N_HEADS:d * 2 * N_HEADS + (g + 1) * N_HEADS]
                                 for g in range(2) for d in range(2)], axis=0) + w["bgifT"][...]
        for j in range(cpm):
            scan_s[i * cpm + j] = gate_scan(gates[:, j * L:(j + 1) * L])
        ag = _dot(hb, w["wrow"][:, :2 * D_CONV]) + w["brow"][:, :2 * D_CONV]
        u = ag[:, :D_CONV] * _sigmoid(ag[:, D_CONV:])
        for seg in range(nseg):
            upad_s[seg, CONV_PAD:CONV_PAD + P, :] = u[seg * P:(seg + 1) * P, :]

        def proj(name, c0, gate, width=2 * LANES):
            w0 = WROW_OFFSET[name] + c0
            b = w["brow"][:, w0:w0 + width]
            if gate is not None:
                b = b + jnp.concatenate([gate] * (width // LANES), axis=1)
            return _dot(hb, w["wrow"][:, w0:w0 + width]) + b

        last = lambda z: z[-8:, -LANES:]
        bk_row = w["brow"][:, BROW_K_OFFSET:BROW_K_OFFSET + D_MLSTM]
        bk_col = jnp.concatenate([bk_row, jnp.zeros((LANES - 1, D_MLSTM), F32)], axis=0).T[:, 0:1]

        def gm_a(c0, gate):
            z = proj("wgm", c0, gate)
            ma_s[rows, c0:c0 + 2 * LANES] = _sigmoid(z)
            return last(z)

        def gm_b(c0, gate):
            z = proj("wgm", D_MODEL + c0, gate)
            sgb_s[rows, c0:c0 + 2 * LANES] = _sigmoid(z)
            return last(z)

        def q_part(c0, gate):
            z = proj("wq", c0, gate)
            q_s[rows, c0:c0 + 2 * LANES] = (z * (HEAD_DIM ** -0.5)).astype(BF16)
            return last(z)

        def v_part(c0, gate):
            z = proj("wv", c0, gate)
            v_s[rows, c0:c0 + 2 * LANES] = z.astype(BF16)
            return last(z)

        def o_part(c0, gate):
            z = proj("wog", c0, gate)
            so_s[rows, c0:c0 + 2 * LANES] = _sigmoid(z)
            return last(z)

        def k_part(c0, gate):
            rs = slice(c0, c0 + 2 * LANES)
            b = bk_col[rs, :] if gate is None else bk_col[rs, :] + gate[:, 0:1]
            z = _dot_nt(w["wkT"][rs, :].astype(BF16), hb) + b
            kt = z.astype(BF16)
            for j in range(cpm):
                kT_s[i * cpm + j, rs, :] = kt[:, j * L:(j + 1) * L]
            return last(z)

        jobs = ([functools.partial(gm_a, c0) for c0 in range(0, D_MODEL, 2 * LANES)]
                + [functools.partial(gm_b, c0) for c0 in range(0, D_MODEL, 2 * LANES)]
                + [functools.partial(f, c0) for f in (q_part, k_part, v_part, o_part)
                   for c0 in range(0, D_MLSTM, 2 * LANES)])
        n_jobs = len(jobs)
        conv = {}
        after = None
        n_pieces = (D_CONV // LANES) * nseg * (P // CONV_RB)
        for cb in range(D_CONV // LANES):
            cs = slice(cb * LANES, (cb + 1) * LANES)
            for seg in range(nseg):
                for rb in range(P // CONV_RB):
                    blk = _conv_block(upad_s, seg, rb * CONV_RB, cs, w["wdw"], w["bdw"], after)
                    conv[(cb, seg, rb)] = blk
                    if jobs and len(conv) * n_jobs >= (n_jobs - len(jobs) + 1) * n_pieces:
                        after = _zero_after(jobs.pop(0)(_zero_after(blk[-8:, :])))
        for job in jobs:
            job(None)
        cu = jnp.concatenate(
            [jnp.concatenate([conv[(cb, seg, rb)] for seg in range(nseg) for rb in range(P // CONV_RB)], axis=0)
             for cb in range(D_CONV // LANES)], axis=1)
        mu = jnp.mean(cu, axis=-1, keepdims=True)
        cc = cu - mu
        cn = cc * lax.rsqrt(jnp.mean(cc * cc, axis=-1, keepdims=True) + EPS) * w["lng"][...] + w["lnb"][...]
        ca = (cn * _sigmoid(cn)).astype(BF16)
        ma_s[rows, :] = ma_s[rows, :] * _dot(ca, w["wco"][...])
        return carry

    if n_mt == 1:
        phase1(0, 0)
    else:
        lax.fori_loop(0, n_mt, phase1, 0)

    ones_col = (lax.broadcasted_iota(jnp.int32, (L, HEAD_DIM), 1) == 0).astype(F32).astype(BF16)
    pad_rows = jnp.zeros((LANES - 3 * N_UNITS, L), F32)

    def gate_prep(c, m_vec):
        sc = scan_s[c]
        a, run_max, bsum = sc[0:N_UNITS], sc[N_UNITS:2 * N_UNITS], sc[2 * N_UNITS:3 * N_UNITS]
        tot, a_max = sc[3 * N_UNITS:4 * N_UNITS, 0:1], sc[4 * N_UNITS:5 * N_UNITS, 0:1]
        big_m = jnp.maximum(m_vec, run_max)
        m_end = jnp.maximum(m_vec, a_max)
        cols = jnp.concatenate(
            [big_m, jnp.exp(m_vec - big_m), jnp.exp(-bsum - big_m), pad_rows], axis=0).T
        return a, cols, jnp.exp(a - m_end), jnp.exp(m_vec - m_end), tot + m_end

    def unit_group(dirs, c, prep, first_chunk, want_state):
        a, cols, wk, decay, _ = prep
        rows = slice(c * L, (c + 1) * L)
        heads = range(N_HEADS)
        units = [(d, hd) for d in dirs for hd in heads]
        hs = [slice(hd * HEAD_DIM, (hd + 1) * HEAD_DIM) for hd in heads]
        idx = {u: u[0] * N_HEADS + u[1] for u in units}
        col = lambda k, u: cols[:, k * N_UNITS + idx[u]:k * N_UNITS + idx[u] + 1]
        row = lambda arr, u: arr[idx[u]:idx[u] + 1, :]
        chained = has_state or not first_chunk
        qc = [q_s[rows, hs[hd]] for hd in heads]
        kTc = [kT_s[c, hs[hd], :] for hd in heads]
        vaug = [jnp.concatenate([v_s[rows, hs[hd]], ones_col], axis=1) for hd in heads]
        qk = [_dot(qc[hd], kTc[hd]) for hd in heads]
        s_mat = {u: (qk[u[1]] * jnp.where(lower if u[0] == 0 else upper, jnp.exp(row(a, u) - col(0, u)), 0.0)
                     ).astype(BF16) for u in units}
        nd = {u: _dot(s_mat[u], vaug[u[1]]) for u in units}
        if chained:
            nd = {u: nd[u] + col(1, u) * _dot(qc[u[1]], cst_s[idx[u]].astype(BF16)) for u in units}
        h = {u: nd[u][:, :HEAD_DIM] * (1.0 / jnp.maximum(jnp.abs(nd[u][:, HEAD_DIM:HEAD_DIM + 1]), col(2, u)))
             for u in units}
        for hd in heads:
            total = h[(dirs[0], hd)]
            for d in dirs[1:]:
                total = total + h[(d, hd)]
            if dirs[0] == 0:
                hm_s[rows, hs[hd]] = total
            else:
                hm_s[rows, hs[hd]] = hm_s[rows, hs[hd]] + total
        if want_state:
            kw = {u: (kTc[u[1]].astype(F32) * row(wk, u)).astype(BF16) for u in units}
            upd = {u: _dot(kw[u], vaug[u[1]]) for u in units}
            for u in units:
                cst_s[idx[u]] = (upd[u] + row(decay, u) * cst_s[idx[u]]) if chained else upd[u]

    dir_rows = lax.broadcasted_iota(jnp.int32, (N_UNITS, 1), 0) >= N_HEADS
    for seq in range(n_seq):
        if has_state:
            n_cols = jnp.concatenate([n0_ref[0], jnp.zeros((LANES - N_UNITS, HEAD_DIM), F32)], axis=0).T
            first_lane = lax.broadcasted_iota(jnp.int32, (HEAD_DIM, HEAD_DIM), 1) == 0
            for idx in range(N_UNITS):
                cst_s[idx, :, :HEAD_DIM] = c0_ref[0, idx]
                cst_s[idx, :, HEAD_DIM:] = jnp.where(first_lane, n_cols[:, idx:idx + 1], 0.0)
            unit_row = lax.broadcasted_iota(jnp.int32, (N_UNITS, 1), 0)
            m_vec = jnp.zeros((N_UNITS, 1), F32)
            for idx in range(N_UNITS):
                m_vec = jnp.where(unit_row == idx, m0_ref[pl.program_id(0), idx], m_vec)
        else:
            m_vec = jnp.zeros((N_UNITS, 1), F32)
        if cps == 1:
            prep = gate_prep(seq, m_vec)
            unit_group([0, 1], seq, prep, True, emit_state)
            m_vec = prep[4]
        else:
            for d in range(2):
                order = list(range(cps)) if d == 0 else list(range(cps - 1, -1, -1))
                for pos, c in enumerate(order):
                    prep = gate_prep(seq * cps + c, m_vec)
                    unit_group([d], seq * cps + c, prep, pos == 0, emit_state or pos < cps - 1)
                    m_vec = jnp.where(dir_rows == (d == 1), prep[4], m_vec)
        if emit_state:
            for idx in range(N_UNITS):
                caug = cst_s[idx]
                cout_ref[0, seq * N_UNITS + idx] = caug[:, :HEAD_DIM]
                nout_ref[0, seq * N_UNITS + idx:seq * N_UNITS + idx + 1, :] = caug[:, HEAD_DIM:].T[0:1, :]
            mout_ref[0, seq * N_UNITS:(seq + 1) * N_UNITS, :] = jnp.broadcast_to(m_vec, (N_UNITS, LANES))

    e_iota = lax.broadcasted_iota(jnp.int32, (LANES, MIX_TM), 0)
    g_of_e = lax.shift_right_logical(e_iota, 2)
    j_of_e = lax.bitwise_and(e_iota, EXPERTS_PER_GROUP - 1)
    r8 = lax.broadcasted_iota(jnp.int32, (8, MIX_TM), 0)
    before_b = (lax.broadcasted_iota(jnp.int32, (MOE_BLK, MOE_BLK), 0)
                < lax.broadcasted_iota(jnp.int32, (MOE_BLK, MOE_BLK), 1)).astype(F32).astype(BF16)

    def phase3(i, carry):
        r0 = pl.multiple_of(i * MIX_TM, MIX_TM)
        rows = pl.ds(r0, MIX_TM)
        hm = hm_s[rows, :]
        heads = []
        for hd in range(N_HEADS):
            hh = hm[:, hd * HEAD_DIM:(hd + 1) * HEAD_DIM]
            heads.append(hh * lax.rsqrt(jnp.mean(hh * hh, axis=-1, keepdims=True) + EPS))
        hn = jnp.concatenate(heads, axis=1) * w["hng"][...]
        hb2 = (so_s[rows, :] * hn).astype(BF16)
        br_b = _dot(hb2, w["wmo"][...])
        mixed = (ma_s[rows, :] + sgb_s[rows, :] * br_b).astype(BF16)
        x1 = x_ref[0, rows, :] + mod_row(2) * _dot(mixed, w["wo"][...])
        x1_ref[0, rows, :] = x1
        xn = x1 * lax.rsqrt(jnp.mean(x1 * x1, axis=-1, keepdims=True) + EPS) * w["g2"][...]
        h2 = xn * (1.0 + mod_row(4)) + mod_row(3)
        h2_ref[0, rows, :] = h2.astype(BF16)

        h2_hi = h2.astype(BF16)
        h2_lo = (h2 - h2_hi.astype(F32)).astype(BF16)
        lg = _dot(h2_hi, w["wrt2"][...])
        lg = lg[:, :LANES] + lg[:, LANES:] + _dot(h2_lo, w["wrt2"][:, :LANES])
        lt = lg.T + w["brtT"][...]
        gl = [lt[N_EXPERTS + g:N_EXPERTS + g + 1, :] for g in range(N_GROUPS)]
        best, gsel = gl[0], jnp.zeros((1, MIX_TM), jnp.int32)
        for g in range(1, N_GROUPS):
            better = gl[g] > best
            gsel = jnp.where(better, g, gsel)
            best = jnp.where(better, gl[g], best)
        gp_sel = 1.0 / sum(jnp.exp(v - best) for v in gl)
        el = []
        for j in range(EXPERTS_PER_GROUP):
            v = lt[j:j + 1, :]
            for g in range(1, N_GROUPS):
                r = g * EXPERTS_PER_GROUP + j
                v = jnp.where(gsel == g, lt[r:r + 1, :], v)
            el.append(v)
        l1, e1 = el[0], jnp.zeros((1, MIX_TM), jnp.int32)
        for j in range(1, EXPERTS_PER_GROUP):
            better = el[j] > l1
            e1 = jnp.where(better, j, e1)
            l1 = jnp.where(better, el[j], l1)
        l2 = jnp.full((1, MIX_TM), -jnp.inf, F32)
        e2 = jnp.zeros((1, MIX_TM), jnp.int32)
        for j in range(EXPERTS_PER_GROUP):
            better = jnp.logical_and(e1 != j, el[j] > l2)
            e2 = jnp.where(better, j, e2)
            l2 = jnp.where(better, el[j], l2)
        r2 = jnp.exp(l2 - l1)
        wt1 = gp_sel / (1.0 + r2)
        wt2 = gp_sel * r2 / (1.0 + r2)
        in_group = g_of_e == gsel
        comb_t = (jnp.where(jnp.logical_and(in_group, j_of_e == e1), wt1, 0.0)
                  + jnp.where(jnp.logical_and(in_group, j_of_e == e2), wt2, 0.0))

        onehot = (r8 == gsel).astype(F32)
        gsel_f = gsel.astype(F32)
        rank = jnp.sum(onehot * _dot(onehot.astype(BF16), before_b), axis=0, keepdims=True)
        r8rows = pl.ds(pl.multiple_of(i * 8, 8), 8)
        route_ref[0, r8rows, :] = jnp.where(r8 == 0, gsel_f, jnp.where(r8 == 1, rank, 0.0))
        cnt_ref[0, r8rows, :] = jnp.broadcast_to(jnp.sum(onehot, axis=1, keepdims=True), (8, LANES))
        comb_t = jnp.where(e_iota == ROUTE_GROUP_LANE, gsel_f,
                           jnp.where(e_iota == ROUTE_RANK_LANE, rank, comb_t))
        comb_ref[0, rows, :] = comb_t.T
        return carry

    if n_mt == 1:
        phase3(0, 0)
    else:
        lax.fori_loop(0, n_mt, phase3, 0)


class _RowWindow(NamedTuple):
    array: jax.Array
    start: int
    n: int


def _const_spec(a):
    if isinstance(a, _RowWindow):
        assert a.start % a.n == 0
        return a.array, pl.BlockSpec((a.n, a.array.shape[1]), lambda b: (a.start // a.n, 0),
                                     pipeline_mode=pl.Buffered(1))
    nd = a.ndim
    return a, pl.BlockSpec(a.shape, lambda b, _nd=nd: (0,) * _nd, pipeline_mode=pl.Buffered(1))


def _mixer(x, T, mod, mod_index, weights, P, state=None, emit_state=False):
    B, R, _ = x.shape
    n_chunks = R // SUB
    n_blk = R // MOE_BLK
    n_seq = R // T
    has_state = state is not None
    seq_mode = {} if R <= MIX_TM else {"pipeline_mode": pl.Buffered(1)}
    in_specs = [
        pl.BlockSpec((1, R, D_MODEL), lambda b: (b, 0, 0), **seq_mode),
        pl.BlockSpec(mod.shape, lambda b: (0, 0, 0)),
    ]
    args = [x, mod]
    if has_state:
        c0, n0, m0 = state
        in_specs += [
            pl.BlockSpec((1, N_UNITS, HEAD_DIM, HEAD_DIM), lambda b: (b, 0, 0, 0)),
            pl.BlockSpec((1, N_UNITS, HEAD_DIM), lambda b: (b, 0, 0)),
            pl.BlockSpec(memory_space=pltpu.SMEM),
        ]
        args += [c0, n0, m0]
    for name in _MIXER_WEIGHTS:
        operand, spec = _const_spec(weights[name])
        in_specs.append(spec)
        args.append(operand)
    out_shape = [
        jax.ShapeDtypeStruct((B, R, D_MODEL), F32),
        jax.ShapeDtypeStruct((B, R, D_MODEL), BF16),
        jax.ShapeDtypeStruct((B, R, LANES), F32),
        jax.ShapeDtypeStruct((B, n_blk * 8, MOE_BLK), F32),
        jax.ShapeDtypeStruct((B, n_blk * 8, LANES), F32),
    ]
    out_specs = [
        pl.BlockSpec((1, R, D_MODEL), lambda b: (b, 0, 0), **seq_mode),
        pl.BlockSpec((1, R, D_MODEL), lambda b: (b, 0, 0), **seq_mode),
        pl.BlockSpec((1, R, LANES), lambda b: (b, 0, 0)),
        pl.BlockSpec((1, n_blk * 8, MOE_BLK), lambda b: (b, 0, 0)),
        pl.BlockSpec((1, n_blk * 8, LANES), lambda b: (b, 0, 0)),
    ]
    if emit_state:
        out_shape += [
            jax.ShapeDtypeStruct((B, n_seq * N_UNITS, HEAD_DIM, HEAD_DIM), F32),
            jax.ShapeDtypeStruct((B, n_seq * N_UNITS, HEAD_DIM), F32),
            jax.ShapeDtypeStruct((B, n_seq * N_UNITS, LANES), F32),
        ]
        out_specs += [
            pl.BlockSpec((1, n_seq * N_UNITS, HEAD_DIM, HEAD_DIM), lambda b: (b, 0, 0, 0)),
            pl.BlockSpec((1, n_seq * N_UNITS, HEAD_DIM), lambda b: (b, 0, 0)),
            pl.BlockSpec((1, n_seq * N_UNITS, LANES), lambda b: (b, 0, 0)),
        ]
    scratch = [
        pltpu.VMEM((R, D_MLSTM), BF16),
        pltpu.VMEM((n_chunks, D_MLSTM, SUB), BF16),
        pltpu.VMEM((R, D_MLSTM), BF16),
        pltpu.VMEM((R, D_MLSTM), F32),
        pltpu.VMEM((n_chunks, 5 * N_UNITS, SUB), F32),
        pltpu.VMEM((R, D_MODEL), F32),
        pltpu.VMEM((R, D_MODEL), F32),
        pltpu.VMEM((R, D_MLSTM), F32),
        pltpu.VMEM((N_UNITS, HEAD_DIM, 2 * HEAD_DIM), F32),
        pltpu.VMEM((MIX_TM // P, P + 2 * CONV_PAD, D_CONV), F32),
    ]
    return pl.pallas_call(
        functools.partial(_mixer_kernel, R, T, P, has_state, emit_state, mod_index),
        grid=(B,),
        in_specs=in_specs,
        out_specs=out_specs,
        out_shape=out_shape,
        scratch_shapes=scratch,
        compiler_params=pltpu.CompilerParams(
            dimension_semantics=("arbitrary",), vmem_limit_bytes=VMEM_LIMIT),
        name="mixer_T%d" % T,
    )(*args)


def _dest_in_block(group, rank, starts):
    dest = rank
    for g in range(N_GROUPS):
        dest = dest + jnp.where(group == float(g), starts[g], 0.0)
    return dest


def _copy_segments(src_refs, dst_refs, src_starts, dst_starts, n_pieces):
    def copy(g, first_piece, n_rows):
        s = pl.multiple_of(src_starts[g] + first_piece * ROW_ALIGN, ROW_ALIGN)
        d = pl.multiple_of(dst_starts[g] + first_piece * ROW_ALIGN, ROW_ALIGN)
        for src, dst in zip(src_refs, dst_refs):
            dst[pl.ds(d, n_rows), :] = src[pl.ds(s, n_rows), :]

    for g in range(N_GROUPS):
        n_runs = lax.shift_right_logical(n_pieces[g], COPY_RUN.bit_length() - 1)

        def run(k, carry, g=g):
            copy(g, k * COPY_RUN, COPY_RUN * ROW_ALIGN)
            return carry

        def single(k, carry, g=g):
            copy(g, k, ROW_ALIGN)
            return carry

        lax.fori_loop(0, n_runs, run, 0)
        lax.fori_loop(n_runs * COPY_RUN, n_pieces[g], single, 0)


def _plan_segments(n_blocks, n_tiles, count, start_ref, npiece_ref, off_ref, tgroup_ref, tvalid_ref):
    align_shift = ROW_ALIGN.bit_length() - 1
    tile_shift = MOE_TM.bit_length() - 1

    def block_starts(blk, carry):
        row = jnp.int32(0)
        for g in range(N_GROUPS):
            n = lax.shift_right_logical(count(blk, g) + (ROW_ALIGN - 1), align_shift)
            npiece_ref[blk * N_GROUPS + g] = n
            start_ref[blk * N_GROUPS + g] = row
            row = row + n * ROW_ALIGN
        return carry

    lax.fori_loop(0, n_blocks, block_starts, 0)

    base_row = jnp.int32(0)
    base_tile = jnp.int32(0)
    last_group = jnp.int32(0)
    for g in range(N_GROUPS):
        def seg_offsets(blk, row, g=g, base_row=base_row):
            off_ref[blk * N_GROUPS + g] = base_row + row
            return row + npiece_ref[blk * N_GROUPS + g] * ROW_ALIGN

        rows = lax.fori_loop(0, n_blocks, seg_offsets, jnp.int32(0))
        tiles = lax.shift_right_logical(rows + (MOE_TM - 1), tile_shift)

        def mark_tiles(t, carry, g=g, base_tile=base_tile):
            tgroup_ref[base_tile + t] = g
            tvalid_ref[base_tile + t] = 1
            return carry

        lax.fori_loop(0, tiles, mark_tiles, 0)
        last_group = jnp.where(tiles > 0, g, last_group)
        base_row = base_row + tiles * MOE_TM
        base_tile = base_tile + tiles

    def mark_unused(t, carry):
        tgroup_ref[t] = last_group
        tvalid_ref[t] = 0
        return carry

    lax.fori_loop(base_tile, n_tiles, mark_unused, 0)


def _dispatch_kernel(n_ctx_blocks, n_blocks, n_tiles,
                     h2c_ref, h2l_ref, cbc_ref, cbl_ref, rtc_ref, rtl_ref, cntc_ref, cntl_ref,
                     xs_ref, cs_ref, start_ref, npiece_ref, off_ref, tgroup_ref, tvalid_ref,
                     sx_s, sc_s):
    b = pl.program_id(0)
    is_ctx = b < n_ctx_blocks

    def count(blk, g):
        vc = cntc_ref[jnp.minimum(blk, n_ctx_blocks - 1), pl.ds(g, 1), pl.ds(0, 1)]
        vl = cntl_ref[jnp.maximum(blk - n_ctx_blocks, 0), pl.ds(g, 1), pl.ds(0, 1)]
        return jnp.where(blk < n_ctx_blocks, vc, vl)[0, 0].astype(jnp.int32)

    @pl.when(b == 0)
    def _():
        _plan_segments(n_blocks, n_tiles, count, start_ref, npiece_ref, off_ref, tgroup_ref, tvalid_ref)
        xs_ref[...] = jnp.zeros_like(xs_ref)
        cs_ref[...] = jnp.zeros_like(cs_ref)

    h2 = jnp.where(is_ctx, h2c_ref[0], h2l_ref[0])
    cb = jnp.where(is_ctx, cbc_ref[0], cbl_ref[0])
    rt = jnp.where(is_ctx, rtc_ref[0], rtl_ref[0])
    starts = [start_ref[b * N_GROUPS + g] for g in range(N_GROUPS)]
    dest = _dest_in_block(rt[0:1, :], rt[1:2, :], [s.astype(F32) for s in starts])
    row = lax.broadcasted_iota(jnp.int32, (SORT_ROWS, MOE_BLK), 0).astype(F32)
    perm = (row == dest).astype(F32).astype(BF16)
    cb_hi = cb.astype(BF16)
    cb_lo = (cb - cb_hi.astype(F32)).astype(BF16)
    sx_s[...] = _dot(perm, h2).astype(BF16)
    sc_s[...] = _dot(perm, jnp.concatenate([cb_hi, cb_lo], axis=1)).astype(BF16)
    _copy_segments((sx_s, sc_s), (xs_ref, cs_ref), starts,
                   [off_ref[b * N_GROUPS + g] for g in range(N_GROUPS)],
                   [npiece_ref[b * N_GROUPS + g] for g in range(N_GROUPS)])


def _experts_kernel(tgroup_ref, tvalid_ref, xs_ref, cs_ref, wg_ref, wu_ref, wd_ref, ys_ref):
    i = pl.program_id(0)

    @pl.when(tvalid_ref[i] == 1)
    def _():
        x = xs_ref[...]
        comb = cs_ref[:, :LANES].astype(F32) + cs_ref[:, LANES:].astype(F32)
        lane = lax.broadcasted_iota(jnp.int32, comb.shape, 1)
        first = tgroup_ref[i] * EXPERTS_PER_GROUP
        acc = None
        for j in range(EXPERTS_PER_GROUP):
            gj = _dot(x, wg_ref[j].astype(BF16))
            uj = _dot(x, wu_ref[j].astype(BF16))
            cw = jnp.sum(jnp.where(lane == first + j, comb, 0.0), axis=1, keepdims=True)
            out = _dot((gj * _sigmoid(gj) * uj * cw).astype(BF16), wd_ref[j].astype(BF16))
            acc = out if acc is None else acc + out
        ys_ref[...] = acc.astype(BF16)

    @pl.when(tvalid_ref[i] == 0)
    def _():
        ys_ref[...] = jnp.zeros_like(ys_ref)


def _combine_kernel(n_ctx_blocks, blocks_per_lat_seq, start_ref, npiece_ref, off_ref,
                    x1c_ref, x1l_ref, cbc_ref, cbl_ref, ys_ref, mod_ref, gf_ref, yc_ref, yl_ref, loc_s):
    b = pl.program_id(0)
    is_ctx = b < n_ctx_blocks
    starts = [start_ref[b * N_GROUPS + g] for g in range(N_GROUPS)]
    loc_s[...] = jnp.zeros_like(loc_s)
    _copy_segments((ys_ref,), (loc_s,), [off_ref[b * N_GROUPS + g] for g in range(N_GROUPS)], starts,
                   [npiece_ref[b * N_GROUPS + g] for g in range(N_GROUPS)])
    cb = jnp.where(is_ctx, cbc_ref[0], cbl_ref[0])
    dest = _dest_in_block(cb[:, ROUTE_GROUP_LANE:ROUTE_GROUP_LANE + 1],
                          cb[:, ROUTE_RANK_LANE:ROUTE_RANK_LANE + 1],
                          [s.astype(F32) for s in starts])
    col = lax.broadcasted_iota(jnp.int32, (MOE_BLK, SORT_ROWS), 1).astype(F32)
    unperm = (col == dest).astype(F32).astype(BF16)
    moe = _dot(unperm, loc_s[...])
    x1 = jnp.where(is_ctx, x1c_ref[0], x1l_ref[0])
    mrow = jnp.where(is_ctx, 0, 1 + jnp.maximum(b - n_ctx_blocks, 0) // blocks_per_lat_seq)
    x2 = x1 + mod_ref[N_ADA - 1, pl.ds(mrow, 1), :] * moe
    y = x2 * lax.rsqrt(jnp.mean(x2 * x2, axis=-1, keepdims=True) + EPS) * gf_ref[...]

    @pl.when(is_ctx)
    def _():
        yc_ref[0] = y

    @pl.when(jnp.logical_not(is_ctx))
    def _():
        yl_ref[0] = y


def _moe(x1c, x1l, h2c, h2l, cbc, cbl, rtc, rtl, cntc, cntl, mod, blocks_per_lat_seq, wg, wu, wd, gf):
    nc, nl = x1c.shape[0], x1l.shape[0]
    nb = nc + nl
    n_rows_max = nb * MOE_BLK + nb * N_GROUPS * (ROW_ALIGN - 1) + N_GROUPS * (MOE_TM - ROW_ALIGN)
    n_tiles = -(-n_rows_max // MOE_TM)
    ns = n_tiles * MOE_TM

    cmap = lambda b, *_: (jnp.minimum(b, nc - 1), 0, 0)
    lmap = lambda b, *_: (jnp.maximum(b - nc, 0), 0, 0)
    whole = lambda *_: (0, 0)
    once = {"pipeline_mode": pl.Buffered(1)}
    arb = pltpu.CompilerParams(dimension_semantics=("arbitrary",), vmem_limit_bytes=VMEM_LIMIT)
    smem = pl.BlockSpec(memory_space=pltpu.SMEM)
    seg_i32 = jax.ShapeDtypeStruct((nb * N_GROUPS,), jnp.int32)
    tile_i32 = jax.ShapeDtypeStruct((n_tiles,), jnp.int32)

    xs, cs, start, npiece, off, tgroup, tvalid = pl.pallas_call(
        functools.partial(_dispatch_kernel, nc, nb, n_tiles),
        grid_spec=pltpu.PrefetchScalarGridSpec(
            num_scalar_prefetch=0, grid=(nb,),
            in_specs=[
                pl.BlockSpec((1, MOE_BLK, D_MODEL), cmap), pl.BlockSpec((1, MOE_BLK, D_MODEL), lmap),
                pl.BlockSpec((1, MOE_BLK, LANES), cmap), pl.BlockSpec((1, MOE_BLK, LANES), lmap),
                pl.BlockSpec((1, 8, MOE_BLK), cmap), pl.BlockSpec((1, 8, MOE_BLK), lmap),
                pl.BlockSpec(cntc.shape, lambda b: (0, 0, 0)), pl.BlockSpec(cntl.shape, lambda b: (0, 0, 0)),
            ],
            out_specs=[pl.BlockSpec((ns, D_MODEL), whole, **once), pl.BlockSpec((ns, 2 * LANES), whole, **once),
                       smem, smem, smem, smem, smem],
            scratch_shapes=[pltpu.VMEM((SORT_ROWS, D_MODEL), BF16), pltpu.VMEM((SORT_ROWS, 2 * LANES), BF16)],
        ),
        out_shape=[jax.ShapeDtypeStruct((ns, D_MODEL), BF16), jax.ShapeDtypeStruct((ns, 2 * LANES), BF16),
                   seg_i32, seg_i32, seg_i32, tile_i32, tile_i32],
        compiler_params=arb,
        name="moe_dispatch",
    )(h2c, h2l, cbc, cbl, rtc, rtl, cntc, cntl)

    wmap = lambda i, tg, tv: (tg[i], 0, 0)
    ys = pl.pallas_call(
        _experts_kernel,
        grid_spec=pltpu.PrefetchScalarGridSpec(
            num_scalar_prefetch=2, grid=(n_tiles,),
            in_specs=[
                pl.BlockSpec((MOE_TM, D_MODEL), lambda i, *_: (i, 0)),
                pl.BlockSpec((MOE_TM, 2 * LANES), lambda i, *_: (i, 0)),
                pl.BlockSpec((EXPERTS_PER_GROUP, D_MODEL, D_EXPERT), wmap),
                pl.BlockSpec((EXPERTS_PER_GROUP, D_MODEL, D_EXPERT), wmap),
                pl.BlockSpec((EXPERTS_PER_GROUP, D_EXPERT, D_MODEL), wmap),
            ],
            out_specs=pl.BlockSpec((MOE_TM, D_MODEL), lambda i, *_: (i, 0)),
        ),
        out_shape=jax.ShapeDtypeStruct((ns, D_MODEL), BF16),
        compiler_params=arb,
        name="moe_experts",
    )(tgroup, tvalid, xs, cs, wg, wu, wd)

    yc, yl = pl.pallas_call(
        functools.partial(_combine_kernel, nc, blocks_per_lat_seq),
        grid_spec=pltpu.PrefetchScalarGridSpec(
            num_scalar_prefetch=3, grid=(nb,),
            in_specs=[
                pl.BlockSpec((1, MOE_BLK, D_MODEL), cmap), pl.BlockSpec((1, MOE_BLK, D_MODEL), lmap),
                pl.BlockSpec((1, MOE_BLK, LANES), cmap), pl.BlockSpec((1, MOE_BLK, LANES), lmap),
                pl.BlockSpec((ns, D_MODEL), whole, **once),
                pl.BlockSpec(mod.shape, lambda *_: (0, 0, 0)),
                pl.BlockSpec((1, D_MODEL), whole),
            ],
            out_specs=[pl.BlockSpec((1, MOE_BLK, D_MODEL), cmap), pl.BlockSpec((1, MOE_BLK, D_MODEL), lmap)],
            scratch_shapes=[pltpu.VMEM((SORT_ROWS, D_MODEL), BF16)],
        ),
        out_shape=[jax.ShapeDtypeStruct((nc, MOE_BLK, D_MODEL), F32),
                   jax.ShapeDtypeStruct((nl, MOE_BLK, D_MODEL), F32)],
        compiler_params=arb,
        name="moe_combine",
    )(start, npiece, off, x1c, x1l, cbc, cbl, ys, mod, gf)
    return yc, yl


def _prep_weights(norm1_g, w_in, b_in, b_gates, w_dw, b_dw, conv_ln_g, conv_ln_b, w_conv_out,
                  mlstm_hn_g, w_mlstm_out, w_o, norm2_g, w_rg, b_rg, w_re, b_re):
    s_a = 2 * D_CONV
    s_q = s_a + D_MLSTM
    s_k = s_q + D_MLSTM
    s_v = s_k + D_MLSTM
    s_o = s_v + D_MLSTM
    s_g = s_o + 4 * N_HEADS
    row = lambda v: v.reshape(1, -1).astype(F32)
    w_t = w_in.T
    keep = [(0, s_q), (s_k, s_o), (s_g, w_in.shape[1])]
    wrow = _transpose_cast(w_t, [r for a, b in keep for r in range(a, b, WPREP_ROWS)])
    bg = (b_in[s_o:s_g] + b_gates.reshape(-1)).reshape(2, 2, N_HEADS).transpose(1, 0, 2).reshape(-1, 1)
    row_window = lambda start, n: _RowWindow(w_t, start, n)
    n_rt = N_EXPERTS + N_GROUPS
    wrt = jnp.pad(jnp.concatenate([w_re, w_rg], axis=1), ((0, 0), (0, LANES - n_rt)))
    wrt_hi = wrt.astype(BF16)
    wrt2 = jnp.concatenate([wrt_hi, (wrt - wrt_hi.astype(F32)).astype(BF16)], axis=1)
    brtT = jnp.pad(jnp.concatenate([b_re, b_rg]), (0, LANES - n_rt)).reshape(LANES, 1)
    return {
        "g1": row(norm1_g),
        "wrow": wrow, "brow": row(jnp.concatenate([b_in[a:b] for a, b in keep] + [b_in[s_q:s_k]])),
        "wkT": row_window(s_q, D_MLSTM), "wgifT": row_window(s_o, 4 * N_HEADS), "bgifT": bg,
        "wdw": w_dw.astype(F32), "bdw": row(b_dw), "lng": row(conv_ln_g), "lnb": row(conv_ln_b),
        "wco": w_conv_out.astype(BF16), "hng": row(mlstm_hn_g), "wmo": w_mlstm_out.astype(BF16),
        "wo": w_o.astype(BF16), "g2": row(norm2_g), "wrt2": wrt2, "brtT": brtT,
    }


def kernel(x_prompt, x_sample, state_C, state_n, state_m, c, c_ctx, norm1_g, w_ada, b_ada, w_in, b_in, b_gates, w_dw, b_dw, conv_ln_g, conv_ln_b, w_conv_out, mlstm_hn_g, w_mlstm_out, w_o, norm2_g, w_rg, b_rg, w_re, b_re, w_e_gate, w_e_up, w_e_down, norm_final_g):
    B, S, _ = x_prompt.shape
    Bd, Sd, _ = x_sample.shape
    assert w_ada.shape[0] == 1, "single trunk layer"
    assert S == SUB and Sd % SUB == 0

    mod = _ada(c_ctx.reshape(1, -1), c, w_ada[0], b_ada[0].reshape(1, -1))

    wts = _prep_weights(norm1_g[0], w_in[0], b_in[0], b_gates[0], w_dw[0], b_dw[0], conv_ln_g[0],
                        conv_ln_b[0], w_conv_out[0], mlstm_hn_g[0], w_mlstm_out[0], w_o[0],
                        norm2_g[0], w_rg[0], b_rg[0], w_re[0], b_re[0])

    x1p, h2p, cbp, rtp, cntp, c_new, n_new, m_new = _mixer(
        x_prompt.reshape(B * S // MIX_TM, MIX_TM, D_MODEL), S, mod, lambda b: 0, wts, P=S, emit_state=True)

    state = (state_C[:, 0].reshape(Bd, N_UNITS, HEAD_DIM, HEAD_DIM), state_n[:, 0].reshape(Bd, N_UNITS, HEAD_DIM),
             state_m[:, 0].reshape(Bd, N_UNITS))
    x1s, h2s, cbs, rts, cnts = _mixer(x_sample, Sd, mod, lambda b: 1 + b, wts, P=GRID_W, state=state)

    nc, nl = B * S // MOE_BLK, Bd * Sd // MOE_BLK
    blk = lambda a, n: a.reshape(n, MOE_BLK, a.shape[-1])
    yp, ys = _moe(blk(x1p, nc), blk(x1s, nl), blk(h2p, nc), blk(h2s, nl), blk(cbp, nc), blk(cbs, nl),
                  rtp.reshape(nc, 8, MOE_BLK), rts.reshape(nl, 8, MOE_BLK),
                  cntp.reshape(nc, 8, LANES), cnts.reshape(nl, 8, LANES),
                  mod, Sd // MOE_BLK, w_e_gate[0], w_e_up[0], w_e_down[0], norm_final_g.reshape(1, -1))

    return (yp.reshape(B, S, D_MODEL), ys.reshape(Bd, Sd, D_MODEL),
            c_new.reshape(B, 1, 2, N_HEADS, HEAD_DIM, HEAD_DIM),
            n_new.reshape(B, 1, 2, N_HEADS, HEAD_DIM),
            m_new[:, :, 0].reshape(B, 1, 2, N_HEADS))
```

```python
import functools
from typing import NamedTuple

import jax
import jax.numpy as jnp
from jax import lax
from jax.experimental import pallas as pl
from jax.experimental.pallas import tpu as pltpu

D_MODEL = 1024
D_CONV = 512
CONV_K = 31
D_MLSTM = 512
N_HEADS = 4
HEAD_DIM = D_MLSTM // N_HEADS
N_GROUPS = 4
EXPERTS_PER_GROUP = 4
N_EXPERTS = N_GROUPS * EXPERTS_PER_GROUP
D_EXPERT = 256
N_ADA = 6
EPS = 1e-6
GRID_W = 64

LANES = 128
SUB = 256
CONV_PAD = 16
CONV_RB = 64
N_UNITS = 2 * N_HEADS
ROW_ALIGN = 16
COPY_RUN = 4
MOE_TM = 512
MIX_TM = 512
MOE_BLK = MIX_TM
SORT_ROWS = MOE_BLK + N_GROUPS * ROW_ALIGN
ADA_PER_STEP = 2
WPREP_ROWS = 512
ROUTE_GROUP_LANE = N_EXPERTS
ROUTE_RANK_LANE = N_EXPERTS + 1
VMEM_LIMIT = 58 * 1024 * 1024

BF16 = jnp.bfloat16
F32 = jnp.float32
NT_DIMS = (((1,), (1,)), ((), ()))


def _dot(a, b):
    return jnp.dot(a, b, preferred_element_type=F32)


def _dot_nt(a, b, precision=None):
    return lax.dot_general(a, b, NT_DIMS, preferred_element_type=F32, precision=precision)


def _sigmoid(x):
    return 0.5 * jnp.tanh(0.5 * x) + 0.5


def _log_sigmoid(x):
    return jnp.minimum(x, 0.0) - jnp.log1p(jnp.exp(-jnp.abs(x)))


def _split3(x):
    hi = x.astype(BF16).astype(F32)
    r1 = x - hi
    mid = r1.astype(BF16).astype(F32)
    lo = (r1 - mid).astype(BF16).astype(F32)
    return hi, mid, lo


def _ada_kernel(cctx_ref, c_ref, w_ref, b_ref, o_ref):
    n = 1 + c_ref.shape[0]
    c = jnp.concatenate([cctx_ref[...], c_ref[...], jnp.zeros((8 - n, D_MODEL), F32)], axis=0)
    s = (c * _sigmoid(c)).astype(BF16)
    out = _dot(s, w_ref[...].astype(BF16)) + b_ref[...]
    for v in range(ADA_PER_STEP):
        o_ref[v] = out[:, v * D_MODEL:(v + 1) * D_MODEL]


def _ada(c_ctx, c, w_ada, b_ada):
    return pl.pallas_call(
        _ada_kernel,
        grid=(N_ADA // ADA_PER_STEP,),
        in_specs=[
            pl.BlockSpec(c_ctx.shape, lambda j: (0, 0)),
            pl.BlockSpec(c.shape, lambda j: (0, 0)),
            pl.BlockSpec((D_MODEL, ADA_PER_STEP * D_MODEL), lambda j: (0, j)),
            pl.BlockSpec((1, ADA_PER_STEP * D_MODEL), lambda j: (0, j)),
        ],
        out_specs=pl.BlockSpec((ADA_PER_STEP, 8, D_MODEL), lambda j: (j, 0, 0)),
        out_shape=jax.ShapeDtypeStruct((N_ADA, 8, D_MODEL), F32),
        compiler_params=pltpu.CompilerParams(dimension_semantics=("arbitrary",)),
        name="ada",
    )(c_ctx, c, w_ada, b_ada)


def _transpose_cast_kernel(starts_ref, wt_ref, o_ref):
    o_ref[...] = wt_ref[...].astype(BF16).T


def _transpose_cast(w_t, row_starts):
    n, k = len(row_starts), w_t.shape[1]
    return pl.pallas_call(
        _transpose_cast_kernel,
        grid_spec=pltpu.PrefetchScalarGridSpec(
            num_scalar_prefetch=1, grid=(n,),
            in_specs=[pl.BlockSpec((pl.Element(WPREP_ROWS), pl.Element(k)), lambda j, starts: (starts[j] * 8, 0))],
            out_specs=pl.BlockSpec((k, WPREP_ROWS), lambda j, starts: (0, j)),
        ),
        out_shape=jax.ShapeDtypeStruct((k, n * WPREP_ROWS), BF16),
        compiler_params=pltpu.CompilerParams(dimension_semantics=("arbitrary",)),
        name="transpose_cast",
    )(jnp.array([r // 8 for r in row_starts], jnp.int32), w_t)


WROW_OFFSET = {"wq": 2 * D_CONV, "wv": 2 * D_CONV + D_MLSTM, "wog": 2 * D_CONV + 2 * D_MLSTM,
               "wgm": 2 * D_CONV + 3 * D_MLSTM}
BROW_K_OFFSET = 2 * D_CONV + 3 * D_MLSTM + 2 * D_MODEL

_MIXER_WEIGHTS = (
    "g1", "wrow", "brow", "wkT", "wgifT", "bgifT", "wdw", "bdw", "lng", "lnb",
    "wco", "hng", "wmo", "wo", "g2", "wrt2", "brtT",
)


def _zero_after(x):
    bits = lax.bitcast_convert_type(x, jnp.uint32)
    bits = lax.shift_right_logical(lax.shift_right_logical(bits, jnp.uint32(16)), jnp.uint32(16))
    return lax.bitcast_convert_type(bits, F32)[0:1, :]


def _conv_block(upad_s, seg, base, cs, wdw_ref, bdw_ref, after=None):
    sub = 8
    first = CONV_PAD - CONV_K // 2
    acc = jnp.broadcast_to(bdw_ref[0:1, cs], (CONV_RB, LANES))
    for r in range(sub):
        z = None
        for a in range((CONV_K + first + sub - 1) // sub):
            j = sub * a + r - first
            if 0 <= j < CONV_K:
                lo = base + sub * a
                tap = wdw_ref[j:j + 1, cs] if after is None else wdw_ref[j:j + 1, cs] + after
                term = tap * upad_s[seg, lo:lo + CONV_RB + sub, cs]
                z = term if z is None else z + term
        acc = acc + z[r:r + CONV_RB, :]
    return acc


def _mixer_kernel(R, T, P, has_state, emit_state, mod_index, *refs):
    L = SUB
    n_mt = R // MIX_TM
    cpm = MIX_TM // L
    n_seq = R // T
    cps = T // L
    nseg = MIX_TM // P
    assert not has_state or n_seq == 1
    it = iter(refs)
    x_ref = next(it)
    mod_ref = next(it)
    if has_state:
        c0_ref = next(it)
        n0_ref = next(it)
        m0_ref = next(it)
    w = {name: next(it) for name in _MIXER_WEIGHTS}
    x1_ref = next(it)
    h2_ref = next(it)
    comb_ref = next(it)
    route_ref = next(it)
    cnt_ref = next(it)
    if emit_state:
        cout_ref = next(it)
        nout_ref = next(it)
        mout_ref = next(it)
    (q_s, kT_s, v_s, so_s, scan_s, ma_s, sgb_s, hm_s, cst_s, upad_s) = [next(it) for _ in range(10)]

    cond_row = mod_index(pl.program_id(0))

    def mod_row(i):
        return mod_ref[i, pl.ds(cond_row, 1), :]

    zpad = jnp.zeros((CONV_PAD, D_CONV), F32)
    for seg in range(nseg):
        upad_s[seg, 0:CONV_PAD, :] = zpad
        upad_s[seg, CONV_PAD + P:CONV_PAD + P + CONV_PAD, :] = zpad

    t_idx = lax.broadcasted_iota(jnp.int32, (L, L), 0)
    s_idx = lax.broadcasted_iota(jnp.int32, (L, L), 1)
    lower = s_idx <= t_idx
    upper = s_idx >= t_idx
    triu_b = upper.astype(F32).astype(BF16)
    lane_u = lax.broadcasted_iota(jnp.int32, (N_UNITS, L), 1)
    is_bwd = lax.broadcasted_iota(jnp.int32, (N_UNITS, L), 0) >= N_HEADS

    def gate_scan(g):
        gi, lf = g[:N_UNITS], _log_sigmoid(g[N_UNITS:])
        pr = _dot(jnp.concatenate(_split3(lf), axis=0).astype(BF16), triu_b)
        pre = pr[0:N_UNITS] + pr[N_UNITS:2 * N_UNITS] + pr[2 * N_UNITS:]
        tot = pre[:, L - 1:L]
        bsum = jnp.where(is_bwd, tot - pre + lf, pre)
        a = gi - bsum
        pm, sm, k = a, a, 1
        while k < L:
            pm = jnp.where(lane_u >= k, jnp.maximum(pm, pltpu.roll(pm, k, axis=1)), pm)
            sm = jnp.where(lane_u < L - k, jnp.maximum(sm, pltpu.roll(sm, L - k, axis=1)), sm)
            k *= 2
        wide = lambda v: jnp.broadcast_to(v, (N_UNITS, L))
        return jnp.concatenate([a, jnp.where(is_bwd, sm, pm), bsum, wide(tot),
                                wide(jnp.max(a, axis=1, keepdims=True))], axis=0)

    def phase1(i, carry):
        r0 = pl.multiple_of(i * MIX_TM, MIX_TM)
        rows = pl.ds(r0, MIX_TM)
        x = x_ref[0, rows, :]
        xn = x * lax.rsqrt(jnp.mean(x * x, axis=-1, keepdims=True) + EPS) * w["g1"][...]
        hb = (xn * (1.0 + mod_row(1)) + mod_row(0)).astype(BF16)

        gates = _dot_nt(w["wgifT"][...].astype(BF16), hb)
        gates = jnp.concatenate([gates[d * 2 * N_HEADS + g * N_HEADS:d * 2 * N_HEADS + (g + 1) * N_HEADS]
                                 for g in range(2) for d in range(2)], axis=0) + w["bgifT"][...]
        for j in range(cpm):
            scan_s[i * cpm + j] = gate_scan(gates[:, j * L:(j + 1) * L])
        ag = _dot(hb, w["wrow"][:, :2 * D_CONV]) + w["brow"][:, :2 * D_CONV]
        u = ag[:, :D_CONV] * _sigmoid(ag[:, D_CONV:])
        for seg in range(nseg):
            upad_s[seg, CONV_PAD:CONV_PAD + P, :] = u[seg * P:(seg + 1) * P, :]

        def proj(name, c0, gate, width=2 * LANES):
            w0 = WROW_OFFSET[name] + c0
            b = w["brow"][:, w0:w0 + width]
            if gate is not None:
                b = b + jnp.concatenate([gate] * (width // LANES), axis=1)
            return _dot(hb, w["wrow"][:, w0:w0 + width]) + b

        last = lambda z: z[-8:, -LANES:]
        bk_row = w["brow"][:, BROW_K_OFFSET:BROW_K_OFFSET + D_MLSTM]
        bk_col = jnp.concatenate([bk_row, jnp.zeros((LANES - 1, D_MLSTM), F32)], axis=0).T[:, 0:1]

        def gm_a(c0, gate):
            z = proj("wgm", c0, gate)
            ma_s[rows, c0:c0 + 2 * LANES] = _sigmoid(z)
            return last(z)

        def gm_b(c0, gate):
            z = proj("wgm", D_MODEL + c0, gate)
            sgb_s[rows, c0:c0 + 2 * LANES] = _sigmoid(z)
            return last(z)

        def q_part(c0, gate):
            z = proj("wq", c0, gate)
            q_s[rows, c0:c0 + 2 * LANES] = (z * (HEAD_DIM ** -0.5)).astype(BF16)
            return last(z)

        def v_part(c0, gate):
            z = proj("wv", c0, gate)
            v_s[rows, c0:c0 + 2 * LANES] = z.astype(BF16)
            return last(z)

        def o_part(c0, gate):
            z = proj("wog", c0, gate)
            so_s[rows, c0:c0 + 2 * LANES] = _sigmoid(z)
            return last(z)

        def k_part(c0, gate):
            rs = slice(c0, c0 + 2 * LANES)
            b = bk_col[rs, :] if gate is None else bk_col[rs, :] + gate[:, 0:1]
            z = _dot_nt(w["wkT"][rs, :].astype(BF16), hb) + b
            kt = z.astype(BF16)
            for j in range(cpm):
                kT_s[i * cpm + j, rs, :] = kt[:, j * L:(j + 1) * L]
            return last(z)

        jobs = ([functools.partial(gm_a, c0) for c0 in range(0, D_MODEL, 2 * LANES)]
                + [functools.partial(gm_b, c0) for c0 in range(0, D_MODEL, 2 * LANES)]
                + [functools.partial(f, c0) for f in (q_part, k_part, v_part, o_part)
                   for c0 in range(0, D_MLSTM, 2 * LANES)])
        n_jobs = len(jobs)
        conv = {}
        after = None
        n_pieces = (D_CONV // LANES) * nseg * (P // CONV_RB)
        for cb in range(D_CONV // LANES):
            cs = slice(cb * LANES, (cb + 1) * LANES)
            for seg in range(nseg):
                for rb in range(P // CONV_RB):
                    blk = _conv_block(upad_s, seg, rb * CONV_RB, cs, w["wdw"], w["bdw"], after)
                    conv[(cb, seg, rb)] = blk
                    if jobs and len(conv) * n_jobs >= (n_jobs - len(jobs) + 1) * n_pieces:
                        after = _zero_after(jobs.pop(0)(_zero_after(blk[-8:, :])))
        for job in jobs:
            job(None)
        cu = jnp.concatenate(
            [jnp.concatenate([conv[(cb, seg, rb)] for seg in range(nseg) for rb in range(P // CONV_RB)], axis=0)
             for cb in range(D_CONV // LANES)], axis=1)
        mu = jnp.mean(cu, axis=-1, keepdims=True)
        cc = cu - mu
        cn = cc * lax.rsqrt(jnp.mean(cc * cc, axis=-1, keepdims=True) + EPS) * w["lng"][...] + w["lnb"][...]
        ca = (cn * _sigmoid(cn)).astype(BF16)
        ma_s[rows, :] = ma_s[rows, :] * _dot(ca, w["wco"][...])
        return carry

    if n_mt == 1:
        phase1(0, 0)
    else:
        lax.fori_loop(0, n_mt, phase1, 0)

    ones_col = (lax.broadcasted_iota(jnp.int32, (L, HEAD_DIM), 1) == 0).astype(F32).astype(BF16)
    pad_rows = jnp.zeros((LANES - 3 * N_UNITS, L), F32)

    def gate_prep(c, m_vec):
        sc = scan_s[c]
        a, run_max, bsum = sc[0:N_UNITS], sc[N_UNITS:2 * N_UNITS], sc[2 * N_UNITS:3 * N_UNITS]
        tot, a_max = sc[3 * N_UNITS:4 * N_UNITS, 0:1], sc[4 * N_UNITS:5 * N_UNITS, 0:1]
        big_m = jnp.maximum(m_vec, run_max)
        m_end = jnp.maximum(m_vec, a_max)
        cols = jnp.concatenate(
            [big_m, jnp.exp(m_vec - big_m), jnp.exp(-bsum - big_m), pad_rows], axis=0).T
        return a, cols, jnp.exp(a - m_end), jnp.exp(m_vec - m_end), tot + m_end

    def unit_group(dirs, c, prep, first_chunk, want_state):
        a, cols, wk, decay, _ = prep
        rows = slice(c * L, (c + 1) * L)
        heads = range(N_HEADS)
        units = [(d, hd) for d in dirs for hd in heads]
        hs = [slice(hd * HEAD_DIM, (hd + 1) * HEAD_DIM) for hd in heads]
        idx = {u: u[0] * N_HEADS + u[1] for u in units}
        col = lambda k, u: cols[:, k * N_UNITS + idx[u]:k * N_UNITS + idx[u] + 1]
        row = lambda arr, u: arr[idx[u]:idx[u] + 1, :]
        chained = has_state or not first_chunk
        qc = [q_s[rows, hs[hd]] for hd in heads]
        kTc = [kT_s[c, hs[hd], :] for hd in heads]
        vaug = [jnp.concatenate([v_s[rows, hs[hd]], ones_col], axis=1) for hd in heads]
        qk = [_dot(qc[hd], kTc[hd]) for hd in heads]
        s_mat = {u: (qk[u[1]] * jnp.where(lower if u[0] == 0 else upper, jnp.exp(row(a, u) - col(0, u)), 0.0)
                     ).astype(BF16) for u in units}
        nd = {u: _dot(s_mat[u], vaug[u[1]]) for u in units}
        if chained:
            nd = {u: nd[u] + col(1, u) * _dot(qc[u[1]], cst_s[idx[u]].astype(BF16)) for u in units}
        h = {u: nd[u][:, :HEAD_DIM] * (1.0 / jnp.maximum(jnp.abs(nd[u][:, HEAD_DIM:HEAD_DIM + 1]), col(2, u)))
             for u in units}
        for hd in heads:
            total = h[(dirs[0], hd)]
            for d in dirs[1:]:
                total = total + h[(d, hd)]
            if dirs[0] == 0:
                hm_s[rows, hs[hd]] = total
            else:
                hm_s[rows, hs[hd]] = hm_s[rows, hs[hd]] + total
        if want_state:
            kw = {u: (kTc[u[1]].astype(F32) * row(wk, u)).astype(BF16) for u in units}
            upd = {u: _dot(kw[u], vaug[u[1]]) for u in units}
            for u in units:
                cst_s[idx[u]] = (upd[u] + row(decay, u) * cst_s[idx[u]]) if chained else upd[u]

    dir_rows = lax.broadcasted_iota(jnp.int32, (N_UNITS, 1), 0) >= N_HEADS
    for seq in range(n_seq):
        if has_state:
            n_cols = jnp.concatenate([n0_ref[0], jnp.zeros((LANES - N_UNITS, HEAD_DIM), F32)], axis=0).T
            first_lane = lax.broadcasted_iota(jnp.int32, (HEAD_DIM, HEAD_DIM), 1) == 0
            for idx in range(N_UNITS):
                cst_s[idx, :, :HEAD_DIM] = c0_ref[0, idx]
                cst_s[idx, :, HEAD_DIM:] = jnp.where(first_lane, n_cols[:, idx:idx + 1], 0.0)
            unit_row = lax.broadcasted_iota(jnp.int32, (N_UNITS, 1), 0)
            m_vec = jnp.zeros((N_UNITS, 1), F32)
            for idx in range(N_UNITS):
                m_vec = jnp.where(unit_row == idx, m0_ref[pl.program_id(0), idx], m_vec)
        else:
            m_vec = jnp.zeros((N_UNITS, 1), F32)
        if cps == 1:
            prep = gate_prep(seq, m_vec)
            unit_group([0, 1], seq, prep, True, emit_state)
            m_vec = prep[4]
        else:
            for d in range(2):
                order = list(range(cps)) if d == 0 else list(range(cps - 1, -1, -1))
                for pos, c in enumerate(order):
                    prep = gate_prep(seq * cps + c, m_vec)
                    unit_group([d], seq * cps + c, prep, pos == 0, emit_state or pos < cps - 1)
                    m_vec = jnp.where(dir_rows == (d == 1), prep[4], m_vec)
        if emit_state:
            for idx in range(N_UNITS):
                caug = cst_s[idx]
                cout_ref[0, seq * N_UNITS + idx] = caug[:, :HEAD_DIM]
                nout_ref[0, seq * N_UNITS + idx:seq * N_UNITS + idx + 1, :] = caug[:, HEAD_DIM:].T[0:1, :]
            mout_ref[0, seq * N_UNITS:(seq + 1) * N_UNITS, :] = jnp.broadcast_to(m_vec, (N_UNITS, LANES))

    e_iota = lax.broadcasted_iota(jnp.int32, (LANES, MIX_TM), 0)
    g_of_e = lax.shift_right_logical(e_iota, 2)
    j_of_e = lax.bitwise_and(e_iota, EXPERTS_PER_GROUP - 1)
    r8 = lax.broadcasted_iota(jnp.int32, (8, MIX_TM), 0)
    before_b = (lax.broadcasted_iota(jnp.int32, (MOE_BLK, MOE_BLK), 0)
                < lax.broadcasted_iota(jnp.int32, (MOE_BLK, MOE_BLK), 1)).astype(F32).astype(BF16)

    def phase3(i, carry):
        r0 = pl.multiple_of(i * MIX_TM, MIX_TM)
        rows = pl.ds(r0, MIX_TM)
        hm = hm_s[rows, :]
        heads = []
        for hd in range(N_HEADS):
            hh = hm[:, hd * HEAD_DIM:(hd + 1) * HEAD_DIM]
            heads.append(hh * lax.rsqrt(jnp.mean(hh * hh, axis=-1, keepdims=True) + EPS))
        hn = jnp.concatenate(heads, axis=1) * w["hng"][...]
        hb2 = (so_s[rows, :] * hn).astype(BF16)
        br_b = _dot(hb2, w["wmo"][...])
        mixed = (ma_s[rows, :] + sgb_s[rows, :] * br_b).astype(BF16)
        x1 = x_ref[0, rows, :] + mod_row(2) * _dot(mixed, w["wo"][...])
        x1_ref[0, rows, :] = x1
        xn = x1 * lax.rsqrt(jnp.mean(x1 * x1, axis=-1, keepdims=True) + EPS) * w["g2"][...]
        h2 = xn * (1.0 + mod_row(4)) + mod_row(3)
        h2_ref[0, rows, :] = h2.astype(BF16)

        h2_hi = h2.astype(BF16)
        h2_lo = (h2 - h2_hi.astype(F32)).astype(BF16)
        lg = _dot(h2_hi, w["wrt2"][...])
        lg = lg[:, :LANES] + lg[:, LANES:] + _dot(h2_lo, w["wrt2"][:, :LANES])
        lt = lg.T + w["brtT"][...]
        gl = [lt[N_EXPERTS + g:N_EXPERTS + g + 1, :] for g in range(N_GROUPS)]
        best, gsel = gl[0], jnp.zeros((1, MIX_TM), jnp.int32)
        for g in range(1, N_GROUPS):
            better = gl[g] > best
            gsel = jnp.where(better, g, gsel)
            best = jnp.where(better, gl[g], best)
        gp_sel = 1.0 / sum(jnp.exp(v - best) for v in gl)
        el = []
        for j in range(EXPERTS_PER_GROUP):
            v = lt[j:j + 1, :]
            for g in range(1, N_GROUPS):
                r = g * EXPERTS_PER_GROUP + j
                v = jnp.where(gsel == g, lt[r:r + 1, :], v)
            el.append(v)
        l1, e1 = el[0], jnp.zeros((1, MIX_TM), jnp.int32)
        for j in range(1, EXPERTS_PER_GROUP):
            better = el[j] > l1
            e1 = jnp.where(better, j, e1)
            l1 = jnp.where(better, el[j], l1)
        l2 = jnp.full((1, MIX_TM), -jnp.inf, F32)
        e2 = jnp.zeros((1, MIX_TM), jnp.int32)
        for j in range(EXPERTS_PER_GROUP):
            better = jnp.logical_and(e1 != j, el[j] > l2)
            e2 = jnp.where(better, j, e2)
            l2 = jnp.where(better, el[j], l2)
        r2 = jnp.exp(l2 - l1)
        wt1 = gp_sel / (1.0 + r2)
        wt2 = gp_sel * r2 / (1.0 + r2)
        in_group = g_of_e == gsel
        comb_t = (jnp.where(jnp.logical_and(in_group, j_of_e == e1), wt1, 0.0)
                  + jnp.where(jnp.logical_and(in_group, j_of_e == e2), wt2, 0.0))

        onehot = (r8 == gsel).astype(F32)
        gsel_f = gsel.astype(F32)
        rank = jnp.sum(onehot * _dot(onehot.astype(BF16), before_b), axis=0, keepdims=True)
        r8rows = pl.ds(pl.multiple_of(i * 8, 8), 8)
        route_ref[0, r8rows, :] = jnp.where(r8 == 0, gsel_f, jnp.where(r8 == 1, rank, 0.0))
        cnt_ref[0, r8rows, :] = jnp.broadcast_to(jnp.sum(onehot, axis=1, keepdims=True), (8, LANES))
        comb_t = jnp.where(e_iota == ROUTE_GROUP_LANE, gsel_f,
                           jnp.where(e_iota == ROUTE_RANK_LANE, rank, comb_t))
        comb_ref[0, rows, :] = comb_t.T
        return carry

    if n_mt == 1:
        phase3(0, 0)
    else:
        lax.fori_loop(0, n_mt, phase3, 0)


class _RowWindow(NamedTuple):
    array: jax.Array
    start: int
    n: int


def _const_spec(a):
    if isinstance(a, _RowWindow):
        assert a.start % a.n == 0
        return a.array, pl.BlockSpec((a.n, a.array.shape[1]), lambda b: (a.start // a.n, 0),
                                     pipeline_mode=pl.Buffered(1))
    nd = a.ndim
    return a, pl.BlockSpec(a.shape, lambda b, _nd=nd: (0,) * _nd, pipeline_mode=pl.Buffered(1))


def _mixer(x, T, mod, mod_index, weights, P, state=None, emit_state=False):
    B, R, _ = x.shape
    n_chunks = R // SUB
    n_blk = R // MOE_BLK
    n_seq = R // T
    has_state = state is not None
    seq_mode = {} if R <= MIX_TM else {"pipeline_mode": pl.Buffered(1)}
    in_specs = [
        pl.BlockSpec((1, R, D_MODEL), lambda b: (b, 0, 0), **seq_mode),
        pl.BlockSpec(mod.shape, lambda b: (0, 0, 0)),
    ]
    args = [x, mod]
    if has_state:
        c0, n0, m0 = state
        in_specs += [
            pl.BlockSpec((1, N_UNITS, HEAD_DIM, HEAD_DIM), lambda b: (b, 0, 0, 0)),
            pl.BlockSpec((1, N_UNITS, HEAD_DIM), lambda b: (b, 0, 0)),
            pl.BlockSpec(memory_space=pltpu.SMEM),
        ]
        args += [c0, n0, m0]
    for name in _MIXER_WEIGHTS:
        operand, spec = _const_spec(weights[name])
        in_specs.append(spec)
        args.append(operand)
    out_shape = [
        jax.ShapeDtypeStruct((B, R, D_MODEL), F32),
        jax.ShapeDtypeStruct((B, R, D_MODEL), BF16),
        jax.ShapeDtypeStruct((B, R, LANES), F32),
        jax.ShapeDtypeStruct((B, n_blk * 8, MOE_BLK), F32),
        jax.ShapeDtypeStruct((B, n_blk * 8, LANES), F32),
    ]
    out_specs = [
        pl.BlockSpec((1, R, D_MODEL), lambda b: (b, 0, 0), **seq_mode),
        pl.BlockSpec((1, R, D_MODEL), lambda b: (b, 0, 0), **seq_mode),
        pl.BlockSpec((1, R, LANES), lambda b: (b, 0, 0)),
        pl.BlockSpec((1, n_blk * 8, MOE_BLK), lambda b: (b, 0, 0)),
        pl.BlockSpec((1, n_blk * 8, LANES), lambda b: (b, 0, 0)),
    ]
    if emit_state:
        out_shape += [
            jax.ShapeDtypeStruct((B, n_seq * N_UNITS, HEAD_DIM, HEAD_DIM), F32),
            jax.ShapeDtypeStruct((B, n_seq * N_UNITS, HEAD_DIM), F32),
            jax.ShapeDtypeStruct((B, n_seq * N_UNITS, LANES), F32),
        ]
        out_specs += [
            pl.BlockSpec((1, n_seq * N_UNITS, HEAD_DIM, HEAD_DIM), lambda b: (b, 0, 0, 0)),
            pl.BlockSpec((1, n_seq * N_UNITS, HEAD_DIM), lambda b: (b, 0, 0)),
            pl.BlockSpec((1, n_seq * N_UNITS, LANES), lambda b: (b, 0, 0)),
        ]
    scratch = [
        pltpu.VMEM((R, D_MLSTM), BF16),
        pltpu.VMEM((n_chunks, D_MLSTM, SUB), BF16),
        pltpu.VMEM((R, D_MLSTM), BF16),
        pltpu.VMEM((R, D_MLSTM), F32),
        pltpu.VMEM((n_chunks, 5 * N_UNITS, SUB), F32),
        pltpu.VMEM((R, D_MODEL), F32),
        pltpu.VMEM((R, D_MODEL), F32),
        pltpu.VMEM((R, D_MLSTM), F32),
        pltpu.VMEM((N_UNITS, HEAD_DIM, 2 * HEAD_DIM), F32),
        pltpu.VMEM((MIX_TM // P, P + 2 * CONV_PAD, D_CONV), F32),
    ]
    return pl.pallas_call(
        functools.partial(_mixer_kernel, R, T, P, has_state, emit_state, mod_index),
        grid=(B,),
        in_specs=in_specs,
        out_specs=out_specs,
        out_shape=out_shape,
        scratch_shapes=scratch,
        compiler_params=pltpu.CompilerParams(
            dimension_semantics=("arbitrary",), vmem_limit_bytes=VMEM_LIMIT),
        name="mixer_T%d" % T,
    )(*args)


def _dest_in_block(group, rank, starts):
    dest = rank
    for g in range(N_GROUPS):
        dest = dest + jnp.where(group == float(g), starts[g], 0.0)
    return dest


def _copy_segments(src_refs, dst_refs, src_starts, dst_starts, n_pieces):
    def copy(g, first_piece, n_rows):
        s = pl.multiple_of(src_starts[g] + first_piece * ROW_ALIGN, ROW_ALIGN)
        d = pl.multiple_of(dst_starts[g] + first_piece * ROW_ALIGN, ROW_ALIGN)
        for src, dst in zip(src_refs, dst_refs):
            dst[pl.ds(d, n_rows), :] = src[pl.ds(s, n_rows), :]

    for g in range(N_GROUPS):
        n_runs = lax.shift_right_logical(n_pieces[g], COPY_RUN.bit_length() - 1)

        def run(k, carry, g=g):
            copy(g, k * COPY_RUN, COPY_RUN * ROW_ALIGN)
            return carry

        def single(k, carry, g=g):
            copy(g, k, ROW_ALIGN)
            return carry

        lax.fori_loop(0, n_runs, run, 0)
        lax.fori_loop(n_runs * COPY_RUN, n_pieces[g], single, 0)


def _plan_segments(n_blocks, n_tiles, count, start_ref, npiece_ref, off_ref, tgroup_ref, tvalid_ref):
    align_shift = ROW_ALIGN.bit_length() - 1
    tile_shift = MOE_TM.bit_length() - 1

    def block_starts(blk, carry):
        row = jnp.int32(0)
        for g in range(N_GROUPS):
            n = lax.shift_right_logical(count(blk, g) + (ROW_ALIGN - 1), align_shift)
            npiece_ref[blk * N_GROUPS + g] = n
            start_ref[blk * N_GROUPS + g] = row
            row = row + n * ROW_ALIGN
        return carry

    lax.fori_loop(0, n_blocks, block_starts, 0)

    base_row = jnp.int32(0)
    base_tile = jnp.int32(0)
    last_group = jnp.int32(0)
    for g in range(N_GROUPS):
        def seg_offsets(blk, row, g=g, base_row=base_row):
            off_ref[blk * N_GROUPS + g] = base_row + row
            return row + npiece_ref[blk * N_GROUPS + g] * ROW_ALIGN

        rows = lax.fori_loop(0, n_blocks, seg_offsets, jnp.int32(0))
        tiles = lax.shift_right_logical(rows + (MOE_TM - 1), tile_shift)

        def mark_tiles(t, carry, g=g, base_tile=base_tile):
            tgroup_ref[base_tile + t] = g
            tvalid_ref[base_tile + t] = 1
            return carry

        lax.fori_loop(0, tiles, mark_tiles, 0)
        last_group = jnp.where(tiles > 0, g, last_group)
        base_row = base_row + tiles * MOE_TM
        base_tile = base_tile + tiles

    def mark_unused(t, carry):
        tgroup_ref[t] = last_group
        tvalid_ref[t] = 0
        return carry

    lax.fori_loop(base_tile, n_tiles, mark_unused, 0)


def _dispatch_kernel(n_ctx_blocks, n_blocks, n_tiles,
                     h2c_ref, h2l_ref, cbc_ref, cbl_ref, rtc_ref, rtl_ref, cntc_ref, cntl_ref,
                     xs_ref, cs_ref, start_ref, npiece_ref, off_ref, tgroup_ref, tvalid_ref,
                     sx_s, sc_s):
    b = pl.program_id(0)
    is_ctx = b < n_ctx_blocks

    def count(blk, g):
        vc = cntc_ref[jnp.minimum(blk, n_ctx_blocks - 1), pl.ds(g, 1), pl.ds(0, 1)]
        vl = cntl_ref[jnp.maximum(blk - n_ctx_blocks, 0), pl.ds(g, 1), pl.ds(0, 1)]
        return jnp.where(blk < n_ctx_blocks, vc, vl)[0, 0].astype(jnp.int32)

    @pl.when(b == 0)
    def _():
        _plan_segments(n_blocks, n_tiles, count, start_ref, npiece_ref, off_ref, tgroup_ref, tvalid_ref)
        xs_ref[...] = jnp.zeros_like(xs_ref)
        cs_ref[...] = jnp.zeros_like(cs_ref)

    h2 = jnp.where(is_ctx, h2c_ref[0], h2l_ref[0])
    cb = jnp.where(is_ctx, cbc_ref[0], cbl_ref[0])
    rt = jnp.where(is_ctx, rtc_ref[0], rtl_ref[0])
    starts = [start_ref[b * N_GROUPS + g] for g in range(N_GROUPS)]
    dest = _dest_in_block(rt[0:1, :], rt[1:2, :], [s.astype(F32) for s in starts])
    row = lax.broadcasted_iota(jnp.int32, (SORT_ROWS, MOE_BLK), 0).astype(F32)
    perm = (row == dest).astype(F32).astype(BF16)
    cb_hi = cb.astype(BF16)
    cb_lo = (cb - cb_hi.astype(F32)).astype(BF16)
    sx_s[...] = _dot(perm, h2).astype(BF16)
    sc_s[...] = _dot(perm, jnp.concatenate([cb_hi, cb_lo], axis=1)).astype(BF16)
    _copy_segments((sx_s, sc_s), (xs_ref, cs_ref), starts,
                   [off_ref[b * N_GROUPS + g] for g in range(N_GROUPS)],
                   [npiece_ref[b * N_GROUPS + g] for g in range(N_GROUPS)])


def _experts_kernel(tgroup_ref, tvalid_ref, xs_ref, cs_ref, wg_ref, wu_ref, wd_ref, ys_ref):
    i = pl.program_id(0)

    @pl.when(tvalid_ref[i] == 1)
    def _():
        x = xs_ref[...]
        comb = cs_ref[:, :LANES].astype(F32) + cs_ref[:, LANES:].astype(F32)
        lane = lax.broadcasted_iota(jnp.int32, comb.shape, 1)
        first = tgroup_ref[i] * EXPERTS_PER_GROUP
        acc = None
        for j in range(EXPERTS_PER_GROUP):
            gj = _dot(x, wg_ref[j].astype(BF16))
            uj = _dot(x, wu_ref[j].astype(BF16))
            cw = jnp.sum(jnp.where(lane == first + j, comb, 0.0), axis=1, keepdims=True)
            out = _dot((gj * _sigmoid(gj) * uj * cw).astype(BF16), wd_ref[j].astype(BF16))
            acc = out if acc is None else acc + out
        ys_ref[...] = acc.astype(BF16)

    @pl.when(tvalid_ref[i] == 0)
    def _():
        ys_ref[...] = jnp.zeros_like(ys_ref)


def _combine_kernel(n_ctx_blocks, blocks_per_lat_seq, start_ref, npiece_ref, off_ref,
                    x1c_ref, x1l_ref, cbc_ref, cbl_ref, ys_ref, mod_ref, gf_ref, yc_ref, yl_ref, loc_s):
    b = pl.program_id(0)
    is_ctx = b < n_ctx_blocks
    starts = [start_ref[b * N_GROUPS + g] for g in range(N_GROUPS)]
    loc_s[...] = jnp.zeros_like(loc_s)
    _copy_segments((ys_ref,), (loc_s,), [off_ref[b * N_GROUPS + g] for g in range(N_GROUPS)], starts,
                   [npiece_ref[b * N_GROUPS + g] for g in range(N_GROUPS)])
    cb = jnp.where(is_ctx, cbc_ref[0], cbl_ref[0])
    dest = _dest_in_block(cb[:, ROUTE_GROUP_LANE:ROUTE_GROUP_LANE + 1],
                          cb[:, ROUTE_RANK_LANE:ROUTE_RANK_LANE + 1],
                          [s.astype(F32) for s in starts])
    col = lax.broadcasted_iota(jnp.int32, (MOE_BLK, SORT_ROWS), 1).astype(F32)
    unperm = (col == dest).astype(F32).astype(BF16)
    moe = _dot(unperm, loc_s[...])
    x1 = jnp.where(is_ctx, x1c_ref[0], x1l_ref[0])
    mrow = jnp.where(is_ctx, 0, 1 + jnp.maximum(b - n_ctx_blocks, 0) // blocks_per_lat_seq)
    x2 = x1 + mod_ref[N_ADA - 1, pl.ds(mrow, 1), :] * moe
    y = x2 * lax.rsqrt(jnp.mean(x2 * x2, axis=-1, keepdims=True) + EPS) * gf_ref[...]

    @pl.when(is_ctx)
    def _():
        yc_ref[0] = y

    @pl.when(jnp.logical_not(is_ctx))
    def _():
        yl_ref[0] = y


def _moe(x1c, x1l, h2c, h2l, cbc, cbl, rtc, rtl, cntc, cntl, mod, blocks_per_lat_seq, wg, wu, wd, gf):
    nc, nl = x1c.shape[0], x1l.shape[0]
    nb = nc + nl
    n_rows_max = nb * MOE_BLK + nb * N_GROUPS * (ROW_ALIGN - 1) + N_GROUPS * (MOE_TM - ROW_ALIGN)
    n_tiles = -(-n_rows_max // MOE_TM)
    ns = n_tiles * MOE_TM

    cmap = lambda b, *_: (jnp.minimum(b, nc - 1), 0, 0)
    lmap = lambda b, *_: (jnp.maximum(b - nc, 0), 0, 0)
    whole = lambda *_: (0, 0)
    once = {"pipeline_mode": pl.Buffered(1)}
    arb = pltpu.CompilerParams(dimension_semantics=("arbitrary",), vmem_limit_bytes=VMEM_LIMIT)
    smem = pl.BlockSpec(memory_space=pltpu.SMEM)
    seg_i32 = jax.ShapeDtypeStruct((nb * N_GROUPS,), jnp.int32)
    tile_i32 = jax.ShapeDtypeStruct((n_tiles,), jnp.int32)

    xs, cs, start, npiece, off, tgroup, tvalid = pl.pallas_call(
        functools.partial(_dispatch_kernel, nc, nb, n_tiles),
        grid_spec=pltpu.PrefetchScalarGridSpec(
            num_scalar_prefetch=0, grid=(nb,),
            in_specs=[
                pl.BlockSpec((1, MOE_BLK, D_MODEL), cmap), pl.BlockSpec((1, MOE_BLK, D_MODEL), lmap),
                pl.BlockSpec((1, MOE_BLK, LANES), cmap), pl.BlockSpec((1, MOE_BLK, LANES), lmap),
                pl.BlockSpec((1, 8, MOE_BLK), cmap), pl.BlockSpec((1, 8, MOE_BLK), lmap),
                pl.BlockSpec(cntc.shape, lambda b: (0, 0, 0)), pl.BlockSpec(cntl.shape, lambda b: (0, 0, 0)),
            ],
            out_specs=[pl.BlockSpec((ns, D_MODEL), whole, **once), pl.BlockSpec((ns, 2 * LANES), whole, **once),
                       smem, smem, smem, smem, smem],
            scratch_shapes=[pltpu.VMEM((SORT_ROWS, D_MODEL), BF16), pltpu.VMEM((SORT_ROWS, 2 * LANES), BF16)],
        ),
        out_shape=[jax.ShapeDtypeStruct((ns, D_MODEL), BF16), jax.ShapeDtypeStruct((ns, 2 * LANES), BF16),
                   seg_i32, seg_i32, seg_i32, tile_i32, tile_i32],
        compiler_params=arb,
        name="moe_dispatch",
    )(h2c, h2l, cbc, cbl, rtc, rtl, cntc, cntl)

    wmap = lambda i, tg, tv: (tg[i], 0, 0)
    ys = pl.pallas_call(
        _experts_kernel,
        grid_spec=pltpu.PrefetchScalarGridSpec(
            num_scalar_prefetch=2, grid=(n_tiles,),
            in_specs=[
                pl.BlockSpec((MOE_TM, D_MODEL), lambda i, *_: (i, 0)),
                pl.BlockSpec((MOE_TM, 2 * LANES), lambda i, *_: (i, 0)),
                pl.BlockSpec((EXPERTS_PER_GROUP, D_MODEL, D_EXPERT), wmap),
                pl.BlockSpec((EXPERTS_PER_GROUP, D_MODEL, D_EXPERT), wmap),
                pl.BlockSpec((EXPERTS_PER_GROUP, D_EXPERT, D_MODEL), wmap),
            ],
            out_specs=pl.BlockSpec((MOE_TM, D_MODEL), lambda i, *_: (i, 0)),
        ),
        out_shape=jax.ShapeDtypeStruct((ns, D_MODEL), BF16),
        compiler_params=arb,
        name="moe_experts",
    )(tgroup, tvalid, xs, cs, wg, wu, wd)

    yc, yl = pl.pallas_call(
        functools.partial(_combine_kernel, nc, blocks_per_lat_seq),
        grid_spec=pltpu.PrefetchScalarGridSpec(
            num_scalar_prefetch=3, grid=(nb,),
            in_specs=[
                pl.BlockSpec((1, MOE_BLK, D_MODEL), cmap), pl.BlockSpec((1, MOE_BLK, D_MODEL), lmap),
                pl.BlockSpec((1, MOE_BLK, LANES), cmap), pl.BlockSpec((1, MOE_BLK, LANES), lmap),
                pl.BlockSpec((ns, D_MODEL), whole, **once),
                pl.BlockSpec(mod.shape, lambda *_: (0, 0, 0)),
                pl.BlockSpec((1, D_MODEL), whole),
            ],
            out_specs=[pl.BlockSpec((1, MOE_BLK, D_MODEL), cmap), pl.BlockSpec((1, MOE_BLK, D_MODEL), lmap)],
            scratch_shapes=[pltpu.VMEM((SORT_ROWS, D_MODEL), BF16)],
        ),
        out_shape=[jax.ShapeDtypeStruct((nc, MOE_BLK, D_MODEL), F32),
                   jax.ShapeDtypeStruct((nl, MOE_BLK, D_MODEL), F32)],
        compiler_params=arb,
        name="moe_combine",
    )(start, npiece, off, x1c, x1l, cbc, cbl, ys, mod, gf)
    return yc, yl


def _prep_weights(norm1_g, w_in, b_in, b_gates, w_dw, b_dw, conv_ln_g, conv_ln_b, w_conv_out,
                  mlstm_hn_g, w_mlstm_out, w_o, norm2_g, w_rg, b_rg, w_re, b_re):
    s_a = 2 * D_CONV
    s_q = s_a + D_MLSTM
    s_k = s_q + D_MLSTM
    s_v = s_k + D_MLSTM
    s_o = s_v + D_MLSTM
    s_g = s_o + 4 * N_HEADS
    row = lambda v: v.reshape(1, -1).astype(F32)
    w_t = w_in.T
    keep = [(0, s_q), (s_k, s_o), (s_g, w_in.shape[1])]
    wrow = _transpose_cast(w_t, [r for a, b in keep for r in range(a, b, WPREP_ROWS)])
    bg = (b_in[s_o:s_g] + b_gates.reshape(-1)).reshape(2, 2, N_HEADS).transpose(1, 0, 2).reshape(-1, 1)
    row_window = lambda start, n: _RowWindow(w_t, start, n)
    n_rt = N_EXPERTS + N_GROUPS
    wrt = jnp.pad(jnp.concatenate([w_re, w_rg], axis=1), ((0, 0), (0, LANES - n_rt)))
    wrt_hi = wrt.astype(BF16)
    wrt2 = jnp.concatenate([wrt_hi, (wrt - wrt_hi.astype(F32)).astype(BF16)], axis=1)
    brtT = jnp.pad(jnp.concatenate([b_re, b_rg]), (0, LANES - n_rt)).reshape(LANES, 1)
    return {
        "g1": row(norm1_g),
        "wrow": wrow, "brow": row(jnp.concatenate([b_in[a:b] for a, b in keep] + [b_in[s_q:s_k]])),
        "wkT": row_window(s_q, D_MLSTM), "wgifT": row_window(s_o, 4 * N_HEADS), "bgifT": bg,
        "wdw": w_dw.astype(F32), "bdw": row(b_dw), "lng": row(conv_ln_g), "lnb": row(conv_ln_b),
        "wco": w_conv_out.astype(BF16), "hng": row(mlstm_hn_g), "wmo": w_mlstm_out.astype(BF16),
        "wo": w_o.astype(BF16), "g2": row(norm2_g), "wrt2": wrt2, "brtT": brtT,
    }


def kernel(x_prompt, x_sample, state_C, state_n, state_m, c, c_ctx, norm1_g, w_ada, b_ada, w_in, b_in, b_gates, w_dw, b_dw, conv_ln_g, conv_ln_b, w_conv_out, mlstm_hn_g, w_mlstm_out, w_o, norm2_g, w_rg, b_rg, w_re, b_re, w_e_gate, w_e_up, w_e_down, norm_final_g):
    B, S, _ = x_prompt.shape
    Bd, Sd, _ = x_sample.shape
    assert w_ada.shape[0] == 1, "single trunk layer"
    assert MIX_TM % S == 0 and S % SUB == 0 and Sd % MIX_TM == 0

    mod = _ada(c_ctx.reshape(1, -1), c, w_ada[0], b_ada[0].reshape(1, -1))

    wts = _prep_weights(norm1_g[0], w_in[0], b_in[0], b_gates[0], w_dw[0], b_dw[0], conv_ln_g[0],
                        conv_ln_b[0], w_conv_out[0], mlstm_hn_g[0], w_mlstm_out[0], w_o[0],
                        norm2_g[0], w_rg[0], b_rg[0], w_re[0], b_re[0])

    x1p, h2p, cbp, rtp, cntp, c_new, n_new, m_new = _mixer(
        x_prompt.reshape(B * S // MIX_TM, MIX_TM, D_MODEL), S, mod, lambda b: 0, wts, P=S, emit_state=True)

    state = (state_C[:, 0].reshape(Bd, N_UNITS, HEAD_DIM, HEAD_DIM), state_n[:, 0].reshape(Bd, N_UNITS, HEAD_DIM),
             state_m[:, 0].reshape(Bd, N_UNITS))
    x1s, h2s, cbs, rts, cnts = _mixer(x_sample, Sd, mod, lambda b: 1 + b, wts, P=GRID_W, state=state)

    nc, nl = B * S // MOE_BLK, Bd * Sd // MOE_BLK
    blk = lambda a, n: a.reshape(n, MOE_BLK, a.shape[-1])
    yp, ys = _moe(blk(x1p, nc), blk(x1s, nl), blk(h2p, nc), blk(h2s, nl), blk(cbp, nc), blk(cbs, nl),
                  rtp.reshape(nc, 8, MOE_BLK), rts.reshape(nl, 8, MOE_BLK),
                  cntp.reshape(nc, 8, LANES), cnts.reshape(nl, 8, LANES),
                  mod, Sd // MOE_BLK, w_e_gate[0], w_e_up[0], w_e_down[0], norm_final_g.reshape(1, -1))

    return (yp.reshape(B, S, D_MODEL), ys.reshape(Bd, Sd, D_MODEL),
            c_new.reshape(B, 1, 2, N_HEADS, HEAD_DIM, HEAD_DIM),
            n_new.reshape(B, 1, 2, N_HEADS, HEAD_DIM),
            m_new[:, :, 0].reshape(B, 1, 2, N_HEADS))
```

```python
import functools
from typing import NamedTuple

import jax
import jax.numpy as jnp
from jax import lax
from jax.experimental import pallas as pl
from jax.experimental.pallas import tpu as pltpu

D_MODEL = 1024
D_CONV = 512
CONV_K = 31
D_MLSTM = 512
N_HEADS = 4
HEAD_DIM = D_MLSTM // N_HEADS
N_GROUPS = 4
EXPERTS_PER_GROUP = 4
N_EXPERTS = N_GROUPS * EXPERTS_PER_GROUP
D_EXPERT = 256
N_ADA = 6
EPS = 1e-6
GRID_W = 64

LANES = 128
SUB = 256
CONV_PAD = 16
CONV_RB = 64
N_UNITS = 2 * N_HEADS
ROW_ALIGN = 16
COPY_RUN = 4
MOE_TM = 512
MIX_TM = 512
MOE_BLK = MIX_TM
SORT_ROWS = MOE_BLK + N_GROUPS * ROW_ALIGN
ADA_PER_STEP = 2
WPREP_ROWS = 512
ROUTE_GROUP_LANE = N_EXPERTS
ROUTE_RANK_LANE = N_EXPERTS + 1
VMEM_LIMIT = 58 * 1024 * 1024

BF16 = jnp.bfloat16
F32 = jnp.float32
NT_DIMS = (((1,), (1,)), ((), ()))


def _dot(a, b):
    return jnp.dot(a, b, preferred_element_type=F32)


def _dot_nt(a, b, precision=None):
    return lax.dot_general(a, b, NT_DIMS, preferred_element_type=F32, precision=precision)


def _sigmoid(x):
    return 0.5 * jnp.tanh(0.5 * x) + 0.5


def _log_sigmoid(x):
    return jnp.minimum(x, 0.0) - jnp.log1p(jnp.exp(-jnp.abs(x)))


def _split3(x):
    hi = x.astype(BF16).astype(F32)
    r1 = x - hi
    mid = r1.astype(BF16).astype(F32)
    lo = (r1 - mid).astype(BF16).astype(F32)
    return hi, mid, lo


def _ada_kernel(cctx_ref, c_ref, w_ref, b_ref, o_ref):
    n = 1 + c_ref.shape[0]
    c = jnp.concatenate([cctx_ref[...], c_ref[...], jnp.zeros((8 - n, D_MODEL), F32)], axis=0)
    s = (c * _sigmoid(c)).astype(BF16)
    out = _dot(s, w_ref[...].astype(BF16)) + b_ref[...]
    for v in range(ADA_PER_STEP):
        o_ref[v] = out[:, v * D_MODEL:(v + 1) * D_MODEL]


def _ada(c_ctx, c, w_ada, b_ada):
    return pl.pallas_call(
        _ada_kernel,
        grid=(N_ADA // ADA_PER_STEP,),
        in_specs=[
            pl.BlockSpec(c_ctx.shape, lambda j: (0, 0)),
            pl.BlockSpec(c.shape, lambda j: (0, 0)),
            pl.BlockSpec((D_MODEL, ADA_PER_STEP * D_MODEL), lambda j: (0, j)),
            pl.BlockSpec((1, ADA_PER_STEP * D_MODEL), lambda j: (0, j)),
        ],
        out_specs=pl.BlockSpec((ADA_PER_STEP, 8, D_MODEL), lambda j: (j, 0, 0)),
        out_shape=jax.ShapeDtypeStruct((N_ADA, 8, D_MODEL), F32),
        compiler_params=pltpu.CompilerParams(dimension_semantics=("arbitrary",)),
        name="ada",
    )(c_ctx, c, w_ada, b_ada)


def _transpose_cast_kernel(starts_ref, wt_ref, o_ref):
    o_ref[...] = wt_ref[...].astype(BF16).T


def _transpose_cast(w_t, row_starts):
    n, k = len(row_starts), w_t.shape[1]
    return pl.pallas_call(
        _transpose_cast_kernel,
        grid_spec=pltpu.PrefetchScalarGridSpec(
            num_scalar_prefetch=1, grid=(n,),
            in_specs=[pl.BlockSpec((pl.Element(WPREP_ROWS), pl.Element(k)), lambda j, starts: (starts[j] * 8, 0))],
            out_specs=pl.BlockSpec((k, WPREP_ROWS), lambda j, starts: (0, j)),
        ),
        out_shape=jax.ShapeDtypeStruct((k, n * WPREP_ROWS), BF16),
        compiler_params=pltpu.CompilerParams(dimension_semantics=("arbitrary",)),
        name="transpose_cast",
    )(jnp.array([r // 8 for r in row_starts], jnp.int32), w_t)


WROW_OFFSET = {"wq": 2 * D_CONV, "wv": 2 * D_CONV + D_MLSTM, "wog": 2 * D_CONV + 2 * D_MLSTM,
               "wgm": 2 * D_CONV + 3 * D_MLSTM}
BROW_K_OFFSET = 2 * D_CONV + 3 * D_MLSTM + 2 * D_MODEL

_MIXER_WEIGHTS = (
    "g1", "wrow", "brow", "wkT", "wgifT", "bgifT", "wdw", "bdw", "lng", "lnb",
    "wco", "hng", "wmo", "wo", "g2", "wrt2", "brtT",
)


def _zero_after(x):
    bits = lax.bitcast_convert_type(x, jnp.uint32)
    bits = lax.shift_right_logical(lax.shift_right_logical(bits, jnp.uint32(16)), jnp.uint32(16))
    return lax.bitcast_convert_type(bits, F32)[0:1, :]


def _conv_block(upad_s, seg, base, cs, wdw_ref, bdw_ref, after=None):
    sub = 8
    first = CONV_PAD - CONV_K // 2
    acc = jnp.broadcast_to(bdw_ref[0:1, cs], (CONV_RB, LANES))
    for r in range(sub):
        z = None
        for a in range((CONV_K + first + sub - 1) // sub):
            j = sub * a + r - first
            if 0 <= j < CONV_K:
                lo = base + sub * a
                tap = wdw_ref[j:j + 1, cs] if after is None else wdw_ref[j:j + 1, cs] + after
                term = tap * upad_s[seg, lo:lo + CONV_RB + sub, cs]
                z = term if z is None else z + term
        acc = acc + z[r:r + CONV_RB, :]
    return acc


def _mixer_kernel(R, T, P, has_state, emit_state, mod_index, *refs):
    L = SUB
    n_mt = R // MIX_TM
    cpm = MIX_TM // L
    n_seq = R // T
    cps = T // L
    nseg = MIX_TM // P
    assert not has_state or n_seq == 1
    it = iter(refs)
    x_ref = next(it)
    mod_ref = next(it)
    if has_state:
        c0_ref = next(it)
        n0_ref = next(it)
        m0_ref = next(it)
    w = {name: next(it) for name in _MIXER_WEIGHTS}
    x1_ref = next(it)
    h2_ref = next(it)
    comb_ref = next(it)
    route_ref = next(it)
    cnt_ref = next(it)
    if emit_state:
        cout_ref = next(it)
        nout_ref = next(it)
        mout_ref = next(it)
    (q_s, kT_s, v_s, so_s, scan_s, ma_s, sgb_s, hm_s, cst_s, upad_s) = [next(it) for _ in range(10)]

    cond_row = mod_index(pl.program_id(0))

    def mod_row(i):
        return mod_ref[i, pl.ds(cond_row, 1), :]

    zpad = jnp.zeros((CONV_PAD, D_CONV), F32)
    for seg in range(nseg):
        upad_s[seg, 0:CONV_PAD, :] = zpad
        upad_s[seg, CONV_PAD + P:CONV_PAD + P + CONV_PAD, :] = zpad

    t_idx = lax.broadcasted_iota(jnp.int32, (L, L), 0)
    s_idx = lax.broadcasted_iota(jnp.int32, (L, L), 1)
    lower = s_idx <= t_idx
    upper = s_idx >= t_idx
    triu_b = upper.astype(F32).astype(BF16)
    lane_u = lax.broadcasted_iota(jnp.int32, (N_UNITS, L), 1)
    is_bwd = lax.broadcasted_iota(jnp.int32, (N_UNITS, L), 0) >= N_HEADS

    def gate_scan(g):
        gi, lf = g[:N_UNITS], _log_sigmoid(g[N_UNITS:])
        pr = _dot(jnp.concatenate(_split3(lf), axis=0).astype(BF16), triu_b)
        pre = pr[0:N_UNITS] + pr[N_UNITS:2 * N_UNITS] + pr[2 * N_UNITS:]
        tot = pre[:, L - 1:L]
        bsum = jnp.where(is_bwd, tot - pre + lf, pre)
        a = gi - bsum
        pm, sm, k = a, a, 1
        while k < L:
            pm = jnp.where(lane_u >= k, jnp.maximum(pm, pltpu.roll(pm, k, axis=1)), pm)
            sm = jnp.where(lane_u < L - k, jnp.maximum(sm, pltpu.roll(sm, L - k, axis=1)), sm)
            k *= 2
        wide = lambda v: jnp.broadcast_to(v, (N_UNITS, L))
        return jnp.concatenate([a, jnp.where(is_bwd, sm, pm), bsum, wide(tot),
                                wide(jnp.max(a, axis=1, keepdims=True))], axis=0)

    def phase1(i, carry):
        r0 = pl.multiple_of(i * MIX_TM, MIX_TM)
        rows = pl.ds(r0, MIX_TM)
        x = x_ref[0, rows, :]
        xn = x * lax.rsqrt(jnp.mean(x * x, axis=-1, keepdims=True) + EPS) * w["g1"][...]
        hb = (xn * (1.0 + mod_row(1)) + mod_row(0)).astype(BF16)

        gates = _dot_nt(w["wgifT"][...].astype(BF16), hb)
        gates = jnp.concatenate([gates[d * 2 * N_HEADS + g * N_HEADS:d * 2 * N_HEADS + (g + 1) * N_HEADS]
                                 for g in range(2) for d in range(2)], axis=0) + w["bgifT"][...]
        for j in range(cpm):
            scan_s[i * cpm + j] = gate_scan(gates[:, j * L:(j + 1) * L])
        ag = _dot(hb, w["wrow"][:, :2 * D_CONV]) + w["brow"][:, :2 * D_CONV]
        u = ag[:, :D_CONV] * _sigmoid(ag[:, D_CONV:])
        for seg in range(nseg):
            upad_s[seg, CONV_PAD:CONV_PAD + P, :] = u[seg * P:(seg + 1) * P, :]

        def proj(name, c0, gate, width=2 * LANES):
            w0 = WROW_OFFSET[name] + c0
            b = w["brow"][:, w0:w0 + width]
            if gate is not None:
                b = b + jnp.concatenate([gate] * (width // LANES), axis=1)
            return _dot(hb, w["wrow"][:, w0:w0 + width]) + b

        last = lambda z: z[-8:, -LANES:]
        bk_row = w["brow"][:, BROW_K_OFFSET:BROW_K_OFFSET + D_MLSTM]
        bk_col = jnp.concatenate([bk_row, jnp.zeros((LANES - 1, D_MLSTM), F32)], axis=0).T[:, 0:1]

        def gm_a(c0, gate):
            z = proj("wgm", c0, gate)
            ma_s[rows, c0:c0 + 2 * LANES] = _sigmoid(z)
            return last(z)

        def gm_b(c0, gate):
            z = proj("wgm", D_MODEL + c0, gate)
            sgb_s[rows, c0:c0 + 2 * LANES] = _sigmoid(z)
            return last(z)

        def q_part(c0, gate):
            z = proj("wq", c0, gate)
            q_s[rows, c0:c0 + 2 * LANES] = (z * (HEAD_DIM ** -0.5)).astype(BF16)
            return last(z)

        def v_part(c0, gate):
            z = proj("wv", c0, gate)
            v_s[rows, c0:c0 + 2 * LANES] = z.astype(BF16)
            return last(z)

        def o_part(c0, gate):
            z = proj("wog", c0, gate)
            so_s[rows, c0:c0 + 2 * LANES] = _sigmoid(z)
            return last(z)

        def k_part(c0, gate):
            rs = slice(c0, c0 + 2 * LANES)
            b = bk_col[rs, :] if gate is None else bk_col[rs, :] + gate[:, 0:1]
            z = _dot_nt(w["wkT"][rs, :].astype(BF16), hb) + b
            kt = z.astype(BF16)
            for j in range(cpm):
                kT_s[i * cpm + j, rs, :] = kt[:, j * L:(j + 1) * L]
            return last(z)

        jobs = ([functools.partial(gm_a, c0) for c0 in range(0, D_MODEL, 2 * LANES)]
                + [functools.partial(gm_b, c0) for c0 in range(0, D_MODEL, 2 * LANES)]
                + [functools.partial(f, c0) for f in (q_part, k_part, v_part, o_part)
                   for c0 in range(0, D_MLSTM, 2 * LANES)])
        n_jobs = len(jobs)
        conv = {}
        after = None
        n_pieces = (D_CONV // LANES) * nseg * (P // CONV_RB)
        for cb in range(D_CONV // LANES):
            cs = slice(cb * LANES, (cb + 1) * LANES)
            for seg in range(nseg):
                for rb in range(P // CONV_RB):
                    blk = _conv_block(upad_s, seg, rb * CONV_RB, cs, w["wdw"], w["bdw"], after)
                    conv[(cb, seg, rb)] = blk
                    if jobs and len(conv) * n_jobs >= (n_jobs - len(jobs) + 1) * n_pieces:
                        after = _zero_after(jobs.pop(0)(_zero_after(blk[-8:, :])))
        for job in jobs:
            job(None)
        cu = jnp.concatenate(
            [jnp.concatenate([conv[(cb, seg, rb)] for seg in range(nseg) for rb in range(P // CONV_RB)], axis=0)
             for cb in range(D_CONV // LANES)], axis=1)
        mu = jnp.mean(cu, axis=-1, keepdims=True)
        cc = cu - mu
        cn = cc * lax.rsqrt(jnp.mean(cc * cc, axis=-1, keepdims=True) + EPS) * w["lng"][...] + w["lnb"][...]
        ca = (cn * _sigmoid(cn)).astype(BF16)
        ma_s[rows, :] = ma_s[rows, :] * _dot(ca, w["wco"][...])
        return carry

    if n_mt == 1:
        phase1(0, 0)
    else:
        lax.fori_loop(0, n_mt, phase1, 0)

    ones_col = (lax.broadcasted_iota(jnp.int32, (L, HEAD_DIM), 1) == 0).astype(F32).astype(BF16)
    pad_rows = jnp.zeros((LANES - 3 * N_UNITS, L), F32)

    def gate_prep(c, m_vec):
        sc = scan_s[c]
        a, run_max, bsum = sc[0:N_UNITS], sc[N_UNITS:2 * N_UNITS], sc[2 * N_UNITS:3 * N_UNITS]
        tot, a_max = sc[3 * N_UNITS:4 * N_UNITS, 0:1], sc[4 * N_UNITS:5 * N_UNITS, 0:1]
        big_m = jnp.maximum(m_vec, run_max)
        m_end = jnp.maximum(m_vec, a_max)
        cols = jnp.concatenate(
            [big_m, jnp.exp(m_vec - big_m), jnp.exp(-bsum - big_m), pad_rows], axis=0).T
        return a, cols, jnp.exp(a - m_end), jnp.exp(m_vec - m_end), tot + m_end

    def unit_group(dirs, c, prep, first_chunk, want_state):
        a, cols, wk, decay, _ = prep
        rows = slice(c * L, (c + 1) * L)
        heads = range(N_HEADS)
        units = [(d, hd) for d in dirs for hd in heads]
        hs = [slice(hd * HEAD_DIM, (hd + 1) * HEAD_DIM) for hd in heads]
        idx = {u: u[0] * N_HEADS + u[1] for u in units}
        col = lambda k, u: cols[:, k * N_UNITS + idx[u]:k * N_UNITS + idx[u] + 1]
        row = lambda arr, u: arr[idx[u]:idx[u] + 1, :]
        chained = has_state or not first_chunk
        qc = [q_s[rows, hs[hd]] for hd in heads]
        kTc = [kT_s[c, hs[hd], :] for hd in heads]
        vaug = [jnp.concatenate([v_s[rows, hs[hd]], ones_col], axis=1) for hd in heads]
        qk = [_dot(qc[hd], kTc[hd]) for hd in heads]
        s_mat = {u: (qk[u[1]] * jnp.where(lower if u[0] == 0 else upper, jnp.exp(row(a, u) - col(0, u)), 0.0)
                     ).astype(BF16) for u in units}
        nd = {u: _dot(s_mat[u], vaug[u[1]]) for u in units}
        if chained:
            nd = {u: nd[u] + col(1, u) * _dot(qc[u[1]], cst_s[idx[u]].astype(BF16)) for u in units}
        h = {u: nd[u][:, :HEAD_DIM] * (1.0 / jnp.maximum(jnp.abs(nd[u][:, HEAD_DIM:HEAD_DIM + 1]), col(2, u)))
             for u in units}
        for hd in heads:
            total = h[(dirs[0], hd)]
            for d in dirs[1:]:
                total = total + h[(d, hd)]
            if dirs[0] == 0:
                hm_s[rows, hs[hd]] = total
            else:
                hm_s[rows, hs[hd]] = hm_s[rows, hs[hd]] + total
        if want_state:
            kw = {u: (kTc[u[1]].astype(F32) * row(wk, u)).astype(BF16) for u in units}
            upd = {u: _dot(kw[u], vaug[u[1]]) for u in units}
            for u in units:
                cst_s[idx[u]] = (upd[u] + row(decay, u) * cst_s[idx[u]]) if chained else upd[u]

    dir_rows = lax.broadcasted_iota(jnp.int32, (N_UNITS, 1), 0) >= N_HEADS
    for seq in range(n_seq):
        if has_state:
            n_cols = jnp.concatenate([n0_ref[0], jnp.zeros((LANES - N_UNITS, HEAD_DIM), F32)], axis=0).T
            first_lane = lax.broadcasted_iota(jnp.int32, (HEAD_DIM, HEAD_DIM), 1) == 0
            for idx in range(N_UNITS):
                cst_s[idx, :, :HEAD_DIM] = c0_ref[0, idx]
                cst_s[idx, :, HEAD_DIM:] = jnp.where(first_lane, n_cols[:, idx:idx + 1], 0.0)
            unit_row = lax.broadcasted_iota(jnp.int32, (N_UNITS, 1), 0)
            m_vec = jnp.zeros((N_UNITS, 1), F32)
            for idx in range(N_UNITS):
                m_vec = jnp.where(unit_row == idx, m0_ref[pl.program_id(0), idx], m_vec)
        else:
            m_vec = jnp.zeros((N_UNITS, 1), F32)
        if cps == 1:
            prep = gate_prep(seq, m_vec)
            unit_group([0, 1], seq, prep, True, emit_state)
            m_vec = prep[4]
        else:
            for d in range(2):
                order = list(range(cps)) if d == 0 else list(range(cps - 1, -1, -1))
                for pos, c in enumerate(order):
                    prep = gate_prep(seq * cps + c, m_vec)
                    unit_group([d], seq * cps + c, prep, pos == 0, emit_state or pos < cps - 1)
                    m_vec = jnp.where(dir_rows == (d == 1), prep[4], m_vec)
        if emit_state:
            for idx in range(N_UNITS):
                caug = cst_s[idx]
                cout_ref[0, seq * N_UNITS + idx] = caug[:, :HEAD_DIM]
                nout_ref[0, seq * N_UNITS + idx:seq * N_UNITS + idx + 1, :] = caug[:, HEAD_DIM:].T[0:1, :]
            mout_ref[0, seq * N_UNITS:(seq + 1) * N_UNITS, :] = jnp.broadcast_to(m_vec, (N_UNITS, LANES))

    e_iota = lax.broadcasted_iota(jnp.int32, (LANES, MIX_TM), 0)
    g_of_e = lax.shift_right_logical(e_iota, 2)
    j_of_e = lax.bitwise_and(e_iota, EXPERTS_PER_GROUP - 1)
    r8 = lax.broadcasted_iota(jnp.int32, (8, MIX_TM), 0)
    before_b = (lax.broadcasted_iota(jnp.int32, (MOE_BLK, MOE_BLK), 0)
                < lax.broadcasted_iota(jnp.int32, (MOE_BLK, MOE_BLK), 1)).astype(F32).astype(BF16)

    def phase3(i, carry):
        r0 = pl.multiple_of(i * MIX_TM, MIX_TM)
        rows = pl.ds(r0, MIX_TM)
        hm = hm_s[rows, :]
        heads = []
        for hd in range(N_HEADS):
            hh = hm[:, hd * HEAD_DIM:(hd + 1) * HEAD_DIM]
            heads.append(hh * lax.rsqrt(jnp.mean(hh * hh, axis=-1, keepdims=True) + EPS))
        hn = jnp.concatenate(heads, axis=1) * w["hng"][...]
        hb2 = (so_s[rows, :] * hn).astype(BF16)
        br_b = _dot(hb2, w["wmo"][...])
        mixed = (ma_s[rows, :] + sgb_s[rows, :] * br_b).astype(BF16)
        x1 = x_ref[0, rows, :] + mod_row(2) * _dot(mixed, w["wo"][...])
        x1_ref[0, rows, :] = x1
        xn = x1 * lax.rsqrt(jnp.mean(x1 * x1, axis=-1, keepdims=True) + EPS) * w["g2"][...]
        h2 = xn * (1.0 + mod_row(4)) + mod_row(3)
        h2_ref[0, rows, :] = h2.astype(BF16)

        h2_hi = h2.astype(BF16)
        h2_lo = (h2 - h2_hi.astype(F32)).astype(BF16)
        lg = _dot(h2_hi, w["wrt2"][...])
        lg = lg[:, :LANES] + lg[:, LANES:] + _dot(h2_lo, w["wrt2"][:, :LANES])
        lt = lg.T + w["brtT"][...]
        gl = [lt[N_EXPERTS + g:N_EXPERTS + g + 1, :] for g in range(N_GROUPS)]
        best, gsel = gl[0], jnp.zeros((1, MIX_TM), jnp.int32)
        for g in range(1, N_GROUPS):
            better = gl[g] > best
            gsel = jnp.where(better, g, gsel)
            best = jnp.where(better, gl[g], best)
        gp_sel = 1.0 / sum(jnp.exp(v - best) for v in gl)
        el = []
        for j in range(EXPERTS_PER_GROUP):
            v = lt[j:j + 1, :]
            for g in range(1, N_GROUPS):
                r = g * EXPERTS_PER_GROUP + j
                v = jnp.where(gsel == g, lt[r:r + 1, :], v)
            el.append(v)
        l1, e1 = el[0], jnp.zeros((1, MIX_TM), jnp.int32)
        for j in range(1, EXPERTS_PER_GROUP):
            better = el[j] > l1
            e1 = jnp.where(better, j, e1)
            l1 = jnp.where(better, el[j], l1)
        l2 = jnp.full((1, MIX_TM), -jnp.inf, F32)
        e2 = jnp.zeros((1, MIX_TM), jnp.int32)
        for j in range(EXPERTS_PER_GROUP):
            better = jnp.logical_and(e1 != j, el[j] > l2)
            e2 = jnp.where(better, j, e2)
            l2 = jnp.where(better, el[j], l2)
        r2 = jnp.exp(l2 - l1)
        wt1 = gp_sel / (1.0 + r2)
        wt2 = gp_sel * r2 / (1.0 + r2)
        in_group = g_of_e == gsel
        comb_t = (jnp.where(jnp.logical_and(in_group, j_of_e == e1), wt1, 0.0)
                  + jnp.where(jnp.logical_and(in_group, j_of_e == e2), wt2, 0.0))

        onehot = (r8 == gsel).astype(F32)
        gsel_f = gsel.astype(F32)
        rank = jnp.sum(onehot * _dot(onehot.astype(BF16), before_b), axis=0, keepdims=True)
        r8rows = pl.ds(pl.multiple_of(i * 8, 8), 8)
        route_ref[0, r8rows, :] = jnp.where(r8 == 0, gsel_f, jnp.where(r8 == 1, rank, 0.0))
        cnt_ref[0, r8rows, :] = jnp.broadcast_to(jnp.sum(onehot, axis=1, keepdims=True), (8, LANES))
        comb_t = jnp.where(e_iota == ROUTE_GROUP_LANE, gsel_f,
                           jnp.where(e_iota == ROUTE_RANK_LANE, rank, comb_t))
        comb_ref[0, rows, :] = comb_t.T
        return carry

    if n_mt == 1:
        phase3(0, 0)
    else:
        lax.fori_loop(0, n_mt, phase3, 0)


class _RowWindow(NamedTuple):
    array: jax.Array
    start: int
    n: int


def _const_spec(a):
    if isinstance(a, _RowWindow):
        assert a.start % a.n == 0
        return a.array, pl.BlockSpec((a.n, a.array.shape[1]), lambda b: (a.start // a.n, 0),
                                     pipeline_mode=pl.Buffered(1))
    nd = a.ndim
    return a, pl.BlockSpec(a.shape, lambda b, _nd=nd: (0,) * _nd, pipeline_mode=pl.Buffered(1))


def _mixer(x, T, mod, mod_index, weights, P, state=None, emit_state=False):
    B, R, _ = x.shape
    n_chunks = R // SUB
    n_blk = R // MOE_BLK
    n_seq = R // T
    has_state = state is not None
    seq_mode = {} if R <= MIX_TM else {"pipeline_mode": pl.Buffered(1)}
    in_specs = [
        pl.BlockSpec((1, R, D_MODEL), lambda b: (b, 0, 0), **seq_mode),
        pl.BlockSpec(mod.shape, lambda b: (0, 0, 0)),
    ]
    args = [x, mod]
    if has_state:
        c0, n0, m0 = state
        in_specs += [
            pl.BlockSpec((1, N_UNITS, HEAD_DIM, HEAD_DIM), lambda b: (b, 0, 0, 0)),
            pl.BlockSpec((1, N_UNITS, HEAD_DIM), lambda b: (b, 0, 0)),
            pl.BlockSpec(memory_space=pltpu.SMEM),
        ]
        args += [c0, n0, m0]
    for name in _MIXER_WEIGHTS:
        operand, spec = _const_spec(weights[name])
        in_specs.append(spec)
        args.append(operand)
    out_shape = [
        jax.ShapeDtypeStruct((B, R, D_MODEL), F32),
        jax.ShapeDtypeStruct((B, R, D_MODEL), BF16),
        jax.ShapeDtypeStruct((B, R, LANES), F32),
        jax.ShapeDtypeStruct((B, n_blk * 8, MOE_BLK), F32),
        jax.ShapeDtypeStruct((B, n_blk * 8, LANES), F32),
    ]
    out_specs = [
        pl.BlockSpec((1, R, D_MODEL), lambda b: (b, 0, 0), **seq_mode),
        pl.BlockSpec((1, R, D_MODEL), lambda b: (b, 0, 0), **seq_mode),
        pl.BlockSpec((1, R, LANES), lambda b: (b, 0, 0)),
        pl.BlockSpec((1, n_blk * 8, MOE_BLK), lambda b: (b, 0, 0)),
        pl.BlockSpec((1, n_blk * 8, LANES), lambda b: (b, 0, 0)),
    ]
    if emit_state:
        out_shape += [
            jax.ShapeDtypeStruct((B, n_seq * N_UNITS, HEAD_DIM, HEAD_DIM), F32),
            jax.ShapeDtypeStruct((B, n_seq * N_UNITS, HEAD_DIM), F32),
            jax.ShapeDtypeStruct((B, n_seq * N_UNITS, LANES), F32),
        ]
        out_specs += [
            pl.BlockSpec((1, n_seq * N_UNITS, HEAD_DIM, HEAD_DIM), lambda b: (b, 0, 0, 0)),
            pl.BlockSpec((1, n_seq * N_UNITS, HEAD_DIM), lambda b: (b, 0, 0)),
            pl.BlockSpec((1, n_seq * N_UNITS, LANES), lambda b: (b, 0, 0)),
        ]
    scratch = [
        pltpu.VMEM((R, D_MLSTM), BF16),
        pltpu.VMEM((n_chunks, D_MLSTM, SUB), BF16),
        pltpu.VMEM((R, D_MLSTM), BF16),
        pltpu.VMEM((R, D_MLSTM), F32),
        pltpu.VMEM((n_chunks, 5 * N_UNITS, SUB), F32),
        pltpu.VMEM((R, D_MODEL), F32),
        pltpu.VMEM((R, D_MODEL), F32),
        pltpu.VMEM((R, D_MLSTM), F32),
        pltpu.VMEM((N_UNITS, HEAD_DIM, 2 * HEAD_DIM), F32),
        pltpu.VMEM((MIX_TM // P, P + 2 * CONV_PAD, D_CONV), F32),
    ]
    return pl.pallas_call(
        functools.partial(_mixer_kernel, R, T, P, has_state, emit_state, mod_index),
        grid=(B,),
        in_specs=in_specs,
        out_specs=out_specs,
        out_shape=out_shape,
        scratch_shapes=scratch,
        compiler_params=pltpu.CompilerParams(
            dimension_semantics=("arbitrary",), vmem_limit_bytes=VMEM_LIMIT),
        name="mixer_T%d" % T,
    )(*args)


def _dest_in_block(group, rank, starts):
    dest = rank
    for g in range(N_GROUPS):
        dest = dest + jnp.where(group == float(g), starts[g], 0.0)
    return dest


def _copy_segments(src_refs, dst_refs, src_starts, dst_starts, n_pieces):
    def copy(g, first_piece, n_rows):
        s = pl.multiple_of(src_starts[g] + first_piece * ROW_ALIGN, ROW_ALIGN)
        d = pl.multiple_of(dst_starts[g] + first_piece * ROW_ALIGN, ROW_ALIGN)
        for src, dst in zip(src_refs, dst_refs):
            dst[pl.ds(d, n_rows), :] = src[pl.ds(s, n_rows), :]

    for g in range(N_GROUPS):
        n_runs = lax.shift_right_logical(n_pieces[g], COPY_RUN.bit_length() - 1)

        def run(k, carry, g=g):
            copy(g, k * COPY_RUN, COPY_RUN * ROW_ALIGN)
            return carry

        def single(k, carry, g=g):
            copy(g, k, ROW_ALIGN)
            return carry

        lax.fori_loop(0, n_runs, run, 0)
        lax.fori_loop(n_runs * COPY_RUN, n_pieces[g], single, 0)


def _plan_segments(n_blocks, n_tiles, count, start_ref, npiece_ref, off_ref, tgroup_ref, tvalid_ref):
    align_shift = ROW_ALIGN.bit_length() - 1
    tile_shift = MOE_TM.bit_length() - 1

    def block_starts(blk, carry):
        row = jnp.int32(0)
        for g in range(N_GROUPS):
            n = lax.shift_right_logical(count(blk, g) + (ROW_ALIGN - 1), align_shift)
            npiece_ref[blk * N_GROUPS + g] = n
            start_ref[blk * N_GROUPS + g] = row
            row = row + n * ROW_ALIGN
        return carry

    lax.fori_loop(0, n_blocks, block_starts, 0)

    base_row = jnp.int32(0)
    base_tile = jnp.int32(0)
    last_group = jnp.int32(0)
    for g in range(N_GROUPS):
        def seg_offsets(blk, row, g=g, base_row=base_row):
            off_ref[blk * N_GROUPS + g] = base_row + row
            return row + npiece_ref[blk * N_GROUPS + g] * ROW_ALIGN

        rows = lax.fori_loop(0, n_blocks, seg_offsets, jnp.int32(0))
        tiles = lax.shift_right_logical(rows + (MOE_TM - 1), tile_shift)

        def mark_tiles(t, carry, g=g, base_tile=base_tile):
            tgroup_ref[base_tile + t] = g
            tvalid_ref[base_tile + t] = 1
            return carry

        lax.fori_loop(0, tiles, mark_tiles, 0)
        last_group = jnp.where(tiles > 0, g, last_group)
        base_row = base_row + tiles * MOE_TM
        base_tile = base_tile + tiles

    def mark_unused(t, carry):
        tgroup_ref[t] = last_group
        tvalid_ref[t] = 0
        return carry

    lax.fori_loop(base_tile, n_tiles, mark_unused, 0)


def _dispatch_kernel(n_ctx_blocks, n_blocks, n_tiles,
                     h2c_ref, h2l_ref, cbc_ref, cbl_ref, rtc_ref, rtl_ref, cntc_ref, cntl_ref,
                     xs_ref, cs_ref, start_ref, npiece_ref, off_ref, tgroup_ref, tvalid_ref,
                     sx_s, sc_s):
    b = pl.program_id(0)
    is_ctx = b < n_ctx_blocks

    def count(blk, g):
        vc = cntc_ref[jnp.minimum(blk, n_ctx_blocks - 1), pl.ds(g, 1), pl.ds(0, 1)]
        vl = cntl_ref[jnp.maximum(blk - n_ctx_blocks, 0), pl.ds(g, 1), pl.ds(0, 1)]
        return jnp.where(blk < n_ctx_blocks, vc, vl)[0, 0].astype(jnp.int32)

    @pl.when(b == 0)
    def _():
        _plan_segments(n_blocks, n_tiles, count, start_ref, npiece_ref, off_ref, tgroup_ref, tvalid_ref)
        xs_ref[...] = jnp.zeros_like(xs_ref)
        cs_ref[...] = jnp.zeros_like(cs_ref)

    starts = [start_ref[b * N_GROUPS + g] for g in range(N_GROUPS)]

    def sort_block(h2_ref, cb_ref, rt_ref):
        h2 = h2_ref[0]
        cb = cb_ref[0]
        rt = rt_ref[0]
        dest = _dest_in_block(rt[0:1, :], rt[1:2, :], [s.astype(F32) for s in starts])
        row = lax.broadcasted_iota(jnp.int32, (SORT_ROWS, MOE_BLK), 0).astype(F32)
        perm = (row == dest).astype(F32).astype(BF16)
        cb_hi = cb.astype(BF16)
        cb_lo = (cb - cb_hi.astype(F32)).astype(BF16)
        sx_s[...] = _dot(perm, h2).astype(BF16)
        sc_s[...] = _dot(perm, jnp.concatenate([cb_hi, cb_lo], axis=1)).astype(BF16)

    pl.when(is_ctx)(functools.partial(sort_block, h2c_ref, cbc_ref, rtc_ref))
    pl.when(jnp.logical_not(is_ctx))(functools.partial(sort_block, h2l_ref, cbl_ref, rtl_ref))
    _copy_segments((sx_s, sc_s), (xs_ref, cs_ref), starts,
                   [off_ref[b * N_GROUPS + g] for g in range(N_GROUPS)],
                   [npiece_ref[b * N_GROUPS + g] for g in range(N_GROUPS)])


def _experts_kernel(tgroup_ref, tvalid_ref, xs_ref, cs_ref, wg_ref, wu_ref, wd_ref, ys_ref):
    i = pl.program_id(0)

    @pl.when(tvalid_ref[i] == 1)
    def _():
        x = xs_ref[...]
        comb = cs_ref[:, :LANES].astype(F32) + cs_ref[:, LANES:].astype(F32)
        lane = lax.broadcasted_iota(jnp.int32, comb.shape, 1)
        first = tgroup_ref[i] * EXPERTS_PER_GROUP
        acc = None
        for j in range(EXPERTS_PER_GROUP):
            gj = _dot(x, wg_ref[j].astype(BF16))
            uj = _dot(x, wu_ref[j].astype(BF16))
            cw = jnp.sum(jnp.where(lane == first + j, comb, 0.0), axis=1, keepdims=True)
            out = _dot((gj * _sigmoid(gj) * uj * cw).astype(BF16), wd_ref[j].astype(BF16))
            acc = out if acc is None else acc + out
        ys_ref[...] = acc.astype(BF16)

    @pl.when(tvalid_ref[i] == 0)
    def _():
        ys_ref[...] = jnp.zeros_like(ys_ref)


def _combine_kernel(n_ctx_blocks, blocks_per_lat_seq, start_ref, npiece_ref, off_ref,
                    x1c_ref, x1l_ref, cbc_ref, cbl_ref, ys_ref, mod_ref, gf_ref, yc_ref, yl_ref, loc_s):
    b = pl.program_id(0)
    is_ctx = b < n_ctx_blocks
    starts = [start_ref[b * N_GROUPS + g] for g in range(N_GROUPS)]
    loc_s[...] = jnp.zeros_like(loc_s)
    _copy_segments((ys_ref,), (loc_s,), [off_ref[b * N_GROUPS + g] for g in range(N_GROUPS)], starts,
                   [npiece_ref[b * N_GROUPS + g] for g in range(N_GROUPS)])
    def finish_block(x1_ref, cb_ref, y_ref, mrow):
        cb = cb_ref[0]
        dest = _dest_in_block(cb[:, ROUTE_GROUP_LANE:ROUTE_GROUP_LANE + 1],
                              cb[:, ROUTE_RANK_LANE:ROUTE_RANK_LANE + 1],
                              [s.astype(F32) for s in starts])
        col = lax.broadcasted_iota(jnp.int32, (MOE_BLK, SORT_ROWS), 1).astype(F32)
        unperm = (col == dest).astype(F32).astype(BF16)
        x2 = x1_ref[0] + mod_ref[N_ADA - 1, pl.ds(mrow, 1), :] * _dot(unperm, loc_s[...])
        y_ref[0] = x2 * lax.rsqrt(jnp.mean(x2 * x2, axis=-1, keepdims=True) + EPS) * gf_ref[...]

    lat_row = 1 + jnp.maximum(b - n_ctx_blocks, 0) // blocks_per_lat_seq
    pl.when(is_ctx)(functools.partial(finish_block, x1c_ref, cbc_ref, yc_ref, 0))
    pl.when(jnp.logical_not(is_ctx))(functools.partial(finish_block, x1l_ref, cbl_ref, yl_ref, lat_row))


def _moe(x1c, x1l, h2c, h2l, cbc, cbl, rtc, rtl, cntc, cntl, mod, blocks_per_lat_seq, wg, wu, wd, gf):
    nc, nl = x1c.shape[0], x1l.shape[0]
    nb = nc + nl
    n_rows_max = nb * MOE_BLK + nb * N_GROUPS * (ROW_ALIGN - 1) + N_GROUPS * (MOE_TM - ROW_ALIGN)
    n_tiles = -(-n_rows_max // MOE_TM)
    ns = n_tiles * MOE_TM

    cmap = lambda b, *_: (jnp.minimum(b, nc - 1), 0, 0)
    lmap = lambda b, *_: (jnp.maximum(b - nc, 0), 0, 0)
    whole = lambda *_: (0, 0)
    once = {"pipeline_mode": pl.Buffered(1)}
    arb = pltpu.CompilerParams(dimension_semantics=("arbitrary",), vmem_limit_bytes=VMEM_LIMIT)
    smem = pl.BlockSpec(memory_space=pltpu.SMEM)
    seg_i32 = jax.ShapeDtypeStruct((nb * N_GROUPS,), jnp.int32)
    tile_i32 = jax.ShapeDtypeStruct((n_tiles,), jnp.int32)

    xs, cs, start, npiece, off, tgroup, tvalid = pl.pallas_call(
        functools.partial(_dispatch_kernel, nc, nb, n_tiles),
        grid_spec=pltpu.PrefetchScalarGridSpec(
            num_scalar_prefetch=0, grid=(nb,),
            in_specs=[
                pl.BlockSpec((1, MOE_BLK, D_MODEL), cmap), pl.BlockSpec((1, MOE_BLK, D_MODEL), lmap),
                pl.BlockSpec((1, MOE_BLK, LANES), cmap), pl.BlockSpec((1, MOE_BLK, LANES), lmap),
                pl.BlockSpec((1, 8, MOE_BLK), cmap), pl.BlockSpec((1, 8, MOE_BLK), lmap),
                pl.BlockSpec(cntc.shape, lambda b: (0, 0, 0)), pl.BlockSpec(cntl.shape, lambda b: (0, 0, 0)),
            ],
            out_specs=[pl.BlockSpec((ns, D_MODEL), whole, **once), pl.BlockSpec((ns, 2 * LANES), whole, **once),
                       smem, smem, smem, smem, smem],
            scratch_shapes=[pltpu.VMEM((SORT_ROWS, D_MODEL), BF16), pltpu.VMEM((SORT_ROWS, 2 * LANES), BF16)],
        ),
        out_shape=[jax.ShapeDtypeStruct((ns, D_MODEL), BF16), jax.ShapeDtypeStruct((ns, 2 * LANES), BF16),
                   seg_i32, seg_i32, seg_i32, tile_i32, tile_i32],
        compiler_params=arb,
        name="moe_dispatch",
    )(h2c, h2l, cbc, cbl, rtc, rtl, cntc, cntl)

    wmap = lambda i, tg, tv: (tg[i], 0, 0)
    ys = pl.pallas_call(
        _experts_kernel,
        grid_spec=pltpu.PrefetchScalarGridSpec(
            num_scalar_prefetch=2, grid=(n_tiles,),
            in_specs=[
                pl.BlockSpec((MOE_TM, D_MODEL), lambda i, *_: (i, 0)),
                pl.BlockSpec((MOE_TM, 2 * LANES), lambda i, *_: (i, 0)),
                pl.BlockSpec((EXPERTS_PER_GROUP, D_MODEL, D_EXPERT), wmap),
                pl.BlockSpec((EXPERTS_PER_GROUP, D_MODEL, D_EXPERT), wmap),
                pl.BlockSpec((EXPERTS_PER_GROUP, D_EXPERT, D_MODEL), wmap),
            ],
            out_specs=pl.BlockSpec((MOE_TM, D_MODEL), lambda i, *_: (i, 0)),
        ),
        out_shape=jax.ShapeDtypeStruct((ns, D_MODEL), BF16),
        compiler_params=arb,
        name="moe_experts",
    )(tgroup, tvalid, xs, cs, wg, wu, wd)

    yc, yl = pl.pallas_call(
        functools.partial(_combine_kernel, nc, blocks_per_lat_seq),
        grid_spec=pltpu.PrefetchScalarGridSpec(
            num_scalar_prefetch=3, grid=(nb,),
            in_specs=[
                pl.BlockSpec((1, MOE_BLK, D_MODEL), cmap), pl.BlockSpec((1, MOE_BLK, D_MODEL), lmap),
                pl.BlockSpec((1, MOE_BLK, LANES), cmap), pl.BlockSpec((1, MOE_BLK, LANES), lmap),
                pl.BlockSpec((ns, D_MODEL), whole, **once),
                pl.BlockSpec(mod.shape, lambda *_: (0, 0, 0)),
                pl.BlockSpec((1, D_MODEL), whole),
            ],
            out_specs=[pl.BlockSpec((1, MOE_BLK, D_MODEL), cmap), pl.BlockSpec((1, MOE_BLK, D_MODEL), lmap)],
            scratch_shapes=[pltpu.VMEM((SORT_ROWS, D_MODEL), BF16)],
        ),
        out_shape=[jax.ShapeDtypeStruct((nc, MOE_BLK, D_MODEL), F32),
                   jax.ShapeDtypeStruct((nl, MOE_BLK, D_MODEL), F32)],
        compiler_params=arb,
        name="moe_combine",
    )(start, npiece, off, x1c, x1l, cbc, cbl, ys, mod, gf)
    return yc, yl


def _prep_weights(norm1_g, w_in, b_in, b_gates, w_dw, b_dw, conv_ln_g, conv_ln_b, w_conv_out,
                  mlstm_hn_g, w_mlstm_out, w_o, norm2_g, w_rg, b_rg, w_re, b_re):
    s_a = 2 * D_CONV
    s_q = s_a + D_MLSTM
    s_k = s_q + D_MLSTM
    s_v = s_k + D_MLSTM
    s_o = s_v + D_MLSTM
    s_g = s_o + 4 * N_HEADS
    row = lambda v: v.reshape(1, -1).astype(F32)
    w_t = w_in.T
    keep = [(0, s_q), (s_k, s_o), (s_g, w_in.shape[1])]
    wrow = _transpose_cast(w_t, [r for a, b in keep for r in range(a, b, WPREP_ROWS)])
    bg = (b_in[s_o:s_g] + b_gates.reshape(-1)).reshape(2, 2, N_HEADS).transpose(1, 0, 2).reshape(-1, 1)
    row_window = lambda start, n: _RowWindow(w_t, start, n)
    n_rt = N_EXPERTS + N_GROUPS
    wrt = jnp.pad(jnp.concatenate([w_re, w_rg], axis=1), ((0, 0), (0, LANES - n_rt)))
    wrt_hi = wrt.astype(BF16)
    wrt2 = jnp.concatenate([wrt_hi, (wrt - wrt_hi.astype(F32)).astype(BF16)], axis=1)
    brtT = jnp.pad(jnp.concatenate([b_re, b_rg]), (0, LANES - n_rt)).reshape(LANES, 1)
    return {
        "g1": row(norm1_g),
        "wrow": wrow, "brow": row(jnp.concatenate([b_in[a:b] for a, b in keep] + [b_in[s_q:s_k]])),
        "wkT": row_window(s_q, D_MLSTM), "wgifT": row_window(s_o, 4 * N_HEADS), "bgifT": bg,
        "wdw": w_dw.astype(F32), "bdw": row(b_dw), "lng": row(conv_ln_g), "lnb": row(conv_ln_b),
        "wco": w_conv_out.astype(BF16), "hng": row(mlstm_hn_g), "wmo": w_mlstm_out.astype(BF16),
        "wo": w_o.astype(BF16), "g2": row(norm2_g), "wrt2": wrt2, "brtT": brtT,
    }


def kernel(x_prompt, x_sample, state_C, state_n, state_m, c, c_ctx, norm1_g, w_ada, b_ada, w_in, b_in, b_gates, w_dw, b_dw, conv_ln_g, conv_ln_b, w_conv_out, mlstm_hn_g, w_mlstm_out, w_o, norm2_g, w_rg, b_rg, w_re, b_re, w_e_gate, w_e_up, w_e_down, norm_final_g):
    B, S, _ = x_prompt.shape
    Bd, Sd, _ = x_sample.shape
    assert w_ada.shape[0] == 1, "single trunk layer"
    assert MIX_TM % S == 0 and S % SUB == 0 and Sd % MIX_TM == 0

    mod = _ada(c_ctx.reshape(1, -1), c, w_ada[0], b_ada[0].reshape(1, -1))

    wts = _prep_weights(norm1_g[0], w_in[0], b_in[0], b_gates[0], w_dw[0], b_dw[0], conv_ln_g[0],
                        conv_ln_b[0], w_conv_out[0], mlstm_hn_g[0], w_mlstm_out[0], w_o[0],
                        norm2_g[0], w_rg[0], b_rg[0], w_re[0], b_re[0])

    x1p, h2p, cbp, rtp, cntp, c_new, n_new, m_new = _mixer(
        x_prompt.reshape(B * S // MIX_TM, MIX_TM, D_MODEL), S, mod, lambda b: 0, wts, P=S, emit_state=True)

    state = (state_C[:, 0].reshape(Bd, N_UNITS, HEAD_DIM, HEAD_DIM), state_n[:, 0].reshape(Bd, N_UNITS, HEAD_DIM),
             state_m[:, 0].reshape(Bd, N_UNITS))
    x1s, h2s, cbs, rts, cnts = _mixer(x_sample, Sd, mod, lambda b: 1 + b, wts, P=GRID_W, state=state)

    nc, nl = B * S // MOE_BLK, Bd * Sd // MOE_BLK
    blk = lambda a, n: a.reshape(n, MOE_BLK, a.shape[-1])
    yp, ys = _moe(blk(x1p, nc), blk(x1s, nl), blk(h2p, nc), blk(h2s, nl), blk(cbp, nc), blk(cbs, nl),
                  rtp.reshape(nc, 8, MOE_BLK), rts.reshape(nl, 8, MOE_BLK),
                  cntp.reshape(nc, 8, LANES), cnts.reshape(nl, 8, LANES),
                  mod, Sd // MOE_BLK, w_e_gate[0], w_e_up[0], w_e_down[0], norm_final_g.reshape(1, -1))

    return (yp.reshape(B, S, D_MODEL), ys.reshape(Bd, Sd, D_MODEL),
            c_new.reshape(B, 1, 2, N_HEADS, HEAD_DIM, HEAD_DIM),
            n_new.reshape(B, 1, 2, N_HEADS, HEAD_DIM),
            m_new[:, :, 0].reshape(B, 1, 2, N_HEADS))
```

```python
import functools
from typing import NamedTuple

import jax
import jax.numpy as jnp
from jax import lax
from jax.experimental import pallas as pl
from jax.experimental.pallas import tpu as pltpu

D_MODEL = 1024
D_CONV = 512
CONV_K = 31
D_MLSTM = 512
N_HEADS = 4
HEAD_DIM = D_MLSTM // N_HEADS
N_GROUPS = 4
EXPERTS_PER_GROUP = 4
N_EXPERTS = N_GROUPS * EXPERTS_PER_GROUP
D_EXPERT = 256
N_ADA = 6
EPS = 1e-6
GRID_W = 64

LANES = 128
SUB = 256
CONV_PAD = 16
CONV_RB = 64
N_UNITS = 2 * N_HEADS
ROW_ALIGN = 16
COPY_RUN = 4
MOE_TM = 512
MIX_TM = 512
MOE_BLK = MIX_TM
SORT_ROWS = MOE_BLK + N_GROUPS * ROW_ALIGN
ADA_PER_STEP = 2
WPREP_ROWS = 512
ROUTE_GROUP_LANE = N_EXPERTS
ROUTE_RANK_LANE = N_EXPERTS + 1
VMEM_LIMIT = 58 * 1024 * 1024

BF16 = jnp.bfloat16
F32 = jnp.float32
NT_DIMS = (((1,), (1,)), ((), ()))


def _dot(a, b):
    return jnp.dot(a, b, preferred_element_type=F32)


def _dot_nt(a, b, precision=None):
    return lax.dot_general(a, b, NT_DIMS, preferred_element_type=F32, precision=precision)


def _sigmoid(x):
    return 0.5 * jnp.tanh(0.5 * x) + 0.5


def _sigmoid_of_half(xh):
    return 0.5 * jnp.tanh(xh) + 0.5


def _log_sigmoid(x):
    return jnp.minimum(x, 0.0) - jnp.log1p(jnp.exp(-jnp.abs(x)))


def _split3(x):
    hi = x.astype(BF16).astype(F32)
    r1 = x - hi
    mid = r1.astype(BF16).astype(F32)
    lo = (r1 - mid).astype(BF16).astype(F32)
    return hi, mid, lo


def _ada_kernel(cctx_ref, c_ref, w_ref, b_ref, o_ref):
    n = 1 + c_ref.shape[0]
    c = jnp.concatenate([cctx_ref[...], c_ref[...], jnp.zeros((8 - n, D_MODEL), F32)], axis=0)
    s = (c * _sigmoid(c)).astype(BF16)
    out = _dot(s, w_ref[...].astype(BF16)) + b_ref[...]
    for v in range(ADA_PER_STEP):
        o_ref[v] = out[:, v * D_MODEL:(v + 1) * D_MODEL]


def _ada(c_ctx, c, w_ada, b_ada):
    return pl.pallas_call(
        _ada_kernel,
        grid=(N_ADA // ADA_PER_STEP,),
        in_specs=[
            pl.BlockSpec(c_ctx.shape, lambda j: (0, 0)),
            pl.BlockSpec(c.shape, lambda j: (0, 0)),
            pl.BlockSpec((D_MODEL, ADA_PER_STEP * D_MODEL), lambda j: (0, j)),
            pl.BlockSpec((1, ADA_PER_STEP * D_MODEL), lambda j: (0, j)),
        ],
        out_specs=pl.BlockSpec((ADA_PER_STEP, 8, D_MODEL), lambda j: (j, 0, 0)),
        out_shape=jax.ShapeDtypeStruct((N_ADA, 8, D_MODEL), F32),
        compiler_params=pltpu.CompilerParams(dimension_semantics=("arbitrary",)),
        name="ada",
    )(c_ctx, c, w_ada, b_ada)


def _transpose_cast_kernel(starts_ref, halve_ref, wt_ref, o_ref):
    scale = jnp.where(halve_ref[pl.program_id(0)] == 1, 0.5, 1.0)
    o_ref[...] = (wt_ref[...] * scale).astype(BF16).T


def _transpose_cast(w_t, row_starts, halve):
    n, k = len(row_starts), w_t.shape[1]
    return pl.pallas_call(
        _transpose_cast_kernel,
        grid_spec=pltpu.PrefetchScalarGridSpec(
            num_scalar_prefetch=2, grid=(n,),
            in_specs=[pl.BlockSpec((pl.Element(WPREP_ROWS), pl.Element(k)), lambda j, starts, hv: (starts[j] * 8, 0))],
            out_specs=pl.BlockSpec((k, WPREP_ROWS), lambda j, starts, hv: (0, j)),
        ),
        out_shape=jax.ShapeDtypeStruct((k, n * WPREP_ROWS), BF16),
        compiler_params=pltpu.CompilerParams(dimension_semantics=("arbitrary",)),
        name="transpose_cast",
    )(jnp.array([r // 8 for r in row_starts], jnp.int32), jnp.array([int(h) for h in halve], jnp.int32), w_t)


WROW_OFFSET = {"wq": 2 * D_CONV, "wv": 2 * D_CONV + D_MLSTM, "wog": 2 * D_CONV + 2 * D_MLSTM,
               "wgm": 2 * D_CONV + 3 * D_MLSTM}
BROW_K_OFFSET = 2 * D_CONV + 3 * D_MLSTM + 2 * D_MODEL

_MIXER_WEIGHTS = (
    "g1", "wrow", "brow", "wkT", "wgifT", "bgifT", "wdw", "bdw", "lng", "lnb",
    "wco", "hng", "wmo", "wo", "g2", "wrt2", "brtT",
)


def _zero_after(x):
    bits = lax.bitcast_convert_type(x, jnp.uint32)
    bits = lax.shift_right_logical(lax.shift_right_logical(bits, jnp.uint32(16)), jnp.uint32(16))
    return lax.bitcast_convert_type(bits, F32)[0:1, :]


def _conv_block(upad_s, seg, base, cs, wdw_ref, bdw_ref, after=None):
    sub = 8
    first = CONV_PAD - CONV_K // 2
    acc = jnp.broadcast_to(bdw_ref[0:1, cs], (CONV_RB, LANES))
    for r in range(sub):
        z = None
        for a in range((CONV_K + first + sub - 1) // sub):
            j = sub * a + r - first
            if 0 <= j < CONV_K:
                lo = base + sub * a
                tap = wdw_ref[j:j + 1, cs] if after is None else wdw_ref[j:j + 1, cs] + after
                term = tap * upad_s[seg, lo:lo + CONV_RB + sub, cs]
                z = term if z is None else z + term
        acc = acc + z[r:r + CONV_RB, :]
    return acc


def _mixer_kernel(R, T, P, has_state, emit_state, mod_index, *refs):
    L = SUB
    n_mt = R // MIX_TM
    cpm = MIX_TM // L
    n_seq = R // T
    cps = T // L
    nseg = MIX_TM // P
    assert not has_state or n_seq == 1
    it = iter(refs)
    x_ref = next(it)
    mod_ref = next(it)
    if has_state:
        c0_ref = next(it)
        n0_ref = next(it)
        m0_ref = next(it)
    w = {name: next(it) for name in _MIXER_WEIGHTS}
    x1_ref = next(it)
    h2_ref = next(it)
    comb_ref = next(it)
    route_ref = next(it)
    cnt_ref = next(it)
    if emit_state:
        cout_ref = next(it)
        nout_ref = next(it)
        mout_ref = next(it)
    (q_s, kT_s, v_s, so_s, scan_s, ma_s, sgb_s, hm_s, cst_s, upad_s) = [next(it) for _ in range(10)]

    cond_row = mod_index(pl.program_id(0))

    def mod_row(i):
        return mod_ref[i, pl.ds(cond_row, 1), :]

    zpad = jnp.zeros((CONV_PAD, D_CONV), F32)
    for seg in range(nseg):
        upad_s[seg, 0:CONV_PAD, :] = zpad
        upad_s[seg, CONV_PAD + P:CONV_PAD + P + CONV_PAD, :] = zpad

    t_idx = lax.broadcasted_iota(jnp.int32, (L, L), 0)
    s_idx = lax.broadcasted_iota(jnp.int32, (L, L), 1)
    lower = s_idx <= t_idx
    upper = s_idx >= t_idx
    triu_b = upper.astype(F32).astype(BF16)
    lane_u = lax.broadcasted_iota(jnp.int32, (N_UNITS, L), 1)
    is_bwd = lax.broadcasted_iota(jnp.int32, (N_UNITS, L), 0) >= N_HEADS

    def gate_scan(g):
        gi, lf = g[:N_UNITS], _log_sigmoid(g[N_UNITS:])
        pr = _dot(jnp.concatenate(_split3(lf), axis=0).astype(BF16), triu_b)
        pre = pr[0:N_UNITS] + pr[N_UNITS:2 * N_UNITS] + pr[2 * N_UNITS:]
        tot = pre[:, L - 1:L]
        bsum = jnp.where(is_bwd, tot - pre + lf, pre)
        a = gi - bsum
        pm, sm, k = a, a, 1
        while k < L:
            pm = jnp.where(lane_u >= k, jnp.maximum(pm, pltpu.roll(pm, k, axis=1)), pm)
            sm = jnp.where(lane_u < L - k, jnp.maximum(sm, pltpu.roll(sm, L - k, axis=1)), sm)
            k *= 2
        wide = lambda v: jnp.broadcast_to(v, (N_UNITS, L))
        return jnp.concatenate([a, jnp.where(is_bwd, sm, pm), bsum, wide(tot),
                                wide(jnp.max(a, axis=1, keepdims=True))], axis=0)

    def phase1(i, carry):
        r0 = pl.multiple_of(i * MIX_TM, MIX_TM)
        rows = pl.ds(r0, MIX_TM)
        x = x_ref[0, rows, :]
        xn = x * lax.rsqrt(jnp.mean(x * x, axis=-1, keepdims=True) + EPS) * w["g1"][...]
        hb = (xn * (1.0 + mod_row(1)) + mod_row(0)).astype(BF16)

        gates = _dot_nt(w["wgifT"][...].astype(BF16), hb)
        gates = jnp.concatenate([gates[d * 2 * N_HEADS + g * N_HEADS:d * 2 * N_HEADS + (g + 1) * N_HEADS]
                                 for g in range(2) for d in range(2)], axis=0) + w["bgifT"][...]
        for j in range(cpm):
            scan_s[i * cpm + j] = gate_scan(gates[:, j * L:(j + 1) * L])
        ag = _dot(hb, w["wrow"][:, :2 * D_CONV]) + w["brow"][:, :2 * D_CONV]
        u = ag[:, :D_CONV] * _sigmoid_of_half(ag[:, D_CONV:])
        for seg in range(nseg):
            upad_s[seg, CONV_PAD:CONV_PAD + P, :] = u[seg * P:(seg + 1) * P, :]

        def proj(name, c0, gate, width=2 * LANES):
            w0 = WROW_OFFSET[name] + c0
            b = w["brow"][:, w0:w0 + width]
            if gate is not None:
                b = b + jnp.concatenate([gate] * (width // LANES), axis=1)
            return _dot(hb, w["wrow"][:, w0:w0 + width]) + b

        last = lambda z: z[-8:, -LANES:]
        bk_row = w["brow"][:, BROW_K_OFFSET:BROW_K_OFFSET + D_MLSTM]
        bk_col = jnp.concatenate([bk_row, jnp.zeros((LANES - 1, D_MLSTM), F32)], axis=0).T[:, 0:1]

        def gm_a(c0, gate):
            z = proj("wgm", c0, gate)
            ma_s[rows, c0:c0 + 2 * LANES] = _sigmoid_of_half(z)
            return last(z)

        def gm_b(c0, gate):
            z = proj("wgm", D_MODEL + c0, gate)
            sgb_s[rows, c0:c0 + 2 * LANES] = _sigmoid_of_half(z)
            return last(z)

        def q_part(c0, gate):
            z = proj("wq", c0, gate)
            q_s[rows, c0:c0 + 2 * LANES] = (z * (HEAD_DIM ** -0.5)).astype(BF16)
            return last(z)

        def v_part(c0, gate):
            z = proj("wv", c0, gate)
            v_s[rows, c0:c0 + 2 * LANES] = z.astype(BF16)
            return last(z)

        def o_part(c0, gate):
            z = proj("wog", c0, gate)
            so_s[rows, c0:c0 + 2 * LANES] = _sigmoid_of_half(z)
            return last(z)

        def k_part(c0, gate):
            rs = slice(c0, c0 + 2 * LANES)
            b = bk_col[rs, :] if gate is None else bk_col[rs, :] + gate[:, 0:1]
            z = _dot_nt(w["wkT"][rs, :].astype(BF16), hb) + b
            kt = z.astype(BF16)
            for j in range(cpm):
                kT_s[i * cpm + j, rs, :] = kt[:, j * L:(j + 1) * L]
            return last(z)

        jobs = ([functools.partial(gm_a, c0) for c0 in range(0, D_MODEL, 2 * LANES)]
                + [functools.partial(gm_b, c0) for c0 in range(0, D_MODEL, 2 * LANES)]
                + [functools.partial(f, c0) for f in (q_part, k_part, v_part, o_part)
                   for c0 in range(0, D_MLSTM, 2 * LANES)])
        n_jobs = len(jobs)
        conv = {}
        after = None
        n_pieces = (D_CONV // LANES) * nseg * (P // CONV_RB)
        for cb in range(D_CONV // LANES):
            cs = slice(cb * LANES, (cb + 1) * LANES)
            for seg in range(nseg):
                for rb in range(P // CONV_RB):
                    blk = _conv_block(upad_s, seg, rb * CONV_RB, cs, w["wdw"], w["bdw"], after)
                    conv[(cb, seg, rb)] = blk
                    if jobs and len(conv) * n_jobs >= (n_jobs - len(jobs) + 1) * n_pieces:
                        after = _zero_after(jobs.pop(0)(_zero_after(blk[-8:, :])))
        for job in jobs:
            job(None)
        cu = jnp.concatenate(
            [jnp.concatenate([conv[(cb, seg, rb)] for seg in range(nseg) for rb in range(P // CONV_RB)], axis=0)
             for cb in range(D_CONV // LANES)], axis=1)
        mu = jnp.mean(cu, axis=-1, keepdims=True)
        cc = cu - mu
        cn = cc * lax.rsqrt(jnp.mean(cc * cc, axis=-1, keepdims=True) + EPS) * w["lng"][...] + w["lnb"][...]
        ca = (cn * _sigmoid(cn)).astype(BF16)
        ma_s[rows, :] = ma_s[rows, :] * _dot(ca, w["wco"][...])
        return carry

    if n_mt == 1:
        phase1(0, 0)
    else:
        lax.fori_loop(0, n_mt, phase1, 0)

    ones_col = (lax.broadcasted_iota(jnp.int32, (L, HEAD_DIM), 1) == 0).astype(F32).astype(BF16)
    pad_rows = jnp.zeros((LANES - 3 * N_UNITS, L), F32)

    def gate_prep(c, m_vec):
        sc = scan_s[c]
        a, run_max, bsum = sc[0:N_UNITS], sc[N_UNITS:2 * N_UNITS], sc[2 * N_UNITS:3 * N_UNITS]
        tot, a_max = sc[3 * N_UNITS:4 * N_UNITS, 0:1], sc[4 * N_UNITS:5 * N_UNITS, 0:1]
        big_m = jnp.maximum(m_vec, run_max)
        m_end = jnp.maximum(m_vec, a_max)
        cols = jnp.concatenate(
            [big_m, jnp.exp(m_vec - big_m), jnp.exp(-bsum - big_m), pad_rows], axis=0).T
        return a, cols, jnp.exp(a - m_end), jnp.exp(m_vec - m_end), tot + m_end

    def unit_group(dirs, c, prep, first_chunk, want_state):
        a, cols, wk, decay, _ = prep
        rows = slice(c * L, (c + 1) * L)
        heads = range(N_HEADS)
        units = [(d, hd) for d in dirs for hd in heads]
        hs = [slice(hd * HEAD_DIM, (hd + 1) * HEAD_DIM) for hd in heads]
        idx = {u: u[0] * N_HEADS + u[1] for u in units}
        col = lambda k, u: cols[:, k * N_UNITS + idx[u]:k * N_UNITS + idx[u] + 1]
        row = lambda arr, u: arr[idx[u]:idx[u] + 1, :]
        chained = has_state or not first_chunk
        qc = [q_s[rows, hs[hd]] for hd in heads]
        kTc = [kT_s[c, hs[hd], :] for hd in heads]
        vaug = [jnp.concatenate([v_s[rows, hs[hd]], ones_col], axis=1) for hd in heads]
        qk = [_dot(qc[hd], kTc[hd]) for hd in heads]
        s_mat = {u: (qk[u[1]] * jnp.where(lower if u[0] == 0 else upper, jnp.exp(row(a, u) - col(0, u)), 0.0)
                     ).astype(BF16) for u in units}
        nd = {u: _dot(s_mat[u], vaug[u[1]]) for u in units}
        if chained:
            nd = {u: nd[u] + col(1, u) * _dot(qc[u[1]], cst_s[idx[u]].astype(BF16)) for u in units}
        h = {u: nd[u][:, :HEAD_DIM] * (1.0 / jnp.maximum(jnp.abs(nd[u][:, HEAD_DIM:HEAD_DIM + 1]), col(2, u)))
             for u in units}
        for hd in heads:
            total = h[(dirs[0], hd)]
            for d in dirs[1:]:
                total = total + h[(d, hd)]
            if dirs[0] == 0:
                hm_s[rows, hs[hd]] = total
            else:
                hm_s[rows, hs[hd]] = hm_s[rows, hs[hd]] + total
        if want_state:
            kw = {u: (kTc[u[1]].astype(F32) * row(wk, u)).astype(BF16) for u in units}
            upd = {u: _dot(kw[u], vaug[u[1]]) for u in units}
            for u in units:
                cst_s[idx[u]] = (upd[u] + row(decay, u) * cst_s[idx[u]]) if chained else upd[u]

    dir_rows = lax.broadcasted_iota(jnp.int32, (N_UNITS, 1), 0) >= N_HEADS
    for seq in range(n_seq):
        if has_state:
            n_cols = jnp.concatenate([n0_ref[0], jnp.zeros((LANES - N_UNITS, HEAD_DIM), F32)], axis=0).T
            first_lane = lax.broadcasted_iota(jnp.int32, (HEAD_DIM, HEAD_DIM), 1) == 0
            for idx in range(N_UNITS):
                cst_s[idx, :, :HEAD_DIM] = c0_ref[0, idx]
                cst_s[idx, :, HEAD_DIM:] = jnp.where(first_lane, n_cols[:, idx:idx + 1], 0.0)
            unit_row = lax.broadcasted_iota(jnp.int32, (N_UNITS, 1), 0)
            m_vec = jnp.zeros((N_UNITS, 1), F32)
            for idx in range(N_UNITS):
                m_vec = jnp.where(unit_row == idx, m0_ref[pl.program_id(0), idx], m_vec)
        else:
            m_vec = jnp.zeros((N_UNITS, 1), F32)
        if cps == 1:
            prep = gate_prep(seq, m_vec)
            unit_group([0, 1], seq, prep, True, emit_state)
            m_vec = prep[4]
        else:
            for d in range(2):
                order = list(range(cps)) if d == 0 else list(range(cps - 1, -1, -1))
                for pos, c in enumerate(order):
                    prep = gate_prep(seq * cps + c, m_vec)
                    unit_group([d], seq * cps + c, prep, pos == 0, emit_state or pos < cps - 1)
                    m_vec = jnp.where(dir_rows == (d == 1), prep[4], m_vec)
        if emit_state:
            for idx in range(N_UNITS):
                caug = cst_s[idx]
                cout_ref[0, seq * N_UNITS + idx] = caug[:, :HEAD_DIM]
                nout_ref[0, seq * N_UNITS + idx:seq * N_UNITS + idx + 1, :] = caug[:, HEAD_DIM:].T[0:1, :]
            mout_ref[0, seq * N_UNITS:(seq + 1) * N_UNITS, :] = jnp.broadcast_to(m_vec, (N_UNITS, LANES))

    e_iota = lax.broadcasted_iota(jnp.int32, (LANES, MIX_TM), 0)
    g_of_e = lax.shift_right_logical(e_iota, 2)
    j_of_e = lax.bitwise_and(e_iota, EXPERTS_PER_GROUP - 1)
    r8 = lax.broadcasted_iota(jnp.int32, (8, MIX_TM), 0)
    before_b = (lax.broadcasted_iota(jnp.int32, (MOE_BLK, MOE_BLK), 0)
                < lax.broadcasted_iota(jnp.int32, (MOE_BLK, MOE_BLK), 1)).astype(F32).astype(BF16)

    def phase3(i, carry):
        r0 = pl.multiple_of(i * MIX_TM, MIX_TM)
        rows = pl.ds(r0, MIX_TM)
        hm = hm_s[rows, :]
        heads = []
        for hd in range(N_HEADS):
            hh = hm[:, hd * HEAD_DIM:(hd + 1) * HEAD_DIM]
            heads.append(hh * lax.rsqrt(jnp.mean(hh * hh, axis=-1, keepdims=True) + EPS))
        hn = jnp.concatenate(heads, axis=1) * w["hng"][...]
        hb2 = (so_s[rows, :] * hn).astype(BF16)
        br_b = _dot(hb2, w["wmo"][...])
        mixed = (ma_s[rows, :] + sgb_s[rows, :] * br_b).astype(BF16)
        x1 = x_ref[0, rows, :] + mod_row(2) * _dot(mixed, w["wo"][...])
        x1_ref[0, rows, :] = x1
        xn = x1 * lax.rsqrt(jnp.mean(x1 * x1, axis=-1, keepdims=True) + EPS) * w["g2"][...]
        h2 = xn * (1.0 + mod_row(4)) + mod_row(3)
        h2_ref[0, rows, :] = h2.astype(BF16)

        h2_hi = h2.astype(BF16)
        h2_lo = (h2 - h2_hi.astype(F32)).astype(BF16)
        lg = _dot(h2_hi, w["wrt2"][...])
        lg = lg[:, :LANES] + lg[:, LANES:] + _dot(h2_lo, w["wrt2"][:, :LANES])
        lt = lg.T + w["brtT"][...]
        gl = [lt[N_EXPERTS + g:N_EXPERTS + g + 1, :] for g in range(N_GROUPS)]
        best, gsel = gl[0], jnp.zeros((1, MIX_TM), jnp.int32)
        for g in range(1, N_GROUPS):
            better = gl[g] > best
            gsel = jnp.where(better, g, gsel)
            best = jnp.where(better, gl[g], best)
        gp_sel = 1.0 / sum(jnp.exp(v - best) for v in gl)
        el = []
        for j in range(EXPERTS_PER_GROUP):
            v = lt[j:j + 1, :]
            for g in range(1, N_GROUPS):
                r = g * EXPERTS_PER_GROUP + j
                v = jnp.where(gsel == g, lt[r:r + 1, :], v)
            el.append(v)
        l1, e1 = el[0], jnp.zeros((1, MIX_TM), jnp.int32)
        for j in range(1, EXPERTS_PER_GROUP):
            better = el[j] > l1
            e1 = jnp.where(better, j, e1)
            l1 = jnp.where(better, el[j], l1)
        l2 = jnp.full((1, MIX_TM), -jnp.inf, F32)
        e2 = jnp.zeros((1, MIX_TM), jnp.int32)
        for j in range(EXPERTS_PER_GROUP):
            better = jnp.logical_and(e1 != j, el[j] > l2)
            e2 = jnp.where(better, j, e2)
            l2 = jnp.where(better, el[j], l2)
        r2 = jnp.exp(l2 - l1)
        wt1 = gp_sel / (1.0 + r2)
        wt2 = gp_sel * r2 / (1.0 + r2)
        in_group = g_of_e == gsel
        comb_t = (jnp.where(jnp.logical_and(in_group, j_of_e == e1), wt1, 0.0)
                  + jnp.where(jnp.logical_and(in_group, j_of_e == e2), wt2, 0.0))

        onehot = (r8 == gsel).astype(F32)
        gsel_f = gsel.astype(F32)
        rank = jnp.sum(onehot * _dot(onehot.astype(BF16), before_b), axis=0, keepdims=True)
        r8rows = pl.ds(pl.multiple_of(i * 8, 8), 8)
        route_ref[0, r8rows, :] = jnp.where(r8 == 0, gsel_f, jnp.where(r8 == 1, rank, 0.0))
        cnt_ref[0, r8rows, :] = jnp.broadcast_to(jnp.sum(onehot, axis=1, keepdims=True), (8, LANES))
        comb_t = jnp.where(e_iota == ROUTE_GROUP_LANE, gsel_f,
                           jnp.where(e_iota == ROUTE_RANK_LANE, rank, comb_t))
        comb_ref[0, rows, :] = comb_t.T
        return carry

    if n_mt == 1:
        phase3(0, 0)
    else:
        lax.fori_loop(0, n_mt, phase3, 0)


class _RowWindow(NamedTuple):
    array: jax.Array
    start: int
    n: int


def _const_spec(a):
    if isinstance(a, _RowWindow):
        assert a.start % a.n == 0
        return a.array, pl.BlockSpec((a.n, a.array.shape[1]), lambda b: (a.start // a.n, 0),
                                     pipeline_mode=pl.Buffered(1))
    nd = a.ndim
    return a, pl.BlockSpec(a.shape, lambda b, _nd=nd: (0,) * _nd, pipeline_mode=pl.Buffered(1))


def _mixer(x, T, mod, mod_index, weights, P, state=None, emit_state=False):
    B, R, _ = x.shape
    n_chunks = R // SUB
    n_blk = R // MOE_BLK
    n_seq = R // T
    has_state = state is not None
    seq_mode = {} if R <= MIX_TM else {"pipeline_mode": pl.Buffered(1)}
    in_specs = [
        pl.BlockSpec((1, R, D_MODEL), lambda b: (b, 0, 0), **seq_mode),
        pl.BlockSpec(mod.shape, lambda b: (0, 0, 0)),
    ]
    args = [x, mod]
    if has_state:
        c0, n0, m0 = state
        in_specs += [
            pl.BlockSpec((1, N_UNITS, HEAD_DIM, HEAD_DIM), lambda b: (b, 0, 0, 0)),
            pl.BlockSpec((1, N_UNITS, HEAD_DIM), lambda b: (b, 0, 0)),
            pl.BlockSpec(memory_space=pltpu.SMEM),
        ]
        args += [c0, n0, m0]
    for name in _MIXER_WEIGHTS:
        operand, spec = _const_spec(weights[name])
        in_specs.append(spec)
        args.append(operand)
    out_shape = [
        jax.ShapeDtypeStruct((B, R, D_MODEL), F32),
        jax.ShapeDtypeStruct((B, R, D_MODEL), BF16),
        jax.ShapeDtypeStruct((B, R, LANES), F32),
        jax.ShapeDtypeStruct((B, n_blk * 8, MOE_BLK), F32),
        jax.ShapeDtypeStruct((B, n_blk * 8, LANES), F32),
    ]
    out_specs = [
        pl.BlockSpec((1, R, D_MODEL), lambda b: (b, 0, 0), **seq_mode),
        pl.BlockSpec((1, R, D_MODEL), lambda b: (b, 0, 0), **seq_mode),
        pl.BlockSpec((1, R, LANES), lambda b: (b, 0, 0)),
        pl.BlockSpec((1, n_blk * 8, MOE_BLK), lambda b: (b, 0, 0)),
        pl.BlockSpec((1, n_blk * 8, LANES), lambda b: (b, 0, 0)),
    ]
    if emit_state:
        out_shape += [
            jax.ShapeDtypeStruct((B, n_seq * N_UNITS, HEAD_DIM, HEAD_DIM), F32),
            jax.ShapeDtypeStruct((B, n_seq * N_UNITS, HEAD_DIM), F32),
            jax.ShapeDtypeStruct((B, n_seq * N_UNITS, LANES), F32),
        ]
        out_specs += [
            pl.BlockSpec((1, n_seq * N_UNITS, HEAD_DIM, HEAD_DIM), lambda b: (b, 0, 0, 0)),
            pl.BlockSpec((1, n_seq * N_UNITS, HEAD_DIM), lambda b: (b, 0, 0)),
            pl.BlockSpec((1, n_seq * N_UNITS, LANES), lambda b: (b, 0, 0)),
        ]
    scratch = [
        pltpu.VMEM((R, D_MLSTM), BF16),
        pltpu.VMEM((n_chunks, D_MLSTM, SUB), BF16),
        pltpu.VMEM((R, D_MLSTM), BF16),
        pltpu.VMEM((R, D_MLSTM), F32),
        pltpu.VMEM((n_chunks, 5 * N_UNITS, SUB), F32),
        pltpu.VMEM((R, D_MODEL), F32),
        pltpu.VMEM((R, D_MODEL), F32),
        pltpu.VMEM((R, D_MLSTM), F32),
        pltpu.VMEM((N_UNITS, HEAD_DIM, 2 * HEAD_DIM), F32),
        pltpu.VMEM((MIX_TM // P, P + 2 * CONV_PAD, D_CONV), F32),
    ]
    return pl.pallas_call(
        functools.partial(_mixer_kernel, R, T, P, has_state, emit_state, mod_index),
        grid=(B,),
        in_specs=in_specs,
        out_specs=out_specs,
        out_shape=out_shape,
        scratch_shapes=scratch,
        compiler_params=pltpu.CompilerParams(
            dimension_semantics=("arbitrary",), vmem_limit_bytes=VMEM_LIMIT),
        name="mixer_T%d" % T,
    )(*args)


def _dest_in_block(group, rank, starts):
    dest = rank
    for g in range(N_GROUPS):
        dest = dest + jnp.where(group == float(g), starts[g], 0.0)
    return dest


def _copy_segments(src_refs, dst_refs, src_starts, dst_starts, n_pieces):
    def copy(g, first_piece, n_rows):
        s = pl.multiple_of(src_starts[g] + first_piece * ROW_ALIGN, ROW_ALIGN)
        d = pl.multiple_of(dst_starts[g] + first_piece * ROW_ALIGN, ROW_ALIGN)
        for src, dst in zip(src_refs, dst_refs):
            dst[pl.ds(d, n_rows), :] = src[pl.ds(s, n_rows), :]

    for g in range(N_GROUPS):
        n_runs = lax.shift_right_logical(n_pieces[g], COPY_RUN.bit_length() - 1)

        def run(k, carry, g=g):
            copy(g, k * COPY_RUN, COPY_RUN * ROW_ALIGN)
            return carry

        def single(k, carry, g=g):
            copy(g, k, ROW_ALIGN)
            return carry

        lax.fori_loop(0, n_runs, run, 0)
        lax.fori_loop(n_runs * COPY_RUN, n_pieces[g], single, 0)


def _plan_segments(n_blocks, n_tiles, count, start_ref, npiece_ref, off_ref, tgroup_ref, tvalid_ref):
    align_shift = ROW_ALIGN.bit_length() - 1
    tile_shift = MOE_TM.bit_length() - 1

    def block_starts(blk, carry):
        row = jnp.int32(0)
        for g in range(N_GROUPS):
            n = lax.shift_right_logical(count(blk, g) + (ROW_ALIGN - 1), align_shift)
            npiece_ref[blk * N_GROUPS + g] = n
            start_ref[blk * N_GROUPS + g] = row
            row = row + n * ROW_ALIGN
        return carry

    lax.fori_loop(0, n_blocks, block_starts, 0)

    base_row = jnp.int32(0)
    base_tile = jnp.int32(0)
    last_group = jnp.int32(0)
    for g in range(N_GROUPS):
        def seg_offsets(blk, row, g=g, base_row=base_row):
            off_ref[blk * N_GROUPS + g] = base_row + row
            return row + npiece_ref[blk * N_GROUPS + g] * ROW_ALIGN

        rows = lax.fori_loop(0, n_blocks, seg_offsets, jnp.int32(0))
        tiles = lax.shift_right_logical(rows + (MOE_TM - 1), tile_shift)

        def mark_tiles(t, carry, g=g, base_tile=base_tile):
            tgroup_ref[base_tile + t] = g
            tvalid_ref[base_tile + t] = 1
            return carry

        lax.fori_loop(0, tiles, mark_tiles, 0)
        last_group = jnp.where(tiles > 0, g, last_group)
        base_row = base_row + tiles * MOE_TM
        base_tile = base_tile + tiles

    def mark_unused(t, carry):
        tgroup_ref[t] = last_group
        tvalid_ref[t] = 0
        return carry

    lax.fori_loop(base_tile, n_tiles, mark_unused, 0)


def _dispatch_kernel(n_ctx_blocks, n_blocks, n_tiles,
                     h2c_ref, h2l_ref, cbc_ref, cbl_ref, rtc_ref, rtl_ref, cntc_ref, cntl_ref,
                     xs_ref, cs_ref, start_ref, npiece_ref, off_ref, tgroup_ref, tvalid_ref,
                     sx_s, sc_s):
    b = pl.program_id(0)
    is_ctx = b < n_ctx_blocks

    def count(blk, g):
        vc = cntc_ref[jnp.minimum(blk, n_ctx_blocks - 1), pl.ds(g, 1), pl.ds(0, 1)]
        vl = cntl_ref[jnp.maximum(blk - n_ctx_blocks, 0), pl.ds(g, 1), pl.ds(0, 1)]
        return jnp.where(blk < n_ctx_blocks, vc, vl)[0, 0].astype(jnp.int32)

    @pl.when(b == 0)
    def _():
        _plan_segments(n_blocks, n_tiles, count, start_ref, npiece_ref, off_ref, tgroup_ref, tvalid_ref)
        xs_ref[...] = jnp.zeros_like(xs_ref)
        cs_ref[...] = jnp.zeros_like(cs_ref)

    starts = [start_ref[b * N_GROUPS + g] for g in range(N_GROUPS)]

    def sort_block(h2_ref, cb_ref, rt_ref):
        h2 = h2_ref[0]
        cb = cb_ref[0]
        rt = rt_ref[0]
        dest = _dest_in_block(rt[0:1, :], rt[1:2, :], [s.astype(F32) for s in starts])
        row = lax.broadcasted_iota(jnp.int32, (SORT_ROWS, MOE_BLK), 0).astype(F32)
        perm = (row == dest).astype(F32).astype(BF16)
        cb_hi = cb.astype(BF16)
        cb_lo = (cb - cb_hi.astype(F32)).astype(BF16)
        sx_s[...] = _dot(perm, h2).astype(BF16)
        sc_s[...] = _dot(perm, jnp.concatenate([cb_hi, cb_lo], axis=1)).astype(BF16)

    pl.when(is_ctx)(functools.partial(sort_block, h2c_ref, cbc_ref, rtc_ref))
    pl.when(jnp.logical_not(is_ctx))(functools.partial(sort_block, h2l_ref, cbl_ref, rtl_ref))
    _copy_segments((sx_s, sc_s), (xs_ref, cs_ref), starts,
                   [off_ref[b * N_GROUPS + g] for g in range(N_GROUPS)],
                   [npiece_ref[b * N_GROUPS + g] for g in range(N_GROUPS)])


def _experts_kernel(tgroup_ref, tvalid_ref, xs_ref, cs_ref, wg_ref, wu_ref, wd_ref, ys_ref):
    i = pl.program_id(0)

    @pl.when(tvalid_ref[i] == 1)
    def _():
        x = xs_ref[...]
        comb = cs_ref[:, :LANES].astype(F32) + cs_ref[:, LANES:].astype(F32)
        lane = lax.broadcasted_iota(jnp.int32, comb.shape, 1)
        first = tgroup_ref[i] * EXPERTS_PER_GROUP
        acc = None
        for j in range(EXPERTS_PER_GROUP):
            gj = _dot(x, wg_ref[j].astype(BF16))
            uj = _dot(x, wu_ref[j].astype(BF16))
            cw = jnp.sum(jnp.where(lane == first + j, comb, 0.0), axis=1, keepdims=True)
            out = _dot((gj * _sigmoid(gj) * uj * cw).astype(BF16), wd_ref[j].astype(BF16))
            acc = out if acc is None else acc + out
        ys_ref[...] = acc.astype(BF16)

    @pl.when(tvalid_ref[i] == 0)
    def _():
        ys_ref[...] = jnp.zeros_like(ys_ref)


def _combine_kernel(n_ctx_blocks, blocks_per_lat_seq, start_ref, npiece_ref, off_ref,
                    x1c_ref, x1l_ref, cbc_ref, cbl_ref, ys_ref, mod_ref, gf_ref, yc_ref, yl_ref, loc_s):
    b = pl.program_id(0)
    is_ctx = b < n_ctx_blocks
    starts = [start_ref[b * N_GROUPS + g] for g in range(N_GROUPS)]
    @pl.when(b == 0)
    def _():
        loc_s[...] = jnp.zeros_like(loc_s)

    _copy_segments((ys_ref,), (loc_s,), [off_ref[b * N_GROUPS + g] for g in range(N_GROUPS)], starts,
                   [npiece_ref[b * N_GROUPS + g] for g in range(N_GROUPS)])
    def finish_block(x1_ref, cb_ref, y_ref, mrow):
        cb = cb_ref[0]
        dest = _dest_in_block(cb[:, ROUTE_GROUP_LANE:ROUTE_GROUP_LANE + 1],
                              cb[:, ROUTE_RANK_LANE:ROUTE_RANK_LANE + 1],
                              [s.astype(F32) for s in starts])
        col = lax.broadcasted_iota(jnp.int32, (MOE_BLK, SORT_ROWS), 1).astype(F32)
        unperm = (col == dest).astype(F32).astype(BF16)
        x2 = x1_ref[0] + mod_ref[N_ADA - 1, pl.ds(mrow, 1), :] * _dot(unperm, loc_s[...])
        y_ref[0] = x2 * lax.rsqrt(jnp.mean(x2 * x2, axis=-1, keepdims=True) + EPS) * gf_ref[...]

    lat_row = 1 + jnp.maximum(b - n_ctx_blocks, 0) // blocks_per_lat_seq
    pl.when(is_ctx)(functools.partial(finish_block, x1c_ref, cbc_ref, yc_ref, 0))
    pl.when(jnp.logical_not(is_ctx))(functools.partial(finish_block, x1l_ref, cbl_ref, yl_ref, lat_row))


def _moe(x1c, x1l, h2c, h2l, cbc, cbl, rtc, rtl, cntc, cntl, mod, blocks_per_lat_seq, wg, wu, wd, gf):
    nc, nl = x1c.shape[0], x1l.shape[0]
    nb = nc + nl
    n_rows_max = nb * MOE_BLK + nb * N_GROUPS * (ROW_ALIGN - 1) + N_GROUPS * (MOE_TM - ROW_ALIGN)
    n_tiles = -(-n_rows_max // MOE_TM)
    ns = n_tiles * MOE_TM

    cmap = lambda b, *_: (jnp.minimum(b, nc - 1), 0, 0)
    lmap = lambda b, *_: (jnp.maximum(b - nc, 0), 0, 0)
    whole = lambda *_: (0, 0)
    once = {"pipeline_mode": pl.Buffered(1)}
    arb = pltpu.CompilerParams(dimension_semantics=("arbitrary",), vmem_limit_bytes=VMEM_LIMIT)
    smem = pl.BlockSpec(memory_space=pltpu.SMEM)
    seg_i32 = jax.ShapeDtypeStruct((nb * N_GROUPS,), jnp.int32)
    tile_i32 = jax.ShapeDtypeStruct((n_tiles,), jnp.int32)

    xs, cs, start, npiece, off, tgroup, tvalid = pl.pallas_call(
        functools.partial(_dispatch_kernel, nc, nb, n_tiles),
        grid_spec=pltpu.PrefetchScalarGridSpec(
            num_scalar_prefetch=0, grid=(nb,),
            in_specs=[
                pl.BlockSpec((1, MOE_BLK, D_MODEL), cmap), pl.BlockSpec((1, MOE_BLK, D_MODEL), lmap),
                pl.BlockSpec((1, MOE_BLK, LANES), cmap), pl.BlockSpec((1, MOE_BLK, LANES), lmap),
                pl.BlockSpec((1, 8, MOE_BLK), cmap), pl.BlockSpec((1, 8, MOE_BLK), lmap),
                pl.BlockSpec(cntc.shape, lambda b: (0, 0, 0)), pl.BlockSpec(cntl.shape, lambda b: (0, 0, 0)),
            ],
            out_specs=[pl.BlockSpec((ns, D_MODEL), whole, **once), pl.BlockSpec((ns, 2 * LANES), whole, **once),
                       smem, smem, smem, smem, smem],
            scratch_shapes=[pltpu.VMEM((SORT_ROWS, D_MODEL), BF16), pltpu.VMEM((SORT_ROWS, 2 * LANES), BF16)],
        ),
        out_shape=[jax.ShapeDtypeStruct((ns, D_MODEL), BF16), jax.ShapeDtypeStruct((ns, 2 * LANES), BF16),
                   seg_i32, seg_i32, seg_i32, tile_i32, tile_i32],
        compiler_params=arb,
        name="moe_dispatch",
    )(h2c, h2l, cbc, cbl, rtc, rtl, cntc, cntl)

    wmap = lambda i, tg, tv: (tg[i], 0, 0)
    ys = pl.pallas_call(
        _experts_kernel,
        grid_spec=pltpu.PrefetchScalarGridSpec(
            num_scalar_prefetch=2, grid=(n_tiles,),
            in_specs=[
                pl.BlockSpec((MOE_TM, D_MODEL), lambda i, *_: (i, 0)),
                pl.BlockSpec((MOE_TM, 2 * LANES), lambda i, *_: (i, 0)),
                pl.BlockSpec((EXPERTS_PER_GROUP, D_MODEL, D_EXPERT), wmap),
                pl.BlockSpec((EXPERTS_PER_GROUP, D_MODEL, D_EXPERT), wmap),
                pl.BlockSpec((EXPERTS_PER_GROUP, D_EXPERT, D_MODEL), wmap),
            ],
            out_specs=pl.BlockSpec((MOE_TM, D_MODEL), lambda i, *_: (i, 0)),
        ),
        out_shape=jax.ShapeDtypeStruct((ns, D_MODEL), BF16),
        compiler_params=arb,
        name="moe_experts",
    )(tgroup, tvalid, xs, cs, wg, wu, wd)

    yc, yl = pl.pallas_call(
        functools.partial(_combine_kernel, nc, blocks_per_lat_seq),
        grid_spec=pltpu.PrefetchScalarGridSpec(
            num_scalar_prefetch=3, grid=(nb,),
            in_specs=[
                pl.BlockSpec((1, MOE_BLK, D_MODEL), cmap), pl.BlockSpec((1, MOE_BLK, D_MODEL), lmap),
                pl.BlockSpec((1, MOE_BLK, LANES), cmap), pl.BlockSpec((1, MOE_BLK, LANES), lmap),
                pl.BlockSpec((ns, D_MODEL), whole, **once),
                pl.BlockSpec(mod.shape, lambda *_: (0, 0, 0)),
                pl.BlockSpec((1, D_MODEL), whole),
            ],
            out_specs=[pl.BlockSpec((1, MOE_BLK, D_MODEL), cmap), pl.BlockSpec((1, MOE_BLK, D_MODEL), lmap)],
            scratch_shapes=[pltpu.VMEM((SORT_ROWS, D_MODEL), BF16)],
        ),
        out_shape=[jax.ShapeDtypeStruct((nc, MOE_BLK, D_MODEL), F32),
                   jax.ShapeDtypeStruct((nl, MOE_BLK, D_MODEL), F32)],
        compiler_params=arb,
        name="moe_combine",
    )(start, npiece, off, x1c, x1l, cbc, cbl, ys, mod, gf)
    return yc, yl


def _prep_weights(norm1_g, w_in, b_in, b_gates, w_dw, b_dw, conv_ln_g, conv_ln_b, w_conv_out,
                  mlstm_hn_g, w_mlstm_out, w_o, norm2_g, w_rg, b_rg, w_re, b_re):
    s_a = 2 * D_CONV
    s_q = s_a + D_MLSTM
    s_k = s_q + D_MLSTM
    s_v = s_k + D_MLSTM
    s_o = s_v + D_MLSTM
    s_g = s_o + 4 * N_HEADS
    row = lambda v: v.reshape(1, -1).astype(F32)
    w_t = w_in.T
    keep = [(0, s_q), (s_k, s_o), (s_g, w_in.shape[1])]
    halved = [(D_CONV, s_a), (s_v, s_o), (s_g, w_in.shape[1])]
    is_halved = lambda r: any(a <= r < b for a, b in halved)
    blocks = [r for a, b in keep for r in range(a, b, WPREP_ROWS)]
    wrow = _transpose_cast(w_t, blocks, [is_halved(r) for r in blocks])
    bias_scale = jnp.array([0.5 if is_halved(r) else 1.0 for a, b in keep for r in range(a, b)]
                           + [1.0] * D_MLSTM, F32)
    bg = (b_in[s_o:s_g] + b_gates.reshape(-1)).reshape(2, 2, N_HEADS).transpose(1, 0, 2).reshape(-1, 1)
    row_window = lambda start, n: _RowWindow(w_t, start, n)
    n_rt = N_EXPERTS + N_GROUPS
    wrt = jnp.pad(jnp.concatenate([w_re, w_rg], axis=1), ((0, 0), (0, LANES - n_rt)))
    wrt_hi = wrt.astype(BF16)
    wrt2 = jnp.concatenate([wrt_hi, (wrt - wrt_hi.astype(F32)).astype(BF16)], axis=1)
    brtT = jnp.pad(jnp.concatenate([b_re, b_rg]), (0, LANES - n_rt)).reshape(LANES, 1)
    return {
        "g1": row(norm1_g),
        "wrow": wrow, "brow": row(jnp.concatenate([b_in[a:b] for a, b in keep] + [b_in[s_q:s_k]]) * bias_scale),
        "wkT": row_window(s_q, D_MLSTM), "wgifT": row_window(s_o, 4 * N_HEADS), "bgifT": bg,
        "wdw": w_dw.astype(F32), "bdw": row(b_dw), "lng": row(conv_ln_g), "lnb": row(conv_ln_b),
        "wco": w_conv_out.astype(BF16), "hng": row(mlstm_hn_g), "wmo": w_mlstm_out.astype(BF16),
        "wo": w_o.astype(BF16), "g2": row(norm2_g), "wrt2": wrt2, "brtT": brtT,
    }


def kernel(x_prompt, x_sample, state_C, state_n, state_m, c, c_ctx, norm1_g, w_ada, b_ada, w_in, b_in, b_gates, w_dw, b_dw, conv_ln_g, conv_ln_b, w_conv_out, mlstm_hn_g, w_mlstm_out, w_o, norm2_g, w_rg, b_rg, w_re, b_re, w_e_gate, w_e_up, w_e_down, norm_final_g):
    B, S, _ = x_prompt.shape
    Bd, Sd, _ = x_sample.shape
    assert w_ada.shape[0] == 1, "single trunk layer"
    assert MIX_TM % S == 0 and S % SUB == 0 and Sd % MIX_TM == 0

    mod = _ada(c_ctx.reshape(1, -1), c, w_ada[0], b_ada[0].reshape(1, -1))

    wts = _prep_weights(norm1_g[0], w_in[0], b_in[0], b_gates[0], w_dw[0], b_dw[0], conv_ln_g[0],
                        conv_ln_b[0], w_conv_out[0], mlstm_hn_g[0], w_mlstm_out[0], w_o[0],
                        norm2_g[0], w_rg[0], b_rg[0], w_re[0], b_re[0])

    x1p, h2p, cbp, rtp, cntp, c_new, n_new, m_new = _mixer(
        x_prompt.reshape(B * S // MIX_TM, MIX_TM, D_MODEL), S, mod, lambda b: 0, wts, P=S, emit_state=True)

    state = (state_C[:, 0].reshape(Bd, N_UNITS, HEAD_DIM, HEAD_DIM), state_n[:, 0].reshape(Bd, N_UNITS, HEAD_DIM),
             state_m[:, 0].reshape(Bd, N_UNITS))
    x1s, h2s, cbs, rts, cnts = _mixer(x_sample, Sd, mod, lambda b: 1 + b, wts, P=GRID_W, state=state)

    nc, nl = B * S // MOE_BLK, Bd * Sd // MOE_BLK
    blk = lambda a, n: a.reshape(n, MOE_BLK, a.shape[-1])
    yp, ys = _moe(blk(x1p, nc), blk(x1s, nl), blk(h2p, nc), blk(h2s, nl), blk(cbp, nc), blk(cbs, nl),
                  rtp.reshape(nc, 8, MOE_BLK), rts.reshape(nl, 8, MOE_BLK),
                  cntp.reshape(nc, 8, LANES), cnts.reshape(nl, 8, LANES),
                  mod, Sd // MOE_BLK, w_e_gate[0], w_e_up[0], w_e_down[0], norm_final_g.reshape(1, -1))

    return (yp.reshape(B, S, D_MODEL), ys.reshape(Bd, Sd, D_MODEL),
            c_new.reshape(B, 1, 2, N_HEADS, HEAD_DIM, HEAD_DIM),
            n_new.reshape(B, 1, 2, N_HEADS, HEAD_DIM),
            m_new[:, :, 0].reshape(B, 1, 2, N_HEADS))
```

```python
import functools
from typing import NamedTuple

import jax
import jax.numpy as jnp
from jax import lax
from jax.experimental import pallas as pl
from jax.experimental.pallas import tpu as pltpu

D_MODEL = 1024
D_CONV = 512
CONV_K = 31
D_MLSTM = 512
N_HEADS = 4
HEAD_DIM = D_MLSTM // N_HEADS
N_GROUPS = 4
EXPERTS_PER_GROUP = 4
N_EXPERTS = N_GROUPS * EXPERTS_PER_GROUP
D_EXPERT = 256
N_ADA = 6
EPS = 1e-6
GRID_W = 64

LANES = 128
SUB = 256
CONV_PAD = 16
CONV_RB = 64
N_UNITS = 2 * N_HEADS
ROW_ALIGN = 16
CHAIN_SLACK = 1
COPY_RUN = 4
MOE_TM = 512
MIX_TM = 512
MOE_BLK = MIX_TM
SORT_ROWS = MOE_BLK + N_GROUPS * ROW_ALIGN
ADA_PER_STEP = 2
WPREP_ROWS = 512
ROUTE_GROUP_LANE = N_EXPERTS
ROUTE_RANK_LANE = N_EXPERTS + 1
VMEM_LIMIT = 58 * 1024 * 1024

BF16 = jnp.bfloat16
F32 = jnp.float32
NT_DIMS = (((1,), (1,)), ((), ()))


def _dot(a, b):
    return jnp.dot(a, b, preferred_element_type=F32)


def _dot_nt(a, b, precision=None):
    return lax.dot_general(a, b, NT_DIMS, preferred_element_type=F32, precision=precision)


def _sigmoid(x):
    return 0.5 * jnp.tanh(0.5 * x) + 0.5


def _sigmoid_of_half(xh):
    return 0.5 * jnp.tanh(xh) + 0.5


def _log_sigmoid(x):
    return jnp.minimum(x, 0.0) - jnp.log1p(jnp.exp(-jnp.abs(x)))


def _split3(x):
    hi = x.astype(BF16).astype(F32)
    r1 = x - hi
    mid = r1.astype(BF16).astype(F32)
    lo = (r1 - mid).astype(BF16).astype(F32)
    return hi, mid, lo


def _ada_kernel(cctx_ref, c_ref, w_ref, b_ref, o_ref):
    n = 1 + c_ref.shape[0]
    c = jnp.concatenate([cctx_ref[...], c_ref[...], jnp.zeros((8 - n, D_MODEL), F32)], axis=0)
    s = (c * _sigmoid(c)).astype(BF16)
    out = _dot(s, w_ref[...].astype(BF16)) + b_ref[...]
    for v in range(ADA_PER_STEP):
        o_ref[v] = out[:, v * D_MODEL:(v + 1) * D_MODEL]


def _ada(c_ctx, c, w_ada, b_ada):
    return pl.pallas_call(
        _ada_kernel,
        grid=(N_ADA // ADA_PER_STEP,),
        in_specs=[
            pl.BlockSpec(c_ctx.shape, lambda j: (0, 0)),
            pl.BlockSpec(c.shape, lambda j: (0, 0)),
            pl.BlockSpec((D_MODEL, ADA_PER_STEP * D_MODEL), lambda j: (0, j)),
            pl.BlockSpec((1, ADA_PER_STEP * D_MODEL), lambda j: (0, j)),
        ],
        out_specs=pl.BlockSpec((ADA_PER_STEP, 8, D_MODEL), lambda j: (j, 0, 0)),
        out_shape=jax.ShapeDtypeStruct((N_ADA, 8, D_MODEL), F32),
        compiler_params=pltpu.CompilerParams(dimension_semantics=("arbitrary",)),
        name="ada",
    )(c_ctx, c, w_ada, b_ada)


def _transpose_cast_kernel(starts_ref, halve_ref, wt_ref, o_ref):
    scale = jnp.where(halve_ref[pl.program_id(0)] == 1, 0.5, 1.0)
    o_ref[...] = (wt_ref[...] * scale).astype(BF16).T


def _transpose_cast(w_t, row_starts, halve):
    n, k = len(row_starts), w_t.shape[1]
    return pl.pallas_call(
        _transpose_cast_kernel,
        grid_spec=pltpu.PrefetchScalarGridSpec(
            num_scalar_prefetch=2, grid=(n,),
            in_specs=[pl.BlockSpec((pl.Element(WPREP_ROWS), pl.Element(k)), lambda j, starts, hv: (starts[j] * 8, 0))],
            out_specs=pl.BlockSpec((k, WPREP_ROWS), lambda j, starts, hv: (0, j)),
        ),
        out_shape=jax.ShapeDtypeStruct((k, n * WPREP_ROWS), BF16),
        compiler_params=pltpu.CompilerParams(dimension_semantics=("arbitrary",)),
        name="transpose_cast",
    )(jnp.array([r // 8 for r in row_starts], jnp.int32), jnp.array([int(h) for h in halve], jnp.int32), w_t)


WROW_OFFSET = {"wq": 2 * D_CONV, "wv": 2 * D_CONV + D_MLSTM, "wog": 2 * D_CONV + 2 * D_MLSTM,
               "wgm": 2 * D_CONV + 3 * D_MLSTM}
BROW_K_OFFSET = 2 * D_CONV + 3 * D_MLSTM + 2 * D_MODEL

_MIXER_WEIGHTS = (
    "g1", "wrow", "brow", "wkT", "wgifT", "bgifT", "wdw", "bdw", "lng", "lnb",
    "wco", "hng", "wmo", "wo", "g2", "wrt2", "brtT",
)


def _zero_after(x):
    bits = lax.bitcast_convert_type(x, jnp.uint32)
    bits = lax.shift_right_logical(lax.shift_right_logical(bits, jnp.uint32(16)), jnp.uint32(16))
    return lax.bitcast_convert_type(bits, F32)[0:1, :]


def _conv_block(upad_s, seg, base, cs, wdw_ref, bdw_ref, after=None):
    sub = 8
    first = CONV_PAD - CONV_K // 2
    acc = jnp.broadcast_to(bdw_ref[0:1, cs], (CONV_RB, LANES))
    for r in range(sub):
        z = None
        for a in range((CONV_K + first + sub - 1) // sub):
            j = sub * a + r - first
            if 0 <= j < CONV_K:
                lo = base + sub * a
                tap = wdw_ref[j:j + 1, cs] if after is None else wdw_ref[j:j + 1, cs] + after
                term = tap * upad_s[seg, lo:lo + CONV_RB + sub, cs]
                z = term if z is None else z + term
        acc = acc + z[r:r + CONV_RB, :]
    return acc


def _mixer_kernel(R, T, P, has_state, emit_state, mod_index, *refs):
    L = SUB
    n_mt = R // MIX_TM
    cpm = MIX_TM // L
    n_seq = R // T
    cps = T // L
    nseg = MIX_TM // P
    assert not has_state or n_seq == 1
    it = iter(refs)
    x_ref = next(it)
    mod_ref = next(it)
    if has_state:
        c0_ref = next(it)
        n0_ref = next(it)
        m0_ref = next(it)
    w = {name: next(it) for name in _MIXER_WEIGHTS}
    x1_ref = next(it)
    h2_ref = next(it)
    comb_ref = next(it)
    route_ref = next(it)
    cnt_ref = next(it)
    if emit_state:
        cout_ref = next(it)
        nout_ref = next(it)
        mout_ref = next(it)
    (q_s, kT_s, v_s, so_s, scan_s, ma_s, sgb_s, hm_s, cst_s, upad_s) = [next(it) for _ in range(10)]

    cond_row = mod_index(pl.program_id(0))

    def mod_row(i):
        return mod_ref[i, pl.ds(cond_row, 1), :]

    zpad = jnp.zeros((CONV_PAD, D_CONV), F32)
    for seg in range(nseg):
        upad_s[seg, 0:CONV_PAD, :] = zpad
        upad_s[seg, CONV_PAD + P:CONV_PAD + P + CONV_PAD, :] = zpad

    t_idx = lax.broadcasted_iota(jnp.int32, (L, L), 0)
    s_idx = lax.broadcasted_iota(jnp.int32, (L, L), 1)
    lower = s_idx <= t_idx
    upper = s_idx >= t_idx
    triu_b = upper.astype(F32).astype(BF16)
    lane_u = lax.broadcasted_iota(jnp.int32, (N_UNITS, L), 1)
    is_bwd = lax.broadcasted_iota(jnp.int32, (N_UNITS, L), 0) >= N_HEADS

    def gate_scan(g):
        gi, lf = g[:N_UNITS], _log_sigmoid(g[N_UNITS:])
        pr = _dot(jnp.concatenate(_split3(lf), axis=0).astype(BF16), triu_b)
        pre = pr[0:N_UNITS] + pr[N_UNITS:2 * N_UNITS] + pr[2 * N_UNITS:]
        tot = pre[:, L - 1:L]
        bsum = jnp.where(is_bwd, tot - pre + lf, pre)
        a = gi - bsum
        pm, sm, k = a, a, 1
        while k < L:
            pm = jnp.where(lane_u >= k, jnp.maximum(pm, pltpu.roll(pm, k, axis=1)), pm)
            sm = jnp.where(lane_u < L - k, jnp.maximum(sm, pltpu.roll(sm, L - k, axis=1)), sm)
            k *= 2
        wide = lambda v: jnp.broadcast_to(v, (N_UNITS, L))
        return jnp.concatenate([a, jnp.where(is_bwd, sm, pm), bsum, wide(tot),
                                wide(jnp.max(a, axis=1, keepdims=True))], axis=0)

    def phase1(i, carry):
        r0 = pl.multiple_of(i * MIX_TM, MIX_TM)
        rows = pl.ds(r0, MIX_TM)
        x = x_ref[0, rows, :]
        xn = x * lax.rsqrt(jnp.mean(x * x, axis=-1, keepdims=True) + EPS) * w["g1"][...]
        hb = (xn * (1.0 + mod_row(1)) + mod_row(0)).astype(BF16)

        gates = _dot_nt(w["wgifT"][...].astype(BF16), hb)
        gates = jnp.concatenate([gates[d * 2 * N_HEADS + g * N_HEADS:d * 2 * N_HEADS + (g + 1) * N_HEADS]
                                 for g in range(2) for d in range(2)], axis=0) + w["bgifT"][...]
        for j in range(cpm):
            scan_s[i * cpm + j] = gate_scan(gates[:, j * L:(j + 1) * L])
        ag = _dot(hb, w["wrow"][:, :2 * D_CONV]) + w["brow"][:, :2 * D_CONV]
        u = ag[:, :D_CONV] * _sigmoid_of_half(ag[:, D_CONV:])
        for seg in range(nseg):
            upad_s[seg, CONV_PAD:CONV_PAD + P, :] = u[seg * P:(seg + 1) * P, :]

        def proj(name, c0, gate, width=2 * LANES):
            w0 = WROW_OFFSET[name] + c0
            b = w["brow"][:, w0:w0 + width]
            if gate is not None:
                b = b + jnp.concatenate([gate] * (width // LANES), axis=1)
            return _dot(hb, w["wrow"][:, w0:w0 + width]) + b

        last = lambda z: z[-8:, -LANES:]
        bk_row = w["brow"][:, BROW_K_OFFSET:BROW_K_OFFSET + D_MLSTM]
        bk_col = jnp.concatenate([bk_row, jnp.zeros((LANES - 1, D_MLSTM), F32)], axis=0).T[:, 0:1]

        def gm_a(c0, gate):
            z = proj("wgm", c0, gate)
            ma_s[rows, c0:c0 + 2 * LANES] = _sigmoid_of_half(z)
            return last(z)

        def gm_b(c0, gate):
            z = proj("wgm", D_MODEL + c0, gate)
            sgb_s[rows, c0:c0 + 2 * LANES] = _sigmoid_of_half(z)
            return last(z)

        def q_part(c0, gate):
            z = proj("wq", c0, gate)
            q_s[rows, c0:c0 + 2 * LANES] = (z * (HEAD_DIM ** -0.5)).astype(BF16)
            return last(z)

        def v_part(c0, gate):
            z = proj("wv", c0, gate)
            v_s[rows, c0:c0 + 2 * LANES] = z.astype(BF16)
            return last(z)

        def o_part(c0, gate):
            z = proj("wog", c0, gate)
            so_s[rows, c0:c0 + 2 * LANES] = _sigmoid_of_half(z)
            return last(z)

        def k_part(c0, gate):
            rs = slice(c0, c0 + 2 * LANES)
            b = bk_col[rs, :] if gate is None else bk_col[rs, :] + gate[:, 0:1]
            z = _dot_nt(w["wkT"][rs, :].astype(BF16), hb) + b
            kt = z.astype(BF16)
            for j in range(cpm):
                kT_s[i * cpm + j, rs, :] = kt[:, j * L:(j + 1) * L]
            return last(z)

        jobs = ([functools.partial(gm_a, c0) for c0 in range(0, D_MODEL, 2 * LANES)]
                + [functools.partial(gm_b, c0) for c0 in range(0, D_MODEL, 2 * LANES)]
                + [functools.partial(f, c0) for f in (q_part, k_part, v_part, o_part)
                   for c0 in range(0, D_MLSTM, 2 * LANES)])
        n_jobs = len(jobs)
        conv = {}
        after, lag = None, [None] * CHAIN_SLACK
        n_pieces = (D_CONV // LANES) * nseg * (P // CONV_RB)
        for cb in range(D_CONV // LANES):
            cs = slice(cb * LANES, (cb + 1) * LANES)
            for seg in range(nseg):
                for rb in range(P // CONV_RB):
                    blk = _conv_block(upad_s, seg, rb * CONV_RB, cs, w["wdw"], w["bdw"], after)
                    conv[(cb, seg, rb)] = blk
                    if jobs and len(conv) * n_jobs >= (n_jobs - len(jobs) + 1) * n_pieces:
                        lag.append(_zero_after(jobs.pop(0)(_zero_after(blk[-8:, :]))))
                        after = lag.pop(0)
        for job in jobs:
            job(None)
        cu = jnp.concatenate(
            [jnp.concatenate([conv[(cb, seg, rb)] for seg in range(nseg) for rb in range(P // CONV_RB)], axis=0)
             for cb in range(D_CONV // LANES)], axis=1)
        mu = jnp.mean(cu, axis=-1, keepdims=True)
        cc = cu - mu
        cn = cc * lax.rsqrt(jnp.mean(cc * cc, axis=-1, keepdims=True) + EPS) * w["lng"][...] + w["lnb"][...]
        ca = (cn * _sigmoid(cn)).astype(BF16)
        ma_s[rows, :] = ma_s[rows, :] * _dot(ca, w["wco"][...])
        return carry

    if n_mt == 1:
        phase1(0, 0)
    else:
        lax.fori_loop(0, n_mt, phase1, 0)

    ones_col = (lax.broadcasted_iota(jnp.int32, (L, HEAD_DIM), 1) == 0).astype(F32).astype(BF16)
    pad_rows = jnp.zeros((LANES - 3 * N_UNITS, L), F32)

    def gate_prep(c, m_vec):
        sc = scan_s[c]
        a, run_max, bsum = sc[0:N_UNITS], sc[N_UNITS:2 * N_UNITS], sc[2 * N_UNITS:3 * N_UNITS]
        tot, a_max = sc[3 * N_UNITS:4 * N_UNITS, 0:1], sc[4 * N_UNITS:5 * N_UNITS, 0:1]
        big_m = jnp.maximum(m_vec, run_max)
        m_end = jnp.maximum(m_vec, a_max)
        cols = jnp.concatenate(
            [big_m, jnp.exp(m_vec - big_m), jnp.exp(-bsum - big_m), pad_rows], axis=0).T
        return a, cols, jnp.exp(a - m_end), jnp.exp(m_vec - m_end), tot + m_end

    def unit_group(dirs, c, prep, first_chunk, want_state):
        a, cols, wk, decay, _ = prep
        rows = slice(c * L, (c + 1) * L)
        heads = range(N_HEADS)
        units = [(d, hd) for d in dirs for hd in heads]
        hs = [slice(hd * HEAD_DIM, (hd + 1) * HEAD_DIM) for hd in heads]
        idx = {u: u[0] * N_HEADS + u[1] for u in units}
        col = lambda k, u: cols[:, k * N_UNITS + idx[u]:k * N_UNITS + idx[u] + 1]
        row = lambda arr, u: arr[idx[u]:idx[u] + 1, :]
        chained = has_state or not first_chunk
        qc = [q_s[rows, hs[hd]] for hd in heads]
        kTc = [kT_s[c, hs[hd], :] for hd in heads]
        vaug = [jnp.concatenate([v_s[rows, hs[hd]], ones_col], axis=1) for hd in heads]
        qk = [_dot(qc[hd], kTc[hd]) for hd in heads]
        s_mat = {u: (qk[u[1]] * jnp.where(lower if u[0] == 0 else upper, jnp.exp(row(a, u) - col(0, u)), 0.0)
                     ).astype(BF16) for u in units}
        nd = {u: _dot(s_mat[u], vaug[u[1]]) for u in units}
        if chained:
            nd = {u: nd[u] + col(1, u) * _dot(qc[u[1]], cst_s[idx[u]].astype(BF16)) for u in units}
        h = {u: nd[u][:, :HEAD_DIM] * (1.0 / jnp.maximum(jnp.abs(nd[u][:, HEAD_DIM:HEAD_DIM + 1]), col(2, u)))
             for u in units}
        for hd in heads:
            total = h[(dirs[0], hd)]
            for d in dirs[1:]:
                total = total + h[(d, hd)]
            if dirs[0] == 0:
                hm_s[rows, hs[hd]] = total
            else:
                hm_s[rows, hs[hd]] = hm_s[rows, hs[hd]] + total
        if want_state:
            kw = {u: (kTc[u[1]].astype(F32) * row(wk, u)).astype(BF16) for u in units}
            upd = {u: _dot(kw[u], vaug[u[1]]) for u in units}
            for u in units:
                cst_s[idx[u]] = (upd[u] + row(decay, u) * cst_s[idx[u]]) if chained else upd[u]

    dir_rows = lax.broadcasted_iota(jnp.int32, (N_UNITS, 1), 0) >= N_HEADS
    for seq in range(n_seq):
        if has_state:
            n_cols = jnp.concatenate([n0_ref[0], jnp.zeros((LANES - N_UNITS, HEAD_DIM), F32)], axis=0).T
            first_lane = lax.broadcasted_iota(jnp.int32, (HEAD_DIM, HEAD_DIM), 1) == 0
            for idx in range(N_UNITS):
                cst_s[idx, :, :HEAD_DIM] = c0_ref[0, idx]
                cst_s[idx, :, HEAD_DIM:] = jnp.where(first_lane, n_cols[:, idx:idx + 1], 0.0)
            unit_row = lax.broadcasted_iota(jnp.int32, (N_UNITS, 1), 0)
            m_vec = jnp.zeros((N_UNITS, 1), F32)
            for idx in range(N_UNITS):
                m_vec = jnp.where(unit_row == idx, m0_ref[pl.program_id(0), idx], m_vec)
        else:
            m_vec = jnp.zeros((N_UNITS, 1), F32)
        if cps == 1:
            prep = gate_prep(seq, m_vec)
            unit_group([0, 1], seq, prep, True, emit_state)
            m_vec = prep[4]
        else:
            for d in range(2):
                order = list(range(cps)) if d == 0 else list(range(cps - 1, -1, -1))
                for pos, c in enumerate(order):
                    prep = gate_prep(seq * cps + c, m_vec)
                    unit_group([d], seq * cps + c, prep, pos == 0, emit_state or pos < cps - 1)
                    m_vec = jnp.where(dir_rows == (d == 1), prep[4], m_vec)
        if emit_state:
            for idx in range(N_UNITS):
                caug = cst_s[idx]
                cout_ref[0, seq * N_UNITS + idx] = caug[:, :HEAD_DIM]
                nout_ref[0, seq * N_UNITS + idx:seq * N_UNITS + idx + 1, :] = caug[:, HEAD_DIM:].T[0:1, :]
            mout_ref[0, seq * N_UNITS:(seq + 1) * N_UNITS, :] = jnp.broadcast_to(m_vec, (N_UNITS, LANES))

    e_iota = lax.broadcasted_iota(jnp.int32, (LANES, MIX_TM), 0)
    g_of_e = lax.shift_right_logical(e_iota, 2)
    j_of_e = lax.bitwise_and(e_iota, EXPERTS_PER_GROUP - 1)
    r8 = lax.broadcasted_iota(jnp.int32, (8, MIX_TM), 0)
    before_b = (lax.broadcasted_iota(jnp.int32, (MOE_BLK, MOE_BLK), 0)
                < lax.broadcasted_iota(jnp.int32, (MOE_BLK, MOE_BLK), 1)).astype(F32).astype(BF16)

    def phase3(i, carry):
        r0 = pl.multiple_of(i * MIX_TM, MIX_TM)
        rows = pl.ds(r0, MIX_TM)
        hm = hm_s[rows, :]
        heads = []
        for hd in range(N_HEADS):
            hh = hm[:, hd * HEAD_DIM:(hd + 1) * HEAD_DIM]
            heads.append(hh * lax.rsqrt(jnp.mean(hh * hh, axis=-1, keepdims=True) + EPS))
        hn = jnp.concatenate(heads, axis=1) * w["hng"][...]
        hb2 = (so_s[rows, :] * hn).astype(BF16)
        br_b = _dot(hb2, w["wmo"][...])
        mixed = (ma_s[rows, :] + sgb_s[rows, :] * br_b).astype(BF16)
        x1 = x_ref[0, rows, :] + mod_row(2) * _dot(mixed, w["wo"][...])
        x1_ref[0, rows, :] = x1
        xn = x1 * lax.rsqrt(jnp.mean(x1 * x1, axis=-1, keepdims=True) + EPS) * w["g2"][...]
        h2 = xn * (1.0 + mod_row(4)) + mod_row(3)
        h2_ref[0, rows, :] = h2.astype(BF16)

        h2_hi = h2.astype(BF16)
        h2_lo = (h2 - h2_hi.astype(F32)).astype(BF16)
        lg = _dot(h2_hi, w["wrt2"][...])
        lg = lg[:, :LANES] + lg[:, LANES:] + _dot(h2_lo, w["wrt2"][:, :LANES])
        lt = lg.T + w["brtT"][...]
        gl = [lt[N_EXPERTS + g:N_EXPERTS + g + 1, :] for g in range(N_GROUPS)]
        best, gsel = gl[0], jnp.zeros((1, MIX_TM), jnp.int32)
        for g in range(1, N_GROUPS):
            better = gl[g] > best
            gsel = jnp.where(better, g, gsel)
            best = jnp.where(better, gl[g], best)
        gp_sel = 1.0 / sum(jnp.exp(v - best) for v in gl)
        el = []
        for j in range(EXPERTS_PER_GROUP):
            v = lt[j:j + 1, :]
            for g in range(1, N_GROUPS):
                r = g * EXPERTS_PER_GROUP + j
                v = jnp.where(gsel == g, lt[r:r + 1, :], v)
            el.append(v)
        l1, e1 = el[0], jnp.zeros((1, MIX_TM), jnp.int32)
        for j in range(1, EXPERTS_PER_GROUP):
            better = el[j] > l1
            e1 = jnp.where(better, j, e1)
            l1 = jnp.where(better, el[j], l1)
        l2 = jnp.full((1, MIX_TM), -jnp.inf, F32)
        e2 = jnp.zeros((1, MIX_TM), jnp.int32)
        for j in range(EXPERTS_PER_GROUP):
            better = jnp.logical_and(e1 != j, el[j] > l2)
            e2 = jnp.where(better, j, e2)
            l2 = jnp.where(better, el[j], l2)
        r2 = jnp.exp(l2 - l1)
        wt1 = gp_sel / (1.0 + r2)
        wt2 = gp_sel * r2 / (1.0 + r2)
        in_group = g_of_e == gsel
        comb_t = (jnp.where(jnp.logical_and(in_group, j_of_e == e1), wt1, 0.0)
                  + jnp.where(jnp.logical_and(in_group, j_of_e == e2), wt2, 0.0))

        onehot = (r8 == gsel).astype(F32)
        gsel_f = gsel.astype(F32)
        rank = jnp.sum(onehot * _dot(onehot.astype(BF16), before_b), axis=0, keepdims=True)
        r8rows = pl.ds(pl.multiple_of(i * 8, 8), 8)
        route_ref[0, r8rows, :] = jnp.where(r8 == 0, gsel_f, jnp.where(r8 == 1, rank, 0.0))
        cnt_ref[0, r8rows, :] = jnp.broadcast_to(jnp.sum(onehot, axis=1, keepdims=True), (8, LANES))
        comb_t = jnp.where(e_iota == ROUTE_GROUP_LANE, gsel_f,
                           jnp.where(e_iota == ROUTE_RANK_LANE, rank, comb_t))
        comb_ref[0, rows, :] = comb_t.T
        return carry

    if n_mt == 1:
        phase3(0, 0)
    else:
        lax.fori_loop(0, n_mt, phase3, 0)


class _RowWindow(NamedTuple):
    array: jax.Array
    start: int
    n: int


def _const_spec(a):
    if isinstance(a, _RowWindow):
        assert a.start % a.n == 0
        return a.array, pl.BlockSpec((a.n, a.array.shape[1]), lambda b: (a.start // a.n, 0),
                                     pipeline_mode=pl.Buffered(1))
    nd = a.ndim
    return a, pl.BlockSpec(a.shape, lambda b, _nd=nd: (0,) * _nd, pipeline_mode=pl.Buffered(1))


def _mixer(x, T, mod, mod_index, weights, P, state=None, emit_state=False):
    B, R, _ = x.shape
    n_chunks = R // SUB
    n_blk = R // MOE_BLK
    n_seq = R // T
    has_state = state is not None
    seq_mode = {} if R <= MIX_TM else {"pipeline_mode": pl.Buffered(1)}
    in_specs = [
        pl.BlockSpec((1, R, D_MODEL), lambda b: (b, 0, 0), **seq_mode),
        pl.BlockSpec(mod.shape, lambda b: (0, 0, 0)),
    ]
    args = [x, mod]
    if has_state:
        c0, n0, m0 = state
        in_specs += [
            pl.BlockSpec((1, N_UNITS, HEAD_DIM, HEAD_DIM), lambda b: (b, 0, 0, 0)),
            pl.BlockSpec((1, N_UNITS, HEAD_DIM), lambda b: (b, 0, 0)),
            pl.BlockSpec(memory_space=pltpu.SMEM),
        ]
        args += [c0, n0, m0]
    for name in _MIXER_WEIGHTS:
        operand, spec = _const_spec(weights[name])
        in_specs.append(spec)
        args.append(operand)
    out_shape = [
        jax.ShapeDtypeStruct((B, R, D_MODEL), F32),
        jax.ShapeDtypeStruct((B, R, D_MODEL), BF16),
        jax.ShapeDtypeStruct((B, R, LANES), F32),
        jax.ShapeDtypeStruct((B, n_blk * 8, MOE_BLK), F32),
        jax.ShapeDtypeStruct((B, n_blk * 8, LANES), F32),
    ]
    out_specs = [
        pl.BlockSpec((1, R, D_MODEL), lambda b: (b, 0, 0), **seq_mode),
        pl.BlockSpec((1, R, D_MODEL), lambda b: (b, 0, 0), **seq_mode),
        pl.BlockSpec((1, R, LANES), lambda b: (b, 0, 0)),
        pl.BlockSpec((1, n_blk * 8, MOE_BLK), lambda b: (b, 0, 0)),
        pl.BlockSpec((1, n_blk * 8, LANES), lambda b: (b, 0, 0)),
    ]
    if emit_state:
        out_shape += [
            jax.ShapeDtypeStruct((B, n_seq * N_UNITS, HEAD_DIM, HEAD_DIM), F32),
            jax.ShapeDtypeStruct((B, n_seq * N_UNITS, HEAD_DIM), F32),
            jax.ShapeDtypeStruct((B, n_seq * N_UNITS, LANES), F32),
        ]
        out_specs += [
            pl.BlockSpec((1, n_seq * N_UNITS, HEAD_DIM, HEAD_DIM), lambda b: (b, 0, 0, 0)),
            pl.BlockSpec((1, n_seq * N_UNITS, HEAD_DIM), lambda b: (b, 0, 0)),
            pl.BlockSpec((1, n_seq * N_UNITS, LANES), lambda b: (b, 0, 0)),
        ]
    scratch = [
        pltpu.VMEM((R, D_MLSTM), BF16),
        pltpu.VMEM((n_chunks, D_MLSTM, SUB), BF16),
        pltpu.VMEM((R, D_MLSTM), BF16),
        pltpu.VMEM((R, D_MLSTM), F32),
        pltpu.VMEM((n_chunks, 5 * N_UNITS, SUB), F32),
        pltpu.VMEM((R, D_MODEL), F32),
        pltpu.VMEM((R, D_MODEL), F32),
        pltpu.VMEM((R, D_MLSTM), F32),
        pltpu.VMEM((N_UNITS, HEAD_DIM, 2 * HEAD_DIM), F32),
        pltpu.VMEM((MIX_TM // P, P + 2 * CONV_PAD, D_CONV), F32),
    ]
    return pl.pallas_call(
        functools.partial(_mixer_kernel, R, T, P, has_state, emit_state, mod_index),
        grid=(B,),
        in_specs=in_specs,
        out_specs=out_specs,
        out_shape=out_shape,
        scratch_shapes=scratch,
        compiler_params=pltpu.CompilerParams(
            dimension_semantics=("arbitrary",), vmem_limit_bytes=VMEM_LIMIT),
        name="mixer_T%d" % T,
    )(*args)


def _dest_in_block(group, rank, starts):
    dest = rank
    for g in range(N_GROUPS):
        dest = dest + jnp.where(group == float(g), starts[g], 0.0)
    return dest


def _copy_segments(src_refs, dst_refs, src_starts, dst_starts, n_pieces):
    def copy(g, first_piece, n_rows):
        s = pl.multiple_of(src_starts[g] + first_piece * ROW_ALIGN, ROW_ALIGN)
        d = pl.multiple_of(dst_starts[g] + first_piece * ROW_ALIGN, ROW_ALIGN)
        for src, dst in zip(src_refs, dst_refs):
            dst[pl.ds(d, n_rows), :] = src[pl.ds(s, n_rows), :]

    for g in range(N_GROUPS):
        n_runs = lax.shift_right_logical(n_pieces[g], COPY_RUN.bit_length() - 1)

        def run(k, carry, g=g):
            copy(g, k * COPY_RUN, COPY_RUN * ROW_ALIGN)
            return carry

        def single(k, carry, g=g):
            copy(g, k, ROW_ALIGN)
            return carry

        lax.fori_loop(0, n_runs, run, 0)
        lax.fori_loop(n_runs * COPY_RUN, n_pieces[g], single, 0)


def _plan_segments(n_blocks, n_tiles, count, start_ref, npiece_ref, off_ref, tgroup_ref, tvalid_ref):
    align_shift = ROW_ALIGN.bit_length() - 1
    tile_shift = MOE_TM.bit_length() - 1

    def block_starts(blk, carry):
        row = jnp.int32(0)
        for g in range(N_GROUPS):
            n = lax.shift_right_logical(count(blk, g) + (ROW_ALIGN - 1), align_shift)
            npiece_ref[blk * N_GROUPS + g] = n
            start_ref[blk * N_GROUPS + g] = row
            row = row + n * ROW_ALIGN
        return carry

    lax.fori_loop(0, n_blocks, block_starts, 0)

    base_row = jnp.int32(0)
    base_tile = jnp.int32(0)
    last_group = jnp.int32(0)
    for g in range(N_GROUPS):
        def seg_offsets(blk, row, g=g, base_row=base_row):
            off_ref[blk * N_GROUPS + g] = base_row + row
            return row + npiece_ref[blk * N_GROUPS + g] * ROW_ALIGN

        rows = lax.fori_loop(0, n_blocks, seg_offsets, jnp.int32(0))
        tiles = lax.shift_right_logical(rows + (MOE_TM - 1), tile_shift)

        def mark_tiles(t, carry, g=g, base_tile=base_tile):
            tgroup_ref[base_tile + t] = g
            tvalid_ref[base_tile + t] = 1
            return carry

        lax.fori_loop(0, tiles, mark_tiles, 0)
        last_group = jnp.where(tiles > 0, g, last_group)
        base_row = base_row + tiles * MOE_TM
        base_tile = base_tile + tiles

    def mark_unused(t, carry):
        tgroup_ref[t] = last_group
        tvalid_ref[t] = 0
        return carry

    lax.fori_loop(base_tile, n_tiles, mark_unused, 0)


def _dispatch_kernel(n_ctx_blocks, n_blocks, n_tiles,
                     h2c_ref, h2l_ref, cbc_ref, cbl_ref, rtc_ref, rtl_ref, cntc_ref, cntl_ref,
                     xs_ref, cs_ref, start_ref, npiece_ref, off_ref, tgroup_ref, tvalid_ref,
                     sx_s, sc_s):
    b = pl.program_id(0)
    is_ctx = b < n_ctx_blocks

    def count(blk, g):
        vc = cntc_ref[jnp.minimum(blk, n_ctx_blocks - 1), pl.ds(g, 1), pl.ds(0, 1)]
        vl = cntl_ref[jnp.maximum(blk - n_ctx_blocks, 0), pl.ds(g, 1), pl.ds(0, 1)]
        return jnp.where(blk < n_ctx_blocks, vc, vl)[0, 0].astype(jnp.int32)

    @pl.when(b == 0)
    def _():
        _plan_segments(n_blocks, n_tiles, count, start_ref, npiece_ref, off_ref, tgroup_ref, tvalid_ref)
        xs_ref[...] = jnp.zeros_like(xs_ref)
        cs_ref[...] = jnp.zeros_like(cs_ref)

    starts = [start_ref[b * N_GROUPS + g] for g in range(N_GROUPS)]

    def sort_block(h2_ref, cb_ref, rt_ref):
        h2 = h2_ref[0]
        cb = cb_ref[0]
        rt = rt_ref[0]
        dest = _dest_in_block(rt[0:1, :], rt[1:2, :], [s.astype(F32) for s in starts])
        row = lax.broadcasted_iota(jnp.int32, (SORT_ROWS, MOE_BLK), 0).astype(F32)
        perm = (row == dest).astype(F32).astype(BF16)
        cb_hi = cb.astype(BF16)
        cb_lo = (cb - cb_hi.astype(F32)).astype(BF16)
        sx_s[...] = _dot(perm, h2).astype(BF16)
        sc_s[...] = _dot(perm, jnp.concatenate([cb_hi, cb_lo], axis=1)).astype(BF16)

    pl.when(is_ctx)(functools.partial(sort_block, h2c_ref, cbc_ref, rtc_ref))
    pl.when(jnp.logical_not(is_ctx))(functools.partial(sort_block, h2l_ref, cbl_ref, rtl_ref))
    _copy_segments((sx_s, sc_s), (xs_ref, cs_ref), starts,
                   [off_ref[b * N_GROUPS + g] for g in range(N_GROUPS)],
                   [npiece_ref[b * N_GROUPS + g] for g in range(N_GROUPS)])


def _experts_kernel(tgroup_ref, tvalid_ref, xs_ref, cs_ref, wg_ref, wu_ref, wd_ref, ys_ref):
    i = pl.program_id(0)

    @pl.when(tvalid_ref[i] == 1)
    def _():
        x = xs_ref[...]
        comb = cs_ref[:, :LANES].astype(F32) + cs_ref[:, LANES:].astype(F32)
        lane = lax.broadcasted_iota(jnp.int32, comb.shape, 1)
        first = tgroup_ref[i] * EXPERTS_PER_GROUP
        acc = None
        for j in range(EXPERTS_PER_GROUP):
            gj = _dot(x, wg_ref[j].astype(BF16))
            uj = _dot(x, wu_ref[j].astype(BF16))
            cw = jnp.sum(jnp.where(lane == first + j, comb, 0.0), axis=1, keepdims=True)
            out = _dot((gj * _sigmoid(gj) * uj * cw).astype(BF16), wd_ref[j].astype(BF16))
            acc = out if acc is None else acc + out
        ys_ref[...] = acc.astype(BF16)

    @pl.when(tvalid_ref[i] == 0)
    def _():
        ys_ref[...] = jnp.zeros_like(ys_ref)


def _combine_kernel(n_ctx_blocks, blocks_per_lat_seq, start_ref, npiece_ref, off_ref,
                    x1c_ref, x1l_ref, cbc_ref, cbl_ref, ys_ref, mod_ref, gf_ref, yc_ref, yl_ref, loc_s):
    b = pl.program_id(0)
    is_ctx = b < n_ctx_blocks
    starts = [start_ref[b * N_GROUPS + g] for g in range(N_GROUPS)]
    @pl.when(b == 0)
    def _():
        loc_s[...] = jnp.zeros_like(loc_s)

    _copy_segments((ys_ref,), (loc_s,), [off_ref[b * N_GROUPS + g] for g in range(N_GROUPS)], starts,
                   [npiece_ref[b * N_GROUPS + g] for g in range(N_GROUPS)])
    def finish_block(x1_ref, cb_ref, y_ref, mrow):
        cb = cb_ref[0]
        dest = _dest_in_block(cb[:, ROUTE_GROUP_LANE:ROUTE_GROUP_LANE + 1],
                              cb[:, ROUTE_RANK_LANE:ROUTE_RANK_LANE + 1],
                              [s.astype(F32) for s in starts])
        col = lax.broadcasted_iota(jnp.int32, (MOE_BLK, SORT_ROWS), 1).astype(F32)
        unperm = (col == dest).astype(F32).astype(BF16)
        x2 = x1_ref[0] + mod_ref[N_ADA - 1, pl.ds(mrow, 1), :] * _dot(unperm, loc_s[...])
        y_ref[0] = x2 * lax.rsqrt(jnp.mean(x2 * x2, axis=-1, keepdims=True) + EPS) * gf_ref[...]

    lat_row = 1 + jnp.maximum(b - n_ctx_blocks, 0) // blocks_per_lat_seq
    pl.when(is_ctx)(functools.partial(finish_block, x1c_ref, cbc_ref, yc_ref, 0))
    pl.when(jnp.logical_not(is_ctx))(functools.partial(finish_block, x1l_ref, cbl_ref, yl_ref, lat_row))


def _moe(x1c, x1l, h2c, h2l, cbc, cbl, rtc, rtl, cntc, cntl, mod, blocks_per_lat_seq, wg, wu, wd, gf):
    nc, nl = x1c.shape[0], x1l.shape[0]
    nb = nc + nl
    n_rows_max = nb * MOE_BLK + nb * N_GROUPS * (ROW_ALIGN - 1) + N_GROUPS * (MOE_TM - ROW_ALIGN)
    n_tiles = -(-n_rows_max // MOE_TM)
    ns = n_tiles * MOE_TM

    cmap = lambda b, *_: (jnp.minimum(b, nc - 1), 0, 0)
    lmap = lambda b, *_: (jnp.maximum(b - nc, 0), 0, 0)
    whole = lambda *_: (0, 0)
    once = {"pipeline_mode": pl.Buffered(1)}
    arb = pltpu.CompilerParams(dimension_semantics=("arbitrary",), vmem_limit_bytes=VMEM_LIMIT)
    smem = pl.BlockSpec(memory_space=pltpu.SMEM)
    seg_i32 = jax.ShapeDtypeStruct((nb * N_GROUPS,), jnp.int32)
    tile_i32 = jax.ShapeDtypeStruct((n_tiles,), jnp.int32)

    xs, cs, start, npiece, off, tgroup, tvalid = pl.pallas_call(
        functools.partial(_dispatch_kernel, nc, nb, n_tiles),
        grid_spec=pltpu.PrefetchScalarGridSpec(
            num_scalar_prefetch=0, grid=(nb,),
            in_specs=[
                pl.BlockSpec((1, MOE_BLK, D_MODEL), cmap), pl.BlockSpec((1, MOE_BLK, D_MODEL), lmap),
                pl.BlockSpec((1, MOE_BLK, LANES), cmap), pl.BlockSpec((1, MOE_BLK, LANES), lmap),
                pl.BlockSpec((1, 8, MOE_BLK), cmap), pl.BlockSpec((1, 8, MOE_BLK), lmap),
                pl.BlockSpec(cntc.shape, lambda b: (0, 0, 0)), pl.BlockSpec(cntl.shape, lambda b: (0, 0, 0)),
            ],
            out_specs=[pl.BlockSpec((ns, D_MODEL), whole, **once), pl.BlockSpec((ns, 2 * LANES), whole, **once),
                       smem, smem, smem, smem, smem],
            scratch_shapes=[pltpu.VMEM((SORT_ROWS, D_MODEL), BF16), pltpu.VMEM((SORT_ROWS, 2 * LANES), BF16)],
        ),
        out_shape=[jax.ShapeDtypeStruct((ns, D_MODEL), BF16), jax.ShapeDtypeStruct((ns, 2 * LANES), BF16),
                   seg_i32, seg_i32, seg_i32, tile_i32, tile_i32],
        compiler_params=arb,
        name="moe_dispatch",
    )(h2c, h2l, cbc, cbl, rtc, rtl, cntc, cntl)

    wmap = lambda i, tg, tv: (tg[i], 0, 0)
    ys = pl.pallas_call(
        _experts_kernel,
        grid_spec=pltpu.PrefetchScalarGridSpec(
            num_scalar_prefetch=2, grid=(n_tiles,),
            in_specs=[
                pl.BlockSpec((MOE_TM, D_MODEL), lambda i, *_: (i, 0)),
                pl.BlockSpec((MOE_TM, 2 * LANES), lambda i, *_: (i, 0)),
                pl.BlockSpec((EXPERTS_PER_GROUP, D_MODEL, D_EXPERT), wmap),
                pl.BlockSpec((EXPERTS_PER_GROUP, D_MODEL, D_EXPERT), wmap),
                pl.BlockSpec((EXPERTS_PER_GROUP, D_EXPERT, D_MODEL), wmap),
            ],
            out_specs=pl.BlockSpec((MOE_TM, D_MODEL), lambda i, *_: (i, 0)),
        ),
        out_shape=jax.ShapeDtypeStruct((ns, D_MODEL), BF16),
        compiler_params=arb,
        name="moe_experts",
    )(tgroup, tvalid, xs, cs, wg, wu, wd)

    yc, yl = pl.pallas_call(
        functools.partial(_combine_kernel, nc, blocks_per_lat_seq),
        grid_spec=pltpu.PrefetchScalarGridSpec(
            num_scalar_prefetch=3, grid=(nb,),
            in_specs=[
                pl.BlockSpec((1, MOE_BLK, D_MODEL), cmap), pl.BlockSpec((1, MOE_BLK, D_MODEL), lmap),
                pl.BlockSpec((1, MOE_BLK, LANES), cmap), pl.BlockSpec((1, MOE_BLK, LANES), lmap),
                pl.BlockSpec((ns, D_MODEL), whole, **once),
                pl.BlockSpec(mod.shape, lambda *_: (0, 0, 0)),
                pl.BlockSpec((1, D_MODEL), whole),
            ],
            out_specs=[pl.BlockSpec((1, MOE_BLK, D_MODEL), cmap), pl.BlockSpec((1, MOE_BLK, D_MODEL), lmap)],
            scratch_shapes=[pltpu.VMEM((SORT_ROWS, D_MODEL), BF16)],
        ),
        out_shape=[jax.ShapeDtypeStruct((nc, MOE_BLK, D_MODEL), F32),
                   jax.ShapeDtypeStruct((nl, MOE_BLK, D_MODEL), F32)],
        compiler_params=arb,
        name="moe_combine",
    )(start, npiece, off, x1c, x1l, cbc, cbl, ys, mod, gf)
    return yc, yl


def _prep_weights(norm1_g, w_in, b_in, b_gates, w_dw, b_dw, conv_ln_g, conv_ln_b, w_conv_out,
                  mlstm_hn_g, w_mlstm_out, w_o, norm2_g, w_rg, b_rg, w_re, b_re):
    s_a = 2 * D_CONV
    s_q = s_a + D_MLSTM
    s_k = s_q + D_MLSTM
    s_v = s_k + D_MLSTM
    s_o = s_v + D_MLSTM
    s_g = s_o + 4 * N_HEADS
    row = lambda v: v.reshape(1, -1).astype(F32)
    w_t = w_in.T
    keep = [(0, s_q), (s_k, s_o), (s_g, w_in.shape[1])]
    halved = [(D_CONV, s_a), (s_v, s_o), (s_g, w_in.shape[1])]
    is_halved = lambda r: any(a <= r < b for a, b in halved)
    blocks = [r for a, b in keep for r in range(a, b, WPREP_ROWS)]
    wrow = _transpose_cast(w_t, blocks, [is_halved(r) for r in blocks])
    bias_scale = jnp.array([0.5 if is_halved(r) else 1.0 for a, b in keep for r in range(a, b)]
                           + [1.0] * D_MLSTM, F32)
    bg = (b_in[s_o:s_g] + b_gates.reshape(-1)).reshape(2, 2, N_HEADS).transpose(1, 0, 2).reshape(-1, 1)
    row_window = lambda start, n: _RowWindow(w_t, start, n)
    n_rt = N_EXPERTS + N_GROUPS
    wrt = jnp.pad(jnp.concatenate([w_re, w_rg], axis=1), ((0, 0), (0, LANES - n_rt)))
    wrt_hi = wrt.astype(BF16)
    wrt2 = jnp.concatenate([wrt_hi, (wrt - wrt_hi.astype(F32)).astype(BF16)], axis=1)
    brtT = jnp.pad(jnp.concatenate([b_re, b_rg]), (0, LANES - n_rt)).reshape(LANES, 1)
    return {
        "g1": row(norm1_g),
        "wrow": wrow, "brow": row(jnp.concatenate([b_in[a:b] for a, b in keep] + [b_in[s_q:s_k]]) * bias_scale),
        "wkT": row_window(s_q, D_MLSTM), "wgifT": row_window(s_o, 4 * N_HEADS), "bgifT": bg,
        "wdw": w_dw.astype(F32), "bdw": row(b_dw), "lng": row(conv_ln_g), "lnb": row(conv_ln_b),
        "wco": w_conv_out.astype(BF16), "hng": row(mlstm_hn_g), "wmo": w_mlstm_out.astype(BF16),
        "wo": w_o.astype(BF16), "g2": row(norm2_g), "wrt2": wrt2, "brtT": brtT,
    }


def kernel(x_prompt, x_sample, state_C, state_n, state_m, c, c_ctx, norm1_g, w_ada, b_ada, w_in, b_in, b_gates, w_dw, b_dw, conv_ln_g, conv_ln_b, w_conv_out, mlstm_hn_g, w_mlstm_out, w_o, norm2_g, w_rg, b_rg, w_re, b_re, w_e_gate, w_e_up, w_e_down, norm_final_g):
    B, S, _ = x_prompt.shape
    Bd, Sd, _ = x_sample.shape
    assert w_ada.shape[0] == 1, "single trunk layer"
    assert MIX_TM % S == 0 and S % SUB == 0 and Sd % MIX_TM == 0

    mod = _ada(c_ctx.reshape(1, -1), c, w_ada[0], b_ada[0].reshape(1, -1))

    wts = _prep_weights(norm1_g[0], w_in[0], b_in[0], b_gates[0], w_dw[0], b_dw[0], conv_ln_g[0],
                        conv_ln_b[0], w_conv_out[0], mlstm_hn_g[0], w_mlstm_out[0], w_o[0],
                        norm2_g[0], w_rg[0], b_rg[0], w_re[0], b_re[0])

    x1p, h2p, cbp, rtp, cntp, c_new, n_new, m_new = _mixer(
        x_prompt.reshape(B * S // MIX_TM, MIX_TM, D_MODEL), S, mod, lambda b: 0, wts, P=S, emit_state=True)

    state = (state_C[:, 0].reshape(Bd, N_UNITS, HEAD_DIM, HEAD_DIM), state_n[:, 0].reshape(Bd, N_UNITS, HEAD_DIM),
             state_m[:, 0].reshape(Bd, N_UNITS))
    x1s, h2s, cbs, rts, cnts = _mixer(x_sample, Sd, mod, lambda b: 1 + b, wts, P=GRID_W, state=state)

    nc, nl = B * S // MOE_BLK, Bd * Sd // MOE_BLK
    blk = lambda a, n: a.reshape(n, MOE_BLK, a.shape[-1])
    yp, ys = _moe(blk(x1p, nc), blk(x1s, nl), blk(h2p, nc), blk(h2s, nl), blk(cbp, nc), blk(cbs, nl),
                  rtp.reshape(nc, 8, MOE_BLK), rts.reshape(nl, 8, MOE_BLK),
                  cntp.reshape(nc, 8, LANES), cnts.reshape(nl, 8, LANES),
                  mod, Sd // MOE_BLK, w_e_gate[0], w_e_up[0], w_e_down[0], norm_final_g.reshape(1, -1))

    return (yp.reshape(B, S, D_MODEL), ys.reshape(Bd, Sd, D_MODEL),
            c_new.reshape(B, 1, 2, N_HEADS, HEAD_DIM, HEAD_DIM),
            n_new.reshape(B, 1, 2, N_HEADS, HEAD_DIM),
            m_new[:, :, 0].reshape(B, 1, 2, N_HEADS))
```

```python
import functools
from typing import NamedTuple

import jax
import jax.numpy as jnp
from jax import lax
from jax.experimental import pallas as pl
from jax.experimental.pallas import tpu as pltpu

D_MODEL = 1024
D_CONV = 512
CONV_K = 31
D_MLSTM = 512
N_HEADS = 4
HEAD_DIM = D_MLSTM // N_HEADS
N_GROUPS = 4
EXPERTS_PER_GROUP = 4
N_EXPERTS = N_GROUPS * EXPERTS_PER_GROUP
D_EXPERT = 256
N_ADA = 6
EPS = 1e-6
GRID_W = 64

LANES = 128
SUB = 256
CONV_PAD = 16
CONV_RB = 64
N_UNITS = 2 * N_HEADS
ROW_ALIGN = 16
CHAIN_SLACK = 1
COPY_RUN = 4
MOE_TM = 512
MIX_TM = 512
MOE_BLK = MIX_TM
SORT_ROWS = MOE_BLK + N_GROUPS * ROW_ALIGN
ADA_PER_STEP = 2
WPREP_ROWS = 512
ROUTE_GROUP_LANE = N_EXPERTS
ROUTE_RANK_LANE = N_EXPERTS + 1
VMEM_LIMIT = 58 * 1024 * 1024

BF16 = jnp.bfloat16
F32 = jnp.float32
NT_DIMS = (((1,), (1,)), ((), ()))


def _dot(a, b):
    return jnp.dot(a, b, preferred_element_type=F32)


def _dot_nt(a, b, precision=None):
    return lax.dot_general(a, b, NT_DIMS, preferred_element_type=F32, precision=precision)


def _sigmoid(x):
    return 0.5 * jnp.tanh(0.5 * x) + 0.5


def _sigmoid_of_half(xh):
    return 0.5 * jnp.tanh(xh) + 0.5


def _log_sigmoid(x):
    return jnp.minimum(x, 0.0) - jnp.log1p(jnp.exp(-jnp.abs(x)))


def _split3(x):
    hi = x.astype(BF16).astype(F32)
    r1 = x - hi
    mid = r1.astype(BF16).astype(F32)
    lo = (r1 - mid).astype(BF16).astype(F32)
    return hi, mid, lo


def _ada_kernel(cctx_ref, c_ref, w_ref, b_ref, o_ref):
    n = 1 + c_ref.shape[0]
    c = jnp.concatenate([cctx_ref[...], c_ref[...], jnp.zeros((8 - n, D_MODEL), F32)], axis=0)
    s = (c * _sigmoid(c)).astype(BF16)
    out = _dot(s, w_ref[...].astype(BF16)) + b_ref[...]
    for v in range(ADA_PER_STEP):
        o_ref[v] = out[:, v * D_MODEL:(v + 1) * D_MODEL]


def _ada(c_ctx, c, w_ada, b_ada):
    return pl.pallas_call(
        _ada_kernel,
        grid=(N_ADA // ADA_PER_STEP,),
        in_specs=[
            pl.BlockSpec(c_ctx.shape, lambda j: (0, 0)),
            pl.BlockSpec(c.shape, lambda j: (0, 0)),
            pl.BlockSpec((D_MODEL, ADA_PER_STEP * D_MODEL), lambda j: (0, j)),
            pl.BlockSpec((1, ADA_PER_STEP * D_MODEL), lambda j: (0, j)),
        ],
        out_specs=pl.BlockSpec((ADA_PER_STEP, 8, D_MODEL), lambda j: (j, 0, 0)),
        out_shape=jax.ShapeDtypeStruct((N_ADA, 8, D_MODEL), F32),
        compiler_params=pltpu.CompilerParams(dimension_semantics=("arbitrary",)),
        name="ada",
    )(c_ctx, c, w_ada, b_ada)


def _transpose_cast_kernel(starts_ref, halve_ref, wt_ref, o_ref):
    scale = jnp.where(halve_ref[pl.program_id(0)] == 1, 0.5, 1.0)
    o_ref[...] = (wt_ref[...] * scale).astype(BF16).T


def _transpose_cast(w_t, row_starts, halve):
    n, k = len(row_starts), w_t.shape[1]
    return pl.pallas_call(
        _transpose_cast_kernel,
        grid_spec=pltpu.PrefetchScalarGridSpec(
            num_scalar_prefetch=2, grid=(n,),
            in_specs=[pl.BlockSpec((pl.Element(WPREP_ROWS), pl.Element(k)), lambda j, starts, hv: (starts[j] * 8, 0))],
            out_specs=pl.BlockSpec((k, WPREP_ROWS), lambda j, starts, hv: (0, j)),
        ),
        out_shape=jax.ShapeDtypeStruct((k, n * WPREP_ROWS), BF16),
        compiler_params=pltpu.CompilerParams(dimension_semantics=("arbitrary",)),
        name="transpose_cast",
    )(jnp.array([r // 8 for r in row_starts], jnp.int32), jnp.array([int(h) for h in halve], jnp.int32), w_t)


WROW_OFFSET = {"wq": 2 * D_CONV, "wv": 2 * D_CONV + D_MLSTM, "wog": 2 * D_CONV + 2 * D_MLSTM,
               "wgm": 2 * D_CONV + 3 * D_MLSTM}
BROW_K_OFFSET = 2 * D_CONV + 3 * D_MLSTM + 2 * D_MODEL

_MIXER_WEIGHTS = (
    "g1", "wrow", "brow", "wkT", "wgifT", "bgifT", "wdw", "bdw", "lng", "lnb",
    "wco", "hng", "wmo", "wo", "g2", "wrt2", "brtT",
)


def _zero_after(x):
    bits = lax.bitcast_convert_type(x, jnp.uint32)
    bits = lax.shift_right_logical(lax.shift_right_logical(bits, jnp.uint32(16)), jnp.uint32(16))
    return lax.bitcast_convert_type(bits, F32)[0:1, :]


def _conv_block(upad_s, seg, base, cs, wdw_ref, bdw_ref, after=None):
    sub = 8
    first = CONV_PAD - CONV_K // 2
    acc = jnp.broadcast_to(bdw_ref[0:1, cs], (CONV_RB, LANES))
    for r in range(sub):
        z = None
        for a in range((CONV_K + first + sub - 1) // sub):
            j = sub * a + r - first
            if 0 <= j < CONV_K:
                lo = base + sub * a
                tap = wdw_ref[j:j + 1, cs] if after is None else wdw_ref[j:j + 1, cs] + after
                term = tap * upad_s[seg, lo:lo + CONV_RB + sub, cs]
                z = term if z is None else z + term
        acc = acc + z[r:r + CONV_RB, :]
    return acc


def _mixer_kernel(R, T, P, has_state, emit_state, mod_index, *refs):
    L = SUB
    n_mt = R // MIX_TM
    cpm = MIX_TM // L
    n_seq = R // T
    cps = T // L
    nseg = MIX_TM // P
    assert not has_state or n_seq == 1
    it = iter(refs)
    x_ref = next(it)
    mod_ref = next(it)
    if has_state:
        c0_ref = next(it)
        n0_ref = next(it)
        m0_ref = next(it)
    w = {name: next(it) for name in _MIXER_WEIGHTS}
    x1_ref = next(it)
    h2_ref = next(it)
    comb_ref = next(it)
    route_ref = next(it)
    cnt_ref = next(it)
    if emit_state:
        cout_ref = next(it)
        nout_ref = next(it)
        mout_ref = next(it)
    (q_s, kT_s, v_s, so_s, scan_s, ma_s, sgb_s, hm_s, cst_s, upad_s) = [next(it) for _ in range(10)]

    cond_row = mod_index(pl.program_id(0))

    def mod_row(i):
        return mod_ref[i, pl.ds(cond_row, 1), :]

    zpad = jnp.zeros((CONV_PAD, D_CONV), F32)
    for seg in range(nseg):
        upad_s[seg, 0:CONV_PAD, :] = zpad
        upad_s[seg, CONV_PAD + P:CONV_PAD + P + CONV_PAD, :] = zpad

    t_idx = lax.broadcasted_iota(jnp.int32, (L, L), 0)
    s_idx = lax.broadcasted_iota(jnp.int32, (L, L), 1)
    lower = s_idx <= t_idx
    upper = s_idx >= t_idx
    triu_b = upper.astype(F32).astype(BF16)
    lane_u = lax.broadcasted_iota(jnp.int32, (N_UNITS, L), 1)
    is_bwd = lax.broadcasted_iota(jnp.int32, (N_UNITS, L), 0) >= N_HEADS

    def gate_scan(g):
        gi, lf = g[:N_UNITS], _log_sigmoid(g[N_UNITS:])
        pr = _dot(jnp.concatenate(_split3(lf), axis=0).astype(BF16), triu_b)
        pre = pr[0:N_UNITS] + pr[N_UNITS:2 * N_UNITS] + pr[2 * N_UNITS:]
        tot = pre[:, L - 1:L]
        bsum = jnp.where(is_bwd, tot - pre + lf, pre)
        a = gi - bsum
        pm, sm, k = a, a, 1
        while k < L:
            pm = jnp.where(lane_u >= k, jnp.maximum(pm, pltpu.roll(pm, k, axis=1)), pm)
            sm = jnp.where(lane_u < L - k, jnp.maximum(sm, pltpu.roll(sm, L - k, axis=1)), sm)
            k *= 2
        wide = lambda v: jnp.broadcast_to(v, (N_UNITS, L))
        return jnp.concatenate([a, jnp.where(is_bwd, sm, pm), bsum, wide(tot),
                                wide(jnp.max(a, axis=1, keepdims=True))], axis=0)

    def phase1(i, carry):
        r0 = pl.multiple_of(i * MIX_TM, MIX_TM)
        rows = pl.ds(r0, MIX_TM)
        x = x_ref[0, rows, :]
        xn = x * lax.rsqrt(jnp.mean(x * x, axis=-1, keepdims=True) + EPS) * w["g1"][...]
        hb = (xn * (1.0 + mod_row(1)) + mod_row(0)).astype(BF16)

        gates = _dot_nt(w["wgifT"][...].astype(BF16), hb)
        gates = jnp.concatenate([gates[d * 2 * N_HEADS + g * N_HEADS:d * 2 * N_HEADS + (g + 1) * N_HEADS]
                                 for g in range(2) for d in range(2)], axis=0) + w["bgifT"][...]
        for j in range(cpm):
            scan_s[i * cpm + j] = gate_scan(gates[:, j * L:(j + 1) * L])
        ag = _dot(hb, w["wrow"][:, :2 * D_CONV]) + w["brow"][:, :2 * D_CONV]
        u = ag[:, :D_CONV] * _sigmoid_of_half(ag[:, D_CONV:])
        for seg in range(nseg):
            upad_s[seg, CONV_PAD:CONV_PAD + P, :] = u[seg * P:(seg + 1) * P, :]

        def proj(name, c0, width=2 * LANES):
            w0 = WROW_OFFSET[name] + c0
            return _dot(hb, w["wrow"][:, w0:w0 + width]) + w["brow"][:, w0:w0 + width]

        last = lambda z: z[-8:, -LANES:]
        bk_row = w["brow"][:, BROW_K_OFFSET:BROW_K_OFFSET + D_MLSTM]
        bk_col = jnp.concatenate([bk_row, jnp.zeros((LANES - 1, D_MLSTM), F32)], axis=0).T[:, 0:1]

        def gm_a(c0):
            z = proj("wgm", c0)
            ma_s[rows, c0:c0 + 2 * LANES] = _sigmoid_of_half(z)
            return last(z)

        def gm_b(c0):
            z = proj("wgm", D_MODEL + c0)
            sgb_s[rows, c0:c0 + 2 * LANES] = _sigmoid_of_half(z)
            return last(z)

        def q_part(c0):
            z = proj("wq", c0)
            q_s[rows, c0:c0 + 2 * LANES] = (z * (HEAD_DIM ** -0.5)).astype(BF16)
            return last(z)

        def v_part(c0):
            z = proj("wv", c0)
            v_s[rows, c0:c0 + 2 * LANES] = z.astype(BF16)
            return last(z)

        def o_part(c0):
            z = proj("wog", c0)
            so_s[rows, c0:c0 + 2 * LANES] = _sigmoid_of_half(z)
            return last(z)

        def k_part(c0):
            rs = slice(c0, c0 + 2 * LANES)
            z = _dot_nt(w["wkT"][rs, :].astype(BF16), hb) + bk_col[rs, :]
            kt = z.astype(BF16)
            for j in range(cpm):
                kT_s[i * cpm + j, rs, :] = kt[:, j * L:(j + 1) * L]
            return last(z)

        jobs = ([functools.partial(gm_a, c0) for c0 in range(0, D_MODEL, 2 * LANES)]
                + [functools.partial(gm_b, c0) for c0 in range(0, D_MODEL, 2 * LANES)]
                + [functools.partial(f, c0) for f in (q_part, k_part, v_part, o_part)
                   for c0 in range(0, D_MLSTM, 2 * LANES)])
        n_jobs = len(jobs)
        conv = {}
        after, lag = None, [None] * CHAIN_SLACK
        n_pieces = (D_CONV // LANES) * nseg * (P // CONV_RB)
        for cb in range(D_CONV // LANES):
            cs = slice(cb * LANES, (cb + 1) * LANES)
            for seg in range(nseg):
                for rb in range(P // CONV_RB):
                    blk = _conv_block(upad_s, seg, rb * CONV_RB, cs, w["wdw"], w["bdw"], after)
                    conv[(cb, seg, rb)] = blk
                    if jobs and len(conv) * n_jobs >= (n_jobs - len(jobs) + 1) * n_pieces:
                        lag.append(_zero_after(jobs.pop(0)()))
                        after = lag.pop(0)
        for job in jobs:
            job()
        cu = jnp.concatenate(
            [jnp.concatenate([conv[(cb, seg, rb)] for seg in range(nseg) for rb in range(P // CONV_RB)], axis=0)
             for cb in range(D_CONV // LANES)], axis=1)
        mu = jnp.mean(cu, axis=-1, keepdims=True)
        cc = cu - mu
        cn = cc * lax.rsqrt(jnp.mean(cc * cc, axis=-1, keepdims=True) + EPS) * w["lng"][...] + w["lnb"][...]
        ca = (cn * _sigmoid(cn)).astype(BF16)
        ma_s[rows, :] = ma_s[rows, :] * _dot(ca, w["wco"][...])
        return carry

    if n_mt == 1:
        phase1(0, 0)
    else:
        lax.fori_loop(0, n_mt, phase1, 0)

    ones_col = (lax.broadcasted_iota(jnp.int32, (L, HEAD_DIM), 1) == 0).astype(F32).astype(BF16)
    pad_rows = jnp.zeros((LANES - 3 * N_UNITS, L), F32)

    def gate_prep(c, m_vec):
        sc = scan_s[c]
        a, run_max, bsum = sc[0:N_UNITS], sc[N_UNITS:2 * N_UNITS], sc[2 * N_UNITS:3 * N_UNITS]
        tot, a_max = sc[3 * N_UNITS:4 * N_UNITS, 0:1], sc[4 * N_UNITS:5 * N_UNITS, 0:1]
        big_m = jnp.maximum(m_vec, run_max)
        m_end = jnp.maximum(m_vec, a_max)
        cols = jnp.concatenate(
            [big_m, jnp.exp(m_vec - big_m), jnp.exp(-bsum - big_m), pad_rows], axis=0).T
        return a, cols, jnp.exp(a - m_end), jnp.exp(m_vec - m_end), tot + m_end

    def unit_group(dirs, c, prep, first_chunk, want_state):
        a, cols, wk, decay, _ = prep
        rows = slice(c * L, (c + 1) * L)
        heads = range(N_HEADS)
        units = [(d, hd) for d in dirs for hd in heads]
        hs = [slice(hd * HEAD_DIM, (hd + 1) * HEAD_DIM) for hd in heads]
        idx = {u: u[0] * N_HEADS + u[1] for u in units}
        col = lambda k, u: cols[:, k * N_UNITS + idx[u]:k * N_UNITS + idx[u] + 1]
        row = lambda arr, u: arr[idx[u]:idx[u] + 1, :]
        chained = has_state or not first_chunk
        qc = [q_s[rows, hs[hd]] for hd in heads]
        kTc = [kT_s[c, hs[hd], :] for hd in heads]
        vaug = [jnp.concatenate([v_s[rows, hs[hd]], ones_col], axis=1) for hd in heads]
        qk = [_dot(qc[hd], kTc[hd]) for hd in heads]
        s_mat = {u: (qk[u[1]] * jnp.where(lower if u[0] == 0 else upper, jnp.exp(row(a, u) - col(0, u)), 0.0)
                     ).astype(BF16) for u in units}
        nd = {u: _dot(s_mat[u], vaug[u[1]]) for u in units}
        if chained:
            nd = {u: nd[u] + col(1, u) * _dot(qc[u[1]], cst_s[idx[u]].astype(BF16)) for u in units}
        h = {u: nd[u][:, :HEAD_DIM] * (1.0 / jnp.maximum(jnp.abs(nd[u][:, HEAD_DIM:HEAD_DIM + 1]), col(2, u)))
             for u in units}
        for hd in heads:
            total = h[(dirs[0], hd)]
            for d in dirs[1:]:
                total = total + h[(d, hd)]
            if dirs[0] == 0:
                hm_s[rows, hs[hd]] = total
            else:
                hm_s[rows, hs[hd]] = hm_s[rows, hs[hd]] + total
        if want_state:
            kw = {u: (kTc[u[1]].astype(F32) * row(wk, u)).astype(BF16) for u in units}
            upd = {u: _dot(kw[u], vaug[u[1]]) for u in units}
            for u in units:
                cst_s[idx[u]] = (upd[u] + row(decay, u) * cst_s[idx[u]]) if chained else upd[u]

    dir_rows = lax.broadcasted_iota(jnp.int32, (N_UNITS, 1), 0) >= N_HEADS
    for seq in range(n_seq):
        if has_state:
            n_cols = jnp.concatenate([n0_ref[0], jnp.zeros((LANES - N_UNITS, HEAD_DIM), F32)], axis=0).T
            first_lane = lax.broadcasted_iota(jnp.int32, (HEAD_DIM, HEAD_DIM), 1) == 0
            for idx in range(N_UNITS):
                cst_s[idx, :, :HEAD_DIM] = c0_ref[0, idx]
                cst_s[idx, :, HEAD_DIM:] = jnp.where(first_lane, n_cols[:, idx:idx + 1], 0.0)
            unit_row = lax.broadcasted_iota(jnp.int32, (N_UNITS, 1), 0)
            m_vec = jnp.zeros((N_UNITS, 1), F32)
            for idx in range(N_UNITS):
                m_vec = jnp.where(unit_row == idx, m0_ref[pl.program_id(0), idx], m_vec)
        else:
            m_vec = jnp.zeros((N_UNITS, 1), F32)
        if cps == 1:
            prep = gate_prep(seq, m_vec)
            unit_group([0, 1], seq, prep, True, emit_state)
            m_vec = prep[4]
        else:
            for d in range(2):
                order = list(range(cps)) if d == 0 else list(range(cps - 1, -1, -1))
                for pos, c in enumerate(order):
                    prep = gate_prep(seq * cps + c, m_vec)
                    unit_group([d], seq * cps + c, prep, pos == 0, emit_state or pos < cps - 1)
                    m_vec = jnp.where(dir_rows == (d == 1), prep[4], m_vec)
        if emit_state:
            for idx in range(N_UNITS):
                caug = cst_s[idx]
                cout_ref[0, seq * N_UNITS + idx] = caug[:, :HEAD_DIM]
                nout_ref[0, seq * N_UNITS + idx:seq * N_UNITS + idx + 1, :] = caug[:, HEAD_DIM:].T[0:1, :]
            mout_ref[0, seq * N_UNITS:(seq + 1) * N_UNITS, :] = jnp.broadcast_to(m_vec, (N_UNITS, LANES))

    e_iota = lax.broadcasted_iota(jnp.int32, (LANES, MIX_TM), 0)
    g_of_e = lax.shift_right_logical(e_iota, 2)
    j_of_e = lax.bitwise_and(e_iota, EXPERTS_PER_GROUP - 1)
    r8 = lax.broadcasted_iota(jnp.int32, (8, MIX_TM), 0)
    before_b = (lax.broadcasted_iota(jnp.int32, (MOE_BLK, MOE_BLK), 0)
                < lax.broadcasted_iota(jnp.int32, (MOE_BLK, MOE_BLK), 1)).astype(F32).astype(BF16)

    def phase3(i, carry):
        r0 = pl.multiple_of(i * MIX_TM, MIX_TM)
        rows = pl.ds(r0, MIX_TM)
        hm = hm_s[rows, :]
        heads = []
        for hd in range(N_HEADS):
            hh = hm[:, hd * HEAD_DIM:(hd + 1) * HEAD_DIM]
            heads.append(hh * lax.rsqrt(jnp.mean(hh * hh, axis=-1, keepdims=True) + EPS))
        hn = jnp.concatenate(heads, axis=1) * w["hng"][...]
        hb2 = (so_s[rows, :] * hn).astype(BF16)
        br_b = _dot(hb2, w["wmo"][...])
        mixed = (ma_s[rows, :] + sgb_s[rows, :] * br_b).astype(BF16)
        x1 = x_ref[0, rows, :] + mod_row(2) * _dot(mixed, w["wo"][...])
        x1_ref[0, rows, :] = x1
        xn = x1 * lax.rsqrt(jnp.mean(x1 * x1, axis=-1, keepdims=True) + EPS) * w["g2"][...]
        h2 = xn * (1.0 + mod_row(4)) + mod_row(3)
        h2_ref[0, rows, :] = h2.astype(BF16)

        h2_hi = h2.astype(BF16)
        h2_lo = (h2 - h2_hi.astype(F32)).astype(BF16)
        lg = _dot(h2_hi, w["wrt2"][...])
        lg = lg[:, :LANES] + lg[:, LANES:] + _dot(h2_lo, w["wrt2"][:, :LANES])
        lt = lg.T + w["brtT"][...]
        gl = [lt[N_EXPERTS + g:N_EXPERTS + g + 1, :] for g in range(N_GROUPS)]
        best, gsel = gl[0], jnp.zeros((1, MIX_TM), jnp.int32)
        for g in range(1, N_GROUPS):
            better = gl[g] > best
            gsel = jnp.where(better, g, gsel)
            best = jnp.where(better, gl[g], best)
        gp_sel = 1.0 / sum(jnp.exp(v - best) for v in gl)
        el = []
        for j in range(EXPERTS_PER_GROUP):
            v = lt[j:j + 1, :]
            for g in range(1, N_GROUPS):
                r = g * EXPERTS_PER_GROUP + j
                v = jnp.where(gsel == g, lt[r:r + 1, :], v)
            el.append(v)
        l1, e1 = el[0], jnp.zeros((1, MIX_TM), jnp.int32)
        for j in range(1, EXPERTS_PER_GROUP):
            better = el[j] > l1
            e1 = jnp.where(better, j, e1)
            l1 = jnp.where(better, el[j], l1)
        l2 = jnp.full((1, MIX_TM), -jnp.inf, F32)
        e2 = jnp.zeros((1, MIX_TM), jnp.int32)
        for j in range(EXPERTS_PER_GROUP):
            better = jnp.logical_and(e1 != j, el[j] > l2)
            e2 = jnp.where(better, j, e2)
            l2 = jnp.where(better, el[j], l2)
        r2 = jnp.exp(l2 - l1)
        wt1 = gp_sel / (1.0 + r2)
        wt2 = gp_sel * r2 / (1.0 + r2)
        in_group = g_of_e == gsel
        comb_t = (jnp.where(jnp.logical_and(in_group, j_of_e == e1), wt1, 0.0)
                  + jnp.where(jnp.logical_and(in_group, j_of_e == e2), wt2, 0.0))

        onehot = (r8 == gsel).astype(F32)
        gsel_f = gsel.astype(F32)
        rank = jnp.sum(onehot * _dot(onehot.astype(BF16), before_b), axis=0, keepdims=True)
        r8rows = pl.ds(pl.multiple_of(i * 8, 8), 8)
        route_ref[0, r8rows, :] = jnp.where(r8 == 0, gsel_f, jnp.where(r8 == 1, rank, 0.0))
        cnt_ref[0, r8rows, :] = jnp.broadcast_to(jnp.sum(onehot, axis=1, keepdims=True), (8, LANES))
        comb_t = jnp.where(e_iota == ROUTE_GROUP_LANE, gsel_f,
                           jnp.where(e_iota == ROUTE_RANK_LANE, rank, comb_t))
        comb_ref[0, rows, :] = comb_t.T
        return carry

    if n_mt == 1:
        phase3(0, 0)
    else:
        lax.fori_loop(0, n_mt, phase3, 0)


class _RowWindow(NamedTuple):
    array: jax.Array
    start: int
    n: int


def _const_spec(a):
    if isinstance(a, _RowWindow):
        assert a.start % a.n == 0
        return a.array, pl.BlockSpec((a.n, a.array.shape[1]), lambda b: (a.start // a.n, 0),
                                     pipeline_mode=pl.Buffered(1))
    nd = a.ndim
    return a, pl.BlockSpec(a.shape, lambda b, _nd=nd: (0,) * _nd, pipeline_mode=pl.Buffered(1))


def _mixer(x, T, mod, mod_index, weights, P, state=None, emit_state=False):
    B, R, _ = x.shape
    n_chunks = R // SUB
    n_blk = R // MOE_BLK
    n_seq = R // T
    has_state = state is not None
    seq_mode = {} if R <= MIX_TM else {"pipeline_mode": pl.Buffered(1)}
    in_specs = [
        pl.BlockSpec((1, R, D_MODEL), lambda b: (b, 0, 0), **seq_mode),
        pl.BlockSpec(mod.shape, lambda b: (0, 0, 0)),
    ]
    args = [x, mod]
    if has_state:
        c0, n0, m0 = state
        in_specs += [
            pl.BlockSpec((1, N_UNITS, HEAD_DIM, HEAD_DIM), lambda b: (b, 0, 0, 0)),
            pl.BlockSpec((1, N_UNITS, HEAD_DIM), lambda b: (b, 0, 0)),
            pl.BlockSpec(memory_space=pltpu.SMEM),
        ]
        args += [c0, n0, m0]
    for name in _MIXER_WEIGHTS:
        operand, spec = _const_spec(weights[name])
        in_specs.append(spec)
        args.append(operand)
    out_shape = [
        jax.ShapeDtypeStruct((B, R, D_MODEL), F32),
        jax.ShapeDtypeStruct((B, R, D_MODEL), BF16),
        jax.ShapeDtypeStruct((B, R, LANES), F32),
        jax.ShapeDtypeStruct((B, n_blk * 8, MOE_BLK), F32),
        jax.ShapeDtypeStruct((B, n_blk * 8, LANES), F32),
    ]
    out_specs = [
        pl.BlockSpec((1, R, D_MODEL), lambda b: (b, 0, 0), **seq_mode),
        pl.BlockSpec((1, R, D_MODEL), lambda b: (b, 0, 0), **seq_mode),
        pl.BlockSpec((1, R, LANES), lambda b: (b, 0, 0)),
        pl.BlockSpec((1, n_blk * 8, MOE_BLK), lambda b: (b, 0, 0)),
        pl.BlockSpec((1, n_blk * 8, LANES), lambda b: (b, 0, 0)),
    ]
    if emit_state:
        out_shape += [
            jax.ShapeDtypeStruct((B, n_seq * N_UNITS, HEAD_DIM, HEAD_DIM), F32),
            jax.ShapeDtypeStruct((B, n_seq * N_UNITS, HEAD_DIM), F32),
            jax.ShapeDtypeStruct((B, n_seq * N_UNITS, LANES), F32),
        ]
        out_specs += [
            pl.BlockSpec((1, n_seq * N_UNITS, HEAD_DIM, HEAD_DIM), lambda b: (b, 0, 0, 0)),
            pl.BlockSpec((1, n_seq * N_UNITS, HEAD_DIM), lambda b: (b, 0, 0)),
            pl.BlockSpec((1, n_seq * N_UNITS, LANES), lambda b: (b, 0, 0)),
        ]
    scratch = [
        pltpu.VMEM((R, D_MLSTM), BF16),
        pltpu.VMEM((n_chunks, D_MLSTM, SUB), BF16),
        pltpu.VMEM((R, D_MLSTM), BF16),
        pltpu.VMEM((R, D_MLSTM), F32),
        pltpu.VMEM((n_chunks, 5 * N_UNITS, SUB), F32),
        pltpu.VMEM((R, D_MODEL), F32),
        pltpu.VMEM((R, D_MODEL), F32),
        pltpu.VMEM((R, D_MLSTM), F32),
        pltpu.VMEM((N_UNITS, HEAD_DIM, 2 * HEAD_DIM), F32),
        pltpu.VMEM((MIX_TM // P, P + 2 * CONV_PAD, D_CONV), F32),
    ]
    return pl.pallas_call(
        functools.partial(_mixer_kernel, R, T, P, has_state, emit_state, mod_index),
        grid=(B,),
        in_specs=in_specs,
        out_specs=out_specs,
        out_shape=out_shape,
        scratch_shapes=scratch,
        compiler_params=pltpu.CompilerParams(
            dimension_semantics=("arbitrary",), vmem_limit_bytes=VMEM_LIMIT),
        name="mixer_T%d" % T,
    )(*args)


def _dest_in_block(group, rank, starts):
    dest = rank
    for g in range(N_GROUPS):
        dest = dest + jnp.where(group == float(g), starts[g], 0.0)
    return dest


def _copy_segments(src_refs, dst_refs, src_starts, dst_starts, n_pieces):
    def copy(g, first_piece, n_rows):
        s = pl.multiple_of(src_starts[g] + first_piece * ROW_ALIGN, ROW_ALIGN)
        d = pl.multiple_of(dst_starts[g] + first_piece * ROW_ALIGN, ROW_ALIGN)
        for src, dst in zip(src_refs, dst_refs):
            dst[pl.ds(d, n_rows), :] = src[pl.ds(s, n_rows), :]

    for g in range(N_GROUPS):
        n_runs = lax.shift_right_logical(n_pieces[g], COPY_RUN.bit_length() - 1)

        def run(k, carry, g=g):
            copy(g, k * COPY_RUN, COPY_RUN * ROW_ALIGN)
            return carry

        def single(k, carry, g=g):
            copy(g, k, ROW_ALIGN)
            return carry

        lax.fori_loop(0, n_runs, run, 0)
        lax.fori_loop(n_runs * COPY_RUN, n_pieces[g], single, 0)


def _plan_segments(n_blocks, n_tiles, count, start_ref, npiece_ref, off_ref, tgroup_ref, tvalid_ref):
    align_shift = ROW_ALIGN.bit_length() - 1
    tile_shift = MOE_TM.bit_length() - 1

    def block_starts(blk, carry):
        row = jnp.int32(0)
        for g in range(N_GROUPS):
            n = lax.shift_right_logical(count(blk, g) + (ROW_ALIGN - 1), align_shift)
            npiece_ref[blk * N_GROUPS + g] = n
            start_ref[blk * N_GROUPS + g] = row
            row = row + n * ROW_ALIGN
        return carry

    lax.fori_loop(0, n_blocks, block_starts, 0)

    base_row = jnp.int32(0)
    base_tile = jnp.int32(0)
    last_group = jnp.int32(0)
    for g in range(N_GROUPS):
        def seg_offsets(blk, row, g=g, base_row=base_row):
            off_ref[blk * N_GROUPS + g] = base_row + row
            return row + npiece_ref[blk * N_GROUPS + g] * ROW_ALIGN

        rows = lax.fori_loop(0, n_blocks, seg_offsets, jnp.int32(0))
        tiles = lax.shift_right_logical(rows + (MOE_TM - 1), tile_shift)

        def mark_tiles(t, carry, g=g, base_tile=base_tile):
            tgroup_ref[base_tile + t] = g
            tvalid_ref[base_tile + t] = 1
            return carry

        lax.fori_loop(0, tiles, mark_tiles, 0)
        last_group = jnp.where(tiles > 0, g, last_group)
        base_row = base_row + tiles * MOE_TM
        base_tile = base_tile + tiles

    def mark_unused(t, carry):
        tgroup_ref[t] = last_group
        tvalid_ref[t] = 0
        return carry

    lax.fori_loop(base_tile, n_tiles, mark_unused, 0)


def _dispatch_kernel(n_ctx_blocks, n_blocks, n_tiles,
                     h2c_ref, h2l_ref, cbc_ref, cbl_ref, rtc_ref, rtl_ref, cntc_ref, cntl_ref,
                     xs_ref, cs_ref, start_ref, npiece_ref, off_ref, tgroup_ref, tvalid_ref,
                     sx_s, sc_s):
    b = pl.program_id(0)
    is_ctx = b < n_ctx_blocks

    def count(blk, g):
        vc = cntc_ref[jnp.minimum(blk, n_ctx_blocks - 1), pl.ds(g, 1), pl.ds(0, 1)]
        vl = cntl_ref[jnp.maximum(blk - n_ctx_blocks, 0), pl.ds(g, 1), pl.ds(0, 1)]
        return jnp.where(blk < n_ctx_blocks, vc, vl)[0, 0].astype(jnp.int32)

    @pl.when(b == 0)
    def _():
        _plan_segments(n_blocks, n_tiles, count, start_ref, npiece_ref, off_ref, tgroup_ref, tvalid_ref)
        xs_ref[...] = jnp.zeros_like(xs_ref)
        cs_ref[...] = jnp.zeros_like(cs_ref)

    starts = [start_ref[b * N_GROUPS + g] for g in range(N_GROUPS)]

    def sort_block(h2_ref, cb_ref, rt_ref):
        h2 = h2_ref[0]
        cb = cb_ref[0]
        rt = rt_ref[0]
        dest = _dest_in_block(rt[0:1, :], rt[1:2, :], [s.astype(F32) for s in starts])
        row = lax.broadcasted_iota(jnp.int32, (SORT_ROWS, MOE_BLK), 0).astype(F32)
        perm = (row == dest).astype(F32).astype(BF16)
        cb_hi = cb.astype(BF16)
        cb_lo = (cb - cb_hi.astype(F32)).astype(BF16)
        sx_s[...] = _dot(perm, h2).astype(BF16)
        sc_s[...] = _dot(perm, jnp.concatenate([cb_hi, cb_lo], axis=1)).astype(BF16)

    pl.when(is_ctx)(functools.partial(sort_block, h2c_ref, cbc_ref, rtc_ref))
    pl.when(jnp.logical_not(is_ctx))(functools.partial(sort_block, h2l_ref, cbl_ref, rtl_ref))
    _copy_segments((sx_s, sc_s), (xs_ref, cs_ref), starts,
                   [off_ref[b * N_GROUPS + g] for g in range(N_GROUPS)],
                   [npiece_ref[b * N_GROUPS + g] for g in range(N_GROUPS)])


def _experts_kernel(tgroup_ref, tvalid_ref, xs_ref, cs_ref, wg_ref, wu_ref, wd_ref, ys_ref):
    i = pl.program_id(0)

    @pl.when(tvalid_ref[i] == 1)
    def _():
        x = xs_ref[...]
        comb = cs_ref[:, :LANES].astype(F32) + cs_ref[:, LANES:].astype(F32)
        lane = lax.broadcasted_iota(jnp.int32, comb.shape, 1)
        first = tgroup_ref[i] * EXPERTS_PER_GROUP
        acc = None
        for j in range(EXPERTS_PER_GROUP):
            gj = _dot(x, wg_ref[j].astype(BF16))
            uj = _dot(x, wu_ref[j].astype(BF16))
            cw = jnp.sum(jnp.where(lane == first + j, comb, 0.0), axis=1, keepdims=True)
            out = _dot((gj * _sigmoid(gj) * uj * cw).astype(BF16), wd_ref[j].astype(BF16))
            acc = out if acc is None else acc + out
        ys_ref[...] = acc.astype(BF16)

    @pl.when(tvalid_ref[i] == 0)
    def _():
        ys_ref[...] = jnp.zeros_like(ys_ref)


def _combine_kernel(n_ctx_blocks, blocks_per_lat_seq, start_ref, npiece_ref, off_ref,
                    x1c_ref, x1l_ref, cbc_ref, cbl_ref, ys_ref, mod_ref, gf_ref, yc_ref, yl_ref, loc_s):
    b = pl.program_id(0)
    is_ctx = b < n_ctx_blocks
    starts = [start_ref[b * N_GROUPS + g] for g in range(N_GROUPS)]
    @pl.when(b == 0)
    def _():
        loc_s[...] = jnp.zeros_like(loc_s)

    _copy_segments((ys_ref,), (loc_s,), [off_ref[b * N_GROUPS + g] for g in range(N_GROUPS)], starts,
                   [npiece_ref[b * N_GROUPS + g] for g in range(N_GROUPS)])
    def finish_block(x1_ref, cb_ref, y_ref, mrow):
        cb = cb_ref[0]
        dest = _dest_in_block(cb[:, ROUTE_GROUP_LANE:ROUTE_GROUP_LANE + 1],
                              cb[:, ROUTE_RANK_LANE:ROUTE_RANK_LANE + 1],
                              [s.astype(F32) for s in starts])
        col = lax.broadcasted_iota(jnp.int32, (MOE_BLK, SORT_ROWS), 1).astype(F32)
        unperm = (col == dest).astype(F32).astype(BF16)
        x2 = x1_ref[0] + mod_ref[N_ADA - 1, pl.ds(mrow, 1), :] * _dot(unperm, loc_s[...])
        y_ref[0] = x2 * lax.rsqrt(jnp.mean(x2 * x2, axis=-1, keepdims=True) + EPS) * gf_ref[...]

    lat_row = 1 + jnp.maximum(b - n_ctx_blocks, 0) // blocks_per_lat_seq
    pl.when(is_ctx)(functools.partial(finish_block, x1c_ref, cbc_ref, yc_ref, 0))
    pl.when(jnp.logical_not(is_ctx))(functools.partial(finish_block, x1l_ref, cbl_ref, yl_ref, lat_row))


def _moe(x1c, x1l, h2c, h2l, cbc, cbl, rtc, rtl, cntc, cntl, mod, blocks_per_lat_seq, wg, wu, wd, gf):
    nc, nl = x1c.shape[0], x1l.shape[0]
    nb = nc + nl
    n_rows_max = nb * MOE_BLK + nb * N_GROUPS * (ROW_ALIGN - 1) + N_GROUPS * (MOE_TM - ROW_ALIGN)
    n_tiles = -(-n_rows_max // MOE_TM)
    ns = n_tiles * MOE_TM

    cmap = lambda b, *_: (jnp.minimum(b, nc - 1), 0, 0)
    lmap = lambda b, *_: (jnp.maximum(b - nc, 0), 0, 0)
    whole = lambda *_: (0, 0)
    once = {"pipeline_mode": pl.Buffered(1)}
    arb = pltpu.CompilerParams(dimension_semantics=("arbitrary",), vmem_limit_bytes=VMEM_LIMIT)
    smem = pl.BlockSpec(memory_space=pltpu.SMEM)
    seg_i32 = jax.ShapeDtypeStruct((nb * N_GROUPS,), jnp.int32)
    tile_i32 = jax.ShapeDtypeStruct((n_tiles,), jnp.int32)

    xs, cs, start, npiece, off, tgroup, tvalid = pl.pallas_call(
        functools.partial(_dispatch_kernel, nc, nb, n_tiles),
        grid_spec=pltpu.PrefetchScalarGridSpec(
            num_scalar_prefetch=0, grid=(nb,),
            in_specs=[
                pl.BlockSpec((1, MOE_BLK, D_MODEL), cmap), pl.BlockSpec((1, MOE_BLK, D_MODEL), lmap),
                pl.BlockSpec((1, MOE_BLK, LANES), cmap), pl.BlockSpec((1, MOE_BLK, LANES), lmap),
                pl.BlockSpec((1, 8, MOE_BLK), cmap), pl.BlockSpec((1, 8, MOE_BLK), lmap),
                pl.BlockSpec(cntc.shape, lambda b: (0, 0, 0)), pl.BlockSpec(cntl.shape, lambda b: (0, 0, 0)),
            ],
            out_specs=[pl.BlockSpec((ns, D_MODEL), whole, **once), pl.BlockSpec((ns, 2 * LANES), whole, **once),
                       smem, smem, smem, smem, smem],
            scratch_shapes=[pltpu.VMEM((SORT_ROWS, D_MODEL), BF16), pltpu.VMEM((SORT_ROWS, 2 * LANES), BF16)],
        ),
        out_shape=[jax.ShapeDtypeStruct((ns, D_MODEL), BF16), jax.ShapeDtypeStruct((ns, 2 * LANES), BF16),
                   seg_i32, seg_i32, seg_i32, tile_i32, tile_i32],
        compiler_params=arb,
        name="moe_dispatch",
    )(h2c, h2l, cbc, cbl, rtc, rtl, cntc, cntl)

    wmap = lambda i, tg, tv: (tg[i], 0, 0)
    ys = pl.pallas_call(
        _experts_kernel,
        grid_spec=pltpu.PrefetchScalarGridSpec(
            num_scalar_prefetch=2, grid=(n_tiles,),
            in_specs=[
                pl.BlockSpec((MOE_TM, D_MODEL), lambda i, *_: (i, 0)),
                pl.BlockSpec((MOE_TM, 2 * LANES), lambda i, *_: (i, 0)),
                pl.BlockSpec((EXPERTS_PER_GROUP, D_MODEL, D_EXPERT), wmap),
                pl.BlockSpec((EXPERTS_PER_GROUP, D_MODEL, D_EXPERT), wmap),
                pl.BlockSpec((EXPERTS_PER_GROUP, D_EXPERT, D_MODEL), wmap),
            ],
            out_specs=pl.BlockSpec((MOE_TM, D_MODEL), lambda i, *_: (i, 0)),
        ),
        out_shape=jax.ShapeDtypeStruct((ns, D_MODEL), BF16),
        compiler_params=arb,
        name="moe_experts",
    )(tgroup, tvalid, xs, cs, wg, wu, wd)

    yc, yl = pl.pallas_call(
        functools.partial(_combine_kernel, nc, blocks_per_lat_seq),
        grid_spec=pltpu.PrefetchScalarGridSpec(
            num_scalar_prefetch=3, grid=(nb,),
            in_specs=[
                pl.BlockSpec((1, MOE_BLK, D_MODEL), cmap), pl.BlockSpec((1, MOE_BLK, D_MODEL), lmap),
                pl.BlockSpec((1, MOE_BLK, LANES), cmap), pl.BlockSpec((1, MOE_BLK, LANES), lmap),
                pl.BlockSpec((ns, D_MODEL), whole, **once),
                pl.BlockSpec(mod.shape, lambda *_: (0, 0, 0)),
                pl.BlockSpec((1, D_MODEL), whole),
            ],
            out_specs=[pl.BlockSpec((1, MOE_BLK, D_MODEL), cmap), pl.BlockSpec((1, MOE_BLK, D_MODEL), lmap)],
            scratch_shapes=[pltpu.VMEM((SORT_ROWS, D_MODEL), BF16)],
        ),
        out_shape=[jax.ShapeDtypeStruct((nc, MOE_BLK, D_MODEL), F32),
                   jax.ShapeDtypeStruct((nl, MOE_BLK, D_MODEL), F32)],
        compiler_params=arb,
        name="moe_combine",
    )(start, npiece, off, x1c, x1l, cbc, cbl, ys, mod, gf)
    return yc, yl


def _prep_weights(norm1_g, w_in, b_in, b_gates, w_dw, b_dw, conv_ln_g, conv_ln_b, w_conv_out,
                  mlstm_hn_g, w_mlstm_out, w_o, norm2_g, w_rg, b_rg, w_re, b_re):
    s_a = 2 * D_CONV
    s_q = s_a + D_MLSTM
    s_k = s_q + D_MLSTM
    s_v = s_k + D_MLSTM
    s_o = s_v + D_MLSTM
    s_g = s_o + 4 * N_HEADS
    row = lambda v: v.reshape(1, -1).astype(F32)
    w_t = w_in.T
    keep = [(0, s_q), (s_k, s_o), (s_g, w_in.shape[1])]
    halved = [(D_CONV, s_a), (s_v, s_o), (s_g, w_in.shape[1])]
    is_halved = lambda r: any(a <= r < b for a, b in halved)
    blocks = [r for a, b in keep for r in range(a, b, WPREP_ROWS)]
    wrow = _transpose_cast(w_t, blocks, [is_halved(r) for r in blocks])
    bias_scale = jnp.array([0.5 if is_halved(r) else 1.0 for a, b in keep for r in range(a, b)]
                           + [1.0] * D_MLSTM, F32)
    bg = (b_in[s_o:s_g] + b_gates.reshape(-1)).reshape(2, 2, N_HEADS).transpose(1, 0, 2).reshape(-1, 1)
    row_window = lambda start, n: _RowWindow(w_t, start, n)
    n_rt = N_EXPERTS + N_GROUPS
    wrt = jnp.pad(jnp.concatenate([w_re, w_rg], axis=1), ((0, 0), (0, LANES - n_rt)))
    wrt_hi = wrt.astype(BF16)
    wrt2 = jnp.concatenate([wrt_hi, (wrt - wrt_hi.astype(F32)).astype(BF16)], axis=1)
    brtT = jnp.pad(jnp.concatenate([b_re, b_rg]), (0, LANES - n_rt)).reshape(LANES, 1)
    return {
        "g1": row(norm1_g),
        "wrow": wrow, "brow": row(jnp.concatenate([b_in[a:b] for a, b in keep] + [b_in[s_q:s_k]]) * bias_scale),
        "wkT": row_window(s_q, D_MLSTM), "wgifT": row_window(s_o, 4 * N_HEADS), "bgifT": bg,
        "wdw": w_dw.astype(F32), "bdw": row(b_dw), "lng": row(conv_ln_g), "lnb": row(conv_ln_b),
        "wco": w_conv_out.astype(BF16), "hng": row(mlstm_hn_g), "wmo": w_mlstm_out.astype(BF16),
        "wo": w_o.astype(BF16), "g2": row(norm2_g), "wrt2": wrt2, "brtT": brtT,
    }


def kernel(x_prompt, x_sample, state_C, state_n, state_m, c, c_ctx, norm1_g, w_ada, b_ada, w_in, b_in, b_gates, w_dw, b_dw, conv_ln_g, conv_ln_b, w_conv_out, mlstm_hn_g, w_mlstm_out, w_o, norm2_g, w_rg, b_rg, w_re, b_re, w_e_gate, w_e_up, w_e_down, norm_final_g):
    B, S, _ = x_prompt.shape
    Bd, Sd, _ = x_sample.shape
    assert w_ada.shape[0] == 1, "single trunk layer"
    assert MIX_TM % S == 0 and S % SUB == 0 and Sd % MIX_TM == 0

    mod = _ada(c_ctx.reshape(1, -1), c, w_ada[0], b_ada[0].reshape(1, -1))

    wts = _prep_weights(norm1_g[0], w_in[0], b_in[0], b_gates[0], w_dw[0], b_dw[0], conv_ln_g[0],
                        conv_ln_b[0], w_conv_out[0], mlstm_hn_g[0], w_mlstm_out[0], w_o[0],
                        norm2_g[0], w_rg[0], b_rg[0], w_re[0], b_re[0])

    x1p, h2p, cbp, rtp, cntp, c_new, n_new, m_new = _mixer(
        x_prompt.reshape(B * S // MIX_TM, MIX_TM, D_MODEL), S, mod, lambda b: 0, wts, P=S, emit_state=True)

    state = (state_C[:, 0].reshape(Bd, N_UNITS, HEAD_DIM, HEAD_DIM), state_n[:, 0].reshape(Bd, N_UNITS, HEAD_DIM),
             state_m[:, 0].reshape(Bd, N_UNITS))
    x1s, h2s, cbs, rts, cnts = _mixer(x_sample, Sd, mod, lambda b: 1 + b, wts, P=GRID_W, state=state)

    nc, nl = B * S // MOE_BLK, Bd * Sd // MOE_BLK
    blk = lambda a, n: a.reshape(n, MOE_BLK, a.shape[-1])
    yp, ys = _moe(blk(x1p, nc), blk(x1s, nl), blk(h2p, nc), blk(h2s, nl), blk(cbp, nc), blk(cbs, nl),
                  rtp.reshape(nc, 8, MOE_BLK), rts.reshape(nl, 8, MOE_BLK),
                  cntp.reshape(nc, 8, LANES), cnts.reshape(nl, 8, LANES),
                  mod, Sd // MOE_BLK, w_e_gate[0], w_e_up[0], w_e_down[0], norm_final_g.reshape(1, -1))

    return (yp.reshape(B, S, D_MODEL), ys.reshape(Bd, Sd, D_MODEL),
            c_new.reshape(B, 1, 2, N_HEADS, HEAD_DIM, HEAD_DIM),
            n_new.reshape(B, 1, 2, N_HEADS, HEAD_DIM),
            m_new[:, :, 0].reshape(B, 1, 2, N_HEADS))
```

```python
import functools
from typing import NamedTuple

import jax
import jax.numpy as jnp
from jax import lax
from jax.experimental import pallas as pl
from jax.experimental.pallas import tpu as pltpu

D_MODEL = 1024
D_CONV = 512
CONV_K = 31
D_MLSTM = 512
N_HEADS = 4
HEAD_DIM = D_MLSTM // N_HEADS
N_GROUPS = 4
EXPERTS_PER_GROUP = 4
N_EXPERTS = N_GROUPS * EXPERTS_PER_GROUP
D_EXPERT = 256
N_ADA = 6
EPS = 1e-6
GRID_W = 64

LANES = 128
SUB = 256
CONV_PAD = 16
CONV_RB = 64
N_UNITS = 2 * N_HEADS
ROW_ALIGN = 16
CHAIN_SLACK = 1
COPY_RUN = 4
MOE_TM = 512
MIX_TM = 512
MOE_BLK = MIX_TM
SORT_ROWS = MOE_BLK + N_GROUPS * ROW_ALIGN
ADA_PER_STEP = 2
WPREP_ROWS = 512
ROUTE_GROUP_LANE = N_EXPERTS
ROUTE_RANK_LANE = N_EXPERTS + 1
VMEM_LIMIT = 58 * 1024 * 1024

BF16 = jnp.bfloat16
F32 = jnp.float32
NT_DIMS = (((1,), (1,)), ((), ()))


def _dot(a, b):
    return jnp.dot(a, b, preferred_element_type=F32)


def _dot_nt(a, b, precision=None):
    return lax.dot_general(a, b, NT_DIMS, preferred_element_type=F32, precision=precision)


def _sigmoid(x):
    return 0.5 * jnp.tanh(0.5 * x) + 0.5


def _sigmoid_of_half(xh):
    return 0.5 * jnp.tanh(xh) + 0.5


def _log_sigmoid(x):
    return jnp.minimum(x, 0.0) - jnp.log1p(jnp.exp(-jnp.abs(x)))


def _split3(x):
    hi = x.astype(BF16).astype(F32)
    r1 = x - hi
    mid = r1.astype(BF16).astype(F32)
    lo = (r1 - mid).astype(BF16).astype(F32)
    return hi, mid, lo


def _ada_kernel(cctx_ref, c_ref, w_ref, b_ref, o_ref):
    n = 1 + c_ref.shape[0]
    c = jnp.concatenate([cctx_ref[...], c_ref[...], jnp.zeros((8 - n, D_MODEL), F32)], axis=0)
    s = (c * _sigmoid(c)).astype(BF16)
    out = _dot(s, w_ref[...].astype(BF16)) + b_ref[...]
    for v in range(ADA_PER_STEP):
        o_ref[v] = out[:, v * D_MODEL:(v + 1) * D_MODEL]


def _ada(c_ctx, c, w_ada, b_ada):
    return pl.pallas_call(
        _ada_kernel,
        grid=(N_ADA // ADA_PER_STEP,),
        in_specs=[
            pl.BlockSpec(c_ctx.shape, lambda j: (0, 0)),
            pl.BlockSpec(c.shape, lambda j: (0, 0)),
            pl.BlockSpec((D_MODEL, ADA_PER_STEP * D_MODEL), lambda j: (0, j)),
            pl.BlockSpec((1, ADA_PER_STEP * D_MODEL), lambda j: (0, j)),
        ],
        out_specs=pl.BlockSpec((ADA_PER_STEP, 8, D_MODEL), lambda j: (j, 0, 0)),
        out_shape=jax.ShapeDtypeStruct((N_ADA, 8, D_MODEL), F32),
        compiler_params=pltpu.CompilerParams(dimension_semantics=("arbitrary",)),
        name="ada",
    )(c_ctx, c, w_ada, b_ada)


def _transpose_cast_kernel(starts_ref, halve_ref, wt_ref, o_ref):
    scale = jnp.where(halve_ref[pl.program_id(0)] == 1, 0.5, 1.0)
    o_ref[...] = (wt_ref[...] * scale).astype(BF16).T


def _transpose_cast(w_t, row_starts, halve):
    n, k = len(row_starts), w_t.shape[1]
    return pl.pallas_call(
        _transpose_cast_kernel,
        grid_spec=pltpu.PrefetchScalarGridSpec(
            num_scalar_prefetch=2, grid=(n,),
            in_specs=[pl.BlockSpec((pl.Element(WPREP_ROWS), pl.Element(k)), lambda j, starts, hv: (starts[j] * 8, 0))],
            out_specs=pl.BlockSpec((k, WPREP_ROWS), lambda j, starts, hv: (0, j)),
        ),
        out_shape=jax.ShapeDtypeStruct((k, n * WPREP_ROWS), BF16),
        compiler_params=pltpu.CompilerParams(dimension_semantics=("arbitrary",)),
        name="transpose_cast",
    )(jnp.array([r // 8 for r in row_starts], jnp.int32), jnp.array([int(h) for h in halve], jnp.int32), w_t)


WROW_OFFSET = {"wq": 2 * D_CONV, "wv": 2 * D_CONV + D_MLSTM, "wog": 2 * D_CONV + 2 * D_MLSTM,
               "wgm": 2 * D_CONV + 3 * D_MLSTM}
BROW_K_OFFSET = 2 * D_CONV + 3 * D_MLSTM + 2 * D_MODEL

_MIXER_WEIGHTS = (
    "g1", "wrow", "brow", "wkT", "wgifT", "bgifT", "wdw", "bdw", "lng", "lnb",
    "wco", "hng", "wmo", "wo", "g2", "wrt2", "brtT",
)


def _zero_after(x):
    bits = lax.bitcast_convert_type(x, jnp.uint32)
    bits = lax.shift_right_logical(lax.shift_right_logical(bits, jnp.uint32(16)), jnp.uint32(16))
    return lax.bitcast_convert_type(bits, F32)[0:1, :]


def _conv_block(upad_s, seg, base, cs, wdw_ref, bdw_ref, after=None):
    sub = 8
    first = CONV_PAD - CONV_K // 2
    acc = jnp.broadcast_to(bdw_ref[0:1, cs], (CONV_RB, LANES))
    for r in range(sub):
        z = None
        for a in range((CONV_K + first + sub - 1) // sub):
            j = sub * a + r - first
            if 0 <= j < CONV_K:
                lo = base + sub * a
                tap = wdw_ref[j:j + 1, cs] if after is None else wdw_ref[j:j + 1, cs] + after
                term = tap * upad_s[seg, lo:lo + CONV_RB + sub, cs]
                z = term if z is None else z + term
        acc = acc + z[r:r + CONV_RB, :]
    return acc


def _mixer_kernel(R, T, P, has_state, emit_state, mod_index, *refs):
    L = SUB
    n_mt = R // MIX_TM
    cpm = MIX_TM // L
    n_seq = R // T
    cps = T // L
    nseg = MIX_TM // P
    assert not has_state or n_seq == 1
    it = iter(refs)
    x_ref = next(it)
    mod_ref = next(it)
    if has_state:
        c0_ref = next(it)
        n0_ref = next(it)
        m0_ref = next(it)
    w = {name: next(it) for name in _MIXER_WEIGHTS}
    x1_ref = next(it)
    h2_ref = next(it)
    comb_ref = next(it)
    route_ref = next(it)
    cnt_ref = next(it)
    if emit_state:
        cout_ref = next(it)
        nout_ref = next(it)
        mout_ref = next(it)
    (q_s, kT_s, v_s, so_s, scan_s, ma_s, sgb_s, hm_s, cst_s, upad_s) = [next(it) for _ in range(10)]

    cond_row = mod_index(pl.program_id(0))

    def mod_row(i):
        return mod_ref[i, pl.ds(cond_row, 1), :]

    zpad = jnp.zeros((CONV_PAD, D_CONV), F32)
    for seg in range(nseg):
        upad_s[seg, 0:CONV_PAD, :] = zpad
        upad_s[seg, CONV_PAD + P:CONV_PAD + P + CONV_PAD, :] = zpad

    t_idx = lax.broadcasted_iota(jnp.int32, (L, L), 0)
    s_idx = lax.broadcasted_iota(jnp.int32, (L, L), 1)
    lower = s_idx <= t_idx
    upper = s_idx >= t_idx
    triu_b = upper.astype(F32).astype(BF16)
    lane_u = lax.broadcasted_iota(jnp.int32, (N_UNITS, L), 1)
    is_bwd = lax.broadcasted_iota(jnp.int32, (N_UNITS, L), 0) >= N_HEADS

    def gate_scan(g):
        gi, lf = g[:N_UNITS], _log_sigmoid(g[N_UNITS:])
        pr = _dot(jnp.concatenate(_split3(lf), axis=0).astype(BF16), triu_b)
        pre = pr[0:N_UNITS] + pr[N_UNITS:2 * N_UNITS] + pr[2 * N_UNITS:]
        tot = pre[:, L - 1:L]
        bsum = jnp.where(is_bwd, tot - pre + lf, pre)
        a = gi - bsum
        pm, sm, k = a, a, 1
        while k < L:
            pm = jnp.where(lane_u >= k, jnp.maximum(pm, pltpu.roll(pm, k, axis=1)), pm)
            sm = jnp.where(lane_u < L - k, jnp.maximum(sm, pltpu.roll(sm, L - k, axis=1)), sm)
            k *= 2
        wide = lambda v: jnp.broadcast_to(v, (N_UNITS, L))
        return jnp.concatenate([a, jnp.where(is_bwd, sm, pm), bsum, wide(tot),
                                wide(jnp.max(a, axis=1, keepdims=True))], axis=0)

    def phase1(i, carry):
        r0 = pl.multiple_of(i * MIX_TM, MIX_TM)
        rows = pl.ds(r0, MIX_TM)
        x = x_ref[0, rows, :]
        xn = x * lax.rsqrt(jnp.mean(x * x, axis=-1, keepdims=True) + EPS) * w["g1"][...]
        hb = (xn * (1.0 + mod_row(1)) + mod_row(0)).astype(BF16)

        ag = _dot(hb, w["wrow"][:, :2 * D_CONV]) + w["brow"][:, :2 * D_CONV]
        u = ag[:, :D_CONV] * _sigmoid_of_half(ag[:, D_CONV:])
        for seg in range(nseg):
            upad_s[seg, CONV_PAD:CONV_PAD + P, :] = u[seg * P:(seg + 1) * P, :]
        gates = _dot_nt(w["wgifT"][...].astype(BF16), hb)
        gates = jnp.concatenate([gates[d * 2 * N_HEADS + g * N_HEADS:d * 2 * N_HEADS + (g + 1) * N_HEADS]
                                 for g in range(2) for d in range(2)], axis=0) + w["bgifT"][...]
        for j in range(cpm):
            scan_s[i * cpm + j] = gate_scan(gates[:, j * L:(j + 1) * L])

        def proj(name, c0, width=2 * LANES):
            w0 = WROW_OFFSET[name] + c0
            return _dot(hb, w["wrow"][:, w0:w0 + width]) + w["brow"][:, w0:w0 + width]

        last = lambda z: z[-8:, -LANES:]
        bk_row = w["brow"][:, BROW_K_OFFSET:BROW_K_OFFSET + D_MLSTM]
        bk_col = jnp.concatenate([bk_row, jnp.zeros((LANES - 1, D_MLSTM), F32)], axis=0).T[:, 0:1]

        def gm_a(c0):
            z = proj("wgm", c0)
            ma_s[rows, c0:c0 + 2 * LANES] = _sigmoid_of_half(z)
            return last(z)

        def gm_b(c0):
            z = proj("wgm", D_MODEL + c0)
            sgb_s[rows, c0:c0 + 2 * LANES] = _sigmoid_of_half(z)
            return last(z)

        def q_part(c0):
            z = proj("wq", c0)
            q_s[rows, c0:c0 + 2 * LANES] = (z * (HEAD_DIM ** -0.5)).astype(BF16)
            return last(z)

        def v_part(c0):
            z = proj("wv", c0)
            v_s[rows, c0:c0 + 2 * LANES] = z.astype(BF16)
            return last(z)

        def o_part(c0):
            z = proj("wog", c0)
            so_s[rows, c0:c0 + 2 * LANES] = _sigmoid_of_half(z)
            return last(z)

        def k_part(c0):
            rs = slice(c0, c0 + 2 * LANES)
            z = _dot_nt(w["wkT"][rs, :].astype(BF16), hb) + bk_col[rs, :]
            kt = z.astype(BF16)
            for j in range(cpm):
                kT_s[i * cpm + j, rs, :] = kt[:, j * L:(j + 1) * L]
            return last(z)

        jobs = ([functools.partial(gm_a, c0) for c0 in range(0, D_MODEL, 2 * LANES)]
                + [functools.partial(gm_b, c0) for c0 in range(0, D_MODEL, 2 * LANES)]
                + [functools.partial(f, c0) for f in (q_part, k_part, v_part, o_part)
                   for c0 in range(0, D_MLSTM, 2 * LANES)])
        n_jobs = len(jobs)
        conv = {}
        after, lag = None, [None] * CHAIN_SLACK
        n_pieces = (D_CONV // LANES) * nseg * (P // CONV_RB)
        for cb in range(D_CONV // LANES):
            cs = slice(cb * LANES, (cb + 1) * LANES)
            for seg in range(nseg):
                for rb in range(P // CONV_RB):
                    blk = _conv_block(upad_s, seg, rb * CONV_RB, cs, w["wdw"], w["bdw"], after)
                    conv[(cb, seg, rb)] = blk
                    if jobs and len(conv) * n_jobs >= (n_jobs - len(jobs) + 1) * n_pieces:
                        lag.append(_zero_after(jobs.pop(0)()))
                        after = lag.pop(0)
        for job in jobs:
            job()
        cu = jnp.concatenate(
            [jnp.concatenate([conv[(cb, seg, rb)] for seg in range(nseg) for rb in range(P // CONV_RB)], axis=0)
             for cb in range(D_CONV // LANES)], axis=1)
        mu = jnp.mean(cu, axis=-1, keepdims=True)
        cc = cu - mu
        cn = cc * lax.rsqrt(jnp.mean(cc * cc, axis=-1, keepdims=True) + EPS) * w["lng"][...] + w["lnb"][...]
        ca = (cn * _sigmoid(cn)).astype(BF16)
        ma_s[rows, :] = ma_s[rows, :] * _dot(ca, w["wco"][...])
        return carry

    if n_mt == 1:
        phase1(0, 0)
    else:
        lax.fori_loop(0, n_mt, phase1, 0)

    ones_col = (lax.broadcasted_iota(jnp.int32, (L, HEAD_DIM), 1) == 0).astype(F32).astype(BF16)
    pad_rows = jnp.zeros((LANES - 3 * N_UNITS, L), F32)

    def gate_prep(c, m_vec):
        sc = scan_s[c]
        a, run_max, bsum = sc[0:N_UNITS], sc[N_UNITS:2 * N_UNITS], sc[2 * N_UNITS:3 * N_UNITS]
        tot, a_max = sc[3 * N_UNITS:4 * N_UNITS, 0:1], sc[4 * N_UNITS:5 * N_UNITS, 0:1]
        big_m = jnp.maximum(m_vec, run_max)
        m_end = jnp.maximum(m_vec, a_max)
        cols = jnp.concatenate(
            [big_m, jnp.exp(m_vec - big_m), jnp.exp(-bsum - big_m), pad_rows], axis=0).T
        return a, cols, jnp.exp(a - m_end), jnp.exp(m_vec - m_end), tot + m_end

    def unit_group(dirs, c, prep, first_chunk, want_state):
        a, cols, wk, decay, _ = prep
        rows = slice(c * L, (c + 1) * L)
        heads = range(N_HEADS)
        units = [(d, hd) for d in dirs for hd in heads]
        hs = [slice(hd * HEAD_DIM, (hd + 1) * HEAD_DIM) for hd in heads]
        idx = {u: u[0] * N_HEADS + u[1] for u in units}
        col = lambda k, u: cols[:, k * N_UNITS + idx[u]:k * N_UNITS + idx[u] + 1]
        row = lambda arr, u: arr[idx[u]:idx[u] + 1, :]
        chained = has_state or not first_chunk
        qc = [q_s[rows, hs[hd]] for hd in heads]
        kTc = [kT_s[c, hs[hd], :] for hd in heads]
        vaug = [jnp.concatenate([v_s[rows, hs[hd]], ones_col], axis=1) for hd in heads]
        qk = [_dot(qc[hd], kTc[hd]) for hd in heads]
        s_mat = {u: (qk[u[1]] * jnp.where(lower if u[0] == 0 else upper, jnp.exp(row(a, u) - col(0, u)), 0.0)
                     ).astype(BF16) for u in units}
        nd = {u: _dot(s_mat[u], vaug[u[1]]) for u in units}
        if chained:
            nd = {u: nd[u] + col(1, u) * _dot(qc[u[1]], cst_s[idx[u]].astype(BF16)) for u in units}
        h = {u: nd[u][:, :HEAD_DIM] * (1.0 / jnp.maximum(jnp.abs(nd[u][:, HEAD_DIM:HEAD_DIM + 1]), col(2, u)))
             for u in units}
        for hd in heads:
            total = h[(dirs[0], hd)]
            for d in dirs[1:]:
                total = total + h[(d, hd)]
            if dirs[0] == 0:
                hm_s[rows, hs[hd]] = total
            else:
                hm_s[rows, hs[hd]] = hm_s[rows, hs[hd]] + total
        if want_state:
            kw = {u: (kTc[u[1]].astype(F32) * row(wk, u)).astype(BF16) for u in units}
            upd = {u: _dot(kw[u], vaug[u[1]]) for u in units}
            for u in units:
                cst_s[idx[u]] = (upd[u] + row(decay, u) * cst_s[idx[u]]) if chained else upd[u]

    dir_rows = lax.broadcasted_iota(jnp.int32, (N_UNITS, 1), 0) >= N_HEADS
    for seq in range(n_seq):
        if has_state:
            n_cols = jnp.concatenate([n0_ref[0], jnp.zeros((LANES - N_UNITS, HEAD_DIM), F32)], axis=0).T
            first_lane = lax.broadcasted_iota(jnp.int32, (HEAD_DIM, HEAD_DIM), 1) == 0
            for idx in range(N_UNITS):
                cst_s[idx, :, :HEAD_DIM] = c0_ref[0, idx]
                cst_s[idx, :, HEAD_DIM:] = jnp.where(first_lane, n_cols[:, idx:idx + 1], 0.0)
            unit_row = lax.broadcasted_iota(jnp.int32, (N_UNITS, 1), 0)
            m_vec = jnp.zeros((N_UNITS, 1), F32)
            for idx in range(N_UNITS):
                m_vec = jnp.where(unit_row == idx, m0_ref[pl.program_id(0), idx], m_vec)
        else:
            m_vec = jnp.zeros((N_UNITS, 1), F32)
        if cps == 1:
            prep = gate_prep(seq, m_vec)
            unit_group([0, 1], seq, prep, True, emit_state)
            m_vec = prep[4]
        else:
            for d in range(2):
                order = list(range(cps)) if d == 0 else list(range(cps - 1, -1, -1))
                for pos, c in enumerate(order):
                    prep = gate_prep(seq * cps + c, m_vec)
                    unit_group([d], seq * cps + c, prep, pos == 0, emit_state or pos < cps - 1)
                    m_vec = jnp.where(dir_rows == (d == 1), prep[4], m_vec)
        if emit_state:
            for idx in range(N_UNITS):
                caug = cst_s[idx]
                cout_ref[0, seq * N_UNITS + idx] = caug[:, :HEAD_DIM]
                nout_ref[0, seq * N_UNITS + idx:seq * N_UNITS + idx + 1, :] = caug[:, HEAD_DIM:].T[0:1, :]
            mout_ref[0, seq * N_UNITS:(seq + 1) * N_UNITS, :] = jnp.broadcast_to(m_vec, (N_UNITS, LANES))

    e_iota = lax.broadcasted_iota(jnp.int32, (LANES, MIX_TM), 0)
    g_of_e = lax.shift_right_logical(e_iota, 2)
    j_of_e = lax.bitwise_and(e_iota, EXPERTS_PER_GROUP - 1)
    r8 = lax.broadcasted_iota(jnp.int32, (8, MIX_TM), 0)
    before_b = (lax.broadcasted_iota(jnp.int32, (MOE_BLK, MOE_BLK), 0)
                < lax.broadcasted_iota(jnp.int32, (MOE_BLK, MOE_BLK), 1)).astype(F32).astype(BF16)

    def phase3(i, carry):
        r0 = pl.multiple_of(i * MIX_TM, MIX_TM)
        rows = pl.ds(r0, MIX_TM)
        hm = hm_s[rows, :]
        heads = []
        for hd in range(N_HEADS):
            hh = hm[:, hd * HEAD_DIM:(hd + 1) * HEAD_DIM]
            heads.append(hh * lax.rsqrt(jnp.mean(hh * hh, axis=-1, keepdims=True) + EPS))
        hn = jnp.concatenate(heads, axis=1) * w["hng"][...]
        hb2 = (so_s[rows, :] * hn).astype(BF16)
        br_b = _dot(hb2, w["wmo"][...])
        mixed = (ma_s[rows, :] + sgb_s[rows, :] * br_b).astype(BF16)
        x1 = x_ref[0, rows, :] + mod_row(2) * _dot(mixed, w["wo"][...])
        x1_ref[0, rows, :] = x1
        xn = x1 * lax.rsqrt(jnp.mean(x1 * x1, axis=-1, keepdims=True) + EPS) * w["g2"][...]
        h2 = xn * (1.0 + mod_row(4)) + mod_row(3)
        h2_ref[0, rows, :] = h2.astype(BF16)

        h2_hi = h2.astype(BF16)
        h2_lo = (h2 - h2_hi.astype(F32)).astype(BF16)
        lg = _dot(h2_hi, w["wrt2"][...])
        lg = lg[:, :LANES] + lg[:, LANES:] + _dot(h2_lo, w["wrt2"][:, :LANES])
        lt = lg.T + w["brtT"][...]
        gl = [lt[N_EXPERTS + g:N_EXPERTS + g + 1, :] for g in range(N_GROUPS)]
        best, gsel = gl[0], jnp.zeros((1, MIX_TM), jnp.int32)
        for g in range(1, N_GROUPS):
            better = gl[g] > best
            gsel = jnp.where(better, g, gsel)
            best = jnp.where(better, gl[g], best)
        gp_sel = 1.0 / sum(jnp.exp(v - best) for v in gl)
        el = []
        for j in range(EXPERTS_PER_GROUP):
            v = lt[j:j + 1, :]
            for g in range(1, N_GROUPS):
                r = g * EXPERTS_PER_GROUP + j
                v = jnp.where(gsel == g, lt[r:r + 1, :], v)
            el.append(v)
        l1, e1 = el[0], jnp.zeros((1, MIX_TM), jnp.int32)
        for j in range(1, EXPERTS_PER_GROUP):
            better = el[j] > l1
            e1 = jnp.where(better, j, e1)
            l1 = jnp.where(better, el[j], l1)
        l2 = jnp.full((1, MIX_TM), -jnp.inf, F32)
        e2 = jnp.zeros((1, MIX_TM), jnp.int32)
        for j in range(EXPERTS_PER_GROUP):
            better = jnp.logical_and(e1 != j, el[j] > l2)
            e2 = jnp.where(better, j, e2)
            l2 = jnp.where(better, el[j], l2)
        r2 = jnp.exp(l2 - l1)
        wt1 = gp_sel / (1.0 + r2)
        wt2 = gp_sel * r2 / (1.0 + r2)
        in_group = g_of_e == gsel
        comb_t = (jnp.where(jnp.logical_and(in_group, j_of_e == e1), wt1, 0.0)
                  + jnp.where(jnp.logical_and(in_group, j_of_e == e2), wt2, 0.0))

        onehot = (r8 == gsel).astype(F32)
        gsel_f = gsel.astype(F32)
        rank = jnp.sum(onehot * _dot(onehot.astype(BF16), before_b), axis=0, keepdims=True)
        r8rows = pl.ds(pl.multiple_of(i * 8, 8), 8)
        route_ref[0, r8rows, :] = jnp.where(r8 == 0, gsel_f, jnp.where(r8 == 1, rank, 0.0))
        cnt_ref[0, r8rows, :] = jnp.broadcast_to(jnp.sum(onehot, axis=1, keepdims=True), (8, LANES))
        comb_t = jnp.where(e_iota == ROUTE_GROUP_LANE, gsel_f,
                           jnp.where(e_iota == ROUTE_RANK_LANE, rank, comb_t))
        comb_ref[0, rows, :] = comb_t.T
        return carry

    if n_mt == 1:
        phase3(0, 0)
    else:
        lax.fori_loop(0, n_mt, phase3, 0)


class _RowWindow(NamedTuple):
    array: jax.Array
    start: int
    n: int


def _const_spec(a):
    if isinstance(a, _RowWindow):
        assert a.start % a.n == 0
        return a.array, pl.BlockSpec((a.n, a.array.shape[1]), lambda b: (a.start // a.n, 0),
                                     pipeline_mode=pl.Buffered(1))
    nd = a.ndim
    return a, pl.BlockSpec(a.shape, lambda b, _nd=nd: (0,) * _nd, pipeline_mode=pl.Buffered(1))


def _mixer(x, T, mod, mod_index, weights, P, state=None, emit_state=False):
    B, R, _ = x.shape
    n_chunks = R // SUB
    n_blk = R // MOE_BLK
    n_seq = R // T
    has_state = state is not None
    seq_mode = {} if R <= MIX_TM else {"pipeline_mode": pl.Buffered(1)}
    in_specs = [
        pl.BlockSpec((1, R, D_MODEL), lambda b: (b, 0, 0), **seq_mode),
        pl.BlockSpec(mod.shape, lambda b: (0, 0, 0)),
    ]
    args = [x, mod]
    if has_state:
        c0, n0, m0 = state
        in_specs += [
            pl.BlockSpec((1, N_UNITS, HEAD_DIM, HEAD_DIM), lambda b: (b, 0, 0, 0)),
            pl.BlockSpec((1, N_UNITS, HEAD_DIM), lambda b: (b, 0, 0)),
            pl.BlockSpec(memory_space=pltpu.SMEM),
        ]
        args += [c0, n0, m0]
    for name in _MIXER_WEIGHTS:
        operand, spec = _const_spec(weights[name])
        in_specs.append(spec)
        args.append(operand)
    out_shape = [
        jax.ShapeDtypeStruct((B, R, D_MODEL), F32),
        jax.ShapeDtypeStruct((B, R, D_MODEL), BF16),
        jax.ShapeDtypeStruct((B, R, LANES), F32),
        jax.ShapeDtypeStruct((B, n_blk * 8, MOE_BLK), F32),
        jax.ShapeDtypeStruct((B, n_blk * 8, LANES), F32),
    ]
    out_specs = [
        pl.BlockSpec((1, R, D_MODEL), lambda b: (b, 0, 0), **seq_mode),
        pl.BlockSpec((1, R, D_MODEL), lambda b: (b, 0, 0), **seq_mode),
        pl.BlockSpec((1, R, LANES), lambda b: (b, 0, 0)),
        pl.BlockSpec((1, n_blk * 8, MOE_BLK), lambda b: (b, 0, 0)),
        pl.BlockSpec((1, n_blk * 8, LANES), lambda b: (b, 0, 0)),
    ]
    if emit_state:
        out_shape += [
            jax.ShapeDtypeStruct((B, n_seq * N_UNITS, HEAD_DIM, HEAD_DIM), F32),
            jax.ShapeDtypeStruct((B, n_seq * N_UNITS, HEAD_DIM), F32),
            jax.ShapeDtypeStruct((B, n_seq * N_UNITS, LANES), F32),
        ]
        out_specs += [
            pl.BlockSpec((1, n_seq * N_UNITS, HEAD_DIM, HEAD_DIM), lambda b: (b, 0, 0, 0)),
            pl.BlockSpec((1, n_seq * N_UNITS, HEAD_DIM), lambda b: (b, 0, 0)),
            pl.BlockSpec((1, n_seq * N_UNITS, LANES), lambda b: (b, 0, 0)),
        ]
    scratch = [
        pltpu.VMEM((R, D_MLSTM), BF16),
        pltpu.VMEM((n_chunks, D_MLSTM, SUB), BF16),
        pltpu.VMEM((R, D_MLSTM), BF16),
        pltpu.VMEM((R, D_MLSTM), F32),
        pltpu.VMEM((n_chunks, 5 * N_UNITS, SUB), F32),
        pltpu.VMEM((R, D_MODEL), F32),
        pltpu.VMEM((R, D_MODEL), F32),
        pltpu.VMEM((R, D_MLSTM), F32),
        pltpu.VMEM((N_UNITS, HEAD_DIM, 2 * HEAD_DIM), F32),
        pltpu.VMEM((MIX_TM // P, P + 2 * CONV_PAD, D_CONV), F32),
    ]
    return pl.pallas_call(
        functools.partial(_mixer_kernel, R, T, P, has_state, emit_state, mod_index),
        grid=(B,),
        in_specs=in_specs,
        out_specs=out_specs,
        out_shape=out_shape,
        scratch_shapes=scratch,
        compiler_params=pltpu.CompilerParams(
            dimension_semantics=("arbitrary",), vmem_limit_bytes=VMEM_LIMIT),
        name="mixer_T%d" % T,
    )(*args)


def _dest_in_block(group, rank, starts):
    dest = rank
    for g in range(N_GROUPS):
        dest = dest + jnp.where(group == float(g), starts[g], 0.0)
    return dest


def _copy_segments(src_refs, dst_refs, src_starts, dst_starts, n_pieces):
    def copy(g, first_piece, n_rows):
        s = pl.multiple_of(src_starts[g] + first_piece * ROW_ALIGN, ROW_ALIGN)
        d = pl.multiple_of(dst_starts[g] + first_piece * ROW_ALIGN, ROW_ALIGN)
        for src, dst in zip(src_refs, dst_refs):
            dst[pl.ds(d, n_rows), :] = src[pl.ds(s, n_rows), :]

    for g in range(N_GROUPS):
        n_runs = lax.shift_right_logical(n_pieces[g], COPY_RUN.bit_length() - 1)

        def run(k, carry, g=g):
            copy(g, k * COPY_RUN, COPY_RUN * ROW_ALIGN)
            return carry

        def single(k, carry, g=g):
            copy(g, k, ROW_ALIGN)
            return carry

        lax.fori_loop(0, n_runs, run, 0)
        lax.fori_loop(n_runs * COPY_RUN, n_pieces[g], single, 0)


def _plan_segments(n_blocks, n_tiles, count, start_ref, npiece_ref, off_ref, tgroup_ref, tvalid_ref):
    align_shift = ROW_ALIGN.bit_length() - 1
    tile_shift = MOE_TM.bit_length() - 1

    def block_starts(blk, carry):
        row = jnp.int32(0)
        for g in range(N_GROUPS):
            n = lax.shift_right_logical(count(blk, g) + (ROW_ALIGN - 1), align_shift)
            npiece_ref[blk * N_GROUPS + g] = n
            start_ref[blk * N_GROUPS + g] = row
            row = row + n * ROW_ALIGN
        return carry

    lax.fori_loop(0, n_blocks, block_starts, 0)

    base_row = jnp.int32(0)
    base_tile = jnp.int32(0)
    last_group = jnp.int32(0)
    for g in range(N_GROUPS):
        def seg_offsets(blk, row, g=g, base_row=base_row):
            off_ref[blk * N_GROUPS + g] = base_row + row
            return row + npiece_ref[blk * N_GROUPS + g] * ROW_ALIGN

        rows = lax.fori_loop(0, n_blocks, seg_offsets, jnp.int32(0))
        tiles = lax.shift_right_logical(rows + (MOE_TM - 1), tile_shift)

        def mark_tiles(t, carry, g=g, base_tile=base_tile):
            tgroup_ref[base_tile + t] = g
            tvalid_ref[base_tile + t] = 1
            return carry

        lax.fori_loop(0, tiles, mark_tiles, 0)
        last_group = jnp.where(tiles > 0, g, last_group)
        base_row = base_row + tiles * MOE_TM
        base_tile = base_tile + tiles

    def mark_unused(t, carry):
        tgroup_ref[t] = last_group
        tvalid_ref[t] = 0
        return carry

    lax.fori_loop(base_tile, n_tiles, mark_unused, 0)


def _dispatch_kernel(n_ctx_blocks, n_blocks, n_tiles,
                     h2c_ref, h2l_ref, cbc_ref, cbl_ref, rtc_ref, rtl_ref, cntc_ref, cntl_ref,
                     xs_ref, cs_ref, start_ref, npiece_ref, off_ref, tgroup_ref, tvalid_ref,
                     sx_s, sc_s):
    b = pl.program_id(0)
    is_ctx = b < n_ctx_blocks

    def count(blk, g):
        vc = cntc_ref[jnp.minimum(blk, n_ctx_blocks - 1), pl.ds(g, 1), pl.ds(0, 1)]
        vl = cntl_ref[jnp.maximum(blk - n_ctx_blocks, 0), pl.ds(g, 1), pl.ds(0, 1)]
        return jnp.where(blk < n_ctx_blocks, vc, vl)[0, 0].astype(jnp.int32)

    @pl.when(b == 0)
    def _():
        _plan_segments(n_blocks, n_tiles, count, start_ref, npiece_ref, off_ref, tgroup_ref, tvalid_ref)
        xs_ref[...] = jnp.zeros_like(xs_ref)
        cs_ref[...] = jnp.zeros_like(cs_ref)

    starts = [start_ref[b * N_GROUPS + g] for g in range(N_GROUPS)]

    def sort_block(h2_ref, cb_ref, rt_ref):
        h2 = h2_ref[0]
        cb = cb_ref[0]
        rt = rt_ref[0]
        dest = _dest_in_block(rt[0:1, :], rt[1:2, :], [s.astype(F32) for s in starts])
        row = lax.broadcasted_iota(jnp.int32, (SORT_ROWS, MOE_BLK), 0).astype(F32)
        perm = (row == dest).astype(F32).astype(BF16)
        cb_hi = cb.astype(BF16)
        cb_lo = (cb - cb_hi.astype(F32)).astype(BF16)
        sx_s[...] = _dot(perm, h2).astype(BF16)
        sc_s[...] = _dot(perm, jnp.concatenate([cb_hi, cb_lo], axis=1)).astype(BF16)

    pl.when(is_ctx)(functools.partial(sort_block, h2c_ref, cbc_ref, rtc_ref))
    pl.when(jnp.logical_not(is_ctx))(functools.partial(sort_block, h2l_ref, cbl_ref, rtl_ref))
    _copy_segments((sx_s, sc_s), (xs_ref, cs_ref), starts,
                   [off_ref[b * N_GROUPS + g] for g in range(N_GROUPS)],
                   [npiece_ref[b * N_GROUPS + g] for g in range(N_GROUPS)])


def _experts_kernel(tgroup_ref, tvalid_ref, xs_ref, cs_ref, wg_ref, wu_ref, wd_ref, ys_ref):
    i = pl.program_id(0)

    @pl.when(tvalid_ref[i] == 1)
    def _():
        x = xs_ref[...]
        comb = cs_ref[:, :LANES].astype(F32) + cs_ref[:, LANES:].astype(F32)
        lane = lax.broadcasted_iota(jnp.int32, comb.shape, 1)
        first = tgroup_ref[i] * EXPERTS_PER_GROUP
        acc = None
        for j in range(EXPERTS_PER_GROUP):
            gj = _dot(x, wg_ref[j].astype(BF16))
            uj = _dot(x, wu_ref[j].astype(BF16))
            cw = jnp.sum(jnp.where(lane == first + j, comb, 0.0), axis=1, keepdims=True)
            out = _dot((gj * _sigmoid(gj) * uj * cw).astype(BF16), wd_ref[j].astype(BF16))
            acc = out if acc is None else acc + out
        ys_ref[...] = acc.astype(BF16)

    @pl.when(tvalid_ref[i] == 0)
    def _():
        ys_ref[...] = jnp.zeros_like(ys_ref)


def _combine_kernel(n_ctx_blocks, blocks_per_lat_seq, start_ref, npiece_ref, off_ref,
                    x1c_ref, x1l_ref, cbc_ref, cbl_ref, ys_ref, mod_ref, gf_ref, yc_ref, yl_ref, loc_s):
    b = pl.program_id(0)
    is_ctx = b < n_ctx_blocks
    starts = [start_ref[b * N_GROUPS + g] for g in range(N_GROUPS)]
    @pl.when(b == 0)
    def _():
        loc_s[...] = jnp.zeros_like(loc_s)

    _copy_segments((ys_ref,), (loc_s,), [off_ref[b * N_GROUPS + g] for g in range(N_GROUPS)], starts,
                   [npiece_ref[b * N_GROUPS + g] for g in range(N_GROUPS)])
    def finish_block(x1_ref, cb_ref, y_ref, mrow):
        cb = cb_ref[0]
        dest = _dest_in_block(cb[:, ROUTE_GROUP_LANE:ROUTE_GROUP_LANE + 1],
                              cb[:, ROUTE_RANK_LANE:ROUTE_RANK_LANE + 1],
                              [s.astype(F32) for s in starts])
        col = lax.broadcasted_iota(jnp.int32, (MOE_BLK, SORT_ROWS), 1).astype(F32)
        unperm = (col == dest).astype(F32).astype(BF16)
        x2 = x1_ref[0] + mod_ref[N_ADA - 1, pl.ds(mrow, 1), :] * _dot(unperm, loc_s[...])
        y_ref[0] = x2 * lax.rsqrt(jnp.mean(x2 * x2, axis=-1, keepdims=True) + EPS) * gf_ref[...]

    lat_row = 1 + jnp.maximum(b - n_ctx_blocks, 0) // blocks_per_lat_seq
    pl.when(is_ctx)(functools.partial(finish_block, x1c_ref, cbc_ref, yc_ref, 0))
    pl.when(jnp.logical_not(is_ctx))(functools.partial(finish_block, x1l_ref, cbl_ref, yl_ref, lat_row))


def _moe(x1c, x1l, h2c, h2l, cbc, cbl, rtc, rtl, cntc, cntl, mod, blocks_per_lat_seq, wg, wu, wd, gf):
    nc, nl = x1c.shape[0], x1l.shape[0]
    nb = nc + nl
    n_rows_max = nb * MOE_BLK + nb * N_GROUPS * (ROW_ALIGN - 1) + N_GROUPS * (MOE_TM - ROW_ALIGN)
    n_tiles = -(-n_rows_max // MOE_TM)
    ns = n_tiles * MOE_TM

    cmap = lambda b, *_: (jnp.minimum(b, nc - 1), 0, 0)
    lmap = lambda b, *_: (jnp.maximum(b - nc, 0), 0, 0)
    whole = lambda *_: (0, 0)
    once = {"pipeline_mode": pl.Buffered(1)}
    arb = pltpu.CompilerParams(dimension_semantics=("arbitrary",), vmem_limit_bytes=VMEM_LIMIT)
    smem = pl.BlockSpec(memory_space=pltpu.SMEM)
    seg_i32 = jax.ShapeDtypeStruct((nb * N_GROUPS,), jnp.int32)
    tile_i32 = jax.ShapeDtypeStruct((n_tiles,), jnp.int32)

    xs, cs, start, npiece, off, tgroup, tvalid = pl.pallas_call(
        functools.partial(_dispatch_kernel, nc, nb, n_tiles),
        grid_spec=pltpu.PrefetchScalarGridSpec(
            num_scalar_prefetch=0, grid=(nb,),
            in_specs=[
                pl.BlockSpec((1, MOE_BLK, D_MODEL), cmap), pl.BlockSpec((1, MOE_BLK, D_MODEL), lmap),
                pl.BlockSpec((1, MOE_BLK, LANES), cmap), pl.BlockSpec((1, MOE_BLK, LANES), lmap),
                pl.BlockSpec((1, 8, MOE_BLK), cmap), pl.BlockSpec((1, 8, MOE_BLK), lmap),
                pl.BlockSpec(cntc.shape, lambda b: (0, 0, 0)), pl.BlockSpec(cntl.shape, lambda b: (0, 0, 0)),
            ],
            out_specs=[pl.BlockSpec((ns, D_MODEL), whole, **once), pl.BlockSpec((ns, 2 * LANES), whole, **once),
                       smem, smem, smem, smem, smem],
            scratch_shapes=[pltpu.VMEM((SORT_ROWS, D_MODEL), BF16), pltpu.VMEM((SORT_ROWS, 2 * LANES), BF16)],
        ),
        out_shape=[jax.ShapeDtypeStruct((ns, D_MODEL), BF16), jax.ShapeDtypeStruct((ns, 2 * LANES), BF16),
                   seg_i32, seg_i32, seg_i32, tile_i32, tile_i32],
        compiler_params=arb,
        name="moe_dispatch",
    )(h2c, h2l, cbc, cbl, rtc, rtl, cntc, cntl)

    wmap = lambda i, tg, tv: (tg[i], 0, 0)
    ys = pl.pallas_call(
        _experts_kernel,
        grid_spec=pltpu.PrefetchScalarGridSpec(
            num_scalar_prefetch=2, grid=(n_tiles,),
            in_specs=[
                pl.BlockSpec((MOE_TM, D_MODEL), lambda i, *_: (i, 0)),
                pl.BlockSpec((MOE_TM, 2 * LANES), lambda i, *_: (i, 0)),
                pl.BlockSpec((EXPERTS_PER_GROUP, D_MODEL, D_EXPERT), wmap),
                pl.BlockSpec((EXPERTS_PER_GROUP, D_MODEL, D_EXPERT), wmap),
                pl.BlockSpec((EXPERTS_PER_GROUP, D_EXPERT, D_MODEL), wmap),
            ],
            out_specs=pl.BlockSpec((MOE_TM, D_MODEL), lambda i, *_: (i, 0)),
        ),
        out_shape=jax.ShapeDtypeStruct((ns, D_MODEL), BF16),
        compiler_params=arb,
        name="moe_experts",
    )(tgroup, tvalid, xs, cs, wg, wu, wd)

    yc, yl = pl.pallas_call(
        functools.partial(_combine_kernel, nc, blocks_per_lat_seq),
        grid_spec=pltpu.PrefetchScalarGridSpec(
            num_scalar_prefetch=3, grid=(nb,),
            in_specs=[
                pl.BlockSpec((1, MOE_BLK, D_MODEL), cmap), pl.BlockSpec((1, MOE_BLK, D_MODEL), lmap),
                pl.BlockSpec((1, MOE_BLK, LANES), cmap), pl.BlockSpec((1, MOE_BLK, LANES), lmap),
                pl.BlockSpec((ns, D_MODEL), whole, **once),
                pl.BlockSpec(mod.shape, lambda *_: (0, 0, 0)),
                pl.BlockSpec((1, D_MODEL), whole),
            ],
            out_specs=[pl.BlockSpec((1, MOE_BLK, D_MODEL), cmap), pl.BlockSpec((1, MOE_BLK, D_MODEL), lmap)],
            scratch_shapes=[pltpu.VMEM((SORT_ROWS, D_MODEL), BF16)],
        ),
        out_shape=[jax.ShapeDtypeStruct((nc, MOE_BLK, D_MODEL), F32),
                   jax.ShapeDtypeStruct((nl, MOE_BLK, D_MODEL), F32)],
        compiler_params=arb,
        name="moe_combine",
    )(start, npiece, off, x1c, x1l, cbc, cbl, ys, mod, gf)
    return yc, yl


def _prep_weights(norm1_g, w_in, b_in, b_gates, w_dw, b_dw, conv_ln_g, conv_ln_b, w_conv_out,
                  mlstm_hn_g, w_mlstm_out, w_o, norm2_g, w_rg, b_rg, w_re, b_re):
    s_a = 2 * D_CONV
    s_q = s_a + D_MLSTM
    s_k = s_q + D_MLSTM
    s_v = s_k + D_MLSTM
    s_o = s_v + D_MLSTM
    s_g = s_o + 4 * N_HEADS
    row = lambda v: v.reshape(1, -1).astype(F32)
    w_t = w_in.T
    keep = [(0, s_q), (s_k, s_o), (s_g, w_in.shape[1])]
    halved = [(D_CONV, s_a), (s_v, s_o), (s_g, w_in.shape[1])]
    is_halved = lambda r: any(a <= r < b for a, b in halved)
    blocks = [r for a, b in keep for r in range(a, b, WPREP_ROWS)]
    wrow = _transpose_cast(w_t, blocks, [is_halved(r) for r in blocks])
    bias_scale = jnp.array([0.5 if is_halved(r) else 1.0 for a, b in keep for r in range(a, b)]
                           + [1.0] * D_MLSTM, F32)
    bg = (b_in[s_o:s_g] + b_gates.reshape(-1)).reshape(2, 2, N_HEADS).transpose(1, 0, 2).reshape(-1, 1)
    row_window = lambda start, n: _RowWindow(w_t, start, n)
    n_rt = N_EXPERTS + N_GROUPS
    wrt = jnp.pad(jnp.concatenate([w_re, w_rg], axis=1), ((0, 0), (0, LANES - n_rt)))
    wrt_hi = wrt.astype(BF16)
    wrt2 = jnp.concatenate([wrt_hi, (wrt - wrt_hi.astype(F32)).astype(BF16)], axis=1)
    brtT = jnp.pad(jnp.concatenate([b_re, b_rg]), (0, LANES - n_rt)).reshape(LANES, 1)
    return {
        "g1": row(norm1_g),
        "wrow": wrow, "brow": row(jnp.concatenate([b_in[a:b] for a, b in keep] + [b_in[s_q:s_k]]) * bias_scale),
        "wkT": row_window(s_q, D_MLSTM), "wgifT": row_window(s_o, 4 * N_HEADS), "bgifT": bg,
        "wdw": w_dw.astype(F32), "bdw": row(b_dw), "lng": row(conv_ln_g), "lnb": row(conv_ln_b),
        "wco": w_conv_out.astype(BF16), "hng": row(mlstm_hn_g), "wmo": w_mlstm_out.astype(BF16),
        "wo": w_o.astype(BF16), "g2": row(norm2_g), "wrt2": wrt2, "brtT": brtT,
    }


def kernel(x_prompt, x_sample, state_C, state_n, state_m, c, c_ctx, norm1_g, w_ada, b_ada, w_in, b_in, b_gates, w_dw, b_dw, conv_ln_g, conv_ln_b, w_conv_out, mlstm_hn_g, w_mlstm_out, w_o, norm2_g, w_rg, b_rg, w_re, b_re, w_e_gate, w_e_up, w_e_down, norm_final_g):
    B, S, _ = x_prompt.shape
    Bd, Sd, _ = x_sample.shape
    assert w_ada.shape[0] == 1, "single trunk layer"
    assert MIX_TM % S == 0 and S % SUB == 0 and Sd % MIX_TM == 0

    mod = _ada(c_ctx.reshape(1, -1), c, w_ada[0], b_ada[0].reshape(1, -1))

    wts = _prep_weights(norm1_g[0], w_in[0], b_in[0], b_gates[0], w_dw[0], b_dw[0], conv_ln_g[0],
                        conv_ln_b[0], w_conv_out[0], mlstm_hn_g[0], w_mlstm_out[0], w_o[0],
                        norm2_g[0], w_rg[0], b_rg[0], w_re[0], b_re[0])

    x1p, h2p, cbp, rtp, cntp, c_new, n_new, m_new = _mixer(
        x_prompt.reshape(B * S // MIX_TM, MIX_TM, D_MODEL), S, mod, lambda b: 0, wts, P=S, emit_state=True)

    state = (state_C[:, 0].reshape(Bd, N_UNITS, HEAD_DIM, HEAD_DIM), state_n[:, 0].reshape(Bd, N_UNITS, HEAD_DIM),
             state_m[:, 0].reshape(Bd, N_UNITS))
    x1s, h2s, cbs, rts, cnts = _mixer(x_sample, Sd, mod, lambda b: 1 + b, wts, P=GRID_W, state=state)

    nc, nl = B * S // MOE_BLK, Bd * Sd // MOE_BLK
    blk = lambda a, n: a.reshape(n, MOE_BLK, a.shape[-1])
    yp, ys = _moe(blk(x1p, nc), blk(x1s, nl), blk(h2p, nc), blk(h2s, nl), blk(cbp, nc), blk(cbs, nl),
                  rtp.reshape(nc, 8, MOE_BLK), rts.reshape(nl, 8, MOE_BLK),
                  cntp.reshape(nc, 8, LANES), cnts.reshape(nl, 8, LANES),
                  mod, Sd // MOE_BLK, w_e_gate[0], w_e_up[0], w_e_down[0], norm_final_g.reshape(1, -1))

    return (yp.reshape(B, S, D_MODEL), ys.reshape(Bd, Sd, D_MODEL),
            c_new.reshape(B, 1, 2, N_HEADS, HEAD_DIM, HEAD_DIM),
            n_new.reshape(B, 1, 2, N_HEADS, HEAD_DIM),
            m_new[:, :, 0].reshape(B, 1, 2, N_HEADS))
```

```python
import functools
from typing import NamedTuple

import jax
import jax.numpy as jnp
from jax import lax
from jax.experimental import pallas as pl
from jax.experimental.pallas import tpu as pltpu

D_MODEL = 1024
D_CONV = 512
CONV_K = 31
D_MLSTM = 512
N_HEADS = 4
HEAD_DIM = D_MLSTM // N_HEADS
N_GROUPS = 4
EXPERTS_PER_GROUP = 4
N_EXPERTS = N_GROUPS * EXPERTS_PER_GROUP
D_EXPERT = 256
N_ADA = 6
EPS = 1e-6
GRID_W = 64

LANES = 128
SUB = 256
CONV_PAD = 16
CONV_RB = 64
N_UNITS = 2 * N_HEADS
ROW_ALIGN = 16
CHAIN_SLACK = 1
COPY_RUN = 4
MOE_TM = 512
MIX_TM = 512
MOE_BLK = MIX_TM
SORT_ROWS = MOE_BLK + N_GROUPS * ROW_ALIGN
ADA_PER_STEP = 2
WPREP_ROWS = 512
ROUTE_GROUP_LANE = N_EXPERTS
ROUTE_RANK_LANE = N_EXPERTS + 1
VMEM_LIMIT = 58 * 1024 * 1024

BF16 = jnp.bfloat16
F32 = jnp.float32
NT_DIMS = (((1,), (1,)), ((), ()))


def _dot(a, b):
    return jnp.dot(a, b, preferred_element_type=F32)


def _dot_nt(a, b, precision=None):
    return lax.dot_general(a, b, NT_DIMS, preferred_element_type=F32, precision=precision)


def _sigmoid(x):
    return 0.5 * jnp.tanh(0.5 * x) + 0.5


def _sigmoid_of_half(xh):
    return 0.5 * jnp.tanh(xh) + 0.5


def _log_sigmoid(x):
    return jnp.minimum(x, 0.0) - jnp.log1p(jnp.exp(-jnp.abs(x)))


def _split3(x):
    hi = x.astype(BF16).astype(F32)
    r1 = x - hi
    mid = r1.astype(BF16).astype(F32)
    lo = (r1 - mid).astype(BF16).astype(F32)
    return hi, mid, lo


def _ada_kernel(cctx_ref, c_ref, w_ref, b_ref, o_ref):
    n = 1 + c_ref.shape[0]
    c = jnp.concatenate([cctx_ref[...], c_ref[...], jnp.zeros((8 - n, D_MODEL), F32)], axis=0)
    s = (c * _sigmoid(c)).astype(BF16)
    out = _dot(s, w_ref[...].astype(BF16)) + b_ref[...]
    for v in range(ADA_PER_STEP):
        o_ref[v] = out[:, v * D_MODEL:(v + 1) * D_MODEL]


def _ada(c_ctx, c, w_ada, b_ada):
    return pl.pallas_call(
        _ada_kernel,
        grid=(N_ADA // ADA_PER_STEP,),
        in_specs=[
            pl.BlockSpec(c_ctx.shape, lambda j: (0, 0)),
            pl.BlockSpec(c.shape, lambda j: (0, 0)),
            pl.BlockSpec((D_MODEL, ADA_PER_STEP * D_MODEL), lambda j: (0, j)),
            pl.BlockSpec((1, ADA_PER_STEP * D_MODEL), lambda j: (0, j)),
        ],
        out_specs=pl.BlockSpec((ADA_PER_STEP, 8, D_MODEL), lambda j: (j, 0, 0)),
        out_shape=jax.ShapeDtypeStruct((N_ADA, 8, D_MODEL), F32),
        compiler_params=pltpu.CompilerParams(dimension_semantics=("arbitrary",)),
        name="ada",
    )(c_ctx, c, w_ada, b_ada)


def _transpose_cast_kernel(starts_ref, halve_ref, wt_ref, o_ref):
    scale = jnp.where(halve_ref[pl.program_id(0)] == 1, 0.5, 1.0)
    o_ref[...] = (wt_ref[...] * scale).astype(BF16).T


def _transpose_cast(w_t, row_starts, halve):
    n, k = len(row_starts), w_t.shape[1]
    return pl.pallas_call(
        _transpose_cast_kernel,
        grid_spec=pltpu.PrefetchScalarGridSpec(
            num_scalar_prefetch=2, grid=(n,),
            in_specs=[pl.BlockSpec((pl.Element(WPREP_ROWS), pl.Element(k)), lambda j, starts, hv: (starts[j] * 8, 0))],
            out_specs=pl.BlockSpec((k, WPREP_ROWS), lambda j, starts, hv: (0, j)),
        ),
        out_shape=jax.ShapeDtypeStruct((k, n * WPREP_ROWS), BF16),
        compiler_params=pltpu.CompilerParams(dimension_semantics=("arbitrary",)),
        name="transpose_cast",
    )(jnp.array([r // 8 for r in row_starts], jnp.int32), jnp.array([int(h) for h in halve], jnp.int32), w_t)


WROW_OFFSET = {"wq": 2 * D_CONV, "wv": 2 * D_CONV + D_MLSTM, "wog": 2 * D_CONV + 2 * D_MLSTM,
               "wgm": 2 * D_CONV + 3 * D_MLSTM}
BROW_K_OFFSET = 2 * D_CONV + 3 * D_MLSTM + 2 * D_MODEL

_MIXER_WEIGHTS = (
    "g1", "wrow", "brow", "wkT", "wgifT", "bgifT", "wdw", "bdw", "lng", "lnb",
    "wco", "hng", "wmo", "wo", "g2", "wrt2", "brtT",
)


def _zero_after(x):
    bits = lax.bitcast_convert_type(x, jnp.uint32)
    bits = lax.shift_right_logical(lax.shift_right_logical(bits, jnp.uint32(16)), jnp.uint32(16))
    return lax.bitcast_convert_type(bits, F32)[0:1, :]


def _conv_block(upad_s, seg, base, cs, wdw_ref, bdw_ref, after=None):
    sub = 8
    first = CONV_PAD - CONV_K // 2
    acc = jnp.broadcast_to(bdw_ref[0:1, cs], (CONV_RB, LANES))
    for r in range(sub):
        z = None
        for a in range((CONV_K + first + sub - 1) // sub):
            j = sub * a + r - first
            if 0 <= j < CONV_K:
                lo = base + sub * a
                tap = wdw_ref[j:j + 1, cs] if after is None else wdw_ref[j:j + 1, cs] + after
                term = tap * upad_s[seg, lo:lo + CONV_RB + sub, cs]
                z = term if z is None else z + term
        acc = acc + z[r:r + CONV_RB, :]
    return acc


def _mixer_kernel(R, T, P, has_state, emit_state, mod_index, *refs):
    L = SUB
    n_mt = R // MIX_TM
    cpm = MIX_TM // L
    n_seq = R // T
    cps = T // L
    nseg = MIX_TM // P
    assert not has_state or n_seq == 1
    it = iter(refs)
    x_ref = next(it)
    mod_ref = next(it)
    if has_state:
        c0_ref = next(it)
        n0_ref = next(it)
        m0_ref = next(it)
    w = {name: next(it) for name in _MIXER_WEIGHTS}
    x1_ref = next(it)
    h2_ref = next(it)
    comb_ref = next(it)
    route_ref = next(it)
    cnt_ref = next(it)
    if emit_state:
        cout_ref = next(it)
        nout_ref = next(it)
        mout_ref = next(it)
    (q_s, kT_s, v_s, so_s, scan_s, ma_s, sgb_s, hm_s, cst_s, upad_s) = [next(it) for _ in range(10)]

    cond_row = mod_index(pl.program_id(0))

    def mod_row(i):
        return mod_ref[i, pl.ds(cond_row, 1), :]

    zpad = jnp.zeros((CONV_PAD, D_CONV), F32)
    for seg in range(nseg):
        upad_s[seg, 0:CONV_PAD, :] = zpad
        upad_s[seg, CONV_PAD + P:CONV_PAD + P + CONV_PAD, :] = zpad

    t_idx = lax.broadcasted_iota(jnp.int32, (L, L), 0)
    s_idx = lax.broadcasted_iota(jnp.int32, (L, L), 1)
    lower = s_idx <= t_idx
    upper = s_idx >= t_idx
    triu_b = upper.astype(F32).astype(BF16)
    lane_u = lax.broadcasted_iota(jnp.int32, (N_UNITS, L), 1)
    is_bwd = lax.broadcasted_iota(jnp.int32, (N_UNITS, L), 0) >= N_HEADS

    def gate_scan(g):
        gi, lf = g[:N_UNITS], _log_sigmoid(g[N_UNITS:])
        pr = _dot(jnp.concatenate(_split3(lf), axis=0).astype(BF16), triu_b)
        pre = pr[0:N_UNITS] + pr[N_UNITS:2 * N_UNITS] + pr[2 * N_UNITS:]
        tot = pre[:, L - 1:L]
        bsum = jnp.where(is_bwd, tot - pre + lf, pre)
        a = gi - bsum
        pm, sm, k = a, a, 1
        while k < L:
            pm = jnp.where(lane_u >= k, jnp.maximum(pm, pltpu.roll(pm, k, axis=1)), pm)
            sm = jnp.where(lane_u < L - k, jnp.maximum(sm, pltpu.roll(sm, L - k, axis=1)), sm)
            k *= 2
        wide = lambda v: jnp.broadcast_to(v, (N_UNITS, L))
        return jnp.concatenate([a, jnp.where(is_bwd, sm, pm), bsum, wide(tot),
                                wide(jnp.max(a, axis=1, keepdims=True))], axis=0)

    def phase1(i, carry):
        r0 = pl.multiple_of(i * MIX_TM, MIX_TM)
        rows = pl.ds(r0, MIX_TM)
        x = x_ref[0, rows, :]
        xn = x * lax.rsqrt(jnp.mean(x * x, axis=-1, keepdims=True) + EPS) * w["g1"][...]
        hb = (xn * (1.0 + mod_row(1)) + mod_row(0)).astype(BF16)

        ag = _dot(hb, w["wrow"][:, :2 * D_CONV]) + w["brow"][:, :2 * D_CONV]
        u = ag[:, :D_CONV] * _sigmoid_of_half(ag[:, D_CONV:])
        for seg in range(nseg):
            upad_s[seg, CONV_PAD:CONV_PAD + P, :] = u[seg * P:(seg + 1) * P, :]
        gates = _dot_nt(w["wgifT"][...].astype(BF16), hb)
        gates = jnp.concatenate([gates[d * 2 * N_HEADS + g * N_HEADS:d * 2 * N_HEADS + (g + 1) * N_HEADS]
                                 for g in range(2) for d in range(2)], axis=0) + w["bgifT"][...]
        for j in range(cpm):
            scan_s[i * cpm + j] = gate_scan(gates[:, j * L:(j + 1) * L])

        def proj(name, c0, width=2 * LANES):
            w0 = WROW_OFFSET[name] + c0
            return _dot(hb, w["wrow"][:, w0:w0 + width]) + w["brow"][:, w0:w0 + width]

        last = lambda z: z[-8:, -LANES:]
        bk_row = w["brow"][:, BROW_K_OFFSET:BROW_K_OFFSET + D_MLSTM]
        bk_col = jnp.concatenate([bk_row, jnp.zeros((LANES - 1, D_MLSTM), F32)], axis=0).T[:, 0:1]

        def gm_a(c0):
            z = proj("wgm", c0)
            ma_s[rows, c0:c0 + 2 * LANES] = _sigmoid_of_half(z)
            return last(z)

        def gm_b(c0):
            z = proj("wgm", D_MODEL + c0)
            sgb_s[rows, c0:c0 + 2 * LANES] = _sigmoid_of_half(z)
            return last(z)

        def q_part(c0):
            z = proj("wq", c0)
            q_s[rows, c0:c0 + 2 * LANES] = (z * (HEAD_DIM ** -0.5)).astype(BF16)
            return last(z)

        def v_part(c0):
            z = proj("wv", c0)
            v_s[rows, c0:c0 + 2 * LANES] = z.astype(BF16)
            return last(z)

        def o_part(c0):
            z = proj("wog", c0)
            so_s[rows, c0:c0 + 2 * LANES] = _sigmoid_of_half(z)
            return last(z)

        def k_part(c0):
            rs = slice(c0, c0 + 2 * LANES)
            z = _dot_nt(w["wkT"][rs, :].astype(BF16), hb) + bk_col[rs, :]
            kt = z.astype(BF16)
            for j in range(cpm):
                kT_s[i * cpm + j, rs, :] = kt[:, j * L:(j + 1) * L]
            return last(z)

        jobs = ([functools.partial(gm_a, c0) for c0 in range(0, D_MODEL, 2 * LANES)]
                + [functools.partial(gm_b, c0) for c0 in range(0, D_MODEL, 2 * LANES)]
                + [functools.partial(f, c0) for f in (q_part, k_part, v_part, o_part)
                   for c0 in range(0, D_MLSTM, 2 * LANES)])
        n_jobs = len(jobs)
        conv = {}
        after, lag = None, [None] * CHAIN_SLACK
        n_pieces = (D_CONV // LANES) * nseg * (P // CONV_RB)
        for cb in range(D_CONV // LANES):
            cs = slice(cb * LANES, (cb + 1) * LANES)
            for seg in range(nseg):
                for rb in range(P // CONV_RB):
                    blk = _conv_block(upad_s, seg, rb * CONV_RB, cs, w["wdw"], w["bdw"], after)
                    conv[(cb, seg, rb)] = blk
                    if jobs and len(conv) * n_jobs >= (n_jobs - len(jobs) + 1) * n_pieces:
                        lag.append(_zero_after(jobs.pop(0)()))
                        after = lag.pop(0)
        for job in jobs:
            job()
        cu = jnp.concatenate(
            [jnp.concatenate([conv[(cb, seg, rb)] for seg in range(nseg) for rb in range(P // CONV_RB)], axis=0)
             for cb in range(D_CONV // LANES)], axis=1)
        mu = jnp.mean(cu, axis=-1, keepdims=True)
        cc = cu - mu
        cn = cc * lax.rsqrt(jnp.mean(cc * cc, axis=-1, keepdims=True) + EPS) * w["lng"][...] + w["lnb"][...]
        ca = (cn * _sigmoid(cn)).astype(BF16)
        ma_s[rows, :] = ma_s[rows, :] * _dot(ca, w["wco"][...])
        return carry

    if n_mt == 1:
        phase1(0, 0)
    else:
        lax.fori_loop(0, n_mt, phase1, 0)

    ones_col = (lax.broadcasted_iota(jnp.int32, (L, HEAD_DIM), 1) == 0).astype(F32).astype(BF16)
    pad_rows = jnp.zeros((LANES - 3 * N_UNITS, L), F32)

    def gate_prep(c, m_vec):
        sc = scan_s[c]
        a, run_max, bsum = sc[0:N_UNITS], sc[N_UNITS:2 * N_UNITS], sc[2 * N_UNITS:3 * N_UNITS]
        tot, a_max = sc[3 * N_UNITS:4 * N_UNITS, 0:1], sc[4 * N_UNITS:5 * N_UNITS, 0:1]
        big_m = jnp.maximum(m_vec, run_max)
        m_end = jnp.maximum(m_vec, a_max)
        cols = jnp.concatenate(
            [big_m, jnp.exp(m_vec - big_m), jnp.exp(-bsum - big_m), pad_rows], axis=0).T
        return a, cols, jnp.exp(a - m_end), jnp.exp(m_vec - m_end), tot + m_end

    def unit_group(dirs, c, prep, first_chunk, want_state):
        a, cols, wk, decay, _ = prep
        rows = slice(c * L, (c + 1) * L)
        heads = range(N_HEADS)
        units = [(d, hd) for d in dirs for hd in heads]
        hs = [slice(hd * HEAD_DIM, (hd + 1) * HEAD_DIM) for hd in heads]
        idx = {u: u[0] * N_HEADS + u[1] for u in units}
        col = lambda k, u: cols[:, k * N_UNITS + idx[u]:k * N_UNITS + idx[u] + 1]
        row = lambda arr, u: arr[idx[u]:idx[u] + 1, :]
        chained = has_state or not first_chunk
        qc = [q_s[rows, hs[hd]] for hd in heads]
        kTc = [kT_s[c, hs[hd], :] for hd in heads]
        vaug = [jnp.concatenate([v_s[rows, hs[hd]], ones_col], axis=1) for hd in heads]
        qk = [_dot(qc[hd], kTc[hd]) for hd in heads]
        s_mat = {u: (qk[u[1]] * jnp.where(lower if u[0] == 0 else upper, jnp.exp(row(a, u) - col(0, u)), 0.0)
                     ).astype(BF16) for u in units}
        nd = {u: _dot(s_mat[u], vaug[u[1]]) for u in units}
        if chained:
            nd = {u: nd[u] + col(1, u) * _dot(qc[u[1]], cst_s[idx[u]].astype(BF16)) for u in units}
        h = {u: nd[u][:, :HEAD_DIM] * (1.0 / jnp.maximum(jnp.abs(nd[u][:, HEAD_DIM:HEAD_DIM + 1]), col(2, u)))
             for u in units}
        for hd in heads:
            total = h[(dirs[0], hd)]
            for d in dirs[1:]:
                total = total + h[(d, hd)]
            if dirs[0] == 0:
                hm_s[rows, hs[hd]] = total
            else:
                hm_s[rows, hs[hd]] = hm_s[rows, hs[hd]] + total
        if want_state:
            kw = {u: (kTc[u[1]].astype(F32) * row(wk, u)).astype(BF16) for u in units}
            upd = {u: _dot(kw[u], vaug[u[1]]) for u in units}
            for u in units:
                cst_s[idx[u]] = (upd[u] + row(decay, u) * cst_s[idx[u]]) if chained else upd[u]

    dir_rows = lax.broadcasted_iota(jnp.int32, (N_UNITS, 1), 0) >= N_HEADS
    for seq in range(n_seq):
        if has_state:
            n_cols = jnp.concatenate([n0_ref[0], jnp.zeros((LANES - N_UNITS, HEAD_DIM), F32)], axis=0).T
            first_lane = lax.broadcasted_iota(jnp.int32, (HEAD_DIM, HEAD_DIM), 1) == 0
            for idx in range(N_UNITS):
                cst_s[idx, :, :HEAD_DIM] = c0_ref[0, idx]
                cst_s[idx, :, HEAD_DIM:] = jnp.where(first_lane, n_cols[:, idx:idx + 1], 0.0)
            unit_row = lax.broadcasted_iota(jnp.int32, (N_UNITS, 1), 0)
            m_vec = jnp.zeros((N_UNITS, 1), F32)
            for idx in range(N_UNITS):
                m_vec = jnp.where(unit_row == idx, m0_ref[pl.program_id(0), idx], m_vec)
        else:
            m_vec = jnp.zeros((N_UNITS, 1), F32)
        if cps == 1:
            prep = gate_prep(seq, m_vec)
            unit_group([0, 1], seq, prep, True, emit_state)
            m_vec = prep[4]
        else:
            for d in range(2):
                order = list(range(cps)) if d == 0 else list(range(cps - 1, -1, -1))
                for pos, c in enumerate(order):
                    prep = gate_prep(seq * cps + c, m_vec)
                    unit_group([d], seq * cps + c, prep, pos == 0, emit_state or pos < cps - 1)
                    m_vec = jnp.where(dir_rows == (d == 1), prep[4], m_vec)
        if emit_state:
            for idx in range(N_UNITS):
                caug = cst_s[idx]
                cout_ref[0, seq * N_UNITS + idx] = caug[:, :HEAD_DIM]
                nout_ref[0, seq * N_UNITS + idx:seq * N_UNITS + idx + 1, :] = caug[:, HEAD_DIM:].T[0:1, :]
            mout_ref[0, seq * N_UNITS:(seq + 1) * N_UNITS, :] = jnp.broadcast_to(m_vec, (N_UNITS, LANES))

    e_iota = lax.broadcasted_iota(jnp.int32, (LANES, MIX_TM), 0)
    g_of_e = lax.shift_right_logical(e_iota, 2)
    j_of_e = lax.bitwise_and(e_iota, EXPERTS_PER_GROUP - 1)
    r8 = lax.broadcasted_iota(jnp.int32, (8, MIX_TM), 0)
    before_b = (lax.broadcasted_iota(jnp.int32, (MOE_BLK, MOE_BLK), 0)
                < lax.broadcasted_iota(jnp.int32, (MOE_BLK, MOE_BLK), 1)).astype(F32).astype(BF16)

    def phase3(i, carry):
        r0 = pl.multiple_of(i * MIX_TM, MIX_TM)
        rows = pl.ds(r0, MIX_TM)
        hm = hm_s[rows, :]
        heads = []
        for hd in range(N_HEADS):
            hh = hm[:, hd * HEAD_DIM:(hd + 1) * HEAD_DIM]
            heads.append(hh * lax.rsqrt(jnp.mean(hh * hh, axis=-1, keepdims=True) + EPS))
        hn = jnp.concatenate(heads, axis=1) * w["hng"][...]
        hb2 = (so_s[rows, :] * hn).astype(BF16)
        br_b = _dot(hb2, w["wmo"][...])
        mixed = (ma_s[rows, :] + sgb_s[rows, :] * br_b).astype(BF16)
        x1 = x_ref[0, rows, :] + mod_row(2) * _dot(mixed, w["wo"][...])
        x1_ref[0, rows, :] = x1
        xn = x1 * lax.rsqrt(jnp.mean(x1 * x1, axis=-1, keepdims=True) + EPS) * w["g2"][...]
        h2 = xn * (1.0 + mod_row(4)) + mod_row(3)
        h2_ref[0, rows, :] = h2.astype(BF16)

        h2_hi = h2.astype(BF16)
        h2_lo = (h2 - h2_hi.astype(F32)).astype(BF16)
        lg = _dot(h2_hi, w["wrt2"][...])
        lg = lg[:, :LANES] + lg[:, LANES:] + _dot(h2_lo, w["wrt2"][:, :LANES])
        lt = lg.T + w["brtT"][...]
        gl = [lt[N_EXPERTS + g:N_EXPERTS + g + 1, :] for g in range(N_GROUPS)]
        best, gsel = gl[0], jnp.zeros((1, MIX_TM), jnp.int32)
        for g in range(1, N_GROUPS):
            better = gl[g] > best
            gsel = jnp.where(better, g, gsel)
            best = jnp.where(better, gl[g], best)
        gp_sel = 1.0 / sum(jnp.exp(v - best) for v in gl)
        el = []
        for j in range(EXPERTS_PER_GROUP):
            v = lt[j:j + 1, :]
            for g in range(1, N_GROUPS):
                r = g * EXPERTS_PER_GROUP + j
                v = jnp.where(gsel == g, lt[r:r + 1, :], v)
            el.append(v)
        l1, e1 = el[0], jnp.zeros((1, MIX_TM), jnp.int32)
        for j in range(1, EXPERTS_PER_GROUP):
            better = el[j] > l1
            e1 = jnp.where(better, j, e1)
            l1 = jnp.where(better, el[j], l1)
        l2 = jnp.full((1, MIX_TM), -jnp.inf, F32)
        e2 = jnp.zeros((1, MIX_TM), jnp.int32)
        for j in range(EXPERTS_PER_GROUP):
            better = jnp.logical_and(e1 != j, el[j] > l2)
            e2 = jnp.where(better, j, e2)
            l2 = jnp.where(better, el[j], l2)
        r2 = jnp.exp(l2 - l1)
        wt1 = gp_sel / (1.0 + r2)
        wt2 = gp_sel * r2 / (1.0 + r2)
        in_group = g_of_e == gsel
        comb_t = (jnp.where(jnp.logical_and(in_group, j_of_e == e1), wt1, 0.0)
                  + jnp.where(jnp.logical_and(in_group, j_of_e == e2), wt2, 0.0))

        onehot = (r8 == gsel).astype(F32)
        gsel_f = gsel.astype(F32)
        rank = jnp.sum(onehot * _dot(onehot.astype(BF16), before_b), axis=0, keepdims=True)
        r8rows = pl.ds(pl.multiple_of(i * 8, 8), 8)
        route_ref[0, r8rows, :] = jnp.where(r8 == 0, gsel_f, jnp.where(r8 == 1, rank, 0.0))
        cnt_ref[0, r8rows, :] = jnp.broadcast_to(jnp.sum(onehot, axis=1, keepdims=True), (8, LANES))
        comb_t = jnp.where(e_iota == ROUTE_GROUP_LANE, gsel_f,
                           jnp.where(e_iota == ROUTE_RANK_LANE, rank, comb_t))
        comb_ref[0, rows, :] = comb_t.T
        return carry

    if n_mt == 1:
        phase3(0, 0)
    else:
        lax.fori_loop(0, n_mt, phase3, 0)


class _RowWindow(NamedTuple):
    array: jax.Array
    start: int
    n: int


def _const_spec(a):
    if isinstance(a, _RowWindow):
        assert a.start % a.n == 0
        return a.array, pl.BlockSpec((a.n, a.array.shape[1]), lambda b: (a.start // a.n, 0),
                                     pipeline_mode=pl.Buffered(1))
    nd = a.ndim
    return a, pl.BlockSpec(a.shape, lambda b, _nd=nd: (0,) * _nd, pipeline_mode=pl.Buffered(1))


def _mixer(x, T, mod, mod_index, weights, P, state=None, emit_state=False):
    B, R, _ = x.shape
    n_chunks = R // SUB
    n_blk = R // MOE_BLK
    n_seq = R // T
    has_state = state is not None
    seq_mode = {} if R <= MIX_TM else {"pipeline_mode": pl.Buffered(1)}
    in_specs = [
        pl.BlockSpec((1, R, D_MODEL), lambda b: (b, 0, 0), **seq_mode),
        pl.BlockSpec(mod.shape, lambda b: (0, 0, 0)),
    ]
    args = [x, mod]
    if has_state:
        c0, n0, m0 = state
        in_specs += [
            pl.BlockSpec((1, N_UNITS, HEAD_DIM, HEAD_DIM), lambda b: (b, 0, 0, 0)),
            pl.BlockSpec((1, N_UNITS, HEAD_DIM), lambda b: (b, 0, 0)),
            pl.BlockSpec(memory_space=pltpu.SMEM),
        ]
        args += [c0, n0, m0]
    for name in _MIXER_WEIGHTS:
        operand, spec = _const_spec(weights[name])
        in_specs.append(spec)
        args.append(operand)
    out_shape = [
        jax.ShapeDtypeStruct((B, R, D_MODEL), F32),
        jax.ShapeDtypeStruct((B, R, D_MODEL), BF16),
        jax.ShapeDtypeStruct((B, R, LANES), F32),
        jax.ShapeDtypeStruct((B, n_blk * 8, MOE_BLK), F32),
        jax.ShapeDtypeStruct((B, n_blk * 8, LANES), F32),
    ]
    out_specs = [
        pl.BlockSpec((1, R, D_MODEL), lambda b: (b, 0, 0), **seq_mode),
        pl.BlockSpec((1, R, D_MODEL), lambda b: (b, 0, 0), **seq_mode),
        pl.BlockSpec((1, R, LANES), lambda b: (b, 0, 0)),
        pl.BlockSpec((1, n_blk * 8, MOE_BLK), lambda b: (b, 0, 0)),
        pl.BlockSpec((1, n_blk * 8, LANES), lambda b: (b, 0, 0)),
    ]
    if emit_state:
        out_shape += [
            jax.ShapeDtypeStruct((B, n_seq * N_UNITS, HEAD_DIM, HEAD_DIM), F32),
            jax.ShapeDtypeStruct((B, n_seq * N_UNITS, HEAD_DIM), F32),
            jax.ShapeDtypeStruct((B, n_seq * N_UNITS, LANES), F32),
        ]
        out_specs += [
            pl.BlockSpec((1, n_seq * N_UNITS, HEAD_DIM, HEAD_DIM), lambda b: (b, 0, 0, 0)),
            pl.BlockSpec((1, n_seq * N_UNITS, HEAD_DIM), lambda b: (b, 0, 0)),
            pl.BlockSpec((1, n_seq * N_UNITS, LANES), lambda b: (b, 0, 0)),
        ]
    scratch = [
        pltpu.VMEM((R, D_MLSTM), BF16),
        pltpu.VMEM((n_chunks, D_MLSTM, SUB), BF16),
        pltpu.VMEM((R, D_MLSTM), BF16),
        pltpu.VMEM((R, D_MLSTM), F32),
        pltpu.VMEM((n_chunks, 5 * N_UNITS, SUB), F32),
        pltpu.VMEM((R, D_MODEL), F32),
        pltpu.VMEM((R, D_MODEL), F32),
        pltpu.VMEM((R, D_MLSTM), F32),
        pltpu.VMEM((N_UNITS, HEAD_DIM, 2 * HEAD_DIM), F32),
        pltpu.VMEM((MIX_TM // P, P + 2 * CONV_PAD, D_CONV), F32),
    ]
    return pl.pallas_call(
        functools.partial(_mixer_kernel, R, T, P, has_state, emit_state, mod_index),
        grid=(B,),
        in_specs=in_specs,
        out_specs=out_specs,
        out_shape=out_shape,
        scratch_shapes=scratch,
        compiler_params=pltpu.CompilerParams(
            dimension_semantics=("arbitrary",), vmem_limit_bytes=VMEM_LIMIT),
        name="mixer_T%d" % T,
    )(*args)


def _dest_in_block(group, rank, starts):
    dest = rank
    for g in range(N_GROUPS):
        dest = dest + jnp.where(group == float(g), starts[g], 0.0)
    return dest


def _copy_segments(src_refs, dst_refs, src_starts, dst_starts, n_pieces):
    def copy(g, first_piece, n_rows):
        s = pl.multiple_of(src_starts[g] + first_piece * ROW_ALIGN, ROW_ALIGN)
        d = pl.multiple_of(dst_starts[g] + first_piece * ROW_ALIGN, ROW_ALIGN)
        for src, dst in zip(src_refs, dst_refs):
            dst[pl.ds(d, n_rows), :] = src[pl.ds(s, n_rows), :]

    for g in range(N_GROUPS):
        n_runs = lax.shift_right_logical(n_pieces[g], COPY_RUN.bit_length() - 1)

        def run(k, carry, g=g):
            copy(g, k * COPY_RUN, COPY_RUN * ROW_ALIGN)
            return carry

        def single(k, carry, g=g):
            copy(g, k, ROW_ALIGN)
            return carry

        lax.fori_loop(0, n_runs, run, 0)
        lax.fori_loop(n_runs * COPY_RUN, n_pieces[g], single, 0)


def _plan_segments(n_blocks, n_tiles, count, start_ref, npiece_ref, off_ref, tgroup_ref, tvalid_ref):
    align_shift = ROW_ALIGN.bit_length() - 1
    tile_shift = MOE_TM.bit_length() - 1

    def block_starts(blk, carry):
        row = jnp.int32(0)
        for g in range(N_GROUPS):
            n = lax.shift_right_logical(count(blk, g) + (ROW_ALIGN - 1), align_shift)
            npiece_ref[blk * N_GROUPS + g] = n
            start_ref[blk * N_GROUPS + g] = row
            row = row + n * ROW_ALIGN
        return carry

    lax.fori_loop(0, n_blocks, block_starts, 0)

    base_row = jnp.int32(0)
    base_tile = jnp.int32(0)
    last_group = jnp.int32(0)
    for g in range(N_GROUPS):
        def seg_offsets(blk, row, g=g, base_row=base_row):
            off_ref[blk * N_GROUPS + g] = base_row + row
            return row + npiece_ref[blk * N_GROUPS + g] * ROW_ALIGN

        rows = lax.fori_loop(0, n_blocks, seg_offsets, jnp.int32(0))
        tiles = lax.shift_right_logical(rows + (MOE_TM - 1), tile_shift)

        def mark_tiles(t, carry, g=g, base_tile=base_tile):
            tgroup_ref[base_tile + t] = g
            tvalid_ref[base_tile + t] = 1
            return carry

        lax.fori_loop(0, tiles, mark_tiles, 0)
        last_group = jnp.where(tiles > 0, g, last_group)
        base_row = base_row + tiles * MOE_TM
        base_tile = base_tile + tiles

    def mark_unused(t, carry):
        tgroup_ref[t] = last_group
        tvalid_ref[t] = 0
        return carry

    lax.fori_loop(base_tile, n_tiles, mark_unused, 0)


def _dispatch_kernel(n_ctx_blocks, n_blocks, n_tiles,
                     h2c_ref, h2l_ref, cbc_ref, cbl_ref, rtc_ref, rtl_ref, cntc_ref, cntl_ref,
                     xs_ref, cs_ref, start_ref, npiece_ref, off_ref, tgroup_ref, tvalid_ref,
                     sx_s, sc_s):
    b = pl.program_id(0)
    is_ctx = b < n_ctx_blocks

    def count(blk, g):
        vc = cntc_ref[jnp.minimum(blk, n_ctx_blocks - 1), pl.ds(g, 1), pl.ds(0, 1)]
        vl = cntl_ref[jnp.maximum(blk - n_ctx_blocks, 0), pl.ds(g, 1), pl.ds(0, 1)]
        return jnp.where(blk < n_ctx_blocks, vc, vl)[0, 0].astype(jnp.int32)

    @pl.when(b == 0)
    def _():
        _plan_segments(n_blocks, n_tiles, count, start_ref, npiece_ref, off_ref, tgroup_ref, tvalid_ref)
        xs_ref[...] = jnp.zeros_like(xs_ref)
        cs_ref[...] = jnp.zeros_like(cs_ref)

    starts = [start_ref[b * N_GROUPS + g] for g in range(N_GROUPS)]

    def sort_block(h2_ref, cb_ref, rt_ref):
        h2 = h2_ref[0]
        cb = cb_ref[0]
        rt = rt_ref[0]
        dest = _dest_in_block(rt[0:1, :], rt[1:2, :], [s.astype(F32) for s in starts])
        row = lax.broadcasted_iota(jnp.int32, (SORT_ROWS, MOE_BLK), 0).astype(F32)
        perm = (row == dest).astype(F32).astype(BF16)
        cb_hi = cb.astype(BF16)
        cb_lo = (cb - cb_hi.astype(F32)).astype(BF16)
        srt = _dot(perm, jnp.concatenate([h2, cb_hi, cb_lo], axis=1)).astype(BF16)
        sx_s[...] = srt[:, :D_MODEL]
        sc_s[...] = srt[:, D_MODEL:]

    pl.when(is_ctx)(functools.partial(sort_block, h2c_ref, cbc_ref, rtc_ref))
    pl.when(jnp.logical_not(is_ctx))(functools.partial(sort_block, h2l_ref, cbl_ref, rtl_ref))
    _copy_segments((sx_s, sc_s), (xs_ref, cs_ref), starts,
                   [off_ref[b * N_GROUPS + g] for g in range(N_GROUPS)],
                   [npiece_ref[b * N_GROUPS + g] for g in range(N_GROUPS)])


def _experts_kernel(tgroup_ref, tvalid_ref, xs_ref, cs_ref, wg_ref, wu_ref, wd_ref, ys_ref):
    i = pl.program_id(0)

    @pl.when(tvalid_ref[i] == 1)
    def _():
        x = xs_ref[...]
        comb = cs_ref[:, :LANES].astype(F32) + cs_ref[:, LANES:].astype(F32)
        lane = lax.broadcasted_iota(jnp.int32, comb.shape, 1)
        first = tgroup_ref[i] * EXPERTS_PER_GROUP
        acc = None
        for j in range(EXPERTS_PER_GROUP):
            gj = _dot(x, wg_ref[j].astype(BF16))
            uj = _dot(x, wu_ref[j].astype(BF16))
            cw = jnp.sum(jnp.where(lane == first + j, comb, 0.0), axis=1, keepdims=True)
            out = _dot((gj * _sigmoid(gj) * uj * cw).astype(BF16), wd_ref[j].astype(BF16))
            acc = out if acc is None else acc + out
        ys_ref[...] = acc.astype(BF16)

    @pl.when(tvalid_ref[i] == 0)
    def _():
        ys_ref[...] = jnp.zeros_like(ys_ref)


def _combine_kernel(n_ctx_blocks, blocks_per_lat_seq, start_ref, npiece_ref, off_ref,
                    x1c_ref, x1l_ref, cbc_ref, cbl_ref, ys_ref, mod_ref, gf_ref, yc_ref, yl_ref, loc_s):
    b = pl.program_id(0)
    is_ctx = b < n_ctx_blocks
    starts = [start_ref[b * N_GROUPS + g] for g in range(N_GROUPS)]
    @pl.when(b == 0)
    def _():
        loc_s[...] = jnp.zeros_like(loc_s)

    _copy_segments((ys_ref,), (loc_s,), [off_ref[b * N_GROUPS + g] for g in range(N_GROUPS)], starts,
                   [npiece_ref[b * N_GROUPS + g] for g in range(N_GROUPS)])
    def finish_block(x1_ref, cb_ref, y_ref, mrow):
        cb = cb_ref[0]
        dest = _dest_in_block(cb[:, ROUTE_GROUP_LANE:ROUTE_GROUP_LANE + 1],
                              cb[:, ROUTE_RANK_LANE:ROUTE_RANK_LANE + 1],
                              [s.astype(F32) for s in starts])
        col = lax.broadcasted_iota(jnp.int32, (MOE_BLK, SORT_ROWS), 1).astype(F32)
        unperm = (col == dest).astype(F32).astype(BF16)
        x2 = x1_ref[0] + mod_ref[N_ADA - 1, pl.ds(mrow, 1), :] * _dot(unperm, loc_s[...])
        y_ref[0] = x2 * lax.rsqrt(jnp.mean(x2 * x2, axis=-1, keepdims=True) + EPS) * gf_ref[...]

    lat_row = 1 + jnp.maximum(b - n_ctx_blocks, 0) // blocks_per_lat_seq
    pl.when(is_ctx)(functools.partial(finish_block, x1c_ref, cbc_ref, yc_ref, 0))
    pl.when(jnp.logical_not(is_ctx))(functools.partial(finish_block, x1l_ref, cbl_ref, yl_ref, lat_row))


def _moe(x1c, x1l, h2c, h2l, cbc, cbl, rtc, rtl, cntc, cntl, mod, blocks_per_lat_seq, wg, wu, wd, gf):
    nc, nl = x1c.shape[0], x1l.shape[0]
    nb = nc + nl
    n_rows_max = nb * MOE_BLK + nb * N_GROUPS * (ROW_ALIGN - 1) + N_GROUPS * (MOE_TM - ROW_ALIGN)
    n_tiles = -(-n_rows_max // MOE_TM)
    ns = n_tiles * MOE_TM

    cmap = lambda b, *_: (jnp.minimum(b, nc - 1), 0, 0)
    lmap = lambda b, *_: (jnp.maximum(b - nc, 0), 0, 0)
    whole = lambda *_: (0, 0)
    once = {"pipeline_mode": pl.Buffered(1)}
    arb = pltpu.CompilerParams(dimension_semantics=("arbitrary",), vmem_limit_bytes=VMEM_LIMIT)
    smem = pl.BlockSpec(memory_space=pltpu.SMEM)
    seg_i32 = jax.ShapeDtypeStruct((nb * N_GROUPS,), jnp.int32)
    tile_i32 = jax.ShapeDtypeStruct((n_tiles,), jnp.int32)

    xs, cs, start, npiece, off, tgroup, tvalid = pl.pallas_call(
        functools.partial(_dispatch_kernel, nc, nb, n_tiles),
        grid_spec=pltpu.PrefetchScalarGridSpec(
            num_scalar_prefetch=0, grid=(nb,),
            in_specs=[
                pl.BlockSpec((1, MOE_BLK, D_MODEL), cmap), pl.BlockSpec((1, MOE_BLK, D_MODEL), lmap),
                pl.BlockSpec((1, MOE_BLK, LANES), cmap), pl.BlockSpec((1, MOE_BLK, LANES), lmap),
                pl.BlockSpec((1, 8, MOE_BLK), cmap), pl.BlockSpec((1, 8, MOE_BLK), lmap),
                pl.BlockSpec(cntc.shape, lambda b: (0, 0, 0)), pl.BlockSpec(cntl.shape, lambda b: (0, 0, 0)),
            ],
            out_specs=[pl.BlockSpec((ns, D_MODEL), whole, **once), pl.BlockSpec((ns, 2 * LANES), whole, **once),
                       smem, smem, smem, smem, smem],
            scratch_shapes=[pltpu.VMEM((SORT_ROWS, D_MODEL), BF16), pltpu.VMEM((SORT_ROWS, 2 * LANES), BF16)],
        ),
        out_shape=[jax.ShapeDtypeStruct((ns, D_MODEL), BF16), jax.ShapeDtypeStruct((ns, 2 * LANES), BF16),
                   seg_i32, seg_i32, seg_i32, tile_i32, tile_i32],
        compiler_params=arb,
        name="moe_dispatch",
    )(h2c, h2l, cbc, cbl, rtc, rtl, cntc, cntl)

    wmap = lambda i, tg, tv: (tg[i], 0, 0)
    ys = pl.pallas_call(
        _experts_kernel,
        grid_spec=pltpu.PrefetchScalarGridSpec(
            num_scalar_prefetch=2, grid=(n_tiles,),
            in_specs=[
                pl.BlockSpec((MOE_TM, D_MODEL), lambda i, *_: (i, 0)),
                pl.BlockSpec((MOE_TM, 2 * LANES), lambda i, *_: (i, 0)),
                pl.BlockSpec((EXPERTS_PER_GROUP, D_MODEL, D_EXPERT), wmap),
                pl.BlockSpec((EXPERTS_PER_GROUP, D_MODEL, D_EXPERT), wmap),
                pl.BlockSpec((EXPERTS_PER_GROUP, D_EXPERT, D_MODEL), wmap),
            ],
            out_specs=pl.BlockSpec((MOE_TM, D_MODEL), lambda i, *_: (i, 0)),
        ),
        out_shape=jax.ShapeDtypeStruct((ns, D_MODEL), BF16),
        compiler_params=arb,
        name="moe_experts",
    )(tgroup, tvalid, xs, cs, wg, wu, wd)

    yc, yl = pl.pallas_call(
        functools.partial(_combine_kernel, nc, blocks_per_lat_seq),
        grid_spec=pltpu.PrefetchScalarGridSpec(
            num_scalar_prefetch=3, grid=(nb,),
            in_specs=[
                pl.BlockSpec((1, MOE_BLK, D_MODEL), cmap), pl.BlockSpec((1, MOE_BLK, D_MODEL), lmap),
                pl.BlockSpec((1, MOE_BLK, LANES), cmap), pl.BlockSpec((1, MOE_BLK, LANES), lmap),
                pl.BlockSpec((ns, D_MODEL), whole, **once),
                pl.BlockSpec(mod.shape, lambda *_: (0, 0, 0)),
                pl.BlockSpec((1, D_MODEL), whole),
            ],
            out_specs=[pl.BlockSpec((1, MOE_BLK, D_MODEL), cmap), pl.BlockSpec((1, MOE_BLK, D_MODEL), lmap)],
            scratch_shapes=[pltpu.VMEM((SORT_ROWS, D_MODEL), BF16)],
        ),
        out_shape=[jax.ShapeDtypeStruct((nc, MOE_BLK, D_MODEL), F32),
                   jax.ShapeDtypeStruct((nl, MOE_BLK, D_MODEL), F32)],
        compiler_params=arb,
        name="moe_combine",
    )(start, npiece, off, x1c, x1l, cbc, cbl, ys, mod, gf)
    return yc, yl


def _prep_weights(norm1_g, w_in, b_in, b_gates, w_dw, b_dw, conv_ln_g, conv_ln_b, w_conv_out,
                  mlstm_hn_g, w_mlstm_out, w_o, norm2_g, w_rg, b_rg, w_re, b_re):
    s_a = 2 * D_CONV
    s_q = s_a + D_MLSTM
    s_k = s_q + D_MLSTM
    s_v = s_k + D_MLSTM
    s_o = s_v + D_MLSTM
    s_g = s_o + 4 * N_HEADS
    row = lambda v: v.reshape(1, -1).astype(F32)
    w_t = w_in.T
    keep = [(0, s_q), (s_k, s_o), (s_g, w_in.shape[1])]
    halved = [(D_CONV, s_a), (s_v, s_o), (s_g, w_in.shape[1])]
    is_halved = lambda r: any(a <= r < b for a, b in halved)
    blocks = [r for a, b in keep for r in range(a, b, WPREP_ROWS)]
    wrow = _transpose_cast(w_t, blocks, [is_halved(r) for r in blocks])
    bias_scale = jnp.array([0.5 if is_halved(r) else 1.0 for a, b in keep for r in range(a, b)]
                           + [1.0] * D_MLSTM, F32)
    bg = (b_in[s_o:s_g] + b_gates.reshape(-1)).reshape(2, 2, N_HEADS).transpose(1, 0, 2).reshape(-1, 1)
    row_window = lambda start, n: _RowWindow(w_t, start, n)
    n_rt = N_EXPERTS + N_GROUPS
    wrt = jnp.pad(jnp.concatenate([w_re, w_rg], axis=1), ((0, 0), (0, LANES - n_rt)))
    wrt_hi = wrt.astype(BF16)
    wrt2 = jnp.concatenate([wrt_hi, (wrt - wrt_hi.astype(F32)).astype(BF16)], axis=1)
    brtT = jnp.pad(jnp.concatenate([b_re, b_rg]), (0, LANES - n_rt)).reshape(LANES, 1)
    return {
        "g1": row(norm1_g),
        "wrow": wrow, "brow": row(jnp.concatenate([b_in[a:b] for a, b in keep] + [b_in[s_q:s_k]]) * bias_scale),
        "wkT": row_window(s_q, D_MLSTM), "wgifT": row_window(s_o, 4 * N_HEADS), "bgifT": bg,
        "wdw": w_dw.astype(F32), "bdw": row(b_dw), "lng": row(conv_ln_g), "lnb": row(conv_ln_b),
        "wco": w_conv_out.astype(BF16), "hng": row(mlstm_hn_g), "wmo": w_mlstm_out.astype(BF16),
        "wo": w_o.astype(BF16), "g2": row(norm2_g), "wrt2": wrt2, "brtT": brtT,
    }


def kernel(x_prompt, x_sample, state_C, state_n, state_m, c, c_ctx, norm1_g, w_ada, b_ada, w_in, b_in, b_gates, w_dw, b_dw, conv_ln_g, conv_ln_b, w_conv_out, mlstm_hn_g, w_mlstm_out, w_o, norm2_g, w_rg, b_rg, w_re, b_re, w_e_gate, w_e_up, w_e_down, norm_final_g):
    B, S, _ = x_prompt.shape
    Bd, Sd, _ = x_sample.shape
    assert w_ada.shape[0] == 1, "single trunk layer"
    assert MIX_TM % S == 0 and S % SUB == 0 and Sd % MIX_TM == 0

    mod = _ada(c_ctx.reshape(1, -1), c, w_ada[0], b_ada[0].reshape(1, -1))

    wts = _prep_weights(norm1_g[0], w_in[0], b_in[0], b_gates[0], w_dw[0], b_dw[0], conv_ln_g[0],
                        conv_ln_b[0], w_conv_out[0], mlstm_hn_g[0], w_mlstm_out[0], w_o[0],
                        norm2_g[0], w_rg[0], b_rg[0], w_re[0], b_re[0])

    x1p, h2p, cbp, rtp, cntp, c_new, n_new, m_new = _mixer(
        x_prompt.reshape(B * S // MIX_TM, MIX_TM, D_MODEL), S, mod, lambda b: 0, wts, P=S, emit_state=True)

    state = (state_C[:, 0].reshape(Bd, N_UNITS, HEAD_DIM, HEAD_DIM), state_n[:, 0].reshape(Bd, N_UNITS, HEAD_DIM),
             state_m[:, 0].reshape(Bd, N_UNITS))
    x1s, h2s, cbs, rts, cnts = _mixer(x_sample, Sd, mod, lambda b: 1 + b, wts, P=GRID_W, state=state)

    nc, nl = B * S // MOE_BLK, Bd * Sd // MOE_BLK
    blk = lambda a, n: a.reshape(n, MOE_BLK, a.shape[-1])
    yp, ys = _moe(blk(x1p, nc), blk(x1s, nl), blk(h2p, nc), blk(h2s, nl), blk(cbp, nc), blk(cbs, nl),
                  rtp.reshape(nc, 8, MOE_BLK), rts.reshape(nl, 8, MOE_BLK),
                  cntp.reshape(nc, 8, LANES), cnts.reshape(nl, 8, LANES),
                  mod, Sd // MOE_BLK, w_e_gate[0], w_e_up[0], w_e_down[0], norm_final_g.reshape(1, -1))

    return (yp.reshape(B, S, D_MODEL), ys.reshape(Bd, Sd, D_MODEL),
            c_new.reshape(B, 1, 2, N_HEADS, HEAD_DIM, HEAD_DIM),
            n_new.reshape(B, 1, 2, N_HEADS, HEAD_DIM),
            m_new[:, :, 0].reshape(B, 1, 2, N_HEADS))
```

```python
import functools
from typing import NamedTuple

import jax
import jax.numpy as jnp
from jax import lax
from jax.experimental import pallas as pl
from jax.experimental.pallas import tpu as pltpu

D_MODEL = 1024
D_CONV = 512
CONV_K = 31
D_MLSTM = 512
N_HEADS = 4
HEAD_DIM = D_MLSTM // N_HEADS
N_GROUPS = 4
EXPERTS_PER_GROUP = 4
N_EXPERTS = N_GROUPS * EXPERTS_PER_GROUP
D_EXPERT = 256
N_ADA = 6
EPS = 1e-6
GRID_W = 64

LANES = 128
SUB = 256
CONV_PAD = 16
CONV_RB = 64
N_UNITS = 2 * N_HEADS
ROW_ALIGN = 16
CHAIN_SLACK = 1
COPY_RUN = 8
MOE_TM = 512
MIX_TM = 512
MOE_BLK = MIX_TM
SORT_ROWS = MOE_BLK + N_GROUPS * ROW_ALIGN
ADA_PER_STEP = 2
WPREP_ROWS = 512
ROUTE_GROUP_LANE = N_EXPERTS
ROUTE_RANK_LANE = N_EXPERTS + 1
VMEM_LIMIT = 58 * 1024 * 1024

BF16 = jnp.bfloat16
F32 = jnp.float32
NT_DIMS = (((1,), (1,)), ((), ()))


def _dot(a, b):
    return jnp.dot(a, b, preferred_element_type=F32)


def _dot_nt(a, b, precision=None):
    return lax.dot_general(a, b, NT_DIMS, preferred_element_type=F32, precision=precision)


def _sigmoid(x):
    return 0.5 * jnp.tanh(0.5 * x) + 0.5


def _sigmoid_of_half(xh):
    return 0.5 * jnp.tanh(xh) + 0.5


def _log_sigmoid(x):
    return jnp.minimum(x, 0.0) - jnp.log1p(jnp.exp(-jnp.abs(x)))


def _split3(x):
    hi = x.astype(BF16).astype(F32)
    r1 = x - hi
    mid = r1.astype(BF16).astype(F32)
    lo = (r1 - mid).astype(BF16).astype(F32)
    return hi, mid, lo


def _ada_kernel(cctx_ref, c_ref, w_ref, b_ref, o_ref):
    n = 1 + c_ref.shape[0]
    c = jnp.concatenate([cctx_ref[...], c_ref[...], jnp.zeros((8 - n, D_MODEL), F32)], axis=0)
    s = (c * _sigmoid(c)).astype(BF16)
    out = _dot(s, w_ref[...].astype(BF16)) + b_ref[...]
    for v in range(ADA_PER_STEP):
        o_ref[v] = out[:, v * D_MODEL:(v + 1) * D_MODEL]


def _ada(c_ctx, c, w_ada, b_ada):
    return pl.pallas_call(
        _ada_kernel,
        grid=(N_ADA // ADA_PER_STEP,),
        in_specs=[
            pl.BlockSpec(c_ctx.shape, lambda j: (0, 0)),
            pl.BlockSpec(c.shape, lambda j: (0, 0)),
            pl.BlockSpec((D_MODEL, ADA_PER_STEP * D_MODEL), lambda j: (0, j)),
            pl.BlockSpec((1, ADA_PER_STEP * D_MODEL), lambda j: (0, j)),
        ],
        out_specs=pl.BlockSpec((ADA_PER_STEP, 8, D_MODEL), lambda j: (j, 0, 0)),
        out_shape=jax.ShapeDtypeStruct((N_ADA, 8, D_MODEL), F32),
        compiler_params=pltpu.CompilerParams(dimension_semantics=("arbitrary",)),
        name="ada",
    )(c_ctx, c, w_ada, b_ada)


def _transpose_cast_kernel(starts_ref, halve_ref, wt_ref, o_ref):
    scale = jnp.where(halve_ref[pl.program_id(0)] == 1, 0.5, 1.0)
    o_ref[...] = (wt_ref[...] * scale).astype(BF16).T


def _transpose_cast(w_t, row_starts, halve):
    n, k = len(row_starts), w_t.shape[1]
    return pl.pallas_call(
        _transpose_cast_kernel,
        grid_spec=pltpu.PrefetchScalarGridSpec(
            num_scalar_prefetch=2, grid=(n,),
            in_specs=[pl.BlockSpec((pl.Element(WPREP_ROWS), pl.Element(k)), lambda j, starts, hv: (starts[j] * 8, 0))],
            out_specs=pl.BlockSpec((k, WPREP_ROWS), lambda j, starts, hv: (0, j)),
        ),
        out_shape=jax.ShapeDtypeStruct((k, n * WPREP_ROWS), BF16),
        compiler_params=pltpu.CompilerParams(dimension_semantics=("arbitrary",)),
        name="transpose_cast",
    )(jnp.array([r // 8 for r in row_starts], jnp.int32), jnp.array([int(h) for h in halve], jnp.int32), w_t)


WROW_OFFSET = {"wq": 2 * D_CONV, "wv": 2 * D_CONV + D_MLSTM, "wog": 2 * D_CONV + 2 * D_MLSTM,
               "wgm": 2 * D_CONV + 3 * D_MLSTM}
BROW_K_OFFSET = 2 * D_CONV + 3 * D_MLSTM + 2 * D_MODEL

_MIXER_WEIGHTS = (
    "g1", "wrow", "brow", "wkT", "wgifT", "bgifT", "wdw", "bdw", "lng", "lnb",
    "wco", "hng", "wmo", "wo", "g2", "wrt2", "brtT",
)


def _zero_after(x):
    bits = lax.bitcast_convert_type(x, jnp.uint32)
    bits = lax.shift_right_logical(lax.shift_right_logical(bits, jnp.uint32(16)), jnp.uint32(16))
    return lax.bitcast_convert_type(bits, F32)[0:1, :]


def _conv_block(upad_s, seg, base, cs, wdw_ref, bdw_ref, after=None):
    sub = 8
    first = CONV_PAD - CONV_K // 2
    acc = jnp.broadcast_to(bdw_ref[0:1, cs], (CONV_RB, LANES))
    for r in range(sub):
        z = None
        for a in range((CONV_K + first + sub - 1) // sub):
            j = sub * a + r - first
            if 0 <= j < CONV_K:
                lo = base + sub * a
                tap = wdw_ref[j:j + 1, cs] if after is None else wdw_ref[j:j + 1, cs] + after
                term = tap * upad_s[seg, lo:lo + CONV_RB + sub, cs]
                z = term if z is None else z + term
        acc = acc + z[r:r + CONV_RB, :]
    return acc


def _mixer_kernel(R, T, P, has_state, emit_state, mod_index, *refs):
    L = SUB
    n_mt = R // MIX_TM
    cpm = MIX_TM // L
    n_seq = R // T
    cps = T // L
    nseg = MIX_TM // P
    assert not has_state or n_seq == 1
    it = iter(refs)
    x_ref = next(it)
    mod_ref = next(it)
    if has_state:
        c0_ref = next(it)
        n0_ref = next(it)
        m0_ref = next(it)
    w = {name: next(it) for name in _MIXER_WEIGHTS}
    x1_ref = next(it)
    h2_ref = next(it)
    comb_ref = next(it)
    route_ref = next(it)
    cnt_ref = next(it)
    if emit_state:
        cout_ref = next(it)
        nout_ref = next(it)
        mout_ref = next(it)
    (q_s, kT_s, v_s, so_s, scan_s, ma_s, sgb_s, hm_s, cst_s, upad_s) = [next(it) for _ in range(10)]

    cond_row = mod_index(pl.program_id(0))

    def mod_row(i):
        return mod_ref[i, pl.ds(cond_row, 1), :]

    zpad = jnp.zeros((CONV_PAD, D_CONV), F32)
    for seg in range(nseg):
        upad_s[seg, 0:CONV_PAD, :] = zpad
        upad_s[seg, CONV_PAD + P:CONV_PAD + P + CONV_PAD, :] = zpad

    t_idx = lax.broadcasted_iota(jnp.int32, (L, L), 0)
    s_idx = lax.broadcasted_iota(jnp.int32, (L, L), 1)
    lower = s_idx <= t_idx
    upper = s_idx >= t_idx
    triu_b = upper.astype(F32).astype(BF16)
    lane_u = lax.broadcasted_iota(jnp.int32, (N_UNITS, L), 1)
    is_bwd = lax.broadcasted_iota(jnp.int32, (N_UNITS, L), 0) >= N_HEADS

    def gate_scan(g):
        gi, lf = g[:N_UNITS], _log_sigmoid(g[N_UNITS:])
        pr = _dot(jnp.concatenate(_split3(lf), axis=0).astype(BF16), triu_b)
        pre = pr[0:N_UNITS] + pr[N_UNITS:2 * N_UNITS] + pr[2 * N_UNITS:]
        tot = pre[:, L - 1:L]
        bsum = jnp.where(is_bwd, tot - pre + lf, pre)
        a = gi - bsum
        pm, sm, k = a, a, 1
        while k < L:
            pm = jnp.where(lane_u >= k, jnp.maximum(pm, pltpu.roll(pm, k, axis=1)), pm)
            sm = jnp.where(lane_u < L - k, jnp.maximum(sm, pltpu.roll(sm, L - k, axis=1)), sm)
            k *= 2
        wide = lambda v: jnp.broadcast_to(v, (N_UNITS, L))
        return jnp.concatenate([a, jnp.where(is_bwd, sm, pm), bsum, wide(tot),
                                wide(jnp.max(a, axis=1, keepdims=True))], axis=0)

    def phase1(i, carry):
        r0 = pl.multiple_of(i * MIX_TM, MIX_TM)
        rows = pl.ds(r0, MIX_TM)
        x = x_ref[0, rows, :]
        xn = x * lax.rsqrt(jnp.mean(x * x, axis=-1, keepdims=True) + EPS) * w["g1"][...]
        hb = (xn * (1.0 + mod_row(1)) + mod_row(0)).astype(BF16)

        ag = _dot(hb, w["wrow"][:, :2 * D_CONV]) + w["brow"][:, :2 * D_CONV]
        u = ag[:, :D_CONV] * _sigmoid_of_half(ag[:, D_CONV:])
        for seg in range(nseg):
            upad_s[seg, CONV_PAD:CONV_PAD + P, :] = u[seg * P:(seg + 1) * P, :]
        gates = _dot_nt(w["wgifT"][...].astype(BF16), hb)
        gates = jnp.concatenate([gates[d * 2 * N_HEADS + g * N_HEADS:d * 2 * N_HEADS + (g + 1) * N_HEADS]
                                 for g in range(2) for d in range(2)], axis=0) + w["bgifT"][...]
        for j in range(cpm):
            scan_s[i * cpm + j] = gate_scan(gates[:, j * L:(j + 1) * L])

        def proj(name, c0, width=2 * LANES):
            w0 = WROW_OFFSET[name] + c0
            return _dot(hb, w["wrow"][:, w0:w0 + width]) + w["brow"][:, w0:w0 + width]

        last = lambda z: z[-8:, -LANES:]
        bk_row = w["brow"][:, BROW_K_OFFSET:BROW_K_OFFSET + D_MLSTM]
        bk_col = jnp.concatenate([bk_row, jnp.zeros((LANES - 1, D_MLSTM), F32)], axis=0).T[:, 0:1]

        def gm_a(c0):
            z = proj("wgm", c0)
            ma_s[rows, c0:c0 + 2 * LANES] = _sigmoid_of_half(z)
            return last(z)

        def gm_b(c0):
            z = proj("wgm", D_MODEL + c0)
            sgb_s[rows, c0:c0 + 2 * LANES] = _sigmoid_of_half(z)
            return last(z)

        def q_part(c0):
            z = proj("wq", c0)
            q_s[rows, c0:c0 + 2 * LANES] = (z * (HEAD_DIM ** -0.5)).astype(BF16)
            return last(z)

        def v_part(c0):
            z = proj("wv", c0)
            v_s[rows, c0:c0 + 2 * LANES] = z.astype(BF16)
            return last(z)

        def o_part(c0):
            z = proj("wog", c0)
            so_s[rows, c0:c0 + 2 * LANES] = _sigmoid_of_half(z)
            return last(z)

        def k_part(c0):
            rs = slice(c0, c0 + 2 * LANES)
            z = _dot_nt(w["wkT"][rs, :].astype(BF16), hb) + bk_col[rs, :]
            kt = z.astype(BF16)
            for j in range(cpm):
                kT_s[i * cpm + j, rs, :] = kt[:, j * L:(j + 1) * L]
            return last(z)

        jobs = ([functools.partial(gm_a, c0) for c0 in range(0, D_MODEL, 2 * LANES)]
                + [functools.partial(gm_b, c0) for c0 in range(0, D_MODEL, 2 * LANES)]
                + [functools.partial(f, c0) for f in (q_part, k_part, v_part, o_part)
                   for c0 in range(0, D_MLSTM, 2 * LANES)])
        n_jobs = len(jobs)
        conv = {}
        after, lag = None, [None] * CHAIN_SLACK
        n_pieces = (D_CONV // LANES) * nseg * (P // CONV_RB)
        for cb in range(D_CONV // LANES):
            cs = slice(cb * LANES, (cb + 1) * LANES)
            for seg in range(nseg):
                for rb in range(P // CONV_RB):
                    blk = _conv_block(upad_s, seg, rb * CONV_RB, cs, w["wdw"], w["bdw"], after)
                    conv[(cb, seg, rb)] = blk
                    if jobs and len(conv) * n_jobs >= (n_jobs - len(jobs) + 1) * n_pieces:
                        lag.append(_zero_after(jobs.pop(0)()))
                        after = lag.pop(0)
        for job in jobs:
            job()
        cu = jnp.concatenate(
            [jnp.concatenate([conv[(cb, seg, rb)] for seg in range(nseg) for rb in range(P // CONV_RB)], axis=0)
             for cb in range(D_CONV // LANES)], axis=1)
        mu = jnp.mean(cu, axis=-1, keepdims=True)
        cc = cu - mu
        cn = cc * lax.rsqrt(jnp.mean(cc * cc, axis=-1, keepdims=True) + EPS) * w["lng"][...] + w["lnb"][...]
        ca = (cn * _sigmoid(cn)).astype(BF16)
        ma_s[rows, :] = ma_s[rows, :] * _dot(ca, w["wco"][...])
        return carry

    if n_mt == 1:
        phase1(0, 0)
    else:
        lax.fori_loop(0, n_mt, phase1, 0)

    ones_col = (lax.broadcasted_iota(jnp.int32, (L, HEAD_DIM), 1) == 0).astype(F32).astype(BF16)
    pad_rows = jnp.zeros((LANES - 3 * N_UNITS, L), F32)

    def gate_prep(c, m_vec):
        sc = scan_s[c]
        a, run_max, bsum = sc[0:N_UNITS], sc[N_UNITS:2 * N_UNITS], sc[2 * N_UNITS:3 * N_UNITS]
        tot, a_max = sc[3 * N_UNITS:4 * N_UNITS, 0:1], sc[4 * N_UNITS:5 * N_UNITS, 0:1]
        big_m = jnp.maximum(m_vec, run_max)
        m_end = jnp.maximum(m_vec, a_max)
        cols = jnp.concatenate(
            [big_m, jnp.exp(m_vec - big_m), jnp.exp(-bsum - big_m), pad_rows], axis=0).T
        return a, cols, jnp.exp(a - m_end), jnp.exp(m_vec - m_end), tot + m_end

    def unit_group(dirs, c, prep, first_chunk, want_state):
        a, cols, wk, decay, _ = prep
        rows = slice(c * L, (c + 1) * L)
        heads = range(N_HEADS)
        units = [(d, hd) for d in dirs for hd in heads]
        hs = [slice(hd * HEAD_DIM, (hd + 1) * HEAD_DIM) for hd in heads]
        idx = {u: u[0] * N_HEADS + u[1] for u in units}
        col = lambda k, u: cols[:, k * N_UNITS + idx[u]:k * N_UNITS + idx[u] + 1]
        row = lambda arr, u: arr[idx[u]:idx[u] + 1, :]
        chained = has_state or not first_chunk
        qc = [q_s[rows, hs[hd]] for hd in heads]
        kTc = [kT_s[c, hs[hd], :] for hd in heads]
        vaug = [jnp.concatenate([v_s[rows, hs[hd]], ones_col], axis=1) for hd in heads]
        qk = [_dot(qc[hd], kTc[hd]) for hd in heads]
        s_mat = {u: (qk[u[1]] * jnp.where(lower if u[0] == 0 else upper, jnp.exp(row(a, u) - col(0, u)), 0.0)
                     ).astype(BF16) for u in units}
        nd = {u: _dot(s_mat[u], vaug[u[1]]) for u in units}
        if chained:
            nd = {u: nd[u] + col(1, u) * _dot(qc[u[1]], cst_s[idx[u]].astype(BF16)) for u in units}
        h = {u: nd[u][:, :HEAD_DIM] * (1.0 / jnp.maximum(jnp.abs(nd[u][:, HEAD_DIM:HEAD_DIM + 1]), col(2, u)))
             for u in units}
        for hd in heads:
            total = h[(dirs[0], hd)]
            for d in dirs[1:]:
                total = total + h[(d, hd)]
            if dirs[0] == 0:
                hm_s[rows, hs[hd]] = total
            else:
                hm_s[rows, hs[hd]] = hm_s[rows, hs[hd]] + total
        if want_state:
            kw = {u: (kTc[u[1]].astype(F32) * row(wk, u)).astype(BF16) for u in units}
            upd = {u: _dot(kw[u], vaug[u[1]]) for u in units}
            for u in units:
                cst_s[idx[u]] = (upd[u] + row(decay, u) * cst_s[idx[u]]) if chained else upd[u]

    dir_rows = lax.broadcasted_iota(jnp.int32, (N_UNITS, 1), 0) >= N_HEADS
    for seq in range(n_seq):
        if has_state:
            n_cols = jnp.concatenate([n0_ref[0], jnp.zeros((LANES - N_UNITS, HEAD_DIM), F32)], axis=0).T
            first_lane = lax.broadcasted_iota(jnp.int32, (HEAD_DIM, HEAD_DIM), 1) == 0
            for idx in range(N_UNITS):
                cst_s[idx, :, :HEAD_DIM] = c0_ref[0, idx]
                cst_s[idx, :, HEAD_DIM:] = jnp.where(first_lane, n_cols[:, idx:idx + 1], 0.0)
            unit_row = lax.broadcasted_iota(jnp.int32, (N_UNITS, 1), 0)
            m_vec = jnp.zeros((N_UNITS, 1), F32)
            for idx in range(N_UNITS):
                m_vec = jnp.where(unit_row == idx, m0_ref[pl.program_id(0), idx], m_vec)
        else:
            m_vec = jnp.zeros((N_UNITS, 1), F32)
        if cps == 1:
            prep = gate_prep(seq, m_vec)
            unit_group([0, 1], seq, prep, True, emit_state)
            m_vec = prep[4]
        else:
            for d in range(2):
                order = list(range(cps)) if d == 0 else list(range(cps - 1, -1, -1))
                for pos, c in enumerate(order):
                    prep = gate_prep(seq * cps + c, m_vec)
                    unit_group([d], seq * cps + c, prep, pos == 0, emit_state or pos < cps - 1)
                    m_vec = jnp.where(dir_rows == (d == 1), prep[4], m_vec)
        if emit_state:
            for idx in range(N_UNITS):
                caug = cst_s[idx]
                cout_ref[0, seq * N_UNITS + idx] = caug[:, :HEAD_DIM]
                nout_ref[0, seq * N_UNITS + idx:seq * N_UNITS + idx + 1, :] = caug[:, HEAD_DIM:].T[0:1, :]
            mout_ref[0, seq * N_UNITS:(seq + 1) * N_UNITS, :] = jnp.broadcast_to(m_vec, (N_UNITS, LANES))

    e_iota = lax.broadcasted_iota(jnp.int32, (LANES, MIX_TM), 0)
    g_of_e = lax.shift_right_logical(e_iota, 2)
    j_of_e = lax.bitwise_and(e_iota, EXPERTS_PER_GROUP - 1)
    r8 = lax.broadcasted_iota(jnp.int32, (8, MIX_TM), 0)
    before_b = (lax.broadcasted_iota(jnp.int32, (MOE_BLK, MOE_BLK), 0)
                < lax.broadcasted_iota(jnp.int32, (MOE_BLK, MOE_BLK), 1)).astype(F32).astype(BF16)

    def phase3(i, carry):
        r0 = pl.multiple_of(i * MIX_TM, MIX_TM)
        rows = pl.ds(r0, MIX_TM)
        hm = hm_s[rows, :]
        heads = []
        for hd in range(N_HEADS):
            hh = hm[:, hd * HEAD_DIM:(hd + 1) * HEAD_DIM]
            heads.append(hh * lax.rsqrt(jnp.mean(hh * hh, axis=-1, keepdims=True) + EPS))
        hn = jnp.concatenate(heads, axis=1) * w["hng"][...]
        hb2 = (so_s[rows, :] * hn).astype(BF16)
        br_b = _dot(hb2, w["wmo"][...])
        mixed = (ma_s[rows, :] + sgb_s[rows, :] * br_b).astype(BF16)
        x1 = x_ref[0, rows, :] + mod_row(2) * _dot(mixed, w["wo"][...])
        x1_ref[0, rows, :] = x1
        xn = x1 * lax.rsqrt(jnp.mean(x1 * x1, axis=-1, keepdims=True) + EPS) * w["g2"][...]
        h2 = xn * (1.0 + mod_row(4)) + mod_row(3)
        h2_ref[0, rows, :] = h2.astype(BF16)

        h2_hi = h2.astype(BF16)
        h2_lo = (h2 - h2_hi.astype(F32)).astype(BF16)
        lg = _dot(h2_hi, w["wrt2"][...])
        lg = lg[:, :LANES] + lg[:, LANES:] + _dot(h2_lo, w["wrt2"][:, :LANES])
        lt = lg.T + w["brtT"][...]
        gl = [lt[N_EXPERTS + g:N_EXPERTS + g + 1, :] for g in range(N_GROUPS)]
        best, gsel = gl[0], jnp.zeros((1, MIX_TM), jnp.int32)
        for g in range(1, N_GROUPS):
            better = gl[g] > best
            gsel = jnp.where(better, g, gsel)
            best = jnp.where(better, gl[g], best)
        gp_sel = 1.0 / sum(jnp.exp(v - best) for v in gl)
        el = []
        for j in range(EXPERTS_PER_GROUP):
            v = lt[j:j + 1, :]
            for g in range(1, N_GROUPS):
                r = g * EXPERTS_PER_GROUP + j
                v = jnp.where(gsel == g, lt[r:r + 1, :], v)
            el.append(v)
        l1, e1 = el[0], jnp.zeros((1, MIX_TM), jnp.int32)
        for j in range(1, EXPERTS_PER_GROUP):
            better = el[j] > l1
            e1 = jnp.where(better, j, e1)
            l1 = jnp.where(better, el[j], l1)
        l2 = jnp.full((1, MIX_TM), -jnp.inf, F32)
        e2 = jnp.zeros((1, MIX_TM), jnp.int32)
        for j in range(EXPERTS_PER_GROUP):
            better = jnp.logical_and(e1 != j, el[j] > l2)
            e2 = jnp.where(better, j, e2)
            l2 = jnp.where(better, el[j], l2)
        r2 = jnp.exp(l2 - l1)
        wt1 = gp_sel / (1.0 + r2)
        wt2 = gp_sel * r2 / (1.0 + r2)
        in_group = g_of_e == gsel
        comb_t = (jnp.where(jnp.logical_and(in_group, j_of_e == e1), wt1, 0.0)
                  + jnp.where(jnp.logical_and(in_group, j_of_e == e2), wt2, 0.0))

        onehot = (r8 == gsel).astype(F32)
        gsel_f = gsel.astype(F32)
        rank = jnp.sum(onehot * _dot(onehot.astype(BF16), before_b), axis=0, keepdims=True)
        r8rows = pl.ds(pl.multiple_of(i * 8, 8), 8)
        route_ref[0, r8rows, :] = jnp.where(r8 == 0, gsel_f, jnp.where(r8 == 1, rank, 0.0))
        cnt_ref[0, r8rows, :] = jnp.broadcast_to(jnp.sum(onehot, axis=1, keepdims=True), (8, LANES))
        comb_t = jnp.where(e_iota == ROUTE_GROUP_LANE, gsel_f,
                           jnp.where(e_iota == ROUTE_RANK_LANE, rank, comb_t))
        comb_ref[0, rows, :] = comb_t.T
        return carry

    if n_mt == 1:
        phase3(0, 0)
    else:
        lax.fori_loop(0, n_mt, phase3, 0)


class _RowWindow(NamedTuple):
    array: jax.Array
    start: int
    n: int


def _const_spec(a):
    if isinstance(a, _RowWindow):
        assert a.start % a.n == 0
        return a.array, pl.BlockSpec((a.n, a.array.shape[1]), lambda b: (a.start // a.n, 0),
                                     pipeline_mode=pl.Buffered(1))
    nd = a.ndim
    return a, pl.BlockSpec(a.shape, lambda b, _nd=nd: (0,) * _nd, pipeline_mode=pl.Buffered(1))


def _mixer(x, T, mod, mod_index, weights, P, state=None, emit_state=False):
    B, R, _ = x.shape
    n_chunks = R // SUB
    n_blk = R // MOE_BLK
    n_seq = R // T
    has_state = state is not None
    seq_mode = {} if R <= MIX_TM else {"pipeline_mode": pl.Buffered(1)}
    in_specs = [
        pl.BlockSpec((1, R, D_MODEL), lambda b: (b, 0, 0), **seq_mode),
        pl.BlockSpec(mod.shape, lambda b: (0, 0, 0)),
    ]
    args = [x, mod]
    if has_state:
        c0, n0, m0 = state
        in_specs += [
            pl.BlockSpec((1, N_UNITS, HEAD_DIM, HEAD_DIM), lambda b: (b, 0, 0, 0)),
            pl.BlockSpec((1, N_UNITS, HEAD_DIM), lambda b: (b, 0, 0)),
            pl.BlockSpec(memory_space=pltpu.SMEM),
        ]
        args += [c0, n0, m0]
    for name in _MIXER_WEIGHTS:
        operand, spec = _const_spec(weights[name])
        in_specs.append(spec)
        args.append(operand)
    out_shape = [
        jax.ShapeDtypeStruct((B, R, D_MODEL), F32),
        jax.ShapeDtypeStruct((B, R, D_MODEL), BF16),
        jax.ShapeDtypeStruct((B, R, LANES), F32),
        jax.ShapeDtypeStruct((B, n_blk * 8, MOE_BLK), F32),
        jax.ShapeDtypeStruct((B, n_blk * 8, LANES), F32),
    ]
    out_specs = [
        pl.BlockSpec((1, R, D_MODEL), lambda b: (b, 0, 0), **seq_mode),
        pl.BlockSpec((1, R, D_MODEL), lambda b: (b, 0, 0), **seq_mode),
        pl.BlockSpec((1, R, LANES), lambda b: (b, 0, 0)),
        pl.BlockSpec((1, n_blk * 8, MOE_BLK), lambda b: (b, 0, 0)),
        pl.BlockSpec((1, n_blk * 8, LANES), lambda b: (b, 0, 0)),
    ]
    if emit_state:
        out_shape += [
            jax.ShapeDtypeStruct((B, n_seq * N_UNITS, HEAD_DIM, HEAD_DIM), F32),
            jax.ShapeDtypeStruct((B, n_seq * N_UNITS, HEAD_DIM), F32),
            jax.ShapeDtypeStruct((B, n_seq * N_UNITS, LANES), F32),
        ]
        out_specs += [
            pl.BlockSpec((1, n_seq * N_UNITS, HEAD_DIM, HEAD_DIM), lambda b: (b, 0, 0, 0)),
            pl.BlockSpec((1, n_seq * N_UNITS, HEAD_DIM), lambda b: (b, 0, 0)),
            pl.BlockSpec((1, n_seq * N_UNITS, LANES), lambda b: (b, 0, 0)),
        ]
    scratch = [
        pltpu.VMEM((R, D_MLSTM), BF16),
        pltpu.VMEM((n_chunks, D_MLSTM, SUB), BF16),
        pltpu.VMEM((R, D_MLSTM), BF16),
        pltpu.VMEM((R, D_MLSTM), F32),
        pltpu.VMEM((n_chunks, 5 * N_UNITS, SUB), F32),
        pltpu.VMEM((R, D_MODEL), F32),
        pltpu.VMEM((R, D_MODEL), F32),
        pltpu.VMEM((R, D_MLSTM), F32),
        pltpu.VMEM((N_UNITS, HEAD_DIM, 2 * HEAD_DIM), F32),
        pltpu.VMEM((MIX_TM // P, P + 2 * CONV_PAD, D_CONV), F32),
    ]
    return pl.pallas_call(
        functools.partial(_mixer_kernel, R, T, P, has_state, emit_state, mod_index),
        grid=(B,),
        in_specs=in_specs,
        out_specs=out_specs,
        out_shape=out_shape,
        scratch_shapes=scratch,
        compiler_params=pltpu.CompilerParams(
            dimension_semantics=("arbitrary",), vmem_limit_bytes=VMEM_LIMIT),
        name="mixer_T%d" % T,
    )(*args)


def _dest_in_block(group, rank, starts):
    dest = rank
    for g in range(N_GROUPS):
        dest = dest + jnp.where(group == float(g), starts[g], 0.0)
    return dest


def _copy_segments(src_refs, dst_refs, src_starts, dst_starts, n_pieces):
    def copy(g, first_piece, n_rows):
        s = pl.multiple_of(src_starts[g] + first_piece * ROW_ALIGN, ROW_ALIGN)
        d = pl.multiple_of(dst_starts[g] + first_piece * ROW_ALIGN, ROW_ALIGN)
        for src, dst in zip(src_refs, dst_refs):
            dst[pl.ds(d, n_rows), :] = src[pl.ds(s, n_rows), :]

    for g in range(N_GROUPS):
        n_runs = lax.shift_right_logical(n_pieces[g], COPY_RUN.bit_length() - 1)

        def run(k, carry, g=g):
            copy(g, k * COPY_RUN, COPY_RUN * ROW_ALIGN)
            return carry

        def single(k, carry, g=g):
            copy(g, k, ROW_ALIGN)
            return carry

        lax.fori_loop(0, n_runs, run, 0)
        lax.fori_loop(n_runs * COPY_RUN, n_pieces[g], single, 0)


def _plan_segments(n_blocks, n_tiles, count, start_ref, npiece_ref, off_ref, tgroup_ref, tvalid_ref):
    align_shift = ROW_ALIGN.bit_length() - 1
    tile_shift = MOE_TM.bit_length() - 1

    def block_starts(blk, carry):
        row = jnp.int32(0)
        for g in range(N_GROUPS):
            n = lax.shift_right_logical(count(blk, g) + (ROW_ALIGN - 1), align_shift)
            npiece_ref[blk * N_GROUPS + g] = n
            start_ref[blk * N_GROUPS + g] = row
            row = row + n * ROW_ALIGN
        return carry

    lax.fori_loop(0, n_blocks, block_starts, 0)

    base_row = jnp.int32(0)
    base_tile = jnp.int32(0)
    last_group = jnp.int32(0)
    for g in range(N_GROUPS):
        def seg_offsets(blk, row, g=g, base_row=base_row):
            off_ref[blk * N_GROUPS + g] = base_row + row
            return row + npiece_ref[blk * N_GROUPS + g] * ROW_ALIGN

        rows = lax.fori_loop(0, n_blocks, seg_offsets, jnp.int32(0))
        tiles = lax.shift_right_logical(rows + (MOE_TM - 1), tile_shift)

        def mark_tiles(t, carry, g=g, base_tile=base_tile):
            tgroup_ref[base_tile + t] = g
            tvalid_ref[base_tile + t] = 1
            return carry

        lax.fori_loop(0, tiles, mark_tiles, 0)
        last_group = jnp.where(tiles > 0, g, last_group)
        base_row = base_row + tiles * MOE_TM
        base_tile = base_tile + tiles

    def mark_unused(t, carry):
        tgroup_ref[t] = last_group
        tvalid_ref[t] = 0
        return carry

    lax.fori_loop(base_tile, n_tiles, mark_unused, 0)


def _dispatch_kernel(n_ctx_blocks, n_blocks, n_tiles,
                     h2c_ref, h2l_ref, cbc_ref, cbl_ref, rtc_ref, rtl_ref, cntc_ref, cntl_ref,
                     xs_ref, cs_ref, start_ref, npiece_ref, off_ref, tgroup_ref, tvalid_ref,
                     sx_s, sc_s):
    b = pl.program_id(0)
    is_ctx = b < n_ctx_blocks

    def count(blk, g):
        vc = cntc_ref[jnp.minimum(blk, n_ctx_blocks - 1), pl.ds(g, 1), pl.ds(0, 1)]
        vl = cntl_ref[jnp.maximum(blk - n_ctx_blocks, 0), pl.ds(g, 1), pl.ds(0, 1)]
        return jnp.where(blk < n_ctx_blocks, vc, vl)[0, 0].astype(jnp.int32)

    @pl.when(b == 0)
    def _():
        _plan_segments(n_blocks, n_tiles, count, start_ref, npiece_ref, off_ref, tgroup_ref, tvalid_ref)
        xs_ref[...] = jnp.zeros_like(xs_ref)
        cs_ref[...] = jnp.zeros_like(cs_ref)

    starts = [start_ref[b * N_GROUPS + g] for g in range(N_GROUPS)]

    def sort_block(h2_ref, cb_ref, rt_ref):
        h2 = h2_ref[0]
        cb = cb_ref[0]
        rt = rt_ref[0]
        dest = _dest_in_block(rt[0:1, :], rt[1:2, :], [s.astype(F32) for s in starts])
        row = lax.broadcasted_iota(jnp.int32, (SORT_ROWS, MOE_BLK), 0).astype(F32)
        perm = (row == dest).astype(F32).astype(BF16)
        cb_hi = cb.astype(BF16)
        cb_lo = (cb - cb_hi.astype(F32)).astype(BF16)
        srt = _dot(perm, jnp.concatenate([h2, cb_hi, cb_lo], axis=1)).astype(BF16)
        sx_s[...] = srt[:, :D_MODEL]
        sc_s[...] = srt[:, D_MODEL:]

    pl.when(is_ctx)(functools.partial(sort_block, h2c_ref, cbc_ref, rtc_ref))
    pl.when(jnp.logical_not(is_ctx))(functools.partial(sort_block, h2l_ref, cbl_ref, rtl_ref))
    _copy_segments((sx_s, sc_s), (xs_ref, cs_ref), starts,
                   [off_ref[b * N_GROUPS + g] for g in range(N_GROUPS)],
                   [npiece_ref[b * N_GROUPS + g] for g in range(N_GROUPS)])


def _experts_kernel(tgroup_ref, tvalid_ref, xs_ref, cs_ref, wg_ref, wu_ref, wd_ref, ys_ref):
    i = pl.program_id(0)

    @pl.when(tvalid_ref[i] == 1)
    def _():
        x = xs_ref[...]
        comb = cs_ref[:, :LANES].astype(F32) + cs_ref[:, LANES:].astype(F32)
        lane = lax.broadcasted_iota(jnp.int32, comb.shape, 1)
        first = tgroup_ref[i] * EXPERTS_PER_GROUP
        acc = None
        for j in range(EXPERTS_PER_GROUP):
            gj = _dot(x, wg_ref[j].astype(BF16))
            uj = _dot(x, wu_ref[j].astype(BF16))
            cw = jnp.sum(jnp.where(lane == first + j, comb, 0.0), axis=1, keepdims=True)
            out = _dot((gj * _sigmoid(gj) * uj * cw).astype(BF16), wd_ref[j].astype(BF16))
            acc = out if acc is None else acc + out
        ys_ref[...] = acc.astype(BF16)

    @pl.when(tvalid_ref[i] == 0)
    def _():
        ys_ref[...] = jnp.zeros_like(ys_ref)


def _combine_kernel(n_ctx_blocks, blocks_per_lat_seq, start_ref, npiece_ref, off_ref,
                    x1c_ref, x1l_ref, cbc_ref, cbl_ref, ys_ref, mod_ref, gf_ref, yc_ref, yl_ref, loc_s):
    b = pl.program_id(0)
    is_ctx = b < n_ctx_blocks
    starts = [start_ref[b * N_GROUPS + g] for g in range(N_GROUPS)]
    @pl.when(b == 0)
    def _():
        loc_s[...] = jnp.zeros_like(loc_s)

    _copy_segments((ys_ref,), (loc_s,), [off_ref[b * N_GROUPS + g] for g in range(N_GROUPS)], starts,
                   [npiece_ref[b * N_GROUPS + g] for g in range(N_GROUPS)])
    def finish_block(x1_ref, cb_ref, y_ref, mrow):
        cb = cb_ref[0]
        dest = _dest_in_block(cb[:, ROUTE_GROUP_LANE:ROUTE_GROUP_LANE + 1],
                              cb[:, ROUTE_RANK_LANE:ROUTE_RANK_LANE + 1],
                              [s.astype(F32) for s in starts])
        col = lax.broadcasted_iota(jnp.int32, (MOE_BLK, SORT_ROWS), 1).astype(F32)
        unperm = (col == dest).astype(F32).astype(BF16)
        x2 = x1_ref[0] + mod_ref[N_ADA - 1, pl.ds(mrow, 1), :] * _dot(unperm, loc_s[...])
        y_ref[0] = x2 * lax.rsqrt(jnp.mean(x2 * x2, axis=-1, keepdims=True) + EPS) * gf_ref[...]

    lat_row = 1 + jnp.maximum(b - n_ctx_blocks, 0) // blocks_per_lat_seq
    pl.when(is_ctx)(functools.partial(finish_block, x1c_ref, cbc_ref, yc_ref, 0))
    pl.when(jnp.logical_not(is_ctx))(functools.partial(finish_block, x1l_ref, cbl_ref, yl_ref, lat_row))


def _moe(x1c, x1l, h2c, h2l, cbc, cbl, rtc, rtl, cntc, cntl, mod, blocks_per_lat_seq, wg, wu, wd, gf):
    nc, nl = x1c.shape[0], x1l.shape[0]
    nb = nc + nl
    n_rows_max = nb * MOE_BLK + nb * N_GROUPS * (ROW_ALIGN - 1) + N_GROUPS * (MOE_TM - ROW_ALIGN)
    n_tiles = -(-n_rows_max // MOE_TM)
    ns = n_tiles * MOE_TM

    cmap = lambda b, *_: (jnp.minimum(b, nc - 1), 0, 0)
    lmap = lambda b, *_: (jnp.maximum(b - nc, 0), 0, 0)
    whole = lambda *_: (0, 0)
    once = {"pipeline_mode": pl.Buffered(1)}
    arb = pltpu.CompilerParams(dimension_semantics=("arbitrary",), vmem_limit_bytes=VMEM_LIMIT)
    smem = pl.BlockSpec(memory_space=pltpu.SMEM)
    seg_i32 = jax.ShapeDtypeStruct((nb * N_GROUPS,), jnp.int32)
    tile_i32 = jax.ShapeDtypeStruct((n_tiles,), jnp.int32)

    xs, cs, start, npiece, off, tgroup, tvalid = pl.pallas_call(
        functools.partial(_dispatch_kernel, nc, nb, n_tiles),
        grid_spec=pltpu.PrefetchScalarGridSpec(
            num_scalar_prefetch=0, grid=(nb,),
            in_specs=[
                pl.BlockSpec((1, MOE_BLK, D_MODEL), cmap), pl.BlockSpec((1, MOE_BLK, D_MODEL), lmap),
                pl.BlockSpec((1, MOE_BLK, LANES), cmap), pl.BlockSpec((1, MOE_BLK, LANES), lmap),
                pl.BlockSpec((1, 8, MOE_BLK), cmap), pl.BlockSpec((1, 8, MOE_BLK), lmap),
                pl.BlockSpec(cntc.shape, lambda b: (0, 0, 0)), pl.BlockSpec(cntl.shape, lambda b: (0, 0, 0)),
            ],
            out_specs=[pl.BlockSpec((ns, D_MODEL), whole, **once), pl.BlockSpec((ns, 2 * LANES), whole, **once),
                       smem, smem, smem, smem, smem],
            scratch_shapes=[pltpu.VMEM((SORT_ROWS, D_MODEL), BF16), pltpu.VMEM((SORT_ROWS, 2 * LANES), BF16)],
        ),
        out_shape=[jax.ShapeDtypeStruct((ns, D_MODEL), BF16), jax.ShapeDtypeStruct((ns, 2 * LANES), BF16),
                   seg_i32, seg_i32, seg_i32, tile_i32, tile_i32],
        compiler_params=arb,
        name="moe_dispatch",
    )(h2c, h2l, cbc, cbl, rtc, rtl, cntc, cntl)

    wmap = lambda i, tg, tv: (tg[i], 0, 0)
    ys = pl.pallas_call(
        _experts_kernel,
        grid_spec=pltpu.PrefetchScalarGridSpec(
            num_scalar_prefetch=2, grid=(n_tiles,),
            in_specs=[
                pl.BlockSpec((MOE_TM, D_MODEL), lambda i, *_: (i, 0)),
                pl.BlockSpec((MOE_TM, 2 * LANES), lambda i, *_: (i, 0)),
                pl.BlockSpec((EXPERTS_PER_GROUP, D_MODEL, D_EXPERT), wmap),
                pl.BlockSpec((EXPERTS_PER_GROUP, D_MODEL, D_EXPERT), wmap),
                pl.BlockSpec((EXPERTS_PER_GROUP, D_EXPERT, D_MODEL), wmap),
            ],
            out_specs=pl.BlockSpec((MOE_TM, D_MODEL), lambda i, *_: (i, 0)),
        ),
        out_shape=jax.ShapeDtypeStruct((ns, D_MODEL), BF16),
        compiler_params=arb,
        name="moe_experts",
    )(tgroup, tvalid, xs, cs, wg, wu, wd)

    yc, yl = pl.pallas_call(
        functools.partial(_combine_kernel, nc, blocks_per_lat_seq),
        grid_spec=pltpu.PrefetchScalarGridSpec(
            num_scalar_prefetch=3, grid=(nb,),
            in_specs=[
                pl.BlockSpec((1, MOE_BLK, D_MODEL), cmap), pl.BlockSpec((1, MOE_BLK, D_MODEL), lmap),
                pl.BlockSpec((1, MOE_BLK, LANES), cmap), pl.BlockSpec((1, MOE_BLK, LANES), lmap),
                pl.BlockSpec((ns, D_MODEL), whole, **once),
                pl.BlockSpec(mod.shape, lambda *_: (0, 0, 0)),
                pl.BlockSpec((1, D_MODEL), whole),
            ],
            out_specs=[pl.BlockSpec((1, MOE_BLK, D_MODEL), cmap), pl.BlockSpec((1, MOE_BLK, D_MODEL), lmap)],
            scratch_shapes=[pltpu.VMEM((SORT_ROWS, D_MODEL), BF16)],
        ),
        out_shape=[jax.ShapeDtypeStruct((nc, MOE_BLK, D_MODEL), F32),
                   jax.ShapeDtypeStruct((nl, MOE_BLK, D_MODEL), F32)],
        compiler_params=arb,
        name="moe_combine",
    )(start, npiece, off, x1c, x1l, cbc, cbl, ys, mod, gf)
    return yc, yl


def _prep_weights(norm1_g, w_in, b_in, b_gates, w_dw, b_dw, conv_ln_g, conv_ln_b, w_conv_out,
                  mlstm_hn_g, w_mlstm_out, w_o, norm2_g, w_rg, b_rg, w_re, b_re):
    s_a = 2 * D_CONV
    s_q = s_a + D_MLSTM
    s_k = s_q + D_MLSTM
    s_v = s_k + D_MLSTM
    s_o = s_v + D_MLSTM
    s_g = s_o + 4 * N_HEADS
    row = lambda v: v.reshape(1, -1).astype(F32)
    w_t = w_in.T
    keep = [(0, s_q), (s_k, s_o), (s_g, w_in.shape[1])]
    halved = [(D_CONV, s_a), (s_v, s_o), (s_g, w_in.shape[1])]
    is_halved = lambda r: any(a <= r < b for a, b in halved)
    blocks = [r for a, b in keep for r in range(a, b, WPREP_ROWS)]
    wrow = _transpose_cast(w_t, blocks, [is_halved(r) for r in blocks])
    bias_scale = jnp.array([0.5 if is_halved(r) else 1.0 for a, b in keep for r in range(a, b)]
                           + [1.0] * D_MLSTM, F32)
    bg = (b_in[s_o:s_g] + b_gates.reshape(-1)).reshape(2, 2, N_HEADS).transpose(1, 0, 2).reshape(-1, 1)
    row_window = lambda start, n: _RowWindow(w_t, start, n)
    n_rt = N_EXPERTS + N_GROUPS
    wrt = jnp.pad(jnp.concatenate([w_re, w_rg], axis=1), ((0, 0), (0, LANES - n_rt)))
    wrt_hi = wrt.astype(BF16)
    wrt2 = jnp.concatenate([wrt_hi, (wrt - wrt_hi.astype(F32)).astype(BF16)], axis=1)
    brtT = jnp.pad(jnp.concatenate([b_re, b_rg]), (0, LANES - n_rt)).reshape(LANES, 1)
    return {
        "g1": row(norm1_g),
        "wrow": wrow, "brow": row(jnp.concatenate([b_in[a:b] for a, b in keep] + [b_in[s_q:s_k]]) * bias_scale),
        "wkT": row_window(s_q, D_MLSTM), "wgifT": row_window(s_o, 4 * N_HEADS), "bgifT": bg,
        "wdw": w_dw.astype(F32), "bdw": row(b_dw), "lng": row(conv_ln_g), "lnb": row(conv_ln_b),
        "wco": w_conv_out.astype(BF16), "hng": row(mlstm_hn_g), "wmo": w_mlstm_out.astype(BF16),
        "wo": w_o.astype(BF16), "g2": row(norm2_g), "wrt2": wrt2, "brtT": brtT,
    }


def kernel(x_prompt, x_sample, state_C, state_n, state_m, c, c_ctx, norm1_g, w_ada, b_ada, w_in, b_in, b_gates, w_dw, b_dw, conv_ln_g, conv_ln_b, w_conv_out, mlstm_hn_g, w_mlstm_out, w_o, norm2_g, w_rg, b_rg, w_re, b_re, w_e_gate, w_e_up, w_e_down, norm_final_g):
    B, S, _ = x_prompt.shape
    Bd, Sd, _ = x_sample.shape
    assert w_ada.shape[0] == 1, "single trunk layer"
    assert MIX_TM % S == 0 and S % SUB == 0 and Sd % MIX_TM == 0

    mod = _ada(c_ctx.reshape(1, -1), c, w_ada[0], b_ada[0].reshape(1, -1))

    wts = _prep_weights(norm1_g[0], w_in[0], b_in[0], b_gates[0], w_dw[0], b_dw[0], conv_ln_g[0],
                        conv_ln_b[0], w_conv_out[0], mlstm_hn_g[0], w_mlstm_out[0], w_o[0],
                        norm2_g[0], w_rg[0], b_rg[0], w_re[0], b_re[0])

    x1p, h2p, cbp, rtp, cntp, c_new, n_new, m_new = _mixer(
        x_prompt.reshape(B * S // MIX_TM, MIX_TM, D_MODEL), S, mod, lambda b: 0, wts, P=S, emit_state=True)

    state = (state_C[:, 0].reshape(Bd, N_UNITS, HEAD_DIM, HEAD_DIM), state_n[:, 0].reshape(Bd, N_UNITS, HEAD_DIM),
             state_m[:, 0].reshape(Bd, N_UNITS))
    x1s, h2s, cbs, rts, cnts = _mixer(x_sample, Sd, mod, lambda b: 1 + b, wts, P=GRID_W, state=state)

    nc, nl = B * S // MOE_BLK, Bd * Sd // MOE_BLK
    blk = lambda a, n: a.reshape(n, MOE_BLK, a.shape[-1])
    yp, ys = _moe(blk(x1p, nc), blk(x1s, nl), blk(h2p, nc), blk(h2s, nl), blk(cbp, nc), blk(cbs, nl),
                  rtp.reshape(nc, 8, MOE_BLK), rts.reshape(nl, 8, MOE_BLK),
                  cntp.reshape(nc, 8, LANES), cnts.reshape(nl, 8, LANES),
                  mod, Sd // MOE_BLK, w_e_gate[0], w_e_up[0], w_e_down[0], norm_final_g.reshape(1, -1))

    return (yp.reshape(B, S, D_MODEL), ys.reshape(Bd, Sd, D_MODEL),
            c_new.reshape(B, 1, 2, N_HEADS, HEAD_DIM, HEAD_DIM),
            n_new.reshape(B, 1, 2, N_HEADS, HEAD_DIM),
            m_new[:, :, 0].reshape(B, 1, 2, N_HEADS))
```

```python
import functools
from typing import NamedTuple

import jax
import jax.numpy as jnp
from jax import lax
from jax.experimental import pallas as pl
from jax.experimental.pallas import tpu as pltpu

D_MODEL = 1024
D_CONV = 512
CONV_K = 31
D_MLSTM = 512
N_HEADS = 4
HEAD_DIM = D_MLSTM // N_HEADS
N_GROUPS = 4
EXPERTS_PER_GROUP = 4
N_EXPERTS = N_GROUPS * EXPERTS_PER_GROUP
D_EXPERT = 256
N_ADA = 6
EPS = 1e-6
GRID_W = 64

LANES = 128
SUB = 256
CONV_PAD = 16
CONV_RB = 64
N_UNITS = 2 * N_HEADS
ROW_ALIGN = 16
CHAIN_SLACK = 1
COPY_RUN = 4
MOE_TM = 512
MIX_TM = 512
MOE_BLK = MIX_TM
SORT_ROWS = MOE_BLK + N_GROUPS * ROW_ALIGN
ADA_PER_STEP = 2
WPREP_ROWS = 512
ROUTE_GROUP_LANE = N_EXPERTS
ROUTE_RANK_LANE = N_EXPERTS + 1
VMEM_LIMIT = 58 * 1024 * 1024

BF16 = jnp.bfloat16
F32 = jnp.float32
NT_DIMS = (((1,), (1,)), ((), ()))


def _dot(a, b):
    return jnp.dot(a, b, preferred_element_type=F32)


def _dot_nt(a, b, precision=None):
    return lax.dot_general(a, b, NT_DIMS, preferred_element_type=F32, precision=precision)


def _sigmoid(x):
    return 0.5 * jnp.tanh(0.5 * x) + 0.5


def _sigmoid_of_half(xh):
    return 0.5 * jnp.tanh(xh) + 0.5


def _log_sigmoid(x):
    return jnp.minimum(x, 0.0) - jnp.log1p(jnp.exp(-jnp.abs(x)))


def _split3(x):
    hi = x.astype(BF16).astype(F32)
    r1 = x - hi
    mid = r1.astype(BF16).astype(F32)
    lo = (r1 - mid).astype(BF16).astype(F32)
    return hi, mid, lo


def _ada_kernel(cctx_ref, c_ref, w_ref, b_ref, o_ref):
    n = 1 + c_ref.shape[0]
    c = jnp.concatenate([cctx_ref[...], c_ref[...], jnp.zeros((8 - n, D_MODEL), F32)], axis=0)
    s = (c * _sigmoid(c)).astype(BF16)
    out = _dot(s, w_ref[...].astype(BF16)) + b_ref[...]
    for v in range(ADA_PER_STEP):
        o_ref[v] = out[:, v * D_MODEL:(v + 1) * D_MODEL]


def _ada(c_ctx, c, w_ada, b_ada):
    return pl.pallas_call(
        _ada_kernel,
        grid=(N_ADA // ADA_PER_STEP,),
        in_specs=[
            pl.BlockSpec(c_ctx.shape, lambda j: (0, 0)),
            pl.BlockSpec(c.shape, lambda j: (0, 0)),
            pl.BlockSpec((D_MODEL, ADA_PER_STEP * D_MODEL), lambda j: (0, j)),
            pl.BlockSpec((1, ADA_PER_STEP * D_MODEL), lambda j: (0, j)),
        ],
        out_specs=pl.BlockSpec((ADA_PER_STEP, 8, D_MODEL), lambda j: (j, 0, 0)),
        out_shape=jax.ShapeDtypeStruct((N_ADA, 8, D_MODEL), F32),
        compiler_params=pltpu.CompilerParams(dimension_semantics=("arbitrary",)),
        name="ada",
    )(c_ctx, c, w_ada, b_ada)


def _transpose_cast_kernel(starts_ref, halve_ref, wt_ref, o_ref):
    scale = jnp.where(halve_ref[pl.program_id(0)] == 1, 0.5, 1.0)
    o_ref[...] = (wt_ref[...] * scale).astype(BF16).T


def _transpose_cast(w_t, row_starts, halve):
    n, k = len(row_starts), w_t.shape[1]
    return pl.pallas_call(
        _transpose_cast_kernel,
        grid_spec=pltpu.PrefetchScalarGridSpec(
            num_scalar_prefetch=2, grid=(n,),
            in_specs=[pl.BlockSpec((pl.Element(WPREP_ROWS), pl.Element(k)), lambda j, starts, hv: (starts[j] * 8, 0))],
            out_specs=pl.BlockSpec((k, WPREP_ROWS), lambda j, starts, hv: (0, j)),
        ),
        out_shape=jax.ShapeDtypeStruct((k, n * WPREP_ROWS), BF16),
        compiler_params=pltpu.CompilerParams(dimension_semantics=("arbitrary",)),
        name="transpose_cast",
    )(jnp.array([r // 8 for r in row_starts], jnp.int32), jnp.array([int(h) for h in halve], jnp.int32), w_t)


WROW_OFFSET = {"wq": 2 * D_CONV, "wv": 2 * D_CONV + D_MLSTM, "wog": 2 * D_CONV + 2 * D_MLSTM,
               "wgm": 2 * D_CONV + 3 * D_MLSTM}
BROW_K_OFFSET = 2 * D_CONV + 3 * D_MLSTM + 2 * D_MODEL

_MIXER_WEIGHTS = (
    "g1", "wrow", "brow", "wkT", "wgifT", "bgifT", "wdw", "bdw", "lng", "lnb",
    "wco", "hng", "wmo", "wo", "g2", "wrt2", "brtT",
)


def _zero_after(x):
    bits = lax.bitcast_convert_type(x, jnp.uint32)
    bits = lax.shift_right_logical(lax.shift_right_logical(bits, jnp.uint32(16)), jnp.uint32(16))
    return lax.bitcast_convert_type(bits, F32)[0:1, :]


def _conv_block(upad_s, seg, base, cs, wdw_ref, bdw_ref, after=None):
    sub = 8
    first = CONV_PAD - CONV_K // 2
    acc = jnp.broadcast_to(bdw_ref[0:1, cs], (CONV_RB, LANES))
    for r in range(sub):
        z = None
        for a in range((CONV_K + first + sub - 1) // sub):
            j = sub * a + r - first
            if 0 <= j < CONV_K:
                lo = base + sub * a
                tap = wdw_ref[j:j + 1, cs] if after is None else wdw_ref[j:j + 1, cs] + after
                term = tap * upad_s[seg, lo:lo + CONV_RB + sub, cs]
                z = term if z is None else z + term
        acc = acc + z[r:r + CONV_RB, :]
    return acc


def _mixer_kernel(R, T, P, has_state, emit_state, mod_index, *refs):
    L = SUB
    n_mt = R // MIX_TM
    cpm = MIX_TM // L
    n_seq = R // T
    cps = T // L
    nseg = MIX_TM // P
    assert not has_state or n_seq == 1
    it = iter(refs)
    x_ref = next(it)
    mod_ref = next(it)
    if has_state:
        c0_ref = next(it)
        n0_ref = next(it)
        m0_ref = next(it)
    w = {name: next(it) for name in _MIXER_WEIGHTS}
    x1_ref = next(it)
    h2_ref = next(it)
    comb_ref = next(it)
    route_ref = next(it)
    cnt_ref = next(it)
    if emit_state:
        cout_ref = next(it)
        nout_ref = next(it)
        mout_ref = next(it)
    (q_s, kT_s, v_s, so_s, scan_s, ma_s, sgb_s, hm_s, cst_s, upad_s) = [next(it) for _ in range(10)]

    cond_row = mod_index(pl.program_id(0))

    def mod_row(i):
        return mod_ref[i, pl.ds(cond_row, 1), :]

    zpad = jnp.zeros((CONV_PAD, D_CONV), F32)
    for seg in range(nseg):
        upad_s[seg, 0:CONV_PAD, :] = zpad
        upad_s[seg, CONV_PAD + P:CONV_PAD + P + CONV_PAD, :] = zpad

    t_idx = lax.broadcasted_iota(jnp.int32, (L, L), 0)
    s_idx = lax.broadcasted_iota(jnp.int32, (L, L), 1)
    lower = s_idx <= t_idx
    upper = s_idx >= t_idx
    triu_b = upper.astype(F32).astype(BF16)
    lane_u = lax.broadcasted_iota(jnp.int32, (N_UNITS, L), 1)
    is_bwd = lax.broadcasted_iota(jnp.int32, (N_UNITS, L), 0) >= N_HEADS

    def gate_scan(g):
        gi, lf = g[:N_UNITS], _log_sigmoid(g[N_UNITS:])
        pr = _dot(jnp.concatenate(_split3(lf), axis=0).astype(BF16), triu_b)
        pre = pr[0:N_UNITS] + pr[N_UNITS:2 * N_UNITS] + pr[2 * N_UNITS:]
        tot = pre[:, L - 1:L]
        bsum = jnp.where(is_bwd, tot - pre + lf, pre)
        a = gi - bsum
        pm, sm, k = a, a, 1
        while k < L:
            pm = jnp.where(lane_u >= k, jnp.maximum(pm, pltpu.roll(pm, k, axis=1)), pm)
            sm = jnp.where(lane_u < L - k, jnp.maximum(sm, pltpu.roll(sm, L - k, axis=1)), sm)
            k *= 2
        wide = lambda v: jnp.broadcast_to(v, (N_UNITS, L))
        return jnp.concatenate([a, jnp.where(is_bwd, sm, pm), bsum, wide(tot),
                                wide(jnp.max(a, axis=1, keepdims=True))], axis=0)

    def phase1(i, carry):
        r0 = pl.multiple_of(i * MIX_TM, MIX_TM)
        rows = pl.ds(r0, MIX_TM)
        x = x_ref[0, rows, :]
        xn = x * lax.rsqrt(jnp.mean(x * x, axis=-1, keepdims=True) + EPS) * w["g1"][...]
        hb = (xn * (1.0 + mod_row(1)) + mod_row(0)).astype(BF16)

        ag = _dot(hb, w["wrow"][:, :2 * D_CONV]) + w["brow"][:, :2 * D_CONV]
        u = ag[:, :D_CONV] * _sigmoid_of_half(ag[:, D_CONV:])
        for seg in range(nseg):
            upad_s[seg, CONV_PAD:CONV_PAD + P, :] = u[seg * P:(seg + 1) * P, :]
        gates = _dot_nt(w["wgifT"][...].astype(BF16), hb)
        gates = jnp.concatenate([gates[d * 2 * N_HEADS + g * N_HEADS:d * 2 * N_HEADS + (g + 1) * N_HEADS]
                                 for g in range(2) for d in range(2)], axis=0) + w["bgifT"][...]
        for j in range(cpm):
            scan_s[i * cpm + j] = gate_scan(gates[:, j * L:(j + 1) * L])

        def proj(name, c0, width=2 * LANES):
            w0 = WROW_OFFSET[name] + c0
            return _dot(hb, w["wrow"][:, w0:w0 + width]) + w["brow"][:, w0:w0 + width]

        last = lambda z: z[-8:, -LANES:]
        bk_row = w["brow"][:, BROW_K_OFFSET:BROW_K_OFFSET + D_MLSTM]
        bk_col = jnp.concatenate([bk_row, jnp.zeros((LANES - 1, D_MLSTM), F32)], axis=0).T[:, 0:1]

        def gm_a(c0):
            z = proj("wgm", c0)
            ma_s[rows, c0:c0 + 2 * LANES] = _sigmoid_of_half(z)
            return last(z)

        def gm_b(c0):
            z = proj("wgm", D_MODEL + c0)
            sgb_s[rows, c0:c0 + 2 * LANES] = _sigmoid_of_half(z)
            return last(z)

        def q_part(c0):
            z = proj("wq", c0)
            q_s[rows, c0:c0 + 2 * LANES] = (z * (HEAD_DIM ** -0.5)).astype(BF16)
            return last(z)

        def v_part(c0):
            z = proj("wv", c0)
            v_s[rows, c0:c0 + 2 * LANES] = z.astype(BF16)
            return last(z)

        def o_part(c0):
            z = proj("wog", c0)
            so_s[rows, c0:c0 + 2 * LANES] = _sigmoid_of_half(z)
            return last(z)

        def k_part(c0):
            rs = slice(c0, c0 + 2 * LANES)
            z = _dot_nt(w["wkT"][rs, :].astype(BF16), hb) + bk_col[rs, :]
            kt = z.astype(BF16)
            for j in range(cpm):
                kT_s[i * cpm + j, rs, :] = kt[:, j * L:(j + 1) * L]
            return last(z)

        jobs = ([functools.partial(gm_a, c0) for c0 in range(0, D_MODEL, 2 * LANES)]
                + [functools.partial(gm_b, c0) for c0 in range(0, D_MODEL, 2 * LANES)]
                + [functools.partial(f, c0) for f in (q_part, k_part, v_part, o_part)
                   for c0 in range(0, D_MLSTM, 2 * LANES)])
        n_jobs = len(jobs)
        conv = {}
        after, lag = None, [None] * CHAIN_SLACK
        n_pieces = (D_CONV // LANES) * nseg * (P // CONV_RB)
        for cb in range(D_CONV // LANES):
            cs = slice(cb * LANES, (cb + 1) * LANES)
            for seg in range(nseg):
                for rb in range(P // CONV_RB):
                    blk = _conv_block(upad_s, seg, rb * CONV_RB, cs, w["wdw"], w["bdw"], after)
                    conv[(cb, seg, rb)] = blk
                    if jobs and len(conv) * n_jobs >= (n_jobs - len(jobs) + 1) * n_pieces:
                        lag.append(_zero_after(jobs.pop(0)()))
                        after = lag.pop(0)
        for job in jobs:
            job()
        cu = jnp.concatenate(
            [jnp.concatenate([conv[(cb, seg, rb)] for seg in range(nseg) for rb in range(P // CONV_RB)], axis=0)
             for cb in range(D_CONV // LANES)], axis=1)
        mu = jnp.mean(cu, axis=-1, keepdims=True)
        cc = cu - mu
        cn = cc * lax.rsqrt(jnp.mean(cc * cc, axis=-1, keepdims=True) + EPS) * w["lng"][...] + w["lnb"][...]
        ca = (cn * _sigmoid(cn)).astype(BF16)
        ma_s[rows, :] = ma_s[rows, :] * _dot(ca, w["wco"][...])
        return carry

    if n_mt == 1:
        phase1(0, 0)
    else:
        lax.fori_loop(0, n_mt, phase1, 0)

    ones_col = (lax.broadcasted_iota(jnp.int32, (L, HEAD_DIM), 1) == 0).astype(F32).astype(BF16)
    pad_rows = jnp.zeros((LANES - 3 * N_UNITS, L), F32)

    def gate_prep(c, m_vec):
        sc = scan_s[c]
        a, run_max, bsum = sc[0:N_UNITS], sc[N_UNITS:2 * N_UNITS], sc[2 * N_UNITS:3 * N_UNITS]
        tot, a_max = sc[3 * N_UNITS:4 * N_UNITS, 0:1], sc[4 * N_UNITS:5 * N_UNITS, 0:1]
        big_m = jnp.maximum(m_vec, run_max)
        m_end = jnp.maximum(m_vec, a_max)
        cols = jnp.concatenate(
            [big_m, jnp.exp(m_vec - big_m), jnp.exp(-bsum - big_m), pad_rows], axis=0).T
        return a, cols, jnp.exp(a - m_end), jnp.exp(m_vec - m_end), tot + m_end

    def unit_group(dirs, c, prep, first_chunk, want_state):
        a, cols, wk, decay, _ = prep
        rows = slice(c * L, (c + 1) * L)
        heads = range(N_HEADS)
        units = [(d, hd) for d in dirs for hd in heads]
        hs = [slice(hd * HEAD_DIM, (hd + 1) * HEAD_DIM) for hd in heads]
        idx = {u: u[0] * N_HEADS + u[1] for u in units}
        col = lambda k, u: cols[:, k * N_UNITS + idx[u]:k * N_UNITS + idx[u] + 1]
        row = lambda arr, u: arr[idx[u]:idx[u] + 1, :]
        chained = has_state or not first_chunk
        qc = [q_s[rows, hs[hd]] for hd in heads]
        kTc = [kT_s[c, hs[hd], :] for hd in heads]
        vaug = [jnp.concatenate([v_s[rows, hs[hd]], ones_col], axis=1) for hd in heads]
        qk = [_dot(qc[hd], kTc[hd]) for hd in heads]
        s_mat = {u: (qk[u[1]] * jnp.where(lower if u[0] == 0 else upper, jnp.exp(row(a, u) - col(0, u)), 0.0)
                     ).astype(BF16) for u in units}
        nd = {u: _dot(s_mat[u], vaug[u[1]]) for u in units}
        if chained:
            nd = {u: nd[u] + col(1, u) * _dot(qc[u[1]], cst_s[idx[u]].astype(BF16)) for u in units}
        h = {u: nd[u][:, :HEAD_DIM] * (1.0 / jnp.maximum(jnp.abs(nd[u][:, HEAD_DIM:HEAD_DIM + 1]), col(2, u)))
             for u in units}
        for hd in heads:
            total = h[(dirs[0], hd)]
            for d in dirs[1:]:
                total = total + h[(d, hd)]
            if dirs[0] == 0:
                hm_s[rows, hs[hd]] = total
            else:
                hm_s[rows, hs[hd]] = hm_s[rows, hs[hd]] + total
        if want_state:
            kw = {u: (kTc[u[1]].astype(F32) * row(wk, u)).astype(BF16) for u in units}
            upd = {u: _dot(kw[u], vaug[u[1]]) for u in units}
            for u in units:
                cst_s[idx[u]] = (upd[u] + row(decay, u) * cst_s[idx[u]]) if chained else upd[u]

    dir_rows = lax.broadcasted_iota(jnp.int32, (N_UNITS, 1), 0) >= N_HEADS
    for seq in range(n_seq):
        if has_state:
            n_cols = jnp.concatenate([n0_ref[0], jnp.zeros((LANES - N_UNITS, HEAD_DIM), F32)], axis=0).T
            first_lane = lax.broadcasted_iota(jnp.int32, (HEAD_DIM, HEAD_DIM), 1) == 0
            for idx in range(N_UNITS):
                cst_s[idx, :, :HEAD_DIM] = c0_ref[0, idx]
                cst_s[idx, :, HEAD_DIM:] = jnp.where(first_lane, n_cols[:, idx:idx + 1], 0.0)
            unit_row = lax.broadcasted_iota(jnp.int32, (N_UNITS, 1), 0)
            m_vec = jnp.zeros((N_UNITS, 1), F32)
            for idx in range(N_UNITS):
                m_vec = jnp.where(unit_row == idx, m0_ref[pl.program_id(0), idx], m_vec)
        else:
            m_vec = jnp.zeros((N_UNITS, 1), F32)
        if cps == 1:
            prep = gate_prep(seq, m_vec)
            unit_group([0, 1], seq, prep, True, emit_state)
            m_vec = prep[4]
        else:
            for d in range(2):
                order = list(range(cps)) if d == 0 else list(range(cps - 1, -1, -1))
                for pos, c in enumerate(order):
                    prep = gate_prep(seq * cps + c, m_vec)
                    unit_group([d], seq * cps + c, prep, pos == 0, emit_state or pos < cps - 1)
                    m_vec = jnp.where(dir_rows == (d == 1), prep[4], m_vec)
        if emit_state:
            for idx in range(N_UNITS):
                caug = cst_s[idx]
                cout_ref[0, seq * N_UNITS + idx] = caug[:, :HEAD_DIM]
                nout_ref[0, seq * N_UNITS + idx:seq * N_UNITS + idx + 1, :] = caug[:, HEAD_DIM:].T[0:1, :]
            mout_ref[0, seq * N_UNITS:(seq + 1) * N_UNITS, :] = jnp.broadcast_to(m_vec, (N_UNITS, LANES))

    e_iota = lax.broadcasted_iota(jnp.int32, (LANES, MIX_TM), 0)
    g_of_e = lax.shift_right_logical(e_iota, 2)
    j_of_e = lax.bitwise_and(e_iota, EXPERTS_PER_GROUP - 1)
    r8 = lax.broadcasted_iota(jnp.int32, (8, MIX_TM), 0)
    before_b = (lax.broadcasted_iota(jnp.int32, (MOE_BLK, MOE_BLK), 0)
                < lax.broadcasted_iota(jnp.int32, (MOE_BLK, MOE_BLK), 1)).astype(F32).astype(BF16)

    def phase3(i, carry):
        r0 = pl.multiple_of(i * MIX_TM, MIX_TM)
        rows = pl.ds(r0, MIX_TM)
        hm = hm_s[rows, :]
        heads = []
        for hd in range(N_HEADS):
            hh = hm[:, hd * HEAD_DIM:(hd + 1) * HEAD_DIM]
            heads.append(hh * lax.rsqrt(jnp.mean(hh * hh, axis=-1, keepdims=True) + EPS))
        hn = jnp.concatenate(heads, axis=1) * w["hng"][...]
        hb2 = (so_s[rows, :] * hn).astype(BF16)
        br_b = _dot(hb2, w["wmo"][...])
        mixed = (ma_s[rows, :] + sgb_s[rows, :] * br_b).astype(BF16)
        x1 = x_ref[0, rows, :] + mod_row(2) * _dot(mixed, w["wo"][...])
        x1_ref[0, rows, :] = x1
        xn = x1 * lax.rsqrt(jnp.mean(x1 * x1, axis=-1, keepdims=True) + EPS) * w["g2"][...]
        h2 = xn * (1.0 + mod_row(4)) + mod_row(3)
        h2_ref[0, rows, :] = h2.astype(BF16)

        h2_hi = h2.astype(BF16)
        h2_lo = (h2 - h2_hi.astype(F32)).astype(BF16)
        lg = _dot(h2_hi, w["wrt2"][...])
        lg = lg[:, :LANES] + lg[:, LANES:] + _dot(h2_lo, w["wrt2"][:, :LANES])
        lt = lg.T + w["brtT"][...]
        gl = [lt[N_EXPERTS + g:N_EXPERTS + g + 1, :] for g in range(N_GROUPS)]
        best, gsel = gl[0], jnp.zeros((1, MIX_TM), jnp.int32)
        for g in range(1, N_GROUPS):
            better = gl[g] > best
            gsel = jnp.where(better, g, gsel)
            best = jnp.where(better, gl[g], best)
        gp_sel = 1.0 / sum(jnp.exp(v - best) for v in gl)
        el = []
        for j in range(EXPERTS_PER_GROUP):
            v = lt[j:j + 1, :]
            for g in range(1, N_GROUPS):
                r = g * EXPERTS_PER_GROUP + j
                v = jnp.where(gsel == g, lt[r:r + 1, :], v)
            el.append(v)
        l1, e1 = el[0], jnp.zeros((1, MIX_TM), jnp.int32)
        for j in range(1, EXPERTS_PER_GROUP):
            better = el[j] > l1
            e1 = jnp.where(better, j, e1)
            l1 = jnp.where(better, el[j], l1)
        l2 = jnp.full((1, MIX_TM), -jnp.inf, F32)
        e2 = jnp.zeros((1, MIX_TM), jnp.int32)
        for j in range(EXPERTS_PER_GROUP):
            better = jnp.logical_and(e1 != j, el[j] > l2)
            e2 = jnp.where(better, j, e2)
            l2 = jnp.where(better, el[j], l2)
        r2 = jnp.exp(l2 - l1)
        wt1 = gp_sel / (1.0 + r2)
        wt2 = gp_sel * r2 / (1.0 + r2)
        in_group = g_of_e == gsel
        comb_t = (jnp.where(jnp.logical_and(in_group, j_of_e == e1), wt1, 0.0)
                  + jnp.where(jnp.logical_and(in_group, j_of_e == e2), wt2, 0.0))

        onehot = (r8 == gsel).astype(F32)
        gsel_f = gsel.astype(F32)
        rank = jnp.sum(onehot * _dot(onehot.astype(BF16), before_b), axis=0, keepdims=True)
        r8rows = pl.ds(pl.multiple_of(i * 8, 8), 8)
        route_ref[0, r8rows, :] = jnp.where(r8 == 0, gsel_f, jnp.where(r8 == 1, rank, 0.0))
        cnt_ref[0, r8rows, :] = jnp.broadcast_to(jnp.sum(onehot, axis=1, keepdims=True), (8, LANES))
        comb_t = jnp.where(e_iota == ROUTE_GROUP_LANE, gsel_f,
                           jnp.where(e_iota == ROUTE_RANK_LANE, rank, comb_t))
        comb_ref[0, rows, :] = comb_t.T
        return carry

    if n_mt == 1:
        phase3(0, 0)
    else:
        lax.fori_loop(0, n_mt, phase3, 0)


class _RowWindow(NamedTuple):
    array: jax.Array
    start: int
    n: int


def _const_spec(a):
    if isinstance(a, _RowWindow):
        assert a.start % a.n == 0
        return a.array, pl.BlockSpec((a.n, a.array.shape[1]), lambda b: (a.start // a.n, 0),
                                     pipeline_mode=pl.Buffered(1))
    nd = a.ndim
    return a, pl.BlockSpec(a.shape, lambda b, _nd=nd: (0,) * _nd, pipeline_mode=pl.Buffered(1))


def _mixer(x, T, mod, mod_index, weights, P, state=None, emit_state=False):
    B, R, _ = x.shape
    n_chunks = R // SUB
    n_blk = R // MOE_BLK
    n_seq = R // T
    has_state = state is not None
    seq_mode = {} if R <= MIX_TM else {"pipeline_mode": pl.Buffered(1)}
    in_specs = [
        pl.BlockSpec((1, R, D_MODEL), lambda b: (b, 0, 0), **seq_mode),
        pl.BlockSpec(mod.shape, lambda b: (0, 0, 0)),
    ]
    args = [x, mod]
    if has_state:
        c0, n0, m0 = state
        in_specs += [
            pl.BlockSpec((1, N_UNITS, HEAD_DIM, HEAD_DIM), lambda b: (b, 0, 0, 0)),
            pl.BlockSpec((1, N_UNITS, HEAD_DIM), lambda b: (b, 0, 0)),
            pl.BlockSpec(memory_space=pltpu.SMEM),
        ]
        args += [c0, n0, m0]
    for name in _MIXER_WEIGHTS:
        operand, spec = _const_spec(weights[name])
        in_specs.append(spec)
        args.append(operand)
    out_shape = [
        jax.ShapeDtypeStruct((B, R, D_MODEL), F32),
        jax.ShapeDtypeStruct((B, R, D_MODEL), BF16),
        jax.ShapeDtypeStruct((B, R, LANES), F32),
        jax.ShapeDtypeStruct((B, n_blk * 8, MOE_BLK), F32),
        jax.ShapeDtypeStruct((B, n_blk * 8, LANES), F32),
    ]
    out_specs = [
        pl.BlockSpec((1, R, D_MODEL), lambda b: (b, 0, 0), **seq_mode),
        pl.BlockSpec((1, R, D_MODEL), lambda b: (b, 0, 0), **seq_mode),
        pl.BlockSpec((1, R, LANES), lambda b: (b, 0, 0)),
        pl.BlockSpec((1, n_blk * 8, MOE_BLK), lambda b: (b, 0, 0)),
        pl.BlockSpec((1, n_blk * 8, LANES), lambda b: (b, 0, 0)),
    ]
    if emit_state:
        out_shape += [
            jax.ShapeDtypeStruct((B, n_seq * N_UNITS, HEAD_DIM, HEAD_DIM), F32),
            jax.ShapeDtypeStruct((B, n_seq * N_UNITS, HEAD_DIM), F32),
            jax.ShapeDtypeStruct((B, n_seq * N_UNITS, LANES), F32),
        ]
        out_specs += [
            pl.BlockSpec((1, n_seq * N_UNITS, HEAD_DIM, HEAD_DIM), lambda b: (b, 0, 0, 0)),
            pl.BlockSpec((1, n_seq * N_UNITS, HEAD_DIM), lambda b: (b, 0, 0)),
            pl.BlockSpec((1, n_seq * N_UNITS, LANES), lambda b: (b, 0, 0)),
        ]
    scratch = [
        pltpu.VMEM((R, D_MLSTM), BF16),
        pltpu.VMEM((n_chunks, D_MLSTM, SUB), BF16),
        pltpu.VMEM((R, D_MLSTM), BF16),
        pltpu.VMEM((R, D_MLSTM), F32),
        pltpu.VMEM((n_chunks, 5 * N_UNITS, SUB), F32),
        pltpu.VMEM((R, D_MODEL), F32),
        pltpu.VMEM((R, D_MODEL), F32),
        pltpu.VMEM((R, D_MLSTM), F32),
        pltpu.VMEM((N_UNITS, HEAD_DIM, 2 * HEAD_DIM), F32),
        pltpu.VMEM((MIX_TM // P, P + 2 * CONV_PAD, D_CONV), F32),
    ]
    return pl.pallas_call(
        functools.partial(_mixer_kernel, R, T, P, has_state, emit_state, mod_index),
        grid=(B,),
        in_specs=in_specs,
        out_specs=out_specs,
        out_shape=out_shape,
        scratch_shapes=scratch,
        compiler_params=pltpu.CompilerParams(
            dimension_semantics=("arbitrary",), vmem_limit_bytes=VMEM_LIMIT),
        name="mixer_T%d" % T,
    )(*args)


def _dest_in_block(group, rank, starts):
    dest = rank
    for g in range(N_GROUPS):
        dest = dest + jnp.where(group == float(g), starts[g], 0.0)
    return dest


def _copy_segments(src_refs, dst_refs, src_starts, dst_starts, n_pieces):
    def copy(g, first_piece, n_rows):
        s = pl.multiple_of(src_starts[g] + first_piece * ROW_ALIGN, ROW_ALIGN)
        d = pl.multiple_of(dst_starts[g] + first_piece * ROW_ALIGN, ROW_ALIGN)
        for src, dst in zip(src_refs, dst_refs):
            dst[pl.ds(d, n_rows), :] = src[pl.ds(s, n_rows), :]

    for g in range(N_GROUPS):
        n_runs = lax.shift_right_logical(n_pieces[g], COPY_RUN.bit_length() - 1)

        def run(k, carry, g=g):
            copy(g, k * COPY_RUN, COPY_RUN * ROW_ALIGN)
            return carry

        def single(k, carry, g=g):
            copy(g, k, ROW_ALIGN)
            return carry

        lax.fori_loop(0, n_runs, run, 0)
        lax.fori_loop(n_runs * COPY_RUN, n_pieces[g], single, 0)


def _plan_segments(n_blocks, n_tiles, count, start_ref, npiece_ref, off_ref, tgroup_ref, tvalid_ref):
    align_shift = ROW_ALIGN.bit_length() - 1
    tile_shift = MOE_TM.bit_length() - 1

    def block_starts(blk, carry):
        row = jnp.int32(0)
        for g in range(N_GROUPS):
            n = lax.shift_right_logical(count(blk, g) + (ROW_ALIGN - 1), align_shift)
            npiece_ref[blk * N_GROUPS + g] = n
            start_ref[blk * N_GROUPS + g] = row
            row = row + n * ROW_ALIGN
        return carry

    lax.fori_loop(0, n_blocks, block_starts, 0)

    base_row = jnp.int32(0)
    base_tile = jnp.int32(0)
    last_group = jnp.int32(0)
    for g in range(N_GROUPS):
        def seg_offsets(blk, row, g=g, base_row=base_row):
            off_ref[blk * N_GROUPS + g] = base_row + row
            return row + npiece_ref[blk * N_GROUPS + g] * ROW_ALIGN

        rows = lax.fori_loop(0, n_blocks, seg_offsets, jnp.int32(0))
        tiles = lax.shift_right_logical(rows + (MOE_TM - 1), tile_shift)

        def mark_tiles(t, carry, g=g, base_tile=base_tile):
            tgroup_ref[base_tile + t] = g
            tvalid_ref[base_tile + t] = 1
            return carry

        lax.fori_loop(0, tiles, mark_tiles, 0)
        last_group = jnp.where(tiles > 0, g, last_group)
        base_row = base_row + tiles * MOE_TM
        base_tile = base_tile + tiles

    def mark_unused(t, carry):
        tgroup_ref[t] = last_group
        tvalid_ref[t] = 0
        return carry

    lax.fori_loop(base_tile, n_tiles, mark_unused, 0)


def _dispatch_kernel(n_ctx_blocks, n_blocks, n_tiles,
                     h2c_ref, h2l_ref, cbc_ref, cbl_ref, rtc_ref, rtl_ref, cntc_ref, cntl_ref,
                     xs_ref, cs_ref, start_ref, npiece_ref, off_ref, tgroup_ref, tvalid_ref,
                     sx_s, sc_s):
    b = pl.program_id(0)
    is_ctx = b < n_ctx_blocks

    def count(blk, g):
        vc = cntc_ref[jnp.minimum(blk, n_ctx_blocks - 1), pl.ds(g, 1), pl.ds(0, 1)]
        vl = cntl_ref[jnp.maximum(blk - n_ctx_blocks, 0), pl.ds(g, 1), pl.ds(0, 1)]
        return jnp.where(blk < n_ctx_blocks, vc, vl)[0, 0].astype(jnp.int32)

    @pl.when(b == 0)
    def _():
        _plan_segments(n_blocks, n_tiles, count, start_ref, npiece_ref, off_ref, tgroup_ref, tvalid_ref)
        xs_ref[...] = jnp.zeros_like(xs_ref)
        cs_ref[...] = jnp.zeros_like(cs_ref)

    starts = [start_ref[b * N_GROUPS + g] for g in range(N_GROUPS)]

    def sort_block(h2_ref, cb_ref, rt_ref):
        h2 = h2_ref[0]
        cb = cb_ref[0]
        rt = rt_ref[0]
        dest = _dest_in_block(rt[0:1, :], rt[1:2, :], [s.astype(F32) for s in starts])
        row = lax.broadcasted_iota(jnp.int32, (SORT_ROWS, MOE_BLK), 0).astype(F32)
        perm = (row == dest).astype(F32).astype(BF16)
        cb_hi = cb.astype(BF16)
        cb_lo = (cb - cb_hi.astype(F32)).astype(BF16)
        srt = _dot(perm, jnp.concatenate([h2, cb_hi, cb_lo], axis=1)).astype(BF16)
        sx_s[...] = srt[:, :D_MODEL]
        sc_s[...] = srt[:, D_MODEL:]

    pl.when(is_ctx)(functools.partial(sort_block, h2c_ref, cbc_ref, rtc_ref))
    pl.when(jnp.logical_not(is_ctx))(functools.partial(sort_block, h2l_ref, cbl_ref, rtl_ref))
    _copy_segments((sx_s, sc_s), (xs_ref, cs_ref), starts,
                   [off_ref[b * N_GROUPS + g] for g in range(N_GROUPS)],
                   [npiece_ref[b * N_GROUPS + g] for g in range(N_GROUPS)])


def _experts_kernel(tgroup_ref, tvalid_ref, xs_ref, cs_ref, wg_ref, wu_ref, wd_ref, ys_ref):
    i = pl.program_id(0)

    @pl.when(tvalid_ref[i] == 1)
    def _():
        x = xs_ref[...]
        comb = cs_ref[:, :LANES].astype(F32) + cs_ref[:, LANES:].astype(F32)
        lane = lax.broadcasted_iota(jnp.int32, comb.shape, 1)
        first = tgroup_ref[i] * EXPERTS_PER_GROUP
        acc = None
        for j in range(EXPERTS_PER_GROUP):
            gj = _dot(x, wg_ref[j].astype(BF16))
            uj = _dot(x, wu_ref[j].astype(BF16))
            cw = jnp.sum(jnp.where(lane == first + j, comb, 0.0), axis=1, keepdims=True)
            out = _dot((gj * _sigmoid(gj) * uj * cw).astype(BF16), wd_ref[j].astype(BF16))
            acc = out if acc is None else acc + out
        ys_ref[...] = acc.astype(BF16)

    @pl.when(tvalid_ref[i] == 0)
    def _():
        ys_ref[...] = jnp.zeros_like(ys_ref)


def _combine_kernel(n_ctx_blocks, blocks_per_lat_seq, start_ref, npiece_ref, off_ref,
                    x1c_ref, x1l_ref, cbc_ref, cbl_ref, ys_ref, mod_ref, gf_ref, yc_ref, yl_ref, loc_s):
    b = pl.program_id(0)
    is_ctx = b < n_ctx_blocks
    starts = [start_ref[b * N_GROUPS + g] for g in range(N_GROUPS)]
    @pl.when(b == 0)
    def _():
        loc_s[...] = jnp.zeros_like(loc_s)

    _copy_segments((ys_ref,), (loc_s,), [off_ref[b * N_GROUPS + g] for g in range(N_GROUPS)], starts,
                   [npiece_ref[b * N_GROUPS + g] for g in range(N_GROUPS)])
    def finish_block(x1_ref, cb_ref, y_ref, mrow):
        cb = cb_ref[0]
        dest = _dest_in_block(cb[:, ROUTE_GROUP_LANE:ROUTE_GROUP_LANE + 1],
                              cb[:, ROUTE_RANK_LANE:ROUTE_RANK_LANE + 1],
                              [s.astype(F32) for s in starts])
        col = lax.broadcasted_iota(jnp.int32, (MOE_BLK, SORT_ROWS), 1).astype(F32)
        unperm = (col == dest).astype(F32).astype(BF16)
        x2 = x1_ref[0] + mod_ref[N_ADA - 1, pl.ds(mrow, 1), :] * _dot(unperm, loc_s[...])
        y_ref[0] = x2 * lax.rsqrt(jnp.mean(x2 * x2, axis=-1, keepdims=True) + EPS) * gf_ref[...]

    lat_row = 1 + jnp.maximum(b - n_ctx_blocks, 0) // blocks_per_lat_seq
    pl.when(is_ctx)(functools.partial(finish_block, x1c_ref, cbc_ref, yc_ref, 0))
    pl.when(jnp.logical_not(is_ctx))(functools.partial(finish_block, x1l_ref, cbl_ref, yl_ref, lat_row))


def _moe(x1c, x1l, h2c, h2l, cbc, cbl, rtc, rtl, cntc, cntl, mod, blocks_per_lat_seq, wg, wu, wd, gf):
    nc, nl = x1c.shape[0], x1l.shape[0]
    nb = nc + nl
    n_rows_max = nb * MOE_BLK + nb * N_GROUPS * (ROW_ALIGN - 1) + N_GROUPS * (MOE_TM - ROW_ALIGN)
    n_tiles = -(-n_rows_max // MOE_TM)
    ns = n_tiles * MOE_TM

    cmap = lambda b, *_: (jnp.minimum(b, nc - 1), 0, 0)
    lmap = lambda b, *_: (jnp.maximum(b - nc, 0), 0, 0)
    whole = lambda *_: (0, 0)
    once = {"pipeline_mode": pl.Buffered(1)}
    arb = pltpu.CompilerParams(dimension_semantics=("arbitrary",), vmem_limit_bytes=VMEM_LIMIT)
    smem = pl.BlockSpec(memory_space=pltpu.SMEM)
    seg_i32 = jax.ShapeDtypeStruct((nb * N_GROUPS,), jnp.int32)
    tile_i32 = jax.ShapeDtypeStruct((n_tiles,), jnp.int32)

    xs, cs, start, npiece, off, tgroup, tvalid = pl.pallas_call(
        functools.partial(_dispatch_kernel, nc, nb, n_tiles),
        grid_spec=pltpu.PrefetchScalarGridSpec(
            num_scalar_prefetch=0, grid=(nb,),
            in_specs=[
                pl.BlockSpec((1, MOE_BLK, D_MODEL), cmap), pl.BlockSpec((1, MOE_BLK, D_MODEL), lmap),
                pl.BlockSpec((1, MOE_BLK, LANES), cmap), pl.BlockSpec((1, MOE_BLK, LANES), lmap),
                pl.BlockSpec((1, 8, MOE_BLK), cmap), pl.BlockSpec((1, 8, MOE_BLK), lmap),
                pl.BlockSpec(cntc.shape, lambda b: (0, 0, 0)), pl.BlockSpec(cntl.shape, lambda b: (0, 0, 0)),
            ],
            out_specs=[pl.BlockSpec((ns, D_MODEL), whole, **once), pl.BlockSpec((ns, 2 * LANES), whole, **once),
                       smem, smem, smem, smem, smem],
            scratch_shapes=[pltpu.VMEM((SORT_ROWS, D_MODEL), BF16), pltpu.VMEM((SORT_ROWS, 2 * LANES), BF16)],
        ),
        out_shape=[jax.ShapeDtypeStruct((ns, D_MODEL), BF16), jax.ShapeDtypeStruct((ns, 2 * LANES), BF16),
                   seg_i32, seg_i32, seg_i32, tile_i32, tile_i32],
        compiler_params=arb,
        name="moe_dispatch",
    )(h2c, h2l, cbc, cbl, rtc, rtl, cntc, cntl)

    wmap = lambda i, tg, tv: (tg[i], 0, 0)
    ys = pl.pallas_call(
        _experts_kernel,
        grid_spec=pltpu.PrefetchScalarGridSpec(
            num_scalar_prefetch=2, grid=(n_tiles,),
            in_specs=[
                pl.BlockSpec((MOE_TM, D_MODEL), lambda i, *_: (i, 0)),
                pl.BlockSpec((MOE_TM, 2 * LANES), lambda i, *_: (i, 0)),
                pl.BlockSpec((EXPERTS_PER_GROUP, D_MODEL, D_EXPERT), wmap),
                pl.BlockSpec((EXPERTS_PER_GROUP, D_MODEL, D_EXPERT), wmap),
                pl.BlockSpec((EXPERTS_PER_GROUP, D_EXPERT, D_MODEL), wmap),
            ],
            out_specs=pl.BlockSpec((MOE_TM, D_MODEL), lambda i, *_: (i, 0)),
        ),
        out_shape=jax.ShapeDtypeStruct((ns, D_MODEL), BF16),
        compiler_params=arb,
        name="moe_experts",
    )(tgroup, tvalid, xs, cs, wg, wu, wd)

    yc, yl = pl.pallas_call(
        functools.partial(_combine_kernel, nc, blocks_per_lat_seq),
        grid_spec=pltpu.PrefetchScalarGridSpec(
            num_scalar_prefetch=3, grid=(nb,),
            in_specs=[
                pl.BlockSpec((1, MOE_BLK, D_MODEL), cmap), pl.BlockSpec((1, MOE_BLK, D_MODEL), lmap),
                pl.BlockSpec((1, MOE_BLK, LANES), cmap), pl.BlockSpec((1, MOE_BLK, LANES), lmap),
                pl.BlockSpec((ns, D_MODEL), whole, **once),
                pl.BlockSpec(mod.shape, lambda *_: (0, 0, 0)),
                pl.BlockSpec((1, D_MODEL), whole),
            ],
            out_specs=[pl.BlockSpec((1, MOE_BLK, D_MODEL), cmap), pl.BlockSpec((1, MOE_BLK, D_MODEL), lmap)],
            scratch_shapes=[pltpu.VMEM((SORT_ROWS, D_MODEL), BF16)],
        ),
        out_shape=[jax.ShapeDtypeStruct((nc, MOE_BLK, D_MODEL), F32),
                   jax.ShapeDtypeStruct((nl, MOE_BLK, D_MODEL), F32)],
        compiler_params=arb,
        name="moe_combine",
    )(start, npiece, off, x1c, x1l, cbc, cbl, ys, mod, gf)
    return yc, yl


def _prep_weights(norm1_g, w_in, b_in, b_gates, w_dw, b_dw, conv_ln_g, conv_ln_b, w_conv_out,
                  mlstm_hn_g, w_mlstm_out, w_o, norm2_g, w_rg, b_rg, w_re, b_re):
    s_a = 2 * D_CONV
    s_q = s_a + D_MLSTM
    s_k = s_q + D_MLSTM
    s_v = s_k + D_MLSTM
    s_o = s_v + D_MLSTM
    s_g = s_o + 4 * N_HEADS
    row = lambda v: v.reshape(1, -1).astype(F32)
    w_t = w_in.T
    keep = [(0, s_q), (s_k, s_o), (s_g, w_in.shape[1])]
    halved = [(D_CONV, s_a), (s_v, s_o), (s_g, w_in.shape[1])]
    is_halved = lambda r: any(a <= r < b for a, b in halved)
    blocks = [r for a, b in keep for r in range(a, b, WPREP_ROWS)]
    wrow = _transpose_cast(w_t, blocks, [is_halved(r) for r in blocks])
    bias_scale = jnp.array([0.5 if is_halved(r) else 1.0 for a, b in keep for r in range(a, b)]
                           + [1.0] * D_MLSTM, F32)
    bg = (b_in[s_o:s_g] + b_gates.reshape(-1)).reshape(2, 2, N_HEADS).transpose(1, 0, 2).reshape(-1, 1)
    row_window = lambda start, n: _RowWindow(w_t, start, n)
    n_rt = N_EXPERTS + N_GROUPS
    wrt = jnp.pad(jnp.concatenate([w_re, w_rg], axis=1), ((0, 0), (0, LANES - n_rt)))
    wrt_hi = wrt.astype(BF16)
    wrt2 = jnp.concatenate([wrt_hi, (wrt - wrt_hi.astype(F32)).astype(BF16)], axis=1)
    brtT = jnp.pad(jnp.concatenate([b_re, b_rg]), (0, LANES - n_rt)).reshape(LANES, 1)
    return {
        "g1": row(norm1_g),
        "wrow": wrow, "brow": row(jnp.concatenate([b_in[a:b] for a, b in keep] + [b_in[s_q:s_k]]) * bias_scale),
        "wkT": row_window(s_q, D_MLSTM), "wgifT": row_window(s_o, 4 * N_HEADS), "bgifT": bg,
        "wdw": w_dw.astype(F32), "bdw": row(b_dw), "lng": row(conv_ln_g), "lnb": row(conv_ln_b),
        "wco": w_conv_out.astype(BF16), "hng": row(mlstm_hn_g), "wmo": w_mlstm_out.astype(BF16),
        "wo": w_o.astype(BF16), "g2": row(norm2_g), "wrt2": wrt2, "brtT": brtT,
    }


def kernel(x_prompt, x_sample, state_C, state_n, state_m, c, c_ctx, norm1_g, w_ada, b_ada, w_in, b_in, b_gates, w_dw, b_dw, conv_ln_g, conv_ln_b, w_conv_out, mlstm_hn_g, w_mlstm_out, w_o, norm2_g, w_rg, b_rg, w_re, b_re, w_e_gate, w_e_up, w_e_down, norm_final_g):
    B, S, _ = x_prompt.shape
    Bd, Sd, _ = x_sample.shape
    assert w_ada.shape[0] == 1, "single trunk layer"
    assert MIX_TM % S == 0 and S % SUB == 0 and Sd % MIX_TM == 0

    mod = _ada(c_ctx.reshape(1, -1), c, w_ada[0], b_ada[0].reshape(1, -1))

    wts = _prep_weights(norm1_g[0], w_in[0], b_in[0], b_gates[0], w_dw[0], b_dw[0], conv_ln_g[0],
                        conv_ln_b[0], w_conv_out[0], mlstm_hn_g[0], w_mlstm_out[0], w_o[0],
                        norm2_g[0], w_rg[0], b_rg[0], w_re[0], b_re[0])

    x1p, h2p, cbp, rtp, cntp, c_new, n_new, m_new = _mixer(
        x_prompt.reshape(B * S // MIX_TM, MIX_TM, D_MODEL), S, mod, lambda b: 0, wts, P=S, emit_state=True)

    state = (state_C[:, 0].reshape(Bd, N_UNITS, HEAD_DIM, HEAD_DIM), state_n[:, 0].reshape(Bd, N_UNITS, HEAD_DIM),
             state_m[:, 0].reshape(Bd, N_UNITS))
    x1s, h2s, cbs, rts, cnts = _mixer(x_sample, Sd, mod, lambda b: 1 + b, wts, P=GRID_W, state=state)

    nc, nl = B * S // MOE_BLK, Bd * Sd // MOE_BLK
    blk = lambda a, n: a.reshape(n, MOE_BLK, a.shape[-1])
    yp, ys = _moe(blk(x1p, nc), blk(x1s, nl), blk(h2p, nc), blk(h2s, nl), blk(cbp, nc), blk(cbs, nl),
                  rtp.reshape(nc, 8, MOE_BLK), rts.reshape(nl, 8, MOE_BLK),
                  cntp.reshape(nc, 8, LANES), cnts.reshape(nl, 8, LANES),
                  mod, Sd // MOE_BLK, w_e_gate[0], w_e_up[0], w_e_down[0], norm_final_g.reshape(1, -1))

    return (yp.reshape(B, S, D_MODEL), ys.reshape(Bd, Sd, D_MODEL),
            c_new.reshape(B, 1, 2, N_HEADS, HEAD_DIM, HEAD_DIM),
            n_new.reshape(B, 1, 2, N_HEADS, HEAD_DIM),
            m_new[:, :, 0].reshape(B, 1, 2, N_HEADS))
```

```python
import functools
from typing import NamedTuple

import jax
import jax.numpy as jnp
from jax import lax
from jax.experimental import pallas as pl
from jax.experimental.pallas import tpu as pltpu

D_MODEL = 1024
D_CONV = 512
CONV_K = 31
D_MLSTM = 512
N_HEADS = 4
HEAD_DIM = D_MLSTM // N_HEADS
N_GROUPS = 4
EXPERTS_PER_GROUP = 4
N_EXPERTS = N_GROUPS * EXPERTS_PER_GROUP
D_EXPERT = 256
N_ADA = 6
EPS = 1e-6
GRID_W = 64

LANES = 128
SUB = 256
CONV_PAD = 16
CONV_RB = 64
N_UNITS = 2 * N_HEADS
ROW_ALIGN = 16
CHAIN_SLACK = 2
COPY_RUN = 4
MOE_TM = 512
MIX_TM = 512
MOE_BLK = MIX_TM
SORT_ROWS = MOE_BLK + N_GROUPS * ROW_ALIGN
ADA_PER_STEP = 2
WPREP_ROWS = 512
ROUTE_GROUP_LANE = N_EXPERTS
ROUTE_RANK_LANE = N_EXPERTS + 1
VMEM_LIMIT = 58 * 1024 * 1024

BF16 = jnp.bfloat16
F32 = jnp.float32
NT_DIMS = (((1,), (1,)), ((), ()))


def _dot(a, b):
    return jnp.dot(a, b, preferred_element_type=F32)


def _dot_nt(a, b, precision=None):
    return lax.dot_general(a, b, NT_DIMS, preferred_element_type=F32, precision=precision)


def _sigmoid(x):
    return 0.5 * jnp.tanh(0.5 * x) + 0.5


def _sigmoid_of_half(xh):
    return 0.5 * jnp.tanh(xh) + 0.5


def _log_sigmoid(x):
    return jnp.minimum(x, 0.0) - jnp.log1p(jnp.exp(-jnp.abs(x)))


def _split3(x):
    hi = x.astype(BF16).astype(F32)
    r1 = x - hi
    mid = r1.astype(BF16).astype(F32)
    lo = (r1 - mid).astype(BF16).astype(F32)
    return hi, mid, lo


def _ada_kernel(cctx_ref, c_ref, w_ref, b_ref, o_ref):
    n = 1 + c_ref.shape[0]
    c = jnp.concatenate([cctx_ref[...], c_ref[...], jnp.zeros((8 - n, D_MODEL), F32)], axis=0)
    s = (c * _sigmoid(c)).astype(BF16)
    out = _dot(s, w_ref[...].astype(BF16)) + b_ref[...]
    for v in range(ADA_PER_STEP):
        o_ref[v] = out[:, v * D_MODEL:(v + 1) * D_MODEL]


def _ada(c_ctx, c, w_ada, b_ada):
    return pl.pallas_call(
        _ada_kernel,
        grid=(N_ADA // ADA_PER_STEP,),
        in_specs=[
            pl.BlockSpec(c_ctx.shape, lambda j: (0, 0)),
            pl.BlockSpec(c.shape, lambda j: (0, 0)),
            pl.BlockSpec((D_MODEL, ADA_PER_STEP * D_MODEL), lambda j: (0, j)),
            pl.BlockSpec((1, ADA_PER_STEP * D_MODEL), lambda j: (0, j)),
        ],
        out_specs=pl.BlockSpec((ADA_PER_STEP, 8, D_MODEL), lambda j: (j, 0, 0)),
        out_shape=jax.ShapeDtypeStruct((N_ADA, 8, D_MODEL), F32),
        compiler_params=pltpu.CompilerParams(dimension_semantics=("arbitrary",)),
        name="ada",
    )(c_ctx, c, w_ada, b_ada)


def _transpose_cast_kernel(starts_ref, halve_ref, wt_ref, o_ref):
    scale = jnp.where(halve_ref[pl.program_id(0)] == 1, 0.5, 1.0)
    o_ref[...] = (wt_ref[...] * scale).astype(BF16).T


def _transpose_cast(w_t, row_starts, halve):
    n, k = len(row_starts), w_t.shape[1]
    return pl.pallas_call(
        _transpose_cast_kernel,
        grid_spec=pltpu.PrefetchScalarGridSpec(
            num_scalar_prefetch=2, grid=(n,),
            in_specs=[pl.BlockSpec((pl.Element(WPREP_ROWS), pl.Element(k)), lambda j, starts, hv: (starts[j] * 8, 0))],
            out_specs=pl.BlockSpec((k, WPREP_ROWS), lambda j, starts, hv: (0, j)),
        ),
        out_shape=jax.ShapeDtypeStruct((k, n * WPREP_ROWS), BF16),
        compiler_params=pltpu.CompilerParams(dimension_semantics=("arbitrary",)),
        name="transpose_cast",
    )(jnp.array([r // 8 for r in row_starts], jnp.int32), jnp.array([int(h) for h in halve], jnp.int32), w_t)


WROW_OFFSET = {"wq": 2 * D_CONV, "wv": 2 * D_CONV + D_MLSTM, "wog": 2 * D_CONV + 2 * D_MLSTM,
               "wgm": 2 * D_CONV + 3 * D_MLSTM}
BROW_K_OFFSET = 2 * D_CONV + 3 * D_MLSTM + 2 * D_MODEL

_MIXER_WEIGHTS = (
    "g1", "wrow", "brow", "wkT", "wgifT", "bgifT", "wdw", "bdw", "lng", "lnb",
    "wco", "hng", "wmo", "wo", "g2", "wrt2", "brtT",
)


def _zero_after(x):
    bits = lax.bitcast_convert_type(x, jnp.uint32)
    bits = lax.shift_right_logical(lax.shift_right_logical(bits, jnp.uint32(16)), jnp.uint32(16))
    return lax.bitcast_convert_type(bits, F32)[0:1, :]


def _conv_block(upad_s, seg, base, cs, wdw_ref, bdw_ref, after=None):
    sub = 8
    first = CONV_PAD - CONV_K // 2
    acc = jnp.broadcast_to(bdw_ref[0:1, cs], (CONV_RB, LANES))
    for r in range(sub):
        z = None
        for a in range((CONV_K + first + sub - 1) // sub):
            j = sub * a + r - first
            if 0 <= j < CONV_K:
                lo = base + sub * a
                tap = wdw_ref[j:j + 1, cs] if after is None else wdw_ref[j:j + 1, cs] + after
                term = tap * upad_s[seg, lo:lo + CONV_RB + sub, cs]
                z = term if z is None else z + term
        acc = acc + z[r:r + CONV_RB, :]
    return acc


def _mixer_kernel(R, T, P, has_state, emit_state, mod_index, *refs):
    L = SUB
    n_mt = R // MIX_TM
    cpm = MIX_TM // L
    n_seq = R // T
    cps = T // L
    nseg = MIX_TM // P
    assert not has_state or n_seq == 1
    it = iter(refs)
    x_ref = next(it)
    mod_ref = next(it)
    if has_state:
        c0_ref = next(it)
        n0_ref = next(it)
        m0_ref = next(it)
    w = {name: next(it) for name in _MIXER_WEIGHTS}
    x1_ref = next(it)
    h2_ref = next(it)
    comb_ref = next(it)
    route_ref = next(it)
    cnt_ref = next(it)
    if emit_state:
        cout_ref = next(it)
        nout_ref = next(it)
        mout_ref = next(it)
    (q_s, kT_s, v_s, so_s, scan_s, ma_s, sgb_s, hm_s, cst_s, upad_s) = [next(it) for _ in range(10)]

    cond_row = mod_index(pl.program_id(0))

    def mod_row(i):
        return mod_ref[i, pl.ds(cond_row, 1), :]

    zpad = jnp.zeros((CONV_PAD, D_CONV), F32)
    for seg in range(nseg):
        upad_s[seg, 0:CONV_PAD, :] = zpad
        upad_s[seg, CONV_PAD + P:CONV_PAD + P + CONV_PAD, :] = zpad

    t_idx = lax.broadcasted_iota(jnp.int32, (L, L), 0)
    s_idx = lax.broadcasted_iota(jnp.int32, (L, L), 1)
    lower = s_idx <= t_idx
    upper = s_idx >= t_idx
    triu_b = upper.astype(F32).astype(BF16)
    lane_u = lax.broadcasted_iota(jnp.int32, (N_UNITS, L), 1)
    is_bwd = lax.broadcasted_iota(jnp.int32, (N_UNITS, L), 0) >= N_HEADS

    def gate_scan(g):
        gi, lf = g[:N_UNITS], _log_sigmoid(g[N_UNITS:])
        pr = _dot(jnp.concatenate(_split3(lf), axis=0).astype(BF16), triu_b)
        pre = pr[0:N_UNITS] + pr[N_UNITS:2 * N_UNITS] + pr[2 * N_UNITS:]
        tot = pre[:, L - 1:L]
        bsum = jnp.where(is_bwd, tot - pre + lf, pre)
        a = gi - bsum
        pm, sm, k = a, a, 1
        while k < L:
            pm = jnp.where(lane_u >= k, jnp.maximum(pm, pltpu.roll(pm, k, axis=1)), pm)
            sm = jnp.where(lane_u < L - k, jnp.maximum(sm, pltpu.roll(sm, L - k, axis=1)), sm)
            k *= 2
        wide = lambda v: jnp.broadcast_to(v, (N_UNITS, L))
        return jnp.concatenate([a, jnp.where(is_bwd, sm, pm), bsum, wide(tot),
                                wide(jnp.max(a, axis=1, keepdims=True))], axis=0)

    def phase1(i, carry):
        r0 = pl.multiple_of(i * MIX_TM, MIX_TM)
        rows = pl.ds(r0, MIX_TM)
        x = x_ref[0, rows, :]
        xn = x * lax.rsqrt(jnp.mean(x * x, axis=-1, keepdims=True) + EPS) * w["g1"][...]
        hb = (xn * (1.0 + mod_row(1)) + mod_row(0)).astype(BF16)

        ag = _dot(hb, w["wrow"][:, :2 * D_CONV]) + w["brow"][:, :2 * D_CONV]
        u = ag[:, :D_CONV] * _sigmoid_of_half(ag[:, D_CONV:])
        for seg in range(nseg):
            upad_s[seg, CONV_PAD:CONV_PAD + P, :] = u[seg * P:(seg + 1) * P, :]
        gates = _dot_nt(w["wgifT"][...].astype(BF16), hb)
        gates = jnp.concatenate([gates[d * 2 * N_HEADS + g * N_HEADS:d * 2 * N_HEADS + (g + 1) * N_HEADS]
                                 for g in range(2) for d in range(2)], axis=0) + w["bgifT"][...]
        for j in range(cpm):
            scan_s[i * cpm + j] = gate_scan(gates[:, j * L:(j + 1) * L])

        def proj(name, c0, width=2 * LANES):
            w0 = WROW_OFFSET[name] + c0
            return _dot(hb, w["wrow"][:, w0:w0 + width]) + w["brow"][:, w0:w0 + width]

        last = lambda z: z[-8:, -LANES:]
        bk_row = w["brow"][:, BROW_K_OFFSET:BROW_K_OFFSET + D_MLSTM]
        bk_col = jnp.concatenate([bk_row, jnp.zeros((LANES - 1, D_MLSTM), F32)], axis=0).T[:, 0:1]

        def gm_a(c0):
            z = proj("wgm", c0)
            ma_s[rows, c0:c0 + 2 * LANES] = _sigmoid_of_half(z)
            return last(z)

        def gm_b(c0):
            z = proj("wgm", D_MODEL + c0)
            sgb_s[rows, c0:c0 + 2 * LANES] = _sigmoid_of_half(z)
            return last(z)

        def q_part(c0):
            z = proj("wq", c0)
            q_s[rows, c0:c0 + 2 * LANES] = (z * (HEAD_DIM ** -0.5)).astype(BF16)
            return last(z)

        def v_part(c0):
            z = proj("wv", c0)
            v_s[rows, c0:c0 + 2 * LANES] = z.astype(BF16)
            return last(z)

        def o_part(c0):
            z = proj("wog", c0)
            so_s[rows, c0:c0 + 2 * LANES] = _sigmoid_of_half(z)
            return last(z)

        def k_part(c0):
            rs = slice(c0, c0 + 2 * LANES)
            z = _dot_nt(w["wkT"][rs, :].astype(BF16), hb) + bk_col[rs, :]
            kt = z.astype(BF16)
            for j in range(cpm):
                kT_s[i * cpm + j, rs, :] = kt[:, j * L:(j + 1) * L]
            return last(z)

        jobs = ([functools.partial(gm_a, c0) for c0 in range(0, D_MODEL, 2 * LANES)]
                + [functools.partial(gm_b, c0) for c0 in range(0, D_MODEL, 2 * LANES)]
                + [functools.partial(f, c0) for f in (q_part, k_part, v_part, o_part)
                   for c0 in range(0, D_MLSTM, 2 * LANES)])
        n_jobs = len(jobs)
        conv = {}
        after, lag = None, [None] * CHAIN_SLACK
        n_pieces = (D_CONV // LANES) * nseg * (P // CONV_RB)
        for cb in range(D_CONV // LANES):
            cs = slice(cb * LANES, (cb + 1) * LANES)
            for seg in range(nseg):
                for rb in range(P // CONV_RB):
                    blk = _conv_block(upad_s, seg, rb * CONV_RB, cs, w["wdw"], w["bdw"], after)
                    conv[(cb, seg, rb)] = blk
                    if jobs and len(conv) * n_jobs >= (n_jobs - len(jobs) + 1) * n_pieces:
                        lag.append(_zero_after(jobs.pop(0)()))
                        after = lag.pop(0)
        for job in jobs:
            job()
        cu = jnp.concatenate(
            [jnp.concatenate([conv[(cb, seg, rb)] for seg in range(nseg) for rb in range(P // CONV_RB)], axis=0)
             for cb in range(D_CONV // LANES)], axis=1)
        mu = jnp.mean(cu, axis=-1, keepdims=True)
        cc = cu - mu
        cn = cc * lax.rsqrt(jnp.mean(cc * cc, axis=-1, keepdims=True) + EPS) * w["lng"][...] + w["lnb"][...]
        ca = (cn * _sigmoid(cn)).astype(BF16)
        ma_s[rows, :] = ma_s[rows, :] * _dot(ca, w["wco"][...])
        return carry

    if n_mt == 1:
        phase1(0, 0)
    else:
        lax.fori_loop(0, n_mt, phase1, 0)

    ones_col = (lax.broadcasted_iota(jnp.int32, (L, HEAD_DIM), 1) == 0).astype(F32).astype(BF16)
    pad_rows = jnp.zeros((LANES - 3 * N_UNITS, L), F32)

    def gate_prep(c, m_vec):
        sc = scan_s[c]
        a, run_max, bsum = sc[0:N_UNITS], sc[N_UNITS:2 * N_UNITS], sc[2 * N_UNITS:3 * N_UNITS]
        tot, a_max = sc[3 * N_UNITS:4 * N_UNITS, 0:1], sc[4 * N_UNITS:5 * N_UNITS, 0:1]
        big_m = jnp.maximum(m_vec, run_max)
        m_end = jnp.maximum(m_vec, a_max)
        cols = jnp.concatenate(
            [big_m, jnp.exp(m_vec - big_m), jnp.exp(-bsum - big_m), pad_rows], axis=0).T
        return a, cols, jnp.exp(a - m_end), jnp.exp(m_vec - m_end), tot + m_end

    def unit_group(dirs, c, prep, first_chunk, want_state):
        a, cols, wk, decay, _ = prep
        rows = slice(c * L, (c + 1) * L)
        heads = range(N_HEADS)
        units = [(d, hd) for d in dirs for hd in heads]
        hs = [slice(hd * HEAD_DIM, (hd + 1) * HEAD_DIM) for hd in heads]
        idx = {u: u[0] * N_HEADS + u[1] for u in units}
        col = lambda k, u: cols[:, k * N_UNITS + idx[u]:k * N_UNITS + idx[u] + 1]
        row = lambda arr, u: arr[idx[u]:idx[u] + 1, :]
        chained = has_state or not first_chunk
        qc = [q_s[rows, hs[hd]] for hd in heads]
        kTc = [kT_s[c, hs[hd], :] for hd in heads]
        vaug = [jnp.concatenate([v_s[rows, hs[hd]], ones_col], axis=1) for hd in heads]
        qk = [_dot(qc[hd], kTc[hd]) for hd in heads]
        s_mat = {u: (qk[u[1]] * jnp.where(lower if u[0] == 0 else upper, jnp.exp(row(a, u) - col(0, u)), 0.0)
                     ).astype(BF16) for u in units}
        nd = {u: _dot(s_mat[u], vaug[u[1]]) for u in units}
        if chained:
            nd = {u: nd[u] + col(1, u) * _dot(qc[u[1]], cst_s[idx[u]].astype(BF16)) for u in units}
        h = {u: nd[u][:, :HEAD_DIM] * (1.0 / jnp.maximum(jnp.abs(nd[u][:, HEAD_DIM:HEAD_DIM + 1]), col(2, u)))
             for u in units}
        for hd in heads:
            total = h[(dirs[0], hd)]
            for d in dirs[1:]:
                total = total + h[(d, hd)]
            if dirs[0] == 0:
                hm_s[rows, hs[hd]] = total
            else:
                hm_s[rows, hs[hd]] = hm_s[rows, hs[hd]] + total
        if want_state:
            kw = {u: (kTc[u[1]].astype(F32) * row(wk, u)).astype(BF16) for u in units}
            upd = {u: _dot(kw[u], vaug[u[1]]) for u in units}
            for u in units:
                cst_s[idx[u]] = (upd[u] + row(decay, u) * cst_s[idx[u]]) if chained else upd[u]

    dir_rows = lax.broadcasted_iota(jnp.int32, (N_UNITS, 1), 0) >= N_HEADS
    for seq in range(n_seq):
        if has_state:
            n_cols = jnp.concatenate([n0_ref[0], jnp.zeros((LANES - N_UNITS, HEAD_DIM), F32)], axis=0).T
            first_lane = lax.broadcasted_iota(jnp.int32, (HEAD_DIM, HEAD_DIM), 1) == 0
            for idx in range(N_UNITS):
                cst_s[idx, :, :HEAD_DIM] = c0_ref[0, idx]
                cst_s[idx, :, HEAD_DIM:] = jnp.where(first_lane, n_cols[:, idx:idx + 1], 0.0)
            unit_row = lax.broadcasted_iota(jnp.int32, (N_UNITS, 1), 0)
            m_vec = jnp.zeros((N_UNITS, 1), F32)
            for idx in range(N_UNITS):
                m_vec = jnp.where(unit_row == idx, m0_ref[pl.program_id(0), idx], m_vec)
        else:
            m_vec = jnp.zeros((N_UNITS, 1), F32)
        if cps == 1:
            prep = gate_prep(seq, m_vec)
            unit_group([0, 1], seq, prep, True, emit_state)
            m_vec = prep[4]
        else:
            for d in range(2):
                order = list(range(cps)) if d == 0 else list(range(cps - 1, -1, -1))
                for pos, c in enumerate(order):
                    prep = gate_prep(seq * cps + c, m_vec)
                    unit_group([d], seq * cps + c, prep, pos == 0, emit_state or pos < cps - 1)
                    m_vec = jnp.where(dir_rows == (d == 1), prep[4], m_vec)
        if emit_state:
            for idx in range(N_UNITS):
                caug = cst_s[idx]
                cout_ref[0, seq * N_UNITS + idx] = caug[:, :HEAD_DIM]
                nout_ref[0, seq * N_UNITS + idx:seq * N_UNITS + idx + 1, :] = caug[:, HEAD_DIM:].T[0:1, :]
            mout_ref[0, seq * N_UNITS:(seq + 1) * N_UNITS, :] = jnp.broadcast_to(m_vec, (N_UNITS, LANES))

    e_iota = lax.broadcasted_iota(jnp.int32, (LANES, MIX_TM), 0)
    g_of_e = lax.shift_right_logical(e_iota, 2)
    j_of_e = lax.bitwise_and(e_iota, EXPERTS_PER_GROUP - 1)
    r8 = lax.broadcasted_iota(jnp.int32, (8, MIX_TM), 0)
    before_b = (lax.broadcasted_iota(jnp.int32, (MOE_BLK, MOE_BLK), 0)
                < lax.broadcasted_iota(jnp.int32, (MOE_BLK, MOE_BLK), 1)).astype(F32).astype(BF16)

    def phase3(i, carry):
        r0 = pl.multiple_of(i * MIX_TM, MIX_TM)
        rows = pl.ds(r0, MIX_TM)
        hm = hm_s[rows, :]
        heads = []
        for hd in range(N_HEADS):
            hh = hm[:, hd * HEAD_DIM:(hd + 1) * HEAD_DIM]
            heads.append(hh * lax.rsqrt(jnp.mean(hh * hh, axis=-1, keepdims=True) + EPS))
        hn = jnp.concatenate(heads, axis=1) * w["hng"][...]
        hb2 = (so_s[rows, :] * hn).astype(BF16)
        br_b = _dot(hb2, w["wmo"][...])
        mixed = (ma_s[rows, :] + sgb_s[rows, :] * br_b).astype(BF16)
        x1 = x_ref[0, rows, :] + mod_row(2) * _dot(mixed, w["wo"][...])
        x1_ref[0, rows, :] = x1
        xn = x1 * lax.rsqrt(jnp.mean(x1 * x1, axis=-1, keepdims=True) + EPS) * w["g2"][...]
        h2 = xn * (1.0 + mod_row(4)) + mod_row(3)
        h2_ref[0, rows, :] = h2.astype(BF16)

        h2_hi = h2.astype(BF16)
        h2_lo = (h2 - h2_hi.astype(F32)).astype(BF16)
        lg = _dot(h2_hi, w["wrt2"][...])
        lg = lg[:, :LANES] + lg[:, LANES:] + _dot(h2_lo, w["wrt2"][:, :LANES])
        lt = lg.T + w["brtT"][...]
        gl = [lt[N_EXPERTS + g:N_EXPERTS + g + 1, :] for g in range(N_GROUPS)]
        best, gsel = gl[0], jnp.zeros((1, MIX_TM), jnp.int32)
        for g in range(1, N_GROUPS):
            better = gl[g] > best
            gsel = jnp.where(better, g, gsel)
            best = jnp.where(better, gl[g], best)
        gp_sel = 1.0 / sum(jnp.exp(v - best) for v in gl)
        el = []
        for j in range(EXPERTS_PER_GROUP):
            v = lt[j:j + 1, :]
            for g in range(1, N_GROUPS):
                r = g * EXPERTS_PER_GROUP + j
                v = jnp.where(gsel == g, lt[r:r + 1, :], v)
            el.append(v)
        l1, e1 = el[0], jnp.zeros((1, MIX_TM), jnp.int32)
        for j in range(1, EXPERTS_PER_GROUP):
            better = el[j] > l1
            e1 = jnp.where(better, j, e1)
            l1 = jnp.where(better, el[j], l1)
        l2 = jnp.full((1, MIX_TM), -jnp.inf, F32)
        e2 = jnp.zeros((1, MIX_TM), jnp.int32)
        for j in range(EXPERTS_PER_GROUP):
            better = jnp.logical_and(e1 != j, el[j] > l2)
            e2 = jnp.where(better, j, e2)
            l2 = jnp.where(better, el[j], l2)
        r2 = jnp.exp(l2 - l1)
        wt1 = gp_sel / (1.0 + r2)
        wt2 = gp_sel * r2 / (1.0 + r2)
        in_group = g_of_e == gsel
        comb_t = (jnp.where(jnp.logical_and(in_group, j_of_e == e1), wt1, 0.0)
                  + jnp.where(jnp.logical_and(in_group, j_of_e == e2), wt2, 0.0))

        onehot = (r8 == gsel).astype(F32)
        gsel_f = gsel.astype(F32)
        rank = jnp.sum(onehot * _dot(onehot.astype(BF16), before_b), axis=0, keepdims=True)
        r8rows = pl.ds(pl.multiple_of(i * 8, 8), 8)
        route_ref[0, r8rows, :] = jnp.where(r8 == 0, gsel_f, jnp.where(r8 == 1, rank, 0.0))
        cnt_ref[0, r8rows, :] = jnp.broadcast_to(jnp.sum(onehot, axis=1, keepdims=True), (8, LANES))
        comb_t = jnp.where(e_iota == ROUTE_GROUP_LANE, gsel_f,
                           jnp.where(e_iota == ROUTE_RANK_LANE, rank, comb_t))
        comb_ref[0, rows, :] = comb_t.T
        return carry

    if n_mt == 1:
        phase3(0, 0)
    else:
        lax.fori_loop(0, n_mt, phase3, 0)


class _RowWindow(NamedTuple):
    array: jax.Array
    start: int
    n: int


def _const_spec(a):
    if isinstance(a, _RowWindow):
        assert a.start % a.n == 0
        return a.array, pl.BlockSpec((a.n, a.array.shape[1]), lambda b: (a.start // a.n, 0),
                                     pipeline_mode=pl.Buffered(1))
    nd = a.ndim
    return a, pl.BlockSpec(a.shape, lambda b, _nd=nd: (0,) * _nd, pipeline_mode=pl.Buffered(1))


def _mixer(x, T, mod, mod_index, weights, P, state=None, emit_state=False):
    B, R, _ = x.shape
    n_chunks = R // SUB
    n_blk = R // MOE_BLK
    n_seq = R // T
    has_state = state is not None
    seq_mode = {} if R <= MIX_TM else {"pipeline_mode": pl.Buffered(1)}
    in_specs = [
        pl.BlockSpec((1, R, D_MODEL), lambda b: (b, 0, 0), **seq_mode),
        pl.BlockSpec(mod.shape, lambda b: (0, 0, 0)),
    ]
    args = [x, mod]
    if has_state:
        c0, n0, m0 = state
        in_specs += [
            pl.BlockSpec((1, N_UNITS, HEAD_DIM, HEAD_DIM), lambda b: (b, 0, 0, 0)),
            pl.BlockSpec((1, N_UNITS, HEAD_DIM), lambda b: (b, 0, 0)),
            pl.BlockSpec(memory_space=pltpu.SMEM),
        ]
        args += [c0, n0, m0]
    for name in _MIXER_WEIGHTS:
        operand, spec = _const_spec(weights[name])
        in_specs.append(spec)
        args.append(operand)
    out_shape = [
        jax.ShapeDtypeStruct((B, R, D_MODEL), F32),
        jax.ShapeDtypeStruct((B, R, D_MODEL), BF16),
        jax.ShapeDtypeStruct((B, R, LANES), F32),
        jax.ShapeDtypeStruct((B, n_blk * 8, MOE_BLK), F32),
        jax.ShapeDtypeStruct((B, n_blk * 8, LANES), F32),
    ]
    out_specs = [
        pl.BlockSpec((1, R, D_MODEL), lambda b: (b, 0, 0), **seq_mode),
        pl.BlockSpec((1, R, D_MODEL), lambda b: (b, 0, 0), **seq_mode),
        pl.BlockSpec((1, R, LANES), lambda b: (b, 0, 0)),
        pl.BlockSpec((1, n_blk * 8, MOE_BLK), lambda b: (b, 0, 0)),
        pl.BlockSpec((1, n_blk * 8, LANES), lambda b: (b, 0, 0)),
    ]
    if emit_state:
        out_shape += [
            jax.ShapeDtypeStruct((B, n_seq * N_UNITS, HEAD_DIM, HEAD_DIM), F32),
            jax.ShapeDtypeStruct((B, n_seq * N_UNITS, HEAD_DIM), F32),
            jax.ShapeDtypeStruct((B, n_seq * N_UNITS, LANES), F32),
        ]
        out_specs += [
            pl.BlockSpec((1, n_seq * N_UNITS, HEAD_DIM, HEAD_DIM), lambda b: (b, 0, 0, 0)),
            pl.BlockSpec((1, n_seq * N_UNITS, HEAD_DIM), lambda b: (b, 0, 0)),
            pl.BlockSpec((1, n_seq * N_UNITS, LANES), lambda b: (b, 0, 0)),
        ]
    scratch = [
        pltpu.VMEM((R, D_MLSTM), BF16),
        pltpu.VMEM((n_chunks, D_MLSTM, SUB), BF16),
        pltpu.VMEM((R, D_MLSTM), BF16),
        pltpu.VMEM((R, D_MLSTM), F32),
        pltpu.VMEM((n_chunks, 5 * N_UNITS, SUB), F32),
        pltpu.VMEM((R, D_MODEL), F32),
        pltpu.VMEM((R, D_MODEL), F32),
        pltpu.VMEM((R, D_MLSTM), F32),
        pltpu.VMEM((N_UNITS, HEAD_DIM, 2 * HEAD_DIM), F32),
        pltpu.VMEM((MIX_TM // P, P + 2 * CONV_PAD, D_CONV), F32),
    ]
    return pl.pallas_call(
        functools.partial(_mixer_kernel, R, T, P, has_state, emit_state, mod_index),
        grid=(B,),
        in_specs=in_specs,
        out_specs=out_specs,
        out_shape=out_shape,
        scratch_shapes=scratch,
        compiler_params=pltpu.CompilerParams(
            dimension_semantics=("arbitrary",), vmem_limit_bytes=VMEM_LIMIT),
        name="mixer_T%d" % T,
    )(*args)


def _dest_in_block(group, rank, starts):
    dest = rank
    for g in range(N_GROUPS):
        dest = dest + jnp.where(group == float(g), starts[g], 0.0)
    return dest


def _copy_segments(src_refs, dst_refs, src_starts, dst_starts, n_pieces):
    def copy(g, first_piece, n_rows):
        s = pl.multiple_of(src_starts[g] + first_piece * ROW_ALIGN, ROW_ALIGN)
        d = pl.multiple_of(dst_starts[g] + first_piece * ROW_ALIGN, ROW_ALIGN)
        for src, dst in zip(src_refs, dst_refs):
            dst[pl.ds(d, n_rows), :] = src[pl.ds(s, n_rows), :]

    for g in range(N_GROUPS):
        n_runs = lax.shift_right_logical(n_pieces[g], COPY_RUN.bit_length() - 1)

        def run(k, carry, g=g):
            copy(g, k * COPY_RUN, COPY_RUN * ROW_ALIGN)
            return carry

        def single(k, carry, g=g):
            copy(g, k, ROW_ALIGN)
            return carry

        lax.fori_loop(0, n_runs, run, 0)
        lax.fori_loop(n_runs * COPY_RUN, n_pieces[g], single, 0)


def _plan_segments(n_blocks, n_tiles, count, start_ref, npiece_ref, off_ref, tgroup_ref, tvalid_ref):
    align_shift = ROW_ALIGN.bit_length() - 1
    tile_shift = MOE_TM.bit_length() - 1

    def block_starts(blk, carry):
        row = jnp.int32(0)
        for g in range(N_GROUPS):
            n = lax.shift_right_logical(count(blk, g) + (ROW_ALIGN - 1), align_shift)
            npiece_ref[blk * N_GROUPS + g] = n
            start_ref[blk * N_GROUPS + g] = row
            row = row + n * ROW_ALIGN
        return carry

    lax.fori_loop(0, n_blocks, block_starts, 0)

    base_row = jnp.int32(0)
    base_tile = jnp.int32(0)
    last_group = jnp.int32(0)
    for g in range(N_GROUPS):
        def seg_offsets(blk, row, g=g, base_row=base_row):
            off_ref[blk * N_GROUPS + g] = base_row + row
            return row + npiece_ref[blk * N_GROUPS + g] * ROW_ALIGN

        rows = lax.fori_loop(0, n_blocks, seg_offsets, jnp.int32(0))
        tiles = lax.shift_right_logical(rows + (MOE_TM - 1), tile_shift)

        def mark_tiles(t, carry, g=g, base_tile=base_tile):
            tgroup_ref[base_tile + t] = g
            tvalid_ref[base_tile + t] = 1
            return carry

        lax.fori_loop(0, tiles, mark_tiles, 0)
        last_group = jnp.where(tiles > 0, g, last_group)
        base_row = base_row + tiles * MOE_TM
        base_tile = base_tile + tiles

    def mark_unused(t, carry):
        tgroup_ref[t] = last_group
        tvalid_ref[t] = 0
        return carry

    lax.fori_loop(base_tile, n_tiles, mark_unused, 0)


def _dispatch_kernel(n_ctx_blocks, n_blocks, n_tiles,
                     h2c_ref, h2l_ref, cbc_ref, cbl_ref, rtc_ref, rtl_ref, cntc_ref, cntl_ref,
                     xs_ref, cs_ref, start_ref, npiece_ref, off_ref, tgroup_ref, tvalid_ref,
                     sx_s, sc_s):
    b = pl.program_id(0)
    is_ctx = b < n_ctx_blocks

    def count(blk, g):
        vc = cntc_ref[jnp.minimum(blk, n_ctx_blocks - 1), pl.ds(g, 1), pl.ds(0, 1)]
        vl = cntl_ref[jnp.maximum(blk - n_ctx_blocks, 0), pl.ds(g, 1), pl.ds(0, 1)]
        return jnp.where(blk < n_ctx_blocks, vc, vl)[0, 0].astype(jnp.int32)

    @pl.when(b == 0)
    def _():
        _plan_segments(n_blocks, n_tiles, count, start_ref, npiece_ref, off_ref, tgroup_ref, tvalid_ref)
        xs_ref[...] = jnp.zeros_like(xs_ref)
        cs_ref[...] = jnp.zeros_like(cs_ref)

    starts = [start_ref[b * N_GROUPS + g] for g in range(N_GROUPS)]

    def sort_block(h2_ref, cb_ref, rt_ref):
        h2 = h2_ref[0]
        cb = cb_ref[0]
        rt = rt_ref[0]
        dest = _dest_in_block(rt[0:1, :], rt[1:2, :], [s.astype(F32) for s in starts])
        row = lax.broadcasted_iota(jnp.int32, (SORT_ROWS, MOE_BLK), 0).astype(F32)
        perm = (row == dest).astype(F32).astype(BF16)
        cb_hi = cb.astype(BF16)
        cb_lo = (cb - cb_hi.astype(F32)).astype(BF16)
        srt = _dot(perm, jnp.concatenate([h2, cb_hi, cb_lo], axis=1)).astype(BF16)
        sx_s[...] = srt[:, :D_MODEL]
        sc_s[...] = srt[:, D_MODEL:]

    pl.when(is_ctx)(functools.partial(sort_block, h2c_ref, cbc_ref, rtc_ref))
    pl.when(jnp.logical_not(is_ctx))(functools.partial(sort_block, h2l_ref, cbl_ref, rtl_ref))
    _copy_segments((sx_s, sc_s), (xs_ref, cs_ref), starts,
                   [off_ref[b * N_GROUPS + g] for g in range(N_GROUPS)],
                   [npiece_ref[b * N_GROUPS + g] for g in range(N_GROUPS)])


def _experts_kernel(tgroup_ref, tvalid_ref, xs_ref, cs_ref, wg_ref, wu_ref, wd_ref, ys_ref):
    i = pl.program_id(0)

    @pl.when(tvalid_ref[i] == 1)
    def _():
        x = xs_ref[...]
        comb = cs_ref[:, :LANES].astype(F32) + cs_ref[:, LANES:].astype(F32)
        lane = lax.broadcasted_iota(jnp.int32, comb.shape, 1)
        first = tgroup_ref[i] * EXPERTS_PER_GROUP
        acc = None
        for j in range(EXPERTS_PER_GROUP):
            gj = _dot(x, wg_ref[j].astype(BF16))
            uj = _dot(x, wu_ref[j].astype(BF16))
            cw = jnp.sum(jnp.where(lane == first + j, comb, 0.0), axis=1, keepdims=True)
            out = _dot((gj * _sigmoid(gj) * uj * cw).astype(BF16), wd_ref[j].astype(BF16))
            acc = out if acc is None else acc + out
        ys_ref[...] = acc.astype(BF16)

    @pl.when(tvalid_ref[i] == 0)
    def _():
        ys_ref[...] = jnp.zeros_like(ys_ref)


def _combine_kernel(n_ctx_blocks, blocks_per_lat_seq, start_ref, npiece_ref, off_ref,
                    x1c_ref, x1l_ref, cbc_ref, cbl_ref, ys_ref, mod_ref, gf_ref, yc_ref, yl_ref, loc_s):
    b = pl.program_id(0)
    is_ctx = b < n_ctx_blocks
    starts = [start_ref[b * N_GROUPS + g] for g in range(N_GROUPS)]
    @pl.when(b == 0)
    def _():
        loc_s[...] = jnp.zeros_like(loc_s)

    _copy_segments((ys_ref,), (loc_s,), [off_ref[b * N_GROUPS + g] for g in range(N_GROUPS)], starts,
                   [npiece_ref[b * N_GROUPS + g] for g in range(N_GROUPS)])
    def finish_block(x1_ref, cb_ref, y_ref, mrow):
        cb = cb_ref[0]
        dest = _dest_in_block(cb[:, ROUTE_GROUP_LANE:ROUTE_GROUP_LANE + 1],
                              cb[:, ROUTE_RANK_LANE:ROUTE_RANK_LANE + 1],
                              [s.astype(F32) for s in starts])
        col = lax.broadcasted_iota(jnp.int32, (MOE_BLK, SORT_ROWS), 1).astype(F32)
        unperm = (col == dest).astype(F32).astype(BF16)
        x2 = x1_ref[0] + mod_ref[N_ADA - 1, pl.ds(mrow, 1), :] * _dot(unperm, loc_s[...])
        y_ref[0] = x2 * lax.rsqrt(jnp.mean(x2 * x2, axis=-1, keepdims=True) + EPS) * gf_ref[...]

    lat_row = 1 + jnp.maximum(b - n_ctx_blocks, 0) // blocks_per_lat_seq
    pl.when(is_ctx)(functools.partial(finish_block, x1c_ref, cbc_ref, yc_ref, 0))
    pl.when(jnp.logical_not(is_ctx))(functools.partial(finish_block, x1l_ref, cbl_ref, yl_ref, lat_row))


def _moe(x1c, x1l, h2c, h2l, cbc, cbl, rtc, rtl, cntc, cntl, mod, blocks_per_lat_seq, wg, wu, wd, gf):
    nc, nl = x1c.shape[0], x1l.shape[0]
    nb = nc + nl
    n_rows_max = nb * MOE_BLK + nb * N_GROUPS * (ROW_ALIGN - 1) + N_GROUPS * (MOE_TM - ROW_ALIGN)
    n_tiles = -(-n_rows_max // MOE_TM)
    ns = n_tiles * MOE_TM

    cmap = lambda b, *_: (jnp.minimum(b, nc - 1), 0, 0)
    lmap = lambda b, *_: (jnp.maximum(b - nc, 0), 0, 0)
    whole = lambda *_: (0, 0)
    once = {"pipeline_mode": pl.Buffered(1)}
    arb = pltpu.CompilerParams(dimension_semantics=("arbitrary",), vmem_limit_bytes=VMEM_LIMIT)
    smem = pl.BlockSpec(memory_space=pltpu.SMEM)
    seg_i32 = jax.ShapeDtypeStruct((nb * N_GROUPS,), jnp.int32)
    tile_i32 = jax.ShapeDtypeStruct((n_tiles,), jnp.int32)

    xs, cs, start, npiece, off, tgroup, tvalid = pl.pallas_call(
        functools.partial(_dispatch_kernel, nc, nb, n_tiles),
        grid_spec=pltpu.PrefetchScalarGridSpec(
            num_scalar_prefetch=0, grid=(nb,),
            in_specs=[
                pl.BlockSpec((1, MOE_BLK, D_MODEL), cmap), pl.BlockSpec((1, MOE_BLK, D_MODEL), lmap),
                pl.BlockSpec((1, MOE_BLK, LANES), cmap), pl.BlockSpec((1, MOE_BLK, LANES), lmap),
                pl.BlockSpec((1, 8, MOE_BLK), cmap), pl.BlockSpec((1, 8, MOE_BLK), lmap),
                pl.BlockSpec(cntc.shape, lambda b: (0, 0, 0)), pl.BlockSpec(cntl.shape, lambda b: (0, 0, 0)),
            ],
            out_specs=[pl.BlockSpec((ns, D_MODEL), whole, **once), pl.BlockSpec((ns, 2 * LANES), whole, **once),
                       smem, smem, smem, smem, smem],
            scratch_shapes=[pltpu.VMEM((SORT_ROWS, D_MODEL), BF16), pltpu.VMEM((SORT_ROWS, 2 * LANES), BF16)],
        ),
        out_shape=[jax.ShapeDtypeStruct((ns, D_MODEL), BF16), jax.ShapeDtypeStruct((ns, 2 * LANES), BF16),
                   seg_i32, seg_i32, seg_i32, tile_i32, tile_i32],
        compiler_params=arb,
        name="moe_dispatch",
    )(h2c, h2l, cbc, cbl, rtc, rtl, cntc, cntl)

    wmap = lambda i, tg, tv: (tg[i], 0, 0)
    ys = pl.pallas_call(
        _experts_kernel,
        grid_spec=pltpu.PrefetchScalarGridSpec(
            num_scalar_prefetch=2, grid=(n_tiles,),
            in_specs=[
                pl.BlockSpec((MOE_TM, D_MODEL), lambda i, *_: (i, 0)),
                pl.BlockSpec((MOE_TM, 2 * LANES), lambda i, *_: (i, 0)),
                pl.BlockSpec((EXPERTS_PER_GROUP, D_MODEL, D_EXPERT), wmap),
                pl.BlockSpec((EXPERTS_PER_GROUP, D_MODEL, D_EXPERT), wmap),
                pl.BlockSpec((EXPERTS_PER_GROUP, D_EXPERT, D_MODEL), wmap),
            ],
            out_specs=pl.BlockSpec((MOE_TM, D_MODEL), lambda i, *_: (i, 0)),
        ),
        out_shape=jax.ShapeDtypeStruct((ns, D_MODEL), BF16),
        compiler_params=arb,
        name="moe_experts",
    )(tgroup, tvalid, xs, cs, wg, wu, wd)

    yc, yl = pl.pallas_call(
        functools.partial(_combine_kernel, nc, blocks_per_lat_seq),
        grid_spec=pltpu.PrefetchScalarGridSpec(
            num_scalar_prefetch=3, grid=(nb,),
            in_specs=[
                pl.BlockSpec((1, MOE_BLK, D_MODEL), cmap), pl.BlockSpec((1, MOE_BLK, D_MODEL), lmap),
                pl.BlockSpec((1, MOE_BLK, LANES), cmap), pl.BlockSpec((1, MOE_BLK, LANES), lmap),
                pl.BlockSpec((ns, D_MODEL), whole, **once),
                pl.BlockSpec(mod.shape, lambda *_: (0, 0, 0)),
                pl.BlockSpec((1, D_MODEL), whole),
            ],
            out_specs=[pl.BlockSpec((1, MOE_BLK, D_MODEL), cmap), pl.BlockSpec((1, MOE_BLK, D_MODEL), lmap)],
            scratch_shapes=[pltpu.VMEM((SORT_ROWS, D_MODEL), BF16)],
        ),
        out_shape=[jax.ShapeDtypeStruct((nc, MOE_BLK, D_MODEL), F32),
                   jax.ShapeDtypeStruct((nl, MOE_BLK, D_MODEL), F32)],
        compiler_params=arb,
        name="moe_combine",
    )(start, npiece, off, x1c, x1l, cbc, cbl, ys, mod, gf)
    return yc, yl


def _prep_weights(norm1_g, w_in, b_in, b_gates, w_dw, b_dw, conv_ln_g, conv_ln_b, w_conv_out,
                  mlstm_hn_g, w_mlstm_out, w_o, norm2_g, w_rg, b_rg, w_re, b_re):
    s_a = 2 * D_CONV
    s_q = s_a + D_MLSTM
    s_k = s_q + D_MLSTM
    s_v = s_k + D_MLSTM
    s_o = s_v + D_MLSTM
    s_g = s_o + 4 * N_HEADS
    row = lambda v: v.reshape(1, -1).astype(F32)
    w_t = w_in.T
    keep = [(0, s_q), (s_k, s_o), (s_g, w_in.shape[1])]
    halved = [(D_CONV, s_a), (s_v, s_o), (s_g, w_in.shape[1])]
    is_halved = lambda r: any(a <= r < b for a, b in halved)
    blocks = [r for a, b in keep for r in range(a, b, WPREP_ROWS)]
    wrow = _transpose_cast(w_t, blocks, [is_halved(r) for r in blocks])
    bias_scale = jnp.array([0.5 if is_halved(r) else 1.0 for a, b in keep for r in range(a, b)]
                           + [1.0] * D_MLSTM, F32)
    bg = (b_in[s_o:s_g] + b_gates.reshape(-1)).reshape(2, 2, N_HEADS).transpose(1, 0, 2).reshape(-1, 1)
    row_window = lambda start, n: _RowWindow(w_t, start, n)
    n_rt = N_EXPERTS + N_GROUPS
    wrt = jnp.pad(jnp.concatenate([w_re, w_rg], axis=1), ((0, 0), (0, LANES - n_rt)))
    wrt_hi = wrt.astype(BF16)
    wrt2 = jnp.concatenate([wrt_hi, (wrt - wrt_hi.astype(F32)).astype(BF16)], axis=1)
    brtT = jnp.pad(jnp.concatenate([b_re, b_rg]), (0, LANES - n_rt)).reshape(LANES, 1)
    return {
        "g1": row(norm1_g),
        "wrow": wrow, "brow": row(jnp.concatenate([b_in[a:b] for a, b in keep] + [b_in[s_q:s_k]]) * bias_scale),
        "wkT": row_window(s_q, D_MLSTM), "wgifT": row_window(s_o, 4 * N_HEADS), "bgifT": bg,
        "wdw": w_dw.astype(F32), "bdw": row(b_dw), "lng": row(conv_ln_g), "lnb": row(conv_ln_b),
        "wco": w_conv_out.astype(BF16), "hng": row(mlstm_hn_g), "wmo": w_mlstm_out.astype(BF16),
        "wo": w_o.astype(BF16), "g2": row(norm2_g), "wrt2": wrt2, "brtT": brtT,
    }


def kernel(x_prompt, x_sample, state_C, state_n, state_m, c, c_ctx, norm1_g, w_ada, b_ada, w_in, b_in, b_gates, w_dw, b_dw, conv_ln_g, conv_ln_b, w_conv_out, mlstm_hn_g, w_mlstm_out, w_o, norm2_g, w_rg, b_rg, w_re, b_re, w_e_gate, w_e_up, w_e_down, norm_final_g):
    B, S, _ = x_prompt.shape
    Bd, Sd, _ = x_sample.shape
    assert w_ada.shape[0] == 1, "single trunk layer"
    assert MIX_TM % S == 0 and S % SUB == 0 and Sd % MIX_TM == 0

    mod = _ada(c_ctx.reshape(1, -1), c, w_ada[0], b_ada[0].reshape(1, -1))

    wts = _prep_weights(norm1_g[0], w_in[0], b_in[0], b_gates[0], w_dw[0], b_dw[0], conv_ln_g[0],
                        conv_ln_b[0], w_conv_out[0], mlstm_hn_g[0], w_mlstm_out[0], w_o[0],
                        norm2_g[0], w_rg[0], b_rg[0], w_re[0], b_re[0])

    x1p, h2p, cbp, rtp, cntp, c_new, n_new, m_new = _mixer(
        x_prompt.reshape(B * S // MIX_TM, MIX_TM, D_MODEL), S, mod, lambda b: 0, wts, P=S, emit_state=True)

    state = (state_C[:, 0].reshape(Bd, N_UNITS, HEAD_DIM, HEAD_DIM), state_n[:, 0].reshape(Bd, N_UNITS, HEAD_DIM),
             state_m[:, 0].reshape(Bd, N_UNITS))
    x1s, h2s, cbs, rts, cnts = _mixer(x_sample, Sd, mod, lambda b: 1 + b, wts, P=GRID_W, state=state)

    nc, nl = B * S // MOE_BLK, Bd * Sd // MOE_BLK
    blk = lambda a, n: a.reshape(n, MOE_BLK, a.shape[-1])
    yp, ys = _moe(blk(x1p, nc), blk(x1s, nl), blk(h2p, nc), blk(h2s, nl), blk(cbp, nc), blk(cbs, nl),
                  rtp.reshape(nc, 8, MOE_BLK), rts.reshape(nl, 8, MOE_BLK),
                  cntp.reshape(nc, 8, LANES), cnts.reshape(nl, 8, LANES),
                  mod, Sd // MOE_BLK, w_e_gate[0], w_e_up[0], w_e_down[0], norm_final_g.reshape(1, -1))

    return (yp.reshape(B, S, D_MODEL), ys.reshape(Bd, Sd, D_MODEL),
            c_new.reshape(B, 1, 2, N_HEADS, HEAD_DIM, HEAD_DIM),
            n_new.reshape(B, 1, 2, N_HEADS, HEAD_DIM),
            m_new[:, :, 0].reshape(B, 1, 2, N_HEADS))
```

```python
import functools
from typing import NamedTuple

import jax
import jax.numpy as jnp
from jax import lax
from jax.experimental import pallas as pl
from jax.experimental.pallas import tpu as pltpu

D_MODEL = 1024
D_CONV = 512
CONV_K = 31
D_MLSTM = 512
N_HEADS = 4
HEAD_DIM = D_MLSTM // N_HEADS
N_GROUPS = 4
EXPERTS_PER_GROUP = 4
N_EXPERTS = N_GROUPS * EXPERTS_PER_GROUP
D_EXPERT = 256
N_ADA = 6
EPS = 1e-6
GRID_W = 64

LANES = 128
SUB = 256
CONV_PAD = 16
CONV_RB = 64
N_UNITS = 2 * N_HEADS
ROW_ALIGN = 16
CHAIN_SLACK = 1
COPY_RUN = 4
MOE_TM = 512
MIX_TM = 512
MOE_BLK = MIX_TM
SORT_ROWS = MOE_BLK + N_GROUPS * ROW_ALIGN
ADA_PER_STEP = 2
WPREP_ROWS = 512
ROUTE_GROUP_LANE = N_EXPERTS
ROUTE_RANK_LANE = N_EXPERTS + 1
VMEM_LIMIT = 58 * 1024 * 1024

BF16 = jnp.bfloat16
F32 = jnp.float32
NT_DIMS = (((1,), (1,)), ((), ()))


def _dot(a, b):
    return jnp.dot(a, b, preferred_element_type=F32)


def _dot_nt(a, b, precision=None):
    return lax.dot_general(a, b, NT_DIMS, preferred_element_type=F32, precision=precision)


def _sigmoid(x):
    return 0.5 * jnp.tanh(0.5 * x) + 0.5


def _sigmoid_of_half(xh):
    return 0.5 * jnp.tanh(xh) + 0.5


def _log_sigmoid(x):
    return jnp.minimum(x, 0.0) - jnp.log1p(jnp.exp(-jnp.abs(x)))


def _split3(x):
    hi = x.astype(BF16).astype(F32)
    r1 = x - hi
    mid = r1.astype(BF16).astype(F32)
    lo = (r1 - mid).astype(BF16).astype(F32)
    return hi, mid, lo


def _ada_kernel(cctx_ref, c_ref, w_ref, b_ref, o_ref):
    n = 1 + c_ref.shape[0]
    c = jnp.concatenate([cctx_ref[...], c_ref[...], jnp.zeros((8 - n, D_MODEL), F32)], axis=0)
    s = (c * _sigmoid(c)).astype(BF16)
    out = _dot(s, w_ref[...].astype(BF16)) + b_ref[...]
    for v in range(ADA_PER_STEP):
        o_ref[v] = out[:, v * D_MODEL:(v + 1) * D_MODEL]


def _ada(c_ctx, c, w_ada, b_ada):
    return pl.pallas_call(
        _ada_kernel,
        grid=(N_ADA // ADA_PER_STEP,),
        in_specs=[
            pl.BlockSpec(c_ctx.shape, lambda j: (0, 0)),
            pl.BlockSpec(c.shape, lambda j: (0, 0)),
            pl.BlockSpec((D_MODEL, ADA_PER_STEP * D_MODEL), lambda j: (0, j)),
            pl.BlockSpec((1, ADA_PER_STEP * D_MODEL), lambda j: (0, j)),
        ],
        out_specs=pl.BlockSpec((ADA_PER_STEP, 8, D_MODEL), lambda j: (j, 0, 0)),
        out_shape=jax.ShapeDtypeStruct((N_ADA, 8, D_MODEL), F32),
        compiler_params=pltpu.CompilerParams(dimension_semantics=("arbitrary",)),
        name="ada",
    )(c_ctx, c, w_ada, b_ada)


def _transpose_cast_kernel(starts_ref, halve_ref, wt_ref, o_ref):
    scale = jnp.where(halve_ref[pl.program_id(0)] == 1, 0.5, 1.0)
    o_ref[...] = (wt_ref[...] * scale).astype(BF16).T


def _transpose_cast(w_t, row_starts, halve):
    n, k = len(row_starts), w_t.shape[1]
    return pl.pallas_call(
        _transpose_cast_kernel,
        grid_spec=pltpu.PrefetchScalarGridSpec(
            num_scalar_prefetch=2, grid=(n,),
            in_specs=[pl.BlockSpec((pl.Element(WPREP_ROWS), pl.Element(k)), lambda j, starts, hv: (starts[j] * 8, 0))],
            out_specs=pl.BlockSpec((k, WPREP_ROWS), lambda j, starts, hv: (0, j)),
        ),
        out_shape=jax.ShapeDtypeStruct((k, n * WPREP_ROWS), BF16),
        compiler_params=pltpu.CompilerParams(dimension_semantics=("arbitrary",)),
        name="transpose_cast",
    )(jnp.array([r // 8 for r in row_starts], jnp.int32), jnp.array([int(h) for h in halve], jnp.int32), w_t)


def _ada_layout_kernel(starts_ref, halve_ref, cctx_ref, c_ref, wada_ref, bada_ref, wt_ref, mod_ref, wrow_ref):
    j = pl.program_id(0)
    scale = jnp.where(halve_ref[j] == 1, 0.5, 1.0)
    wrow_ref[...] = (wt_ref[...] * scale).astype(BF16).T

    @pl.when(j < N_ADA // ADA_PER_STEP)
    def _():
        _ada_kernel(cctx_ref, c_ref, wada_ref, bada_ref, mod_ref)


def _ada_and_layout(c_ctx, c, w_ada, b_ada, w_t, row_starts, halve):
    n, k = len(row_starts), w_t.shape[1]
    n_ada = N_ADA // ADA_PER_STEP
    assert n >= n_ada
    ada_col = lambda j, *_: (0, jnp.minimum(j, n_ada - 1))
    return pl.pallas_call(
        _ada_layout_kernel,
        grid_spec=pltpu.PrefetchScalarGridSpec(
            num_scalar_prefetch=2, grid=(n,),
            in_specs=[
                pl.BlockSpec(c_ctx.shape, lambda j, *_: (0, 0)),
                pl.BlockSpec(c.shape, lambda j, *_: (0, 0)),
                pl.BlockSpec((D_MODEL, ADA_PER_STEP * D_MODEL), ada_col),
                pl.BlockSpec((1, ADA_PER_STEP * D_MODEL), ada_col),
                pl.BlockSpec((pl.Element(WPREP_ROWS), pl.Element(k)), lambda j, starts, hv: (starts[j] * 8, 0)),
            ],
            out_specs=[
                pl.BlockSpec((ADA_PER_STEP, 8, D_MODEL), lambda j, *_: (jnp.minimum(j, n_ada - 1), 0, 0)),
                pl.BlockSpec((k, WPREP_ROWS), lambda j, *_: (0, j)),
            ],
        ),
        out_shape=[jax.ShapeDtypeStruct((N_ADA, 8, D_MODEL), F32),
                   jax.ShapeDtypeStruct((k, n * WPREP_ROWS), BF16)],
        compiler_params=pltpu.CompilerParams(dimension_semantics=("arbitrary",), vmem_limit_bytes=VMEM_LIMIT),
        name="ada_layout",
    )(jnp.array([r // 8 for r in row_starts], jnp.int32), jnp.array([int(h) for h in halve], jnp.int32),
      c_ctx, c, w_ada, b_ada, w_t)


WROW_OFFSET = {"wq": 2 * D_CONV, "wv": 2 * D_CONV + D_MLSTM, "wog": 2 * D_CONV + 2 * D_MLSTM,
               "wgm": 2 * D_CONV + 3 * D_MLSTM}
BROW_K_OFFSET = 2 * D_CONV + 3 * D_MLSTM + 2 * D_MODEL

_MIXER_WEIGHTS = (
    "g1", "wrow", "brow", "wkT", "wgifT", "bgifT", "wdw", "bdw", "lng", "lnb",
    "wco", "hng", "wmo", "wo", "g2", "wrt2", "brtT",
)


def _zero_after(x):
    bits = lax.bitcast_convert_type(x, jnp.uint32)
    bits = lax.shift_right_logical(lax.shift_right_logical(bits, jnp.uint32(16)), jnp.uint32(16))
    return lax.bitcast_convert_type(bits, F32)[0:1, :]


def _conv_block(upad_s, seg, base, cs, wdw_ref, bdw_ref, after=None):
    sub = 8
    first = CONV_PAD - CONV_K // 2
    acc = jnp.broadcast_to(bdw_ref[0:1, cs], (CONV_RB, LANES))
    for r in range(sub):
        z = None
        for a in range((CONV_K + first + sub - 1) // sub):
            j = sub * a + r - first
            if 0 <= j < CONV_K:
                lo = base + sub * a
                tap = wdw_ref[j:j + 1, cs] if after is None else wdw_ref[j:j + 1, cs] + after
                term = tap * upad_s[seg, lo:lo + CONV_RB + sub, cs]
                z = term if z is None else z + term
        acc = acc + z[r:r + CONV_RB, :]
    return acc


def _mixer_kernel(R, T, P, has_state, emit_state, mod_index, *refs):
    L = SUB
    n_mt = R // MIX_TM
    cpm = MIX_TM // L
    n_seq = R // T
    cps = T // L
    nseg = MIX_TM // P
    assert not has_state or n_seq == 1
    it = iter(refs)
    x_ref = next(it)
    mod_ref = next(it)
    if has_state:
        c0_ref = next(it)
        n0_ref = next(it)
        m0_ref = next(it)
    w = {name: next(it) for name in _MIXER_WEIGHTS}
    x1_ref = next(it)
    h2_ref = next(it)
    comb_ref = next(it)
    route_ref = next(it)
    cnt_ref = next(it)
    if emit_state:
        cout_ref = next(it)
        nout_ref = next(it)
        mout_ref = next(it)
    (q_s, kT_s, v_s, so_s, scan_s, ma_s, sgb_s, hm_s, cst_s, upad_s) = [next(it) for _ in range(10)]

    cond_row = mod_index(pl.program_id(0))

    def mod_row(i):
        return mod_ref[i, pl.ds(cond_row, 1), :]

    zpad = jnp.zeros((CONV_PAD, D_CONV), F32)
    for seg in range(nseg):
        upad_s[seg, 0:CONV_PAD, :] = zpad
        upad_s[seg, CONV_PAD + P:CONV_PAD + P + CONV_PAD, :] = zpad

    t_idx = lax.broadcasted_iota(jnp.int32, (L, L), 0)
    s_idx = lax.broadcasted_iota(jnp.int32, (L, L), 1)
    lower = s_idx <= t_idx
    upper = s_idx >= t_idx
    triu_b = upper.astype(F32).astype(BF16)
    lane_u = lax.broadcasted_iota(jnp.int32, (N_UNITS, L), 1)
    is_bwd = lax.broadcasted_iota(jnp.int32, (N_UNITS, L), 0) >= N_HEADS

    def gate_scan(g):
        gi, lf = g[:N_UNITS], _log_sigmoid(g[N_UNITS:])
        pr = _dot(jnp.concatenate(_split3(lf), axis=0).astype(BF16), triu_b)
        pre = pr[0:N_UNITS] + pr[N_UNITS:2 * N_UNITS] + pr[2 * N_UNITS:]
        tot = pre[:, L - 1:L]
        bsum = jnp.where(is_bwd, tot - pre + lf, pre)
        a = gi - bsum
        pm, sm, k = a, a, 1
        while k < L:
            pm = jnp.where(lane_u >= k, jnp.maximum(pm, pltpu.roll(pm, k, axis=1)), pm)
            sm = jnp.where(lane_u < L - k, jnp.maximum(sm, pltpu.roll(sm, L - k, axis=1)), sm)
            k *= 2
        wide = lambda v: jnp.broadcast_to(v, (N_UNITS, L))
        return jnp.concatenate([a, jnp.where(is_bwd, sm, pm), bsum, wide(tot),
                                wide(jnp.max(a, axis=1, keepdims=True))], axis=0)

    def phase1(i, carry):
        r0 = pl.multiple_of(i * MIX_TM, MIX_TM)
        rows = pl.ds(r0, MIX_TM)
        x = x_ref[0, rows, :]
        xn = x * lax.rsqrt(jnp.mean(x * x, axis=-1, keepdims=True) + EPS) * w["g1"][...]
        hb = (xn * (1.0 + mod_row(1)) + mod_row(0)).astype(BF16)

        ag = _dot(hb, w["wrow"][:, :2 * D_CONV]) + w["brow"][:, :2 * D_CONV]
        u = ag[:, :D_CONV] * _sigmoid_of_half(ag[:, D_CONV:])
        for seg in range(nseg):
            upad_s[seg, CONV_PAD:CONV_PAD + P, :] = u[seg * P:(seg + 1) * P, :]
        gates = _dot_nt(w["wgifT"][...].astype(BF16), hb)
        gates = jnp.concatenate([gates[d * 2 * N_HEADS + g * N_HEADS:d * 2 * N_HEADS + (g + 1) * N_HEADS]
                                 for g in range(2) for d in range(2)], axis=0) + w["bgifT"][...]
        for j in range(cpm):
            scan_s[i * cpm + j] = gate_scan(gates[:, j * L:(j + 1) * L])

        def proj(name, c0, width=2 * LANES):
            w0 = WROW_OFFSET[name] + c0
            return _dot(hb, w["wrow"][:, w0:w0 + width]) + w["brow"][:, w0:w0 + width]

        last = lambda z: z[-8:, -LANES:]
        bk_row = w["brow"][:, BROW_K_OFFSET:BROW_K_OFFSET + D_MLSTM]
        bk_col = jnp.concatenate([bk_row, jnp.zeros((LANES - 1, D_MLSTM), F32)], axis=0).T[:, 0:1]

        def gm_a(c0):
            z = proj("wgm", c0)
            ma_s[rows, c0:c0 + 2 * LANES] = _sigmoid_of_half(z)
            return last(z)

        def gm_b(c0):
            z = proj("wgm", D_MODEL + c0)
            sgb_s[rows, c0:c0 + 2 * LANES] = _sigmoid_of_half(z)
            return last(z)

        def q_part(c0):
            z = proj("wq", c0)
            q_s[rows, c0:c0 + 2 * LANES] = (z * (HEAD_DIM ** -0.5)).astype(BF16)
            return last(z)

        def v_part(c0):
            z = proj("wv", c0)
            v_s[rows, c0:c0 + 2 * LANES] = z.astype(BF16)
            return last(z)

        def o_part(c0):
            z = proj("wog", c0)
            so_s[rows, c0:c0 + 2 * LANES] = _sigmoid_of_half(z)
            return last(z)

        def k_part(c0):
            rs = slice(c0, c0 + 2 * LANES)
            z = _dot_nt(w["wkT"][rs, :].astype(BF16), hb) + bk_col[rs, :]
            kt = z.astype(BF16)
            for j in range(cpm):
                kT_s[i * cpm + j, rs, :] = kt[:, j * L:(j + 1) * L]
            return last(z)

        jobs = ([functools.partial(gm_a, c0) for c0 in range(0, D_MODEL, 2 * LANES)]
                + [functools.partial(gm_b, c0) for c0 in range(0, D_MODEL, 2 * LANES)]
                + [functools.partial(f, c0) for f in (q_part, k_part, v_part, o_part)
                   for c0 in range(0, D_MLSTM, 2 * LANES)])
        n_jobs = len(jobs)
        conv = {}
        after, lag = None, [None] * CHAIN_SLACK
        n_pieces = (D_CONV // LANES) * nseg * (P // CONV_RB)
        for cb in range(D_CONV // LANES):
            cs = slice(cb * LANES, (cb + 1) * LANES)
            for seg in range(nseg):
                for rb in range(P // CONV_RB):
                    blk = _conv_block(upad_s, seg, rb * CONV_RB, cs, w["wdw"], w["bdw"], after)
                    conv[(cb, seg, rb)] = blk
                    if jobs and len(conv) * n_jobs >= (n_jobs - len(jobs) + 1) * n_pieces:
                        lag.append(_zero_after(jobs.pop(0)()))
                        after = lag.pop(0)
        for job in jobs:
            job()
        cu = jnp.concatenate(
            [jnp.concatenate([conv[(cb, seg, rb)] for seg in range(nseg) for rb in range(P // CONV_RB)], axis=0)
             for cb in range(D_CONV // LANES)], axis=1)
        mu = jnp.mean(cu, axis=-1, keepdims=True)
        cc = cu - mu
        cn = cc * lax.rsqrt(jnp.mean(cc * cc, axis=-1, keepdims=True) + EPS) * w["lng"][...] + w["lnb"][...]
        ca = (cn * _sigmoid(cn)).astype(BF16)
        ma_s[rows, :] = ma_s[rows, :] * _dot(ca, w["wco"][...])
        return carry

    if n_mt == 1:
        phase1(0, 0)
    else:
        lax.fori_loop(0, n_mt, phase1, 0)

    ones_col = (lax.broadcasted_iota(jnp.int32, (L, HEAD_DIM), 1) == 0).astype(F32).astype(BF16)
    pad_rows = jnp.zeros((LANES - 3 * N_UNITS, L), F32)

    def gate_prep(c, m_vec):
        sc = scan_s[c]
        a, run_max, bsum = sc[0:N_UNITS], sc[N_UNITS:2 * N_UNITS], sc[2 * N_UNITS:3 * N_UNITS]
        tot, a_max = sc[3 * N_UNITS:4 * N_UNITS, 0:1], sc[4 * N_UNITS:5 * N_UNITS, 0:1]
        big_m = jnp.maximum(m_vec, run_max)
        m_end = jnp.maximum(m_vec, a_max)
        cols = jnp.concatenate(
            [big_m, jnp.exp(m_vec - big_m), jnp.exp(-bsum - big_m), pad_rows], axis=0).T
        return a, cols, jnp.exp(a - m_end), jnp.exp(m_vec - m_end), tot + m_end

    def unit_group(dirs, c, prep, first_chunk, want_state):
        a, cols, wk, decay, _ = prep
        rows = slice(c * L, (c + 1) * L)
        heads = range(N_HEADS)
        units = [(d, hd) for d in dirs for hd in heads]
        hs = [slice(hd * HEAD_DIM, (hd + 1) * HEAD_DIM) for hd in heads]
        idx = {u: u[0] * N_HEADS + u[1] for u in units}
        col = lambda k, u: cols[:, k * N_UNITS + idx[u]:k * N_UNITS + idx[u] + 1]
        row = lambda arr, u: arr[idx[u]:idx[u] + 1, :]
        chained = has_state or not first_chunk
        qc = [q_s[rows, hs[hd]] for hd in heads]
        kTc = [kT_s[c, hs[hd], :] for hd in heads]
        vaug = [jnp.concatenate([v_s[rows, hs[hd]], ones_col], axis=1) for hd in heads]
        qk = [_dot(qc[hd], kTc[hd]) for hd in heads]
        s_mat = {u: (qk[u[1]] * jnp.where(lower if u[0] == 0 else upper, jnp.exp(row(a, u) - col(0, u)), 0.0)
                     ).astype(BF16) for u in units}
        nd = {u: _dot(s_mat[u], vaug[u[1]]) for u in units}
        if chained:
            nd = {u: nd[u] + col(1, u) * _dot(qc[u[1]], cst_s[idx[u]].astype(BF16)) for u in units}
        h = {u: nd[u][:, :HEAD_DIM] * (1.0 / jnp.maximum(jnp.abs(nd[u][:, HEAD_DIM:HEAD_DIM + 1]), col(2, u)))
             for u in units}
        for hd in heads:
            total = h[(dirs[0], hd)]
            for d in dirs[1:]:
                total = total + h[(d, hd)]
            if dirs[0] == 0:
                hm_s[rows, hs[hd]] = total
            else:
                hm_s[rows, hs[hd]] = hm_s[rows, hs[hd]] + total
        if want_state:
            kw = {u: (kTc[u[1]].astype(F32) * row(wk, u)).astype(BF16) for u in units}
            upd = {u: _dot(kw[u], vaug[u[1]]) for u in units}
            for u in units:
                cst_s[idx[u]] = (upd[u] + row(decay, u) * cst_s[idx[u]]) if chained else upd[u]

    dir_rows = lax.broadcasted_iota(jnp.int32, (N_UNITS, 1), 0) >= N_HEADS
    for seq in range(n_seq):
        if has_state:
            n_cols = jnp.concatenate([n0_ref[0], jnp.zeros((LANES - N_UNITS, HEAD_DIM), F32)], axis=0).T
            first_lane = lax.broadcasted_iota(jnp.int32, (HEAD_DIM, HEAD_DIM), 1) == 0
            for idx in range(N_UNITS):
                cst_s[idx, :, :HEAD_DIM] = c0_ref[0, idx]
                cst_s[idx, :, HEAD_DIM:] = jnp.where(first_lane, n_cols[:, idx:idx + 1], 0.0)
            unit_row = lax.broadcasted_iota(jnp.int32, (N_UNITS, 1), 0)
            m_vec = jnp.zeros((N_UNITS, 1), F32)
            for idx in range(N_UNITS):
                m_vec = jnp.where(unit_row == idx, m0_ref[pl.program_id(0), idx], m_vec)
        else:
            m_vec = jnp.zeros((N_UNITS, 1), F32)
        if cps == 1:
            prep = gate_prep(seq, m_vec)
            unit_group([0, 1], seq, prep, True, emit_state)
            m_vec = prep[4]
        else:
            for d in range(2):
                order = list(range(cps)) if d == 0 else list(range(cps - 1, -1, -1))
                for pos, c in enumerate(order):
                    prep = gate_prep(seq * cps + c, m_vec)
                    unit_group([d], seq * cps + c, prep, pos == 0, emit_state or pos < cps - 1)
                    m_vec = jnp.where(dir_rows == (d == 1), prep[4], m_vec)
        if emit_state:
            for idx in range(N_UNITS):
                caug = cst_s[idx]
                cout_ref[0, seq * N_UNITS + idx] = caug[:, :HEAD_DIM]
                nout_ref[0, seq * N_UNITS + idx:seq * N_UNITS + idx + 1, :] = caug[:, HEAD_DIM:].T[0:1, :]
            mout_ref[0, seq * N_UNITS:(seq + 1) * N_UNITS, :] = jnp.broadcast_to(m_vec, (N_UNITS, LANES))

    e_iota = lax.broadcasted_iota(jnp.int32, (LANES, MIX_TM), 0)
    g_of_e = lax.shift_right_logical(e_iota, 2)
    j_of_e = lax.bitwise_and(e_iota, EXPERTS_PER_GROUP - 1)
    r8 = lax.broadcasted_iota(jnp.int32, (8, MIX_TM), 0)
    before_b = (lax.broadcasted_iota(jnp.int32, (MOE_BLK, MOE_BLK), 0)
                < lax.broadcasted_iota(jnp.int32, (MOE_BLK, MOE_BLK), 1)).astype(F32).astype(BF16)

    def phase3(i, carry):
        r0 = pl.multiple_of(i * MIX_TM, MIX_TM)
        rows = pl.ds(r0, MIX_TM)
        hm = hm_s[rows, :]
        heads = []
        for hd in range(N_HEADS):
            hh = hm[:, hd * HEAD_DIM:(hd + 1) * HEAD_DIM]
            heads.append(hh * lax.rsqrt(jnp.mean(hh * hh, axis=-1, keepdims=True) + EPS))
        hn = jnp.concatenate(heads, axis=1) * w["hng"][...]
        hb2 = (so_s[rows, :] * hn).astype(BF16)
        br_b = _dot(hb2, w["wmo"][...])
        mixed = (ma_s[rows, :] + sgb_s[rows, :] * br_b).astype(BF16)
        x1 = x_ref[0, rows, :] + mod_row(2) * _dot(mixed, w["wo"][...])
        x1_ref[0, rows, :] = x1
        xn = x1 * lax.rsqrt(jnp.mean(x1 * x1, axis=-1, keepdims=True) + EPS) * w["g2"][...]
        h2 = xn * (1.0 + mod_row(4)) + mod_row(3)
        h2_ref[0, rows, :] = h2.astype(BF16)

        h2_hi = h2.astype(BF16)
        h2_lo = (h2 - h2_hi.astype(F32)).astype(BF16)
        lg = _dot(h2_hi, w["wrt2"][...])
        lg = lg[:, :LANES] + lg[:, LANES:] + _dot(h2_lo, w["wrt2"][:, :LANES])
        lt = lg.T + w["brtT"][...]
        gl = [lt[N_EXPERTS + g:N_EXPERTS + g + 1, :] for g in range(N_GROUPS)]
        best, gsel = gl[0], jnp.zeros((1, MIX_TM), jnp.int32)
        for g in range(1, N_GROUPS):
            better = gl[g] > best
            gsel = jnp.where(better, g, gsel)
            best = jnp.where(better, gl[g], best)
        gp_sel = 1.0 / sum(jnp.exp(v - best) for v in gl)
        el = []
        for j in range(EXPERTS_PER_GROUP):
            v = lt[j:j + 1, :]
            for g in range(1, N_GROUPS):
                r = g * EXPERTS_PER_GROUP + j
                v = jnp.where(gsel == g, lt[r:r + 1, :], v)
            el.append(v)
        l1, e1 = el[0], jnp.zeros((1, MIX_TM), jnp.int32)
        for j in range(1, EXPERTS_PER_GROUP):
            better = el[j] > l1
            e1 = jnp.where(better, j, e1)
            l1 = jnp.where(better, el[j], l1)
        l2 = jnp.full((1, MIX_TM), -jnp.inf, F32)
        e2 = jnp.zeros((1, MIX_TM), jnp.int32)
        for j in range(EXPERTS_PER_GROUP):
            better = jnp.logical_and(e1 != j, el[j] > l2)
            e2 = jnp.where(better, j, e2)
            l2 = jnp.where(better, el[j], l2)
        r2 = jnp.exp(l2 - l1)
        wt1 = gp_sel / (1.0 + r2)
        wt2 = gp_sel * r2 / (1.0 + r2)
        in_group = g_of_e == gsel
        comb_t = (jnp.where(jnp.logical_and(in_group, j_of_e == e1), wt1, 0.0)
                  + jnp.where(jnp.logical_and(in_group, j_of_e == e2), wt2, 0.0))

        onehot = (r8 == gsel).astype(F32)
        gsel_f = gsel.astype(F32)
        rank = jnp.sum(onehot * _dot(onehot.astype(BF16), before_b), axis=0, keepdims=True)
        r8rows = pl.ds(pl.multiple_of(i * 8, 8), 8)
        route_ref[0, r8rows, :] = jnp.where(r8 == 0, gsel_f, jnp.where(r8 == 1, rank, 0.0))
        cnt_ref[0, r8rows, :] = jnp.broadcast_to(jnp.sum(onehot, axis=1, keepdims=True), (8, LANES))
        comb_t = jnp.where(e_iota == ROUTE_GROUP_LANE, gsel_f,
                           jnp.where(e_iota == ROUTE_RANK_LANE, rank, comb_t))
        comb_ref[0, rows, :] = comb_t.T
        return carry

    if n_mt == 1:
        phase3(0, 0)
    else:
        lax.fori_loop(0, n_mt, phase3, 0)


class _RowWindow(NamedTuple):
    array: jax.Array
    start: int
    n: int


def _const_spec(a):
    if isinstance(a, _RowWindow):
        assert a.start % a.n == 0
        return a.array, pl.BlockSpec((a.n, a.array.shape[1]), lambda b: (a.start // a.n, 0),
                                     pipeline_mode=pl.Buffered(1))
    nd = a.ndim
    return a, pl.BlockSpec(a.shape, lambda b, _nd=nd: (0,) * _nd, pipeline_mode=pl.Buffered(1))


def _mixer(x, T, mod, mod_index, weights, P, state=None, emit_state=False):
    B, R, _ = x.shape
    n_chunks = R // SUB
    n_blk = R // MOE_BLK
    n_seq = R // T
    has_state = state is not None
    seq_mode = {} if R <= MIX_TM else {"pipeline_mode": pl.Buffered(1)}
    in_specs = [
        pl.BlockSpec((1, R, D_MODEL), lambda b: (b, 0, 0), **seq_mode),
        pl.BlockSpec(mod.shape, lambda b: (0, 0, 0)),
    ]
    args = [x, mod]
    if has_state:
        c0, n0, m0 = state
        in_specs += [
            pl.BlockSpec((1, N_UNITS, HEAD_DIM, HEAD_DIM), lambda b: (b, 0, 0, 0)),
            pl.BlockSpec((1, N_UNITS, HEAD_DIM), lambda b: (b, 0, 0)),
            pl.BlockSpec(memory_space=pltpu.SMEM),
        ]
        args += [c0, n0, m0]
    for name in _MIXER_WEIGHTS:
        operand, spec = _const_spec(weights[name])
        in_specs.append(spec)
        args.append(operand)
    out_shape = [
        jax.ShapeDtypeStruct((B, R, D_MODEL), F32),
        jax.ShapeDtypeStruct((B, R, D_MODEL), BF16),
        jax.ShapeDtypeStruct((B, R, LANES), F32),
        jax.ShapeDtypeStruct((B, n_blk * 8, MOE_BLK), F32),
        jax.ShapeDtypeStruct((B, n_blk * 8, LANES), F32),
    ]
    out_specs = [
        pl.BlockSpec((1, R, D_MODEL), lambda b: (b, 0, 0), **seq_mode),
        pl.BlockSpec((1, R, D_MODEL), lambda b: (b, 0, 0), **seq_mode),
        pl.BlockSpec((1, R, LANES), lambda b: (b, 0, 0)),
        pl.BlockSpec((1, n_blk * 8, MOE_BLK), lambda b: (b, 0, 0)),
        pl.BlockSpec((1, n_blk * 8, LANES), lambda b: (b, 0, 0)),
    ]
    if emit_state:
        out_shape += [
            jax.ShapeDtypeStruct((B, n_seq * N_UNITS, HEAD_DIM, HEAD_DIM), F32),
            jax.ShapeDtypeStruct((B, n_seq * N_UNITS, HEAD_DIM), F32),
            jax.ShapeDtypeStruct((B, n_seq * N_UNITS, LANES), F32),
        ]
        out_specs += [
            pl.BlockSpec((1, n_seq * N_UNITS, HEAD_DIM, HEAD_DIM), lambda b: (b, 0, 0, 0)),
            pl.BlockSpec((1, n_seq * N_UNITS, HEAD_DIM), lambda b: (b, 0, 0)),
            pl.BlockSpec((1, n_seq * N_UNITS, LANES), lambda b: (b, 0, 0)),
        ]
    scratch = [
        pltpu.VMEM((R, D_MLSTM), BF16),
        pltpu.VMEM((n_chunks, D_MLSTM, SUB), BF16),
        pltpu.VMEM((R, D_MLSTM), BF16),
        pltpu.VMEM((R, D_MLSTM), F32),
        pltpu.VMEM((n_chunks, 5 * N_UNITS, SUB), F32),
        pltpu.VMEM((R, D_MODEL), F32),
        pltpu.VMEM((R, D_MODEL), F32),
        pltpu.VMEM((R, D_MLSTM), F32),
        pltpu.VMEM((N_UNITS, HEAD_DIM, 2 * HEAD_DIM), F32),
        pltpu.VMEM((MIX_TM // P, P + 2 * CONV_PAD, D_CONV), F32),
    ]
    return pl.pallas_call(
        functools.partial(_mixer_kernel, R, T, P, has_state, emit_state, mod_index),
        grid=(B,),
        in_specs=in_specs,
        out_specs=out_specs,
        out_shape=out_shape,
        scratch_shapes=scratch,
        compiler_params=pltpu.CompilerParams(
            dimension_semantics=("arbitrary",), vmem_limit_bytes=VMEM_LIMIT),
        name="mixer_T%d" % T,
    )(*args)


def _dest_in_block(group, rank, starts):
    dest = rank
    for g in range(N_GROUPS):
        dest = dest + jnp.where(group == float(g), starts[g], 0.0)
    return dest


def _copy_segments(src_refs, dst_refs, src_starts, dst_starts, n_pieces):
    def copy(g, first_piece, n_rows):
        s = pl.multiple_of(src_starts[g] + first_piece * ROW_ALIGN, ROW_ALIGN)
        d = pl.multiple_of(dst_starts[g] + first_piece * ROW_ALIGN, ROW_ALIGN)
        for src, dst in zip(src_refs, dst_refs):
            dst[pl.ds(d, n_rows), :] = src[pl.ds(s, n_rows), :]

    for g in range(N_GROUPS):
        n_runs = lax.shift_right_logical(n_pieces[g], COPY_RUN.bit_length() - 1)

        def run(k, carry, g=g):
            copy(g, k * COPY_RUN, COPY_RUN * ROW_ALIGN)
            return carry

        def single(k, carry, g=g):
            copy(g, k, ROW_ALIGN)
            return carry

        lax.fori_loop(0, n_runs, run, 0)
        lax.fori_loop(n_runs * COPY_RUN, n_pieces[g], single, 0)


def _plan_segments(n_blocks, n_tiles, count, start_ref, npiece_ref, off_ref, tgroup_ref, tvalid_ref):
    align_shift = ROW_ALIGN.bit_length() - 1
    tile_shift = MOE_TM.bit_length() - 1

    def block_starts(blk, carry):
        row = jnp.int32(0)
        for g in range(N_GROUPS):
            n = lax.shift_right_logical(count(blk, g) + (ROW_ALIGN - 1), align_shift)
            npiece_ref[blk * N_GROUPS + g] = n
            start_ref[blk * N_GROUPS + g] = row
            row = row + n * ROW_ALIGN
        return carry

    lax.fori_loop(0, n_blocks, block_starts, 0)

    base_row = jnp.int32(0)
    base_tile = jnp.int32(0)
    last_group = jnp.int32(0)
    for g in range(N_GROUPS):
        def seg_offsets(blk, row, g=g, base_row=base_row):
            off_ref[blk * N_GROUPS + g] = base_row + row
            return row + npiece_ref[blk * N_GROUPS + g] * ROW_ALIGN

        rows = lax.fori_loop(0, n_blocks, seg_offsets, jnp.int32(0))
        tiles = lax.shift_right_logical(rows + (MOE_TM - 1), tile_shift)

        def mark_tiles(t, carry, g=g, base_tile=base_tile):
            tgroup_ref[base_tile + t] = g
            tvalid_ref[base_tile + t] = 1
            return carry

        lax.fori_loop(0, tiles, mark_tiles, 0)
        last_group = jnp.where(tiles > 0, g, last_group)
        base_row = base_row + tiles * MOE_TM
        base_tile = base_tile + tiles

    def mark_unused(t, carry):
        tgroup_ref[t] = last_group
        tvalid_ref[t] = 0
        return carry

    lax.fori_loop(base_tile, n_tiles, mark_unused, 0)


def _dispatch_kernel(n_ctx_blocks, n_blocks, n_tiles,
                     h2c_ref, h2l_ref, cbc_ref, cbl_ref, rtc_ref, rtl_ref, cntc_ref, cntl_ref,
                     xs_ref, cs_ref, start_ref, npiece_ref, off_ref, tgroup_ref, tvalid_ref,
                     sx_s, sc_s):
    b = pl.program_id(0)
    is_ctx = b < n_ctx_blocks

    def count(blk, g):
        vc = cntc_ref[jnp.minimum(blk, n_ctx_blocks - 1), pl.ds(g, 1), pl.ds(0, 1)]
        vl = cntl_ref[jnp.maximum(blk - n_ctx_blocks, 0), pl.ds(g, 1), pl.ds(0, 1)]
        return jnp.where(blk < n_ctx_blocks, vc, vl)[0, 0].astype(jnp.int32)

    @pl.when(b == 0)
    def _():
        _plan_segments(n_blocks, n_tiles, count, start_ref, npiece_ref, off_ref, tgroup_ref, tvalid_ref)
        xs_ref[...] = jnp.zeros_like(xs_ref)
        cs_ref[...] = jnp.zeros_like(cs_ref)

    starts = [start_ref[b * N_GROUPS + g] for g in range(N_GROUPS)]

    def sort_block(h2_ref, cb_ref, rt_ref):
        h2 = h2_ref[0]
        cb = cb_ref[0]
        rt = rt_ref[0]
        dest = _dest_in_block(rt[0:1, :], rt[1:2, :], [s.astype(F32) for s in starts])
        row = lax.broadcasted_iota(jnp.int32, (SORT_ROWS, MOE_BLK), 0).astype(F32)
        perm = (row == dest).astype(F32).astype(BF16)
        cb_hi = cb.astype(BF16)
        cb_lo = (cb - cb_hi.astype(F32)).astype(BF16)
        srt = _dot(perm, jnp.concatenate([h2, cb_hi, cb_lo], axis=1)).astype(BF16)
        sx_s[...] = srt[:, :D_MODEL]
        sc_s[...] = srt[:, D_MODEL:]

    pl.when(is_ctx)(functools.partial(sort_block, h2c_ref, cbc_ref, rtc_ref))
    pl.when(jnp.logical_not(is_ctx))(functools.partial(sort_block, h2l_ref, cbl_ref, rtl_ref))
    _copy_segments((sx_s, sc_s), (xs_ref, cs_ref), starts,
                   [off_ref[b * N_GROUPS + g] for g in range(N_GROUPS)],
                   [npiece_ref[b * N_GROUPS + g] for g in range(N_GROUPS)])


def _experts_kernel(tgroup_ref, tvalid_ref, xs_ref, cs_ref, wg_ref, wu_ref, wd_ref, ys_ref):
    i = pl.program_id(0)

    @pl.when(tvalid_ref[i] == 1)
    def _():
        x = xs_ref[...]
        comb = cs_ref[:, :LANES].astype(F32) + cs_ref[:, LANES:].astype(F32)
        lane = lax.broadcasted_iota(jnp.int32, comb.shape, 1)
        first = tgroup_ref[i] * EXPERTS_PER_GROUP
        acc = None
        for j in range(EXPERTS_PER_GROUP):
            gj = _dot(x, wg_ref[j].astype(BF16))
            uj = _dot(x, wu_ref[j].astype(BF16))
            cw = jnp.sum(jnp.where(lane == first + j, comb, 0.0), axis=1, keepdims=True)
            out = _dot((gj * _sigmoid(gj) * uj * cw).astype(BF16), wd_ref[j].astype(BF16))
            acc = out if acc is None else acc + out
        ys_ref[...] = acc.astype(BF16)

    @pl.when(tvalid_ref[i] == 0)
    def _():
        ys_ref[...] = jnp.zeros_like(ys_ref)


def _combine_kernel(n_ctx_blocks, blocks_per_lat_seq, start_ref, npiece_ref, off_ref,
                    x1c_ref, x1l_ref, cbc_ref, cbl_ref, ys_ref, mod_ref, gf_ref, yc_ref, yl_ref, loc_s):
    b = pl.program_id(0)
    is_ctx = b < n_ctx_blocks
    starts = [start_ref[b * N_GROUPS + g] for g in range(N_GROUPS)]
    @pl.when(b == 0)
    def _():
        loc_s[...] = jnp.zeros_like(loc_s)

    _copy_segments((ys_ref,), (loc_s,), [off_ref[b * N_GROUPS + g] for g in range(N_GROUPS)], starts,
                   [npiece_ref[b * N_GROUPS + g] for g in range(N_GROUPS)])
    def finish_block(x1_ref, cb_ref, y_ref, mrow):
        cb = cb_ref[0]
        dest = _dest_in_block(cb[:, ROUTE_GROUP_LANE:ROUTE_GROUP_LANE + 1],
                              cb[:, ROUTE_RANK_LANE:ROUTE_RANK_LANE + 1],
                              [s.astype(F32) for s in starts])
        col = lax.broadcasted_iota(jnp.int32, (MOE_BLK, SORT_ROWS), 1).astype(F32)
        unperm = (col == dest).astype(F32).astype(BF16)
        x2 = x1_ref[0] + mod_ref[N_ADA - 1, pl.ds(mrow, 1), :] * _dot(unperm, loc_s[...])
        y_ref[0] = x2 * lax.rsqrt(jnp.mean(x2 * x2, axis=-1, keepdims=True) + EPS) * gf_ref[...]

    lat_row = 1 + jnp.maximum(b - n_ctx_blocks, 0) // blocks_per_lat_seq
    pl.when(is_ctx)(functools.partial(finish_block, x1c_ref, cbc_ref, yc_ref, 0))
    pl.when(jnp.logical_not(is_ctx))(functools.partial(finish_block, x1l_ref, cbl_ref, yl_ref, lat_row))


def _moe(x1c, x1l, h2c, h2l, cbc, cbl, rtc, rtl, cntc, cntl, mod, blocks_per_lat_seq, wg, wu, wd, gf):
    nc, nl = x1c.shape[0], x1l.shape[0]
    nb = nc + nl
    n_rows_max = nb * MOE_BLK + nb * N_GROUPS * (ROW_ALIGN - 1) + N_GROUPS * (MOE_TM - ROW_ALIGN)
    n_tiles = -(-n_rows_max // MOE_TM)
    ns = n_tiles * MOE_TM

    cmap = lambda b, *_: (jnp.minimum(b, nc - 1), 0, 0)
    lmap = lambda b, *_: (jnp.maximum(b - nc, 0), 0, 0)
    whole = lambda *_: (0, 0)
    once = {"pipeline_mode": pl.Buffered(1)}
    arb = pltpu.CompilerParams(dimension_semantics=("arbitrary",), vmem_limit_bytes=VMEM_LIMIT)
    smem = pl.BlockSpec(memory_space=pltpu.SMEM)
    seg_i32 = jax.ShapeDtypeStruct((nb * N_GROUPS,), jnp.int32)
    tile_i32 = jax.ShapeDtypeStruct((n_tiles,), jnp.int32)

    xs, cs, start, npiece, off, tgroup, tvalid = pl.pallas_call(
        functools.partial(_dispatch_kernel, nc, nb, n_tiles),
        grid_spec=pltpu.PrefetchScalarGridSpec(
            num_scalar_prefetch=0, grid=(nb,),
            in_specs=[
                pl.BlockSpec((1, MOE_BLK, D_MODEL), cmap), pl.BlockSpec((1, MOE_BLK, D_MODEL), lmap),
                pl.BlockSpec((1, MOE_BLK, LANES), cmap), pl.BlockSpec((1, MOE_BLK, LANES), lmap),
                pl.BlockSpec((1, 8, MOE_BLK), cmap), pl.BlockSpec((1, 8, MOE_BLK), lmap),
                pl.BlockSpec(cntc.shape, lambda b: (0, 0, 0)), pl.BlockSpec(cntl.shape, lambda b: (0, 0, 0)),
            ],
            out_specs=[pl.BlockSpec((ns, D_MODEL), whole, **once), pl.BlockSpec((ns, 2 * LANES), whole, **once),
                       smem, smem, smem, smem, smem],
            scratch_shapes=[pltpu.VMEM((SORT_ROWS, D_MODEL), BF16), pltpu.VMEM((SORT_ROWS, 2 * LANES), BF16)],
        ),
        out_shape=[jax.ShapeDtypeStruct((ns, D_MODEL), BF16), jax.ShapeDtypeStruct((ns, 2 * LANES), BF16),
                   seg_i32, seg_i32, seg_i32, tile_i32, tile_i32],
        compiler_params=arb,
        name="moe_dispatch",
    )(h2c, h2l, cbc, cbl, rtc, rtl, cntc, cntl)

    wmap = lambda i, tg, tv: (tg[i], 0, 0)
    ys = pl.pallas_call(
        _experts_kernel,
        grid_spec=pltpu.PrefetchScalarGridSpec(
            num_scalar_prefetch=2, grid=(n_tiles,),
            in_specs=[
                pl.BlockSpec((MOE_TM, D_MODEL), lambda i, *_: (i, 0)),
                pl.BlockSpec((MOE_TM, 2 * LANES), lambda i, *_: (i, 0)),
                pl.BlockSpec((EXPERTS_PER_GROUP, D_MODEL, D_EXPERT), wmap),
                pl.BlockSpec((EXPERTS_PER_GROUP, D_MODEL, D_EXPERT), wmap),
                pl.BlockSpec((EXPERTS_PER_GROUP, D_EXPERT, D_MODEL), wmap),
            ],
            out_specs=pl.BlockSpec((MOE_TM, D_MODEL), lambda i, *_: (i, 0)),
        ),
        out_shape=jax.ShapeDtypeStruct((ns, D_MODEL), BF16),
        compiler_params=arb,
        name="moe_experts",
    )(tgroup, tvalid, xs, cs, wg, wu, wd)

    yc, yl = pl.pallas_call(
        functools.partial(_combine_kernel, nc, blocks_per_lat_seq),
        grid_spec=pltpu.PrefetchScalarGridSpec(
            num_scalar_prefetch=3, grid=(nb,),
            in_specs=[
                pl.BlockSpec((1, MOE_BLK, D_MODEL), cmap), pl.BlockSpec((1, MOE_BLK, D_MODEL), lmap),
                pl.BlockSpec((1, MOE_BLK, LANES), cmap), pl.BlockSpec((1, MOE_BLK, LANES), lmap),
                pl.BlockSpec((ns, D_MODEL), whole, **once),
                pl.BlockSpec(mod.shape, lambda *_: (0, 0, 0)),
                pl.BlockSpec((1, D_MODEL), whole),
            ],
            out_specs=[pl.BlockSpec((1, MOE_BLK, D_MODEL), cmap), pl.BlockSpec((1, MOE_BLK, D_MODEL), lmap)],
            scratch_shapes=[pltpu.VMEM((SORT_ROWS, D_MODEL), BF16)],
        ),
        out_shape=[jax.ShapeDtypeStruct((nc, MOE_BLK, D_MODEL), F32),
                   jax.ShapeDtypeStruct((nl, MOE_BLK, D_MODEL), F32)],
        compiler_params=arb,
        name="moe_combine",
    )(start, npiece, off, x1c, x1l, cbc, cbl, ys, mod, gf)
    return yc, yl


def _prep_weights(norm1_g, w_in, b_in, b_gates, w_dw, b_dw, conv_ln_g, conv_ln_b, w_conv_out,
                  mlstm_hn_g, w_mlstm_out, w_o, norm2_g, w_rg, b_rg, w_re, b_re, ada_args):
    s_a = 2 * D_CONV
    s_q = s_a + D_MLSTM
    s_k = s_q + D_MLSTM
    s_v = s_k + D_MLSTM
    s_o = s_v + D_MLSTM
    s_g = s_o + 4 * N_HEADS
    row = lambda v: v.reshape(1, -1).astype(F32)
    w_t = w_in.T
    keep = [(0, s_q), (s_k, s_o), (s_g, w_in.shape[1])]
    halved = [(D_CONV, s_a), (s_v, s_o), (s_g, w_in.shape[1])]
    is_halved = lambda r: any(a <= r < b for a, b in halved)
    blocks = [r for a, b in keep for r in range(a, b, WPREP_ROWS)]
    mod, wrow = _ada_and_layout(*ada_args, w_t, blocks, [is_halved(r) for r in blocks])
    bias_scale = jnp.array([0.5 if is_halved(r) else 1.0 for a, b in keep for r in range(a, b)]
                           + [1.0] * D_MLSTM, F32)
    bg = (b_in[s_o:s_g] + b_gates.reshape(-1)).reshape(2, 2, N_HEADS).transpose(1, 0, 2).reshape(-1, 1)
    row_window = lambda start, n: _RowWindow(w_t, start, n)
    n_rt = N_EXPERTS + N_GROUPS
    wrt = jnp.pad(jnp.concatenate([w_re, w_rg], axis=1), ((0, 0), (0, LANES - n_rt)))
    wrt_hi = wrt.astype(BF16)
    wrt2 = jnp.concatenate([wrt_hi, (wrt - wrt_hi.astype(F32)).astype(BF16)], axis=1)
    brtT = jnp.pad(jnp.concatenate([b_re, b_rg]), (0, LANES - n_rt)).reshape(LANES, 1)
    return {
        "g1": row(norm1_g),
        "wrow": wrow, "brow": row(jnp.concatenate([b_in[a:b] for a, b in keep] + [b_in[s_q:s_k]]) * bias_scale),
        "wkT": row_window(s_q, D_MLSTM), "wgifT": row_window(s_o, 4 * N_HEADS), "bgifT": bg,
        "wdw": w_dw.astype(F32), "bdw": row(b_dw), "lng": row(conv_ln_g), "lnb": row(conv_ln_b),
        "wco": w_conv_out.astype(BF16), "hng": row(mlstm_hn_g), "wmo": w_mlstm_out.astype(BF16),
        "wo": w_o.astype(BF16), "g2": row(norm2_g), "wrt2": wrt2, "brtT": brtT,
    }, mod


def kernel(x_prompt, x_sample, state_C, state_n, state_m, c, c_ctx, norm1_g, w_ada, b_ada, w_in, b_in, b_gates, w_dw, b_dw, conv_ln_g, conv_ln_b, w_conv_out, mlstm_hn_g, w_mlstm_out, w_o, norm2_g, w_rg, b_rg, w_re, b_re, w_e_gate, w_e_up, w_e_down, norm_final_g):
    B, S, _ = x_prompt.shape
    Bd, Sd, _ = x_sample.shape
    assert w_ada.shape[0] == 1, "single trunk layer"
    assert MIX_TM % S == 0 and S % SUB == 0 and Sd % MIX_TM == 0

    wts, mod = _prep_weights(norm1_g[0], w_in[0], b_in[0], b_gates[0], w_dw[0], b_dw[0], conv_ln_g[0],
                             conv_ln_b[0], w_conv_out[0], mlstm_hn_g[0], w_mlstm_out[0], w_o[0],
                             norm2_g[0], w_rg[0], b_rg[0], w_re[0], b_re[0],
                             (c_ctx.reshape(1, -1), c, w_ada[0], b_ada[0].reshape(1, -1)))

    x1p, h2p, cbp, rtp, cntp, c_new, n_new, m_new = _mixer(
        x_prompt.reshape(B * S // MIX_TM, MIX_TM, D_MODEL), S, mod, lambda b: 0, wts, P=S, emit_state=True)

    state = (state_C[:, 0].reshape(Bd, N_UNITS, HEAD_DIM, HEAD_DIM), state_n[:, 0].reshape(Bd, N_UNITS, HEAD_DIM),
             state_m[:, 0].reshape(Bd, N_UNITS))
    x1s, h2s, cbs, rts, cnts = _mixer(x_sample, Sd, mod, lambda b: 1 + b, wts, P=GRID_W, state=state)

    nc, nl = B * S // MOE_BLK, Bd * Sd // MOE_BLK
    blk = lambda a, n: a.reshape(n, MOE_BLK, a.shape[-1])
    yp, ys = _moe(blk(x1p, nc), blk(x1s, nl), blk(h2p, nc), blk(h2s, nl), blk(cbp, nc), blk(cbs, nl),
                  rtp.reshape(nc, 8, MOE_BLK), rts.reshape(nl, 8, MOE_BLK),
                  cntp.reshape(nc, 8, LANES), cnts.reshape(nl, 8, LANES),
                  mod, Sd // MOE_BLK, w_e_gate[0], w_e_up[0], w_e_down[0], norm_final_g.reshape(1, -1))

    return (yp.reshape(B, S, D_MODEL), ys.reshape(Bd, Sd, D_MODEL),
            c_new.reshape(B, 1, 2, N_HEADS, HEAD_DIM, HEAD_DIM),
            n_new.reshape(B, 1, 2, N_HEADS, HEAD_DIM),
            m_new[:, :, 0].reshape(B, 1, 2, N_HEADS))
```

```python
import functools
from typing import NamedTuple

import jax
import jax.numpy as jnp
from jax import lax
from jax.experimental import pallas as pl
from jax.experimental.pallas import tpu as pltpu

D_MODEL = 1024
D_CONV = 512
CONV_K = 31
D_MLSTM = 512
N_HEADS = 4
HEAD_DIM = D_MLSTM // N_HEADS
N_GROUPS = 4
EXPERTS_PER_GROUP = 4
N_EXPERTS = N_GROUPS * EXPERTS_PER_GROUP
D_EXPERT = 256
N_ADA = 6
EPS = 1e-6
GRID_W = 64

LANES = 128
SUB = 256
CONV_PAD = 16
CONV_RB = 64
N_UNITS = 2 * N_HEADS
ROW_ALIGN = 16
CHAIN_SLACK = 1
COPY_RUN = 4
MOE_TM = 512
MIX_TM = 512
MOE_BLK = MIX_TM
SORT_ROWS = MOE_BLK + N_GROUPS * ROW_ALIGN
ADA_PER_STEP = 2
WPREP_ROWS = 512
ROUTE_GROUP_LANE = N_EXPERTS
ROUTE_RANK_LANE = N_EXPERTS + 1
VMEM_LIMIT = 58 * 1024 * 1024

BF16 = jnp.bfloat16
F32 = jnp.float32
NT_DIMS = (((1,), (1,)), ((), ()))


def _dot(a, b):
    return jnp.dot(a, b, preferred_element_type=F32)


def _dot_nt(a, b, precision=None):
    return lax.dot_general(a, b, NT_DIMS, preferred_element_type=F32, precision=precision)


def _sigmoid(x):
    return 0.5 * jnp.tanh(0.5 * x) + 0.5


def _sigmoid_of_half(xh):
    return 0.5 * jnp.tanh(xh) + 0.5


def _log_sigmoid(x):
    return jnp.minimum(x, 0.0) - jnp.log1p(jnp.exp(-jnp.abs(x)))


def _split3(x):
    hi = x.astype(BF16).astype(F32)
    r1 = x - hi
    mid = r1.astype(BF16).astype(F32)
    lo = (r1 - mid).astype(BF16).astype(F32)
    return hi, mid, lo


def _ada_kernel(cctx_ref, c_ref, w_ref, b_ref, o_ref):
    n = 1 + c_ref.shape[0]
    c = jnp.concatenate([cctx_ref[...], c_ref[...], jnp.zeros((8 - n, D_MODEL), F32)], axis=0)
    s = (c * _sigmoid(c)).astype(BF16)
    out = _dot(s, w_ref[...].astype(BF16)) + b_ref[...]
    for v in range(ADA_PER_STEP):
        o_ref[v] = out[:, v * D_MODEL:(v + 1) * D_MODEL]


def _ada_layout_kernel(starts_ref, halve_ref, cctx_ref, c_ref, wada_ref, bada_ref, wt_ref, mod_ref, wrow_ref):
    j = pl.program_id(0)
    scale = jnp.where(halve_ref[j] == 1, 0.5, 1.0)
    wrow_ref[...] = (wt_ref[...] * scale).astype(BF16).T

    @pl.when(j < N_ADA // ADA_PER_STEP)
    def _():
        _ada_kernel(cctx_ref, c_ref, wada_ref, bada_ref, mod_ref)


def _ada_and_layout(c_ctx, c, w_ada, b_ada, w_t, row_starts, halve):
    n, k = len(row_starts), w_t.shape[1]
    n_ada = N_ADA // ADA_PER_STEP
    assert n >= n_ada
    ada_col = lambda j, *_: (0, jnp.minimum(j, n_ada - 1))
    return pl.pallas_call(
        _ada_layout_kernel,
        grid_spec=pltpu.PrefetchScalarGridSpec(
            num_scalar_prefetch=2, grid=(n,),
            in_specs=[
                pl.BlockSpec(c_ctx.shape, lambda j, *_: (0, 0)),
                pl.BlockSpec(c.shape, lambda j, *_: (0, 0)),
                pl.BlockSpec((D_MODEL, ADA_PER_STEP * D_MODEL), ada_col),
                pl.BlockSpec((1, ADA_PER_STEP * D_MODEL), ada_col),
                pl.BlockSpec((pl.Element(WPREP_ROWS), pl.Element(k)), lambda j, starts, hv: (starts[j] * 8, 0)),
            ],
            out_specs=[
                pl.BlockSpec((ADA_PER_STEP, 8, D_MODEL), lambda j, *_: (jnp.minimum(j, n_ada - 1), 0, 0)),
                pl.BlockSpec((k, WPREP_ROWS), lambda j, *_: (0, j)),
            ],
        ),
        out_shape=[jax.ShapeDtypeStruct((N_ADA, 8, D_MODEL), F32),
                   jax.ShapeDtypeStruct((k, n * WPREP_ROWS), BF16)],
        compiler_params=pltpu.CompilerParams(dimension_semantics=("arbitrary",), vmem_limit_bytes=VMEM_LIMIT),
        name="ada_layout",
    )(jnp.array([r // 8 for r in row_starts], jnp.int32), jnp.array([int(h) for h in halve], jnp.int32),
      c_ctx, c, w_ada, b_ada, w_t)


WROW_OFFSET = {"wq": 2 * D_CONV, "wv": 2 * D_CONV + D_MLSTM, "wog": 2 * D_CONV + 2 * D_MLSTM,
               "wgm": 2 * D_CONV + 3 * D_MLSTM}
BROW_K_OFFSET = 2 * D_CONV + 3 * D_MLSTM + 2 * D_MODEL

_MIXER_WEIGHTS = (
    "g1", "wrow", "brow", "wkT", "wgifT", "bgifT", "wdw", "bdw", "lng", "lnb",
    "wco", "hng", "wmo", "wo", "g2", "wrt2", "brtT",
)


def _zero_after(x):
    bits = lax.bitcast_convert_type(x, jnp.uint32)
    bits = lax.shift_right_logical(lax.shift_right_logical(bits, jnp.uint32(16)), jnp.uint32(16))
    return lax.bitcast_convert_type(bits, F32)[0:1, :]


def _conv_block(upad_s, seg, base, cs, wdw_ref, bdw_ref, after=None):
    sub = 8
    first = CONV_PAD - CONV_K // 2
    acc = jnp.broadcast_to(bdw_ref[0:1, cs], (CONV_RB, LANES))
    for r in range(sub):
        z = None
        for a in range((CONV_K + first + sub - 1) // sub):
            j = sub * a + r - first
            if 0 <= j < CONV_K:
                lo = base + sub * a
                tap = wdw_ref[j:j + 1, cs] if after is None else wdw_ref[j:j + 1, cs] + after
                term = tap * upad_s[seg, lo:lo + CONV_RB + sub, cs]
                z = term if z is None else z + term
        acc = acc + z[r:r + CONV_RB, :]
    return acc


def _mixer_kernel(R, T, P, has_state, emit_state, mod_index, *refs):
    L = SUB
    n_mt = R // MIX_TM
    cpm = MIX_TM // L
    n_seq = R // T
    cps = T // L
    nseg = MIX_TM // P
    assert not has_state or n_seq == 1
    it = iter(refs)
    x_ref = next(it)
    mod_ref = next(it)
    if has_state:
        c0_ref = next(it)
        n0_ref = next(it)
        m0_ref = next(it)
    w = {name: next(it) for name in _MIXER_WEIGHTS}
    x1_ref = next(it)
    h2_ref = next(it)
    comb_ref = next(it)
    route_ref = next(it)
    cnt_ref = next(it)
    if emit_state:
        cout_ref = next(it)
        nout_ref = next(it)
        mout_ref = next(it)
    (q_s, kT_s, v_s, so_s, scan_s, ma_s, sgb_s, hm_s, cst_s, upad_s) = [next(it) for _ in range(10)]

    cond_row = mod_index(pl.program_id(0))

    def mod_row(i):
        return mod_ref[i, pl.ds(cond_row, 1), :]

    zpad = jnp.zeros((CONV_PAD, D_CONV), F32)
    for seg in range(nseg):
        upad_s[seg, 0:CONV_PAD, :] = zpad
        upad_s[seg, CONV_PAD + P:CONV_PAD + P + CONV_PAD, :] = zpad

    t_idx = lax.broadcasted_iota(jnp.int32, (L, L), 0)
    s_idx = lax.broadcasted_iota(jnp.int32, (L, L), 1)
    lower = s_idx <= t_idx
    upper = s_idx >= t_idx
    triu_b = upper.astype(F32).astype(BF16)
    lane_u = lax.broadcasted_iota(jnp.int32, (N_UNITS, L), 1)
    is_bwd = lax.broadcasted_iota(jnp.int32, (N_UNITS, L), 0) >= N_HEADS

    def gate_scan(g):
        gi, lf = g[:N_UNITS], _log_sigmoid(g[N_UNITS:])
        pr = _dot(jnp.concatenate(_split3(lf), axis=0).astype(BF16), triu_b)
        pre = pr[0:N_UNITS] + pr[N_UNITS:2 * N_UNITS] + pr[2 * N_UNITS:]
        tot = pre[:, L - 1:L]
        bsum = jnp.where(is_bwd, tot - pre + lf, pre)
        a = gi - bsum
        pm, sm, k = a, a, 1
        while k < L:
            pm = jnp.where(lane_u >= k, jnp.maximum(pm, pltpu.roll(pm, k, axis=1)), pm)
            sm = jnp.where(lane_u < L - k, jnp.maximum(sm, pltpu.roll(sm, L - k, axis=1)), sm)
            k *= 2
        wide = lambda v: jnp.broadcast_to(v, (N_UNITS, L))
        return jnp.concatenate([a, jnp.where(is_bwd, sm, pm), bsum, wide(tot),
                                wide(jnp.max(a, axis=1, keepdims=True))], axis=0)

    def phase1(i, carry):
        r0 = pl.multiple_of(i * MIX_TM, MIX_TM)
        rows = pl.ds(r0, MIX_TM)
        x = x_ref[0, rows, :]
        xn = x * lax.rsqrt(jnp.mean(x * x, axis=-1, keepdims=True) + EPS) * w["g1"][...]
        hb = (xn * (1.0 + mod_row(1)) + mod_row(0)).astype(BF16)

        ag = _dot(hb, w["wrow"][:, :2 * D_CONV]) + w["brow"][:, :2 * D_CONV]
        u = ag[:, :D_CONV] * _sigmoid_of_half(ag[:, D_CONV:])
        for seg in range(nseg):
            upad_s[seg, CONV_PAD:CONV_PAD + P, :] = u[seg * P:(seg + 1) * P, :]
        gates = _dot_nt(w["wgifT"][...].astype(BF16), hb)
        gates = jnp.concatenate([gates[d * 2 * N_HEADS + g * N_HEADS:d * 2 * N_HEADS + (g + 1) * N_HEADS]
                                 for g in range(2) for d in range(2)], axis=0) + w["bgifT"][...]
        for j in range(cpm):
            scan_s[i * cpm + j] = gate_scan(gates[:, j * L:(j + 1) * L])

        def proj(name, c0, width=2 * LANES):
            w0 = WROW_OFFSET[name] + c0
            return _dot(hb, w["wrow"][:, w0:w0 + width]) + w["brow"][:, w0:w0 + width]

        last = lambda z: z[-8:, -LANES:]
        bk_row = w["brow"][:, BROW_K_OFFSET:BROW_K_OFFSET + D_MLSTM]
        bk_col = jnp.concatenate([bk_row, jnp.zeros((LANES - 1, D_MLSTM), F32)], axis=0).T[:, 0:1]

        def gm_a(c0):
            z = proj("wgm", c0)
            ma_s[rows, c0:c0 + 2 * LANES] = _sigmoid_of_half(z)
            return last(z)

        def gm_b(c0):
            z = proj("wgm", D_MODEL + c0)
            sgb_s[rows, c0:c0 + 2 * LANES] = _sigmoid_of_half(z)
            return last(z)

        def q_part(c0):
            z = proj("wq", c0)
            q_s[rows, c0:c0 + 2 * LANES] = (z * (HEAD_DIM ** -0.5)).astype(BF16)
            return last(z)

        def v_part(c0):
            z = proj("wv", c0)
            v_s[rows, c0:c0 + 2 * LANES] = z.astype(BF16)
            return last(z)

        def o_part(c0):
            z = proj("wog", c0)
            so_s[rows, c0:c0 + 2 * LANES] = _sigmoid_of_half(z)
            return last(z)

        def k_part(c0):
            rs = slice(c0, c0 + 2 * LANES)
            z = _dot_nt(w["wkT"][rs, :].astype(BF16), hb) + bk_col[rs, :]
            kt = z.astype(BF16)
            for j in range(cpm):
                kT_s[i * cpm + j, rs, :] = kt[:, j * L:(j + 1) * L]
            return last(z)

        jobs = ([functools.partial(gm_a, c0) for c0 in range(0, D_MODEL, 2 * LANES)]
                + [functools.partial(gm_b, c0) for c0 in range(0, D_MODEL, 2 * LANES)]
                + [functools.partial(f, c0) for f in (q_part, k_part, v_part, o_part)
                   for c0 in range(0, D_MLSTM, 2 * LANES)])
        n_jobs = len(jobs)
        conv = {}
        after, lag = None, [None] * CHAIN_SLACK
        n_pieces = (D_CONV // LANES) * nseg * (P // CONV_RB)
        for cb in range(D_CONV // LANES):
            cs = slice(cb * LANES, (cb + 1) * LANES)
            for seg in range(nseg):
                for rb in range(P // CONV_RB):
                    blk = _conv_block(upad_s, seg, rb * CONV_RB, cs, w["wdw"], w["bdw"], after)
                    conv[(cb, seg, rb)] = blk
                    if jobs and len(conv) * n_jobs >= (n_jobs - len(jobs) + 1) * n_pieces:
                        lag.append(_zero_after(jobs.pop(0)()))
                        after = lag.pop(0)
        for job in jobs:
            job()
        cu = jnp.concatenate(
            [jnp.concatenate([conv[(cb, seg, rb)] for seg in range(nseg) for rb in range(P // CONV_RB)], axis=0)
             for cb in range(D_CONV // LANES)], axis=1)
        mu = jnp.mean(cu, axis=-1, keepdims=True)
        cc = cu - mu
        cn = cc * lax.rsqrt(jnp.mean(cc * cc, axis=-1, keepdims=True) + EPS) * w["lng"][...] + w["lnb"][...]
        ca = (cn * _sigmoid(cn)).astype(BF16)
        ma_s[rows, :] = ma_s[rows, :] * _dot(ca, w["wco"][...])
        return carry

    if n_mt == 1:
        phase1(0, 0)
    else:
        lax.fori_loop(0, n_mt, phase1, 0)

    ones_col = (lax.broadcasted_iota(jnp.int32, (L, HEAD_DIM), 1) == 0).astype(F32).astype(BF16)
    pad_rows = jnp.zeros((LANES - 3 * N_UNITS, L), F32)

    def gate_prep(c, m_vec):
        sc = scan_s[c]
        a, run_max, bsum = sc[0:N_UNITS], sc[N_UNITS:2 * N_UNITS], sc[2 * N_UNITS:3 * N_UNITS]
        tot, a_max = sc[3 * N_UNITS:4 * N_UNITS, 0:1], sc[4 * N_UNITS:5 * N_UNITS, 0:1]
        big_m = jnp.maximum(m_vec, run_max)
        m_end = jnp.maximum(m_vec, a_max)
        cols = jnp.concatenate(
            [big_m, jnp.exp(m_vec - big_m), jnp.exp(-bsum - big_m), pad_rows], axis=0).T
        return a, cols, jnp.exp(a - m_end), jnp.exp(m_vec - m_end), tot + m_end

    def unit_group(dirs, c, prep, first_chunk, want_state):
        a, cols, wk, decay, _ = prep
        rows = slice(c * L, (c + 1) * L)
        heads = range(N_HEADS)
        units = [(d, hd) for d in dirs for hd in heads]
        hs = [slice(hd * HEAD_DIM, (hd + 1) * HEAD_DIM) for hd in heads]
        idx = {u: u[0] * N_HEADS + u[1] for u in units}
        col = lambda k, u: cols[:, k * N_UNITS + idx[u]:k * N_UNITS + idx[u] + 1]
        row = lambda arr, u: arr[idx[u]:idx[u] + 1, :]
        chained = has_state or not first_chunk
        qc = [q_s[rows, hs[hd]] for hd in heads]
        kTc = [kT_s[c, hs[hd], :] for hd in heads]
        vaug = [jnp.concatenate([v_s[rows, hs[hd]], ones_col], axis=1) for hd in heads]
        qk = [_dot(qc[hd], kTc[hd]) for hd in heads]
        s_mat = {u: (qk[u[1]] * jnp.where(lower if u[0] == 0 else upper, jnp.exp(row(a, u) - col(0, u)), 0.0)
                     ).astype(BF16) for u in units}
        nd = {u: _dot(s_mat[u], vaug[u[1]]) for u in units}
        if chained:
            nd = {u: nd[u] + col(1, u) * _dot(qc[u[1]], cst_s[idx[u]].astype(BF16)) for u in units}
        h = {u: nd[u][:, :HEAD_DIM] * (1.0 / jnp.maximum(jnp.abs(nd[u][:, HEAD_DIM:HEAD_DIM + 1]), col(2, u)))
             for u in units}
        for hd in heads:
            total = h[(dirs[0], hd)]
            for d in dirs[1:]:
                total = total + h[(d, hd)]
            if dirs[0] == 0:
                hm_s[rows, hs[hd]] = total
            else:
                hm_s[rows, hs[hd]] = hm_s[rows, hs[hd]] + total
        if want_state:
            kw = {u: (kTc[u[1]].astype(F32) * row(wk, u)).astype(BF16) for u in units}
            upd = {u: _dot(kw[u], vaug[u[1]]) for u in units}
            for u in units:
                cst_s[idx[u]] = (upd[u] + row(decay, u) * cst_s[idx[u]]) if chained else upd[u]

    dir_rows = lax.broadcasted_iota(jnp.int32, (N_UNITS, 1), 0) >= N_HEADS
    for seq in range(n_seq):
        if has_state:
            n_cols = jnp.concatenate([n0_ref[0], jnp.zeros((LANES - N_UNITS, HEAD_DIM), F32)], axis=0).T
            first_lane = lax.broadcasted_iota(jnp.int32, (HEAD_DIM, HEAD_DIM), 1) == 0
            for idx in range(N_UNITS):
                cst_s[idx, :, :HEAD_DIM] = c0_ref[0, idx]
                cst_s[idx, :, HEAD_DIM:] = jnp.where(first_lane, n_cols[:, idx:idx + 1], 0.0)
            unit_row = lax.broadcasted_iota(jnp.int32, (N_UNITS, 1), 0)
            m_vec = jnp.zeros((N_UNITS, 1), F32)
            for idx in range(N_UNITS):
                m_vec = jnp.where(unit_row == idx, m0_ref[pl.program_id(0), idx], m_vec)
        else:
            m_vec = jnp.zeros((N_UNITS, 1), F32)
        if cps == 1:
            prep = gate_prep(seq, m_vec)
            unit_group([0, 1], seq, prep, True, emit_state)
            m_vec = prep[4]
        else:
            for d in range(2):
                order = list(range(cps)) if d == 0 else list(range(cps - 1, -1, -1))
                for pos, c in enumerate(order):
                    prep = gate_prep(seq * cps + c, m_vec)
                    unit_group([d], seq * cps + c, prep, pos == 0, emit_state or pos < cps - 1)
                    m_vec = jnp.where(dir_rows == (d == 1), prep[4], m_vec)
        if emit_state:
            for idx in range(N_UNITS):
                caug = cst_s[idx]
                cout_ref[0, seq * N_UNITS + idx] = caug[:, :HEAD_DIM]
                nout_ref[0, seq * N_UNITS + idx:seq * N_UNITS + idx + 1, :] = caug[:, HEAD_DIM:].T[0:1, :]
            mout_ref[0, seq * N_UNITS:(seq + 1) * N_UNITS, :] = jnp.broadcast_to(m_vec, (N_UNITS, LANES))

    e_iota = lax.broadcasted_iota(jnp.int32, (LANES, MIX_TM), 0)
    g_of_e = lax.shift_right_logical(e_iota, 2)
    j_of_e = lax.bitwise_and(e_iota, EXPERTS_PER_GROUP - 1)
    r8 = lax.broadcasted_iota(jnp.int32, (8, MIX_TM), 0)
    before_b = (lax.broadcasted_iota(jnp.int32, (MOE_BLK, MOE_BLK), 0)
                < lax.broadcasted_iota(jnp.int32, (MOE_BLK, MOE_BLK), 1)).astype(F32).astype(BF16)

    def phase3(i, carry):
        r0 = pl.multiple_of(i * MIX_TM, MIX_TM)
        rows = pl.ds(r0, MIX_TM)
        hm = hm_s[rows, :]
        heads = []
        for hd in range(N_HEADS):
            hh = hm[:, hd * HEAD_DIM:(hd + 1) * HEAD_DIM]
            heads.append(hh * lax.rsqrt(jnp.mean(hh * hh, axis=-1, keepdims=True) + EPS))
        hn = jnp.concatenate(heads, axis=1) * w["hng"][...]
        hb2 = (so_s[rows, :] * hn).astype(BF16)
        br_b = _dot(hb2, w["wmo"][...])
        mixed = (ma_s[rows, :] + sgb_s[rows, :] * br_b).astype(BF16)
        x1 = x_ref[0, rows, :] + mod_row(2) * _dot(mixed, w["wo"][...])
        x1_ref[0, rows, :] = x1
        xn = x1 * lax.rsqrt(jnp.mean(x1 * x1, axis=-1, keepdims=True) + EPS) * w["g2"][...]
        h2 = xn * (1.0 + mod_row(4)) + mod_row(3)
        h2_ref[0, rows, :] = h2.astype(BF16)

        h2_hi = h2.astype(BF16)
        h2_lo = (h2 - h2_hi.astype(F32)).astype(BF16)
        lg = _dot(h2_hi, w["wrt2"][...])
        lg = lg[:, :LANES] + lg[:, LANES:] + _dot(h2_lo, w["wrt2"][:, :LANES])
        lt = lg.T + w["brtT"][...]
        gl = [lt[N_EXPERTS + g:N_EXPERTS + g + 1, :] for g in range(N_GROUPS)]
        best, gsel = gl[0], jnp.zeros((1, MIX_TM), jnp.int32)
        for g in range(1, N_GROUPS):
            better = gl[g] > best
            gsel = jnp.where(better, g, gsel)
            best = jnp.where(better, gl[g], best)
        gp_sel = 1.0 / sum(jnp.exp(v - best) for v in gl)
        el = []
        for j in range(EXPERTS_PER_GROUP):
            v = lt[j:j + 1, :]
            for g in range(1, N_GROUPS):
                r = g * EXPERTS_PER_GROUP + j
                v = jnp.where(gsel == g, lt[r:r + 1, :], v)
            el.append(v)
        l1, e1 = el[0], jnp.zeros((1, MIX_TM), jnp.int32)
        for j in range(1, EXPERTS_PER_GROUP):
            better = el[j] > l1
            e1 = jnp.where(better, j, e1)
            l1 = jnp.where(better, el[j], l1)
        l2 = jnp.full((1, MIX_TM), -jnp.inf, F32)
        e2 = jnp.zeros((1, MIX_TM), jnp.int32)
        for j in range(EXPERTS_PER_GROUP):
            better = jnp.logical_and(e1 != j, el[j] > l2)
            e2 = jnp.where(better, j, e2)
            l2 = jnp.where(better, el[j], l2)
        r2 = jnp.exp(l2 - l1)
        wt1 = gp_sel / (1.0 + r2)
        wt2 = gp_sel * r2 / (1.0 + r2)
        in_group = g_of_e == gsel
        comb_t = (jnp.where(jnp.logical_and(in_group, j_of_e == e1), wt1, 0.0)
                  + jnp.where(jnp.logical_and(in_group, j_of_e == e2), wt2, 0.0))

        onehot = (r8 == gsel).astype(F32)
        gsel_f = gsel.astype(F32)
        rank = jnp.sum(onehot * _dot(onehot.astype(BF16), before_b), axis=0, keepdims=True)
        r8rows = pl.ds(pl.multiple_of(i * 8, 8), 8)
        route_ref[0, r8rows, :] = jnp.where(r8 == 0, gsel_f, jnp.where(r8 == 1, rank, 0.0))
        cnt_ref[0, r8rows, :] = jnp.broadcast_to(jnp.sum(onehot, axis=1, keepdims=True), (8, LANES))
        comb_t = jnp.where(e_iota == ROUTE_GROUP_LANE, gsel_f,
                           jnp.where(e_iota == ROUTE_RANK_LANE, rank, comb_t))
        comb_ref[0, rows, :] = comb_t.T
        return carry

    if n_mt == 1:
        phase3(0, 0)
    else:
        lax.fori_loop(0, n_mt, phase3, 0)


class _RowWindow(NamedTuple):
    array: jax.Array
    start: int
    n: int


def _const_spec(a):
    if isinstance(a, _RowWindow):
        assert a.start % a.n == 0
        return a.array, pl.BlockSpec((a.n, a.array.shape[1]), lambda b: (a.start // a.n, 0),
                                     pipeline_mode=pl.Buffered(1))
    nd = a.ndim
    return a, pl.BlockSpec(a.shape, lambda b, _nd=nd: (0,) * _nd, pipeline_mode=pl.Buffered(1))


def _mixer(x, T, mod, mod_index, weights, P, state=None, emit_state=False):
    B, R, _ = x.shape
    n_chunks = R // SUB
    n_blk = R // MOE_BLK
    n_seq = R // T
    has_state = state is not None
    seq_mode = {} if R <= MIX_TM else {"pipeline_mode": pl.Buffered(1)}
    in_specs = [
        pl.BlockSpec((1, R, D_MODEL), lambda b: (b, 0, 0), **seq_mode),
        pl.BlockSpec(mod.shape, lambda b: (0, 0, 0)),
    ]
    args = [x, mod]
    if has_state:
        c0, n0, m0 = state
        in_specs += [
            pl.BlockSpec((1, N_UNITS, HEAD_DIM, HEAD_DIM), lambda b: (b, 0, 0, 0)),
            pl.BlockSpec((1, N_UNITS, HEAD_DIM), lambda b: (b, 0, 0)),
            pl.BlockSpec(memory_space=pltpu.SMEM),
        ]
        args += [c0, n0, m0]
    for name in _MIXER_WEIGHTS:
        operand, spec = _const_spec(weights[name])
        in_specs.append(spec)
        args.append(operand)
    out_shape = [
        jax.ShapeDtypeStruct((B, R, D_MODEL), F32),
        jax.ShapeDtypeStruct((B, R, D_MODEL), BF16),
        jax.ShapeDtypeStruct((B, R, LANES), F32),
        jax.ShapeDtypeStruct((B, n_blk * 8, MOE_BLK), F32),
        jax.ShapeDtypeStruct((B, n_blk * 8, LANES), F32),
    ]
    out_specs = [
        pl.BlockSpec((1, R, D_MODEL), lambda b: (b, 0, 0), **seq_mode),
        pl.BlockSpec((1, R, D_MODEL), lambda b: (b, 0, 0), **seq_mode),
        pl.BlockSpec((1, R, LANES), lambda b: (b, 0, 0)),
        pl.BlockSpec((1, n_blk * 8, MOE_BLK), lambda b: (b, 0, 0)),
        pl.BlockSpec((1, n_blk * 8, LANES), lambda b: (b, 0, 0)),
    ]
    if emit_state:
        out_shape += [
            jax.ShapeDtypeStruct((B, n_seq * N_UNITS, HEAD_DIM, HEAD_DIM), F32),
            jax.ShapeDtypeStruct((B, n_seq * N_UNITS, HEAD_DIM), F32),
            jax.ShapeDtypeStruct((B, n_seq * N_UNITS, LANES), F32),
        ]
        out_specs += [
            pl.BlockSpec((1, n_seq * N_UNITS, HEAD_DIM, HEAD_DIM), lambda b: (b, 0, 0, 0)),
            pl.BlockSpec((1, n_seq * N_UNITS, HEAD_DIM), lambda b: (b, 0, 0)),
            pl.BlockSpec((1, n_seq * N_UNITS, LANES), lambda b: (b, 0, 0)),
        ]
    scratch = [
        pltpu.VMEM((R, D_MLSTM), BF16),
        pltpu.VMEM((n_chunks, D_MLSTM, SUB), BF16),
        pltpu.VMEM((R, D_MLSTM), BF16),
        pltpu.VMEM((R, D_MLSTM), F32),
        pltpu.VMEM((n_chunks, 5 * N_UNITS, SUB), F32),
        pltpu.VMEM((R, D_MODEL), F32),
        pltpu.VMEM((R, D_MODEL), F32),
        pltpu.VMEM((R, D_MLSTM), F32),
        pltpu.VMEM((N_UNITS, HEAD_DIM, 2 * HEAD_DIM), F32),
        pltpu.VMEM((MIX_TM // P, P + 2 * CONV_PAD, D_CONV), F32),
    ]
    return pl.pallas_call(
        functools.partial(_mixer_kernel, R, T, P, has_state, emit_state, mod_index),
        grid=(B,),
        in_specs=in_specs,
        out_specs=out_specs,
        out_shape=out_shape,
        scratch_shapes=scratch,
        compiler_params=pltpu.CompilerParams(
            dimension_semantics=("arbitrary",), vmem_limit_bytes=VMEM_LIMIT),
        name="mixer_T%d" % T,
    )(*args)


def _dest_in_block(group, rank, starts):
    dest = rank
    for g in range(N_GROUPS):
        dest = dest + jnp.where(group == float(g), starts[g], 0.0)
    return dest


def _copy_segments(src_refs, dst_refs, src_starts, dst_starts, n_pieces):
    def copy(g, first_piece, n_rows):
        s = pl.multiple_of(src_starts[g] + first_piece * ROW_ALIGN, ROW_ALIGN)
        d = pl.multiple_of(dst_starts[g] + first_piece * ROW_ALIGN, ROW_ALIGN)
        for src, dst in zip(src_refs, dst_refs):
            dst[pl.ds(d, n_rows), :] = src[pl.ds(s, n_rows), :]

    for g in range(N_GROUPS):
        n_runs = lax.shift_right_logical(n_pieces[g], COPY_RUN.bit_length() - 1)

        def run(k, carry, g=g):
            copy(g, k * COPY_RUN, COPY_RUN * ROW_ALIGN)
            return carry

        def single(k, carry, g=g):
            copy(g, k, ROW_ALIGN)
            return carry

        lax.fori_loop(0, n_runs, run, 0)
        lax.fori_loop(n_runs * COPY_RUN, n_pieces[g], single, 0)


def _plan_segments(n_blocks, n_tiles, count, start_ref, npiece_ref, off_ref, tgroup_ref, tvalid_ref):
    align_shift = ROW_ALIGN.bit_length() - 1
    tile_shift = MOE_TM.bit_length() - 1

    def block_starts(blk, carry):
        row = jnp.int32(0)
        for g in range(N_GROUPS):
            n = lax.shift_right_logical(count(blk, g) + (ROW_ALIGN - 1), align_shift)
            npiece_ref[blk * N_GROUPS + g] = n
            start_ref[blk * N_GROUPS + g] = row
            row = row + n * ROW_ALIGN
        return carry

    lax.fori_loop(0, n_blocks, block_starts, 0)

    base_row = jnp.int32(0)
    base_tile = jnp.int32(0)
    last_group = jnp.int32(0)
    for g in range(N_GROUPS):
        def seg_offsets(blk, row, g=g, base_row=base_row):
            off_ref[blk * N_GROUPS + g] = base_row + row
            return row + npiece_ref[blk * N_GROUPS + g] * ROW_ALIGN

        rows = lax.fori_loop(0, n_blocks, seg_offsets, jnp.int32(0))
        tiles = lax.shift_right_logical(rows + (MOE_TM - 1), tile_shift)

        def mark_tiles(t, carry, g=g, base_tile=base_tile):
            tgroup_ref[base_tile + t] = g
            tvalid_ref[base_tile + t] = 1
            return carry

        lax.fori_loop(0, tiles, mark_tiles, 0)
        last_group = jnp.where(tiles > 0, g, last_group)
        base_row = base_row + tiles * MOE_TM
        base_tile = base_tile + tiles

    def mark_unused(t, carry):
        tgroup_ref[t] = last_group
        tvalid_ref[t] = 0
        return carry

    lax.fori_loop(base_tile, n_tiles, mark_unused, 0)


def _dispatch_kernel(n_ctx_blocks, n_blocks, n_tiles,
                     h2c_ref, h2l_ref, cbc_ref, cbl_ref, rtc_ref, rtl_ref, cntc_ref, cntl_ref,
                     xs_ref, cs_ref, start_ref, npiece_ref, off_ref, tgroup_ref, tvalid_ref,
                     sx_s, sc_s):
    b = pl.program_id(0)
    is_ctx = b < n_ctx_blocks

    def count(blk, g):
        vc = cntc_ref[jnp.minimum(blk, n_ctx_blocks - 1), pl.ds(g, 1), pl.ds(0, 1)]
        vl = cntl_ref[jnp.maximum(blk - n_ctx_blocks, 0), pl.ds(g, 1), pl.ds(0, 1)]
        return jnp.where(blk < n_ctx_blocks, vc, vl)[0, 0].astype(jnp.int32)

    @pl.when(b == 0)
    def _():
        _plan_segments(n_blocks, n_tiles, count, start_ref, npiece_ref, off_ref, tgroup_ref, tvalid_ref)
        xs_ref[...] = jnp.zeros_like(xs_ref)
        cs_ref[...] = jnp.zeros_like(cs_ref)

    starts = [start_ref[b * N_GROUPS + g] for g in range(N_GROUPS)]

    def sort_block(h2_ref, cb_ref, rt_ref):
        h2 = h2_ref[0]
        cb = cb_ref[0]
        rt = rt_ref[0]
        dest = _dest_in_block(rt[0:1, :], rt[1:2, :], [s.astype(F32) for s in starts])
        row = lax.broadcasted_iota(jnp.int32, (SORT_ROWS, MOE_BLK), 0).astype(F32)
        perm = (row == dest).astype(F32).astype(BF16)
        cb_hi = cb.astype(BF16)
        cb_lo = (cb - cb_hi.astype(F32)).astype(BF16)
        srt = _dot(perm, jnp.concatenate([h2, cb_hi, cb_lo], axis=1)).astype(BF16)
        sx_s[...] = srt[:, :D_MODEL]
        sc_s[...] = srt[:, D_MODEL:]

    pl.when(is_ctx)(functools.partial(sort_block, h2c_ref, cbc_ref, rtc_ref))
    pl.when(jnp.logical_not(is_ctx))(functools.partial(sort_block, h2l_ref, cbl_ref, rtl_ref))
    _copy_segments((sx_s, sc_s), (xs_ref, cs_ref), starts,
                   [off_ref[b * N_GROUPS + g] for g in range(N_GROUPS)],
                   [npiece_ref[b * N_GROUPS + g] for g in range(N_GROUPS)])


def _experts_kernel(tgroup_ref, tvalid_ref, xs_ref, cs_ref, wg_ref, wu_ref, wd_ref, ys_ref):
    i = pl.program_id(0)

    @pl.when(tvalid_ref[i] == 1)
    def _():
        x = xs_ref[...]
        comb = cs_ref[:, :LANES].astype(F32) + cs_ref[:, LANES:].astype(F32)
        lane = lax.broadcasted_iota(jnp.int32, comb.shape, 1)
        first = tgroup_ref[i] * EXPERTS_PER_GROUP
        acc = None
        for j in range(EXPERTS_PER_GROUP):
            gj = _dot(x, wg_ref[j].astype(BF16))
            uj = _dot(x, wu_ref[j].astype(BF16))
            cw = jnp.sum(jnp.where(lane == first + j, comb, 0.0), axis=1, keepdims=True)
            out = _dot((gj * _sigmoid(gj) * uj * cw).astype(BF16), wd_ref[j].astype(BF16))
            acc = out if acc is None else acc + out
        ys_ref[...] = acc.astype(BF16)

    @pl.when(tvalid_ref[i] == 0)
    def _():
        ys_ref[...] = jnp.zeros_like(ys_ref)


def _combine_kernel(n_ctx_blocks, blocks_per_lat_seq, start_ref, npiece_ref, off_ref,
                    x1c_ref, x1l_ref, cbc_ref, cbl_ref, ys_ref, mod_ref, gf_ref, yc_ref, yl_ref, loc_s):
    b = pl.program_id(0)
    is_ctx = b < n_ctx_blocks
    starts = [start_ref[b * N_GROUPS + g] for g in range(N_GROUPS)]
    @pl.when(b == 0)
    def _():
        loc_s[...] = jnp.zeros_like(loc_s)

    _copy_segments((ys_ref,), (loc_s,), [off_ref[b * N_GROUPS + g] for g in range(N_GROUPS)], starts,
                   [npiece_ref[b * N_GROUPS + g] for g in range(N_GROUPS)])
    def finish_block(x1_ref, cb_ref, y_ref, mrow):
        cb = cb_ref[0]
        dest = _dest_in_block(cb[:, ROUTE_GROUP_LANE:ROUTE_GROUP_LANE + 1],
                              cb[:, ROUTE_RANK_LANE:ROUTE_RANK_LANE + 1],
                              [s.astype(F32) for s in starts])
        col = lax.broadcasted_iota(jnp.int32, (MOE_BLK, SORT_ROWS), 1).astype(F32)
        unperm = (col == dest).astype(F32).astype(BF16)
        x2 = x1_ref[0] + mod_ref[N_ADA - 1, pl.ds(mrow, 1), :] * _dot(unperm, loc_s[...])
        y_ref[0] = x2 * lax.rsqrt(jnp.mean(x2 * x2, axis=-1, keepdims=True) + EPS) * gf_ref[...]

    lat_row = 1 + jnp.maximum(b - n_ctx_blocks, 0) // blocks_per_lat_seq
    pl.when(is_ctx)(functools.partial(finish_block, x1c_ref, cbc_ref, yc_ref, 0))
    pl.when(jnp.logical_not(is_ctx))(functools.partial(finish_block, x1l_ref, cbl_ref, yl_ref, lat_row))


def _moe(x1c, x1l, h2c, h2l, cbc, cbl, rtc, rtl, cntc, cntl, mod, blocks_per_lat_seq, wg, wu, wd, gf):
    nc, nl = x1c.shape[0], x1l.shape[0]
    nb = nc + nl
    n_rows_max = nb * MOE_BLK + nb * N_GROUPS * (ROW_ALIGN - 1) + N_GROUPS * (MOE_TM - ROW_ALIGN)
    n_tiles = -(-n_rows_max // MOE_TM)
    ns = n_tiles * MOE_TM

    cmap = lambda b, *_: (jnp.minimum(b, nc - 1), 0, 0)
    lmap = lambda b, *_: (jnp.maximum(b - nc, 0), 0, 0)
    whole = lambda *_: (0, 0)
    once = {"pipeline_mode": pl.Buffered(1)}
    arb = pltpu.CompilerParams(dimension_semantics=("arbitrary",), vmem_limit_bytes=VMEM_LIMIT)
    smem = pl.BlockSpec(memory_space=pltpu.SMEM)
    seg_i32 = jax.ShapeDtypeStruct((nb * N_GROUPS,), jnp.int32)
    tile_i32 = jax.ShapeDtypeStruct((n_tiles,), jnp.int32)

    xs, cs, start, npiece, off, tgroup, tvalid = pl.pallas_call(
        functools.partial(_dispatch_kernel, nc, nb, n_tiles),
        grid_spec=pltpu.PrefetchScalarGridSpec(
            num_scalar_prefetch=0, grid=(nb,),
            in_specs=[
                pl.BlockSpec((1, MOE_BLK, D_MODEL), cmap), pl.BlockSpec((1, MOE_BLK, D_MODEL), lmap),
                pl.BlockSpec((1, MOE_BLK, LANES), cmap), pl.BlockSpec((1, MOE_BLK, LANES), lmap),
                pl.BlockSpec((1, 8, MOE_BLK), cmap), pl.BlockSpec((1, 8, MOE_BLK), lmap),
                pl.BlockSpec(cntc.shape, lambda b: (0, 0, 0)), pl.BlockSpec(cntl.shape, lambda b: (0, 0, 0)),
            ],
            out_specs=[pl.BlockSpec((ns, D_MODEL), whole, **once), pl.BlockSpec((ns, 2 * LANES), whole, **once),
                       smem, smem, smem, smem, smem],
            scratch_shapes=[pltpu.VMEM((SORT_ROWS, D_MODEL), BF16), pltpu.VMEM((SORT_ROWS, 2 * LANES), BF16)],
        ),
        out_shape=[jax.ShapeDtypeStruct((ns, D_MODEL), BF16), jax.ShapeDtypeStruct((ns, 2 * LANES), BF16),
                   seg_i32, seg_i32, seg_i32, tile_i32, tile_i32],
        compiler_params=arb,
        name="moe_dispatch",
    )(h2c, h2l, cbc, cbl, rtc, rtl, cntc, cntl)

    wmap = lambda i, tg, tv: (tg[i], 0, 0)
    ys = pl.pallas_call(
        _experts_kernel,
        grid_spec=pltpu.PrefetchScalarGridSpec(
            num_scalar_prefetch=2, grid=(n_tiles,),
            in_specs=[
                pl.BlockSpec((MOE_TM, D_MODEL), lambda i, *_: (i, 0)),
                pl.BlockSpec((MOE_TM, 2 * LANES), lambda i, *_: (i, 0)),
                pl.BlockSpec((EXPERTS_PER_GROUP, D_MODEL, D_EXPERT), wmap),
                pl.BlockSpec((EXPERTS_PER_GROUP, D_MODEL, D_EXPERT), wmap),
                pl.BlockSpec((EXPERTS_PER_GROUP, D_EXPERT, D_MODEL), wmap),
            ],
            out_specs=pl.BlockSpec((MOE_TM, D_MODEL), lambda i, *_: (i, 0)),
        ),
        out_shape=jax.ShapeDtypeStruct((ns, D_MODEL), BF16),
        compiler_params=arb,
        name="moe_experts",
    )(tgroup, tvalid, xs, cs, wg, wu, wd)

    yc, yl = pl.pallas_call(
        functools.partial(_combine_kernel, nc, blocks_per_lat_seq),
        grid_spec=pltpu.PrefetchScalarGridSpec(
            num_scalar_prefetch=3, grid=(nb,),
            in_specs=[
                pl.BlockSpec((1, MOE_BLK, D_MODEL), cmap), pl.BlockSpec((1, MOE_BLK, D_MODEL), lmap),
                pl.BlockSpec((1, MOE_BLK, LANES), cmap), pl.BlockSpec((1, MOE_BLK, LANES), lmap),
                pl.BlockSpec((ns, D_MODEL), whole, **once),
                pl.BlockSpec(mod.shape, lambda *_: (0, 0, 0)),
                pl.BlockSpec((1, D_MODEL), whole),
            ],
            out_specs=[pl.BlockSpec((1, MOE_BLK, D_MODEL), cmap), pl.BlockSpec((1, MOE_BLK, D_MODEL), lmap)],
            scratch_shapes=[pltpu.VMEM((SORT_ROWS, D_MODEL), BF16)],
        ),
        out_shape=[jax.ShapeDtypeStruct((nc, MOE_BLK, D_MODEL), F32),
                   jax.ShapeDtypeStruct((nl, MOE_BLK, D_MODEL), F32)],
        compiler_params=arb,
        name="moe_combine",
    )(start, npiece, off, x1c, x1l, cbc, cbl, ys, mod, gf)
    return yc, yl


def _prep_weights(norm1_g, w_in, b_in, b_gates, w_dw, b_dw, conv_ln_g, conv_ln_b, w_conv_out,
                  mlstm_hn_g, w_mlstm_out, w_o, norm2_g, w_rg, b_rg, w_re, b_re, ada_args):
    s_a = 2 * D_CONV
    s_q = s_a + D_MLSTM
    s_k = s_q + D_MLSTM
    s_v = s_k + D_MLSTM
    s_o = s_v + D_MLSTM
    s_g = s_o + 4 * N_HEADS
    row = lambda v: v.reshape(1, -1).astype(F32)
    w_t = w_in.T
    keep = [(0, s_q), (s_k, s_o), (s_g, w_in.shape[1])]
    halved = [(D_CONV, s_a), (s_v, s_o), (s_g, w_in.shape[1])]
    is_halved = lambda r: any(a <= r < b for a, b in halved)
    blocks = [r for a, b in keep for r in range(a, b, WPREP_ROWS)]
    mod, wrow = _ada_and_layout(*ada_args, w_t, blocks, [is_halved(r) for r in blocks])
    bias_scale = jnp.array([0.5 if is_halved(r) else 1.0 for a, b in keep for r in range(a, b)]
                           + [1.0] * D_MLSTM, F32)
    bg = (b_in[s_o:s_g] + b_gates.reshape(-1)).reshape(2, 2, N_HEADS).transpose(1, 0, 2).reshape(-1, 1)
    row_window = lambda start, n: _RowWindow(w_t, start, n)
    n_rt = N_EXPERTS + N_GROUPS
    wrt = jnp.pad(jnp.concatenate([w_re, w_rg], axis=1), ((0, 0), (0, LANES - n_rt)))
    wrt_hi = wrt.astype(BF16)
    wrt2 = jnp.concatenate([wrt_hi, (wrt - wrt_hi.astype(F32)).astype(BF16)], axis=1)
    brtT = jnp.pad(jnp.concatenate([b_re, b_rg]), (0, LANES - n_rt)).reshape(LANES, 1)
    return {
        "g1": row(norm1_g),
        "wrow": wrow, "brow": row(jnp.concatenate([b_in[a:b] for a, b in keep] + [b_in[s_q:s_k]]) * bias_scale),
        "wkT": row_window(s_q, D_MLSTM), "wgifT": row_window(s_o, 4 * N_HEADS), "bgifT": bg,
        "wdw": w_dw.astype(F32), "bdw": row(b_dw), "lng": row(conv_ln_g), "lnb": row(conv_ln_b),
        "wco": w_conv_out.astype(BF16), "hng": row(mlstm_hn_g), "wmo": w_mlstm_out.astype(BF16),
        "wo": w_o.astype(BF16), "g2": row(norm2_g), "wrt2": wrt2, "brtT": brtT,
    }, mod


def kernel(x_prompt, x_sample, state_C, state_n, state_m, c, c_ctx, norm1_g, w_ada, b_ada, w_in, b_in, b_gates, w_dw, b_dw, conv_ln_g, conv_ln_b, w_conv_out, mlstm_hn_g, w_mlstm_out, w_o, norm2_g, w_rg, b_rg, w_re, b_re, w_e_gate, w_e_up, w_e_down, norm_final_g):
    B, S, _ = x_prompt.shape
    Bd, Sd, _ = x_sample.shape
    assert w_ada.shape[0] == 1, "single trunk layer"
    assert MIX_TM % S == 0 and S % SUB == 0 and Sd % MIX_TM == 0

    wts, mod = _prep_weights(norm1_g[0], w_in[0], b_in[0], b_gates[0], w_dw[0], b_dw[0], conv_ln_g[0],
                             conv_ln_b[0], w_conv_out[0], mlstm_hn_g[0], w_mlstm_out[0], w_o[0],
                             norm2_g[0], w_rg[0], b_rg[0], w_re[0], b_re[0],
                             (c_ctx.reshape(1, -1), c, w_ada[0], b_ada[0].reshape(1, -1)))

    x1p, h2p, cbp, rtp, cntp, c_new, n_new, m_new = _mixer(
        x_prompt.reshape(B * S // MIX_TM, MIX_TM, D_MODEL), S, mod, lambda b: 0, wts, P=S, emit_state=True)

    state = (state_C[:, 0].reshape(Bd, N_UNITS, HEAD_DIM, HEAD_DIM), state_n[:, 0].reshape(Bd, N_UNITS, HEAD_DIM),
             state_m[:, 0].reshape(Bd, N_UNITS))
    x1s, h2s, cbs, rts, cnts = _mixer(x_sample, Sd, mod, lambda b: 1 + b, wts, P=GRID_W, state=state)

    nc, nl = B * S // MOE_BLK, Bd * Sd // MOE_BLK
    blk = lambda a, n: a.reshape(n, MOE_BLK, a.shape[-1])
    yp, ys = _moe(blk(x1p, nc), blk(x1s, nl), blk(h2p, nc), blk(h2s, nl), blk(cbp, nc), blk(cbs, nl),
                  rtp.reshape(nc, 8, MOE_BLK), rts.reshape(nl, 8, MOE_BLK),
                  cntp.reshape(nc, 8, LANES), cnts.reshape(nl, 8, LANES),
                  mod, Sd // MOE_BLK, w_e_gate[0], w_e_up[0], w_e_down[0], norm_final_g.reshape(1, -1))

    return (yp.reshape(B, S, D_MODEL), ys.reshape(Bd, Sd, D_MODEL),
            c_new.reshape(B, 1, 2, N_HEADS, HEAD_DIM, HEAD_DIM),
            n_new.reshape(B, 1, 2, N_HEADS, HEAD_DIM),
            m_new[:, :, 0].reshape(B, 1, 2, N_HEADS))
```
